```python
import math
import jax, jax.numpy as jnp
from jax import lax
import numpy as np

D_MODEL = 1024
BATCH = 8
SEQ = 8192
DEPTH = 2

SSD_EXPAND = 2
D_INNER = SSD_EXPAND * D_MODEL
SSD_HEAD_DIM = 64
SSD_HEADS = D_INNER // SSD_HEAD_DIM
SSD_GROUPS = 8
SSD_STATE = 128
SSD_CONV = 5
SSD_CHUNK = 128
CONV_DIM = D_INNER + 2 * SSD_GROUPS * SSD_STATE
ATTN_HEAD_DIM = 64
ATTN_Q_HEADS = D_MODEL // ATTN_HEAD_DIM
ATTN_KV_HEADS = 4
ATTN_WIDTH = ATTN_Q_HEADS * ATTN_HEAD_DIM
KV_WIDTH = ATTN_KV_HEADS * ATTN_HEAD_DIM
WINDOW = 128
BLOCK = 128
N_BUCKETS = 32
MAX_DISTANCE = 128
D_FF = 4 * D_MODEL
EPS = 1e-6
IN_SPLITS = (D_INNER, CONV_DIM, 2 * SSD_HEADS, ATTN_WIDTH, KV_WIDTH, KV_WIDTH, 2 * D_MODEL)
N_IN = sum(IN_SPLITS)

kernel_name = 'hybrid_ssd_swa_encoder'


def rmsnorm(x, g):
    xf = x.astype(jnp.float32)
    y = xf * lax.rsqrt(jnp.mean(xf * xf, axis=-1, keepdims=True) + EPS)
    return (y * g.astype(jnp.float32)).astype(x.dtype)


def split_cols(t, sizes):
    idx, acc = [], 0
    for s in sizes[:-1]:
        acc += s
        idx.append(acc)
    return jnp.split(t, idx, axis=-1)


def depthwise_conv(u, w, b):
    pad = (SSD_CONV - 1) // 2
    y = lax.conv_general_dilated(u, w.astype(u.dtype), window_strides=(1,), padding=[(pad, pad)],
                                 dimension_numbers=('NWC', 'WIO', 'NWC'),
                                 feature_group_count=u.shape[-1])
    return y + b.astype(u.dtype)


def segsum_exp(a):
    cs = jnp.cumsum(a, axis=-1)
    diff = cs[..., :, None] - cs[..., None, :]
    L = a.shape[-1]
    mask = jnp.tril(jnp.ones((L, L), dtype=bool))
    return jnp.exp(jnp.where(mask, diff, -jnp.inf))


def ssd_scan(x, dt, a, b, c):
    Bsz, S, G, R, P = x.shape
    N = b.shape[-1]
    nc, L = S // SSD_CHUNK, SSD_CHUNK
    x = x.reshape(Bsz, nc, L, G, R, P)
    dt = dt.reshape(Bsz, nc, L, G, R)
    b = b.reshape(Bsz, nc, L, G, N)
    c = c.reshape(Bsz, nc, L, G, N)
    a_dt = jnp.moveaxis(dt * a, 2, -1)
    a_cs = jnp.cumsum(a_dt, axis=-1)
    xdt = x * dt[..., None]
    decay = segsum_exp(a_dt)
    cb = jnp.einsum('bclgn,bcsgn->bcgls', c, b)
    y_diag = jnp.einsum('bcgls,bcgrls,bcsgrp->bclgrp', cb, decay, xdt)
    decay_states = jnp.exp(a_cs[..., -1:] - a_cs)
    states = jnp.einsum('bclgn,bcgrl,bclgrp->bcgrpn', b, decay_states, xdt)
    chunk_decay = jnp.exp(a_cs[..., -1])

    def step(h, inp):
        st, dec = inp
        return h * dec[..., None, None] + st, h

    h0 = jnp.zeros((Bsz, G, R, P, N), x.dtype)
    _, prev = lax.scan(step, h0, (jnp.moveaxis(states, 1, 0), jnp.moveaxis(chunk_decay, 1, 0)))
    prev = jnp.moveaxis(prev, 0, 1)
    y_off = jnp.einsum('bclgn,bcgrpn,bcgrl->bclgrp', c, prev, jnp.exp(a_cs))
    return (y_diag + y_off).reshape(Bsz, S, G, R, P)


def gated_rmsnorm(y, z, w):
    u = y * jax.nn.silu(z.astype(jnp.float32))
    ug = u.reshape(u.shape[:-1] + (SSD_GROUPS, D_INNER // SSD_GROUPS))
    ug = ug * lax.rsqrt(jnp.mean(ug * ug, axis=-1, keepdims=True) + EPS)
    return ug.reshape(u.shape) * w.astype(jnp.float32)


def ssd_branch(z, xbc, dt_raw, conv_w, conv_b, dt_bias, a_log, d_skip, norm_w, w_out):
    Bsz, S = z.shape[:2]
    G, R, P, N = SSD_GROUPS, SSD_HEADS // SSD_GROUPS, SSD_HEAD_DIM, SSD_STATE
    xbc = jax.nn.silu(depthwise_conv(xbc, conv_w, conv_b))
    xs, bs, cs = split_cols(xbc, (D_INNER, G * N, G * N))
    xs = xs.astype(jnp.float32).reshape(Bsz, S, G, R, P)
    bs = bs.astype(jnp.float32).reshape(Bsz, S, G, N)
    cs = cs.astype(jnp.float32).reshape(Bsz, S, G, N)
    dt = jax.nn.softplus(dt_raw.astype(jnp.float32).reshape(Bsz, S, 2, G, R)
                         + dt_bias.astype(jnp.float32).reshape(2, G, R))
    a = -jnp.exp(a_log.astype(jnp.float32)).reshape(2, G, R)
    y_fwd = ssd_scan(xs, dt[:, :, 0], a[0], bs, cs)
    fl = lambda t: jnp.flip(t, axis=1)
    y_bwd = fl(ssd_scan(fl(xs), fl(dt[:, :, 1]), a[1], fl(bs), fl(cs)))
    y = y_fwd + y_bwd + xs * d_skip.astype(jnp.float32).reshape(G, R, 1)
    y = gated_rmsnorm(y.reshape(Bsz, S, D_INNER), z, norm_w)
    return y.astype(z.dtype) @ w_out


def t5_bucket(rel):
    nb = N_BUCKETS // 2
    max_exact = nb // 2
    ret = jnp.where(rel > 0, nb, 0)
    n = jnp.abs(rel)
    nf = jnp.maximum(n, 1).astype(jnp.float32)
    large = max_exact + (jnp.log(nf / max_exact) / math.log(MAX_DISTANCE / max_exact)
                         * (nb - max_exact)).astype(jnp.int32)
    large = jnp.minimum(large, nb - 1)
    return ret + jnp.where(n < max_exact, n, large)


def window_attention(q, k, v, sink, rel_table):
    Bsz, S = q.shape[:2]
    nb = S // BLOCK
    rep = ATTN_Q_HEADS // ATTN_KV_HEADS
    qb = q.reshape(Bsz, nb, BLOCK, ATTN_KV_HEADS, rep, ATTN_HEAD_DIM)

    def band(t):
        t = t.reshape(Bsz, S, ATTN_KV_HEADS, ATTN_HEAD_DIM)
        t = jnp.pad(t, ((0, 0), (BLOCK, BLOCK), (0, 0), (0, 0)))
        t = t.reshape(Bsz, nb + 2, BLOCK, ATTN_KV_HEADS, ATTN_HEAD_DIM)
        return jnp.concatenate([t[:, :-2], t[:, 1:-1], t[:, 2:]], axis=2)

    kb, vb = band(k), band(v)
    logits = jnp.einsum('bnqgrd,bnkgd->bngrqk', qb, kb).astype(jnp.float32) * (ATTN_HEAD_DIM ** -0.5)
    i = jnp.arange(BLOCK)[:, None]
    j = jnp.arange(3 * BLOCK)[None, :]
    rel = j - BLOCK - i
    bias = rel_table[t5_bucket(rel)].astype(jnp.float32)
    bias = jnp.transpose(bias, (2, 0, 1)).reshape(ATTN_KV_HEADS, rep, BLOCK, 3 * BLOCK)
    kpos = jnp.arange(nb)[:, None] * BLOCK + j - BLOCK
    valid = (jnp.abs(rel) <= WINDOW)[None] & ((kpos >= 0) & (kpos < S))[:, None, :]
    logits = jnp.where(valid[None, :, None, None], logits + bias, -jnp.inf)
    sink_l = sink.astype(jnp.float32).reshape(1, 1, ATTN_KV_HEADS, rep, 1, 1)
    m = jnp.maximum(jnp.max(logits, axis=-1, keepdims=True), sink_l)
    p = jnp.exp(logits - m)
    p = p / (jnp.sum(p, axis=-1, keepdims=True) + jnp.exp(sink_l - m))
    out = jnp.einsum('bngrqk,bnkgd->bnqgrd', p.astype(v.dtype), vb)
    return out.reshape(Bsz, S, ATTN_WIDTH)


def _fwd_setup_inputs(seed: int = 0) -> dict:
    key = jax.random.key(seed)
    ks = jax.random.split(key, 24)
    f32 = jnp.float32
    nrm = lambda k, shape, s: jax.random.normal(k, shape, f32) * s
    gain = lambda k, shape: 1.0 + 0.05 * jax.random.normal(k, shape, f32)
    dt0 = jnp.exp(jax.random.uniform(ks[10], (DEPTH, 2, SSD_HEADS), f32, math.log(1e-3), math.log(1e-1)))
    return {
        'x': jax.random.normal(ks[0], (BATCH, SEQ, D_MODEL), f32),
        'pre_mix_norm': gain(ks[1], (DEPTH, D_MODEL)),
        'w_in': nrm(ks[2], (DEPTH, D_MODEL, N_IN), D_MODEL ** -0.5),
        'b_gate': nrm(ks[3], (DEPTH, 2 * D_MODEL), 0.1),
        'conv_w': nrm(ks[4], (DEPTH, SSD_CONV, 1, CONV_DIM), SSD_CONV ** -0.5),
        'conv_b': nrm(ks[5], (DEPTH, CONV_DIM), 0.02),
        'dt_bias': dt0 + jnp.log(-jnp.expm1(-dt0)),
        'a_log': jnp.log(jax.random.uniform(ks[6], (DEPTH, 2, SSD_HEADS), f32, 1.0, 16.0)),
        'd_skip': gain(ks[7], (DEPTH, SSD_HEADS)),
        'ssd_norm': gain(ks[8], (DEPTH, D_INNER)),
        'w_ssd_out': nrm(ks[9], (DEPTH, D_INNER, D_MODEL), D_INNER ** -0.5),
        'attn_sink': nrm(ks[11], (DEPTH, ATTN_Q_HEADS), 0.5),
        'rel_bias_table': nrm(ks[12], (N_BUCKETS, ATTN_Q_HEADS), 0.5),
        'w_attn_out': nrm(ks[13], (DEPTH, ATTN_WIDTH, D_MODEL), ATTN_WIDTH ** -0.5),
        'w_o': nrm(ks[14], (DEPTH, D_MODEL, D_MODEL), D_MODEL ** -0.5),
        'post_mix_norm': gain(ks[15], (DEPTH, D_MODEL)),
        'pre_mlp_norm': gain(ks[16], (DEPTH, D_MODEL)),
        'w_mlp_in': nrm(ks[17], (DEPTH, D_MODEL, D_FF), D_MODEL ** -0.5),
        'w_mlp_out': nrm(ks[18], (DEPTH, D_FF, D_MODEL), D_FF ** -0.5),
        'post_mlp_norm': gain(ks[19], (DEPTH, D_MODEL)),
    }


def _fwd_reference(x, pre_mix_norm, w_in, b_gate, conv_w, conv_b, dt_bias, a_log, d_skip, ssd_norm,
              w_ssd_out, attn_sink, rel_bias_table, w_attn_out, w_o, post_mix_norm,
              pre_mlp_norm, w_mlp_in, w_mlp_out, post_mlp_norm):
    for l in range(DEPTH):
        h = rmsnorm(x, pre_mix_norm[l])
        proj = h @ w_in[l]
        z, xbc, dt_raw, q, k, v, gates = split_cols(proj, IN_SPLITS)
        y_ssd = ssd_branch(z, xbc, dt_raw, conv_w[l], conv_b[l], dt_bias[l], a_log[l],
                           d_skip[l], ssd_norm[l], w_ssd_out[l])
        y_attn = window_attention(q, k, v, attn_sink[l], rel_bias_table) @ w_attn_out[l]
        g = jax.nn.sigmoid((gates + b_gate[l]).astype(jnp.float32)).astype(x.dtype)
        g_ssd, g_attn = g[..., :D_MODEL], g[..., D_MODEL:]
        mixed = (g_ssd * y_ssd + g_attn * y_attn) @ w_o[l]
        x = x + rmsnorm(mixed, post_mix_norm[l])
        h = rmsnorm(x, pre_mlp_norm[l])
        f = jnp.square(jax.nn.relu(h @ w_mlp_in[l])) @ w_mlp_out[l]
        x = x + rmsnorm(f, post_mlp_norm[l])
    return x


import jax as _jax
import jax.numpy as _jnp

TWIN_FORMAT = 'train_step'
FWD_PARAMS = ['x', 'pre_mix_norm', 'w_in', 'b_gate', 'conv_w', 'conv_b', 'dt_bias', 'a_log', 'd_skip', 'ssd_norm', 'w_ssd_out', 'attn_sink', 'rel_bias_table', 'w_attn_out', 'w_o', 'post_mix_norm', 'pre_mlp_norm', 'w_mlp_in', 'w_mlp_out', 'post_mlp_norm']
TWIN_WEIGHTS = ['pre_mix_norm', 'w_in', 'b_gate', 'conv_w', 'conv_b', 'dt_bias', 'a_log', 'd_skip', 'ssd_norm', 'w_ssd_out', 'attn_sink', 'rel_bias_table', 'w_attn_out', 'w_o', 'post_mix_norm', 'pre_mlp_norm', 'w_mlp_in', 'w_mlp_out', 'post_mlp_norm']
TWIN_DIFF_INPUT = 'x'
TWIN_INPUTS = ['x', 'pre_mix_norm', 'w_in', 'b_gate', 'conv_w', 'conv_b', 'dt_bias', 'a_log', 'd_skip', 'ssd_norm', 'w_ssd_out', 'attn_sink', 'rel_bias_table', 'w_attn_out', 'w_o', 'post_mix_norm', 'pre_mlp_norm', 'w_mlp_in', 'w_mlp_out', 'post_mlp_norm', 'loss_target', 'm_pre_mix_norm', 'm_w_in', 'm_b_gate', 'm_conv_w', 'm_conv_b', 'm_dt_bias', 'm_a_log', 'm_d_skip', 'm_ssd_norm', 'm_w_ssd_out', 'm_attn_sink', 'm_rel_bias_table', 'm_w_attn_out', 'm_w_o', 'm_post_mix_norm', 'm_pre_mlp_norm', 'm_w_mlp_in', 'm_w_mlp_out', 'm_post_mlp_norm', 'v_pre_mix_norm', 'v_w_in', 'v_b_gate', 'v_conv_w', 'v_conv_b', 'v_dt_bias', 'v_a_log', 'v_d_skip', 'v_ssd_norm', 'v_w_ssd_out', 'v_attn_sink', 'v_rel_bias_table', 'v_w_attn_out', 'v_w_o', 'v_post_mix_norm', 'v_pre_mlp_norm', 'v_w_mlp_in', 'v_w_mlp_out', 'v_post_mlp_norm']
TWIN_OUTPUTS = ['loss', 'grad_x', 'grad_pre_mix_norm', 'grad_w_in', 'grad_b_gate', 'grad_conv_w', 'grad_conv_b', 'grad_dt_bias', 'grad_a_log', 'grad_d_skip', 'grad_ssd_norm', 'grad_w_ssd_out', 'grad_attn_sink', 'grad_rel_bias_table', 'grad_w_attn_out', 'grad_w_o', 'grad_post_mix_norm', 'grad_pre_mlp_norm', 'grad_w_mlp_in', 'grad_w_mlp_out', 'grad_post_mlp_norm', 'delta_pre_mix_norm', 'delta_w_in', 'delta_b_gate', 'delta_conv_w', 'delta_conv_b', 'delta_dt_bias', 'delta_a_log', 'delta_d_skip', 'delta_ssd_norm', 'delta_w_ssd_out', 'delta_attn_sink', 'delta_rel_bias_table', 'delta_w_attn_out', 'delta_w_o', 'delta_post_mix_norm', 'delta_pre_mlp_norm', 'delta_w_mlp_in', 'delta_w_mlp_out', 'delta_post_mlp_norm', 'new_m_pre_mix_norm', 'new_m_w_in', 'new_m_b_gate', 'new_m_conv_w', 'new_m_conv_b', 'new_m_dt_bias', 'new_m_a_log', 'new_m_d_skip', 'new_m_ssd_norm', 'new_m_w_ssd_out', 'new_m_attn_sink', 'new_m_rel_bias_table', 'new_m_w_attn_out', 'new_m_w_o', 'new_m_post_mix_norm', 'new_m_pre_mlp_norm', 'new_m_w_mlp_in', 'new_m_w_mlp_out', 'new_m_post_mlp_norm', 'new_v_pre_mix_norm', 'new_v_w_in', 'new_v_b_gate', 'new_v_conv_w', 'new_v_conv_b', 'new_v_dt_bias', 'new_v_a_log', 'new_v_d_skip', 'new_v_ssd_norm', 'new_v_w_ssd_out', 'new_v_attn_sink', 'new_v_rel_bias_table', 'new_v_w_attn_out', 'new_v_w_o', 'new_v_post_mix_norm', 'new_v_pre_mlp_norm', 'new_v_w_mlp_in', 'new_v_w_mlp_out', 'new_v_post_mlp_norm']
TWIN_LEAF_KINDS = {'loss': 'loss', 'grad_x': 'grad_x', 'grad_pre_mix_norm': 'grad_w', 'grad_w_in': 'grad_w', 'grad_b_gate': 'grad_w', 'grad_conv_w': 'grad_w', 'grad_conv_b': 'grad_w', 'grad_dt_bias': 'grad_w', 'grad_a_log': 'grad_w', 'grad_d_skip': 'grad_w', 'grad_ssd_norm': 'grad_w', 'grad_w_ssd_out': 'grad_w', 'grad_attn_sink': 'grad_w', 'grad_rel_bias_table': 'grad_w', 'grad_w_attn_out': 'grad_w', 'grad_w_o': 'grad_w', 'grad_post_mix_norm': 'grad_w', 'grad_pre_mlp_norm': 'grad_w', 'grad_w_mlp_in': 'grad_w', 'grad_w_mlp_out': 'grad_w', 'grad_post_mlp_norm': 'grad_w', 'delta_pre_mix_norm': 'delta_w', 'delta_w_in': 'delta_w', 'delta_b_gate': 'delta_w', 'delta_conv_w': 'delta_w', 'delta_conv_b': 'delta_w', 'delta_dt_bias': 'delta_w', 'delta_a_log': 'delta_w', 'delta_d_skip': 'delta_w', 'delta_ssd_norm': 'delta_w', 'delta_w_ssd_out': 'delta_w', 'delta_attn_sink': 'delta_w', 'delta_rel_bias_table': 'delta_w', 'delta_w_attn_out': 'delta_w', 'delta_w_o': 'delta_w', 'delta_post_mix_norm': 'delta_w', 'delta_pre_mlp_norm': 'delta_w', 'delta_w_mlp_in': 'delta_w', 'delta_w_mlp_out': 'delta_w', 'delta_post_mlp_norm': 'delta_w', 'new_m_pre_mix_norm': 'new_m', 'new_m_w_in': 'new_m', 'new_m_b_gate': 'new_m', 'new_m_conv_w': 'new_m', 'new_m_conv_b': 'new_m', 'new_m_dt_bias': 'new_m', 'new_m_a_log': 'new_m', 'new_m_d_skip': 'new_m', 'new_m_ssd_norm': 'new_m', 'new_m_w_ssd_out': 'new_m', 'new_m_attn_sink': 'new_m', 'new_m_rel_bias_table': 'new_m', 'new_m_w_attn_out': 'new_m', 'new_m_w_o': 'new_m', 'new_m_post_mix_norm': 'new_m', 'new_m_pre_mlp_norm': 'new_m', 'new_m_w_mlp_in': 'new_m', 'new_m_w_mlp_out': 'new_m', 'new_m_post_mlp_norm': 'new_m', 'new_v_pre_mix_norm': 'new_v', 'new_v_w_in': 'new_v', 'new_v_b_gate': 'new_v', 'new_v_conv_w': 'new_v', 'new_v_conv_b': 'new_v', 'new_v_dt_bias': 'new_v', 'new_v_a_log': 'new_v', 'new_v_d_skip': 'new_v', 'new_v_ssd_norm': 'new_v', 'new_v_w_ssd_out': 'new_v', 'new_v_attn_sink': 'new_v', 'new_v_rel_bias_table': 'new_v', 'new_v_w_attn_out': 'new_v', 'new_v_w_o': 'new_v', 'new_v_post_mix_norm': 'new_v', 'new_v_pre_mlp_norm': 'new_v', 'new_v_w_mlp_in': 'new_v', 'new_v_w_mlp_out': 'new_v', 'new_v_post_mlp_norm': 'new_v'}


def _forward(args):
    return _fwd_reference(*[args[k] for k in FWD_PARAMS])


def _output_shape():
    def fwd():
        inp = _fwd_setup_inputs(0)
        return _fwd_reference(*[inp[k] for k in FWD_PARAMS])
    out = _jax.eval_shape(fwd)
    return out.shape, out.dtype

N_MICROBATCH = 1
ADAM_LR = 0.001
ADAM_B1 = 0.9
ADAM_B2 = 0.999
ADAM_EPS = 1e-08
ADAM_WD = 0.01
ADAM_STEP = 10
PER_EXAMPLE_BATCH_AXIS = {'x': 0, 'loss_target': 0}
SHARED_INPUTS = []
_WEIGHT_DTYPES = {'pre_mix_norm': _jnp.float32, 'w_in': _jnp.float32, 'b_gate': _jnp.float32, 'conv_w': _jnp.float32, 'conv_b': _jnp.float32, 'dt_bias': _jnp.float32, 'a_log': _jnp.float32, 'd_skip': _jnp.float32, 'ssd_norm': _jnp.float32, 'w_ssd_out': _jnp.float32, 'attn_sink': _jnp.float32, 'rel_bias_table': _jnp.float32, 'w_attn_out': _jnp.float32, 'w_o': _jnp.float32, 'post_mix_norm': _jnp.float32, 'pre_mlp_norm': _jnp.float32, 'w_mlp_in': _jnp.float32, 'w_mlp_out': _jnp.float32, 'post_mlp_norm': _jnp.float32}
MOMENT_SCALE = {'pre_mix_norm': 8.542546e+00, 'w_in': 3.030213e+00, 'b_gate': 4.151385e+00, 'conv_w': 4.186011e+00, 'conv_b': 1.315965e+01, 'dt_bias': 3.689204e+00, 'a_log': 1.729484e+01, 'd_skip': 1.887461e+01, 'ssd_norm': 9.390970e+00, 'w_ssd_out': 1.244019e+01, 'attn_sink': 1.540505e-01, 'rel_bias_table': 2.014839e-01, 'w_attn_out': 7.268085e+00, 'w_o': 1.470683e+01, 'post_mix_norm': 6.672516e+01, 'pre_mlp_norm': 6.912993e+00, 'w_mlp_in': 3.446107e+00, 'w_mlp_out': 1.990666e+01, 'post_mlp_norm': 7.065559e+01}


def _to_microbatches(a, axis):
    t = _jnp.moveaxis(a, axis, 0)
    t = t.reshape((N_MICROBATCH, t.shape[0] // N_MICROBATCH) + t.shape[1:])
    return _jnp.moveaxis(t, 1, axis + 1)


def setup_inputs(seed: int = 0) -> dict:
    inp = _fwd_setup_inputs(seed)
    key = _jax.random.fold_in(_jax.random.key(seed), 7919)
    shape, _ = _output_shape()
    out = dict(inp)
    out["loss_target"] = _jax.random.normal(_jax.random.fold_in(key, 0), shape, _jnp.float32)
    for i, name in enumerate(TWIN_WEIGHTS):
        w = inp[name].astype(_jnp.float32)
        if MOMENT_SCALE is None:
            s = _jnp.sqrt(_jnp.mean(_jnp.square(w)) + 1e-30)
        else:
            s = MOMENT_SCALE[name]
        km, kv = _jax.random.split(_jax.random.fold_in(key, i + 1))
        out[name] = w
        out["m_" + name] = s * _jax.random.normal(km, w.shape, _jnp.float32)
        out["v_" + name] = (s * s) * _jax.random.uniform(kv, w.shape, _jnp.float32, 0.5, 1.5)
    if N_MICROBATCH > 1:
        for name, axis in PER_EXAMPLE_BATCH_AXIS.items():
            out[name] = _to_microbatches(out[name], axis)
    return {'x': out['x'], 'pre_mix_norm': out['pre_mix_norm'], 'w_in': out['w_in'], 'b_gate': out['b_gate'], 'conv_w': out['conv_w'], 'conv_b': out['conv_b'], 'dt_bias': out['dt_bias'], 'a_log': out['a_log'], 'd_skip': out['d_skip'], 'ssd_norm': out['ssd_norm'], 'w_ssd_out': out['w_ssd_out'], 'attn_sink': out['attn_sink'], 'rel_bias_table': out['rel_bias_table'], 'w_attn_out': out['w_attn_out'], 'w_o': out['w_o'], 'post_mix_norm': out['post_mix_norm'], 'pre_mlp_norm': out['pre_mlp_norm'], 'w_mlp_in': out['w_mlp_in'], 'w_mlp_out': out['w_mlp_out'], 'post_mlp_norm': out['post_mlp_norm'], 'loss_target': out['loss_target'], 'm_pre_mix_norm': out['m_pre_mix_norm'], 'm_w_in': out['m_w_in'], 'm_b_gate': out['m_b_gate'], 'm_conv_w': out['m_conv_w'], 'm_conv_b': out['m_conv_b'], 'm_dt_bias': out['m_dt_bias'], 'm_a_log': out['m_a_log'], 'm_d_skip': out['m_d_skip'], 'm_ssd_norm': out['m_ssd_norm'], 'm_w_ssd_out': out['m_w_ssd_out'], 'm_attn_sink': out['m_attn_sink'], 'm_rel_bias_table': out['m_rel_bias_table'], 'm_w_attn_out': out['m_w_attn_out'], 'm_w_o': out['m_w_o'], 'm_post_mix_norm': out['m_post_mix_norm'], 'm_pre_mlp_norm': out['m_pre_mlp_norm'], 'm_w_mlp_in': out['m_w_mlp_in'], 'm_w_mlp_out': out['m_w_mlp_out'], 'm_post_mlp_norm': out['m_post_mlp_norm'], 'v_pre_mix_norm': out['v_pre_mix_norm'], 'v_w_in': out['v_w_in'], 'v_b_gate': out['v_b_gate'], 'v_conv_w': out['v_conv_w'], 'v_conv_b': out['v_conv_b'], 'v_dt_bias': out['v_dt_bias'], 'v_a_log': out['v_a_log'], 'v_d_skip': out['v_d_skip'], 'v_ssd_norm': out['v_ssd_norm'], 'v_w_ssd_out': out['v_w_ssd_out'], 'v_attn_sink': out['v_attn_sink'], 'v_rel_bias_table': out['v_rel_bias_table'], 'v_w_attn_out': out['v_w_attn_out'], 'v_w_o': out['v_w_o'], 'v_post_mix_norm': out['v_post_mix_norm'], 'v_pre_mlp_norm': out['v_pre_mlp_norm'], 'v_w_mlp_in': out['v_w_mlp_in'], 'v_w_mlp_out': out['v_w_mlp_out'], 'v_post_mlp_norm': out['v_post_mlp_norm']}


def _loss(weights, diff, rest, loss_target):
    with _jax.named_scope("forward"):
        args = {**rest, TWIN_DIFF_INPUT: diff, **{k: w.astype(_WEIGHT_DTYPES[k]) for k, w in weights.items()}}
        y = _forward(args)
    with _jax.named_scope("loss_head"):
        err = _jnp.square(y.astype(_jnp.float32) - loss_target)
        return 0.5 * _jnp.sum(_jnp.mean(err, axis=-1)) if err.ndim else 0.5 * err


def _adamw(w, g, m, v):
    m = ADAM_B1 * m + (1.0 - ADAM_B1) * g
    v = ADAM_B2 * v + (1.0 - ADAM_B2) * _jnp.square(g)
    m_hat = m / (1.0 - ADAM_B1 ** ADAM_STEP)
    v_hat = v / (1.0 - ADAM_B2 ** ADAM_STEP)
    delta = -ADAM_LR * (m_hat / (_jnp.sqrt(v_hat) + ADAM_EPS) + ADAM_WD * w)
    return delta, m, v


def reference(x, pre_mix_norm, w_in, b_gate, conv_w, conv_b, dt_bias, a_log, d_skip, ssd_norm, w_ssd_out, attn_sink, rel_bias_table, w_attn_out, w_o, post_mix_norm, pre_mlp_norm, w_mlp_in, w_mlp_out, post_mlp_norm, loss_target, m_pre_mix_norm, m_w_in, m_b_gate, m_conv_w, m_conv_b, m_dt_bias, m_a_log, m_d_skip, m_ssd_norm, m_w_ssd_out, m_attn_sink, m_rel_bias_table, m_w_attn_out, m_w_o, m_post_mix_norm, m_pre_mlp_norm, m_w_mlp_in, m_w_mlp_out, m_post_mlp_norm, v_pre_mix_norm, v_w_in, v_b_gate, v_conv_w, v_conv_b, v_dt_bias, v_a_log, v_d_skip, v_ssd_norm, v_w_ssd_out, v_attn_sink, v_rel_bias_table, v_w_attn_out, v_w_o, v_post_mix_norm, v_pre_mlp_norm, v_w_mlp_in, v_w_mlp_out, v_post_mlp_norm):
    given = dict(x=x, pre_mix_norm=pre_mix_norm, w_in=w_in, b_gate=b_gate, conv_w=conv_w, conv_b=conv_b, dt_bias=dt_bias, a_log=a_log, d_skip=d_skip, ssd_norm=ssd_norm, w_ssd_out=w_ssd_out, attn_sink=attn_sink, rel_bias_table=rel_bias_table, w_attn_out=w_attn_out, w_o=w_o, post_mix_norm=post_mix_norm, pre_mlp_norm=pre_mlp_norm, w_mlp_in=w_mlp_in, w_mlp_out=w_mlp_out, post_mlp_norm=post_mlp_norm, loss_target=loss_target, m_pre_mix_norm=m_pre_mix_norm, m_w_in=m_w_in, m_b_gate=m_b_gate, m_conv_w=m_conv_w, m_conv_b=m_conv_b, m_dt_bias=m_dt_bias, m_a_log=m_a_log, m_d_skip=m_d_skip, m_ssd_norm=m_ssd_norm, m_w_ssd_out=m_w_ssd_out, m_attn_sink=m_attn_sink, m_rel_bias_table=m_rel_bias_table, m_w_attn_out=m_w_attn_out, m_w_o=m_w_o, m_post_mix_norm=m_post_mix_norm, m_pre_mlp_norm=m_pre_mlp_norm, m_w_mlp_in=m_w_mlp_in, m_w_mlp_out=m_w_mlp_out, m_post_mlp_norm=m_post_mlp_norm, v_pre_mix_norm=v_pre_mix_norm, v_w_in=v_w_in, v_b_gate=v_b_gate, v_conv_w=v_conv_w, v_conv_b=v_conv_b, v_dt_bias=v_dt_bias, v_a_log=v_a_log, v_d_skip=v_d_skip, v_ssd_norm=v_ssd_norm, v_w_ssd_out=v_w_ssd_out, v_attn_sink=v_attn_sink, v_rel_bias_table=v_rel_bias_table, v_w_attn_out=v_w_attn_out, v_w_o=v_w_o, v_post_mix_norm=v_post_mix_norm, v_pre_mlp_norm=v_pre_mlp_norm, v_w_mlp_in=v_w_mlp_in, v_w_mlp_out=v_w_mlp_out, v_post_mlp_norm=v_post_mlp_norm)
    weights = {n: given[n] for n in TWIN_WEIGHTS}
    shared = {n: given[n] for n in SHARED_INPUTS}
    per_example = {n: given[n] for n in ['x']}
    grad_fn = _jax.value_and_grad(_loss, argnums=(0, 1))

    def one_microbatch(ex, loss_target):
        ex = dict(ex)
        diff = ex.pop(TWIN_DIFF_INPUT)
        return grad_fn(weights, diff, {**shared, **ex}, loss_target)

    if N_MICROBATCH == 1:
        loss, (grad_w, grad_x) = one_microbatch(per_example, given["loss_target"])
    else:
        def body(carry, xs):
            loss_sum, grad_sum = carry
            l_k, (gw_k, gx_k) = one_microbatch(xs[0], xs[1])
            with _jax.named_scope("update"):
                return (loss_sum + l_k, _jax.tree.map(_jnp.add, grad_sum, gw_k)), gx_k

        init = (_jnp.zeros((), _jnp.float32), _jax.tree.map(_jnp.zeros_like, weights))
        (loss, grad_w), grad_x = _jax.lax.scan(body, init, (per_example, given["loss_target"]))
    with _jax.named_scope("update"):
        delta_w, new_m, new_v = {}, {}, {}
        for n in TWIN_WEIGHTS:
            delta_w[n], new_m[n], new_v[n] = _adamw(weights[n], grad_w[n], given["m_" + n], given["v_" + n])
    return (loss, grad_x, *[grad_w[n] for n in TWIN_WEIGHTS], *[delta_w[n] for n in TWIN_WEIGHTS],
            *[new_m[n] for n in TWIN_WEIGHTS], *[new_v[n] for n in TWIN_WEIGHTS])
```

```python
import functools
import math

import numpy as np
import jax
import jax.numpy as jnp
from jax import lax
from jax.experimental import pallas as pl
from jax.experimental.pallas import tpu as pltpu

F32 = jnp.float32
BF16 = jnp.bfloat16

D_MODEL = 1024
DEPTH = 2
D_INNER = 2048
SSD_HEADS = 32
SSD_HEAD_DIM = 64
SSD_GROUPS = 8
SSD_REP = 4
SSD_STATE = 128
SSD_CONV = 5
CHUNK = 128
CONV_DIM = 4096
Q_HEADS = 16
KV_HEADS = 4
HEAD_DIM = 64
ATTN_WIDTH = 1024
KV_WIDTH = 256
WINDOW = 128
BLOCK = 128
N_BUCKETS = 32
MAX_DISTANCE = 128
D_FF = 4096
EPS = 1e-6
IN_SPLITS = (D_INNER, CONV_DIM, 2 * SSD_HEADS, ATTN_WIDTH, KV_WIDTH, KV_WIDTH, 2 * D_MODEL)
N_IN = sum(IN_SPLITS)
N_DEV = 8

ADAM_LR = 0.001
ADAM_B1 = 0.9
ADAM_B2 = 0.999
ADAM_EPS = 1e-08
ADAM_WD = 0.01
ADAM_STEP = 10

LANE = 128
HEAD_PAD = LANE
Z0 = 0
GT0 = 2048
Q0 = 4096
XBC0 = 6144
XS0 = 6144
B0 = 8192
C0 = 9216
K0 = 10240
V0 = 10752
DT0 = 11264
NP = 11520
QP = Q_HEADS * HEAD_PAD
KP = KV_HEADS * HEAD_PAD
NEG = -1e30
VMEM_LIMIT = 56 * 1024 * 1024


def _params(n_grid):
    return pltpu.CompilerParams(dimension_semantics=("arbitrary",) * n_grid, vmem_limit_bytes=VMEM_LIMIT)


def _dot(a, b, dims=((1,), (0,))):
    return lax.dot_general(a, b, (dims, ((), ())), preferred_element_type=F32)


def _mx(a):
    return a.astype(BF16)


def _split_dot(x, e):
    hi = x.astype(BF16)
    lo = (x - hi.astype(F32)).astype(BF16)
    return _dot(hi, e) + _dot(lo, e)


def _softplus(x):
    u = jnp.exp(-jnp.abs(x))
    w = 1.0 + u
    log1p = jnp.where(w == 1.0, u, jnp.log(w) * u / jnp.where(w == 1.0, 1.0, w - 1.0))
    return jnp.maximum(x, 0.0) + log1p


def _sigmoid(x):
    return 1.0 / (1.0 + jnp.exp(-x))


def _mm_body(a_ref, b_ref, o_ref, acc_ref, *, nk, ta, tb, split_a):
    k = pl.program_id(2)

    @pl.when(k == 0)
    def _():
        acc_ref[...] = jnp.zeros_like(acc_ref)

    dims = ((0 if ta else 1,), (1 if tb else 0,))
    b = _mx(b_ref[...])
    if split_a:
        a32 = a_ref[...].astype(F32)
        hi = a32.astype(BF16)
        lo = (a32 - hi.astype(F32)).astype(BF16)
        acc_ref[...] += _dot(hi, b, dims) + _dot(lo, b, dims)
    else:
        acc_ref[...] += _dot(_mx(a_ref[...]), b, dims)

    @pl.when(k == nk - 1)
    def _():
        o_ref[...] = acc_ref[...].astype(o_ref.dtype)


def _tile(n, pref):
    t = min(n, pref)
    while n % t:
        t -= LANE
    assert t > 0 and n % t == 0, (n, pref)
    return t


def _mm(a, b, name, ta=False, tb=False, out_dtype=F32, tm=1024, tn=1280, tk=1024, split_a=False):
    m, k = (a.shape[1], a.shape[0]) if ta else a.shape
    n = b.shape[0] if tb else b.shape[1]
    assert (b.shape[1] if tb else b.shape[0]) == k
    tm, tn, tk = _tile(m, tm), _tile(n, tn), _tile(k, tk)
    nk = k // tk
    a_spec = pl.BlockSpec((tk, tm), lambda i, j, kk: (kk, i)) if ta else pl.BlockSpec((tm, tk), lambda i, j, kk: (i, kk))
    b_spec = pl.BlockSpec((tn, tk), lambda i, j, kk: (j, kk)) if tb else pl.BlockSpec((tk, tn), lambda i, j, kk: (kk, j))
    return pl.pallas_call(
        functools.partial(_mm_body, nk=nk, ta=ta, tb=tb, split_a=split_a),
        name=name,
        grid=(m // tm, n // tn, nk),
        in_specs=[a_spec, b_spec],
        out_specs=pl.BlockSpec((tm, tn), lambda i, j, kk: (i, j)),
        out_shape=jax.ShapeDtypeStruct((m, n), out_dtype),
        scratch_shapes=[pltpu.VMEM((tm, tn), F32)],
        compiler_params=_params(3),
    )(a, b)


def _rms_fwd_body(*refs, has_res):
    if has_res:
        x_ref, g_ref, r_ref, o_ref = refs
    else:
        x_ref, g_ref, o_ref = refs
    x = x_ref[...]
    y = x * lax.rsqrt(jnp.mean(x * x, axis=-1, keepdims=True) + EPS) * g_ref[...]
    if has_res:
        y = r_ref[...] + y
    o_ref[...] = y.astype(o_ref.dtype)


def _rms_fwd(x, g, name, res=None, out_dtype=F32, tr=512):
    t, d = x.shape
    tr = min(tr, t)
    row = pl.BlockSpec((tr, d), lambda i: (i, 0))
    vec = pl.BlockSpec((1, d), lambda i: (0, 0))
    args = [x, g.reshape(1, d)] + ([res] if res is not None else [])
    return pl.pallas_call(
        functools.partial(_rms_fwd_body, has_res=res is not None),
        name=name,
        grid=(t // tr,),
        in_specs=[row, vec] + ([row] if res is not None else []),
        out_specs=row,
        out_shape=jax.ShapeDtypeStruct((t, d), out_dtype),
        compiler_params=_params(1),
    )(*args)


def _rms_bwd_body(*refs, has_add):
    if has_add:
        x_ref, g_ref, dy_ref, add_ref, dx_ref, dg_ref = refs
    else:
        x_ref, g_ref, dy_ref, dx_ref, dg_ref = refs

    @pl.when(pl.program_id(0) == 0)
    def _():
        dg_ref[...] = jnp.zeros_like(dg_ref)

    x = x_ref[...]
    dy = dy_ref[...].astype(F32)
    rstd = lax.rsqrt(jnp.mean(x * x, axis=-1, keepdims=True) + EPS)
    n = x * rstd
    dn = dy * g_ref[...]
    dx = rstd * (dn - n * jnp.mean(dn * n, axis=-1, keepdims=True))
    if has_add:
        dx = dx + add_ref[...]
    dx_ref[...] = dx.astype(dx_ref.dtype)
    dg_ref[...] += jnp.sum(dy * n, axis=0, keepdims=True)


def _rms_bwd(x, g, dy, name, add=None, out_dtype=F32, tr=512):
    t, d = x.shape
    tr = min(tr, t)
    row = pl.BlockSpec((tr, d), lambda i: (i, 0))
    vec = pl.BlockSpec((1, d), lambda i: (0, 0))
    args = [x, g.reshape(1, d), dy] + ([add] if add is not None else [])
    return pl.pallas_call(
        functools.partial(_rms_bwd_body, has_add=add is not None),
        name=name,
        grid=(t // tr,),
        in_specs=[row, vec, row] + ([row] if add is not None else []),
        out_specs=[row, vec],
        out_shape=[jax.ShapeDtypeStruct((t, d), out_dtype), jax.ShapeDtypeStruct((1, d), F32)],
        compiler_params=_params(1),
    )(*args)


def _relu2_body(f_ref, o_ref):
    r = jnp.maximum(f_ref[...], 0.0)
    o_ref[...] = (r * r).astype(o_ref.dtype)


def _relu2(f, name, tr=512):
    t, d = f.shape
    tr = min(tr, t)
    row = pl.BlockSpec((tr, d), lambda i: (i, 0))
    return pl.pallas_call(functools.partial(_relu2_body), name=name, grid=(t // tr,), in_specs=[row], out_specs=row,
                          out_shape=jax.ShapeDtypeStruct((t, d), BF16), compiler_params=_params(1))(f)


def _relu2_bwd_body(da_ref, f_ref, o_ref):
    o_ref[...] = (da_ref[...] * 2.0 * jnp.maximum(f_ref[...], 0.0)).astype(o_ref.dtype)


def _relu2_bwd(da, f, name, tr=512):
    t, d = f.shape
    tr = min(tr, t)
    row = pl.BlockSpec((tr, d), lambda i: (i, 0))
    return pl.pallas_call(functools.partial(_relu2_bwd_body), name=name, grid=(t // tr,), in_specs=[row, row], out_specs=row,
                          out_shape=jax.ShapeDtypeStruct((t, d), BF16), compiler_params=_params(1))(da, f)


def _loss_body(y_ref, t_ref, dy_ref, l_ref):
    @pl.when(pl.program_id(0) == 0)
    def _():
        l_ref[...] = jnp.zeros_like(l_ref)

    e = y_ref[...] - t_ref[...]
    dy_ref[...] = e * (1.0 / D_MODEL)
    s = jnp.sum(e * e, axis=0, keepdims=True)
    acc = s[:, 0:LANE]
    for j in range(1, D_MODEL // LANE):
        acc = acc + s[:, j * LANE:(j + 1) * LANE]
    l_ref[...] += acc


def _loss(y, target, tr=512):
    t, d = y.shape
    tr = min(tr, t)
    row = pl.BlockSpec((tr, d), lambda i: (i, 0))
    return pl.pallas_call(
        functools.partial(_loss_body), name="loss_head", grid=(t // tr,), in_specs=[row, row],
        out_specs=[row, pl.BlockSpec((1, LANE), lambda i: (0, 0))],
        out_shape=[jax.ShapeDtypeStruct((t, d), F32), jax.ShapeDtypeStruct((1, LANE), F32)],
        compiler_params=_params(1),
    )(y, target)


CONV_CT = 128
CONV_RB = 512
CONV_PAD = 8


def _conv_rows(t):
    return min(CONV_RB, t)


def _conv_fwd_body(u_ref, w_ref, b_ref, o_ref, pad_ref, *, t):
    rb = _conv_rows(t)
    zeros = jnp.zeros((CONV_PAD, CONV_CT), F32)
    pad_ref[pl.ds(0, CONV_PAD), :] = zeros
    pad_ref[pl.ds(t + CONV_PAD, CONV_PAD), :] = zeros
    pad_ref[pl.ds(CONV_PAD, t), :] = u_ref[...]
    w = w_ref[...]
    b = b_ref[...]
    for c in range(t // rb):
        base = CONV_PAD + c * rb
        acc = b + w[0:1, :] * pad_ref[pl.ds(base - 2, rb), :]
        for k in range(1, SSD_CONV):
            acc = acc + w[k:k + 1, :] * pad_ref[pl.ds(base + k - 2, rb), :]
        o_ref[pl.ds(c * rb, rb), :] = acc * _sigmoid(acc)


def _conv_fwd(proj, conv_w8, conv_b):
    t = proj.shape[0]
    off = XBC0 // CONV_CT
    return pl.pallas_call(
        functools.partial(_conv_fwd_body, t=t), name="conv_fwd", grid=(CONV_DIM // CONV_CT,),
        in_specs=[pl.BlockSpec((t, CONV_CT), lambda i: (0, off + i)), pl.BlockSpec((8, CONV_CT), lambda i: (0, i)),
                  pl.BlockSpec((1, CONV_CT), lambda i: (0, i))],
        out_specs=pl.BlockSpec((t, CONV_CT), lambda i: (0, i)),
        out_shape=jax.ShapeDtypeStruct((t, CONV_DIM), F32),
        scratch_shapes=[pltpu.VMEM((t + 2 * CONV_PAD, CONV_CT), F32)],
        compiler_params=_params(1),
    )(proj, conv_w8, conv_b)


def _conv_bwd_body(u_ref, da_ref, w_ref, b_ref, du_ref, dw_ref, pad_ref, pad2_ref, *, t):
    rb = _conv_rows(t)
    zeros = jnp.zeros((CONV_PAD, CONV_CT), F32)
    for p in (pad_ref, pad2_ref):
        p[pl.ds(0, CONV_PAD), :] = zeros
        p[pl.ds(t + CONV_PAD, CONV_PAD), :] = zeros
    pad_ref[pl.ds(CONV_PAD, t), :] = u_ref[...]
    w = w_ref[...]
    b = b_ref[...]
    dw = [jnp.zeros((1, CONV_CT), F32) for _ in range(SSD_CONV + 1)]
    for c in range(t // rb):
        base = CONV_PAD + c * rb
        us = [pad_ref[pl.ds(base + k - 2, rb), :] for k in range(SSD_CONV)]
        acc = b + w[0:1, :] * us[0]
        for k in range(1, SSD_CONV):
            acc = acc + w[k:k + 1, :] * us[k]
        sg = _sigmoid(acc)
        dyc = da_ref[pl.ds(c * rb, rb), :] * (sg * (1.0 + acc * (1.0 - sg)))
        pad2_ref[pl.ds(base, rb), :] = dyc
        for k in range(SSD_CONV):
            dw[k] = dw[k] + jnp.sum(dyc * us[k], axis=0, keepdims=True)
        dw[SSD_CONV] = dw[SSD_CONV] + jnp.sum(dyc, axis=0, keepdims=True)
    for c in range(t // rb):
        base = CONV_PAD + c * rb
        acc = w[0:1, :] * pad2_ref[pl.ds(base + 2, rb), :]
        for k in range(1, SSD_CONV):
            acc = acc + w[k:k + 1, :] * pad2_ref[pl.ds(base + 2 - k, rb), :]
        du_ref[pl.ds(c * rb, rb), :] = acc.astype(du_ref.dtype)
    dw_ref[...] = jnp.concatenate(dw + [jnp.zeros((2, CONV_CT), F32)], axis=0)


def _conv_bwd(proj, dact, conv_w8, conv_b):
    t = proj.shape[0]
    off = XBC0 // CONV_CT
    col = pl.BlockSpec((t, CONV_CT), lambda i: (0, i))
    return pl.pallas_call(
        functools.partial(_conv_bwd_body, t=t), name="conv_bwd", grid=(CONV_DIM // CONV_CT,),
        in_specs=[pl.BlockSpec((t, CONV_CT), lambda i: (0, off + i)), col, pl.BlockSpec((8, CONV_CT), lambda i: (0, i)),
                  pl.BlockSpec((1, CONV_CT), lambda i: (0, i))],
        out_specs=[col, pl.BlockSpec((8, CONV_CT), lambda i: (0, i))],
        out_shape=[jax.ShapeDtypeStruct((t, CONV_DIM), BF16), jax.ShapeDtypeStruct((8, CONV_DIM), F32)],
        scratch_shapes=[pltpu.VMEM((t + 2 * CONV_PAD, CONV_CT), F32), pltpu.VMEM((t + 2 * CONV_PAD, CONV_CT), F32)],
        compiler_params=_params(1),
    )(proj, dact, conv_w8, conv_b)


GW = SSD_REP * SSD_HEAD_DIM


def _expand_np(hoff):
    e = np.zeros((LANE, D_INNER), np.float32)
    for h in range(SSD_HEADS):
        e[hoff + h, h * SSD_HEAD_DIM:(h + 1) * SSD_HEAD_DIM] = 1.0
    return e


def _head_masks():
    lane = lax.broadcasted_iota(jnp.int32, (1, GW), 1)
    return [((lane >= r * SSD_HEAD_DIM) & (lane < (r + 1) * SSD_HEAD_DIM)).astype(F32) for r in range(SSD_REP)]


def _ssd_common(dtc_ref, dtr_ref, bc_ref, br_ref, alc_ref, alr_ref, e_ref, rev):
    L = CHUNK
    ri = lax.broadcasted_iota(jnp.int32, (L, L), 0)
    ci = lax.broadcasted_iota(jnp.int32, (L, L), 1)
    tri = (ri <= ci) if rev else (ri >= ci)
    trit = (ri >= ci) if rev else (ri <= ci)
    raw_c = dtc_ref[...] + bc_ref[...]
    dt_c = _softplus(raw_c)
    a_c = -jnp.exp(alc_ref[...])
    cs_c = jnp.dot(tri.astype(F32), dt_c * a_c, precision=lax.Precision.HIGHEST, preferred_element_type=F32)
    dt_r = _softplus(dtr_ref[...] + br_ref[...])
    a_r = -jnp.exp(alr_ref[...])
    cs_r = jnp.dot(dt_r * a_r, trit.astype(F32), precision=lax.Precision.HIGHEST, preferred_element_type=F32)
    il = 0 if rev else L - 1
    e_c = jnp.exp(cs_c)
    w_c = jnp.exp(cs_c[il:il + 1, :] - cs_c)
    ex = _split_dot(jnp.concatenate([dt_c, e_c, w_c], axis=0), e_ref[...])
    return dict(tri=tri, trit=trit, raw_c=raw_c, dt_c=dt_c, a_c=a_c, cs_c=cs_c, cs_r=cs_r, il=il, e_last=e_c[il:il + 1],
                dt_x=ex[0:L], e_x=ex[L:2 * L], w_x=ex[2 * L:3 * L], cd_x=ex[L + il:L + il + 1])


def _ssd_fwd_body(xs_ref, b_ref, c_ref, dtc_ref, dtr_ref, bc_ref, br_ref, alc_ref, alr_ref, e_ref, y_ref, st_ref, s_scr,
                  *, rev, hoff):
    L = CHUNK

    @pl.when(pl.program_id(0) == 0)
    def _():
        s_scr[...] = jnp.zeros_like(s_scr)

    st_ref[0] = s_scr[...]
    q = _ssd_common(dtc_ref, dtr_ref, bc_ref, br_ref, alc_ref, alr_ref, e_ref, rev)
    masks = _head_masks()
    xdt = xs_ref[...] * q["dt_x"]
    xw = xdt * q["w_x"]
    for g in range(SSD_GROUPS):
        gc = slice(g * GW, (g + 1) * GW)
        nc = slice(g * SSD_STATE, (g + 1) * SSD_STATE)
        bg = _mx(b_ref[:, nc])
        cg = _mx(c_ref[:, nc])
        cb = _dot(cg, bg, ((1,), (1,)))
        sg = s_scr[:, gc]
        y = _dot(cg, _mx(sg)) * q["e_x"][:, gc]
        ms = []
        for r in range(SSD_REP):
            hh = hoff + g * SSD_REP + r
            diff = q["cs_c"][:, hh:hh + 1] - q["cs_r"][hh:hh + 1, :]
            ms.append(_mx(cb * jnp.exp(jnp.where(q["tri"], diff, NEG))))
        y4 = _dot(jnp.concatenate(ms, axis=0), _mx(xdt[:, gc]))
        for r in range(SSD_REP):
            y = y + y4[r * L:(r + 1) * L] * masks[r]
        y_ref[:, gc] = y
        s_scr[:, gc] = sg * q["cd_x"][:, gc] + _dot(bg, _mx(xw[:, gc]), ((0,), (0,)))


def _ssd_specs(t, rev):
    nc = t // CHUNK
    cm = (lambda i: nc - 1 - i) if rev else (lambda i: i)
    xs = pl.BlockSpec((CHUNK, D_INNER), lambda i: (cm(i), 0))
    bb = pl.BlockSpec((CHUNK, SSD_GROUPS * SSD_STATE), lambda i: (cm(i), 2))
    cc = pl.BlockSpec((CHUNK, SSD_GROUPS * SSD_STATE), lambda i: (cm(i), 3))
    dtc = pl.BlockSpec((CHUNK, LANE), lambda i: (cm(i), DT0 // LANE))
    dtr = pl.BlockSpec((LANE, CHUNK), lambda i: (0, cm(i)))
    colv = pl.BlockSpec((1, LANE), lambda i: (0, 0))
    rowv = pl.BlockSpec((LANE, 1), lambda i: (0, 0))
    return cm, xs, bb, cc, dtc, dtr, colv, rowv


def _ssd_fwd(xbc, proj, dtt, bias, alog, rev):
    t = xbc.shape[0]
    nc = t // CHUNK
    cm, xs, bb, cc, dtc, dtr, colv, rowv = _ssd_specs(t, rev)
    hoff = SSD_HEADS if rev else 0
    e = jnp.asarray(_expand_np(hoff), BF16)
    return pl.pallas_call(
        functools.partial(_ssd_fwd_body, rev=rev, hoff=hoff), name="ssd_fwd_rev" if rev else "ssd_fwd", grid=(nc,),
        in_specs=[xs, bb, cc, dtc, dtr, colv, rowv, colv, rowv, pl.BlockSpec((LANE, D_INNER), lambda i: (0, 0))],
        out_specs=[xs, pl.BlockSpec((1, SSD_STATE, D_INNER), lambda i: (cm(i), 0, 0))],
        out_shape=[jax.ShapeDtypeStruct((t, D_INNER), F32), jax.ShapeDtypeStruct((nc, SSD_STATE, D_INNER), F32)],
        scratch_shapes=[pltpu.VMEM((SSD_STATE, D_INNER), F32)],
        compiler_params=_params(1),
    )(xbc, xbc, xbc, proj, dtt, bias.reshape(1, LANE), bias.reshape(LANE, 1), alog.reshape(1, LANE), alog.reshape(LANE, 1), e)


def _ssd_bwd_body(*refs, rev, hoff, first):
    L = CHUNK
    (xs_ref, b_ref, c_ref, dtc_ref, dtr_ref, bc_ref, br_ref, alc_ref, alr_ref, e_ref, et_ref, dy_ref, sp_ref,
     dsk_ref) = refs[:14]
    if first:
        dx_ref, ddt_ref, sm_ref, ds_scr = refs[14:]
    else:
        pdx_ref, pddt_ref, psm_ref, dx_ref, ddt_ref, sm_ref, ds_scr = refs[14:]

    @pl.when(pl.program_id(0) == 0)
    def _():
        ds_scr[...] = jnp.zeros_like(ds_scr)
        sm_ref[...] = jnp.zeros_like(sm_ref) if first else psm_ref[...]

    q = _ssd_common(dtc_ref, dtr_ref, bc_ref, br_ref, alc_ref, alr_ref, e_ref, rev)
    masks = _head_masks()
    xs = xs_ref[...]
    dy = dy_ref[...]
    xdt = xs * q["dt_x"]
    xw = xdt * q["w_x"]
    dye = dy * q["e_x"]
    ds_old = ds_scr[...]
    dxdt_parts, q1_parts, q2_parts = [], [], []
    ddiag = jnp.zeros((L, LANE), F32)
    for g in range(SSD_GROUPS):
        gc = slice(g * GW, (g + 1) * GW)
        nc = slice(g * SSD_STATE, (g + 1) * SSD_STATE)
        bg = _mx(b_ref[:, nc])
        cg = _mx(c_ref[:, nc])
        cb = _dot(cg, bg, ((1,), (1,)))
        cbt = _dot(bg, cg, ((1,), (1,)))
        dy_g = dy[:, gc]
        xdt_g = xdt[:, gc]
        dsg = ds_old[:, gc]
        dm = _dot(_mx(jnp.concatenate([dy_g * m for m in masks], axis=0)), _mx(xdt_g), ((1,), (1,)))
        dmt = _dot(_mx(jnp.concatenate([xdt_g * m for m in masks], axis=0)), _mx(dy_g), ((1,), (1,)))
        dcb = jnp.zeros((L, L), F32)
        dcbt = jnp.zeros((L, L), F32)
        mts = []
        for r in range(SSD_REP):
            hh = hoff + g * SSD_REP + r
            col = q["cs_c"][:, hh:hh + 1]
            row = q["cs_r"][hh:hh + 1, :]
            dec = jnp.exp(jnp.where(q["tri"], col - row, NEG))
            dect = jnp.exp(jnp.where(q["trit"], row - col, NEG))
            pd = dm[r * L:(r + 1) * L] * dec
            pdt = dmt[r * L:(r + 1) * L] * dect
            dcb = dcb + pd
            dcbt = dcbt + pdt
            mts.append(_mx(cbt * dect))
            ddiag = ddiag + (jnp.sum(pd * cb, axis=1, keepdims=True) - jnp.sum(pdt * cbt, axis=1, keepdims=True)) * _onehot_lane(hh)
        x4 = _dot(jnp.concatenate(mts, axis=0), _mx(dy_g))
        bds = _dot(bg, _mx(dsg))
        dxdt_g = bds * q["w_x"][:, gc]
        for r in range(SSD_REP):
            dxdt_g = dxdt_g + x4[r * L:(r + 1) * L] * masks[r]
        dxdt_parts.append(dxdt_g)
        spg = _mx(sp_ref[0, :, gc])
        q1_parts.append(dye[:, gc] * _dot(cg, spg))
        q2_parts.append(xw[:, gc] * bds)
        dc_g = _dot(_mx(dcb), bg) + _dot(_mx(dye[:, gc]), spg, ((1,), (1,)))
        db_g = _dot(_mx(dcbt), cg) + _dot(_mx(xw[:, gc]), _mx(dsg), ((1,), (1,)))
        boff = D_INNER + g * SSD_STATE
        coff = D_INNER + SSD_GROUPS * SSD_STATE + g * SSD_STATE
        if first:
            dx_ref[:, boff:boff + SSD_STATE] = db_g
            dx_ref[:, coff:coff + SSD_STATE] = dc_g
        else:
            dx_ref[:, boff:boff + SSD_STATE] = pdx_ref[:, boff:boff + SSD_STATE] + db_g
            dx_ref[:, coff:coff + SSD_STATE] = pdx_ref[:, coff:coff + SSD_STATE] + dc_g
        ds_scr[:, gc] = dsg * q["cd_x"][:, gc] + _dot(cg, _mx(dye[:, gc]), ((0,), (0,)))
    dxdt = jnp.concatenate(dxdt_parts, axis=1)
    et = et_ref[...]
    hs = _split_dot(jnp.concatenate([jnp.concatenate(q1_parts, axis=1), jnp.concatenate(q2_parts, axis=1), dxdt * xs], axis=0), et)
    q1, q2, r3 = hs[0:L], hs[L:2 * L], hs[2 * L:3 * L]
    t_prev = _split_dot(jnp.sum(ds_old * sp_ref[0], axis=0, keepdims=True), et) * q["e_last"]
    rows = lax.broadcasted_iota(jnp.int32, (L, LANE), 0)
    dcs = ddiag + q1 - q2 + jnp.where(rows == q["il"], jnp.sum(q2, axis=0, keepdims=True) + t_prev, 0.0)
    dad = jnp.dot(q["trit"].astype(F32), dcs, precision=lax.Precision.HIGHEST, preferred_element_type=F32)
    ddt = dad * q["a_c"] + r3
    ddt_raw = ddt * _sigmoid(q["raw_c"])
    dal = jnp.sum(dad * q["dt_c"], axis=0, keepdims=True) * q["a_c"]
    dbias = jnp.sum(ddt_raw, axis=0, keepdims=True)
    sm_ref[...] += jnp.concatenate([dal, dbias, jnp.zeros((6, LANE), F32)], axis=0)
    dxs = dxdt * q["dt_x"]
    if first:
        dx_ref[:, 0:D_INNER] = dxs + dy * dsk_ref[...]
        ddt_ref[...] = ddt_raw
    else:
        dx_ref[:, 0:D_INNER] = pdx_ref[:, 0:D_INNER] + dxs
        ddt_ref[...] = pddt_ref[...] + ddt_raw


def _ssd_bwd(xbc, proj, dtt, bias, alog, dy, states, dskip_x, rev, prev=None):
    t = xbc.shape[0]
    nc = t // CHUNK
    first = prev is None
    cm, xs, bb, cc, dtc, dtr, colv, rowv = _ssd_specs(t, not rev)
    hoff = SSD_HEADS if rev else 0
    e_np = _expand_np(hoff)
    e = jnp.asarray(e_np, BF16)
    et = jnp.asarray(e_np.T, BF16)
    st_spec = pl.BlockSpec((1, SSD_STATE, D_INNER), lambda i: (cm(i), 0, 0))
    dxo = pl.BlockSpec((CHUNK, CONV_DIM), lambda i: (cm(i), 0))
    ddto = pl.BlockSpec((CHUNK, LANE), lambda i: (cm(i), 0))
    smo = pl.BlockSpec((8, LANE), lambda i: (0, 0))
    in_specs = [xs, bb, cc, dtc, dtr, colv, rowv, colv, rowv, pl.BlockSpec((LANE, D_INNER), lambda i: (0, 0)),
                pl.BlockSpec((D_INNER, LANE), lambda i: (0, 0)), xs, st_spec, pl.BlockSpec((1, D_INNER), lambda i: (0, 0))]
    args = [xbc, xbc, xbc, proj, dtt, bias.reshape(1, LANE), bias.reshape(LANE, 1), alog.reshape(1, LANE),
            alog.reshape(LANE, 1), e, et, dy, states, dskip_x]
    if not first:
        in_specs += [dxo, ddto, smo]
        args += list(prev)
    return pl.pallas_call(
        functools.partial(_ssd_bwd_body, rev=rev, hoff=hoff, first=first), name="ssd_bwd_rev" if rev else "ssd_bwd",
        grid=(nc,), in_specs=in_specs, out_specs=[dxo, ddto, smo],
        out_shape=[jax.ShapeDtypeStruct((t, CONV_DIM), F32), jax.ShapeDtypeStruct((t, LANE), F32),
                   jax.ShapeDtypeStruct((8, LANE), F32)],
        scratch_shapes=[pltpu.VMEM((SSD_STATE, D_INNER), F32)],
        compiler_params=_params(1),
    )(*args)


def _gnorm_parts(yf_ref, yb_ref, xs_ref, z_ref, dsk_ref):
    xs = xs_ref[...]
    y = yf_ref[...] + yb_ref[...] + xs * dsk_ref[...]
    z = z_ref[...]
    sg = _sigmoid(z)
    s = z * sg
    v = y * s
    rs = []
    for g in range(SSD_GROUPS):
        vg = v[:, g * GW:(g + 1) * GW]
        rs.append(jnp.broadcast_to(lax.rsqrt(jnp.mean(vg * vg, axis=-1, keepdims=True) + EPS), vg.shape))
    return xs, y, z, sg, s, v, jnp.concatenate(rs, axis=1)


def _gnorm_fwd_body(yf_ref, yb_ref, xs_ref, z_ref, dsk_ref, w_ref, o_ref):
    _, _, _, _, _, v, rstd = _gnorm_parts(yf_ref, yb_ref, xs_ref, z_ref, dsk_ref)
    o_ref[...] = (v * rstd * w_ref[...]).astype(o_ref.dtype)


def _gnorm_fwd(yf, yb, xbc, proj, dskip_x, norm_w, tr=256):
    t = yf.shape[0]
    tr = min(tr, t)
    row = pl.BlockSpec((tr, D_INNER), lambda i: (i, 0))
    vec = pl.BlockSpec((1, D_INNER), lambda i: (0, 0))
    return pl.pallas_call(
        functools.partial(_gnorm_fwd_body), name="gnorm_fwd", grid=(t // tr,),
        in_specs=[row, row, row, pl.BlockSpec((tr, D_INNER), lambda i: (i, Z0 // D_INNER)), vec, vec],
        out_specs=row, out_shape=jax.ShapeDtypeStruct((t, D_INNER), BF16), compiler_params=_params(1),
    )(yf, yb, xbc, proj, dskip_x, norm_w.reshape(1, D_INNER))


def _gnorm_bwd_body(du_ref, yf_ref, yb_ref, xs_ref, z_ref, dsk_ref, w_ref, et_ref, dy_ref, dz_ref, dw_ref, dd_ref, acc_ref, *, nsteps):
    i = pl.program_id(0)

    @pl.when(i == 0)
    def _():
        dw_ref[...] = jnp.zeros_like(dw_ref)
        acc_ref[...] = jnp.zeros_like(acc_ref)

    xs, y, z, sg, s, v, rstd = _gnorm_parts(yf_ref, yb_ref, xs_ref, z_ref, dsk_ref)
    du = du_ref[...]
    n = v * rstd
    dn = du * w_ref[...]
    dw_ref[...] += jnp.sum(du * n, axis=0, keepdims=True)
    prod = dn * n
    means = []
    for g in range(SSD_GROUPS):
        pg = prod[:, g * GW:(g + 1) * GW]
        means.append(jnp.broadcast_to(jnp.mean(pg, axis=-1, keepdims=True), pg.shape))
    dv = rstd * (dn - n * jnp.concatenate(means, axis=1))
    dy = dv * s
    dy_ref[...] = dy
    dz_ref[...] = (dv * y * (sg * (1.0 + z * (1.0 - sg)))).astype(dz_ref.dtype)
    acc_ref[...] += jnp.sum(dy * xs, axis=0, keepdims=True)

    @pl.when(i == nsteps - 1)
    def _():
        dd_ref[...] = _split_dot(acc_ref[...], et_ref[...])


def _gnorm_bwd(du, yf, yb, xbc, proj, dskip_x, norm_w, tr=256):
    t = yf.shape[0]
    tr = min(tr, t)
    row = pl.BlockSpec((tr, D_INNER), lambda i: (i, 0))
    vec = pl.BlockSpec((1, D_INNER), lambda i: (0, 0))
    et = jnp.asarray(_expand_np(0).T, BF16)
    return pl.pallas_call(
        functools.partial(_gnorm_bwd_body, nsteps=t // tr), name="gnorm_bwd", grid=(t // tr,),
        in_specs=[row, row, row, row, pl.BlockSpec((tr, D_INNER), lambda i: (i, Z0 // D_INNER)), vec, vec,
                  pl.BlockSpec((D_INNER, LANE), lambda i: (0, 0))],
        out_specs=[row, row, vec, pl.BlockSpec((1, LANE), lambda i: (0, 0))],
        out_shape=[jax.ShapeDtypeStruct((t, D_INNER), F32), jax.ShapeDtypeStruct((t, D_INNER), BF16),
                   jax.ShapeDtypeStruct((1, D_INNER), F32), jax.ShapeDtypeStruct((1, LANE), F32)],
        scratch_shapes=[pltpu.VMEM((1, D_INNER), F32)],
        compiler_params=_params(1),
    )(du, yf, yb, xbc, proj, dskip_x, norm_w.reshape(1, D_INNER), et)


def _gate_fwd_body(gt_ref, b_ref, ys_ref, ya_ref, o_ref):
    g = _sigmoid(gt_ref[...] + b_ref[...])
    o_ref[...] = (g[:, 0:D_MODEL] * ys_ref[...] + g[:, D_MODEL:] * ya_ref[...]).astype(o_ref.dtype)


def _gate_fwd(proj, b_gate, y_ssd, y_attn, tr=512):
    t = y_ssd.shape[0]
    tr = min(tr, t)
    row = pl.BlockSpec((tr, D_MODEL), lambda i: (i, 0))
    return pl.pallas_call(
        functools.partial(_gate_fwd_body), name="gate_fwd", grid=(t // tr,),
        in_specs=[pl.BlockSpec((tr, 2 * D_MODEL), lambda i: (i, GT0 // (2 * D_MODEL))),
                  pl.BlockSpec((1, 2 * D_MODEL), lambda i: (0, 0)), row, row],
        out_specs=row, out_shape=jax.ShapeDtypeStruct((t, D_MODEL), BF16), compiler_params=_params(1),
    )(proj, b_gate.reshape(1, 2 * D_MODEL), y_ssd, y_attn)


def _gate_bwd_body(dm_ref, gt_ref, b_ref, ys_ref, ya_ref, dys_ref, dya_ref, dgt_ref, db_ref):
    @pl.when(pl.program_id(0) == 0)
    def _():
        db_ref[...] = jnp.zeros_like(db_ref)

    g = _sigmoid(gt_ref[...] + b_ref[...])
    gs = g[:, 0:D_MODEL]
    ga = g[:, D_MODEL:]
    dm = dm_ref[...]
    dys_ref[...] = (dm * gs).astype(dys_ref.dtype)
    dya_ref[...] = (dm * ga).astype(dya_ref.dtype)
    dgt = jnp.concatenate([dm * ys_ref[...] * gs * (1.0 - gs), dm * ya_ref[...] * ga * (1.0 - ga)], axis=1)
    dgt_ref[...] = dgt.astype(dgt_ref.dtype)
    db_ref[...] += jnp.sum(dgt, axis=0, keepdims=True)


def _gate_bwd(dmix, proj, b_gate, y_ssd, y_attn, tr=512):
    t = y_ssd.shape[0]
    tr = min(tr, t)
    row = pl.BlockSpec((tr, D_MODEL), lambda i: (i, 0))
    row2 = pl.BlockSpec((tr, 2 * D_MODEL), lambda i: (i, 0))
    vec2 = pl.BlockSpec((1, 2 * D_MODEL), lambda i: (0, 0))
    return pl.pallas_call(
        functools.partial(_gate_bwd_body), name="gate_bwd", grid=(t // tr,),
        in_specs=[row, pl.BlockSpec((tr, 2 * D_MODEL), lambda i: (i, GT0 // (2 * D_MODEL))), vec2, row, row],
        out_specs=[row, row, row2, vec2],
        out_shape=[jax.ShapeDtypeStruct((t, D_MODEL), BF16), jax.ShapeDtypeStruct((t, D_MODEL), BF16),
                   jax.ShapeDtypeStruct((t, 2 * D_MODEL), BF16), jax.ShapeDtypeStruct((1, 2 * D_MODEL), F32)],
        compiler_params=_params(1),
    )(dmix, proj, b_gate.reshape(1, 2 * D_MODEL), y_ssd, y_attn)


def _bucket_np():
    i = np.arange(BLOCK)[:, None]
    j = np.arange(3 * BLOCK)[None, :]
    rel = j - BLOCK - i
    nb = N_BUCKETS // 2
    max_exact = nb // 2
    ret = np.where(rel > 0, nb, 0)
    n = np.abs(rel)
    nf = np.maximum(n, 1).astype(np.float32)
    large = max_exact + (np.log(nf / np.float32(max_exact)) / np.float32(math.log(MAX_DISTANCE / max_exact))
                         * np.float32(nb - max_exact)).astype(np.int32)
    large = np.minimum(large, nb - 1)
    bucket = ret + np.where(n < max_exact, n, large)
    band = np.abs(rel) <= WINDOW
    return bucket, band


def _onehot_np():
    bucket, _ = _bucket_np()
    oh = np.zeros((BLOCK * 3 * BLOCK, LANE), np.float32)
    oh[np.arange(oh.shape[0]), bucket.reshape(-1)] = 1.0
    return oh


def _attn_logits(q_ref, kg, bias_ref, mask_ref, edge, h):
    qh = _mx(q_ref[:, h * HEAD_PAD:(h + 1) * HEAD_PAD])
    s = _dot(qh, kg, ((1,), (1,))) * (HEAD_DIM ** -0.5) + bias_ref[h] + mask_ref[...] + edge
    return qh, s


def _attn_edge(n, nb):
    col = lax.broadcasted_iota(jnp.int32, (1, 3 * BLOCK), 1)
    bad = ((col < BLOCK) & (n == 0)) | ((col >= 2 * BLOCK) & (n == nb - 1))
    return jnp.where(bad, NEG, 0.0)


def _onehot_lane(h):
    return (lax.broadcasted_iota(jnp.int32, (1, LANE), 1) == h).astype(F32)


def _attn_fwd_body(q_ref, kp_ref, kc_ref, kn_ref, vp_ref, vc_ref, vn_ref, bias_ref, mask_ref, sink_ref, o_ref, lse_ref, *, nb):
    n = pl.program_id(0)
    edge = _attn_edge(n, nb)
    lse = jnp.zeros((BLOCK, LANE), F32)
    for g in range(KV_HEADS):
        gc = slice(g * HEAD_PAD, (g + 1) * HEAD_PAD)
        kg = _mx(jnp.concatenate([kp_ref[:, gc], kc_ref[:, gc], kn_ref[:, gc]], axis=0))
        vg = _mx(jnp.concatenate([vp_ref[:, gc], vc_ref[:, gc], vn_ref[:, gc]], axis=0))
        for r in range(Q_HEADS // KV_HEADS):
            h = g * (Q_HEADS // KV_HEADS) + r
            _, s = _attn_logits(q_ref, kg, bias_ref, mask_ref, edge, h)
            sink = sink_ref[:, h:h + 1]
            m = jnp.maximum(jnp.max(s, axis=-1, keepdims=True), sink)
            p = jnp.exp(s - m)
            den = jnp.sum(p, axis=-1, keepdims=True) + jnp.exp(sink - m)
            o_ref[:, h * HEAD_PAD:(h + 1) * HEAD_PAD] = _dot(_mx(p / den), vg).astype(o_ref.dtype)
            lse = lse + (m + jnp.log(den)) * _onehot_lane(h)
    lse_ref[...] = lse


def _attn_specs(t, clamp):
    nb = t // BLOCK
    cur = (lambda n: jnp.minimum(n, nb - 1)) if clamp else (lambda n: n)
    prv = lambda n: jnp.maximum(cur(n) - 1, 0)
    nxt = lambda n: jnp.minimum(cur(n) + 1, nb - 1)
    kb, vb = K0 // KP, V0 // KP
    qs = pl.BlockSpec((BLOCK, QP), lambda n: (cur(n), Q0 // QP))
    ks = [pl.BlockSpec((BLOCK, KP), lambda n, f=f: (f(n), kb)) for f in (prv, cur, nxt)]
    vs = [pl.BlockSpec((BLOCK, KP), lambda n, f=f: (f(n), vb)) for f in (prv, cur, nxt)]
    consts = [pl.BlockSpec((Q_HEADS, BLOCK, 3 * BLOCK), lambda n: (0, 0, 0)), pl.BlockSpec((BLOCK, 3 * BLOCK), lambda n: (0, 0)),
              pl.BlockSpec((1, LANE), lambda n: (0, 0))]
    return nb, cur, qs, ks, vs, consts


def _attn_fwd(proj, bias, mask, sink):
    t = proj.shape[0]
    nb, cur, qs, ks, vs, consts = _attn_specs(t, False)
    return pl.pallas_call(
        functools.partial(_attn_fwd_body, nb=nb), name="attn_fwd", grid=(nb,),
        in_specs=[qs] + ks + vs + consts,
        out_specs=[pl.BlockSpec((BLOCK, QP), lambda n: (n, 0)), pl.BlockSpec((BLOCK, LANE), lambda n: (n, 0))],
        out_shape=[jax.ShapeDtypeStruct((t, QP), BF16), jax.ShapeDtypeStruct((t, LANE), F32)],
        compiler_params=_params(1),
    )(proj, proj, proj, proj, proj, proj, proj, bias, mask, sink)


def _attn_bwd_body(q_ref, kp_ref, kc_ref, kn_ref, vp_ref, vc_ref, vn_ref, bias_ref, mask_ref, sink_ref, do_ref, o_ref, lse_ref,
                   dq_ref, dk_ref, dv_ref, dbias_ref, dsink_ref, ak, bk, av, bv, *, nb):
    n = pl.program_id(0)
    scale = HEAD_DIM ** -0.5

    @pl.when(n == 0)
    def _():
        for r in (ak, bk, av, bv, dbias_ref, dsink_ref):
            r[...] = jnp.zeros_like(r)

    @pl.when(n < nb)
    def _():
        edge = _attn_edge(n, nb)
        dsink = jnp.zeros((1, LANE), F32)
        for g in range(KV_HEADS):
            gc = slice(g * HEAD_PAD, (g + 1) * HEAD_PAD)
            kg = _mx(jnp.concatenate([kp_ref[:, gc], kc_ref[:, gc], kn_ref[:, gc]], axis=0))
            vg = _mx(jnp.concatenate([vp_ref[:, gc], vc_ref[:, gc], vn_ref[:, gc]], axis=0))
            dk_c = jnp.zeros((3 * BLOCK, HEAD_PAD), F32)
            dv_c = jnp.zeros((3 * BLOCK, HEAD_PAD), F32)
            for r in range(Q_HEADS // KV_HEADS):
                h = g * (Q_HEADS // KV_HEADS) + r
                hc = slice(h * HEAD_PAD, (h + 1) * HEAD_PAD)
                qh, s = _attn_logits(q_ref, kg, bias_ref, mask_ref, edge, h)
                lse = lse_ref[:, h:h + 1]
                p = jnp.exp(s - lse)
                do = do_ref[:, hc]
                dp = _dot(_mx(do), vg, ((1,), (1,)))
                delta = jnp.sum(do * o_ref[:, hc].astype(F32), axis=-1, keepdims=True)
                ds = p * (dp - delta)
                psink = jnp.exp(sink_ref[:, h:h + 1] - lse)
                dsink = dsink - jnp.sum(psink * delta, axis=0, keepdims=True) * _onehot_lane(h)
                dbias_ref[h] += ds
                dsb = _mx(ds)
                dq_ref[:, hc] = (_dot(dsb, kg) * scale).astype(dq_ref.dtype)
                dk_c = dk_c + _dot(dsb, qh, ((0,), (0,))) * scale
                dv_c = dv_c + _dot(_mx(p), _mx(do), ((0,), (0,)))
            for acc_a, acc_b, out, c in ((ak, bk, dk_ref, dk_c), (av, bv, dv_ref, dv_c)):
                out[:, gc] = (acc_a[:, gc] + c[0:BLOCK]).astype(out.dtype)
                acc_a[:, gc] = acc_b[:, gc] + c[BLOCK:2 * BLOCK]
                acc_b[:, gc] = c[2 * BLOCK:3 * BLOCK]
        dsink_ref[...] += dsink

    @pl.when(n == nb)
    def _():
        dk_ref[...] = ak[...].astype(dk_ref.dtype)
        dv_ref[...] = av[...].astype(dv_ref.dtype)


def _attn_bwd(proj, bias, mask, sink, do, o, lse):
    t = proj.shape[0]
    nb, cur, qs, ks, vs, consts = _attn_specs(t, True)
    rowq = pl.BlockSpec((BLOCK, QP), lambda n: (cur(n), 0))
    late = pl.BlockSpec((BLOCK, KP), lambda n: (jnp.maximum(n - 1, 0), 0))
    return pl.pallas_call(
        functools.partial(_attn_bwd_body, nb=nb), name="attn_bwd", grid=(nb + 1,),
        in_specs=[qs] + ks + vs + consts + [rowq, rowq, pl.BlockSpec((BLOCK, LANE), lambda n: (cur(n), 0))],
        out_specs=[rowq, late, late, pl.BlockSpec((Q_HEADS, BLOCK, 3 * BLOCK), lambda n: (0, 0, 0)),
                   pl.BlockSpec((1, LANE), lambda n: (0, 0))],
        out_shape=[jax.ShapeDtypeStruct((t, QP), BF16), jax.ShapeDtypeStruct((t, KP), BF16), jax.ShapeDtypeStruct((t, KP), BF16),
                   jax.ShapeDtypeStruct((Q_HEADS, BLOCK, 3 * BLOCK), F32), jax.ShapeDtypeStruct((1, LANE), F32)],
        scratch_shapes=[pltpu.VMEM((BLOCK, KP), F32) for _ in range(4)],
        compiler_params=_params(1),
    )(proj, proj, proj, proj, proj, proj, proj, bias, mask, sink, do, o, lse)


def _adamw_body(gp_ref, w_ref, m_ref, v_ref, g_ref, d_ref, nm_ref, nv_ref, *, nparts):
    g = gp_ref[0].astype(F32)
    for j in range(1, nparts):
        g = g + gp_ref[j].astype(F32)
    m = ADAM_B1 * m_ref[...] + (1.0 - ADAM_B1) * g
    v = ADAM_B2 * v_ref[...] + (1.0 - ADAM_B2) * (g * g)
    m_hat = m / (1.0 - ADAM_B1 ** ADAM_STEP)
    v_hat = v / (1.0 - ADAM_B2 ** ADAM_STEP)
    g_ref[...] = g
    d_ref[...] = -ADAM_LR * (m_hat / (jnp.sqrt(v_hat) + ADAM_EPS) + ADAM_WD * w_ref[...])
    nm_ref[...] = m
    nv_ref[...] = v


def _adamw(gparts, w, m, v, name, tr=256):
    p, r, c = gparts.shape
    tr = min(tr, r)
    assert r % tr == 0
    row = pl.BlockSpec((tr, c), lambda i: (i, 0))
    sd = jax.ShapeDtypeStruct((r, c), F32)
    return pl.pallas_call(
        functools.partial(_adamw_body, nparts=p), name=name, grid=(r // tr,),
        in_specs=[pl.BlockSpec((p, tr, c), lambda i: (0, i, 0)), row, row, row],
        out_specs=[row, row, row, row], out_shape=[sd, sd, sd, sd], compiler_params=_params(1),
    )(gparts, w, m, v)


MESH = pl.DeviceIdType.MESH
N_REL = N_DEV - 1


def _allgather_body(*refs, n):
    ins, outs = refs[:n], refs[n:2 * n]
    send_sems, recv_sems, local_sems = refs[2 * n:]
    x, y, c = lax.axis_index("x"), lax.axis_index("y"), lax.axis_index("c")
    me, sibling = (x, y, c), (x, y, 1 - c)
    chips = [(1 - x, y), (x, 1 - y), (1 - x, 1 - y)]

    def slot(a, p):
        return outs[a].at[4 * p[0] + 2 * p[1] + p[2]]

    def copy(a, k, block, to, src=None):
        return pltpu.make_async_remote_copy(
            src_ref=slot(a, block) if src is None else src, dst_ref=slot(a, block),
            send_sem=send_sems.at[a * N_REL + k], recv_sem=recv_sems.at[a * N_REL + k], device_id=to, device_id_type=MESH)

    mine = [pltpu.make_async_copy(ins[a], slot(a, me), local_sems.at[a]) for a in range(n)]
    for cp in mine:
        cp.start()
    first = []
    for a in range(n):
        first.append(copy(a, 0, me, sibling, src=ins[a]))
        first += [copy(a, 1 + j, me, (*chip, c), src=ins[a]) for j, chip in enumerate(chips)]
    for cp in first:
        cp.start()
    passed = []
    for j, chip in enumerate(chips):
        for a in range(n):
            copy(a, 1 + j, (*chip, c), me).wait_recv()
            fw = copy(a, 4 + j, (*chip, c), sibling)
            fw.start()
            passed.append(fw)
    for a in range(n):
        copy(a, 0, sibling, me).wait_recv()
        for j, chip in enumerate(chips):
            copy(a, 4 + j, (*chip, 1 - c), me).wait_recv()
    for cp in first + passed:
        cp.wait_send()
    for cp in mine:
        cp.wait()


def _allgather(arrs, name, vmem=False):
    n = len(arrs)
    space = pltpu.VMEM if vmem else pl.ANY
    spec = pl.BlockSpec(memory_space=space)
    return pl.pallas_call(
        functools.partial(_allgather_body, n=n), name=name,
        in_specs=[spec] * n, out_specs=[spec] * n,
        out_shape=[jax.ShapeDtypeStruct((N_DEV,) + a.shape, a.dtype) for a in arrs],
        scratch_shapes=[pltpu.SemaphoreType.DMA((n * N_REL,)), pltpu.SemaphoreType.DMA((n * N_REL,)),
                        pltpu.SemaphoreType.DMA((n,))],
        compiler_params=pltpu.CompilerParams(vmem_limit_bytes=VMEM_LIMIT),
    )(*arrs)


def _alltoall_body(*refs, n):
    ins, outs = refs[:n], refs[n:2 * n]
    send_sems, recv_sems, local_sems = refs[2 * n:]
    x, y, c = lax.axis_index("x"), lax.axis_index("y"), lax.axis_index("c")
    me = 4 * x + 2 * y + c
    mine = [pltpu.make_async_copy(ins[a].at[me], outs[a].at[me], local_sems.at[a]) for a in range(n)]
    for cp in mine:
        cp.start()
    copies = []
    for a in range(n):
        for k in range(1, N_DEV):
            px = 1 - x if k & 4 else x
            py = 1 - y if k & 2 else y
            pc = 1 - c if k & 1 else c
            peer = 4 * px + 2 * py + pc
            send = pltpu.make_async_remote_copy(
                src_ref=ins[a].at[peer], dst_ref=outs[a].at[me], send_sem=send_sems.at[a * N_REL + k - 1],
                recv_sem=recv_sems.at[a * N_REL + k - 1], device_id=(px, py, pc), device_id_type=MESH)
            recv = pltpu.make_async_remote_copy(
                src_ref=ins[a].at[me], dst_ref=outs[a].at[peer], send_sem=send_sems.at[a * N_REL + k - 1],
                recv_sem=recv_sems.at[a * N_REL + k - 1], device_id=(px, py, pc), device_id_type=MESH)
            send.start()
            copies.append((send, recv))
    for send, recv in copies:
        send.wait_send()
        recv.wait_recv()
    for cp in mine:
        cp.wait()


def _alltoall(arrs, name):
    n = len(arrs)
    spec = pl.BlockSpec(memory_space=pl.ANY)
    return pl.pallas_call(
        functools.partial(_alltoall_body, n=n), name=name,
        in_specs=[spec] * n, out_specs=[spec] * n,
        out_shape=[jax.ShapeDtypeStruct(a.shape, a.dtype) for a in arrs],
        scratch_shapes=[pltpu.SemaphoreType.DMA((n * N_REL,)), pltpu.SemaphoreType.DMA((n * N_REL,)),
                        pltpu.SemaphoreType.DMA((n,))],
    )(*arrs)


def _pad_heads(w, axis):
    shp = w.shape
    heads = shp[axis] // HEAD_DIM
    w = w.reshape(shp[:axis] + (heads, HEAD_DIM) + shp[axis + 1:])
    pad = [(0, 0)] * w.ndim
    pad[axis + 1] = (0, HEAD_PAD - HEAD_DIM)
    w = jnp.pad(w, pad)
    return w.reshape(shp[:axis] + (heads * HEAD_PAD,) + shp[axis + 1:])


def _unpad_heads(w, axis):
    shp = w.shape
    heads = shp[axis] // HEAD_PAD
    w = w.reshape(shp[:axis] + (heads, HEAD_PAD) + shp[axis + 1:])
    w = lax.slice_in_dim(w, 0, HEAD_DIM, axis=axis + 1)
    return w.reshape(shp[:axis] + (heads * HEAD_DIM,) + shp[axis + 1:])


def _w_in_to_padded(w):
    idx = np.cumsum((0,) + IN_SPLITS)
    z, xbc, dt, q, k, v, gates = [w[:, idx[i]:idx[i + 1]] for i in range(7)]
    zeros = lambda n: jnp.zeros((w.shape[0], n), w.dtype)
    return jnp.concatenate([z, gates, _pad_heads(q, 1), xbc, _pad_heads(k, 1), _pad_heads(v, 1), dt, zeros(LANE - 2 * SSD_HEADS),
                            zeros(NP - DT0 - LANE)], axis=1)


def _w_in_from_padded(w):
    z = w[:, Z0:Z0 + D_INNER]
    gates = w[:, GT0:GT0 + 2 * D_MODEL]
    q = _unpad_heads(w[:, Q0:Q0 + QP], 1)
    xbc = w[:, XBC0:XBC0 + CONV_DIM]
    k = _unpad_heads(w[:, K0:K0 + KP], 1)
    v = _unpad_heads(w[:, V0:V0 + KP], 1)
    dt = w[:, DT0:DT0 + 2 * SSD_HEADS]
    return jnp.concatenate([z, xbc, dt, q, k, v, gates], axis=1)


def _pad_lane(v):
    v = v.reshape(-1)
    return jnp.pad(v, (0, LANE - v.shape[0]))


def _layer_consts(p):
    c = dict(p)
    c["conv_w8"] = jnp.pad(p["conv_w"].reshape(SSD_CONV, CONV_DIM), ((0, 8 - SSD_CONV), (0, 0)))
    c["conv_b1"] = p["conv_b"].reshape(1, CONV_DIM)
    c["dtb"] = _pad_lane(p["dt_bias"])
    c["alog"] = _pad_lane(p["a_log"])
    c["dskip_x"] = jnp.repeat(p["d_skip"], SSD_HEAD_DIM).reshape(1, D_INNER)
    c["sink"] = _pad_lane(p["attn_sink"]).reshape(1, LANE)
    return c


def _layer_fwd(x, c, w, bias, mask, l):
    s = {"x": x}
    s["h"] = _rms_fwd(x, c["pre_mix_norm"], f"pre_mix_norm_{l}", out_dtype=BF16)
    proj = _mm(s["h"], w["w_in"], f"in_proj_{l}")
    s["proj"] = proj
    s["xbc"] = _conv_fwd(proj, c["conv_w8"], c["conv_b1"])
    s["dtt"] = proj[:, DT0:DT0 + LANE].T
    s["yf"], s["stf"] = _ssd_fwd(s["xbc"], proj, s["dtt"], c["dtb"], c["alog"], False)
    s["yb"], s["stb"] = _ssd_fwd(s["xbc"], proj, s["dtt"], c["dtb"], c["alog"], True)
    s["u"] = _gnorm_fwd(s["yf"], s["yb"], s["xbc"], proj, c["dskip_x"], c["ssd_norm"])
    s["y_ssd"] = _mm(s["u"], w["w_ssd_out"], f"ssd_out_{l}")
    s["o"], s["lse"] = _attn_fwd(proj, bias, mask, c["sink"])
    s["y_attn"] = _mm(s["o"], w["w_attn_out"], f"attn_out_{l}")
    s["mix"] = _gate_fwd(proj, c["b_gate"], s["y_ssd"], s["y_attn"])
    s["mixed"] = _mm(s["mix"], w["w_o"], f"w_o_{l}")
    s["x1"] = _rms_fwd(s["mixed"], c["post_mix_norm"], f"post_mix_norm_{l}", res=x)
    s["h2"] = _rms_fwd(s["x1"], c["pre_mlp_norm"], f"pre_mlp_norm_{l}", out_dtype=BF16)
    s["f1"] = _mm(s["h2"], w["w_mlp_in"], f"mlp_in_{l}")
    s["a"] = _relu2(s["f1"], f"relu2_{l}")
    s["f"] = _mm(s["a"], w["w_mlp_out"], f"mlp_out_{l}")
    x2 = _rms_fwd(s["f"], c["post_mlp_norm"], f"post_mlp_norm_{l}", res=s["x1"])
    return x2, s


def _layer_bwd(dx2, s, c, w, bias, mask, l):
    gw, gs = {}, {}
    df, gs["post_mlp_norm"] = _rms_bwd(s["f"], c["post_mlp_norm"], dx2, f"post_mlp_norm_bwd_{l}", out_dtype=BF16)
    da = _mm(df, w["w_mlp_out"], f"mlp_out_dx_{l}", tb=True)
    gw["w_mlp_out"] = _mm(s["a"], df, f"mlp_out_dw_{l}", ta=True)
    df1 = _relu2_bwd(da, s["f1"], f"relu2_bwd_{l}")
    gw["w_mlp_in"] = _mm(s["h2"], df1, f"mlp_in_dw_{l}", ta=True)
    dh2 = _mm(df1, w["w_mlp_in"], f"mlp_in_dx_{l}", tb=True)
    dx1, gs["pre_mlp_norm"] = _rms_bwd(s["x1"], c["pre_mlp_norm"], dh2, f"pre_mlp_norm_bwd_{l}", add=dx2)
    dmixed, gs["post_mix_norm"] = _rms_bwd(s["mixed"], c["post_mix_norm"], dx1, f"post_mix_norm_bwd_{l}", out_dtype=BF16)
    gw["w_o"] = _mm(s["mix"], dmixed, f"w_o_dw_{l}", ta=True)
    dmix = _mm(dmixed, w["w_o"], f"w_o_dx_{l}", tb=True)
    dys, dya, dgt, gs["b_gate"] = _gate_bwd(dmix, s["proj"], c["b_gate"], s["y_ssd"], s["y_attn"])
    gw["w_ssd_out"] = _mm(s["u"], dys, f"ssd_out_dw_{l}", ta=True)
    du = _mm(dys, w["w_ssd_out"], f"ssd_out_dx_{l}", tb=True)
    gw["w_attn_out"] = _mm(s["o"], dya, f"attn_out_dw_{l}", ta=True)
    do = _mm(dya, w["w_attn_out"], f"attn_out_dx_{l}", tb=True)
    dq, dk, dv, dbias, dsink = _attn_bwd(s["proj"], bias, mask, c["sink"], do, s["o"], s["lse"])
    gs["attn_sink"] = dsink
    dy, dz, gs["ssd_norm"], gs["d_skip"] = _gnorm_bwd(du, s["yf"], s["yb"], s["xbc"], s["proj"], c["dskip_x"], c["ssd_norm"])
    part = _ssd_bwd(s["xbc"], s["proj"], s["dtt"], c["dtb"], c["alog"], dy, s["stf"], c["dskip_x"], False)
    dxbc, ddt, sm = _ssd_bwd(s["xbc"], s["proj"], s["dtt"], c["dtb"], c["alog"], dy, s["stb"], c["dskip_x"], True, prev=part)
    gs["a_log"], gs["dt_bias"] = sm[0:1], sm[1:2]
    dxbc_raw, gs["conv"] = _conv_bwd(s["proj"], dxbc, c["conv_w8"], c["conv_b1"])
    t = dx2.shape[0]
    dproj = jnp.concatenate([dz, dgt, dq, dxbc_raw, dk, dv, ddt.astype(BF16), jnp.zeros((t, NP - DT0 - LANE), BF16)], axis=1)
    gw["w_in"] = _mm(s["h"], dproj, f"in_proj_dw_{l}", ta=True)
    dh = _mm(dproj, w["w_in"], f"in_proj_dx_{l}", tb=True)
    dx, gs["pre_mix_norm"] = _rms_bwd(s["x"], c["pre_mix_norm"], dh, f"pre_mix_norm_bwd_{l}", add=dx1)
    return dx, gw, gs, dbias


SMALL = (("pre_mix_norm", DEPTH * D_MODEL), ("b_gate", DEPTH * 2 * D_MODEL), ("conv_w", DEPTH * SSD_CONV * CONV_DIM),
         ("conv_b", DEPTH * CONV_DIM), ("dt_bias", DEPTH * 2 * SSD_HEADS), ("a_log", DEPTH * 2 * SSD_HEADS),
         ("d_skip", DEPTH * SSD_HEADS), ("ssd_norm", DEPTH * D_INNER), ("attn_sink", DEPTH * Q_HEADS),
         ("rel_bias_table", N_BUCKETS * Q_HEADS), ("post_mix_norm", DEPTH * D_MODEL), ("pre_mlp_norm", DEPTH * D_MODEL),
         ("post_mlp_norm", DEPTH * D_MODEL))


def _small_rows(n):
    return -(-n // LANE)


def _pack_small(vals, fill=0.0):
    rows = []
    for name, n in SMALL:
        v = vals[name].reshape(-1).astype(F32)
        rows.append(jnp.pad(v, (0, _small_rows(n) * LANE - n), constant_values=fill).reshape(-1, LANE))
    out = jnp.concatenate(rows, axis=0)
    return jnp.pad(out, ((0, -out.shape[0] % 8), (0, 0)), constant_values=fill)


def _unpack_small(packed, shapes):
    out, r = {}, 0
    for name, n in SMALL:
        nr = _small_rows(n)
        out[name] = packed[r:r + nr].reshape(-1)[:n].reshape(shapes[name])
        r += nr
    return out


BIG = ("w_in", "w_ssd_out", "w_attn_out", "w_o", "w_mlp_in", "w_mlp_out")
ROWS = ("w_ssd_out", "w_attn_out", "w_o", "w_mlp_out")


def _pack_rows(vals, lead):
    return jnp.concatenate([vals[n].reshape(lead + (-1, D_MODEL)) for n in ROWS], axis=len(lead))


def _unpack_rows(packed, shapes):
    out, r = {}, 0
    for n in ROWS:
        nr = shapes[n][0] * shapes[n][1]
        out[n] = packed[..., r:r + nr, :]
        r += nr
    return out


def kernel(x, pre_mix_norm, w_in, b_gate, conv_w, conv_b, dt_bias, a_log, d_skip, ssd_norm, w_ssd_out, attn_sink, rel_bias_table, w_attn_out, w_o, post_mix_norm, pre_mlp_norm, w_mlp_in, w_mlp_out, post_mlp_norm, loss_target, m_pre_mix_norm, m_w_in, m_b_gate, m_conv_w, m_conv_b, m_dt_bias, m_a_log, m_d_skip, m_ssd_norm, m_w_ssd_out, m_attn_sink, m_rel_bias_table, m_w_attn_out, m_w_o, m_post_mix_norm, m_pre_mlp_norm, m_w_mlp_in, m_w_mlp_out, m_post_mlp_norm, v_pre_mix_norm, v_w_in, v_b_gate, v_conv_w, v_conv_b, v_dt_bias, v_a_log, v_d_skip, v_ssd_norm, v_w_ssd_out, v_attn_sink, v_rel_bias_table, v_w_attn_out, v_w_o, v_post_mix_norm, v_pre_mlp_norm, v_w_mlp_in, v_w_mlp_out, v_post_mlp_norm):
    names = ("pre_mix_norm", "w_in", "b_gate", "conv_w", "conv_b", "dt_bias", "a_log", "d_skip", "ssd_norm", "w_ssd_out",
             "attn_sink", "rel_bias_table", "w_attn_out", "w_o", "post_mix_norm", "pre_mlp_norm", "w_mlp_in", "w_mlp_out",
             "post_mlp_norm")
    W = dict(zip(names, (pre_mix_norm, w_in, b_gate, conv_w, conv_b, dt_bias, a_log, d_skip, ssd_norm, w_ssd_out, attn_sink,
                         rel_bias_table, w_attn_out, w_o, post_mix_norm, pre_mlp_norm, w_mlp_in, w_mlp_out, post_mlp_norm)))
    M = dict(zip(names, (m_pre_mix_norm, m_w_in, m_b_gate, m_conv_w, m_conv_b, m_dt_bias, m_a_log, m_d_skip, m_ssd_norm,
                         m_w_ssd_out, m_attn_sink, m_rel_bias_table, m_w_attn_out, m_w_o, m_post_mix_norm, m_pre_mlp_norm,
                         m_w_mlp_in, m_w_mlp_out, m_post_mlp_norm)))
    V = dict(zip(names, (v_pre_mix_norm, v_w_in, v_b_gate, v_conv_w, v_conv_b, v_dt_bias, v_a_log, v_d_skip, v_ssd_norm,
                         v_w_ssd_out, v_attn_sink, v_rel_bias_table, v_w_attn_out, v_w_o, v_post_mix_norm, v_pre_mlp_norm,
                         v_w_mlp_in, v_w_mlp_out, v_post_mlp_norm)))
    t = x.shape[1]
    shard = {n: W[n].shape for n in names}
    dev = 4 * lax.axis_index("x") + 2 * lax.axis_index("y") + lax.axis_index("c")
    cshard = CONV_DIM // N_DEV

    send_rows = _pack_rows({n: W[n].astype(BF16) for n in ROWS}, ())
    g_in, g_mi, g_rows = _allgather(
        [W["w_in"].astype(BF16).reshape(DEPTH * D_MODEL, -1), W["w_mlp_in"].astype(BF16).reshape(DEPTH * D_MODEL, -1), send_rows],
        "gather_weights")
    (g_conv,) = _allgather([conv_w.reshape(-1, LANE)], "gather_conv_w", vmem=True)
    conv_full = g_conv.reshape(N_DEV, DEPTH, SSD_CONV, cshard).transpose(1, 2, 0, 3).reshape(DEPTH, SSD_CONV, CONV_DIM)
    g_in = g_in.reshape(N_DEV, DEPTH, D_MODEL, -1)
    g_mi = g_mi.reshape(N_DEV, DEPTH, D_MODEL, -1)
    rows = _unpack_rows(g_rows, shard)
    full = []
    for l in range(DEPTH):
        wl = {"w_in": _w_in_to_padded(g_in[:, l].transpose(1, 0, 2).reshape(D_MODEL, N_IN)),
              "w_mlp_in": g_mi[:, l].transpose(1, 0, 2).reshape(D_MODEL, D_FF)}
        for n in ROWS:
            per = shard[n][1]
            wl[n] = rows[n][:, l * per:(l + 1) * per].reshape(N_DEV * per, D_MODEL)
        wl["w_attn_out"] = _pad_heads(wl["w_attn_out"], 0)
        full.append(wl)

    _, band = _bucket_np()
    onehot = jnp.asarray(_onehot_np(), BF16)
    table_t = jnp.pad(rel_bias_table.T, ((0, 0), (0, LANE - N_BUCKETS)))
    bias = _mm(table_t, onehot, "t5_bias", tb=True, tm=Q_HEADS, tn=3 * BLOCK * BLOCK // 8, split_a=True)
    bias = bias.reshape(Q_HEADS, BLOCK, 3 * BLOCK)
    mask = jnp.asarray(np.where(band, 0.0, NEG), F32)

    consts = []
    for l in range(DEPTH):
        p = {n: W[n][l] for n in names if n not in BIG and n not in ("rel_bias_table", "conv_w")}
        p["conv_w"] = conv_full[l]
        consts.append(_layer_consts(p))
    h = x[0]
    saved = []
    for l in range(DEPTH):
        h, s = _layer_fwd(h, consts[l], full[l], bias, mask, l)
        saved.append(s)
    dy, lsum = _loss(h, loss_target[0])
    loss = lax.psum(0.5 / D_MODEL * jnp.sum(lsum), ("x", "y", "c"))

    gws, gss, dbs = [None] * DEPTH, [None] * DEPTH, [None] * DEPTH
    for l in reversed(range(DEPTH)):
        dy, gws[l], gss[l], dbs[l] = _layer_bwd(dy, saved[l], consts[l], full[l], bias, mask, l)
    grad_x = dy[None]
    dbias = jnp.concatenate([d.reshape(Q_HEADS, -1) for d in dbs], axis=1)
    d_table = _mm(dbias, jnp.concatenate([onehot] * DEPTH, axis=0), "t5_bias_bwd", tm=Q_HEADS, tk=3 * BLOCK * BLOCK // 8,
                  split_a=True)

    sg = {}
    for n in ("pre_mix_norm", "b_gate", "ssd_norm", "post_mix_norm", "pre_mlp_norm", "post_mlp_norm"):
        sg[n] = jnp.stack([gss[l][n].reshape(-1) for l in range(DEPTH)])
    sg["conv_w"] = jnp.stack([gss[l]["conv"][0:SSD_CONV] for l in range(DEPTH)])
    sg["conv_b"] = jnp.stack([gss[l]["conv"][SSD_CONV] for l in range(DEPTH)])
    sg["dt_bias"] = jnp.stack([gss[l]["dt_bias"][0, 0:2 * SSD_HEADS] for l in range(DEPTH)])
    sg["a_log"] = jnp.stack([gss[l]["a_log"][0, 0:2 * SSD_HEADS] for l in range(DEPTH)])
    sg["d_skip"] = jnp.stack([gss[l]["d_skip"][0, 0:SSD_HEADS] for l in range(DEPTH)])
    sg["attn_sink"] = jnp.stack([gss[l]["attn_sink"][0, 0:Q_HEADS] for l in range(DEPTH)])
    sg["rel_bias_table"] = d_table[:, 0:N_BUCKETS].T
    (small_parts,) = _allgather([_pack_small(sg)], "gather_small_grads", vmem=True)
    sshape = {n: W[n].shape for n, _ in SMALL}
    sshape["conv_w"] = (DEPTH, SSD_CONV, 1, CONV_DIM)
    pk = lambda d, fill: _pack_small({n: (jnp.full((DEPTH, SSD_CONV, CONV_DIM), fill, F32) if n == "conv_w" else d[n])
                                      for n, _ in SMALL}, fill)
    s_out = [_unpack_small(o, sshape) for o in _adamw(small_parts, pk(W, 1.0), pk(M, 1.0), pk(V, 1.0), "adamw_small", tr=1024)]
    conv_g = lax.dynamic_slice_in_dim(s_out[0]["conv_w"], dev * cshard, cshard, axis=3)
    c_out = _adamw(conv_g.reshape(1, DEPTH * SSD_CONV, cshard), conv_w.reshape(-1, cshard), m_conv_w.reshape(-1, cshard),
                   v_conv_w.reshape(-1, cshard), "adamw_conv_w", tr=DEPTH * SSD_CONV)
    for i in range(4):
        s_out[i]["conv_w"] = c_out[i].reshape(conv_w.shape)

    per = {n: (DEPTH,) + shard[n][1:] for n in BIG}
    send = {}
    gi = jnp.stack([_w_in_from_padded(gws[l]["w_in"]) for l in range(DEPTH)])
    send["w_in"] = gi.reshape(DEPTH, D_MODEL, N_DEV, -1).transpose(2, 0, 1, 3).astype(BF16)
    gm = jnp.stack([gws[l]["w_mlp_in"] for l in range(DEPTH)])
    send["w_mlp_in"] = gm.reshape(DEPTH, D_MODEL, N_DEV, -1).transpose(2, 0, 1, 3).astype(BF16)
    for n in ROWS:
        g = jnp.stack([_unpad_heads(gws[l][n], 0) if n == "w_attn_out" else gws[l][n] for l in range(DEPTH)])
        send[n] = g.reshape(DEPTH, N_DEV, -1, D_MODEL).transpose(1, 0, 2, 3).astype(BF16)
    r_in, r_mi, r_rows = _alltoall(
        [send["w_in"].reshape(N_DEV, DEPTH * D_MODEL, -1), send["w_mlp_in"].reshape(N_DEV, DEPTH * D_MODEL, -1),
         _pack_rows(send, (N_DEV,))], "exchange_grads")
    b_out = {}
    for n, parts in (("w_in", r_in), ("w_mlp_in", r_mi)):
        o = _adamw(parts, W[n].reshape(parts.shape[1:]), M[n].reshape(parts.shape[1:]), V[n].reshape(parts.shape[1:]), f"adamw_{n}")
        b_out[n] = [a.reshape(shard[n]) for a in o]
    o = _adamw(r_rows, _pack_rows(W, ()), _pack_rows(M, ()), _pack_rows(V, ()), "adamw_rows")
    un = [_unpack_rows(a, shard) for a in o]
    for n in ROWS:
        b_out[n] = [un[i][n].reshape(shard[n]) for i in range(4)]

    outs = [loss, grad_x]
    for i in range(4):
        for n in names:
            outs.append(b_out[n][i] if n in BIG else s_out[i][n])
    return tuple(outs)
```

```python
import functools
import math

import numpy as np
import jax
import jax.numpy as jnp
from jax import lax
from jax.experimental import pallas as pl
from jax.experimental.pallas import tpu as pltpu

F32 = jnp.float32
BF16 = jnp.bfloat16

D_MODEL = 1024
DEPTH = 2
D_INNER = 2048
SSD_HEADS = 32
SSD_HEAD_DIM = 64
SSD_GROUPS = 8
SSD_REP = 4
SSD_STATE = 128
SSD_CONV = 5
CHUNK = 128
CONV_DIM = 4096
Q_HEADS = 16
KV_HEADS = 4
HEAD_DIM = 64
ATTN_WIDTH = 1024
KV_WIDTH = 256
WINDOW = 128
BLOCK = 128
N_BUCKETS = 32
MAX_DISTANCE = 128
D_FF = 4096
EPS = 1e-6
IN_SPLITS = (D_INNER, CONV_DIM, 2 * SSD_HEADS, ATTN_WIDTH, KV_WIDTH, KV_WIDTH, 2 * D_MODEL)
N_IN = sum(IN_SPLITS)
N_DEV = 8

ADAM_LR = 0.001
ADAM_B1 = 0.9
ADAM_B2 = 0.999
ADAM_EPS = 1e-08
ADAM_WD = 0.01
ADAM_STEP = 10

LANE = 128
HEAD_PAD = LANE
Z0 = 0
GT0 = 2048
Q0 = 4096
XBC0 = 6144
XS0 = 6144
B0 = 8192
C0 = 9216
K0 = 10240
V0 = 10752
DT0 = 11264
NP = 11520
QP = Q_HEADS * HEAD_PAD
KP = KV_HEADS * HEAD_PAD
NEG = -1e30
VMEM_LIMIT = 56 * 1024 * 1024


def _params(n_grid):
    return pltpu.CompilerParams(dimension_semantics=("arbitrary",) * n_grid, vmem_limit_bytes=VMEM_LIMIT)


def _dot(a, b, dims=((1,), (0,))):
    return lax.dot_general(a, b, (dims, ((), ())), preferred_element_type=F32)


def _mx(a):
    return a.astype(BF16)


def _split_dot(x, e):
    hi = x.astype(BF16)
    lo = (x - hi.astype(F32)).astype(BF16)
    return _dot(hi, e) + _dot(lo, e)


def _softplus(x):
    u = jnp.exp(-jnp.abs(x))
    w = 1.0 + u
    log1p = jnp.where(w == 1.0, u, jnp.log(w) * u / jnp.where(w == 1.0, 1.0, w - 1.0))
    return jnp.maximum(x, 0.0) + log1p


def _sigmoid(x):
    return 1.0 / (1.0 + jnp.exp(-x))


def _mm_body(*refs, grid, ta, tb, split_a, mode, comm):
    n_in = 3 if mode == "relu2_bwd" else 2
    n_out = 2 if mode == "relu2" else 1
    ins, cin, outs, cout, (acc_ref,), csem = _split_refs(refs, n_in, n_out, comm)
    a_ref, b_ref = ins[0], ins[1]
    i, j, k = pl.program_id(0), pl.program_id(1), pl.program_id(2)
    nk = grid[2]
    if comm is not None:
        comm.run(cin, cout, csem, (i * grid[1] + j) * nk + k, grid[0] * grid[1] * nk)

    @pl.when(k == 0)
    def _():
        acc_ref[...] = jnp.zeros_like(acc_ref)

    dims = ((0 if ta else 1,), (1 if tb else 0,))
    b = _mx(b_ref[...])
    if split_a:
        a32 = a_ref[...].astype(F32)
        hi = a32.astype(BF16)
        lo = (a32 - hi.astype(F32)).astype(BF16)
        acc_ref[...] += _dot(hi, b, dims) + _dot(lo, b, dims)
    else:
        acc_ref[...] += _dot(_mx(a_ref[...]), b, dims)

    @pl.when(k == nk - 1)
    def _():
        acc = acc_ref[...]
        if mode == "relu2":
            outs[0][...] = acc
            r = jnp.maximum(acc, 0.0)
            outs[1][...] = (r * r).astype(outs[1].dtype)
        elif mode == "relu2_bwd":
            outs[0][...] = (acc * 2.0 * jnp.maximum(ins[2][...], 0.0)).astype(outs[0].dtype)
        else:
            outs[0][...] = acc.astype(outs[0].dtype)


def _tile(n, pref):
    t = min(n, pref)
    while n % t:
        t -= LANE
    assert t > 0 and n % t == 0, (n, pref)
    return t


def _mm(a, b, name, ta=False, tb=False, out_dtype=F32, tm=1024, tn=1280, tk=1024, split_a=False, mode=None, extra=None,
        comm=None):
    m, k = (a.shape[1], a.shape[0]) if ta else a.shape
    n = b.shape[0] if tb else b.shape[1]
    assert (b.shape[1] if tb else b.shape[0]) == k
    tm, tn, tk = _tile(m, tm), _tile(n, tn), _tile(k, tk)
    grid = (m // tm, n // tn, k // tk)
    a_spec = pl.BlockSpec((tk, tm), lambda i, j, kk: (kk, i)) if ta else pl.BlockSpec((tm, tk), lambda i, j, kk: (i, kk))
    b_spec = pl.BlockSpec((tn, tk), lambda i, j, kk: (j, kk)) if tb else pl.BlockSpec((tk, tn), lambda i, j, kk: (kk, j))
    o_spec = pl.BlockSpec((tm, tn), lambda i, j, kk: (i, j))
    in_specs, args = [a_spec, b_spec], [a, b]
    out_specs, out_shape = [o_spec], [jax.ShapeDtypeStruct((m, n), out_dtype)]
    if mode == "relu2":
        out_specs, out_shape = [o_spec, o_spec], [jax.ShapeDtypeStruct((m, n), F32), jax.ShapeDtypeStruct((m, n), BF16)]
    elif mode == "relu2_bwd":
        in_specs, args = in_specs + [o_spec], args + [extra]
    scratch = [pltpu.VMEM((tm, tn), F32)]
    if comm is not None:
        in_specs, args = in_specs + [ANY] * comm.n, args + comm.arrs
        out_specs, out_shape = out_specs + [ANY] * comm.n, out_shape + comm.out_shapes()
        scratch = scratch + comm.sems()
    res = pl.pallas_call(
        functools.partial(_mm_body, grid=grid, ta=ta, tb=tb, split_a=split_a, mode=mode, comm=comm),
        name=name, grid=grid, in_specs=in_specs, out_specs=out_specs, out_shape=out_shape, scratch_shapes=scratch,
        compiler_params=_params(3),
    )(*args)
    return res[0] if len(res) == 1 else res


def _rms_fwd_body(*refs, has_res):
    if has_res:
        x_ref, g_ref, r_ref, o_ref = refs
    else:
        x_ref, g_ref, o_ref = refs
    x = x_ref[...]
    y = x * lax.rsqrt(jnp.mean(x * x, axis=-1, keepdims=True) + EPS) * g_ref[...]
    if has_res:
        y = r_ref[...] + y
    o_ref[...] = y.astype(o_ref.dtype)


def _rms_fwd(x, g, name, res=None, out_dtype=F32, tr=512):
    t, d = x.shape
    tr = min(tr, t)
    row = pl.BlockSpec((tr, d), lambda i: (i, 0))
    vec = pl.BlockSpec((1, d), lambda i: (0, 0))
    args = [x, g.reshape(1, d)] + ([res] if res is not None else [])
    return pl.pallas_call(
        functools.partial(_rms_fwd_body, has_res=res is not None),
        name=name,
        grid=(t // tr,),
        in_specs=[row, vec] + ([row] if res is not None else []),
        out_specs=row,
        out_shape=jax.ShapeDtypeStruct((t, d), out_dtype),
        compiler_params=_params(1),
    )(*args)


def _rms_bwd_body(*refs, has_add):
    if has_add:
        x_ref, g_ref, dy_ref, add_ref, dx_ref, dg_ref = refs
    else:
        x_ref, g_ref, dy_ref, dx_ref, dg_ref = refs

    @pl.when(pl.program_id(0) == 0)
    def _():
        dg_ref[...] = jnp.zeros_like(dg_ref)

    x = x_ref[...]
    dy = dy_ref[...].astype(F32)
    rstd = lax.rsqrt(jnp.mean(x * x, axis=-1, keepdims=True) + EPS)
    n = x * rstd
    dn = dy * g_ref[...]
    dx = rstd * (dn - n * jnp.mean(dn * n, axis=-1, keepdims=True))
    if has_add:
        dx = dx + add_ref[...]
    dx_ref[...] = dx.astype(dx_ref.dtype)
    dg_ref[...] += jnp.sum(dy * n, axis=0, keepdims=True)


def _rms_bwd(x, g, dy, name, add=None, out_dtype=F32, tr=512):
    t, d = x.shape
    tr = min(tr, t)
    row = pl.BlockSpec((tr, d), lambda i: (i, 0))
    vec = pl.BlockSpec((1, d), lambda i: (0, 0))
    args = [x, g.reshape(1, d), dy] + ([add] if add is not None else [])
    return pl.pallas_call(
        functools.partial(_rms_bwd_body, has_add=add is not None),
        name=name,
        grid=(t // tr,),
        in_specs=[row, vec, row] + ([row] if add is not None else []),
        out_specs=[row, vec],
        out_shape=[jax.ShapeDtypeStruct((t, d), out_dtype), jax.ShapeDtypeStruct((1, d), F32)],
        compiler_params=_params(1),
    )(*args)


def _loss_body(y_ref, t_ref, dy_ref, l_ref):
    @pl.when(pl.program_id(0) == 0)
    def _():
        l_ref[...] = jnp.zeros_like(l_ref)

    e = y_ref[...] - t_ref[...]
    dy_ref[...] = e * (1.0 / D_MODEL)
    s = jnp.sum(e * e, axis=0, keepdims=True)
    acc = s[:, 0:LANE]
    for j in range(1, D_MODEL // LANE):
        acc = acc + s[:, j * LANE:(j + 1) * LANE]
    l_ref[...] += acc


def _loss(y, target, tr=512):
    t, d = y.shape
    tr = min(tr, t)
    row = pl.BlockSpec((tr, d), lambda i: (i, 0))
    return pl.pallas_call(
        functools.partial(_loss_body), name="loss_head", grid=(t // tr,), in_specs=[row, row],
        out_specs=[row, pl.BlockSpec((1, LANE), lambda i: (0, 0))],
        out_shape=[jax.ShapeDtypeStruct((t, d), F32), jax.ShapeDtypeStruct((1, LANE), F32)],
        compiler_params=_params(1),
    )(y, target)


CONV_CT = 128
CONV_RB = 512
CONV_PAD = 8


def _conv_rows(t):
    return min(CONV_RB, t)


def _conv_fwd_body(u_ref, w_ref, b_ref, o_ref, pad_ref, *, t):
    rb = _conv_rows(t)
    zeros = jnp.zeros((CONV_PAD, CONV_CT), F32)
    pad_ref[pl.ds(0, CONV_PAD), :] = zeros
    pad_ref[pl.ds(t + CONV_PAD, CONV_PAD), :] = zeros
    pad_ref[pl.ds(CONV_PAD, t), :] = u_ref[...]
    w = w_ref[...]
    b = b_ref[...]
    for c in range(t // rb):
        base = CONV_PAD + c * rb
        acc = b + w[0:1, :] * pad_ref[pl.ds(base - 2, rb), :]
        for k in range(1, SSD_CONV):
            acc = acc + w[k:k + 1, :] * pad_ref[pl.ds(base + k - 2, rb), :]
        o_ref[pl.ds(c * rb, rb), :] = acc * _sigmoid(acc)


def _conv_fwd(proj, conv_w8, conv_b):
    t = proj.shape[0]
    off = XBC0 // CONV_CT
    return pl.pallas_call(
        functools.partial(_conv_fwd_body, t=t), name="conv_fwd", grid=(CONV_DIM // CONV_CT,),
        in_specs=[pl.BlockSpec((t, CONV_CT), lambda i: (0, off + i)), pl.BlockSpec((8, CONV_CT), lambda i: (0, i)),
                  pl.BlockSpec((1, CONV_CT), lambda i: (0, i))],
        out_specs=pl.BlockSpec((t, CONV_CT), lambda i: (0, i)),
        out_shape=jax.ShapeDtypeStruct((t, CONV_DIM), F32),
        scratch_shapes=[pltpu.VMEM((t + 2 * CONV_PAD, CONV_CT), F32)],
        compiler_params=_params(1),
    )(proj, conv_w8, conv_b)


def _conv_bwd_body(u_ref, da_ref, w_ref, b_ref, du_ref, dw_ref, pad_ref, pad2_ref, *, t):
    rb = _conv_rows(t)
    zeros = jnp.zeros((CONV_PAD, CONV_CT), F32)
    for p in (pad_ref, pad2_ref):
        p[pl.ds(0, CONV_PAD), :] = zeros
        p[pl.ds(t + CONV_PAD, CONV_PAD), :] = zeros
    pad_ref[pl.ds(CONV_PAD, t), :] = u_ref[...]
    w = w_ref[...]
    b = b_ref[...]
    dw = [jnp.zeros((1, CONV_CT), F32) for _ in range(SSD_CONV + 1)]
    for c in range(t // rb):
        base = CONV_PAD + c * rb
        us = [pad_ref[pl.ds(base + k - 2, rb), :] for k in range(SSD_CONV)]
        acc = b + w[0:1, :] * us[0]
        for k in range(1, SSD_CONV):
            acc = acc + w[k:k + 1, :] * us[k]
        sg = _sigmoid(acc)
        dyc = da_ref[pl.ds(c * rb, rb), :] * (sg * (1.0 + acc * (1.0 - sg)))
        pad2_ref[pl.ds(base, rb), :] = dyc
        for k in range(SSD_CONV):
            dw[k] = dw[k] + jnp.sum(dyc * us[k], axis=0, keepdims=True)
        dw[SSD_CONV] = dw[SSD_CONV] + jnp.sum(dyc, axis=0, keepdims=True)
    for c in range(t // rb):
        base = CONV_PAD + c * rb
        acc = w[0:1, :] * pad2_ref[pl.ds(base + 2, rb), :]
        for k in range(1, SSD_CONV):
            acc = acc + w[k:k + 1, :] * pad2_ref[pl.ds(base + 2 - k, rb), :]
        du_ref[pl.ds(c * rb, rb), :] = acc.astype(du_ref.dtype)
    dw_ref[...] = jnp.concatenate(dw + [jnp.zeros((2, CONV_CT), F32)], axis=0)


def _conv_bwd(proj, dact, conv_w8, conv_b):
    t = proj.shape[0]
    off = XBC0 // CONV_CT
    col = pl.BlockSpec((t, CONV_CT), lambda i: (0, i))
    return pl.pallas_call(
        functools.partial(_conv_bwd_body, t=t), name="conv_bwd", grid=(CONV_DIM // CONV_CT,),
        in_specs=[pl.BlockSpec((t, CONV_CT), lambda i: (0, off + i)), col, pl.BlockSpec((8, CONV_CT), lambda i: (0, i)),
                  pl.BlockSpec((1, CONV_CT), lambda i: (0, i))],
        out_specs=[col, pl.BlockSpec((8, CONV_CT), lambda i: (0, i))],
        out_shape=[jax.ShapeDtypeStruct((t, CONV_DIM), BF16), jax.ShapeDtypeStruct((8, CONV_DIM), F32)],
        scratch_shapes=[pltpu.VMEM((t + 2 * CONV_PAD, CONV_CT), F32), pltpu.VMEM((t + 2 * CONV_PAD, CONV_CT), F32)],
        compiler_params=_params(1),
    )(proj, dact, conv_w8, conv_b)


GW = SSD_REP * SSD_HEAD_DIM


def _expand_np(hoff):
    e = np.zeros((LANE, D_INNER), np.float32)
    for h in range(SSD_HEADS):
        e[hoff + h, h * SSD_HEAD_DIM:(h + 1) * SSD_HEAD_DIM] = 1.0
    return e


def _head_masks():
    lane = lax.broadcasted_iota(jnp.int32, (1, GW), 1)
    return [((lane >= r * SSD_HEAD_DIM) & (lane < (r + 1) * SSD_HEAD_DIM)).astype(F32) for r in range(SSD_REP)]


def _ssd_common(dtc_ref, dtr_ref, bc_ref, br_ref, alc_ref, alr_ref, e_ref, rev):
    L = CHUNK
    ri = lax.broadcasted_iota(jnp.int32, (L, L), 0)
    ci = lax.broadcasted_iota(jnp.int32, (L, L), 1)
    tri = (ri <= ci) if rev else (ri >= ci)
    trit = (ri >= ci) if rev else (ri <= ci)
    raw_c = dtc_ref[...] + bc_ref[...]
    dt_c = _softplus(raw_c)
    a_c = -jnp.exp(alc_ref[...])
    cs_c = jnp.dot(tri.astype(F32), dt_c * a_c, precision=lax.Precision.HIGHEST, preferred_element_type=F32)
    dt_r = _softplus(dtr_ref[...] + br_ref[...])
    a_r = -jnp.exp(alr_ref[...])
    cs_r = jnp.dot(dt_r * a_r, trit.astype(F32), precision=lax.Precision.HIGHEST, preferred_element_type=F32)
    il = 0 if rev else L - 1
    e_c = jnp.exp(cs_c)
    w_c = jnp.exp(cs_c[il:il + 1, :] - cs_c)
    ex = _split_dot(jnp.concatenate([dt_c, e_c, w_c], axis=0), e_ref[...])
    return dict(tri=tri, trit=trit, raw_c=raw_c, dt_c=dt_c, a_c=a_c, cs_c=cs_c, cs_r=cs_r, il=il, e_last=e_c[il:il + 1],
                dt_x=ex[0:L], e_x=ex[L:2 * L], w_x=ex[2 * L:3 * L], cd_x=ex[L + il:L + il + 1])


def _ssd_fwd_body(*refs, rev, hoff, nsteps, comm):
    L = CHUNK
    ins, cin, (y_ref, st_ref), cout, (s_scr,), csem = _split_refs(refs, 10, 2, comm)
    xs_ref, b_ref, c_ref, dtc_ref, dtr_ref, bc_ref, br_ref, alc_ref, alr_ref, e_ref = ins
    if comm is not None:
        comm.run(cin, cout, csem, pl.program_id(0), nsteps)

    @pl.when(pl.program_id(0) == 0)
    def _():
        s_scr[...] = jnp.zeros_like(s_scr)

    st_ref[0] = s_scr[...]
    q = _ssd_common(dtc_ref, dtr_ref, bc_ref, br_ref, alc_ref, alr_ref, e_ref, rev)
    masks = _head_masks()
    xdt = xs_ref[...] * q["dt_x"]
    xw = xdt * q["w_x"]
    for g in range(SSD_GROUPS):
        gc = slice(g * GW, (g + 1) * GW)
        nc = slice(g * SSD_STATE, (g + 1) * SSD_STATE)
        bg = _mx(b_ref[:, nc])
        cg = _mx(c_ref[:, nc])
        cb = _dot(cg, bg, ((1,), (1,)))
        sg = s_scr[:, gc]
        y = _dot(cg, _mx(sg)) * q["e_x"][:, gc]
        ms = []
        for r in range(SSD_REP):
            hh = hoff + g * SSD_REP + r
            diff = q["cs_c"][:, hh:hh + 1] - q["cs_r"][hh:hh + 1, :]
            ms.append(_mx(cb * jnp.exp(jnp.where(q["tri"], diff, NEG))))
        y4 = _dot(jnp.concatenate(ms, axis=0), _mx(xdt[:, gc]))
        for r in range(SSD_REP):
            y = y + y4[r * L:(r + 1) * L] * masks[r]
        y_ref[:, gc] = y
        s_scr[:, gc] = sg * q["cd_x"][:, gc] + _dot(bg, _mx(xw[:, gc]), ((0,), (0,)))


def _ssd_specs(t, rev):
    nc = t // CHUNK
    cm = (lambda i: nc - 1 - i) if rev else (lambda i: i)
    xs = pl.BlockSpec((CHUNK, D_INNER), lambda i: (cm(i), 0))
    bb = pl.BlockSpec((CHUNK, SSD_GROUPS * SSD_STATE), lambda i: (cm(i), 2))
    cc = pl.BlockSpec((CHUNK, SSD_GROUPS * SSD_STATE), lambda i: (cm(i), 3))
    dtc = pl.BlockSpec((CHUNK, LANE), lambda i: (cm(i), DT0 // LANE))
    dtr = pl.BlockSpec((LANE, CHUNK), lambda i: (0, cm(i)))
    colv = pl.BlockSpec((1, LANE), lambda i: (0, 0))
    rowv = pl.BlockSpec((LANE, 1), lambda i: (0, 0))
    return cm, xs, bb, cc, dtc, dtr, colv, rowv


def _ssd_fwd(xbc, proj, dtt, bias, alog, rev, comm=None):
    t = xbc.shape[0]
    nc = t // CHUNK
    cm, xs, bb, cc, dtc, dtr, colv, rowv = _ssd_specs(t, rev)
    hoff = SSD_HEADS if rev else 0
    e = jnp.asarray(_expand_np(hoff), BF16)
    in_specs = [xs, bb, cc, dtc, dtr, colv, rowv, colv, rowv, pl.BlockSpec((LANE, D_INNER), lambda i: (0, 0))]
    args = [xbc, xbc, xbc, proj, dtt, bias.reshape(1, LANE), bias.reshape(LANE, 1), alog.reshape(1, LANE), alog.reshape(LANE, 1), e]
    out_specs = [xs, pl.BlockSpec((1, SSD_STATE, D_INNER), lambda i: (cm(i), 0, 0))]
    out_shape = [jax.ShapeDtypeStruct((t, D_INNER), F32), jax.ShapeDtypeStruct((nc, SSD_STATE, D_INNER), F32)]
    scratch = [pltpu.VMEM((SSD_STATE, D_INNER), F32)]
    if comm is not None:
        in_specs, args = in_specs + [ANY] * comm.n, args + comm.arrs
        out_specs, out_shape = out_specs + [ANY] * comm.n, out_shape + comm.out_shapes()
        scratch = scratch + comm.sems()
    return pl.pallas_call(
        functools.partial(_ssd_fwd_body, rev=rev, hoff=hoff, nsteps=nc, comm=comm), name="ssd_fwd_rev" if rev else "ssd_fwd",
        grid=(nc,), in_specs=in_specs, out_specs=out_specs, out_shape=out_shape, scratch_shapes=scratch,
        compiler_params=_params(1),
    )(*args)


def _ssd_bwd_body(*refs, rev, hoff, first, nsteps, comm):
    L = CHUNK
    ins, cin, (dx_ref, ddt_ref, sm_ref), cout, (ds_scr,), csem = _split_refs(refs, 14 if first else 17, 3, comm)
    (xs_ref, b_ref, c_ref, dtc_ref, dtr_ref, bc_ref, br_ref, alc_ref, alr_ref, e_ref, et_ref, dy_ref, sp_ref,
     dsk_ref) = ins[:14]
    if not first:
        pdx_ref, pddt_ref, psm_ref = ins[14:]
    if comm is not None:
        comm.run(cin, cout, csem, pl.program_id(0), nsteps)

    @pl.when(pl.program_id(0) == 0)
    def _():
        ds_scr[...] = jnp.zeros_like(ds_scr)
        sm_ref[...] = jnp.zeros_like(sm_ref) if first else psm_ref[...]

    q = _ssd_common(dtc_ref, dtr_ref, bc_ref, br_ref, alc_ref, alr_ref, e_ref, rev)
    masks = _head_masks()
    xs = xs_ref[...]
    dy = dy_ref[...]
    xdt = xs * q["dt_x"]
    xw = xdt * q["w_x"]
    dye = dy * q["e_x"]
    ds_old = ds_scr[...]
    dxdt_parts, q1_parts, q2_parts = [], [], []
    ddiag = jnp.zeros((L, LANE), F32)
    for g in range(SSD_GROUPS):
        gc = slice(g * GW, (g + 1) * GW)
        nc = slice(g * SSD_STATE, (g + 1) * SSD_STATE)
        bg = _mx(b_ref[:, nc])
        cg = _mx(c_ref[:, nc])
        cb = _dot(cg, bg, ((1,), (1,)))
        cbt = _dot(bg, cg, ((1,), (1,)))
        dy_g = dy[:, gc]
        xdt_g = xdt[:, gc]
        dsg = ds_old[:, gc]
        dm = _dot(_mx(jnp.concatenate([dy_g * m for m in masks], axis=0)), _mx(xdt_g), ((1,), (1,)))
        dmt = _dot(_mx(jnp.concatenate([xdt_g * m for m in masks], axis=0)), _mx(dy_g), ((1,), (1,)))
        dcb = jnp.zeros((L, L), F32)
        dcbt = jnp.zeros((L, L), F32)
        mts = []
        for r in range(SSD_REP):
            hh = hoff + g * SSD_REP + r
            col = q["cs_c"][:, hh:hh + 1]
            row = q["cs_r"][hh:hh + 1, :]
            dec = jnp.exp(jnp.where(q["tri"], col - row, NEG))
            dect = jnp.exp(jnp.where(q["trit"], row - col, NEG))
            pd = dm[r * L:(r + 1) * L] * dec
            pdt = dmt[r * L:(r + 1) * L] * dect
            dcb = dcb + pd
            dcbt = dcbt + pdt
            mts.append(_mx(cbt * dect))
            ddiag = ddiag + (jnp.sum(pd * cb, axis=1, keepdims=True) - jnp.sum(pdt * cbt, axis=1, keepdims=True)) * _onehot_lane(hh)
        x4 = _dot(jnp.concatenate(mts, axis=0), _mx(dy_g))
        bds = _dot(bg, _mx(dsg))
        dxdt_g = bds * q["w_x"][:, gc]
        for r in range(SSD_REP):
            dxdt_g = dxdt_g + x4[r * L:(r + 1) * L] * masks[r]
        dxdt_parts.append(dxdt_g)
        spg = _mx(sp_ref[0, :, gc])
        q1_parts.append(dye[:, gc] * _dot(cg, spg))
        q2_parts.append(xw[:, gc] * bds)
        dc_g = _dot(_mx(dcb), bg) + _dot(_mx(dye[:, gc]), spg, ((1,), (1,)))
        db_g = _dot(_mx(dcbt), cg) + _dot(_mx(xw[:, gc]), _mx(dsg), ((1,), (1,)))
        boff = D_INNER + g * SSD_STATE
        coff = D_INNER + SSD_GROUPS * SSD_STATE + g * SSD_STATE
        if first:
            dx_ref[:, boff:boff + SSD_STATE] = db_g
            dx_ref[:, coff:coff + SSD_STATE] = dc_g
        else:
            dx_ref[:, boff:boff + SSD_STATE] = pdx_ref[:, boff:boff + SSD_STATE] + db_g
            dx_ref[:, coff:coff + SSD_STATE] = pdx_ref[:, coff:coff + SSD_STATE] + dc_g
        ds_scr[:, gc] = dsg * q["cd_x"][:, gc] + _dot(cg, _mx(dye[:, gc]), ((0,), (0,)))
    dxdt = jnp.concatenate(dxdt_parts, axis=1)
    et = et_ref[...]
    hs = _split_dot(jnp.concatenate([jnp.concatenate(q1_parts, axis=1), jnp.concatenate(q2_parts, axis=1), dxdt * xs], axis=0), et)
    q1, q2, r3 = hs[0:L], hs[L:2 * L], hs[2 * L:3 * L]
    t_prev = _split_dot(jnp.sum(ds_old * sp_ref[0], axis=0, keepdims=True), et) * q["e_last"]
    rows = lax.broadcasted_iota(jnp.int32, (L, LANE), 0)
    dcs = ddiag + q1 - q2 + jnp.where(rows == q["il"], jnp.sum(q2, axis=0, keepdims=True) + t_prev, 0.0)
    dad = jnp.dot(q["trit"].astype(F32), dcs, precision=lax.Precision.HIGHEST, preferred_element_type=F32)
    ddt = dad * q["a_c"] + r3
    ddt_raw = ddt * _sigmoid(q["raw_c"])
    dal = jnp.sum(dad * q["dt_c"], axis=0, keepdims=True) * q["a_c"]
    dbias = jnp.sum(ddt_raw, axis=0, keepdims=True)
    sm_ref[...] += jnp.concatenate([dal, dbias, jnp.zeros((6, LANE), F32)], axis=0)
    dxs = dxdt * q["dt_x"]
    if first:
        dx_ref[:, 0:D_INNER] = dxs + dy * dsk_ref[...]
        ddt_ref[...] = ddt_raw
    else:
        dx_ref[:, 0:D_INNER] = pdx_ref[:, 0:D_INNER] + dxs
        ddt_ref[...] = pddt_ref[...] + ddt_raw


def _ssd_bwd(xbc, proj, dtt, bias, alog, dy, states, dskip_x, rev, prev=None, comm=None):
    t = xbc.shape[0]
    nc = t // CHUNK
    first = prev is None
    cm, xs, bb, cc, dtc, dtr, colv, rowv = _ssd_specs(t, not rev)
    hoff = SSD_HEADS if rev else 0
    e_np = _expand_np(hoff)
    e = jnp.asarray(e_np, BF16)
    et = jnp.asarray(e_np.T, BF16)
    st_spec = pl.BlockSpec((1, SSD_STATE, D_INNER), lambda i: (cm(i), 0, 0))
    dxo = pl.BlockSpec((CHUNK, CONV_DIM), lambda i: (cm(i), 0))
    ddto = pl.BlockSpec((CHUNK, LANE), lambda i: (cm(i), 0))
    smo = pl.BlockSpec((8, LANE), lambda i: (0, 0))
    in_specs = [xs, bb, cc, dtc, dtr, colv, rowv, colv, rowv, pl.BlockSpec((LANE, D_INNER), lambda i: (0, 0)),
                pl.BlockSpec((D_INNER, LANE), lambda i: (0, 0)), xs, st_spec, pl.BlockSpec((1, D_INNER), lambda i: (0, 0))]
    args = [xbc, xbc, xbc, proj, dtt, bias.reshape(1, LANE), bias.reshape(LANE, 1), alog.reshape(1, LANE),
            alog.reshape(LANE, 1), e, et, dy, states, dskip_x]
    if not first:
        in_specs += [dxo, ddto, smo]
        args += list(prev)
    out_specs = [dxo, ddto, smo]
    out_shape = [jax.ShapeDtypeStruct((t, CONV_DIM), F32), jax.ShapeDtypeStruct((t, LANE), F32),
                 jax.ShapeDtypeStruct((8, LANE), F32)]
    scratch = [pltpu.VMEM((SSD_STATE, D_INNER), F32)]
    if comm is not None:
        in_specs, args = in_specs + [ANY] * comm.n, args + comm.arrs
        out_specs, out_shape = out_specs + [ANY] * comm.n, out_shape + comm.out_shapes()
        scratch = scratch + comm.sems()
    return pl.pallas_call(
        functools.partial(_ssd_bwd_body, rev=rev, hoff=hoff, first=first, nsteps=nc, comm=comm),
        name="ssd_bwd_rev" if rev else "ssd_bwd", grid=(nc,), in_specs=in_specs, out_specs=out_specs, out_shape=out_shape,
        scratch_shapes=scratch, compiler_params=_params(1),
    )(*args)


def _gnorm_parts(yf_ref, yb_ref, xs_ref, z_ref, dsk_ref):
    xs = xs_ref[...]
    y = yf_ref[...] + yb_ref[...] + xs * dsk_ref[...]
    z = z_ref[...]
    sg = _sigmoid(z)
    s = z * sg
    v = y * s
    rs = []
    for g in range(SSD_GROUPS):
        vg = v[:, g * GW:(g + 1) * GW]
        rs.append(jnp.broadcast_to(lax.rsqrt(jnp.mean(vg * vg, axis=-1, keepdims=True) + EPS), vg.shape))
    return xs, y, z, sg, s, v, jnp.concatenate(rs, axis=1)


def _gnorm_fwd_body(yf_ref, yb_ref, xs_ref, z_ref, dsk_ref, w_ref, o_ref):
    _, _, _, _, _, v, rstd = _gnorm_parts(yf_ref, yb_ref, xs_ref, z_ref, dsk_ref)
    o_ref[...] = (v * rstd * w_ref[...]).astype(o_ref.dtype)


def _gnorm_fwd(yf, yb, xbc, proj, dskip_x, norm_w, tr=256):
    t = yf.shape[0]
    tr = min(tr, t)
    row = pl.BlockSpec((tr, D_INNER), lambda i: (i, 0))
    vec = pl.BlockSpec((1, D_INNER), lambda i: (0, 0))
    return pl.pallas_call(
        functools.partial(_gnorm_fwd_body), name="gnorm_fwd", grid=(t // tr,),
        in_specs=[row, row, row, pl.BlockSpec((tr, D_INNER), lambda i: (i, Z0 // D_INNER)), vec, vec],
        out_specs=row, out_shape=jax.ShapeDtypeStruct((t, D_INNER), BF16), compiler_params=_params(1),
    )(yf, yb, xbc, proj, dskip_x, norm_w.reshape(1, D_INNER))


def _gnorm_bwd_body(du_ref, yf_ref, yb_ref, xs_ref, z_ref, dsk_ref, w_ref, et_ref, dy_ref, dz_ref, dw_ref, dd_ref, acc_ref, *, nsteps):
    i = pl.program_id(0)

    @pl.when(i == 0)
    def _():
        dw_ref[...] = jnp.zeros_like(dw_ref)
        acc_ref[...] = jnp.zeros_like(acc_ref)

    xs, y, z, sg, s, v, rstd = _gnorm_parts(yf_ref, yb_ref, xs_ref, z_ref, dsk_ref)
    du = du_ref[...]
    n = v * rstd
    dn = du * w_ref[...]
    dw_ref[...] += jnp.sum(du * n, axis=0, keepdims=True)
    prod = dn * n
    means = []
    for g in range(SSD_GROUPS):
        pg = prod[:, g * GW:(g + 1) * GW]
        means.append(jnp.broadcast_to(jnp.mean(pg, axis=-1, keepdims=True), pg.shape))
    dv = rstd * (dn - n * jnp.concatenate(means, axis=1))
    dy = dv * s
    dy_ref[...] = dy
    dz_ref[...] = (dv * y * (sg * (1.0 + z * (1.0 - sg)))).astype(dz_ref.dtype)
    acc_ref[...] += jnp.sum(dy * xs, axis=0, keepdims=True)

    @pl.when(i == nsteps - 1)
    def _():
        dd_ref[...] = _split_dot(acc_ref[...], et_ref[...])


def _gnorm_bwd(du, yf, yb, xbc, proj, dskip_x, norm_w, tr=256):
    t = yf.shape[0]
    tr = min(tr, t)
    row = pl.BlockSpec((tr, D_INNER), lambda i: (i, 0))
    vec = pl.BlockSpec((1, D_INNER), lambda i: (0, 0))
    et = jnp.asarray(_expand_np(0).T, BF16)
    return pl.pallas_call(
        functools.partial(_gnorm_bwd_body, nsteps=t // tr), name="gnorm_bwd", grid=(t // tr,),
        in_specs=[row, row, row, row, pl.BlockSpec((tr, D_INNER), lambda i: (i, Z0 // D_INNER)), vec, vec,
                  pl.BlockSpec((D_INNER, LANE), lambda i: (0, 0))],
        out_specs=[row, row, vec, pl.BlockSpec((1, LANE), lambda i: (0, 0))],
        out_shape=[jax.ShapeDtypeStruct((t, D_INNER), F32), jax.ShapeDtypeStruct((t, D_INNER), BF16),
                   jax.ShapeDtypeStruct((1, D_INNER), F32), jax.ShapeDtypeStruct((1, LANE), F32)],
        scratch_shapes=[pltpu.VMEM((1, D_INNER), F32)],
        compiler_params=_params(1),
    )(du, yf, yb, xbc, proj, dskip_x, norm_w.reshape(1, D_INNER), et)


def _gate_fwd_body(gt_ref, b_ref, ys_ref, ya_ref, o_ref):
    g = _sigmoid(gt_ref[...] + b_ref[...])
    o_ref[...] = (g[:, 0:D_MODEL] * ys_ref[...] + g[:, D_MODEL:] * ya_ref[...]).astype(o_ref.dtype)


def _gate_fwd(proj, b_gate, y_ssd, y_attn, tr=512):
    t = y_ssd.shape[0]
    tr = min(tr, t)
    row = pl.BlockSpec((tr, D_MODEL), lambda i: (i, 0))
    return pl.pallas_call(
        functools.partial(_gate_fwd_body), name="gate_fwd", grid=(t // tr,),
        in_specs=[pl.BlockSpec((tr, 2 * D_MODEL), lambda i: (i, GT0 // (2 * D_MODEL))),
                  pl.BlockSpec((1, 2 * D_MODEL), lambda i: (0, 0)), row, row],
        out_specs=row, out_shape=jax.ShapeDtypeStruct((t, D_MODEL), BF16), compiler_params=_params(1),
    )(proj, b_gate.reshape(1, 2 * D_MODEL), y_ssd, y_attn)


def _gate_bwd_body(dm_ref, gt_ref, b_ref, ys_ref, ya_ref, dys_ref, dya_ref, dgt_ref, db_ref):
    @pl.when(pl.program_id(0) == 0)
    def _():
        db_ref[...] = jnp.zeros_like(db_ref)

    g = _sigmoid(gt_ref[...] + b_ref[...])
    gs = g[:, 0:D_MODEL]
    ga = g[:, D_MODEL:]
    dm = dm_ref[...]
    dys_ref[...] = (dm * gs).astype(dys_ref.dtype)
    dya_ref[...] = (dm * ga).astype(dya_ref.dtype)
    dgt = jnp.concatenate([dm * ys_ref[...] * gs * (1.0 - gs), dm * ya_ref[...] * ga * (1.0 - ga)], axis=1)
    dgt_ref[...] = dgt.astype(dgt_ref.dtype)
    db_ref[...] += jnp.sum(dgt, axis=0, keepdims=True)


def _gate_bwd(dmix, proj, b_gate, y_ssd, y_attn, tr=512):
    t = y_ssd.shape[0]
    tr = min(tr, t)
    row = pl.BlockSpec((tr, D_MODEL), lambda i: (i, 0))
    row2 = pl.BlockSpec((tr, 2 * D_MODEL), lambda i: (i, 0))
    vec2 = pl.BlockSpec((1, 2 * D_MODEL), lambda i: (0, 0))
    return pl.pallas_call(
        functools.partial(_gate_bwd_body), name="gate_bwd", grid=(t // tr,),
        in_specs=[row, pl.BlockSpec((tr, 2 * D_MODEL), lambda i: (i, GT0 // (2 * D_MODEL))), vec2, row, row],
        out_specs=[row, row, row2, vec2],
        out_shape=[jax.ShapeDtypeStruct((t, D_MODEL), BF16), jax.ShapeDtypeStruct((t, D_MODEL), BF16),
                   jax.ShapeDtypeStruct((t, 2 * D_MODEL), BF16), jax.ShapeDtypeStruct((1, 2 * D_MODEL), F32)],
        compiler_params=_params(1),
    )(dmix, proj, b_gate.reshape(1, 2 * D_MODEL), y_ssd, y_attn)


def _bucket_np():
    i = np.arange(BLOCK)[:, None]
    j = np.arange(3 * BLOCK)[None, :]
    rel = j - BLOCK - i
    nb = N_BUCKETS // 2
    max_exact = nb // 2
    ret = np.where(rel > 0, nb, 0)
    n = np.abs(rel)
    nf = np.maximum(n, 1).astype(np.float32)
    large = max_exact + (np.log(nf / np.float32(max_exact)) / np.float32(math.log(MAX_DISTANCE / max_exact))
                         * np.float32(nb - max_exact)).astype(np.int32)
    large = np.minimum(large, nb - 1)
    bucket = ret + np.where(n < max_exact, n, large)
    band = np.abs(rel) <= WINDOW
    return bucket, band


def _onehot_np():
    bucket, _ = _bucket_np()
    oh = np.zeros((BLOCK * 3 * BLOCK, LANE), np.float32)
    oh[np.arange(oh.shape[0]), bucket.reshape(-1)] = 1.0
    return oh


def _attn_logits(q_ref, kg, bias_ref, mask_ref, edge, h):
    qh = _mx(q_ref[:, h * HEAD_PAD:(h + 1) * HEAD_PAD])
    s = _dot(qh, kg, ((1,), (1,))) * (HEAD_DIM ** -0.5) + bias_ref[h] + mask_ref[...] + edge
    return qh, s


def _attn_edge(n, nb):
    col = lax.broadcasted_iota(jnp.int32, (1, 3 * BLOCK), 1)
    bad = ((col < BLOCK) & (n == 0)) | ((col >= 2 * BLOCK) & (n == nb - 1))
    return jnp.where(bad, NEG, 0.0)


def _onehot_lane(h):
    return (lax.broadcasted_iota(jnp.int32, (1, LANE), 1) == h).astype(F32)


def _attn_fwd_body(q_ref, kp_ref, kc_ref, kn_ref, vp_ref, vc_ref, vn_ref, bias_ref, mask_ref, sink_ref, o_ref, lse_ref, *, nb):
    n = pl.program_id(0)
    edge = _attn_edge(n, nb)
    lse = jnp.zeros((BLOCK, LANE), F32)
    for g in range(KV_HEADS):
        gc = slice(g * HEAD_PAD, (g + 1) * HEAD_PAD)
        kg = _mx(jnp.concatenate([kp_ref[:, gc], kc_ref[:, gc], kn_ref[:, gc]], axis=0))
        vg = _mx(jnp.concatenate([vp_ref[:, gc], vc_ref[:, gc], vn_ref[:, gc]], axis=0))
        for r in range(Q_HEADS // KV_HEADS):
            h = g * (Q_HEADS // KV_HEADS) + r
            _, s = _attn_logits(q_ref, kg, bias_ref, mask_ref, edge, h)
            sink = sink_ref[:, h:h + 1]
            m = jnp.maximum(jnp.max(s, axis=-1, keepdims=True), sink)
            p = jnp.exp(s - m)
            den = jnp.sum(p, axis=-1, keepdims=True) + jnp.exp(sink - m)
            o_ref[:, h * HEAD_PAD:(h + 1) * HEAD_PAD] = _dot(_mx(p * (1.0 / den)), vg).astype(o_ref.dtype)
            lse = lse + (m + jnp.log(den)) * _onehot_lane(h)
    lse_ref[...] = lse


def _attn_specs(t, clamp):
    nb = t // BLOCK
    cur = (lambda n: jnp.minimum(n, nb - 1)) if clamp else (lambda n: n)
    prv = lambda n: jnp.maximum(cur(n) - 1, 0)
    nxt = lambda n: jnp.minimum(cur(n) + 1, nb - 1)
    kb, vb = K0 // KP, V0 // KP
    qs = pl.BlockSpec((BLOCK, QP), lambda n: (cur(n), Q0 // QP))
    ks = [pl.BlockSpec((BLOCK, KP), lambda n, f=f: (f(n), kb)) for f in (prv, cur, nxt)]
    vs = [pl.BlockSpec((BLOCK, KP), lambda n, f=f: (f(n), vb)) for f in (prv, cur, nxt)]
    consts = [pl.BlockSpec((Q_HEADS, BLOCK, 3 * BLOCK), lambda n: (0, 0, 0)), pl.BlockSpec((BLOCK, 3 * BLOCK), lambda n: (0, 0)),
              pl.BlockSpec((1, LANE), lambda n: (0, 0))]
    return nb, cur, qs, ks, vs, consts


def _attn_fwd(proj, bias, mask, sink):
    t = proj.shape[0]
    nb, cur, qs, ks, vs, consts = _attn_specs(t, False)
    return pl.pallas_call(
        functools.partial(_attn_fwd_body, nb=nb), name="attn_fwd", grid=(nb,),
        in_specs=[qs] + ks + vs + consts,
        out_specs=[pl.BlockSpec((BLOCK, QP), lambda n: (n, 0)), pl.BlockSpec((BLOCK, LANE), lambda n: (n, 0))],
        out_shape=[jax.ShapeDtypeStruct((t, QP), BF16), jax.ShapeDtypeStruct((t, LANE), F32)],
        compiler_params=_params(1),
    )(proj, proj, proj, proj, proj, proj, proj, bias, mask, sink)


def _attn_bwd_body(q_ref, kp_ref, kc_ref, kn_ref, vp_ref, vc_ref, vn_ref, bias_ref, mask_ref, sink_ref, do_ref, o_ref, lse_ref,
                   dq_ref, dk_ref, dv_ref, dbias_ref, dsink_ref, ak, bk, av, bv, *, nb):
    n = pl.program_id(0)
    scale = HEAD_DIM ** -0.5

    @pl.when(n == 0)
    def _():
        for r in (ak, bk, av, bv, dbias_ref, dsink_ref):
            r[...] = jnp.zeros_like(r)

    @pl.when(n < nb)
    def _():
        edge = _attn_edge(n, nb)
        dsink = jnp.zeros((1, LANE), F32)
        for g in range(KV_HEADS):
            gc = slice(g * HEAD_PAD, (g + 1) * HEAD_PAD)
            kg = _mx(jnp.concatenate([kp_ref[:, gc], kc_ref[:, gc], kn_ref[:, gc]], axis=0))
            vg = _mx(jnp.concatenate([vp_ref[:, gc], vc_ref[:, gc], vn_ref[:, gc]], axis=0))
            dk_c = jnp.zeros((3 * BLOCK, HEAD_PAD), F32)
            dv_c = jnp.zeros((3 * BLOCK, HEAD_PAD), F32)
            for r in range(Q_HEADS // KV_HEADS):
                h = g * (Q_HEADS // KV_HEADS) + r
                hc = slice(h * HEAD_PAD, (h + 1) * HEAD_PAD)
                qh, s = _attn_logits(q_ref, kg, bias_ref, mask_ref, edge, h)
                lse = lse_ref[:, h:h + 1]
                p = jnp.exp(s - lse)
                do = do_ref[:, hc]
                dp = _dot(_mx(do), vg, ((1,), (1,)))
                delta = jnp.sum(do * o_ref[:, hc].astype(F32), axis=-1, keepdims=True)
                ds = p * (dp - delta)
                psink = jnp.exp(sink_ref[:, h:h + 1] - lse)
                dsink = dsink - jnp.sum(psink * delta, axis=0, keepdims=True) * _onehot_lane(h)
                dbias_ref[h] += ds
                dsb = _mx(ds)
                dq_ref[:, hc] = (_dot(dsb, kg) * scale).astype(dq_ref.dtype)
                dk_c = dk_c + _dot(dsb, qh, ((0,), (0,))) * scale
                dv_c = dv_c + _dot(_mx(p), _mx(do), ((0,), (0,)))
            for acc_a, acc_b, out, c in ((ak, bk, dk_ref, dk_c), (av, bv, dv_ref, dv_c)):
                out[:, gc] = (acc_a[:, gc] + c[0:BLOCK]).astype(out.dtype)
                acc_a[:, gc] = acc_b[:, gc] + c[BLOCK:2 * BLOCK]
                acc_b[:, gc] = c[2 * BLOCK:3 * BLOCK]
        dsink_ref[...] += dsink

    @pl.when(n == nb)
    def _():
        dk_ref[...] = ak[...].astype(dk_ref.dtype)
        dv_ref[...] = av[...].astype(dv_ref.dtype)


def _attn_bwd(proj, bias, mask, sink, do, o, lse):
    t = proj.shape[0]
    nb, cur, qs, ks, vs, consts = _attn_specs(t, True)
    rowq = pl.BlockSpec((BLOCK, QP), lambda n: (cur(n), 0))
    late = pl.BlockSpec((BLOCK, KP), lambda n: (jnp.maximum(n - 1, 0), 0))
    return pl.pallas_call(
        functools.partial(_attn_bwd_body, nb=nb), name="attn_bwd", grid=(nb + 1,),
        in_specs=[qs] + ks + vs + consts + [rowq, rowq, pl.BlockSpec((BLOCK, LANE), lambda n: (cur(n), 0))],
        out_specs=[rowq, late, late, pl.BlockSpec((Q_HEADS, BLOCK, 3 * BLOCK), lambda n: (0, 0, 0)),
                   pl.BlockSpec((1, LANE), lambda n: (0, 0))],
        out_shape=[jax.ShapeDtypeStruct((t, QP), BF16), jax.ShapeDtypeStruct((t, KP), BF16), jax.ShapeDtypeStruct((t, KP), BF16),
                   jax.ShapeDtypeStruct((Q_HEADS, BLOCK, 3 * BLOCK), F32), jax.ShapeDtypeStruct((1, LANE), F32)],
        scratch_shapes=[pltpu.VMEM((BLOCK, KP), F32) for _ in range(4)],
        compiler_params=_params(1),
    )(proj, proj, proj, proj, proj, proj, proj, bias, mask, sink, do, o, lse)


def _adamw_body(gp_ref, w_ref, m_ref, v_ref, g_ref, d_ref, nm_ref, nv_ref, *, nparts):
    g = gp_ref[0].astype(F32)
    for j in range(1, nparts):
        g = g + gp_ref[j].astype(F32)
    m = ADAM_B1 * m_ref[...] + (1.0 - ADAM_B1) * g
    v = ADAM_B2 * v_ref[...] + (1.0 - ADAM_B2) * (g * g)
    m_hat = m / (1.0 - ADAM_B1 ** ADAM_STEP)
    v_hat = v / (1.0 - ADAM_B2 ** ADAM_STEP)
    g_ref[...] = g
    d_ref[...] = -ADAM_LR * (m_hat / (jnp.sqrt(v_hat) + ADAM_EPS) + ADAM_WD * w_ref[...])
    nm_ref[...] = m
    nv_ref[...] = v


def _adamw(gparts, w, m, v, name, tr=256):
    p, r, c = gparts.shape
    tr = min(tr, r)
    assert r % tr == 0
    row = pl.BlockSpec((tr, c), lambda i: (i, 0))
    sd = jax.ShapeDtypeStruct((r, c), F32)
    return pl.pallas_call(
        functools.partial(_adamw_body, nparts=p), name=name, grid=(r // tr,),
        in_specs=[pl.BlockSpec((p, tr, c), lambda i: (0, i, 0)), row, row, row],
        out_specs=[row, row, row, row], out_shape=[sd, sd, sd, sd], compiler_params=_params(1),
    )(gparts, w, m, v)


MESH = pl.DeviceIdType.MESH
N_REL = N_DEV - 1


def _gather_phases(ins, outs, sems):
    n = len(ins)
    send_sems, recv_sems, local_sems = sems
    x, y, c = lax.axis_index("x"), lax.axis_index("y"), lax.axis_index("c")
    me, sibling = (x, y, c), (x, y, 1 - c)
    chips = [(1 - x, y), (x, 1 - y), (1 - x, 1 - y)]

    def slot(a, p):
        return outs[a].at[4 * p[0] + 2 * p[1] + p[2]]

    def copy(a, k, block, to, src=None):
        return pltpu.make_async_remote_copy(
            src_ref=slot(a, block) if src is None else src, dst_ref=slot(a, block),
            send_sem=send_sems.at[a * N_REL + k], recv_sem=recv_sems.at[a * N_REL + k], device_id=to, device_id_type=MESH)

    mine = [pltpu.make_async_copy(ins[a], slot(a, me), local_sems.at[a]) for a in range(n)]
    first = []
    for a in range(n):
        first.append(copy(a, 0, me, sibling, src=ins[a]))
        first += [copy(a, 1 + j, me, (*chip, c), src=ins[a]) for j, chip in enumerate(chips)]
    passed = [copy(a, 4 + j, (*chip, c), sibling) for j, chip in enumerate(chips) for a in range(n)]

    def start():
        for cp in mine + first:
            cp.start()

    def pass_on():
        i = 0
        for j, chip in enumerate(chips):
            for a in range(n):
                copy(a, 1 + j, (*chip, c), me).wait_recv()
                passed[i].start()
                i += 1

    def finish():
        for a in range(n):
            copy(a, 0, sibling, me).wait_recv()
            for j, chip in enumerate(chips):
                copy(a, 4 + j, (*chip, 1 - c), me).wait_recv()
        for cp in first + passed:
            cp.wait_send()
        for cp in mine:
            cp.wait()

    return start, pass_on, finish


def _exchange_phases(ins, outs, sems):
    n = len(ins)
    send_sems, recv_sems, local_sems = sems
    x, y, c = lax.axis_index("x"), lax.axis_index("y"), lax.axis_index("c")
    me = 4 * x + 2 * y + c
    mine = [pltpu.make_async_copy(ins[a].at[me], outs[a].at[me], local_sems.at[a]) for a in range(n)]
    copies = []
    for a in range(n):
        for k in range(1, N_DEV):
            px = 1 - x if k & 4 else x
            py = 1 - y if k & 2 else y
            pc = 1 - c if k & 1 else c
            peer = 4 * px + 2 * py + pc
            send = pltpu.make_async_remote_copy(
                src_ref=ins[a].at[peer], dst_ref=outs[a].at[me], send_sem=send_sems.at[a * N_REL + k - 1],
                recv_sem=recv_sems.at[a * N_REL + k - 1], device_id=(px, py, pc), device_id_type=MESH)
            recv = pltpu.make_async_remote_copy(
                src_ref=ins[a].at[me], dst_ref=outs[a].at[peer], send_sem=send_sems.at[a * N_REL + k - 1],
                recv_sem=recv_sems.at[a * N_REL + k - 1], device_id=(px, py, pc), device_id_type=MESH)
            copies.append((send, recv))

    def start():
        for cp in mine:
            cp.start()
        for send, _ in copies:
            send.start()

    def finish():
        for send, recv in copies:
            send.wait_send()
            recv.wait_recv()
        for cp in mine:
            cp.wait()

    return start, None, finish


class _Comm:
    def __init__(self, kind, arrs):
        self.kind, self.arrs, self.n = kind, list(arrs), len(arrs)

    def out_shapes(self):
        if self.kind == "gather":
            return [jax.ShapeDtypeStruct((N_DEV,) + a.shape, a.dtype) for a in self.arrs]
        return [jax.ShapeDtypeStruct(a.shape, a.dtype) for a in self.arrs]

    def sems(self):
        return [pltpu.SemaphoreType.DMA((self.n * N_REL,)), pltpu.SemaphoreType.DMA((self.n * N_REL,)),
                pltpu.SemaphoreType.DMA((self.n,))]

    def run(self, ins, outs, sems, step, nsteps):
        start, mid, finish = (_gather_phases if self.kind == "gather" else _exchange_phases)(ins, outs, sems)
        if nsteps is None:
            start()
            if mid is not None:
                mid()
            finish()
            return
        pl.when(step == 0)(start)
        if mid is not None:
            pl.when(step == nsteps // 2)(mid)
        pl.when(step == nsteps - 1)(finish)


def _split_refs(refs, n_in, n_out, comm):
    k = comm.n if comm is not None else 0
    i = 0
    parts = []
    for cnt in (n_in, k, n_out, k):
        parts.append(refs[i:i + cnt])
        i += cnt
    rest = refs[i:]
    if comm is None:
        return parts[0], (), parts[2], (), rest, ()
    return parts[0], parts[1], parts[2], parts[3], rest[:len(rest) - 3], rest[len(rest) - 3:]


def _comm_body(*refs, comm):
    n = comm.n
    comm.run(refs[:n], refs[n:2 * n], refs[2 * n:], None, None)


def _communicate(comm, name, vmem=False):
    spec = pl.BlockSpec(memory_space=pltpu.VMEM if vmem else pl.ANY)
    return pl.pallas_call(
        functools.partial(_comm_body, comm=comm), name=name,
        in_specs=[spec] * comm.n, out_specs=[spec] * comm.n, out_shape=comm.out_shapes(), scratch_shapes=comm.sems(),
        compiler_params=pltpu.CompilerParams(vmem_limit_bytes=VMEM_LIMIT),
    )(*comm.arrs)


ANY = pl.BlockSpec(memory_space=pl.ANY)


def _pad_heads(w, axis):
    shp = w.shape
    heads = shp[axis] // HEAD_DIM
    w = w.reshape(shp[:axis] + (heads, HEAD_DIM) + shp[axis + 1:])
    pad = [(0, 0)] * w.ndim
    pad[axis + 1] = (0, HEAD_PAD - HEAD_DIM)
    w = jnp.pad(w, pad)
    return w.reshape(shp[:axis] + (heads * HEAD_PAD,) + shp[axis + 1:])


def _unpad_heads(w, axis):
    shp = w.shape
    heads = shp[axis] // HEAD_PAD
    w = w.reshape(shp[:axis] + (heads, HEAD_PAD) + shp[axis + 1:])
    w = lax.slice_in_dim(w, 0, HEAD_DIM, axis=axis + 1)
    return w.reshape(shp[:axis] + (heads * HEAD_DIM,) + shp[axis + 1:])


def _w_in_to_padded(w):
    idx = np.cumsum((0,) + IN_SPLITS)
    z, xbc, dt, q, k, v, gates = [w[:, idx[i]:idx[i + 1]] for i in range(7)]
    zeros = lambda n: jnp.zeros((w.shape[0], n), w.dtype)
    return jnp.concatenate([z, gates, _pad_heads(q, 1), xbc, _pad_heads(k, 1), _pad_heads(v, 1), dt, zeros(LANE - 2 * SSD_HEADS),
                            zeros(NP - DT0 - LANE)], axis=1)


def _w_in_from_padded(w):
    z = w[:, Z0:Z0 + D_INNER]
    gates = w[:, GT0:GT0 + 2 * D_MODEL]
    q = _unpad_heads(w[:, Q0:Q0 + QP], 1)
    xbc = w[:, XBC0:XBC0 + CONV_DIM]
    k = _unpad_heads(w[:, K0:K0 + KP], 1)
    v = _unpad_heads(w[:, V0:V0 + KP], 1)
    dt = w[:, DT0:DT0 + 2 * SSD_HEADS]
    return jnp.concatenate([z, xbc, dt, q, k, v, gates], axis=1)


def _pad_lane(v):
    v = v.reshape(-1)
    return jnp.pad(v, (0, LANE - v.shape[0]))


def _layer_consts(p):
    c = dict(p)
    c["conv_w8"] = jnp.pad(p["conv_w"].reshape(SSD_CONV, CONV_DIM), ((0, 8 - SSD_CONV), (0, 0)))
    c["conv_b1"] = p["conv_b"].reshape(1, CONV_DIM)
    c["dtb"] = _pad_lane(p["dt_bias"])
    c["alog"] = _pad_lane(p["a_log"])
    c["dskip_x"] = jnp.repeat(p["d_skip"], SSD_HEAD_DIM).reshape(1, D_INNER)
    c["sink"] = _pad_lane(p["attn_sink"]).reshape(1, LANE)
    return c


def _hosted(res, n_own, hook):
    res = list(res)
    if hook is not None:
        hook[1](res[n_own:])
    return res[0] if n_own == 1 else res[:n_own]


def _layer_fwd(x, c, w, bias, mask, l, hooks):
    comm = lambda k: hooks[k][0] if k in hooks else None
    s = {"x": x}
    s["h"] = _rms_fwd(x, c["pre_mix_norm"], f"pre_mix_norm_{l}", out_dtype=BF16)
    res = _mm(s["h"], w["w_in"], f"in_proj_{l}", comm=comm("in_proj"))
    proj = _hosted(res, 1, hooks.get("in_proj")) if "in_proj" in hooks else res
    s["proj"] = proj
    s["xbc"] = _conv_fwd(proj, c["conv_w8"], c["conv_b1"])
    s["dtt"] = proj[:, DT0:DT0 + LANE].T
    s["yf"], s["stf"] = _hosted(_ssd_fwd(s["xbc"], proj, s["dtt"], c["dtb"], c["alog"], False, comm=comm("ssd_fwd")), 2,
                                hooks.get("ssd_fwd"))
    s["yb"], s["stb"] = _hosted(_ssd_fwd(s["xbc"], proj, s["dtt"], c["dtb"], c["alog"], True, comm=comm("ssd_fwd_rev")), 2,
                                hooks.get("ssd_fwd_rev"))
    s["u"] = _gnorm_fwd(s["yf"], s["yb"], s["xbc"], proj, c["dskip_x"], c["ssd_norm"])
    s["y_ssd"] = _mm(s["u"], w["w_ssd_out"], f"ssd_out_{l}")
    s["o"], s["lse"] = _attn_fwd(proj, bias, mask, c["sink"])
    s["y_attn"] = _mm(s["o"], w["w_attn_out"], f"attn_out_{l}")
    s["mix"] = _gate_fwd(proj, c["b_gate"], s["y_ssd"], s["y_attn"])
    s["mixed"] = _mm(s["mix"], w["w_o"], f"w_o_{l}")
    s["x1"] = _rms_fwd(s["mixed"], c["post_mix_norm"], f"post_mix_norm_{l}", res=x)
    s["h2"] = _rms_fwd(s["x1"], c["pre_mlp_norm"], f"pre_mlp_norm_{l}", out_dtype=BF16)
    s["f1"], s["a"] = _mm(s["h2"], w["w_mlp_in"], f"mlp_in_{l}", mode="relu2")
    s["f"] = _mm(s["a"], w["w_mlp_out"], f"mlp_out_{l}")
    x2 = _rms_fwd(s["f"], c["post_mlp_norm"], f"post_mlp_norm_{l}", res=s["x1"])
    return x2, s


def _send_rest(gw):
    s_mi = gw["w_mlp_in"].reshape(D_MODEL, N_DEV, -1).transpose(1, 0, 2).astype(BF16)
    parts = [(_unpad_heads(gw[n], 0) if n == "w_attn_out" else gw[n]).reshape(N_DEV, -1, D_MODEL) for n in ROWS]
    return [s_mi, jnp.concatenate(parts, axis=1).astype(BF16)]


def _send_w_in(g):
    return [_w_in_from_padded(g).reshape(D_MODEL, N_DEV, -1).transpose(1, 0, 2).astype(BF16)]


def _layer_bwd(dx2, s, c, w, bias, mask, l, earlier=None, host_own=False):
    gw, gs, got = {}, {}, {}
    df, gs["post_mlp_norm"] = _rms_bwd(s["f"], c["post_mlp_norm"], dx2, f"post_mlp_norm_bwd_{l}", out_dtype=BF16)
    df1 = _mm(df, w["w_mlp_out"], f"mlp_out_dx_{l}", tb=True, mode="relu2_bwd", extra=s["f1"], out_dtype=BF16)
    gw["w_mlp_out"] = _mm(s["a"], df, f"mlp_out_dw_{l}", ta=True)
    gw["w_mlp_in"] = _mm(s["h2"], df1, f"mlp_in_dw_{l}", ta=True)
    dh2 = _mm(df1, w["w_mlp_in"], f"mlp_in_dx_{l}", tb=True)
    dx1, gs["pre_mlp_norm"] = _rms_bwd(s["x1"], c["pre_mlp_norm"], dh2, f"pre_mlp_norm_bwd_{l}", add=dx2)
    dmixed, gs["post_mix_norm"] = _rms_bwd(s["mixed"], c["post_mix_norm"], dx1, f"post_mix_norm_bwd_{l}", out_dtype=BF16)
    gw["w_o"] = _mm(s["mix"], dmixed, f"w_o_dw_{l}", ta=True)
    dmix = _mm(dmixed, w["w_o"], f"w_o_dx_{l}", tb=True)
    dys, dya, dgt, gs["b_gate"] = _gate_bwd(dmix, s["proj"], c["b_gate"], s["y_ssd"], s["y_attn"])
    gw["w_ssd_out"] = _mm(s["u"], dys, f"ssd_out_dw_{l}", ta=True)
    du = _mm(dys, w["w_ssd_out"], f"ssd_out_dx_{l}", tb=True)
    gw["w_attn_out"] = _mm(s["o"], dya, f"attn_out_dw_{l}", ta=True)
    do = _mm(dya, w["w_attn_out"], f"attn_out_dx_{l}", tb=True)
    dq, dk, dv, dbias, dsink = _attn_bwd(s["proj"], bias, mask, c["sink"], do, s["o"], s["lse"])
    gs["attn_sink"] = dsink
    dy, dz, gs["ssd_norm"], gs["d_skip"] = _gnorm_bwd(du, s["yf"], s["yb"], s["xbc"], s["proj"], c["dskip_x"], c["ssd_norm"])
    res = _ssd_bwd(s["xbc"], s["proj"], s["dtt"], c["dtb"], c["alog"], dy, s["stf"], c["dskip_x"], False,
                   comm=None if earlier is None else _Comm("exchange", earlier))
    part, got["earlier"] = res[:3], res[3:]
    res = _ssd_bwd(s["xbc"], s["proj"], s["dtt"], c["dtb"], c["alog"], dy, s["stb"], c["dskip_x"], True, prev=part,
                   comm=_Comm("exchange", _send_rest(gw)) if host_own else None)
    (dxbc, ddt, sm), got["rest"] = res[:3], res[3:]
    gs["a_log"], gs["dt_bias"] = sm[0:1], sm[1:2]
    dxbc_raw, gs["conv"] = _conv_bwd(s["proj"], dxbc, c["conv_w8"], c["conv_b1"])
    t = dx2.shape[0]
    dproj = jnp.concatenate([dz, dgt, dq, dxbc_raw, dk, dv, ddt.astype(BF16), jnp.zeros((t, NP - DT0 - LANE), BF16)], axis=1)
    gw["w_in"] = _mm(s["h"], dproj, f"in_proj_dw_{l}", ta=True)
    if host_own:
        dh, *got["w_in"] = _mm(dproj, w["w_in"], f"in_proj_dx_{l}", tb=True, comm=_Comm("exchange", _send_w_in(gw["w_in"])))
    else:
        dh = _mm(dproj, w["w_in"], f"in_proj_dx_{l}", tb=True)
    dx, gs["pre_mix_norm"] = _rms_bwd(s["x"], c["pre_mix_norm"], dh, f"pre_mix_norm_bwd_{l}", add=dx1)
    return dx, gw, gs, dbias, got


SMALL = (("pre_mix_norm", DEPTH * D_MODEL), ("b_gate", DEPTH * 2 * D_MODEL), ("conv_w", DEPTH * SSD_CONV * CONV_DIM),
         ("conv_b", DEPTH * CONV_DIM), ("dt_bias", DEPTH * 2 * SSD_HEADS), ("a_log", DEPTH * 2 * SSD_HEADS),
         ("d_skip", DEPTH * SSD_HEADS), ("ssd_norm", DEPTH * D_INNER), ("attn_sink", DEPTH * Q_HEADS),
         ("rel_bias_table", N_BUCKETS * Q_HEADS), ("post_mix_norm", DEPTH * D_MODEL), ("pre_mlp_norm", DEPTH * D_MODEL),
         ("post_mlp_norm", DEPTH * D_MODEL))


def _small_rows(n):
    return -(-n // LANE)


def _pack_small(vals, fill=0.0):
    rows = []
    for name, n in SMALL:
        v = vals[name].reshape(-1).astype(F32)
        rows.append(jnp.pad(v, (0, _small_rows(n) * LANE - n), constant_values=fill).reshape(-1, LANE))
    out = jnp.concatenate(rows, axis=0)
    return jnp.pad(out, ((0, -out.shape[0] % 8), (0, 0)), constant_values=fill)


def _unpack_small(packed, shapes):
    out, r = {}, 0
    for name, n in SMALL:
        nr = _small_rows(n)
        out[name] = packed[r:r + nr].reshape(-1)[:n].reshape(shapes[name])
        r += nr
    return out


BIG = ("w_in", "w_ssd_out", "w_attn_out", "w_o", "w_mlp_in", "w_mlp_out")
ROWS = ("w_ssd_out", "w_attn_out", "w_o", "w_mlp_out")


def _pack_rows(vals, lead):
    return jnp.concatenate([vals[n].reshape(lead + (-1, D_MODEL)) for n in ROWS], axis=len(lead))


def kernel(x, pre_mix_norm, w_in, b_gate, conv_w, conv_b, dt_bias, a_log, d_skip, ssd_norm, w_ssd_out, attn_sink, rel_bias_table, w_attn_out, w_o, post_mix_norm, pre_mlp_norm, w_mlp_in, w_mlp_out, post_mlp_norm, loss_target, m_pre_mix_norm, m_w_in, m_b_gate, m_conv_w, m_conv_b, m_dt_bias, m_a_log, m_d_skip, m_ssd_norm, m_w_ssd_out, m_attn_sink, m_rel_bias_table, m_w_attn_out, m_w_o, m_post_mix_norm, m_pre_mlp_norm, m_w_mlp_in, m_w_mlp_out, m_post_mlp_norm, v_pre_mix_norm, v_w_in, v_b_gate, v_conv_w, v_conv_b, v_dt_bias, v_a_log, v_d_skip, v_ssd_norm, v_w_ssd_out, v_attn_sink, v_rel_bias_table, v_w_attn_out, v_w_o, v_post_mix_norm, v_pre_mlp_norm, v_w_mlp_in, v_w_mlp_out, v_post_mlp_norm):
    names = ("pre_mix_norm", "w_in", "b_gate", "conv_w", "conv_b", "dt_bias", "a_log", "d_skip", "ssd_norm", "w_ssd_out",
             "attn_sink", "rel_bias_table", "w_attn_out", "w_o", "post_mix_norm", "pre_mlp_norm", "w_mlp_in", "w_mlp_out",
             "post_mlp_norm")
    W = dict(zip(names, (pre_mix_norm, w_in, b_gate, conv_w, conv_b, dt_bias, a_log, d_skip, ssd_norm, w_ssd_out, attn_sink,
                         rel_bias_table, w_attn_out, w_o, post_mix_norm, pre_mlp_norm, w_mlp_in, w_mlp_out, post_mlp_norm)))
    M = dict(zip(names, (m_pre_mix_norm, m_w_in, m_b_gate, m_conv_w, m_conv_b, m_dt_bias, m_a_log, m_d_skip, m_ssd_norm,
                         m_w_ssd_out, m_attn_sink, m_rel_bias_table, m_w_attn_out, m_w_o, m_post_mix_norm, m_pre_mlp_norm,
                         m_w_mlp_in, m_w_mlp_out, m_post_mlp_norm)))
    V = dict(zip(names, (v_pre_mix_norm, v_w_in, v_b_gate, v_conv_w, v_conv_b, v_dt_bias, v_a_log, v_d_skip, v_ssd_norm,
                         v_w_ssd_out, v_attn_sink, v_rel_bias_table, v_w_attn_out, v_w_o, v_post_mix_norm, v_pre_mlp_norm,
                         v_w_mlp_in, v_w_mlp_out, v_post_mlp_norm)))
    t = x.shape[1]
    shard = {n: W[n].shape for n in names}
    dev = 4 * lax.axis_index("x") + 2 * lax.axis_index("y") + lax.axis_index("c")
    cshard = CONV_DIM // N_DEV

    sh_in = [w_in[l].astype(BF16) for l in range(DEPTH)]
    sh_rest = [[w_mlp_in[l].astype(BF16), _pack_rows({n: W[n][l].astype(BF16) for n in ROWS}, ())] for l in range(DEPTH)]
    full = [{} for _ in range(DEPTH)]

    def take_w_in(l):
        def take(got):
            full[l]["w_in"] = _w_in_to_padded(got[0].transpose(1, 0, 2).reshape(D_MODEL, N_IN))
        return take

    def take_rest(l):
        def take(got):
            full[l]["w_mlp_in"] = got[0].transpose(1, 0, 2).reshape(D_MODEL, D_FF)
            r = 0
            for n in ROWS:
                per = shard[n][1]
                full[l][n] = got[1][:, r:r + per].reshape(N_DEV * per, D_MODEL)
                r += per
            full[l]["w_attn_out"] = _pad_heads(full[l]["w_attn_out"], 0)
        return take

    take_w_in(0)(_communicate(_Comm("gather", [sh_in[0]]), "gather_w_in_0"))
    (g_conv,) = _communicate(_Comm("gather", [conv_w.reshape(-1, LANE)]), "gather_conv_w", vmem=True)
    conv_full = g_conv.reshape(N_DEV, DEPTH, SSD_CONV, cshard).transpose(1, 2, 0, 3).reshape(DEPTH, SSD_CONV, CONV_DIM)
    hooks = [{"in_proj": (_Comm("gather", sh_rest[0]), take_rest(0)),
              "ssd_fwd": (_Comm("gather", [sh_in[1]]), take_w_in(1)),
              "ssd_fwd_rev": (_Comm("gather", sh_rest[1]), take_rest(1))}, {}]

    _, band = _bucket_np()
    onehot = jnp.asarray(_onehot_np(), BF16)
    table_t = jnp.pad(rel_bias_table.T, ((0, 0), (0, LANE - N_BUCKETS)))
    bias = _mm(table_t, onehot, "t5_bias", tb=True, tm=Q_HEADS, tn=3 * BLOCK * BLOCK // 8, split_a=True)
    bias = bias.reshape(Q_HEADS, BLOCK, 3 * BLOCK)
    mask = jnp.asarray(np.where(band, 0.0, NEG), F32)

    consts = []
    for l in range(DEPTH):
        p = {n: W[n][l] for n in names if n not in BIG and n not in ("rel_bias_table", "conv_w")}
        p["conv_w"] = conv_full[l]
        consts.append(_layer_consts(p))
    h = x[0]
    saved = []
    for l in range(DEPTH):
        h, s = _layer_fwd(h, consts[l], full[l], bias, mask, l, hooks[l])
        saved.append(s)
    dy, lsum = _loss(h, loss_target[0])
    loss = lax.psum(0.5 / D_MODEL * jnp.sum(lsum), ("x", "y", "c"))

    gss, dbs = [None] * DEPTH, [None] * DEPTH
    dy, gw1, gss[1], dbs[1], _ = _layer_bwd(dy, saved[1], consts[1], full[1], bias, mask, 1)
    dy, _, gss[0], dbs[0], got = _layer_bwd(dy, saved[0], consts[0], full[0], bias, mask, 0,
                                            earlier=_send_w_in(gw1["w_in"]) + _send_rest(gw1), host_own=True)
    recv = [list(got["w_in"]) + list(got["rest"]), list(got["earlier"])]
    grad_x = dy[None]
    dbias = jnp.concatenate([d.reshape(Q_HEADS, -1) for d in dbs], axis=1)
    d_table = _mm(dbias, jnp.concatenate([onehot] * DEPTH, axis=0), "t5_bias_bwd", tm=Q_HEADS, tk=3 * BLOCK * BLOCK // 8,
                  split_a=True)

    sg = {}
    for n in ("pre_mix_norm", "b_gate", "ssd_norm", "post_mix_norm", "pre_mlp_norm", "post_mlp_norm"):
        sg[n] = jnp.stack([gss[l][n].reshape(-1) for l in range(DEPTH)])
    sg["conv_w"] = jnp.stack([gss[l]["conv"][0:SSD_CONV] for l in range(DEPTH)])
    sg["conv_b"] = jnp.stack([gss[l]["conv"][SSD_CONV] for l in range(DEPTH)])
    sg["dt_bias"] = jnp.stack([gss[l]["dt_bias"][0, 0:2 * SSD_HEADS] for l in range(DEPTH)])
    sg["a_log"] = jnp.stack([gss[l]["a_log"][0, 0:2 * SSD_HEADS] for l in range(DEPTH)])
    sg["d_skip"] = jnp.stack([gss[l]["d_skip"][0, 0:SSD_HEADS] for l in range(DEPTH)])
    sg["attn_sink"] = jnp.stack([gss[l]["attn_sink"][0, 0:Q_HEADS] for l in range(DEPTH)])
    sg["rel_bias_table"] = d_table[:, 0:N_BUCKETS].T
    (small_parts,) = _communicate(_Comm("gather", [_pack_small(sg)]), "gather_small_grads", vmem=True)
    sshape = {n: W[n].shape for n, _ in SMALL}
    sshape["conv_w"] = (DEPTH, SSD_CONV, 1, CONV_DIM)
    pk = lambda d, fill: _pack_small({n: (jnp.full((DEPTH, SSD_CONV, CONV_DIM), fill, F32) if n == "conv_w" else d[n])
                                      for n, _ in SMALL}, fill)
    s_out = [_unpack_small(o, sshape) for o in _adamw(small_parts, pk(W, 1.0), pk(M, 1.0), pk(V, 1.0), "adamw_small", tr=1024)]
    conv_g = lax.dynamic_slice_in_dim(s_out[0]["conv_w"], dev * cshard, cshard, axis=3)
    c_out = _adamw(conv_g.reshape(1, DEPTH * SSD_CONV, cshard), conv_w.reshape(-1, cshard), m_conv_w.reshape(-1, cshard),
                   v_conv_w.reshape(-1, cshard), "adamw_conv_w", tr=DEPTH * SSD_CONV)
    for i in range(4):
        s_out[i]["conv_w"] = c_out[i].reshape(conv_w.shape)

    layer_out = []
    for l in range(DEPTH):
        r_in, r_mi, r_rows = recv[l]
        lo = {"w_in": _adamw(r_in, w_in[l], m_w_in[l], v_w_in[l], f"adamw_w_in_{l}"),
              "w_mlp_in": _adamw(r_mi, w_mlp_in[l], m_w_mlp_in[l], v_w_mlp_in[l], f"adamw_w_mlp_in_{l}")}
        o = _adamw(r_rows, *[_pack_rows({n: d[n][l] for n in ROWS}, ()) for d in (W, M, V)], f"adamw_rows_{l}")
        r = 0
        for n in ROWS:
            per = shard[n][1]
            lo[n] = [a[r:r + per] for a in o]
            r += per
        layer_out.append(lo)
    b_out = {n: [jnp.stack([layer_out[l][n][i] for l in range(DEPTH)]) for i in range(4)] for n in BIG}

    outs = [loss, grad_x]
    for i in range(4):
        for n in names:
            outs.append(b_out[n][i] if n in BIG else s_out[i][n])
    return tuple(outs)
```

```python
import functools
import math

import numpy as np
import jax
import jax.numpy as jnp
from jax import lax
from jax.experimental import pallas as pl
from jax.experimental.pallas import tpu as pltpu

F32 = jnp.float32
BF16 = jnp.bfloat16

D_MODEL = 1024
DEPTH = 2
D_INNER = 2048
SSD_HEADS = 32
SSD_HEAD_DIM = 64
SSD_GROUPS = 8
SSD_REP = 4
SSD_STATE = 128
SSD_CONV = 5
CHUNK = 128
CONV_DIM = 4096
Q_HEADS = 16
KV_HEADS = 4
HEAD_DIM = 64
ATTN_WIDTH = 1024
KV_WIDTH = 256
WINDOW = 128
BLOCK = 128
N_BUCKETS = 32
MAX_DISTANCE = 128
D_FF = 4096
EPS = 1e-6
IN_SPLITS = (D_INNER, CONV_DIM, 2 * SSD_HEADS, ATTN_WIDTH, KV_WIDTH, KV_WIDTH, 2 * D_MODEL)
N_IN = sum(IN_SPLITS)
N_DEV = 8

ADAM_LR = 0.001
ADAM_B1 = 0.9
ADAM_B2 = 0.999
ADAM_EPS = 1e-08
ADAM_WD = 0.01
ADAM_STEP = 10

LANE = 128
HEAD_PAD = LANE
Z0 = 0
GT0 = 2048
Q0 = 4096
XBC0 = 6144
XS0 = 6144
B0 = 8192
C0 = 9216
K0 = 10240
V0 = 10752
DT0 = 11264
NP = 11520
QP = Q_HEADS * HEAD_PAD
KP = KV_HEADS * HEAD_PAD
NEG = -1e30
VMEM_LIMIT = 56 * 1024 * 1024


def _params(n_grid):
    return pltpu.CompilerParams(dimension_semantics=("arbitrary",) * n_grid, vmem_limit_bytes=VMEM_LIMIT)


def _dot(a, b, dims=((1,), (0,))):
    return lax.dot_general(a, b, (dims, ((), ())), preferred_element_type=F32)


def _mx(a):
    return a.astype(BF16)


def _split_dot(x, e):
    hi = x.astype(BF16)
    lo = (x - hi.astype(F32)).astype(BF16)
    return _dot(hi, e) + _dot(lo, e)


def _softplus(x):
    u = jnp.exp(-jnp.abs(x))
    w = 1.0 + u
    log1p = jnp.where(w == 1.0, u, jnp.log(w) * u / jnp.where(w == 1.0, 1.0, w - 1.0))
    return jnp.maximum(x, 0.0) + log1p


def _sigmoid(x):
    return 1.0 / (1.0 + jnp.exp(-x))


def _mm_body(*refs, grid, ta, tb, split_a, mode, comm):
    n_in = 3 if mode == "relu2_bwd" else 2
    n_out = 2 if mode == "relu2" else 1
    ins, cin, outs, cout, (acc_ref,), csem = _split_refs(refs, n_in, n_out, comm)
    a_ref, b_ref = ins[0], ins[1]
    i, j, k = pl.program_id(0), pl.program_id(1), pl.program_id(2)
    nk = grid[2]
    if comm is not None:
        comm.run(cin, cout, csem, (i * grid[1] + j) * nk + k, grid[0] * grid[1] * nk)

    @pl.when(k == 0)
    def _():
        acc_ref[...] = jnp.zeros_like(acc_ref)

    dims = ((0 if ta else 1,), (1 if tb else 0,))
    b = _mx(b_ref[...])
    if split_a:
        a32 = a_ref[...].astype(F32)
        hi = a32.astype(BF16)
        lo = (a32 - hi.astype(F32)).astype(BF16)
        acc_ref[...] += _dot(hi, b, dims) + _dot(lo, b, dims)
    else:
        acc_ref[...] += _dot(_mx(a_ref[...]), b, dims)

    @pl.when(k == nk - 1)
    def _():
        acc = acc_ref[...]
        if mode == "relu2":
            outs[0][...] = acc
            r = jnp.maximum(acc, 0.0)
            outs[1][...] = (r * r).astype(outs[1].dtype)
        elif mode == "relu2_bwd":
            outs[0][...] = (acc * 2.0 * jnp.maximum(ins[2][...], 0.0)).astype(outs[0].dtype)
        else:
            outs[0][...] = acc.astype(outs[0].dtype)


def _tile(n, pref):
    t = min(n, pref)
    while n % t:
        t -= LANE
    assert t > 0 and n % t == 0, (n, pref)
    return t


def _mm(a, b, name, ta=False, tb=False, out_dtype=F32, tm=1024, tn=1280, tk=1024, split_a=False, mode=None, extra=None,
        comm=None):
    m, k = (a.shape[1], a.shape[0]) if ta else a.shape
    n = b.shape[0] if tb else b.shape[1]
    assert (b.shape[1] if tb else b.shape[0]) == k
    tm, tn, tk = _tile(m, tm), _tile(n, tn), _tile(k, tk)
    grid = (m // tm, n // tn, k // tk)
    a_spec = pl.BlockSpec((tk, tm), lambda i, j, kk: (kk, i)) if ta else pl.BlockSpec((tm, tk), lambda i, j, kk: (i, kk))
    b_spec = pl.BlockSpec((tn, tk), lambda i, j, kk: (j, kk)) if tb else pl.BlockSpec((tk, tn), lambda i, j, kk: (kk, j))
    o_spec = pl.BlockSpec((tm, tn), lambda i, j, kk: (i, j))
    in_specs, args = [a_spec, b_spec], [a, b]
    out_specs, out_shape = [o_spec], [jax.ShapeDtypeStruct((m, n), out_dtype)]
    if mode == "relu2":
        out_specs, out_shape = [o_spec, o_spec], [jax.ShapeDtypeStruct((m, n), F32), jax.ShapeDtypeStruct((m, n), BF16)]
    elif mode == "relu2_bwd":
        in_specs, args = in_specs + [o_spec], args + [extra]
    scratch = [pltpu.VMEM((tm, tn), F32)]
    if comm is not None:
        in_specs, args = in_specs + [ANY] * comm.n, args + comm.arrs
        out_specs, out_shape = out_specs + [ANY] * comm.n, out_shape + comm.out_shapes()
        scratch = scratch + comm.sems()
    res = pl.pallas_call(
        functools.partial(_mm_body, grid=grid, ta=ta, tb=tb, split_a=split_a, mode=mode, comm=comm),
        name=name, grid=grid, in_specs=in_specs, out_specs=out_specs, out_shape=out_shape, scratch_shapes=scratch,
        compiler_params=_params(3),
    )(*args)
    return res[0] if len(res) == 1 else res


def _rms_fwd_body(*refs, has_res):
    if has_res:
        x_ref, g_ref, r_ref, o_ref = refs
    else:
        x_ref, g_ref, o_ref = refs
    x = x_ref[...]
    y = x * lax.rsqrt(jnp.mean(x * x, axis=-1, keepdims=True) + EPS) * g_ref[...]
    if has_res:
        y = r_ref[...] + y
    o_ref[...] = y.astype(o_ref.dtype)


def _rms_fwd(x, g, name, res=None, out_dtype=F32, tr=512):
    t, d = x.shape
    tr = min(tr, t)
    row = pl.BlockSpec((tr, d), lambda i: (i, 0))
    vec = pl.BlockSpec((1, d), lambda i: (0, 0))
    args = [x, g.reshape(1, d)] + ([res] if res is not None else [])
    return pl.pallas_call(
        functools.partial(_rms_fwd_body, has_res=res is not None),
        name=name,
        grid=(t // tr,),
        in_specs=[row, vec] + ([row] if res is not None else []),
        out_specs=row,
        out_shape=jax.ShapeDtypeStruct((t, d), out_dtype),
        compiler_params=_params(1),
    )(*args)


def _rms_bwd_body(*refs, has_add):
    if has_add:
        x_ref, g_ref, dy_ref, add_ref, dx_ref, dg_ref = refs
    else:
        x_ref, g_ref, dy_ref, dx_ref, dg_ref = refs

    @pl.when(pl.program_id(0) == 0)
    def _():
        dg_ref[...] = jnp.zeros_like(dg_ref)

    x = x_ref[...]
    dy = dy_ref[...].astype(F32)
    rstd = lax.rsqrt(jnp.mean(x * x, axis=-1, keepdims=True) + EPS)
    n = x * rstd
    dn = dy * g_ref[...]
    dx = rstd * (dn - n * jnp.mean(dn * n, axis=-1, keepdims=True))
    if has_add:
        dx = dx + add_ref[...]
    dx_ref[...] = dx.astype(dx_ref.dtype)
    dg_ref[...] += jnp.sum(dy * n, axis=0, keepdims=True)


def _rms_bwd(x, g, dy, name, add=None, out_dtype=F32, tr=512):
    t, d = x.shape
    tr = min(tr, t)
    row = pl.BlockSpec((tr, d), lambda i: (i, 0))
    vec = pl.BlockSpec((1, d), lambda i: (0, 0))
    args = [x, g.reshape(1, d), dy] + ([add] if add is not None else [])
    return pl.pallas_call(
        functools.partial(_rms_bwd_body, has_add=add is not None),
        name=name,
        grid=(t // tr,),
        in_specs=[row, vec, row] + ([row] if add is not None else []),
        out_specs=[row, vec],
        out_shape=[jax.ShapeDtypeStruct((t, d), out_dtype), jax.ShapeDtypeStruct((1, d), F32)],
        compiler_params=_params(1),
    )(*args)


def _loss_body(y_ref, t_ref, dy_ref, l_ref):
    @pl.when(pl.program_id(0) == 0)
    def _():
        l_ref[...] = jnp.zeros_like(l_ref)

    e = y_ref[...] - t_ref[...]
    dy_ref[...] = e * (1.0 / D_MODEL)
    s = jnp.sum(e * e, axis=0, keepdims=True)
    acc = s[:, 0:LANE]
    for j in range(1, D_MODEL // LANE):
        acc = acc + s[:, j * LANE:(j + 1) * LANE]
    l_ref[...] += acc


def _loss(y, target, tr=512):
    t, d = y.shape
    tr = min(tr, t)
    row = pl.BlockSpec((tr, d), lambda i: (i, 0))
    return pl.pallas_call(
        functools.partial(_loss_body), name="loss_head", grid=(t // tr,), in_specs=[row, row],
        out_specs=[row, pl.BlockSpec((1, LANE), lambda i: (0, 0))],
        out_shape=[jax.ShapeDtypeStruct((t, d), F32), jax.ShapeDtypeStruct((1, LANE), F32)],
        compiler_params=_params(1),
    )(y, target)


CONV_CT = 128
CONV_RB = 512
CONV_PAD = 8


def _conv_rows(t):
    return min(CONV_RB, t)


def _conv_fwd_body(u_ref, w_ref, b_ref, o_ref, pad_ref, *, t):
    rb = _conv_rows(t)
    zeros = jnp.zeros((CONV_PAD, CONV_CT), F32)
    pad_ref[pl.ds(0, CONV_PAD), :] = zeros
    pad_ref[pl.ds(t + CONV_PAD, CONV_PAD), :] = zeros
    pad_ref[pl.ds(CONV_PAD, t), :] = u_ref[...]
    w = w_ref[...]
    b = b_ref[...]
    for c in range(t // rb):
        base = CONV_PAD + c * rb
        acc = b + w[0:1, :] * pad_ref[pl.ds(base - 2, rb), :]
        for k in range(1, SSD_CONV):
            acc = acc + w[k:k + 1, :] * pad_ref[pl.ds(base + k - 2, rb), :]
        o_ref[pl.ds(c * rb, rb), :] = acc * _sigmoid(acc)


def _conv_fwd(proj, conv_w8, conv_b):
    t = proj.shape[0]
    off = XBC0 // CONV_CT
    return pl.pallas_call(
        functools.partial(_conv_fwd_body, t=t), name="conv_fwd", grid=(CONV_DIM // CONV_CT,),
        in_specs=[pl.BlockSpec((t, CONV_CT), lambda i: (0, off + i)), pl.BlockSpec((8, CONV_CT), lambda i: (0, i)),
                  pl.BlockSpec((1, CONV_CT), lambda i: (0, i))],
        out_specs=pl.BlockSpec((t, CONV_CT), lambda i: (0, i)),
        out_shape=jax.ShapeDtypeStruct((t, CONV_DIM), F32),
        scratch_shapes=[pltpu.VMEM((t + 2 * CONV_PAD, CONV_CT), F32)],
        compiler_params=_params(1),
    )(proj, conv_w8, conv_b)


def _conv_bwd_body(u_ref, da_ref, w_ref, b_ref, du_ref, dw_ref, pad_ref, pad2_ref, *, t):
    rb = _conv_rows(t)
    zeros = jnp.zeros((CONV_PAD, CONV_CT), F32)
    for p in (pad_ref, pad2_ref):
        p[pl.ds(0, CONV_PAD), :] = zeros
        p[pl.ds(t + CONV_PAD, CONV_PAD), :] = zeros
    pad_ref[pl.ds(CONV_PAD, t), :] = u_ref[...]
    w = w_ref[...]
    b = b_ref[...]
    dw = [jnp.zeros((1, CONV_CT), F32) for _ in range(SSD_CONV + 1)]
    for c in range(t // rb):
        base = CONV_PAD + c * rb
        us = [pad_ref[pl.ds(base + k - 2, rb), :] for k in range(SSD_CONV)]
        acc = b + w[0:1, :] * us[0]
        for k in range(1, SSD_CONV):
            acc = acc + w[k:k + 1, :] * us[k]
        sg = _sigmoid(acc)
        dyc = da_ref[pl.ds(c * rb, rb), :] * (sg * (1.0 + acc * (1.0 - sg)))
        pad2_ref[pl.ds(base, rb), :] = dyc
        for k in range(SSD_CONV):
            dw[k] = dw[k] + jnp.sum(dyc * us[k], axis=0, keepdims=True)
        dw[SSD_CONV] = dw[SSD_CONV] + jnp.sum(dyc, axis=0, keepdims=True)
    for c in range(t // rb):
        base = CONV_PAD + c * rb
        acc = w[0:1, :] * pad2_ref[pl.ds(base + 2, rb), :]
        for k in range(1, SSD_CONV):
            acc = acc + w[k:k + 1, :] * pad2_ref[pl.ds(base + 2 - k, rb), :]
        du_ref[pl.ds(c * rb, rb), :] = acc.astype(du_ref.dtype)
    dw_ref[...] = jnp.concatenate(dw + [jnp.zeros((2, CONV_CT), F32)], axis=0)


def _conv_bwd(proj, dact, conv_w8, conv_b):
    t = proj.shape[0]
    off = XBC0 // CONV_CT
    col = pl.BlockSpec((t, CONV_CT), lambda i: (0, i))
    return pl.pallas_call(
        functools.partial(_conv_bwd_body, t=t), name="conv_bwd", grid=(CONV_DIM // CONV_CT,),
        in_specs=[pl.BlockSpec((t, CONV_CT), lambda i: (0, off + i)), col, pl.BlockSpec((8, CONV_CT), lambda i: (0, i)),
                  pl.BlockSpec((1, CONV_CT), lambda i: (0, i))],
        out_specs=[col, pl.BlockSpec((8, CONV_CT), lambda i: (0, i))],
        out_shape=[jax.ShapeDtypeStruct((t, CONV_DIM), BF16), jax.ShapeDtypeStruct((8, CONV_DIM), F32)],
        scratch_shapes=[pltpu.VMEM((t + 2 * CONV_PAD, CONV_CT), F32), pltpu.VMEM((t + 2 * CONV_PAD, CONV_CT), F32)],
        compiler_params=_params(1),
    )(proj, dact, conv_w8, conv_b)


GW = SSD_REP * SSD_HEAD_DIM


def _expand_np(hoff):
    e = np.zeros((LANE, D_INNER), np.float32)
    for h in range(SSD_HEADS):
        e[hoff + h, h * SSD_HEAD_DIM:(h + 1) * SSD_HEAD_DIM] = 1.0
    return e


def _head_masks():
    lane = lax.broadcasted_iota(jnp.int32, (1, GW), 1)
    return [((lane >= r * SSD_HEAD_DIM) & (lane < (r + 1) * SSD_HEAD_DIM)).astype(F32) for r in range(SSD_REP)]


def _ssd_common(dtc_ref, dtr_ref, bc_ref, br_ref, alc_ref, alr_ref, e_ref, rev):
    L = CHUNK
    ri = lax.broadcasted_iota(jnp.int32, (L, L), 0)
    ci = lax.broadcasted_iota(jnp.int32, (L, L), 1)
    tri = (ri <= ci) if rev else (ri >= ci)
    trit = (ri >= ci) if rev else (ri <= ci)
    raw_c = dtc_ref[...] + bc_ref[...]
    dt_c = _softplus(raw_c)
    a_c = -jnp.exp(alc_ref[...])
    cs_c = jnp.dot(tri.astype(F32), dt_c * a_c, precision=lax.Precision.HIGHEST, preferred_element_type=F32)
    dt_r = _softplus(dtr_ref[...] + br_ref[...])
    a_r = -jnp.exp(alr_ref[...])
    cs_r = jnp.dot(dt_r * a_r, trit.astype(F32), precision=lax.Precision.HIGHEST, preferred_element_type=F32)
    il = 0 if rev else L - 1
    e_c = jnp.exp(cs_c)
    w_c = jnp.exp(cs_c[il:il + 1, :] - cs_c)
    ex = _split_dot(jnp.concatenate([dt_c, e_c, w_c], axis=0), e_ref[...])
    return dict(tri=tri, trit=trit, raw_c=raw_c, dt_c=dt_c, a_c=a_c, cs_c=cs_c, cs_r=cs_r, il=il, e_last=e_c[il:il + 1],
                dt_x=ex[0:L], e_x=ex[L:2 * L], w_x=ex[2 * L:3 * L], cd_x=ex[L + il:L + il + 1])


def _ssd_fwd_body(*refs, rev, hoff, nsteps, comm):
    L = CHUNK
    ins, cin, (y_ref, st_ref), cout, (s_scr,), csem = _split_refs(refs, 10, 2, comm)
    xs_ref, b_ref, c_ref, dtc_ref, dtr_ref, bc_ref, br_ref, alc_ref, alr_ref, e_ref = ins
    if comm is not None:
        comm.run(cin, cout, csem, pl.program_id(0), nsteps)

    @pl.when(pl.program_id(0) == 0)
    def _():
        s_scr[...] = jnp.zeros_like(s_scr)

    st_ref[0] = s_scr[...]
    q = _ssd_common(dtc_ref, dtr_ref, bc_ref, br_ref, alc_ref, alr_ref, e_ref, rev)
    masks = _head_masks()
    xdt = xs_ref[...] * q["dt_x"]
    xw = xdt * q["w_x"]
    for g in range(SSD_GROUPS):
        gc = slice(g * GW, (g + 1) * GW)
        nc = slice(g * SSD_STATE, (g + 1) * SSD_STATE)
        bg = _mx(b_ref[:, nc])
        cg = _mx(c_ref[:, nc])
        cb = _dot(cg, bg, ((1,), (1,)))
        sg = s_scr[:, gc]
        y = _dot(cg, _mx(sg)) * q["e_x"][:, gc]
        ms = []
        for r in range(SSD_REP):
            hh = hoff + g * SSD_REP + r
            diff = q["cs_c"][:, hh:hh + 1] - q["cs_r"][hh:hh + 1, :]
            ms.append(_mx(cb * jnp.exp(jnp.where(q["tri"], diff, NEG))))
        y4 = _dot(jnp.concatenate(ms, axis=0), _mx(xdt[:, gc]))
        for r in range(SSD_REP):
            y = y + y4[r * L:(r + 1) * L] * masks[r]
        y_ref[:, gc] = y
        s_scr[:, gc] = sg * q["cd_x"][:, gc] + _dot(bg, _mx(xw[:, gc]), ((0,), (0,)))


def _dt_rows_body(p_ref, o_ref):
    o_ref[...] = p_ref[...].T


def _dt_rows(proj, tr=512):
    t = proj.shape[0]
    tr = min(tr, t)
    return pl.pallas_call(
        functools.partial(_dt_rows_body), name="dt_rows", grid=(t // tr,),
        in_specs=[pl.BlockSpec((tr, LANE), lambda i: (i, DT0 // LANE))], out_specs=pl.BlockSpec((LANE, tr), lambda i: (0, i)),
        out_shape=jax.ShapeDtypeStruct((LANE, t), F32), compiler_params=_params(1),
    )(proj)


def _ssd_specs(t, rev):
    nc = t // CHUNK
    cm = (lambda i: nc - 1 - i) if rev else (lambda i: i)
    xs = pl.BlockSpec((CHUNK, D_INNER), lambda i: (cm(i), 0))
    bb = pl.BlockSpec((CHUNK, SSD_GROUPS * SSD_STATE), lambda i: (cm(i), 2))
    cc = pl.BlockSpec((CHUNK, SSD_GROUPS * SSD_STATE), lambda i: (cm(i), 3))
    dtc = pl.BlockSpec((CHUNK, LANE), lambda i: (cm(i), DT0 // LANE))
    dtr = pl.BlockSpec((LANE, CHUNK), lambda i: (0, cm(i)))
    colv = pl.BlockSpec((1, LANE), lambda i: (0, 0))
    rowv = pl.BlockSpec((LANE, 1), lambda i: (0, 0))
    return cm, xs, bb, cc, dtc, dtr, colv, rowv


def _ssd_fwd(xbc, proj, dtt, bias, alog, rev, comm=None):
    t = xbc.shape[0]
    nc = t // CHUNK
    cm, xs, bb, cc, dtc, dtr, colv, rowv = _ssd_specs(t, rev)
    hoff = SSD_HEADS if rev else 0
    e = jnp.asarray(_expand_np(hoff), BF16)
    in_specs = [xs, bb, cc, dtc, dtr, colv, rowv, colv, rowv, pl.BlockSpec((LANE, D_INNER), lambda i: (0, 0))]
    args = [xbc, xbc, xbc, proj, dtt, bias.reshape(1, LANE), bias.reshape(LANE, 1), alog.reshape(1, LANE), alog.reshape(LANE, 1), e]
    out_specs = [xs, pl.BlockSpec((1, SSD_STATE, D_INNER), lambda i: (cm(i), 0, 0))]
    out_shape = [jax.ShapeDtypeStruct((t, D_INNER), F32), jax.ShapeDtypeStruct((nc, SSD_STATE, D_INNER), F32)]
    scratch = [pltpu.VMEM((SSD_STATE, D_INNER), F32)]
    if comm is not None:
        in_specs, args = in_specs + [ANY] * comm.n, args + comm.arrs
        out_specs, out_shape = out_specs + [ANY] * comm.n, out_shape + comm.out_shapes()
        scratch = scratch + comm.sems()
    return pl.pallas_call(
        functools.partial(_ssd_fwd_body, rev=rev, hoff=hoff, nsteps=nc, comm=comm), name="ssd_fwd_rev" if rev else "ssd_fwd",
        grid=(nc,), in_specs=in_specs, out_specs=out_specs, out_shape=out_shape, scratch_shapes=scratch,
        compiler_params=_params(1),
    )(*args)


def _ssd_bwd_body(*refs, rev, hoff, first, nsteps, comm):
    L = CHUNK
    ins, cin, (dx_ref, ddt_ref, sm_ref), cout, (ds_scr,), csem = _split_refs(refs, 14 if first else 17, 3, comm)
    (xs_ref, b_ref, c_ref, dtc_ref, dtr_ref, bc_ref, br_ref, alc_ref, alr_ref, e_ref, et_ref, dy_ref, sp_ref,
     dsk_ref) = ins[:14]
    if not first:
        pdx_ref, pddt_ref, psm_ref = ins[14:]
    if comm is not None:
        comm.run(cin, cout, csem, pl.program_id(0), nsteps)

    @pl.when(pl.program_id(0) == 0)
    def _():
        ds_scr[...] = jnp.zeros_like(ds_scr)
        sm_ref[...] = jnp.zeros_like(sm_ref) if first else psm_ref[...]

    q = _ssd_common(dtc_ref, dtr_ref, bc_ref, br_ref, alc_ref, alr_ref, e_ref, rev)
    masks = _head_masks()
    xs = xs_ref[...]
    dy = dy_ref[...]
    xdt = xs * q["dt_x"]
    xw = xdt * q["w_x"]
    dye = dy * q["e_x"]
    ds_old = ds_scr[...]
    dxdt_parts, q1_parts, q2_parts = [], [], []
    ddiag = jnp.zeros((L, LANE), F32)
    for g in range(SSD_GROUPS):
        gc = slice(g * GW, (g + 1) * GW)
        nc = slice(g * SSD_STATE, (g + 1) * SSD_STATE)
        bg = _mx(b_ref[:, nc])
        cg = _mx(c_ref[:, nc])
        cb = _dot(cg, bg, ((1,), (1,)))
        cbt = _dot(bg, cg, ((1,), (1,)))
        dy_g = dy[:, gc]
        xdt_g = xdt[:, gc]
        dsg = ds_old[:, gc]
        dm = _dot(_mx(jnp.concatenate([dy_g * m for m in masks], axis=0)), _mx(xdt_g), ((1,), (1,)))
        dmt = _dot(_mx(jnp.concatenate([xdt_g * m for m in masks], axis=0)), _mx(dy_g), ((1,), (1,)))
        dcb = jnp.zeros((L, L), F32)
        dcbt = jnp.zeros((L, L), F32)
        mts = []
        for r in range(SSD_REP):
            hh = hoff + g * SSD_REP + r
            col = q["cs_c"][:, hh:hh + 1]
            row = q["cs_r"][hh:hh + 1, :]
            dec = jnp.exp(jnp.where(q["tri"], col - row, NEG))
            dect = jnp.exp(jnp.where(q["trit"], row - col, NEG))
            pd = dm[r * L:(r + 1) * L] * dec
            pdt = dmt[r * L:(r + 1) * L] * dect
            dcb = dcb + pd
            dcbt = dcbt + pdt
            mts.append(_mx(cbt * dect))
            ddiag = ddiag + (jnp.sum(pd * cb, axis=1, keepdims=True) - jnp.sum(pdt * cbt, axis=1, keepdims=True)) * _onehot_lane(hh)
        x4 = _dot(jnp.concatenate(mts, axis=0), _mx(dy_g))
        bds = _dot(bg, _mx(dsg))
        dxdt_g = bds * q["w_x"][:, gc]
        for r in range(SSD_REP):
            dxdt_g = dxdt_g + x4[r * L:(r + 1) * L] * masks[r]
        dxdt_parts.append(dxdt_g)
        spg = _mx(sp_ref[0, :, gc])
        q1_parts.append(dye[:, gc] * _dot(cg, spg))
        q2_parts.append(xw[:, gc] * bds)
        dc_g = _dot(_mx(dcb), bg) + _dot(_mx(dye[:, gc]), spg, ((1,), (1,)))
        db_g = _dot(_mx(dcbt), cg) + _dot(_mx(xw[:, gc]), _mx(dsg), ((1,), (1,)))
        boff = D_INNER + g * SSD_STATE
        coff = D_INNER + SSD_GROUPS * SSD_STATE + g * SSD_STATE
        if first:
            dx_ref[:, boff:boff + SSD_STATE] = db_g
            dx_ref[:, coff:coff + SSD_STATE] = dc_g
        else:
            dx_ref[:, boff:boff + SSD_STATE] = pdx_ref[:, boff:boff + SSD_STATE] + db_g
            dx_ref[:, coff:coff + SSD_STATE] = pdx_ref[:, coff:coff + SSD_STATE] + dc_g
        ds_scr[:, gc] = dsg * q["cd_x"][:, gc] + _dot(cg, _mx(dye[:, gc]), ((0,), (0,)))
    dxdt = jnp.concatenate(dxdt_parts, axis=1)
    et = et_ref[...]
    hs = _dot(_mx(jnp.concatenate([jnp.concatenate(q1_parts, axis=1), jnp.concatenate(q2_parts, axis=1), dxdt * xs], axis=0)), et)
    q1, q2, r3 = hs[0:L], hs[L:2 * L], hs[2 * L:3 * L]
    t_prev = _split_dot(jnp.sum(ds_old * sp_ref[0], axis=0, keepdims=True), et) * q["e_last"]
    rows = lax.broadcasted_iota(jnp.int32, (L, LANE), 0)
    dcs = ddiag + q1 - q2 + jnp.where(rows == q["il"], jnp.sum(q2, axis=0, keepdims=True) + t_prev, 0.0)
    dad = jnp.dot(q["trit"].astype(F32), dcs, precision=lax.Precision.HIGHEST, preferred_element_type=F32)
    ddt = dad * q["a_c"] + r3
    ddt_raw = ddt * _sigmoid(q["raw_c"])
    dal = jnp.sum(dad * q["dt_c"], axis=0, keepdims=True) * q["a_c"]
    dbias = jnp.sum(ddt_raw, axis=0, keepdims=True)
    sm_ref[...] += jnp.concatenate([dal, dbias, jnp.zeros((6, LANE), F32)], axis=0)
    dxs = dxdt * q["dt_x"]
    if first:
        dx_ref[:, 0:D_INNER] = dxs + dy * dsk_ref[...]
        ddt_ref[...] = ddt_raw
    else:
        dx_ref[:, 0:D_INNER] = pdx_ref[:, 0:D_INNER] + dxs
        ddt_ref[...] = pddt_ref[...] + ddt_raw


def _ssd_bwd(xbc, proj, dtt, bias, alog, dy, states, dskip_x, rev, prev=None, comm=None):
    t = xbc.shape[0]
    nc = t // CHUNK
    first = prev is None
    cm, xs, bb, cc, dtc, dtr, colv, rowv = _ssd_specs(t, not rev)
    hoff = SSD_HEADS if rev else 0
    e_np = _expand_np(hoff)
    e = jnp.asarray(e_np, BF16)
    et = jnp.asarray(e_np.T, BF16)
    st_spec = pl.BlockSpec((1, SSD_STATE, D_INNER), lambda i: (cm(i), 0, 0))
    dxo = pl.BlockSpec((CHUNK, CONV_DIM), lambda i: (cm(i), 0))
    ddto = pl.BlockSpec((CHUNK, LANE), lambda i: (cm(i), 0))
    smo = pl.BlockSpec((8, LANE), lambda i: (0, 0))
    in_specs = [xs, bb, cc, dtc, dtr, colv, rowv, colv, rowv, pl.BlockSpec((LANE, D_INNER), lambda i: (0, 0)),
                pl.BlockSpec((D_INNER, LANE), lambda i: (0, 0)), xs, st_spec, pl.BlockSpec((1, D_INNER), lambda i: (0, 0))]
    args = [xbc, xbc, xbc, proj, dtt, bias.reshape(1, LANE), bias.reshape(LANE, 1), alog.reshape(1, LANE),
            alog.reshape(LANE, 1), e, et, dy, states, dskip_x]
    if not first:
        in_specs += [dxo, ddto, smo]
        args += list(prev)
    out_specs = [dxo, ddto, smo]
    out_shape = [jax.ShapeDtypeStruct((t, CONV_DIM), F32), jax.ShapeDtypeStruct((t, LANE), F32),
                 jax.ShapeDtypeStruct((8, LANE), F32)]
    scratch = [pltpu.VMEM((SSD_STATE, D_INNER), F32)]
    if comm is not None:
        in_specs, args = in_specs + [ANY] * comm.n, args + comm.arrs
        out_specs, out_shape = out_specs + [ANY] * comm.n, out_shape + comm.out_shapes()
        scratch = scratch + comm.sems()
    return pl.pallas_call(
        functools.partial(_ssd_bwd_body, rev=rev, hoff=hoff, first=first, nsteps=nc, comm=comm),
        name="ssd_bwd_rev" if rev else "ssd_bwd", grid=(nc,), in_specs=in_specs, out_specs=out_specs, out_shape=out_shape,
        scratch_shapes=scratch, compiler_params=_params(1),
    )(*args)


def _gnorm_parts(yf_ref, yb_ref, xs_ref, z_ref, dsk_ref):
    xs = xs_ref[...]
    y = yf_ref[...] + yb_ref[...] + xs * dsk_ref[...]
    z = z_ref[...]
    sg = _sigmoid(z)
    s = z * sg
    v = y * s
    rs = []
    for g in range(SSD_GROUPS):
        vg = v[:, g * GW:(g + 1) * GW]
        rs.append(jnp.broadcast_to(lax.rsqrt(jnp.mean(vg * vg, axis=-1, keepdims=True) + EPS), vg.shape))
    return xs, y, z, sg, s, v, jnp.concatenate(rs, axis=1)


def _gnorm_fwd_body(yf_ref, yb_ref, xs_ref, z_ref, dsk_ref, w_ref, o_ref):
    _, _, _, _, _, v, rstd = _gnorm_parts(yf_ref, yb_ref, xs_ref, z_ref, dsk_ref)
    o_ref[...] = (v * rstd * w_ref[...]).astype(o_ref.dtype)


def _gnorm_fwd(yf, yb, xbc, proj, dskip_x, norm_w, tr=256):
    t = yf.shape[0]
    tr = min(tr, t)
    row = pl.BlockSpec((tr, D_INNER), lambda i: (i, 0))
    vec = pl.BlockSpec((1, D_INNER), lambda i: (0, 0))
    return pl.pallas_call(
        functools.partial(_gnorm_fwd_body), name="gnorm_fwd", grid=(t // tr,),
        in_specs=[row, row, row, pl.BlockSpec((tr, D_INNER), lambda i: (i, Z0 // D_INNER)), vec, vec],
        out_specs=row, out_shape=jax.ShapeDtypeStruct((t, D_INNER), BF16), compiler_params=_params(1),
    )(yf, yb, xbc, proj, dskip_x, norm_w.reshape(1, D_INNER))


def _gnorm_bwd_body(du_ref, yf_ref, yb_ref, xs_ref, z_ref, dsk_ref, w_ref, et_ref, dy_ref, dz_ref, dw_ref, dd_ref, acc_ref, *, nsteps):
    i = pl.program_id(0)

    @pl.when(i == 0)
    def _():
        dw_ref[...] = jnp.zeros_like(dw_ref)
        acc_ref[...] = jnp.zeros_like(acc_ref)

    xs, y, z, sg, s, v, rstd = _gnorm_parts(yf_ref, yb_ref, xs_ref, z_ref, dsk_ref)
    du = du_ref[...]
    n = v * rstd
    dn = du * w_ref[...]
    dw_ref[...] += jnp.sum(du * n, axis=0, keepdims=True)
    prod = dn * n
    means = []
    for g in range(SSD_GROUPS):
        pg = prod[:, g * GW:(g + 1) * GW]
        means.append(jnp.broadcast_to(jnp.mean(pg, axis=-1, keepdims=True), pg.shape))
    dv = rstd * (dn - n * jnp.concatenate(means, axis=1))
    dy = dv * s
    dy_ref[...] = dy
    dz_ref[...] = (dv * y * (sg * (1.0 + z * (1.0 - sg)))).astype(dz_ref.dtype)
    acc_ref[...] += jnp.sum(dy * xs, axis=0, keepdims=True)

    @pl.when(i == nsteps - 1)
    def _():
        dd_ref[...] = _split_dot(acc_ref[...], et_ref[...])


def _gnorm_bwd(du, yf, yb, xbc, proj, dskip_x, norm_w, tr=256):
    t = yf.shape[0]
    tr = min(tr, t)
    row = pl.BlockSpec((tr, D_INNER), lambda i: (i, 0))
    vec = pl.BlockSpec((1, D_INNER), lambda i: (0, 0))
    et = jnp.asarray(_expand_np(0).T, BF16)
    return pl.pallas_call(
        functools.partial(_gnorm_bwd_body, nsteps=t // tr), name="gnorm_bwd", grid=(t // tr,),
        in_specs=[row, row, row, row, pl.BlockSpec((tr, D_INNER), lambda i: (i, Z0 // D_INNER)), vec, vec,
                  pl.BlockSpec((D_INNER, LANE), lambda i: (0, 0))],
        out_specs=[row, row, vec, pl.BlockSpec((1, LANE), lambda i: (0, 0))],
        out_shape=[jax.ShapeDtypeStruct((t, D_INNER), F32), jax.ShapeDtypeStruct((t, D_INNER), BF16),
                   jax.ShapeDtypeStruct((1, D_INNER), F32), jax.ShapeDtypeStruct((1, LANE), F32)],
        scratch_shapes=[pltpu.VMEM((1, D_INNER), F32)],
        compiler_params=_params(1),
    )(du, yf, yb, xbc, proj, dskip_x, norm_w.reshape(1, D_INNER), et)


def _gate_fwd_body(gt_ref, b_ref, ys_ref, ya_ref, o_ref):
    g = _sigmoid(gt_ref[...] + b_ref[...])
    o_ref[...] = (g[:, 0:D_MODEL] * ys_ref[...] + g[:, D_MODEL:] * ya_ref[...]).astype(o_ref.dtype)


def _gate_fwd(proj, b_gate, y_ssd, y_attn, tr=512):
    t = y_ssd.shape[0]
    tr = min(tr, t)
    row = pl.BlockSpec((tr, D_MODEL), lambda i: (i, 0))
    return pl.pallas_call(
        functools.partial(_gate_fwd_body), name="gate_fwd", grid=(t // tr,),
        in_specs=[pl.BlockSpec((tr, 2 * D_MODEL), lambda i: (i, GT0 // (2 * D_MODEL))),
                  pl.BlockSpec((1, 2 * D_MODEL), lambda i: (0, 0)), row, row],
        out_specs=row, out_shape=jax.ShapeDtypeStruct((t, D_MODEL), BF16), compiler_params=_params(1),
    )(proj, b_gate.reshape(1, 2 * D_MODEL), y_ssd, y_attn)


def _gate_bwd_body(dm_ref, gt_ref, b_ref, ys_ref, ya_ref, dys_ref, dya_ref, dgt_ref, db_ref):
    @pl.when(pl.program_id(0) == 0)
    def _():
        db_ref[...] = jnp.zeros_like(db_ref)

    g = _sigmoid(gt_ref[...] + b_ref[...])
    gs = g[:, 0:D_MODEL]
    ga = g[:, D_MODEL:]
    dm = dm_ref[...]
    dys_ref[...] = (dm * gs).astype(dys_ref.dtype)
    dya_ref[...] = (dm * ga).astype(dya_ref.dtype)
    dgt = jnp.concatenate([dm * ys_ref[...] * gs * (1.0 - gs), dm * ya_ref[...] * ga * (1.0 - ga)], axis=1)
    dgt_ref[...] = dgt.astype(dgt_ref.dtype)
    db_ref[...] += jnp.sum(dgt, axis=0, keepdims=True)


def _gate_bwd(dmix, proj, b_gate, y_ssd, y_attn, tr=512):
    t = y_ssd.shape[0]
    tr = min(tr, t)
    row = pl.BlockSpec((tr, D_MODEL), lambda i: (i, 0))
    row2 = pl.BlockSpec((tr, 2 * D_MODEL), lambda i: (i, 0))
    vec2 = pl.BlockSpec((1, 2 * D_MODEL), lambda i: (0, 0))
    return pl.pallas_call(
        functools.partial(_gate_bwd_body), name="gate_bwd", grid=(t // tr,),
        in_specs=[row, pl.BlockSpec((tr, 2 * D_MODEL), lambda i: (i, GT0 // (2 * D_MODEL))), vec2, row, row],
        out_specs=[row, row, row2, vec2],
        out_shape=[jax.ShapeDtypeStruct((t, D_MODEL), BF16), jax.ShapeDtypeStruct((t, D_MODEL), BF16),
                   jax.ShapeDtypeStruct((t, 2 * D_MODEL), BF16), jax.ShapeDtypeStruct((1, 2 * D_MODEL), F32)],
        compiler_params=_params(1),
    )(dmix, proj, b_gate.reshape(1, 2 * D_MODEL), y_ssd, y_attn)


def _bucket_np():
    i = np.arange(BLOCK)[:, None]
    j = np.arange(3 * BLOCK)[None, :]
    rel = j - BLOCK - i
    nb = N_BUCKETS // 2
    max_exact = nb // 2
    ret = np.where(rel > 0, nb, 0)
    n = np.abs(rel)
    nf = np.maximum(n, 1).astype(np.float32)
    large = max_exact + (np.log(nf / np.float32(max_exact)) / np.float32(math.log(MAX_DISTANCE / max_exact))
                         * np.float32(nb - max_exact)).astype(np.int32)
    large = np.minimum(large, nb - 1)
    bucket = ret + np.where(n < max_exact, n, large)
    band = np.abs(rel) <= WINDOW
    return bucket, band


def _onehot_np():
    bucket, _ = _bucket_np()
    oh = np.zeros((BLOCK * 3 * BLOCK, LANE), np.float32)
    oh[np.arange(oh.shape[0]), bucket.reshape(-1)] = 1.0
    return oh


REP = Q_HEADS // KV_HEADS


def _group_cols(ref, g):
    return jnp.concatenate([ref[:, (g * REP + r) * HEAD_PAD:(g * REP + r + 1) * HEAD_PAD] for r in range(REP)], axis=0)


def _group_lane(ref, g, rows):
    return jnp.concatenate([jnp.broadcast_to(ref[:, g * REP + r:g * REP + r + 1], (rows, 1)) for r in range(REP)], axis=0)


def _attn_kv(refs, g):
    gc = slice(g * HEAD_PAD, (g + 1) * HEAD_PAD)
    return _mx(jnp.concatenate([r[:, gc] for r in refs], axis=0))


def _attn_logits(q_ref, kg, bias_ref, edge, g):
    q4 = _mx(_group_cols(q_ref, g))
    s = _dot(q4, kg, ((1,), (1,))) * (HEAD_DIM ** -0.5) + bias_ref[g] + edge
    return q4, s


def _attn_edge(n, nb):
    col = lax.broadcasted_iota(jnp.int32, (1, 3 * BLOCK), 1)
    bad = ((col < BLOCK) & (n == 0)) | ((col >= 2 * BLOCK) & (n == nb - 1))
    return jnp.where(bad, NEG, 0.0)


def _onehot_lane(h):
    return (lax.broadcasted_iota(jnp.int32, (1, LANE), 1) == h).astype(F32)


def _attn_fwd_body(q_ref, kp_ref, kc_ref, kn_ref, vp_ref, vc_ref, vn_ref, bias_ref, sink_ref, o_ref, lse_ref, *, nb):
    n = pl.program_id(0)
    edge = _attn_edge(n, nb)
    lse = jnp.zeros((BLOCK, LANE), F32)
    for g in range(KV_HEADS):
        kg = _attn_kv((kp_ref, kc_ref, kn_ref), g)
        vg = _attn_kv((vp_ref, vc_ref, vn_ref), g)
        _, s = _attn_logits(q_ref, kg, bias_ref, edge, g)
        sink = _group_lane(sink_ref, g, BLOCK)
        m = jnp.maximum(jnp.max(s, axis=-1, keepdims=True), sink)
        p = jnp.exp(s - m)
        den = jnp.sum(p, axis=-1, keepdims=True) + jnp.exp(sink - m)
        o4 = _dot(_mx(p * (1.0 / den)), vg)
        l4 = m + jnp.log(den)
        for r in range(REP):
            h = g * REP + r
            o_ref[:, h * HEAD_PAD:(h + 1) * HEAD_PAD] = o4[r * BLOCK:(r + 1) * BLOCK].astype(o_ref.dtype)
            lse = lse + l4[r * BLOCK:(r + 1) * BLOCK] * _onehot_lane(h)
    lse_ref[...] = lse


def _attn_specs(t, clamp):
    nb = t // BLOCK
    cur = (lambda n: jnp.minimum(n, nb - 1)) if clamp else (lambda n: n)
    prv = lambda n: jnp.maximum(cur(n) - 1, 0)
    nxt = lambda n: jnp.minimum(cur(n) + 1, nb - 1)
    kb, vb = K0 // KP, V0 // KP
    qs = pl.BlockSpec((BLOCK, QP), lambda n: (cur(n), Q0 // QP))
    ks = [pl.BlockSpec((BLOCK, KP), lambda n, f=f: (f(n), kb)) for f in (prv, cur, nxt)]
    vs = [pl.BlockSpec((BLOCK, KP), lambda n, f=f: (f(n), vb)) for f in (prv, cur, nxt)]
    consts = [pl.BlockSpec((KV_HEADS, REP * BLOCK, 3 * BLOCK), lambda n: (0, 0, 0)), pl.BlockSpec((1, LANE), lambda n: (0, 0))]
    return nb, cur, qs, ks, vs, consts


def _attn_fwd(proj, bias, sink):
    t = proj.shape[0]
    nb, cur, qs, ks, vs, consts = _attn_specs(t, False)
    return pl.pallas_call(
        functools.partial(_attn_fwd_body, nb=nb), name="attn_fwd", grid=(nb,),
        in_specs=[qs] + ks + vs + consts,
        out_specs=[pl.BlockSpec((BLOCK, QP), lambda n: (n, 0)), pl.BlockSpec((BLOCK, LANE), lambda n: (n, 0))],
        out_shape=[jax.ShapeDtypeStruct((t, QP), BF16), jax.ShapeDtypeStruct((t, LANE), F32)],
        compiler_params=_params(1),
    )(proj, proj, proj, proj, proj, proj, proj, bias, sink)


def _attn_bwd_body(q_ref, kp_ref, kc_ref, kn_ref, vp_ref, vc_ref, vn_ref, bias_ref, sink_ref, do_ref, o_ref, lse_ref,
                   dq_ref, dk_ref, dv_ref, dbias_ref, dsink_ref, ak, bk, av, bv, *, nb):
    n = pl.program_id(0)
    scale = HEAD_DIM ** -0.5

    @pl.when(n == 0)
    def _():
        for r in (ak, bk, av, bv, dbias_ref, dsink_ref):
            r[...] = jnp.zeros_like(r)

    @pl.when(n < nb)
    def _():
        edge = _attn_edge(n, nb)
        dsink = jnp.zeros((1, LANE), F32)
        for g in range(KV_HEADS):
            gc = slice(g * HEAD_PAD, (g + 1) * HEAD_PAD)
            kg = _attn_kv((kp_ref, kc_ref, kn_ref), g)
            vg = _attn_kv((vp_ref, vc_ref, vn_ref), g)
            q4, s = _attn_logits(q_ref, kg, bias_ref, edge, g)
            lse = _group_lane(lse_ref, g, BLOCK)
            p = jnp.exp(s - lse)
            do = _group_cols(do_ref, g)
            dp = _dot(_mx(do), vg, ((1,), (1,)))
            delta = jnp.sum(do * _group_cols(o_ref, g).astype(F32), axis=-1, keepdims=True)
            ds = p * (dp - delta)
            sunk = jnp.exp(_group_lane(sink_ref, g, BLOCK) - lse) * delta
            dbias_ref[g] += ds
            dsb = _mx(ds)
            dq4 = _dot(dsb, kg) * scale
            for r in range(REP):
                h = g * REP + r
                dq_ref[:, h * HEAD_PAD:(h + 1) * HEAD_PAD] = dq4[r * BLOCK:(r + 1) * BLOCK].astype(dq_ref.dtype)
                dsink = dsink - jnp.sum(sunk[r * BLOCK:(r + 1) * BLOCK], axis=0, keepdims=True) * _onehot_lane(h)
            dk_c = _dot(dsb, q4, ((0,), (0,))) * scale
            dv_c = _dot(_mx(p), _mx(do), ((0,), (0,)))
            for acc_a, acc_b, out, c in ((ak, bk, dk_ref, dk_c), (av, bv, dv_ref, dv_c)):
                out[:, gc] = (acc_a[:, gc] + c[0:BLOCK]).astype(out.dtype)
                acc_a[:, gc] = acc_b[:, gc] + c[BLOCK:2 * BLOCK]
                acc_b[:, gc] = c[2 * BLOCK:3 * BLOCK]
        dsink_ref[...] += dsink

    @pl.when(n == nb)
    def _():
        dk_ref[...] = ak[...].astype(dk_ref.dtype)
        dv_ref[...] = av[...].astype(dv_ref.dtype)


def _attn_bwd(proj, bias, sink, do, o, lse):
    t = proj.shape[0]
    nb, cur, qs, ks, vs, consts = _attn_specs(t, True)
    rowq = pl.BlockSpec((BLOCK, QP), lambda n: (cur(n), 0))
    late = pl.BlockSpec((BLOCK, KP), lambda n: (jnp.maximum(n - 1, 0), 0))
    return pl.pallas_call(
        functools.partial(_attn_bwd_body, nb=nb), name="attn_bwd", grid=(nb + 1,),
        in_specs=[qs] + ks + vs + consts + [rowq, rowq, pl.BlockSpec((BLOCK, LANE), lambda n: (cur(n), 0))],
        out_specs=[rowq, late, late, pl.BlockSpec((KV_HEADS, REP * BLOCK, 3 * BLOCK), lambda n: (0, 0, 0)),
                   pl.BlockSpec((1, LANE), lambda n: (0, 0))],
        out_shape=[jax.ShapeDtypeStruct((t, QP), BF16), jax.ShapeDtypeStruct((t, KP), BF16), jax.ShapeDtypeStruct((t, KP), BF16),
                   jax.ShapeDtypeStruct((KV_HEADS, REP * BLOCK, 3 * BLOCK), F32), jax.ShapeDtypeStruct((1, LANE), F32)],
        scratch_shapes=[pltpu.VMEM((BLOCK, KP), F32) for _ in range(4)],
        compiler_params=_params(1),
    )(proj, proj, proj, proj, proj, proj, proj, bias, sink, do, o, lse)


def _adamw_body(gp_ref, w_ref, m_ref, v_ref, g_ref, d_ref, nm_ref, nv_ref, *, nparts):
    g = gp_ref[0].astype(F32)
    for j in range(1, nparts):
        g = g + gp_ref[j].astype(F32)
    m = ADAM_B1 * m_ref[...] + (1.0 - ADAM_B1) * g
    v = ADAM_B2 * v_ref[...] + (1.0 - ADAM_B2) * (g * g)
    m_hat = m / (1.0 - ADAM_B1 ** ADAM_STEP)
    v_hat = v / (1.0 - ADAM_B2 ** ADAM_STEP)
    g_ref[...] = g
    d_ref[...] = -ADAM_LR * (m_hat / (jnp.sqrt(v_hat) + ADAM_EPS) + ADAM_WD * w_ref[...])
    nm_ref[...] = m
    nv_ref[...] = v


def _adamw(gparts, w, m, v, name, tr=256):
    p, r, c = gparts.shape
    tr = min(tr, r)
    assert r % tr == 0
    row = pl.BlockSpec((tr, c), lambda i: (i, 0))
    sd = jax.ShapeDtypeStruct((r, c), F32)
    return pl.pallas_call(
        functools.partial(_adamw_body, nparts=p), name=name, grid=(r // tr,),
        in_specs=[pl.BlockSpec((p, tr, c), lambda i: (0, i, 0)), row, row, row],
        out_specs=[row, row, row, row], out_shape=[sd, sd, sd, sd], compiler_params=_params(1),
    )(gparts, w, m, v)


MESH = pl.DeviceIdType.MESH
N_REL = N_DEV - 1


def _gather_phases(ins, outs, sems):
    n = len(ins)
    send_sems, recv_sems, local_sems = sems
    x, y, c = lax.axis_index("x"), lax.axis_index("y"), lax.axis_index("c")
    me, sibling = (x, y, c), (x, y, 1 - c)
    chips = [(1 - x, y), (x, 1 - y), (1 - x, 1 - y)]

    def slot(a, p):
        return outs[a].at[4 * p[0] + 2 * p[1] + p[2]]

    def copy(a, k, block, to, src=None):
        return pltpu.make_async_remote_copy(
            src_ref=slot(a, block) if src is None else src, dst_ref=slot(a, block),
            send_sem=send_sems.at[a * N_REL + k], recv_sem=recv_sems.at[a * N_REL + k], device_id=to, device_id_type=MESH)

    mine = [pltpu.make_async_copy(ins[a], slot(a, me), local_sems.at[a]) for a in range(n)]
    first = []
    for a in range(n):
        first.append(copy(a, 0, me, sibling, src=ins[a]))
        first += [copy(a, 1 + j, me, (*chip, c), src=ins[a]) for j, chip in enumerate(chips)]
    passed = [copy(a, 4 + j, (*chip, c), sibling) for j, chip in enumerate(chips) for a in range(n)]

    def start():
        for cp in mine + first:
            cp.start()

    def pass_on():
        i = 0
        for j, chip in enumerate(chips):
            for a in range(n):
                copy(a, 1 + j, (*chip, c), me).wait_recv()
                passed[i].start()
                i += 1

    def finish():
        for a in range(n):
            copy(a, 0, sibling, me).wait_recv()
            for j, chip in enumerate(chips):
                copy(a, 4 + j, (*chip, 1 - c), me).wait_recv()
        for cp in first + passed:
            cp.wait_send()
        for cp in mine:
            cp.wait()

    return start, pass_on, finish


def _exchange_phases(ins, outs, sems):
    n = len(ins)
    send_sems, recv_sems, local_sems = sems
    x, y, c = lax.axis_index("x"), lax.axis_index("y"), lax.axis_index("c")
    me = 4 * x + 2 * y + c
    mine = [pltpu.make_async_copy(ins[a].at[me], outs[a].at[me], local_sems.at[a]) for a in range(n)]
    copies = []
    for a in range(n):
        for k in range(1, N_DEV):
            px = 1 - x if k & 4 else x
            py = 1 - y if k & 2 else y
            pc = 1 - c if k & 1 else c
            peer = 4 * px + 2 * py + pc
            send = pltpu.make_async_remote_copy(
                src_ref=ins[a].at[peer], dst_ref=outs[a].at[me], send_sem=send_sems.at[a * N_REL + k - 1],
                recv_sem=recv_sems.at[a * N_REL + k - 1], device_id=(px, py, pc), device_id_type=MESH)
            recv = pltpu.make_async_remote_copy(
                src_ref=ins[a].at[me], dst_ref=outs[a].at[peer], send_sem=send_sems.at[a * N_REL + k - 1],
                recv_sem=recv_sems.at[a * N_REL + k - 1], device_id=(px, py, pc), device_id_type=MESH)
            copies.append((send, recv))

    def start():
        for cp in mine:
            cp.start()
        for send, _ in copies:
            send.start()

    def finish():
        for send, recv in copies:
            send.wait_send()
            recv.wait_recv()
        for cp in mine:
            cp.wait()

    return start, None, finish


class _Comm:
    def __init__(self, kind, arrs):
        self.kind, self.arrs, self.n = kind, list(arrs), len(arrs)

    def out_shapes(self):
        if self.kind == "gather":
            return [jax.ShapeDtypeStruct((N_DEV,) + a.shape, a.dtype) for a in self.arrs]
        return [jax.ShapeDtypeStruct(a.shape, a.dtype) for a in self.arrs]

    def sems(self):
        return [pltpu.SemaphoreType.DMA((self.n * N_REL,)), pltpu.SemaphoreType.DMA((self.n * N_REL,)),
                pltpu.SemaphoreType.DMA((self.n,))]

    def run(self, ins, outs, sems, step, nsteps):
        start, mid, finish = (_gather_phases if self.kind == "gather" else _exchange_phases)(ins, outs, sems)
        if nsteps is None:
            start()
            if mid is not None:
                mid()
            finish()
            return
        pl.when(step == 0)(start)
        if mid is not None:
            pl.when(step == nsteps // 2)(mid)
        pl.when(step == nsteps - 1)(finish)


def _split_refs(refs, n_in, n_out, comm):
    k = comm.n if comm is not None else 0
    i = 0
    parts = []
    for cnt in (n_in, k, n_out, k):
        parts.append(refs[i:i + cnt])
        i += cnt
    rest = refs[i:]
    if comm is None:
        return parts[0], (), parts[2], (), rest, ()
    return parts[0], parts[1], parts[2], parts[3], rest[:len(rest) - 3], rest[len(rest) - 3:]


def _comm_body(*refs, comm):
    n = comm.n
    comm.run(refs[:n], refs[n:2 * n], refs[2 * n:], None, None)


def _communicate(comm, name, vmem=False):
    spec = pl.BlockSpec(memory_space=pltpu.VMEM if vmem else pl.ANY)
    return pl.pallas_call(
        functools.partial(_comm_body, comm=comm), name=name,
        in_specs=[spec] * comm.n, out_specs=[spec] * comm.n, out_shape=comm.out_shapes(), scratch_shapes=comm.sems(),
        compiler_params=pltpu.CompilerParams(vmem_limit_bytes=VMEM_LIMIT),
    )(*comm.arrs)


ANY = pl.BlockSpec(memory_space=pl.ANY)


def _pad_heads(w, axis):
    shp = w.shape
    heads = shp[axis] // HEAD_DIM
    w = w.reshape(shp[:axis] + (heads, HEAD_DIM) + shp[axis + 1:])
    pad = [(0, 0)] * w.ndim
    pad[axis + 1] = (0, HEAD_PAD - HEAD_DIM)
    w = jnp.pad(w, pad)
    return w.reshape(shp[:axis] + (heads * HEAD_PAD,) + shp[axis + 1:])


def _unpad_heads(w, axis):
    shp = w.shape
    heads = shp[axis] // HEAD_PAD
    w = w.reshape(shp[:axis] + (heads, HEAD_PAD) + shp[axis + 1:])
    w = lax.slice_in_dim(w, 0, HEAD_DIM, axis=axis + 1)
    return w.reshape(shp[:axis] + (heads * HEAD_DIM,) + shp[axis + 1:])


def _w_in_to_padded(w):
    idx = np.cumsum((0,) + IN_SPLITS)
    z, xbc, dt, q, k, v, gates = [w[:, idx[i]:idx[i + 1]] for i in range(7)]
    zeros = lambda n: jnp.zeros((w.shape[0], n), w.dtype)
    return jnp.concatenate([z, gates, _pad_heads(q, 1), xbc, _pad_heads(k, 1), _pad_heads(v, 1), dt, zeros(LANE - 2 * SSD_HEADS),
                            zeros(NP - DT0 - LANE)], axis=1)


def _w_in_from_padded(w):
    z = w[:, Z0:Z0 + D_INNER]
    gates = w[:, GT0:GT0 + 2 * D_MODEL]
    q = _unpad_heads(w[:, Q0:Q0 + QP], 1)
    xbc = w[:, XBC0:XBC0 + CONV_DIM]
    k = _unpad_heads(w[:, K0:K0 + KP], 1)
    v = _unpad_heads(w[:, V0:V0 + KP], 1)
    dt = w[:, DT0:DT0 + 2 * SSD_HEADS]
    return jnp.concatenate([z, xbc, dt, q, k, v, gates], axis=1)


def _pad_lane(v):
    v = v.reshape(-1)
    return jnp.pad(v, (0, LANE - v.shape[0]))


def _layer_consts(p):
    c = dict(p)
    c["conv_w8"] = jnp.pad(p["conv_w"].reshape(SSD_CONV, CONV_DIM), ((0, 8 - SSD_CONV), (0, 0)))
    c["conv_b1"] = p["conv_b"].reshape(1, CONV_DIM)
    c["dtb"] = _pad_lane(p["dt_bias"])
    c["alog"] = _pad_lane(p["a_log"])
    c["dskip_x"] = jnp.repeat(p["d_skip"], SSD_HEAD_DIM).reshape(1, D_INNER)
    c["sink"] = _pad_lane(p["attn_sink"]).reshape(1, LANE)
    return c


def _hosted(res, n_own, hook):
    res = list(res)
    if hook is not None:
        hook[1](res[n_own:])
    return res[0] if n_own == 1 else res[:n_own]


def _layer_fwd(x, c, w, bias, l, hooks):
    comm = lambda k: hooks[k][0] if k in hooks else None
    s = {"x": x}
    s["h"] = _rms_fwd(x, c["pre_mix_norm"], f"pre_mix_norm_{l}", out_dtype=BF16)
    res = _mm(s["h"], w["w_in"], f"in_proj_{l}", comm=comm("in_proj"))
    proj = _hosted(res, 1, hooks.get("in_proj")) if "in_proj" in hooks else res
    s["proj"] = proj
    s["xbc"] = _conv_fwd(proj, c["conv_w8"], c["conv_b1"])
    s["dtt"] = _dt_rows(proj)
    s["yf"], s["stf"] = _hosted(_ssd_fwd(s["xbc"], proj, s["dtt"], c["dtb"], c["alog"], False, comm=comm("ssd_fwd")), 2,
                                hooks.get("ssd_fwd"))
    s["yb"], s["stb"] = _hosted(_ssd_fwd(s["xbc"], proj, s["dtt"], c["dtb"], c["alog"], True, comm=comm("ssd_fwd_rev")), 2,
                                hooks.get("ssd_fwd_rev"))
    s["u"] = _gnorm_fwd(s["yf"], s["yb"], s["xbc"], proj, c["dskip_x"], c["ssd_norm"])
    s["y_ssd"] = _mm(s["u"], w["w_ssd_out"], f"ssd_out_{l}")
    s["o"], s["lse"] = _attn_fwd(proj, bias, c["sink"])
    s["y_attn"] = _mm(s["o"], w["w_attn_out"], f"attn_out_{l}")
    s["mix"] = _gate_fwd(proj, c["b_gate"], s["y_ssd"], s["y_attn"])
    s["mixed"] = _mm(s["mix"], w["w_o"], f"w_o_{l}")
    s["x1"] = _rms_fwd(s["mixed"], c["post_mix_norm"], f"post_mix_norm_{l}", res=x)
    s["h2"] = _rms_fwd(s["x1"], c["pre_mlp_norm"], f"pre_mlp_norm_{l}", out_dtype=BF16)
    s["f1"], s["a"] = _mm(s["h2"], w["w_mlp_in"], f"mlp_in_{l}", mode="relu2")
    s["f"] = _mm(s["a"], w["w_mlp_out"], f"mlp_out_{l}")
    x2 = _rms_fwd(s["f"], c["post_mlp_norm"], f"post_mlp_norm_{l}", res=s["x1"])
    return x2, s


def _send_rest(gw):
    s_mi = gw["w_mlp_in"].reshape(D_MODEL, N_DEV, -1).transpose(1, 0, 2).astype(BF16)
    parts = [(_unpad_heads(gw[n], 0) if n == "w_attn_out" else gw[n]).reshape(N_DEV, -1, D_MODEL) for n in ROWS]
    return [s_mi, jnp.concatenate(parts, axis=1).astype(BF16)]


def _send_w_in(g):
    return [_w_in_from_padded(g).reshape(D_MODEL, N_DEV, -1).transpose(1, 0, 2).astype(BF16)]


def _layer_bwd(dx2, s, c, w, bias, l, earlier=None, host_own=False):
    gw, gs, got = {}, {}, {}
    df, gs["post_mlp_norm"] = _rms_bwd(s["f"], c["post_mlp_norm"], dx2, f"post_mlp_norm_bwd_{l}", out_dtype=BF16)
    df1 = _mm(df, w["w_mlp_out"], f"mlp_out_dx_{l}", tb=True, mode="relu2_bwd", extra=s["f1"], out_dtype=BF16)
    gw["w_mlp_out"] = _mm(s["a"], df, f"mlp_out_dw_{l}", ta=True)
    gw["w_mlp_in"] = _mm(s["h2"], df1, f"mlp_in_dw_{l}", ta=True)
    dh2 = _mm(df1, w["w_mlp_in"], f"mlp_in_dx_{l}", tb=True)
    dx1, gs["pre_mlp_norm"] = _rms_bwd(s["x1"], c["pre_mlp_norm"], dh2, f"pre_mlp_norm_bwd_{l}", add=dx2)
    dmixed, gs["post_mix_norm"] = _rms_bwd(s["mixed"], c["post_mix_norm"], dx1, f"post_mix_norm_bwd_{l}", out_dtype=BF16)
    gw["w_o"] = _mm(s["mix"], dmixed, f"w_o_dw_{l}", ta=True)
    dmix = _mm(dmixed, w["w_o"], f"w_o_dx_{l}", tb=True)
    dys, dya, dgt, gs["b_gate"] = _gate_bwd(dmix, s["proj"], c["b_gate"], s["y_ssd"], s["y_attn"])
    gw["w_ssd_out"] = _mm(s["u"], dys, f"ssd_out_dw_{l}", ta=True)
    du = _mm(dys, w["w_ssd_out"], f"ssd_out_dx_{l}", tb=True)
    gw["w_attn_out"] = _mm(s["o"], dya, f"attn_out_dw_{l}", ta=True)
    do = _mm(dya, w["w_attn_out"], f"attn_out_dx_{l}", tb=True)
    dq, dk, dv, dbias, dsink = _attn_bwd(s["proj"], bias, c["sink"], do, s["o"], s["lse"])
    gs["attn_sink"] = dsink
    dy, dz, gs["ssd_norm"], gs["d_skip"] = _gnorm_bwd(du, s["yf"], s["yb"], s["xbc"], s["proj"], c["dskip_x"], c["ssd_norm"])
    res = _ssd_bwd(s["xbc"], s["proj"], s["dtt"], c["dtb"], c["alog"], dy, s["stf"], c["dskip_x"], False,
                   comm=None if earlier is None else _Comm("exchange", earlier))
    part, got["earlier"] = res[:3], res[3:]
    res = _ssd_bwd(s["xbc"], s["proj"], s["dtt"], c["dtb"], c["alog"], dy, s["stb"], c["dskip_x"], True, prev=part,
                   comm=_Comm("exchange", _send_rest(gw)) if host_own else None)
    (dxbc, ddt, sm), got["rest"] = res[:3], res[3:]
    gs["a_log"], gs["dt_bias"] = sm[0:1], sm[1:2]
    dxbc_raw, gs["conv"] = _conv_bwd(s["proj"], dxbc, c["conv_w8"], c["conv_b1"])
    t = dx2.shape[0]
    dproj = jnp.concatenate([dz, dgt, dq, dxbc_raw, dk, dv, ddt.astype(BF16), jnp.zeros((t, NP - DT0 - LANE), BF16)], axis=1)
    gw["w_in"] = _mm(s["h"], dproj, f"in_proj_dw_{l}", ta=True)
    if host_own:
        dh, *got["w_in"] = _mm(dproj, w["w_in"], f"in_proj_dx_{l}", tb=True, comm=_Comm("exchange", _send_w_in(gw["w_in"])))
    else:
        dh = _mm(dproj, w["w_in"], f"in_proj_dx_{l}", tb=True)
    dx, gs["pre_mix_norm"] = _rms_bwd(s["x"], c["pre_mix_norm"], dh, f"pre_mix_norm_bwd_{l}", add=dx1)
    return dx, gw, gs, dbias, got


SMALL = (("pre_mix_norm", DEPTH * D_MODEL), ("b_gate", DEPTH * 2 * D_MODEL), ("conv_w", DEPTH * SSD_CONV * CONV_DIM),
         ("conv_b", DEPTH * CONV_DIM), ("dt_bias", DEPTH * 2 * SSD_HEADS), ("a_log", DEPTH * 2 * SSD_HEADS),
         ("d_skip", DEPTH * SSD_HEADS), ("ssd_norm", DEPTH * D_INNER), ("attn_sink", DEPTH * Q_HEADS),
         ("rel_bias_table", N_BUCKETS * Q_HEADS), ("post_mix_norm", DEPTH * D_MODEL), ("pre_mlp_norm", DEPTH * D_MODEL),
         ("post_mlp_norm", DEPTH * D_MODEL))


def _small_rows(n):
    return -(-n // LANE)


def _pack_small(vals, fill=0.0):
    rows = []
    for name, n in SMALL:
        v = vals[name].reshape(-1).astype(F32)
        rows.append(jnp.pad(v, (0, _small_rows(n) * LANE - n), constant_values=fill).reshape(-1, LANE))
    out = jnp.concatenate(rows, axis=0)
    return jnp.pad(out, ((0, -out.shape[0] % 8), (0, 0)), constant_values=fill)


def _unpack_small(packed, shapes):
    out, r = {}, 0
    for name, n in SMALL:
        nr = _small_rows(n)
        out[name] = packed[r:r + nr].reshape(-1)[:n].reshape(shapes[name])
        r += nr
    return out


BIG = ("w_in", "w_ssd_out", "w_attn_out", "w_o", "w_mlp_in", "w_mlp_out")
ROWS = ("w_ssd_out", "w_attn_out", "w_o", "w_mlp_out")


def _pack_rows(vals, lead):
    return jnp.concatenate([vals[n].reshape(lead + (-1, D_MODEL)) for n in ROWS], axis=len(lead))


def kernel(x, pre_mix_norm, w_in, b_gate, conv_w, conv_b, dt_bias, a_log, d_skip, ssd_norm, w_ssd_out, attn_sink, rel_bias_table, w_attn_out, w_o, post_mix_norm, pre_mlp_norm, w_mlp_in, w_mlp_out, post_mlp_norm, loss_target, m_pre_mix_norm, m_w_in, m_b_gate, m_conv_w, m_conv_b, m_dt_bias, m_a_log, m_d_skip, m_ssd_norm, m_w_ssd_out, m_attn_sink, m_rel_bias_table, m_w_attn_out, m_w_o, m_post_mix_norm, m_pre_mlp_norm, m_w_mlp_in, m_w_mlp_out, m_post_mlp_norm, v_pre_mix_norm, v_w_in, v_b_gate, v_conv_w, v_conv_b, v_dt_bias, v_a_log, v_d_skip, v_ssd_norm, v_w_ssd_out, v_attn_sink, v_rel_bias_table, v_w_attn_out, v_w_o, v_post_mix_norm, v_pre_mlp_norm, v_w_mlp_in, v_w_mlp_out, v_post_mlp_norm):
    names = ("pre_mix_norm", "w_in", "b_gate", "conv_w", "conv_b", "dt_bias", "a_log", "d_skip", "ssd_norm", "w_ssd_out",
             "attn_sink", "rel_bias_table", "w_attn_out", "w_o", "post_mix_norm", "pre_mlp_norm", "w_mlp_in", "w_mlp_out",
             "post_mlp_norm")
    W = dict(zip(names, (pre_mix_norm, w_in, b_gate, conv_w, conv_b, dt_bias, a_log, d_skip, ssd_norm, w_ssd_out, attn_sink,
                         rel_bias_table, w_attn_out, w_o, post_mix_norm, pre_mlp_norm, w_mlp_in, w_mlp_out, post_mlp_norm)))
    M = dict(zip(names, (m_pre_mix_norm, m_w_in, m_b_gate, m_conv_w, m_conv_b, m_dt_bias, m_a_log, m_d_skip, m_ssd_norm,
                         m_w_ssd_out, m_attn_sink, m_rel_bias_table, m_w_attn_out, m_w_o, m_post_mix_norm, m_pre_mlp_norm,
                         m_w_mlp_in, m_w_mlp_out, m_post_mlp_norm)))
    V = dict(zip(names, (v_pre_mix_norm, v_w_in, v_b_gate, v_conv_w, v_conv_b, v_dt_bias, v_a_log, v_d_skip, v_ssd_norm,
                         v_w_ssd_out, v_attn_sink, v_rel_bias_table, v_w_attn_out, v_w_o, v_post_mix_norm, v_pre_mlp_norm,
                         v_w_mlp_in, v_w_mlp_out, v_post_mlp_norm)))
    t = x.shape[1]
    shard = {n: W[n].shape for n in names}
    dev = 4 * lax.axis_index("x") + 2 * lax.axis_index("y") + lax.axis_index("c")
    cshard = CONV_DIM // N_DEV

    sh_in = [w_in[l].astype(BF16) for l in range(DEPTH)]
    sh_rest = [[w_mlp_in[l].astype(BF16), _pack_rows({n: W[n][l].astype(BF16) for n in ROWS}, ())] for l in range(DEPTH)]
    full = [{} for _ in range(DEPTH)]

    def take_w_in(l):
        def take(got):
            full[l]["w_in"] = _w_in_to_padded(got[0].transpose(1, 0, 2).reshape(D_MODEL, N_IN))
        return take

    def take_rest(l):
        def take(got):
            full[l]["w_mlp_in"] = got[0].transpose(1, 0, 2).reshape(D_MODEL, D_FF)
            r = 0
            for n in ROWS:
                per = shard[n][1]
                full[l][n] = got[1][:, r:r + per].reshape(N_DEV * per, D_MODEL)
                r += per
            full[l]["w_attn_out"] = _pad_heads(full[l]["w_attn_out"], 0)
        return take

    take_w_in(0)(_communicate(_Comm("gather", [sh_in[0]]), "gather_w_in_0"))
    (g_conv,) = _communicate(_Comm("gather", [conv_w.reshape(-1, LANE)]), "gather_conv_w", vmem=True)
    conv_full = g_conv.reshape(N_DEV, DEPTH, SSD_CONV, cshard).transpose(1, 2, 0, 3).reshape(DEPTH, SSD_CONV, CONV_DIM)
    hooks = [{"in_proj": (_Comm("gather", sh_rest[0]), take_rest(0)),
              "ssd_fwd": (_Comm("gather", [sh_in[1]]), take_w_in(1)),
              "ssd_fwd_rev": (_Comm("gather", sh_rest[1]), take_rest(1))}, {}]

    _, band = _bucket_np()
    onehot = jnp.asarray(_onehot_np(), BF16)
    table_t = jnp.pad(rel_bias_table.T, ((0, 0), (0, LANE - N_BUCKETS)))
    bias = _mm(table_t, onehot, "t5_bias", tb=True, tm=Q_HEADS, tn=3 * BLOCK * BLOCK // 8, split_a=True)
    bias = (bias.reshape(Q_HEADS, BLOCK, 3 * BLOCK) + jnp.asarray(np.where(band, 0.0, NEG), F32)).reshape(
        KV_HEADS, REP * BLOCK, 3 * BLOCK)

    consts = []
    for l in range(DEPTH):
        p = {n: W[n][l] for n in names if n not in BIG and n not in ("rel_bias_table", "conv_w")}
        p["conv_w"] = conv_full[l]
        consts.append(_layer_consts(p))
    h = x[0]
    saved = []
    for l in range(DEPTH):
        h, s = _layer_fwd(h, consts[l], full[l], bias, l, hooks[l])
        saved.append(s)
    dy, lsum = _loss(h, loss_target[0])
    loss = lax.psum(0.5 / D_MODEL * jnp.sum(lsum), ("x", "y", "c"))

    gss, dbs = [None] * DEPTH, [None] * DEPTH
    dy, gw1, gss[1], dbs[1], _ = _layer_bwd(dy, saved[1], consts[1], full[1], bias, 1)
    dy, _, gss[0], dbs[0], got = _layer_bwd(dy, saved[0], consts[0], full[0], bias, 0,
                                            earlier=_send_w_in(gw1["w_in"]) + _send_rest(gw1), host_own=True)
    recv = [list(got["w_in"]) + list(got["rest"]), list(got["earlier"])]
    grad_x = dy[None]
    dbias = jnp.concatenate([d.reshape(Q_HEADS, -1) for d in dbs], axis=1)
    d_table = _mm(dbias, jnp.concatenate([onehot] * DEPTH, axis=0), "t5_bias_bwd", tm=Q_HEADS, tk=3 * BLOCK * BLOCK // 8,
                  split_a=True)

    sg = {}
    for n in ("pre_mix_norm", "b_gate", "ssd_norm", "post_mix_norm", "pre_mlp_norm", "post_mlp_norm"):
        sg[n] = jnp.stack([gss[l][n].reshape(-1) for l in range(DEPTH)])
    sg["conv_w"] = jnp.stack([gss[l]["conv"][0:SSD_CONV] for l in range(DEPTH)])
    sg["conv_b"] = jnp.stack([gss[l]["conv"][SSD_CONV] for l in range(DEPTH)])
    sg["dt_bias"] = jnp.stack([gss[l]["dt_bias"][0, 0:2 * SSD_HEADS] for l in range(DEPTH)])
    sg["a_log"] = jnp.stack([gss[l]["a_log"][0, 0:2 * SSD_HEADS] for l in range(DEPTH)])
    sg["d_skip"] = jnp.stack([gss[l]["d_skip"][0, 0:SSD_HEADS] for l in range(DEPTH)])
    sg["attn_sink"] = jnp.stack([gss[l]["attn_sink"][0, 0:Q_HEADS] for l in range(DEPTH)])
    sg["rel_bias_table"] = d_table[:, 0:N_BUCKETS].T
    (small_parts,) = _communicate(_Comm("gather", [_pack_small(sg)]), "gather_small_grads", vmem=True)
    sshape = {n: W[n].shape for n, _ in SMALL}
    sshape["conv_w"] = (DEPTH, SSD_CONV, 1, CONV_DIM)
    pk = lambda d, fill: _pack_small({n: (jnp.full((DEPTH, SSD_CONV, CONV_DIM), fill, F32) if n == "conv_w" else d[n])
                                      for n, _ in SMALL}, fill)
    s_out = [_unpack_small(o, sshape) for o in _adamw(small_parts, pk(W, 1.0), pk(M, 1.0), pk(V, 1.0), "adamw_small", tr=1024)]
    conv_g = lax.dynamic_slice_in_dim(s_out[0]["conv_w"], dev * cshard, cshard, axis=3)
    c_out = _adamw(conv_g.reshape(1, DEPTH * SSD_CONV, cshard), conv_w.reshape(-1, cshard), m_conv_w.reshape(-1, cshard),
                   v_conv_w.reshape(-1, cshard), "adamw_conv_w", tr=DEPTH * SSD_CONV)
    for i in range(4):
        s_out[i]["conv_w"] = c_out[i].reshape(conv_w.shape)

    layer_out = []
    for l in range(DEPTH):
        r_in, r_mi, r_rows = recv[l]
        lo = {"w_in": _adamw(r_in, w_in[l], m_w_in[l], v_w_in[l], f"adamw_w_in_{l}"),
              "w_mlp_in": _adamw(r_mi, w_mlp_in[l], m_w_mlp_in[l], v_w_mlp_in[l], f"adamw_w_mlp_in_{l}")}
        o = _adamw(r_rows, *[_pack_rows({n: d[n][l] for n in ROWS}, ()) for d in (W, M, V)], f"adamw_rows_{l}")
        r = 0
        for n in ROWS:
            per = shard[n][1]
            lo[n] = [a[r:r + per] for a in o]
            r += per
        layer_out.append(lo)
    b_out = {n: [jnp.stack([layer_out[l][n][i] for l in range(DEPTH)]) for i in range(4)] for n in BIG}

    outs = [loss, grad_x]
    for i in range(4):
        for n in names:
            outs.append(b_out[n][i] if n in BIG else s_out[i][n])
    return tuple(outs)
```

```python
import functools
import math

import numpy as np
import jax
import jax.numpy as jnp
from jax import lax
from jax.experimental import pallas as pl
from jax.experimental.pallas import tpu as pltpu

F32 = jnp.float32
BF16 = jnp.bfloat16

D_MODEL = 1024
DEPTH = 2
D_INNER = 2048
SSD_HEADS = 32
SSD_HEAD_DIM = 64
SSD_GROUPS = 8
SSD_REP = 4
SSD_STATE = 128
SSD_CONV = 5
CHUNK = 128
CONV_DIM = 4096
Q_HEADS = 16
KV_HEADS = 4
HEAD_DIM = 64
ATTN_WIDTH = 1024
KV_WIDTH = 256
WINDOW = 128
BLOCK = 128
N_BUCKETS = 32
MAX_DISTANCE = 128
D_FF = 4096
EPS = 1e-6
IN_SPLITS = (D_INNER, CONV_DIM, 2 * SSD_HEADS, ATTN_WIDTH, KV_WIDTH, KV_WIDTH, 2 * D_MODEL)
N_IN = sum(IN_SPLITS)
N_DEV = 8

ADAM_LR = 0.001
ADAM_B1 = 0.9
ADAM_B2 = 0.999
ADAM_EPS = 1e-08
ADAM_WD = 0.01
ADAM_STEP = 10

LANE = 128
HEAD_PAD = LANE
Z0 = 0
GT0 = 2048
Q0 = 4096
XBC0 = 6144
XS0 = 6144
B0 = 8192
C0 = 9216
K0 = 10240
V0 = 10752
DT0 = 11264
NP = 11520
QP = Q_HEADS * HEAD_PAD
KP = KV_HEADS * HEAD_PAD
NEG = -1e30
VMEM_LIMIT = 56 * 1024 * 1024


def _params(n_grid):
    return pltpu.CompilerParams(dimension_semantics=("arbitrary",) * n_grid, vmem_limit_bytes=VMEM_LIMIT)


def _dot(a, b, dims=((1,), (0,))):
    return lax.dot_general(a, b, (dims, ((), ())), preferred_element_type=F32)


def _mx(a):
    return a.astype(BF16)


def _split_dot(x, e):
    hi = x.astype(BF16)
    lo = (x - hi.astype(F32)).astype(BF16)
    return _dot(hi, e) + _dot(lo, e)


def _softplus(x):
    u = jnp.exp(-jnp.abs(x))
    w = 1.0 + u
    log1p = jnp.where(w == 1.0, u, jnp.log(w) * u / jnp.where(w == 1.0, 1.0, w - 1.0))
    return jnp.maximum(x, 0.0) + log1p


def _sigmoid(x):
    return 1.0 / (1.0 + jnp.exp(-x))


def _mm_body(*refs, grid, ta, tb, split_a, mode, comm):
    n_in = 3 if mode == "relu2_bwd" else 2
    n_out = 2 if mode == "relu2" else 1
    ins, cin, outs, cout, (acc_ref,), csem = _split_refs(refs, n_in, n_out, comm)
    a_ref, b_ref = ins[0], ins[1]
    i, j, k = pl.program_id(0), pl.program_id(1), pl.program_id(2)
    nk = grid[2]
    if comm is not None:
        comm.run(cin, cout, csem, (i * grid[1] + j) * nk + k, grid[0] * grid[1] * nk)

    @pl.when(k == 0)
    def _():
        acc_ref[...] = jnp.zeros_like(acc_ref)

    dims = ((0 if ta else 1,), (1 if tb else 0,))
    b = _mx(b_ref[...])
    if split_a:
        a32 = a_ref[...].astype(F32)
        hi = a32.astype(BF16)
        lo = (a32 - hi.astype(F32)).astype(BF16)
        acc_ref[...] += _dot(hi, b, dims) + _dot(lo, b, dims)
    else:
        acc_ref[...] += _dot(_mx(a_ref[...]), b, dims)

    @pl.when(k == nk - 1)
    def _():
        acc = acc_ref[...]
        if mode == "relu2":
            outs[0][...] = acc
            r = jnp.maximum(acc, 0.0)
            outs[1][...] = (r * r).astype(outs[1].dtype)
        elif mode == "relu2_bwd":
            outs[0][...] = (acc * 2.0 * jnp.maximum(ins[2][...], 0.0)).astype(outs[0].dtype)
        else:
            outs[0][...] = acc.astype(outs[0].dtype)


def _tile(n, pref):
    t = min(n, pref)
    while n % t:
        t -= LANE
    assert t > 0 and n % t == 0, (n, pref)
    return t


def _mm(a, b, name, ta=False, tb=False, out_dtype=F32, tm=1024, tn=1280, tk=1024, split_a=False, mode=None, extra=None,
        comm=None):
    m, k = (a.shape[1], a.shape[0]) if ta else a.shape
    n = b.shape[0] if tb else b.shape[1]
    assert (b.shape[1] if tb else b.shape[0]) == k
    tm, tn, tk = _tile(m, tm), _tile(n, tn), _tile(k, tk)
    grid = (m // tm, n // tn, k // tk)
    a_spec = pl.BlockSpec((tk, tm), lambda i, j, kk: (kk, i)) if ta else pl.BlockSpec((tm, tk), lambda i, j, kk: (i, kk))
    b_spec = pl.BlockSpec((tn, tk), lambda i, j, kk: (j, kk)) if tb else pl.BlockSpec((tk, tn), lambda i, j, kk: (kk, j))
    o_spec = pl.BlockSpec((tm, tn), lambda i, j, kk: (i, j))
    in_specs, args = [a_spec, b_spec], [a, b]
    out_specs, out_shape = [o_spec], [jax.ShapeDtypeStruct((m, n), out_dtype)]
    if mode == "relu2":
        out_specs, out_shape = [o_spec, o_spec], [jax.ShapeDtypeStruct((m, n), F32), jax.ShapeDtypeStruct((m, n), BF16)]
    elif mode == "relu2_bwd":
        in_specs, args = in_specs + [o_spec], args + [extra]
    scratch = [pltpu.VMEM((tm, tn), F32)]
    if comm is not None:
        in_specs, args = in_specs + [ANY] * comm.n, args + comm.arrs
        out_specs, out_shape = out_specs + [ANY] * comm.n, out_shape + comm.out_shapes()
        scratch = scratch + comm.sems()
    res = pl.pallas_call(
        functools.partial(_mm_body, grid=grid, ta=ta, tb=tb, split_a=split_a, mode=mode, comm=comm),
        name=name, grid=grid, in_specs=in_specs, out_specs=out_specs, out_shape=out_shape, scratch_shapes=scratch,
        compiler_params=_params(3),
    )(*args)
    return res[0] if len(res) == 1 else res


def _rms_fwd_body(*refs, has_res):
    if has_res:
        x_ref, g_ref, r_ref, o_ref = refs
    else:
        x_ref, g_ref, o_ref = refs
    x = x_ref[...]
    y = x * lax.rsqrt(jnp.mean(x * x, axis=-1, keepdims=True) + EPS) * g_ref[...]
    if has_res:
        y = r_ref[...] + y
    o_ref[...] = y.astype(o_ref.dtype)


def _rms_fwd(x, g, name, res=None, out_dtype=F32, tr=512):
    t, d = x.shape
    tr = min(tr, t)
    row = pl.BlockSpec((tr, d), lambda i: (i, 0))
    vec = pl.BlockSpec((1, d), lambda i: (0, 0))
    args = [x, g.reshape(1, d)] + ([res] if res is not None else [])
    return pl.pallas_call(
        functools.partial(_rms_fwd_body, has_res=res is not None),
        name=name,
        grid=(t // tr,),
        in_specs=[row, vec] + ([row] if res is not None else []),
        out_specs=row,
        out_shape=jax.ShapeDtypeStruct((t, d), out_dtype),
        compiler_params=_params(1),
    )(*args)


def _rms_bwd_body(*refs, has_add):
    if has_add:
        x_ref, g_ref, dy_ref, add_ref, dx_ref, dg_ref = refs
    else:
        x_ref, g_ref, dy_ref, dx_ref, dg_ref = refs

    @pl.when(pl.program_id(0) == 0)
    def _():
        dg_ref[...] = jnp.zeros_like(dg_ref)

    x = x_ref[...]
    dy = dy_ref[...].astype(F32)
    rstd = lax.rsqrt(jnp.mean(x * x, axis=-1, keepdims=True) + EPS)
    n = x * rstd
    dn = dy * g_ref[...]
    dx = rstd * (dn - n * jnp.mean(dn * n, axis=-1, keepdims=True))
    if has_add:
        dx = dx + add_ref[...]
    dx_ref[...] = dx.astype(dx_ref.dtype)
    dg_ref[...] += jnp.sum(dy * n, axis=0, keepdims=True)


def _rms_bwd(x, g, dy, name, add=None, out_dtype=F32, tr=512):
    t, d = x.shape
    tr = min(tr, t)
    row = pl.BlockSpec((tr, d), lambda i: (i, 0))
    vec = pl.BlockSpec((1, d), lambda i: (0, 0))
    args = [x, g.reshape(1, d), dy] + ([add] if add is not None else [])
    return pl.pallas_call(
        functools.partial(_rms_bwd_body, has_add=add is not None),
        name=name,
        grid=(t // tr,),
        in_specs=[row, vec, row] + ([row] if add is not None else []),
        out_specs=[row, vec],
        out_shape=[jax.ShapeDtypeStruct((t, d), out_dtype), jax.ShapeDtypeStruct((1, d), F32)],
        compiler_params=_params(1),
    )(*args)


def _loss_body(y_ref, t_ref, dy_ref, l_ref):
    @pl.when(pl.program_id(0) == 0)
    def _():
        l_ref[...] = jnp.zeros_like(l_ref)

    e = y_ref[...] - t_ref[...]
    dy_ref[...] = e * (1.0 / D_MODEL)
    s = jnp.sum(e * e, axis=0, keepdims=True)
    acc = s[:, 0:LANE]
    for j in range(1, D_MODEL // LANE):
        acc = acc + s[:, j * LANE:(j + 1) * LANE]
    l_ref[...] += acc


def _loss(y, target, tr=512):
    t, d = y.shape
    tr = min(tr, t)
    row = pl.BlockSpec((tr, d), lambda i: (i, 0))
    return pl.pallas_call(
        functools.partial(_loss_body), name="loss_head", grid=(t // tr,), in_specs=[row, row],
        out_specs=[row, pl.BlockSpec((1, LANE), lambda i: (0, 0))],
        out_shape=[jax.ShapeDtypeStruct((t, d), F32), jax.ShapeDtypeStruct((1, LANE), F32)],
        compiler_params=_params(1),
    )(y, target)


CONV_CT = 128
CONV_RB = 512
CONV_PAD = 8


def _conv_rows(t):
    return min(CONV_RB, t)


def _conv_fwd_body(u_ref, w_ref, b_ref, o_ref, pad_ref, *, t):
    rb = _conv_rows(t)
    zeros = jnp.zeros((CONV_PAD, CONV_CT), F32)
    pad_ref[pl.ds(0, CONV_PAD), :] = zeros
    pad_ref[pl.ds(t + CONV_PAD, CONV_PAD), :] = zeros
    pad_ref[pl.ds(CONV_PAD, t), :] = u_ref[...]
    w = w_ref[...]
    b = b_ref[...]
    for c in range(t // rb):
        base = CONV_PAD + c * rb
        acc = b + w[0:1, :] * pad_ref[pl.ds(base - 2, rb), :]
        for k in range(1, SSD_CONV):
            acc = acc + w[k:k + 1, :] * pad_ref[pl.ds(base + k - 2, rb), :]
        o_ref[pl.ds(c * rb, rb), :] = acc * _sigmoid(acc)


def _conv_fwd(proj, conv_w8, conv_b):
    t = proj.shape[0]
    off = XBC0 // CONV_CT
    return pl.pallas_call(
        functools.partial(_conv_fwd_body, t=t), name="conv_fwd", grid=(CONV_DIM // CONV_CT,),
        in_specs=[pl.BlockSpec((t, CONV_CT), lambda i: (0, off + i)), pl.BlockSpec((8, CONV_CT), lambda i: (0, i)),
                  pl.BlockSpec((1, CONV_CT), lambda i: (0, i))],
        out_specs=pl.BlockSpec((t, CONV_CT), lambda i: (0, i)),
        out_shape=jax.ShapeDtypeStruct((t, CONV_DIM), F32),
        scratch_shapes=[pltpu.VMEM((t + 2 * CONV_PAD, CONV_CT), F32)],
        compiler_params=_params(1),
    )(proj, conv_w8, conv_b)


def _conv_bwd_body(u_ref, da_ref, w_ref, b_ref, du_ref, dw_ref, pad_ref, pad2_ref, *, t):
    rb = _conv_rows(t)
    zeros = jnp.zeros((CONV_PAD, CONV_CT), F32)
    for p in (pad_ref, pad2_ref):
        p[pl.ds(0, CONV_PAD), :] = zeros
        p[pl.ds(t + CONV_PAD, CONV_PAD), :] = zeros
    pad_ref[pl.ds(CONV_PAD, t), :] = u_ref[...]
    w = w_ref[...]
    b = b_ref[...]
    dw = [jnp.zeros((1, CONV_CT), F32) for _ in range(SSD_CONV + 1)]
    for c in range(t // rb):
        base = CONV_PAD + c * rb
        us = [pad_ref[pl.ds(base + k - 2, rb), :] for k in range(SSD_CONV)]
        acc = b + w[0:1, :] * us[0]
        for k in range(1, SSD_CONV):
            acc = acc + w[k:k + 1, :] * us[k]
        sg = _sigmoid(acc)
        dyc = da_ref[pl.ds(c * rb, rb), :] * (sg * (1.0 + acc * (1.0 - sg)))
        pad2_ref[pl.ds(base, rb), :] = dyc
        for k in range(SSD_CONV):
            dw[k] = dw[k] + jnp.sum(dyc * us[k], axis=0, keepdims=True)
        dw[SSD_CONV] = dw[SSD_CONV] + jnp.sum(dyc, axis=0, keepdims=True)
    for c in range(t // rb):
        base = CONV_PAD + c * rb
        acc = w[0:1, :] * pad2_ref[pl.ds(base + 2, rb), :]
        for k in range(1, SSD_CONV):
            acc = acc + w[k:k + 1, :] * pad2_ref[pl.ds(base + 2 - k, rb), :]
        du_ref[pl.ds(c * rb, rb), :] = acc.astype(du_ref.dtype)
    dw_ref[...] = jnp.concatenate(dw + [jnp.zeros((2, CONV_CT), F32)], axis=0)


def _conv_bwd(proj, dact, conv_w8, conv_b):
    t = proj.shape[0]
    off = XBC0 // CONV_CT
    col = pl.BlockSpec((t, CONV_CT), lambda i: (0, i))
    return pl.pallas_call(
        functools.partial(_conv_bwd_body, t=t), name="conv_bwd", grid=(CONV_DIM // CONV_CT,),
        in_specs=[pl.BlockSpec((t, CONV_CT), lambda i: (0, off + i)), col, pl.BlockSpec((8, CONV_CT), lambda i: (0, i)),
                  pl.BlockSpec((1, CONV_CT), lambda i: (0, i))],
        out_specs=[col, pl.BlockSpec((8, CONV_CT), lambda i: (0, i))],
        out_shape=[jax.ShapeDtypeStruct((t, CONV_DIM), BF16), jax.ShapeDtypeStruct((8, CONV_DIM), F32)],
        scratch_shapes=[pltpu.VMEM((t + 2 * CONV_PAD, CONV_CT), F32), pltpu.VMEM((t + 2 * CONV_PAD, CONV_CT), F32)],
        compiler_params=_params(1),
    )(proj, dact, conv_w8, conv_b)


GW = SSD_REP * SSD_HEAD_DIM


def _expand_np(hoff):
    e = np.zeros((LANE, D_INNER), np.float32)
    for h in range(SSD_HEADS):
        e[hoff + h, h * SSD_HEAD_DIM:(h + 1) * SSD_HEAD_DIM] = 1.0
    return e


def _head_masks():
    lane = lax.broadcasted_iota(jnp.int32, (1, GW), 1)
    return [((lane >= r * SSD_HEAD_DIM) & (lane < (r + 1) * SSD_HEAD_DIM)).astype(F32) for r in range(SSD_REP)]


def _ssd_common(dtc_ref, dtr_ref, bc_ref, br_ref, alc_ref, alr_ref, e_ref, rev):
    L = CHUNK
    ri = lax.broadcasted_iota(jnp.int32, (L, L), 0)
    ci = lax.broadcasted_iota(jnp.int32, (L, L), 1)
    tri = (ri <= ci) if rev else (ri >= ci)
    trit = (ri >= ci) if rev else (ri <= ci)
    raw_c = dtc_ref[...] + bc_ref[...]
    dt_c = _softplus(raw_c)
    a_c = -jnp.exp(alc_ref[...])
    cs_c = jnp.dot(tri.astype(F32), dt_c * a_c, precision=lax.Precision.HIGHEST, preferred_element_type=F32)
    dt_r = _softplus(dtr_ref[...] + br_ref[...])
    a_r = -jnp.exp(alr_ref[...])
    cs_r = jnp.dot(dt_r * a_r, trit.astype(F32), precision=lax.Precision.HIGHEST, preferred_element_type=F32)
    il = 0 if rev else L - 1
    e_c = jnp.exp(cs_c)
    w_c = jnp.exp(cs_c[il:il + 1, :] - cs_c)
    ex = _dot(_mx(jnp.concatenate([dt_c, e_c, w_c], axis=0)), e_ref[...])
    return dict(tri=tri, trit=trit, raw_c=raw_c, dt_c=dt_c, a_c=a_c, cs_c=cs_c, cs_r=cs_r, il=il, e_last=e_c[il:il + 1],
                dt_x=ex[0:L], e_x=ex[L:2 * L], w_x=ex[2 * L:3 * L], cd_x=ex[L + il:L + il + 1])


def _ssd_fwd_body(*refs, rev, hoff, nsteps, comm):
    L = CHUNK
    ins, cin, (y_ref, st_ref), cout, (s_scr,), csem = _split_refs(refs, 10, 2, comm)
    xs_ref, b_ref, c_ref, dtc_ref, dtr_ref, bc_ref, br_ref, alc_ref, alr_ref, e_ref = ins
    if comm is not None:
        comm.run(cin, cout, csem, pl.program_id(0), nsteps)

    @pl.when(pl.program_id(0) == 0)
    def _():
        s_scr[...] = jnp.zeros_like(s_scr)

    st_ref[0] = s_scr[...]
    q = _ssd_common(dtc_ref, dtr_ref, bc_ref, br_ref, alc_ref, alr_ref, e_ref, rev)
    masks = _head_masks()
    xdt = xs_ref[...] * q["dt_x"]
    xw = xdt * q["w_x"]
    early = []
    for g in range(SSD_GROUPS):
        gc = slice(g * GW, (g + 1) * GW)
        nc = slice(g * SSD_STATE, (g + 1) * SSD_STATE)
        bg = _mx(b_ref[:, nc])
        cg = _mx(c_ref[:, nc])
        sg = s_scr[:, gc]
        early.append((_dot(cg, bg, ((1,), (1,))), _dot(cg, _mx(sg)) * q["e_x"][:, gc]))
        s_scr[:, gc] = sg * q["cd_x"][:, gc] + _dot(bg, _mx(xw[:, gc]), ((0,), (0,)))
    for g in range(SSD_GROUPS):
        gc = slice(g * GW, (g + 1) * GW)
        cb, y = early[g]
        ms = []
        for r in range(SSD_REP):
            hh = hoff + g * SSD_REP + r
            diff = q["cs_c"][:, hh:hh + 1] - q["cs_r"][hh:hh + 1, :]
            ms.append(_mx(cb * jnp.exp(jnp.where(q["tri"], diff, NEG))))
        y4 = _dot(jnp.concatenate(ms, axis=0), _mx(xdt[:, gc]))
        for r in range(SSD_REP):
            y = y + y4[r * L:(r + 1) * L] * masks[r]
        y_ref[:, gc] = y


def _dt_rows_body(p_ref, o_ref):
    o_ref[...] = p_ref[...].T


def _dt_rows(proj, tr=512):
    t = proj.shape[0]
    tr = min(tr, t)
    return pl.pallas_call(
        functools.partial(_dt_rows_body), name="dt_rows", grid=(t // tr,),
        in_specs=[pl.BlockSpec((tr, LANE), lambda i: (i, DT0 // LANE))], out_specs=pl.BlockSpec((LANE, tr), lambda i: (0, i)),
        out_shape=jax.ShapeDtypeStruct((LANE, t), F32), compiler_params=_params(1),
    )(proj)


def _ssd_specs(t, rev):
    nc = t // CHUNK
    cm = (lambda i: nc - 1 - i) if rev else (lambda i: i)
    xs = pl.BlockSpec((CHUNK, D_INNER), lambda i: (cm(i), 0))
    bb = pl.BlockSpec((CHUNK, SSD_GROUPS * SSD_STATE), lambda i: (cm(i), 2))
    cc = pl.BlockSpec((CHUNK, SSD_GROUPS * SSD_STATE), lambda i: (cm(i), 3))
    dtc = pl.BlockSpec((CHUNK, LANE), lambda i: (cm(i), DT0 // LANE))
    dtr = pl.BlockSpec((LANE, CHUNK), lambda i: (0, cm(i)))
    colv = pl.BlockSpec((1, LANE), lambda i: (0, 0))
    rowv = pl.BlockSpec((LANE, 1), lambda i: (0, 0))
    return cm, xs, bb, cc, dtc, dtr, colv, rowv


def _ssd_fwd(xbc, proj, dtt, bias, alog, rev, comm=None):
    t = xbc.shape[0]
    nc = t // CHUNK
    cm, xs, bb, cc, dtc, dtr, colv, rowv = _ssd_specs(t, rev)
    hoff = SSD_HEADS if rev else 0
    e = jnp.asarray(_expand_np(hoff), BF16)
    in_specs = [xs, bb, cc, dtc, dtr, colv, rowv, colv, rowv, pl.BlockSpec((LANE, D_INNER), lambda i: (0, 0))]
    args = [xbc, xbc, xbc, proj, dtt, bias.reshape(1, LANE), bias.reshape(LANE, 1), alog.reshape(1, LANE), alog.reshape(LANE, 1), e]
    out_specs = [xs, pl.BlockSpec((1, SSD_STATE, D_INNER), lambda i: (cm(i), 0, 0))]
    out_shape = [jax.ShapeDtypeStruct((t, D_INNER), F32), jax.ShapeDtypeStruct((nc, SSD_STATE, D_INNER), F32)]
    scratch = [pltpu.VMEM((SSD_STATE, D_INNER), F32)]
    if comm is not None:
        in_specs, args = in_specs + [ANY] * comm.n, args + comm.arrs
        out_specs, out_shape = out_specs + [ANY] * comm.n, out_shape + comm.out_shapes()
        scratch = scratch + comm.sems()
    return pl.pallas_call(
        functools.partial(_ssd_fwd_body, rev=rev, hoff=hoff, nsteps=nc, comm=comm), name="ssd_fwd_rev" if rev else "ssd_fwd",
        grid=(nc,), in_specs=in_specs, out_specs=out_specs, out_shape=out_shape, scratch_shapes=scratch,
        compiler_params=_params(1),
    )(*args)


def _ssd_bwd_body(*refs, rev, hoff, first, nsteps, comm):
    L = CHUNK
    ins, cin, (dx_ref, ddt_ref, sm_ref), cout, (ds_scr,), csem = _split_refs(refs, 14 if first else 17, 3, comm)
    (xs_ref, b_ref, c_ref, dtc_ref, dtr_ref, bc_ref, br_ref, alc_ref, alr_ref, e_ref, et_ref, dy_ref, sp_ref,
     dsk_ref) = ins[:14]
    if not first:
        pdx_ref, pddt_ref, psm_ref = ins[14:]
    if comm is not None:
        comm.run(cin, cout, csem, pl.program_id(0), nsteps)

    @pl.when(pl.program_id(0) == 0)
    def _():
        ds_scr[...] = jnp.zeros_like(ds_scr)
        sm_ref[...] = jnp.zeros_like(sm_ref) if first else psm_ref[...]

    q = _ssd_common(dtc_ref, dtr_ref, bc_ref, br_ref, alc_ref, alr_ref, e_ref, rev)
    masks = _head_masks()
    xs = xs_ref[...]
    dy = dy_ref[...]
    xdt = xs * q["dt_x"]
    xw = xdt * q["w_x"]
    dye = dy * q["e_x"]
    ds_old = ds_scr[...]
    dxdt_parts, q1_parts, q2_parts = [], [], []
    ddiag = jnp.zeros((L, LANE), F32)
    early = []
    for g in range(SSD_GROUPS):
        gc = slice(g * GW, (g + 1) * GW)
        nc = slice(g * SSD_STATE, (g + 1) * SSD_STATE)
        bg = _mx(b_ref[:, nc])
        cg = _mx(c_ref[:, nc])
        dy_g = _mx(dy[:, gc])
        xdt_g = _mx(xdt[:, gc])
        dsg = ds_old[:, gc]
        spg = _mx(sp_ref[0, :, gc])
        dye_g = _mx(dye[:, gc])
        bds = _dot(bg, _mx(dsg))
        early.append(dict(
            bg=bg, cg=cg, dy_g=dy_g, bds=bds,
            cb=_dot(cg, bg, ((1,), (1,))), cbt=_dot(bg, cg, ((1,), (1,))),
            dm=_dot(_mx(jnp.concatenate([dy[:, gc] * m for m in masks], axis=0)), xdt_g, ((1,), (1,))),
            dmt=_dot(_mx(jnp.concatenate([xdt[:, gc] * m for m in masks], axis=0)), dy_g, ((1,), (1,))),
            dc=_dot(dye_g, spg, ((1,), (1,))), db=_dot(_mx(xw[:, gc]), _mx(dsg), ((1,), (1,)))))
        q1_parts.append(dye[:, gc] * _dot(cg, spg))
        q2_parts.append(xw[:, gc] * bds)
        ds_scr[:, gc] = dsg * q["cd_x"][:, gc] + _dot(cg, dye_g, ((0,), (0,)))
    for g in range(SSD_GROUPS):
        gc = slice(g * GW, (g + 1) * GW)
        e = early[g]
        bg, cg, dy_g, bds, cb, cbt, dm, dmt = (e[k] for k in ("bg", "cg", "dy_g", "bds", "cb", "cbt", "dm", "dmt"))
        dcb = jnp.zeros((L, L), F32)
        dcbt = jnp.zeros((L, L), F32)
        mts = []
        for r in range(SSD_REP):
            hh = hoff + g * SSD_REP + r
            col = q["cs_c"][:, hh:hh + 1]
            row = q["cs_r"][hh:hh + 1, :]
            dec = jnp.exp(jnp.where(q["tri"], col - row, NEG))
            dect = jnp.exp(jnp.where(q["trit"], row - col, NEG))
            pd = dm[r * L:(r + 1) * L] * dec
            pdt = dmt[r * L:(r + 1) * L] * dect
            dcb = dcb + pd
            dcbt = dcbt + pdt
            mts.append(_mx(cbt * dect))
            ddiag = ddiag + (jnp.sum(pd * cb, axis=1, keepdims=True) - jnp.sum(pdt * cbt, axis=1, keepdims=True)) * _onehot_lane(hh)
        x4 = _dot(jnp.concatenate(mts, axis=0), dy_g)
        dxdt_g = bds * q["w_x"][:, gc]
        for r in range(SSD_REP):
            dxdt_g = dxdt_g + x4[r * L:(r + 1) * L] * masks[r]
        dxdt_parts.append(dxdt_g)
        dc_g = _dot(_mx(dcb), bg) + e["dc"]
        db_g = _dot(_mx(dcbt), cg) + e["db"]
        boff = D_INNER + g * SSD_STATE
        coff = D_INNER + SSD_GROUPS * SSD_STATE + g * SSD_STATE
        if first:
            dx_ref[:, boff:boff + SSD_STATE] = db_g
            dx_ref[:, coff:coff + SSD_STATE] = dc_g
        else:
            dx_ref[:, boff:boff + SSD_STATE] = pdx_ref[:, boff:boff + SSD_STATE] + db_g
            dx_ref[:, coff:coff + SSD_STATE] = pdx_ref[:, coff:coff + SSD_STATE] + dc_g
    dxdt = jnp.concatenate(dxdt_parts, axis=1)
    et = et_ref[...]
    hs = _dot(_mx(jnp.concatenate([jnp.concatenate(q1_parts, axis=1), jnp.concatenate(q2_parts, axis=1), dxdt * xs], axis=0)), et)
    q1, q2, r3 = hs[0:L], hs[L:2 * L], hs[2 * L:3 * L]
    t_prev = _split_dot(jnp.sum(ds_old * sp_ref[0], axis=0, keepdims=True), et) * q["e_last"]
    rows = lax.broadcasted_iota(jnp.int32, (L, LANE), 0)
    dcs = ddiag + q1 - q2 + jnp.where(rows == q["il"], jnp.sum(q2, axis=0, keepdims=True) + t_prev, 0.0)
    dad = jnp.dot(q["trit"].astype(F32), dcs, precision=lax.Precision.HIGHEST, preferred_element_type=F32)
    ddt = dad * q["a_c"] + r3
    ddt_raw = ddt * _sigmoid(q["raw_c"])
    dal = jnp.sum(dad * q["dt_c"], axis=0, keepdims=True) * q["a_c"]
    dbias = jnp.sum(ddt_raw, axis=0, keepdims=True)
    sm_ref[...] += jnp.concatenate([dal, dbias, jnp.zeros((6, LANE), F32)], axis=0)
    dxs = dxdt * q["dt_x"]
    if first:
        dx_ref[:, 0:D_INNER] = dxs + dy * dsk_ref[...]
        ddt_ref[...] = ddt_raw
    else:
        dx_ref[:, 0:D_INNER] = pdx_ref[:, 0:D_INNER] + dxs
        ddt_ref[...] = pddt_ref[...] + ddt_raw


def _ssd_bwd(xbc, proj, dtt, bias, alog, dy, states, dskip_x, rev, prev=None, comm=None):
    t = xbc.shape[0]
    nc = t // CHUNK
    first = prev is None
    cm, xs, bb, cc, dtc, dtr, colv, rowv = _ssd_specs(t, not rev)
    hoff = SSD_HEADS if rev else 0
    e_np = _expand_np(hoff)
    e = jnp.asarray(e_np, BF16)
    et = jnp.asarray(e_np.T, BF16)
    st_spec = pl.BlockSpec((1, SSD_STATE, D_INNER), lambda i: (cm(i), 0, 0))
    dxo = pl.BlockSpec((CHUNK, CONV_DIM), lambda i: (cm(i), 0))
    ddto = pl.BlockSpec((CHUNK, LANE), lambda i: (cm(i), 0))
    smo = pl.BlockSpec((8, LANE), lambda i: (0, 0))
    in_specs = [xs, bb, cc, dtc, dtr, colv, rowv, colv, rowv, pl.BlockSpec((LANE, D_INNER), lambda i: (0, 0)),
                pl.BlockSpec((D_INNER, LANE), lambda i: (0, 0)), xs, st_spec, pl.BlockSpec((1, D_INNER), lambda i: (0, 0))]
    args = [xbc, xbc, xbc, proj, dtt, bias.reshape(1, LANE), bias.reshape(LANE, 1), alog.reshape(1, LANE),
            alog.reshape(LANE, 1), e, et, dy, states, dskip_x]
    if not first:
        in_specs += [dxo, ddto, smo]
        args += list(prev)
    out_specs = [dxo, ddto, smo]
    out_shape = [jax.ShapeDtypeStruct((t, CONV_DIM), F32), jax.ShapeDtypeStruct((t, LANE), F32),
                 jax.ShapeDtypeStruct((8, LANE), F32)]
    scratch = [pltpu.VMEM((SSD_STATE, D_INNER), F32)]
    if comm is not None:
        in_specs, args = in_specs + [ANY] * comm.n, args + comm.arrs
        out_specs, out_shape = out_specs + [ANY] * comm.n, out_shape + comm.out_shapes()
        scratch = scratch + comm.sems()
    return pl.pallas_call(
        functools.partial(_ssd_bwd_body, rev=rev, hoff=hoff, first=first, nsteps=nc, comm=comm),
        name="ssd_bwd_rev" if rev else "ssd_bwd", grid=(nc,), in_specs=in_specs, out_specs=out_specs, out_shape=out_shape,
        scratch_shapes=scratch, compiler_params=_params(1),
    )(*args)


def _gnorm_parts(yf_ref, yb_ref, xs_ref, z_ref, dsk_ref):
    xs = xs_ref[...]
    y = yf_ref[...] + yb_ref[...] + xs * dsk_ref[...]
    z = z_ref[...]
    sg = _sigmoid(z)
    s = z * sg
    v = y * s
    rs = []
    for g in range(SSD_GROUPS):
        vg = v[:, g * GW:(g + 1) * GW]
        rs.append(jnp.broadcast_to(lax.rsqrt(jnp.mean(vg * vg, axis=-1, keepdims=True) + EPS), vg.shape))
    return xs, y, z, sg, s, v, jnp.concatenate(rs, axis=1)


def _gnorm_fwd_body(yf_ref, yb_ref, xs_ref, z_ref, dsk_ref, w_ref, o_ref):
    _, _, _, _, _, v, rstd = _gnorm_parts(yf_ref, yb_ref, xs_ref, z_ref, dsk_ref)
    o_ref[...] = (v * rstd * w_ref[...]).astype(o_ref.dtype)


def _gnorm_fwd(yf, yb, xbc, proj, dskip_x, norm_w, tr=256):
    t = yf.shape[0]
    tr = min(tr, t)
    row = pl.BlockSpec((tr, D_INNER), lambda i: (i, 0))
    vec = pl.BlockSpec((1, D_INNER), lambda i: (0, 0))
    return pl.pallas_call(
        functools.partial(_gnorm_fwd_body), name="gnorm_fwd", grid=(t // tr,),
        in_specs=[row, row, row, pl.BlockSpec((tr, D_INNER), lambda i: (i, Z0 // D_INNER)), vec, vec],
        out_specs=row, out_shape=jax.ShapeDtypeStruct((t, D_INNER), BF16), compiler_params=_params(1),
    )(yf, yb, xbc, proj, dskip_x, norm_w.reshape(1, D_INNER))


def _gnorm_bwd_body(du_ref, yf_ref, yb_ref, xs_ref, z_ref, dsk_ref, w_ref, et_ref, dy_ref, dz_ref, dw_ref, dd_ref, acc_ref, *, nsteps):
    i = pl.program_id(0)

    @pl.when(i == 0)
    def _():
        dw_ref[...] = jnp.zeros_like(dw_ref)
        acc_ref[...] = jnp.zeros_like(acc_ref)

    xs, y, z, sg, s, v, rstd = _gnorm_parts(yf_ref, yb_ref, xs_ref, z_ref, dsk_ref)
    du = du_ref[...]
    n = v * rstd
    dn = du * w_ref[...]
    dw_ref[...] += jnp.sum(du * n, axis=0, keepdims=True)
    prod = dn * n
    means = []
    for g in range(SSD_GROUPS):
        pg = prod[:, g * GW:(g + 1) * GW]
        means.append(jnp.broadcast_to(jnp.mean(pg, axis=-1, keepdims=True), pg.shape))
    dv = rstd * (dn - n * jnp.concatenate(means, axis=1))
    dy = dv * s
    dy_ref[...] = dy
    dz_ref[...] = (dv * y * (sg * (1.0 + z * (1.0 - sg)))).astype(dz_ref.dtype)
    acc_ref[...] += jnp.sum(dy * xs, axis=0, keepdims=True)

    @pl.when(i == nsteps - 1)
    def _():
        dd_ref[...] = _split_dot(acc_ref[...], et_ref[...])


def _gnorm_bwd(du, yf, yb, xbc, proj, dskip_x, norm_w, tr=256):
    t = yf.shape[0]
    tr = min(tr, t)
    row = pl.BlockSpec((tr, D_INNER), lambda i: (i, 0))
    vec = pl.BlockSpec((1, D_INNER), lambda i: (0, 0))
    et = jnp.asarray(_expand_np(0).T, BF16)
    return pl.pallas_call(
        functools.partial(_gnorm_bwd_body, nsteps=t // tr), name="gnorm_bwd", grid=(t // tr,),
        in_specs=[row, row, row, row, pl.BlockSpec((tr, D_INNER), lambda i: (i, Z0 // D_INNER)), vec, vec,
                  pl.BlockSpec((D_INNER, LANE), lambda i: (0, 0))],
        out_specs=[row, row, vec, pl.BlockSpec((1, LANE), lambda i: (0, 0))],
        out_shape=[jax.ShapeDtypeStruct((t, D_INNER), F32), jax.ShapeDtypeStruct((t, D_INNER), BF16),
                   jax.ShapeDtypeStruct((1, D_INNER), F32), jax.ShapeDtypeStruct((1, LANE), F32)],
        scratch_shapes=[pltpu.VMEM((1, D_INNER), F32)],
        compiler_params=_params(1),
    )(du, yf, yb, xbc, proj, dskip_x, norm_w.reshape(1, D_INNER), et)


def _gate_fwd_body(gt_ref, b_ref, ys_ref, ya_ref, o_ref):
    g = _sigmoid(gt_ref[...] + b_ref[...])
    o_ref[...] = (g[:, 0:D_MODEL] * ys_ref[...] + g[:, D_MODEL:] * ya_ref[...]).astype(o_ref.dtype)


def _gate_fwd(proj, b_gate, y_ssd, y_attn, tr=512):
    t = y_ssd.shape[0]
    tr = min(tr, t)
    row = pl.BlockSpec((tr, D_MODEL), lambda i: (i, 0))
    return pl.pallas_call(
        functools.partial(_gate_fwd_body), name="gate_fwd", grid=(t // tr,),
        in_specs=[pl.BlockSpec((tr, 2 * D_MODEL), lambda i: (i, GT0 // (2 * D_MODEL))),
                  pl.BlockSpec((1, 2 * D_MODEL), lambda i: (0, 0)), row, row],
        out_specs=row, out_shape=jax.ShapeDtypeStruct((t, D_MODEL), BF16), compiler_params=_params(1),
    )(proj, b_gate.reshape(1, 2 * D_MODEL), y_ssd, y_attn)


def _gate_bwd_body(dm_ref, gt_ref, b_ref, ys_ref, ya_ref, dys_ref, dya_ref, dgt_ref, db_ref):
    @pl.when(pl.program_id(0) == 0)
    def _():
        db_ref[...] = jnp.zeros_like(db_ref)

    g = _sigmoid(gt_ref[...] + b_ref[...])
    gs = g[:, 0:D_MODEL]
    ga = g[:, D_MODEL:]
    dm = dm_ref[...]
    dys_ref[...] = (dm * gs).astype(dys_ref.dtype)
    dya_ref[...] = (dm * ga).astype(dya_ref.dtype)
    dgt = jnp.concatenate([dm * ys_ref[...] * gs * (1.0 - gs), dm * ya_ref[...] * ga * (1.0 - ga)], axis=1)
    dgt_ref[...] = dgt.astype(dgt_ref.dtype)
    db_ref[...] += jnp.sum(dgt, axis=0, keepdims=True)


def _gate_bwd(dmix, proj, b_gate, y_ssd, y_attn, tr=512):
    t = y_ssd.shape[0]
    tr = min(tr, t)
    row = pl.BlockSpec((tr, D_MODEL), lambda i: (i, 0))
    row2 = pl.BlockSpec((tr, 2 * D_MODEL), lambda i: (i, 0))
    vec2 = pl.BlockSpec((1, 2 * D_MODEL), lambda i: (0, 0))
    return pl.pallas_call(
        functools.partial(_gate_bwd_body), name="gate_bwd", grid=(t // tr,),
        in_specs=[row, pl.BlockSpec((tr, 2 * D_MODEL), lambda i: (i, GT0 // (2 * D_MODEL))), vec2, row, row],
        out_specs=[row, row, row2, vec2],
        out_shape=[jax.ShapeDtypeStruct((t, D_MODEL), BF16), jax.ShapeDtypeStruct((t, D_MODEL), BF16),
                   jax.ShapeDtypeStruct((t, 2 * D_MODEL), BF16), jax.ShapeDtypeStruct((1, 2 * D_MODEL), F32)],
        compiler_params=_params(1),
    )(dmix, proj, b_gate.reshape(1, 2 * D_MODEL), y_ssd, y_attn)


def _bucket_np():
    i = np.arange(BLOCK)[:, None]
    j = np.arange(3 * BLOCK)[None, :]
    rel = j - BLOCK - i
    nb = N_BUCKETS // 2
    max_exact = nb // 2
    ret = np.where(rel > 0, nb, 0)
    n = np.abs(rel)
    nf = np.maximum(n, 1).astype(np.float32)
    large = max_exact + (np.log(nf / np.float32(max_exact)) / np.float32(math.log(MAX_DISTANCE / max_exact))
                         * np.float32(nb - max_exact)).astype(np.int32)
    large = np.minimum(large, nb - 1)
    bucket = ret + np.where(n < max_exact, n, large)
    band = np.abs(rel) <= WINDOW
    return bucket, band


def _onehot_np():
    bucket, _ = _bucket_np()
    oh = np.zeros((BLOCK * 3 * BLOCK, LANE), np.float32)
    oh[np.arange(oh.shape[0]), bucket.reshape(-1)] = 1.0
    return oh


REP = Q_HEADS // KV_HEADS


def _group_cols(ref, g):
    return jnp.concatenate([ref[:, (g * REP + r) * HEAD_PAD:(g * REP + r + 1) * HEAD_PAD] for r in range(REP)], axis=0)


def _group_lane(ref, g, rows):
    return jnp.concatenate([jnp.broadcast_to(ref[:, g * REP + r:g * REP + r + 1], (rows, 1)) for r in range(REP)], axis=0)


def _attn_kv(refs, g):
    gc = slice(g * HEAD_PAD, (g + 1) * HEAD_PAD)
    return _mx(jnp.concatenate([r[:, gc] for r in refs], axis=0))


def _attn_logits(q_ref, kg, bias_ref, edge, g):
    q4 = _mx(_group_cols(q_ref, g))
    s = _dot(q4, kg, ((1,), (1,))) * (HEAD_DIM ** -0.5) + bias_ref[g] + edge
    return q4, s


def _attn_edge(n, nb):
    col = lax.broadcasted_iota(jnp.int32, (1, 3 * BLOCK), 1)
    bad = ((col < BLOCK) & (n == 0)) | ((col >= 2 * BLOCK) & (n == nb - 1))
    return jnp.where(bad, NEG, 0.0)


def _onehot_lane(h):
    return (lax.broadcasted_iota(jnp.int32, (1, LANE), 1) == h).astype(F32)


def _attn_fwd_body(q_ref, kp_ref, kc_ref, kn_ref, vp_ref, vc_ref, vn_ref, bias_ref, sink_ref, o_ref, lse_ref, *, nb):
    n = pl.program_id(0)
    edge = _attn_edge(n, nb)
    lse = jnp.zeros((BLOCK, LANE), F32)
    logits = [_attn_logits(q_ref, _attn_kv((kp_ref, kc_ref, kn_ref), g), bias_ref, edge, g)[1] for g in range(KV_HEADS)]
    for g in range(KV_HEADS):
        vg = _attn_kv((vp_ref, vc_ref, vn_ref), g)
        s = logits[g]
        sink = _group_lane(sink_ref, g, BLOCK)
        m = jnp.maximum(jnp.max(s, axis=-1, keepdims=True), sink)
        p = jnp.exp(s - m)
        den = jnp.sum(p, axis=-1, keepdims=True) + jnp.exp(sink - m)
        o4 = _dot(_mx(p * (1.0 / den)), vg)
        l4 = m + jnp.log(den)
        for r in range(REP):
            h = g * REP + r
            o_ref[:, h * HEAD_PAD:(h + 1) * HEAD_PAD] = o4[r * BLOCK:(r + 1) * BLOCK].astype(o_ref.dtype)
            lse = lse + l4[r * BLOCK:(r + 1) * BLOCK] * _onehot_lane(h)
    lse_ref[...] = lse


def _attn_specs(t, clamp):
    nb = t // BLOCK
    cur = (lambda n: jnp.minimum(n, nb - 1)) if clamp else (lambda n: n)
    prv = lambda n: jnp.maximum(cur(n) - 1, 0)
    nxt = lambda n: jnp.minimum(cur(n) + 1, nb - 1)
    kb, vb = K0 // KP, V0 // KP
    qs = pl.BlockSpec((BLOCK, QP), lambda n: (cur(n), Q0 // QP))
    ks = [pl.BlockSpec((BLOCK, KP), lambda n, f=f: (f(n), kb)) for f in (prv, cur, nxt)]
    vs = [pl.BlockSpec((BLOCK, KP), lambda n, f=f: (f(n), vb)) for f in (prv, cur, nxt)]
    consts = [pl.BlockSpec((KV_HEADS, REP * BLOCK, 3 * BLOCK), lambda n: (0, 0, 0)), pl.BlockSpec((1, LANE), lambda n: (0, 0))]
    return nb, cur, qs, ks, vs, consts


def _attn_fwd(proj, bias, sink):
    t = proj.shape[0]
    nb, cur, qs, ks, vs, consts = _attn_specs(t, False)
    return pl.pallas_call(
        functools.partial(_attn_fwd_body, nb=nb), name="attn_fwd", grid=(nb,),
        in_specs=[qs] + ks + vs + consts,
        out_specs=[pl.BlockSpec((BLOCK, QP), lambda n: (n, 0)), pl.BlockSpec((BLOCK, LANE), lambda n: (n, 0))],
        out_shape=[jax.ShapeDtypeStruct((t, QP), BF16), jax.ShapeDtypeStruct((t, LANE), F32)],
        compiler_params=_params(1),
    )(proj, proj, proj, proj, proj, proj, proj, bias, sink)


def _attn_bwd_body(q_ref, kp_ref, kc_ref, kn_ref, vp_ref, vc_ref, vn_ref, bias_ref, sink_ref, do_ref, o_ref, lse_ref,
                   dq_ref, dk_ref, dv_ref, dbias_ref, dsink_ref, ak, bk, av, bv, *, nb):
    n = pl.program_id(0)
    scale = HEAD_DIM ** -0.5

    @pl.when(n == 0)
    def _():
        for r in (ak, bk, av, bv, dbias_ref, dsink_ref):
            r[...] = jnp.zeros_like(r)

    @pl.when(n < nb)
    def _():
        edge = _attn_edge(n, nb)
        dsink = jnp.zeros((1, LANE), F32)
        early = []
        for g in range(KV_HEADS):
            kg = _attn_kv((kp_ref, kc_ref, kn_ref), g)
            q4, s = _attn_logits(q_ref, kg, bias_ref, edge, g)
            do = _group_cols(do_ref, g)
            early.append((kg, q4, s, do, _dot(_mx(do), _attn_kv((vp_ref, vc_ref, vn_ref), g), ((1,), (1,)))))
        for g in range(KV_HEADS):
            gc = slice(g * HEAD_PAD, (g + 1) * HEAD_PAD)
            kg, q4, s, do, dp = early[g]
            lse = _group_lane(lse_ref, g, BLOCK)
            p = jnp.exp(s - lse)
            delta = jnp.sum(do * _group_cols(o_ref, g).astype(F32), axis=-1, keepdims=True)
            ds = p * (dp - delta)
            sunk = jnp.exp(_group_lane(sink_ref, g, BLOCK) - lse) * delta
            dbias_ref[g] += ds
            dsb = _mx(ds)
            dq4 = _dot(dsb, kg) * scale
            for r in range(REP):
                h = g * REP + r
                dq_ref[:, h * HEAD_PAD:(h + 1) * HEAD_PAD] = dq4[r * BLOCK:(r + 1) * BLOCK].astype(dq_ref.dtype)
                dsink = dsink - jnp.sum(sunk[r * BLOCK:(r + 1) * BLOCK], axis=0, keepdims=True) * _onehot_lane(h)
            dk_c = _dot(dsb, q4, ((0,), (0,))) * scale
            dv_c = _dot(_mx(p), _mx(do), ((0,), (0,)))
            for acc_a, acc_b, out, c in ((ak, bk, dk_ref, dk_c), (av, bv, dv_ref, dv_c)):
                out[:, gc] = (acc_a[:, gc] + c[0:BLOCK]).astype(out.dtype)
                acc_a[:, gc] = acc_b[:, gc] + c[BLOCK:2 * BLOCK]
                acc_b[:, gc] = c[2 * BLOCK:3 * BLOCK]
        dsink_ref[...] += dsink

    @pl.when(n == nb)
    def _():
        dk_ref[...] = ak[...].astype(dk_ref.dtype)
        dv_ref[...] = av[...].astype(dv_ref.dtype)


def _attn_bwd(proj, bias, sink, do, o, lse):
    t = proj.shape[0]
    nb, cur, qs, ks, vs, consts = _attn_specs(t, True)
    rowq = pl.BlockSpec((BLOCK, QP), lambda n: (cur(n), 0))
    late = pl.BlockSpec((BLOCK, KP), lambda n: (jnp.maximum(n - 1, 0), 0))
    return pl.pallas_call(
        functools.partial(_attn_bwd_body, nb=nb), name="attn_bwd", grid=(nb + 1,),
        in_specs=[qs] + ks + vs + consts + [rowq, rowq, pl.BlockSpec((BLOCK, LANE), lambda n: (cur(n), 0))],
        out_specs=[rowq, late, late, pl.BlockSpec((KV_HEADS, REP * BLOCK, 3 * BLOCK), lambda n: (0, 0, 0)),
                   pl.BlockSpec((1, LANE), lambda n: (0, 0))],
        out_shape=[jax.ShapeDtypeStruct((t, QP), BF16), jax.ShapeDtypeStruct((t, KP), BF16), jax.ShapeDtypeStruct((t, KP), BF16),
                   jax.ShapeDtypeStruct((KV_HEADS, REP * BLOCK, 3 * BLOCK), F32), jax.ShapeDtypeStruct((1, LANE), F32)],
        scratch_shapes=[pltpu.VMEM((BLOCK, KP), F32) for _ in range(4)],
        compiler_params=_params(1),
    )(proj, proj, proj, proj, proj, proj, proj, bias, sink, do, o, lse)


def _adamw_body(gp_ref, w_ref, m_ref, v_ref, g_ref, d_ref, nm_ref, nv_ref, *, nparts):
    g = gp_ref[0].astype(F32)
    for j in range(1, nparts):
        g = g + gp_ref[j].astype(F32)
    m = ADAM_B1 * m_ref[...] + (1.0 - ADAM_B1) * g
    v = ADAM_B2 * v_ref[...] + (1.0 - ADAM_B2) * (g * g)
    m_hat = m / (1.0 - ADAM_B1 ** ADAM_STEP)
    v_hat = v / (1.0 - ADAM_B2 ** ADAM_STEP)
    g_ref[...] = g
    d_ref[...] = -ADAM_LR * (m_hat / (jnp.sqrt(v_hat) + ADAM_EPS) + ADAM_WD * w_ref[...])
    nm_ref[...] = m
    nv_ref[...] = v


def _adamw(gparts, w, m, v, name, tr=256):
    p, r, c = gparts.shape
    tr = min(tr, r)
    assert r % tr == 0
    row = pl.BlockSpec((tr, c), lambda i: (i, 0))
    sd = jax.ShapeDtypeStruct((r, c), F32)
    return pl.pallas_call(
        functools.partial(_adamw_body, nparts=p), name=name, grid=(r // tr,),
        in_specs=[pl.BlockSpec((p, tr, c), lambda i: (0, i, 0)), row, row, row],
        out_specs=[row, row, row, row], out_shape=[sd, sd, sd, sd], compiler_params=_params(1),
    )(gparts, w, m, v)


MESH = pl.DeviceIdType.MESH
N_REL = N_DEV - 1


def _gather_phases(ins, outs, sems):
    n = len(ins)
    send_sems, recv_sems, local_sems = sems
    x, y, c = lax.axis_index("x"), lax.axis_index("y"), lax.axis_index("c")
    me, sibling = (x, y, c), (x, y, 1 - c)
    chips = [(1 - x, y), (x, 1 - y), (1 - x, 1 - y)]

    def slot(a, p):
        return outs[a].at[4 * p[0] + 2 * p[1] + p[2]]

    def copy(a, k, block, to, src=None):
        return pltpu.make_async_remote_copy(
            src_ref=slot(a, block) if src is None else src, dst_ref=slot(a, block),
            send_sem=send_sems.at[a * N_REL + k], recv_sem=recv_sems.at[a * N_REL + k], device_id=to, device_id_type=MESH)

    mine = [pltpu.make_async_copy(ins[a], slot(a, me), local_sems.at[a]) for a in range(n)]
    first = []
    for a in range(n):
        first.append(copy(a, 0, me, sibling, src=ins[a]))
        first += [copy(a, 1 + j, me, (*chip, c), src=ins[a]) for j, chip in enumerate(chips)]
    passed = [copy(a, 4 + j, (*chip, c), sibling) for j, chip in enumerate(chips) for a in range(n)]

    def start():
        for cp in mine + first:
            cp.start()

    def pass_on():
        i = 0
        for j, chip in enumerate(chips):
            for a in range(n):
                copy(a, 1 + j, (*chip, c), me).wait_recv()
                passed[i].start()
                i += 1

    def finish():
        for a in range(n):
            copy(a, 0, sibling, me).wait_recv()
            for j, chip in enumerate(chips):
                copy(a, 4 + j, (*chip, 1 - c), me).wait_recv()
        for cp in first + passed:
            cp.wait_send()
        for cp in mine:
            cp.wait()

    return start, pass_on, finish


def _exchange_phases(ins, outs, sems):
    n = len(ins)
    send_sems, recv_sems, local_sems = sems
    x, y, c = lax.axis_index("x"), lax.axis_index("y"), lax.axis_index("c")
    me = 4 * x + 2 * y + c
    mine = [pltpu.make_async_copy(ins[a].at[me], outs[a].at[me], local_sems.at[a]) for a in range(n)]
    copies = []
    for a in range(n):
        for k in range(1, N_DEV):
            px = 1 - x if k & 4 else x
            py = 1 - y if k & 2 else y
            pc = 1 - c if k & 1 else c
            peer = 4 * px + 2 * py + pc
            send = pltpu.make_async_remote_copy(
                src_ref=ins[a].at[peer], dst_ref=outs[a].at[me], send_sem=send_sems.at[a * N_REL + k - 1],
                recv_sem=recv_sems.at[a * N_REL + k - 1], device_id=(px, py, pc), device_id_type=MESH)
            recv = pltpu.make_async_remote_copy(
                src_ref=ins[a].at[me], dst_ref=outs[a].at[peer], send_sem=send_sems.at[a * N_REL + k - 1],
                recv_sem=recv_sems.at[a * N_REL + k - 1], device_id=(px, py, pc), device_id_type=MESH)
            copies.append((send, recv))

    def start():
        for cp in mine:
            cp.start()
        for send, _ in copies:
            send.start()

    def finish():
        for send, recv in copies:
            send.wait_send()
            recv.wait_recv()
        for cp in mine:
            cp.wait()

    return start, None, finish


class _Comm:
    def __init__(self, kind, arrs):
        self.kind, self.arrs, self.n = kind, list(arrs), len(arrs)

    def out_shapes(self):
        if self.kind == "gather":
            return [jax.ShapeDtypeStruct((N_DEV,) + a.shape, a.dtype) for a in self.arrs]
        return [jax.ShapeDtypeStruct(a.shape, a.dtype) for a in self.arrs]

    def sems(self):
        return [pltpu.SemaphoreType.DMA((self.n * N_REL,)), pltpu.SemaphoreType.DMA((self.n * N_REL,)),
                pltpu.SemaphoreType.DMA((self.n,))]

    def run(self, ins, outs, sems, step, nsteps):
        start, mid, finish = (_gather_phases if self.kind == "gather" else _exchange_phases)(ins, outs, sems)
        if nsteps is None:
            start()
            if mid is not None:
                mid()
            finish()
            return
        pl.when(step == 0)(start)
        if mid is not None:
            pl.when(step == nsteps // 2)(mid)
        pl.when(step == nsteps - 1)(finish)


def _split_refs(refs, n_in, n_out, comm):
    k = comm.n if comm is not None else 0
    i = 0
    parts = []
    for cnt in (n_in, k, n_out, k):
        parts.append(refs[i:i + cnt])
        i += cnt
    rest = refs[i:]
    if comm is None:
        return parts[0], (), parts[2], (), rest, ()
    return parts[0], parts[1], parts[2], parts[3], rest[:len(rest) - 3], rest[len(rest) - 3:]


def _comm_body(*refs, comm):
    n = comm.n
    comm.run(refs[:n], refs[n:2 * n], refs[2 * n:], None, None)


def _communicate(comm, name, vmem=False):
    spec = pl.BlockSpec(memory_space=pltpu.VMEM if vmem else pl.ANY)
    return pl.pallas_call(
        functools.partial(_comm_body, comm=comm), name=name,
        in_specs=[spec] * comm.n, out_specs=[spec] * comm.n, out_shape=comm.out_shapes(), scratch_shapes=comm.sems(),
        compiler_params=pltpu.CompilerParams(vmem_limit_bytes=VMEM_LIMIT),
    )(*comm.arrs)


ANY = pl.BlockSpec(memory_space=pl.ANY)


def _pad_heads(w, axis):
    shp = w.shape
    heads = shp[axis] // HEAD_DIM
    w = w.reshape(shp[:axis] + (heads, HEAD_DIM) + shp[axis + 1:])
    pad = [(0, 0)] * w.ndim
    pad[axis + 1] = (0, HEAD_PAD - HEAD_DIM)
    w = jnp.pad(w, pad)
    return w.reshape(shp[:axis] + (heads * HEAD_PAD,) + shp[axis + 1:])


def _unpad_heads(w, axis):
    shp = w.shape
    heads = shp[axis] // HEAD_PAD
    w = w.reshape(shp[:axis] + (heads, HEAD_PAD) + shp[axis + 1:])
    w = lax.slice_in_dim(w, 0, HEAD_DIM, axis=axis + 1)
    return w.reshape(shp[:axis] + (heads * HEAD_DIM,) + shp[axis + 1:])


def _w_in_to_padded(w):
    idx = np.cumsum((0,) + IN_SPLITS)
    z, xbc, dt, q, k, v, gates = [w[:, idx[i]:idx[i + 1]] for i in range(7)]
    zeros = lambda n: jnp.zeros((w.shape[0], n), w.dtype)
    return jnp.concatenate([z, gates, _pad_heads(q, 1), xbc, _pad_heads(k, 1), _pad_heads(v, 1), dt, zeros(LANE - 2 * SSD_HEADS),
                            zeros(NP - DT0 - LANE)], axis=1)


def _w_in_from_padded(w):
    z = w[:, Z0:Z0 + D_INNER]
    gates = w[:, GT0:GT0 + 2 * D_MODEL]
    q = _unpad_heads(w[:, Q0:Q0 + QP], 1)
    xbc = w[:, XBC0:XBC0 + CONV_DIM]
    k = _unpad_heads(w[:, K0:K0 + KP], 1)
    v = _unpad_heads(w[:, V0:V0 + KP], 1)
    dt = w[:, DT0:DT0 + 2 * SSD_HEADS]
    return jnp.concatenate([z, xbc, dt, q, k, v, gates], axis=1)


def _pad_lane(v):
    v = v.reshape(-1)
    return jnp.pad(v, (0, LANE - v.shape[0]))


def _layer_consts(p):
    c = dict(p)
    c["conv_w8"] = jnp.pad(p["conv_w"].reshape(SSD_CONV, CONV_DIM), ((0, 8 - SSD_CONV), (0, 0)))
    c["conv_b1"] = p["conv_b"].reshape(1, CONV_DIM)
    c["dtb"] = _pad_lane(p["dt_bias"])
    c["alog"] = _pad_lane(p["a_log"])
    c["dskip_x"] = jnp.repeat(p["d_skip"], SSD_HEAD_DIM).reshape(1, D_INNER)
    c["sink"] = _pad_lane(p["attn_sink"]).reshape(1, LANE)
    return c


def _hosted(res, n_own, hook):
    res = list(res)
    if hook is not None:
        hook[1](res[n_own:])
    return res[0] if n_own == 1 else res[:n_own]


def _layer_fwd(x, c, w, bias, l, hooks):
    comm = lambda k: hooks[k][0] if k in hooks else None
    s = {"x": x}
    s["h"] = _rms_fwd(x, c["pre_mix_norm"], f"pre_mix_norm_{l}", out_dtype=BF16)
    res = _mm(s["h"], w["w_in"], f"in_proj_{l}", comm=comm("in_proj"))
    proj = _hosted(res, 1, hooks.get("in_proj")) if "in_proj" in hooks else res
    s["proj"] = proj
    s["xbc"] = _conv_fwd(proj, c["conv_w8"], c["conv_b1"])
    s["dtt"] = _dt_rows(proj)
    s["yf"], s["stf"] = _hosted(_ssd_fwd(s["xbc"], proj, s["dtt"], c["dtb"], c["alog"], False, comm=comm("ssd_fwd")), 2,
                                hooks.get("ssd_fwd"))
    s["yb"], s["stb"] = _hosted(_ssd_fwd(s["xbc"], proj, s["dtt"], c["dtb"], c["alog"], True, comm=comm("ssd_fwd_rev")), 2,
                                hooks.get("ssd_fwd_rev"))
    s["u"] = _gnorm_fwd(s["yf"], s["yb"], s["xbc"], proj, c["dskip_x"], c["ssd_norm"])
    s["y_ssd"] = _mm(s["u"], w["w_ssd_out"], f"ssd_out_{l}")
    s["o"], s["lse"] = _attn_fwd(proj, bias, c["sink"])
    s["y_attn"] = _mm(s["o"], w["w_attn_out"], f"attn_out_{l}")
    s["mix"] = _gate_fwd(proj, c["b_gate"], s["y_ssd"], s["y_attn"])
    s["mixed"] = _mm(s["mix"], w["w_o"], f"w_o_{l}")
    s["x1"] = _rms_fwd(s["mixed"], c["post_mix_norm"], f"post_mix_norm_{l}", res=x)
    s["h2"] = _rms_fwd(s["x1"], c["pre_mlp_norm"], f"pre_mlp_norm_{l}", out_dtype=BF16)
    s["f1"], s["a"] = _mm(s["h2"], w["w_mlp_in"], f"mlp_in_{l}", mode="relu2")
    s["f"] = _mm(s["a"], w["w_mlp_out"], f"mlp_out_{l}")
    x2 = _rms_fwd(s["f"], c["post_mlp_norm"], f"post_mlp_norm_{l}", res=s["x1"])
    return x2, s


def _send_rest(gw):
    s_mi = gw["w_mlp_in"].reshape(D_MODEL, N_DEV, -1).transpose(1, 0, 2).astype(BF16)
    parts = [(_unpad_heads(gw[n], 0) if n == "w_attn_out" else gw[n]).reshape(N_DEV, -1, D_MODEL) for n in ROWS]
    return [s_mi, jnp.concatenate(parts, axis=1).astype(BF16)]


def _send_w_in(g):
    return [_w_in_from_padded(g).reshape(D_MODEL, N_DEV, -1).transpose(1, 0, 2).astype(BF16)]


def _layer_bwd(dx2, s, c, w, bias, l, earlier=None, host_own=False):
    gw, gs, got = {}, {}, {}
    df, gs["post_mlp_norm"] = _rms_bwd(s["f"], c["post_mlp_norm"], dx2, f"post_mlp_norm_bwd_{l}", out_dtype=BF16)
    df1 = _mm(df, w["w_mlp_out"], f"mlp_out_dx_{l}", tb=True, mode="relu2_bwd", extra=s["f1"], out_dtype=BF16)
    gw["w_mlp_out"] = _mm(s["a"], df, f"mlp_out_dw_{l}", ta=True)
    gw["w_mlp_in"] = _mm(s["h2"], df1, f"mlp_in_dw_{l}", ta=True)
    dh2 = _mm(df1, w["w_mlp_in"], f"mlp_in_dx_{l}", tb=True)
    dx1, gs["pre_mlp_norm"] = _rms_bwd(s["x1"], c["pre_mlp_norm"], dh2, f"pre_mlp_norm_bwd_{l}", add=dx2)
    dmixed, gs["post_mix_norm"] = _rms_bwd(s["mixed"], c["post_mix_norm"], dx1, f"post_mix_norm_bwd_{l}", out_dtype=BF16)
    gw["w_o"] = _mm(s["mix"], dmixed, f"w_o_dw_{l}", ta=True)
    dmix = _mm(dmixed, w["w_o"], f"w_o_dx_{l}", tb=True)
    dys, dya, dgt, gs["b_gate"] = _gate_bwd(dmix, s["proj"], c["b_gate"], s["y_ssd"], s["y_attn"])
    gw["w_ssd_out"] = _mm(s["u"], dys, f"ssd_out_dw_{l}", ta=True)
    du = _mm(dys, w["w_ssd_out"], f"ssd_out_dx_{l}", tb=True)
    gw["w_attn_out"] = _mm(s["o"], dya, f"attn_out_dw_{l}", ta=True)
    do = _mm(dya, w["w_attn_out"], f"attn_out_dx_{l}", tb=True)
    dq, dk, dv, dbias, dsink = _attn_bwd(s["proj"], bias, c["sink"], do, s["o"], s["lse"])
    gs["attn_sink"] = dsink
    dy, dz, gs["ssd_norm"], gs["d_skip"] = _gnorm_bwd(du, s["yf"], s["yb"], s["xbc"], s["proj"], c["dskip_x"], c["ssd_norm"])
    res = _ssd_bwd(s["xbc"], s["proj"], s["dtt"], c["dtb"], c["alog"], dy, s["stf"], c["dskip_x"], False,
                   comm=None if earlier is None else _Comm("exchange", earlier))
    part, got["earlier"] = res[:3], res[3:]
    res = _ssd_bwd(s["xbc"], s["proj"], s["dtt"], c["dtb"], c["alog"], dy, s["stb"], c["dskip_x"], True, prev=part,
                   comm=_Comm("exchange", _send_rest(gw)) if host_own else None)
    (dxbc, ddt, sm), got["rest"] = res[:3], res[3:]
    gs["a_log"], gs["dt_bias"] = sm[0:1], sm[1:2]
    dxbc_raw, gs["conv"] = _conv_bwd(s["proj"], dxbc, c["conv_w8"], c["conv_b1"])
    t = dx2.shape[0]
    dproj = jnp.concatenate([dz, dgt, dq, dxbc_raw, dk, dv, ddt.astype(BF16), jnp.zeros((t, NP - DT0 - LANE), BF16)], axis=1)
    gw["w_in"] = _mm(s["h"], dproj, f"in_proj_dw_{l}", ta=True)
    if host_own:
        dh, *got["w_in"] = _mm(dproj, w["w_in"], f"in_proj_dx_{l}", tb=True, comm=_Comm("exchange", _send_w_in(gw["w_in"])))
    else:
        dh = _mm(dproj, w["w_in"], f"in_proj_dx_{l}", tb=True)
    dx, gs["pre_mix_norm"] = _rms_bwd(s["x"], c["pre_mix_norm"], dh, f"pre_mix_norm_bwd_{l}", add=dx1)
    return dx, gw, gs, dbias, got


SMALL = (("pre_mix_norm", DEPTH * D_MODEL), ("b_gate", DEPTH * 2 * D_MODEL), ("conv_w", DEPTH * SSD_CONV * CONV_DIM),
         ("conv_b", DEPTH * CONV_DIM), ("dt_bias", DEPTH * 2 * SSD_HEADS), ("a_log", DEPTH * 2 * SSD_HEADS),
         ("d_skip", DEPTH * SSD_HEADS), ("ssd_norm", DEPTH * D_INNER), ("attn_sink", DEPTH * Q_HEADS),
         ("rel_bias_table", N_BUCKETS * Q_HEADS), ("post_mix_norm", DEPTH * D_MODEL), ("pre_mlp_norm", DEPTH * D_MODEL),
         ("post_mlp_norm", DEPTH * D_MODEL))


def _small_rows(n):
    return -(-n // LANE)


def _pack_small(vals, fill=0.0):
    rows = []
    for name, n in SMALL:
        v = vals[name].reshape(-1).astype(F32)
        rows.append(jnp.pad(v, (0, _small_rows(n) * LANE - n), constant_values=fill).reshape(-1, LANE))
    out = jnp.concatenate(rows, axis=0)
    return jnp.pad(out, ((0, -out.shape[0] % 8), (0, 0)), constant_values=fill)


def _unpack_small(packed, shapes):
    out, r = {}, 0
    for name, n in SMALL:
        nr = _small_rows(n)
        out[name] = packed[r:r + nr].reshape(-1)[:n].reshape(shapes[name])
        r += nr
    return out


BIG = ("w_in", "w_ssd_out", "w_attn_out", "w_o", "w_mlp_in", "w_mlp_out")
ROWS = ("w_ssd_out", "w_attn_out", "w_o", "w_mlp_out")


def _pack_rows(vals, lead):
    return jnp.concatenate([vals[n].reshape(lead + (-1, D_MODEL)) for n in ROWS], axis=len(lead))


def kernel(x, pre_mix_norm, w_in, b_gate, conv_w, conv_b, dt_bias, a_log, d_skip, ssd_norm, w_ssd_out, attn_sink, rel_bias_table, w_attn_out, w_o, post_mix_norm, pre_mlp_norm, w_mlp_in, w_mlp_out, post_mlp_norm, loss_target, m_pre_mix_norm, m_w_in, m_b_gate, m_conv_w, m_conv_b, m_dt_bias, m_a_log, m_d_skip, m_ssd_norm, m_w_ssd_out, m_attn_sink, m_rel_bias_table, m_w_attn_out, m_w_o, m_post_mix_norm, m_pre_mlp_norm, m_w_mlp_in, m_w_mlp_out, m_post_mlp_norm, v_pre_mix_norm, v_w_in, v_b_gate, v_conv_w, v_conv_b, v_dt_bias, v_a_log, v_d_skip, v_ssd_norm, v_w_ssd_out, v_attn_sink, v_rel_bias_table, v_w_attn_out, v_w_o, v_post_mix_norm, v_pre_mlp_norm, v_w_mlp_in, v_w_mlp_out, v_post_mlp_norm):
    names = ("pre_mix_norm", "w_in", "b_gate", "conv_w", "conv_b", "dt_bias", "a_log", "d_skip", "ssd_norm", "w_ssd_out",
             "attn_sink", "rel_bias_table", "w_attn_out", "w_o", "post_mix_norm", "pre_mlp_norm", "w_mlp_in", "w_mlp_out",
             "post_mlp_norm")
    W = dict(zip(names, (pre_mix_norm, w_in, b_gate, conv_w, conv_b, dt_bias, a_log, d_skip, ssd_norm, w_ssd_out, attn_sink,
                         rel_bias_table, w_attn_out, w_o, post_mix_norm, pre_mlp_norm, w_mlp_in, w_mlp_out, post_mlp_norm)))
    M = dict(zip(names, (m_pre_mix_norm, m_w_in, m_b_gate, m_conv_w, m_conv_b, m_dt_bias, m_a_log, m_d_skip, m_ssd_norm,
                         m_w_ssd_out, m_attn_sink, m_rel_bias_table, m_w_attn_out, m_w_o, m_post_mix_norm, m_pre_mlp_norm,
                         m_w_mlp_in, m_w_mlp_out, m_post_mlp_norm)))
    V = dict(zip(names, (v_pre_mix_norm, v_w_in, v_b_gate, v_conv_w, v_conv_b, v_dt_bias, v_a_log, v_d_skip, v_ssd_norm,
                         v_w_ssd_out, v_attn_sink, v_rel_bias_table, v_w_attn_out, v_w_o, v_post_mix_norm, v_pre_mlp_norm,
                         v_w_mlp_in, v_w_mlp_out, v_post_mlp_norm)))
    t = x.shape[1]
    shard = {n: W[n].shape for n in names}
    dev = 4 * lax.axis_index("x") + 2 * lax.axis_index("y") + lax.axis_index("c")
    cshard = CONV_DIM // N_DEV

    sh_in = [w_in[l].astype(BF16) for l in range(DEPTH)]
    sh_rest = [[w_mlp_in[l].astype(BF16), _pack_rows({n: W[n][l].astype(BF16) for n in ROWS}, ())] for l in range(DEPTH)]
    full = [{} for _ in range(DEPTH)]

    def take_w_in(l):
        def take(got):
            full[l]["w_in"] = _w_in_to_padded(got[0].transpose(1, 0, 2).reshape(D_MODEL, N_IN))
        return take

    def take_rest(l):
        def take(got):
            full[l]["w_mlp_in"] = got[0].transpose(1, 0, 2).reshape(D_MODEL, D_FF)
            r = 0
            for n in ROWS:
                per = shard[n][1]
                full[l][n] = got[1][:, r:r + per].reshape(N_DEV * per, D_MODEL)
                r += per
            full[l]["w_attn_out"] = _pad_heads(full[l]["w_attn_out"], 0)
        return take

    take_w_in(0)(_communicate(_Comm("gather", [sh_in[0]]), "gather_w_in_0"))
    (g_conv,) = _communicate(_Comm("gather", [conv_w.reshape(-1, LANE)]), "gather_conv_w", vmem=True)
    conv_full = g_conv.reshape(N_DEV, DEPTH, SSD_CONV, cshard).transpose(1, 2, 0, 3).reshape(DEPTH, SSD_CONV, CONV_DIM)
    hooks = [{"in_proj": (_Comm("gather", sh_rest[0]), take_rest(0)),
              "ssd_fwd": (_Comm("gather", [sh_in[1]]), take_w_in(1)),
              "ssd_fwd_rev": (_Comm("gather", sh_rest[1]), take_rest(1))}, {}]

    _, band = _bucket_np()
    onehot = jnp.asarray(_onehot_np(), BF16)
    table_t = jnp.pad(rel_bias_table.T, ((0, 0), (0, LANE - N_BUCKETS)))
    bias = _mm(table_t, onehot, "t5_bias", tb=True, tm=Q_HEADS, tn=3 * BLOCK * BLOCK // 8, split_a=True)
    bias = (bias.reshape(Q_HEADS, BLOCK, 3 * BLOCK) + jnp.asarray(np.where(band, 0.0, NEG), F32)).reshape(
        KV_HEADS, REP * BLOCK, 3 * BLOCK)

    consts = []
    for l in range(DEPTH):
        p = {n: W[n][l] for n in names if n not in BIG and n not in ("rel_bias_table", "conv_w")}
        p["conv_w"] = conv_full[l]
        consts.append(_layer_consts(p))
    h = x[0]
    saved = []
    for l in range(DEPTH):
        h, s = _layer_fwd(h, consts[l], full[l], bias, l, hooks[l])
        saved.append(s)
    dy, lsum = _loss(h, loss_target[0])
    loss = lax.psum(0.5 / D_MODEL * jnp.sum(lsum), ("x", "y", "c"))

    gss, dbs = [None] * DEPTH, [None] * DEPTH
    dy, gw1, gss[1], dbs[1], _ = _layer_bwd(dy, saved[1], consts[1], full[1], bias, 1)
    dy, _, gss[0], dbs[0], got = _layer_bwd(dy, saved[0], consts[0], full[0], bias, 0,
                                            earlier=_send_w_in(gw1["w_in"]) + _send_rest(gw1), host_own=True)
    recv = [list(got["w_in"]) + list(got["rest"]), list(got["earlier"])]
    grad_x = dy[None]
    dbias = jnp.concatenate([d.reshape(Q_HEADS, -1) for d in dbs], axis=1)
    d_table = _mm(dbias, jnp.concatenate([onehot] * DEPTH, axis=0), "t5_bias_bwd", tm=Q_HEADS, tk=3 * BLOCK * BLOCK // 8,
                  split_a=True)

    sg = {}
    for n in ("pre_mix_norm", "b_gate", "ssd_norm", "post_mix_norm", "pre_mlp_norm", "post_mlp_norm"):
        sg[n] = jnp.stack([gss[l][n].reshape(-1) for l in range(DEPTH)])
    sg["conv_w"] = jnp.stack([gss[l]["conv"][0:SSD_CONV] for l in range(DEPTH)])
    sg["conv_b"] = jnp.stack([gss[l]["conv"][SSD_CONV] for l in range(DEPTH)])
    sg["dt_bias"] = jnp.stack([gss[l]["dt_bias"][0, 0:2 * SSD_HEADS] for l in range(DEPTH)])
    sg["a_log"] = jnp.stack([gss[l]["a_log"][0, 0:2 * SSD_HEADS] for l in range(DEPTH)])
    sg["d_skip"] = jnp.stack([gss[l]["d_skip"][0, 0:SSD_HEADS] for l in range(DEPTH)])
    sg["attn_sink"] = jnp.stack([gss[l]["attn_sink"][0, 0:Q_HEADS] for l in range(DEPTH)])
    sg["rel_bias_table"] = d_table[:, 0:N_BUCKETS].T
    (small_parts,) = _communicate(_Comm("gather", [_pack_small(sg)]), "gather_small_grads", vmem=True)
    sshape = {n: W[n].shape for n, _ in SMALL}
    sshape["conv_w"] = (DEPTH, SSD_CONV, 1, CONV_DIM)
    pk = lambda d, fill: _pack_small({n: (jnp.full((DEPTH, SSD_CONV, CONV_DIM), fill, F32) if n == "conv_w" else d[n])
                                      for n, _ in SMALL}, fill)
    s_out = [_unpack_small(o, sshape) for o in _adamw(small_parts, pk(W, 1.0), pk(M, 1.0), pk(V, 1.0), "adamw_small", tr=1024)]
    conv_g = lax.dynamic_slice_in_dim(s_out[0]["conv_w"], dev * cshard, cshard, axis=3)
    c_out = _adamw(conv_g.reshape(1, DEPTH * SSD_CONV, cshard), conv_w.reshape(-1, cshard), m_conv_w.reshape(-1, cshard),
                   v_conv_w.reshape(-1, cshard), "adamw_conv_w", tr=DEPTH * SSD_CONV)
    for i in range(4):
        s_out[i]["conv_w"] = c_out[i].reshape(conv_w.shape)

    layer_out = []
    for l in range(DEPTH):
        r_in, r_mi, r_rows = recv[l]
        lo = {"w_in": _adamw(r_in, w_in[l], m_w_in[l], v_w_in[l], f"adamw_w_in_{l}"),
              "w_mlp_in": _adamw(r_mi, w_mlp_in[l], m_w_mlp_in[l], v_w_mlp_in[l], f"adamw_w_mlp_in_{l}")}
        o = _adamw(r_rows, *[_pack_rows({n: d[n][l] for n in ROWS}, ()) for d in (W, M, V)], f"adamw_rows_{l}")
        r = 0
        for n in ROWS:
            per = shard[n][1]
            lo[n] = [a[r:r + per] for a in o]
            r += per
        layer_out.append(lo)
    b_out = {n: [jnp.stack([layer_out[l][n][i] for l in range(DEPTH)]) for i in range(4)] for n in BIG}

    outs = [loss, grad_x]
    for i in range(4):
        for n in names:
            outs.append(b_out[n][i] if n in BIG else s_out[i][n])
    return tuple(outs)
```

```python
import functools
import math

import numpy as np
import jax
import jax.numpy as jnp
from jax import lax
from jax.experimental import pallas as pl
from jax.experimental.pallas import tpu as pltpu

F32 = jnp.float32
BF16 = jnp.bfloat16

D_MODEL = 1024
DEPTH = 2
D_INNER = 2048
SSD_HEADS = 32
SSD_HEAD_DIM = 64
SSD_GROUPS = 8
SSD_REP = 4
SSD_STATE = 128
SSD_CONV = 5
CHUNK = 128
CONV_DIM = 4096
Q_HEADS = 16
KV_HEADS = 4
HEAD_DIM = 64
ATTN_WIDTH = 1024
KV_WIDTH = 256
WINDOW = 128
BLOCK = 128
N_BUCKETS = 32
MAX_DISTANCE = 128
D_FF = 4096
EPS = 1e-6
IN_SPLITS = (D_INNER, CONV_DIM, 2 * SSD_HEADS, ATTN_WIDTH, KV_WIDTH, KV_WIDTH, 2 * D_MODEL)
N_IN = sum(IN_SPLITS)
N_DEV = 8

ADAM_LR = 0.001
ADAM_B1 = 0.9
ADAM_B2 = 0.999
ADAM_EPS = 1e-08
ADAM_WD = 0.01
ADAM_STEP = 10

LANE = 128
HEAD_PAD = LANE
Z0 = 0
GT0 = 2048
Q0 = 4096
XBC0 = 6144
XS0 = 6144
B0 = 8192
C0 = 9216
K0 = 10240
V0 = 10752
DT0 = 11264
NP = 11520
QP = Q_HEADS * HEAD_PAD
KP = KV_HEADS * HEAD_PAD
NEG = -1e30
VMEM_LIMIT = 56 * 1024 * 1024


def _params(n_grid):
    return pltpu.CompilerParams(dimension_semantics=("arbitrary",) * n_grid, vmem_limit_bytes=VMEM_LIMIT)


def _dot(a, b, dims=((1,), (0,))):
    return lax.dot_general(a, b, (dims, ((), ())), preferred_element_type=F32)


def _mx(a):
    return a.astype(BF16)


def _split_dot(x, e):
    hi = x.astype(BF16)
    lo = (x - hi.astype(F32)).astype(BF16)
    return _dot(hi, e) + _dot(lo, e)


def _softplus(x):
    u = jnp.exp(-jnp.abs(x))
    w = 1.0 + u
    log1p = jnp.where(w == 1.0, u, jnp.log(w) * u / jnp.where(w == 1.0, 1.0, w - 1.0))
    return jnp.maximum(x, 0.0) + log1p


def _sigmoid(x):
    return 1.0 / (1.0 + jnp.exp(-x))


def _mm_body(*refs, grid, ta, tb, split_a, mode, comm):
    n_in = 3 if mode == "relu2_bwd" else 2
    n_out = 2 if mode == "relu2" else 1
    ins, cin, outs, cout, (acc_ref,), csem = _split_refs(refs, n_in, n_out, comm)
    a_ref, b_ref = ins[0], ins[1]
    i, j, k = pl.program_id(0), pl.program_id(1), pl.program_id(2)
    nk = grid[2]
    if comm is not None:
        comm.run(cin, cout, csem, (i * grid[1] + j) * nk + k, grid[0] * grid[1] * nk)

    dims = ((0 if ta else 1,), (1 if tb else 0,))
    b = _mx(b_ref[...])
    if split_a:
        a32 = a_ref[...].astype(F32)
        hi = a32.astype(BF16)
        lo = (a32 - hi.astype(F32)).astype(BF16)
        part = _dot(hi, b, dims) + _dot(lo, b, dims)
    else:
        part = _dot(_mx(a_ref[...]), b, dims)

    if nk > 1:
        @pl.when(k == 0)
        def _():
            acc_ref[...] = part

        @pl.when((k > 0) & (k < nk - 1))
        def _():
            acc_ref[...] += part

    @pl.when(k == nk - 1)
    def _():
        acc = part if nk == 1 else acc_ref[...] + part
        if mode == "relu2":
            outs[0][...] = acc
            r = jnp.maximum(acc, 0.0)
            outs[1][...] = (r * r).astype(outs[1].dtype)
        elif mode == "relu2_bwd":
            outs[0][...] = (acc * 2.0 * jnp.maximum(ins[2][...], 0.0)).astype(outs[0].dtype)
        else:
            outs[0][...] = acc.astype(outs[0].dtype)


def _tile(n, pref):
    t = min(n, pref)
    while n % t:
        t -= LANE
    assert t > 0 and n % t == 0, (n, pref)
    return t


def _mm(a, b, name, ta=False, tb=False, out_dtype=F32, tm=1024, tn=1280, tk=1024, split_a=False, mode=None, extra=None,
        comm=None):
    m, k = (a.shape[1], a.shape[0]) if ta else a.shape
    n = b.shape[0] if tb else b.shape[1]
    assert (b.shape[1] if tb else b.shape[0]) == k
    tm, tn, tk = _tile(m, tm), _tile(n, tn), _tile(k, tk)
    grid = (m // tm, n // tn, k // tk)
    a_spec = pl.BlockSpec((tk, tm), lambda i, j, kk: (kk, i)) if ta else pl.BlockSpec((tm, tk), lambda i, j, kk: (i, kk))
    b_spec = pl.BlockSpec((tn, tk), lambda i, j, kk: (j, kk)) if tb else pl.BlockSpec((tk, tn), lambda i, j, kk: (kk, j))
    o_spec = pl.BlockSpec((tm, tn), lambda i, j, kk: (i, j))
    in_specs, args = [a_spec, b_spec], [a, b]
    out_specs, out_shape = [o_spec], [jax.ShapeDtypeStruct((m, n), out_dtype)]
    if mode == "relu2":
        out_specs, out_shape = [o_spec, o_spec], [jax.ShapeDtypeStruct((m, n), F32), jax.ShapeDtypeStruct((m, n), BF16)]
    elif mode == "relu2_bwd":
        in_specs, args = in_specs + [o_spec], args + [extra]
    scratch = [pltpu.VMEM((tm, tn) if grid[2] > 1 else (8, LANE), F32)]
    if comm is not None:
        in_specs, args = in_specs + [ANY] * comm.n, args + comm.arrs
        out_specs, out_shape = out_specs + [ANY] * comm.n, out_shape + comm.out_shapes()
        scratch = scratch + comm.sems()
    res = pl.pallas_call(
        functools.partial(_mm_body, grid=grid, ta=ta, tb=tb, split_a=split_a, mode=mode, comm=comm),
        name=name, grid=grid, in_specs=in_specs, out_specs=out_specs, out_shape=out_shape, scratch_shapes=scratch,
        compiler_params=_params(3),
    )(*args)
    return res[0] if len(res) == 1 else res


def _rms_fwd_body(*refs, has_res):
    if has_res:
        x_ref, g_ref, r_ref, o_ref = refs
    else:
        x_ref, g_ref, o_ref = refs
    x = x_ref[...]
    y = x * lax.rsqrt(jnp.mean(x * x, axis=-1, keepdims=True) + EPS) * g_ref[...]
    if has_res:
        y = r_ref[...] + y
    o_ref[...] = y.astype(o_ref.dtype)


def _rms_fwd(x, g, name, res=None, out_dtype=F32, tr=512):
    t, d = x.shape
    tr = min(tr, t)
    row = pl.BlockSpec((tr, d), lambda i: (i, 0))
    vec = pl.BlockSpec((1, d), lambda i: (0, 0))
    args = [x, g.reshape(1, d)] + ([res] if res is not None else [])
    return pl.pallas_call(
        functools.partial(_rms_fwd_body, has_res=res is not None),
        name=name,
        grid=(t // tr,),
        in_specs=[row, vec] + ([row] if res is not None else []),
        out_specs=row,
        out_shape=jax.ShapeDtypeStruct((t, d), out_dtype),
        compiler_params=_params(1),
    )(*args)


def _rms_bwd_body(*refs, has_add):
    if has_add:
        x_ref, g_ref, dy_ref, add_ref, dx_ref, dg_ref = refs
    else:
        x_ref, g_ref, dy_ref, dx_ref, dg_ref = refs

    @pl.when(pl.program_id(0) == 0)
    def _():
        dg_ref[...] = jnp.zeros_like(dg_ref)

    x = x_ref[...]
    dy = dy_ref[...].astype(F32)
    rstd = lax.rsqrt(jnp.mean(x * x, axis=-1, keepdims=True) + EPS)
    n = x * rstd
    dn = dy * g_ref[...]
    dx = rstd * (dn - n * jnp.mean(dn * n, axis=-1, keepdims=True))
    if has_add:
        dx = dx + add_ref[...]
    dx_ref[...] = dx.astype(dx_ref.dtype)
    dg_ref[...] += jnp.sum(dy * n, axis=0, keepdims=True)


def _rms_bwd(x, g, dy, name, add=None, out_dtype=F32, tr=512):
    t, d = x.shape
    tr = min(tr, t)
    row = pl.BlockSpec((tr, d), lambda i: (i, 0))
    vec = pl.BlockSpec((1, d), lambda i: (0, 0))
    args = [x, g.reshape(1, d), dy] + ([add] if add is not None else [])
    return pl.pallas_call(
        functools.partial(_rms_bwd_body, has_add=add is not None),
        name=name,
        grid=(t // tr,),
        in_specs=[row, vec, row] + ([row] if add is not None else []),
        out_specs=[row, vec],
        out_shape=[jax.ShapeDtypeStruct((t, d), out_dtype), jax.ShapeDtypeStruct((1, d), F32)],
        compiler_params=_params(1),
    )(*args)


def _loss_body(y_ref, t_ref, dy_ref, l_ref):
    @pl.when(pl.program_id(0) == 0)
    def _():
        l_ref[...] = jnp.zeros_like(l_ref)

    e = y_ref[...] - t_ref[...]
    dy_ref[...] = e * (1.0 / D_MODEL)
    s = jnp.sum(e * e, axis=0, keepdims=True)
    acc = s[:, 0:LANE]
    for j in range(1, D_MODEL // LANE):
        acc = acc + s[:, j * LANE:(j + 1) * LANE]
    l_ref[...] += acc


def _loss(y, target, tr=512):
    t, d = y.shape
    tr = min(tr, t)
    row = pl.BlockSpec((tr, d), lambda i: (i, 0))
    return pl.pallas_call(
        functools.partial(_loss_body), name="loss_head", grid=(t // tr,), in_specs=[row, row],
        out_specs=[row, pl.BlockSpec((1, LANE), lambda i: (0, 0))],
        out_shape=[jax.ShapeDtypeStruct((t, d), F32), jax.ShapeDtypeStruct((1, LANE), F32)],
        compiler_params=_params(1),
    )(y, target)


CONV_CT = 128
CONV_RB = 512
CONV_PAD = 8


def _conv_rows(t):
    return min(CONV_RB, t)


def _conv_fwd_body(u_ref, w_ref, b_ref, o_ref, pad_ref, *, t):
    rb = _conv_rows(t)
    zeros = jnp.zeros((CONV_PAD, CONV_CT), F32)
    pad_ref[pl.ds(0, CONV_PAD), :] = zeros
    pad_ref[pl.ds(t + CONV_PAD, CONV_PAD), :] = zeros
    pad_ref[pl.ds(CONV_PAD, t), :] = u_ref[...]
    w = w_ref[...]
    b = b_ref[...]
    for c in range(t // rb):
        base = CONV_PAD + c * rb
        acc = b + w[0:1, :] * pad_ref[pl.ds(base - 2, rb), :]
        for k in range(1, SSD_CONV):
            acc = acc + w[k:k + 1, :] * pad_ref[pl.ds(base + k - 2, rb), :]
        o_ref[pl.ds(c * rb, rb), :] = acc * _sigmoid(acc)


def _conv_fwd(proj, conv_w8, conv_b):
    t = proj.shape[0]
    off = XBC0 // CONV_CT
    return pl.pallas_call(
        functools.partial(_conv_fwd_body, t=t), name="conv_fwd", grid=(CONV_DIM // CONV_CT,),
        in_specs=[pl.BlockSpec((t, CONV_CT), lambda i: (0, off + i)), pl.BlockSpec((8, CONV_CT), lambda i: (0, i)),
                  pl.BlockSpec((1, CONV_CT), lambda i: (0, i))],
        out_specs=pl.BlockSpec((t, CONV_CT), lambda i: (0, i)),
        out_shape=jax.ShapeDtypeStruct((t, CONV_DIM), F32),
        scratch_shapes=[pltpu.VMEM((t + 2 * CONV_PAD, CONV_CT), F32)],
        compiler_params=_params(1),
    )(proj, conv_w8, conv_b)


def _conv_bwd_body(u_ref, da_ref, w_ref, b_ref, du_ref, dw_ref, pad_ref, pad2_ref, *, t):
    rb = _conv_rows(t)
    zeros = jnp.zeros((CONV_PAD, CONV_CT), F32)
    for p in (pad_ref, pad2_ref):
        p[pl.ds(0, CONV_PAD), :] = zeros
        p[pl.ds(t + CONV_PAD, CONV_PAD), :] = zeros
    pad_ref[pl.ds(CONV_PAD, t), :] = u_ref[...]
    w = w_ref[...]
    b = b_ref[...]
    dw = [jnp.zeros((1, CONV_CT), F32) for _ in range(SSD_CONV + 1)]
    for c in range(t // rb):
        base = CONV_PAD + c * rb
        us = [pad_ref[pl.ds(base + k - 2, rb), :] for k in range(SSD_CONV)]
        acc = b + w[0:1, :] * us[0]
        for k in range(1, SSD_CONV):
            acc = acc + w[k:k + 1, :] * us[k]
        sg = _sigmoid(acc)
        dyc = da_ref[pl.ds(c * rb, rb), :] * (sg * (1.0 + acc * (1.0 - sg)))
        pad2_ref[pl.ds(base, rb), :] = dyc
        for k in range(SSD_CONV):
            dw[k] = dw[k] + jnp.sum(dyc * us[k], axis=0, keepdims=True)
        dw[SSD_CONV] = dw[SSD_CONV] + jnp.sum(dyc, axis=0, keepdims=True)
    for c in range(t // rb):
        base = CONV_PAD + c * rb
        acc = w[0:1, :] * pad2_ref[pl.ds(base + 2, rb), :]
        for k in range(1, SSD_CONV):
            acc = acc + w[k:k + 1, :] * pad2_ref[pl.ds(base + 2 - k, rb), :]
        du_ref[pl.ds(c * rb, rb), :] = acc.astype(du_ref.dtype)
    dw_ref[...] = jnp.concatenate(dw + [jnp.zeros((2, CONV_CT), F32)], axis=0)


def _conv_bwd(proj, dact, conv_w8, conv_b):
    t = proj.shape[0]
    off = XBC0 // CONV_CT
    col = pl.BlockSpec((t, CONV_CT), lambda i: (0, i))
    return pl.pallas_call(
        functools.partial(_conv_bwd_body, t=t), name="conv_bwd", grid=(CONV_DIM // CONV_CT,),
        in_specs=[pl.BlockSpec((t, CONV_CT), lambda i: (0, off + i)), col, pl.BlockSpec((8, CONV_CT), lambda i: (0, i)),
                  pl.BlockSpec((1, CONV_CT), lambda i: (0, i))],
        out_specs=[col, pl.BlockSpec((8, CONV_CT), lambda i: (0, i))],
        out_shape=[jax.ShapeDtypeStruct((t, CONV_DIM), BF16), jax.ShapeDtypeStruct((8, CONV_DIM), F32)],
        scratch_shapes=[pltpu.VMEM((t + 2 * CONV_PAD, CONV_CT), F32), pltpu.VMEM((t + 2 * CONV_PAD, CONV_CT), F32)],
        compiler_params=_params(1),
    )(proj, dact, conv_w8, conv_b)


GW = SSD_REP * SSD_HEAD_DIM


def _expand_np(hoff):
    e = np.zeros((LANE, D_INNER), np.float32)
    for h in range(SSD_HEADS):
        e[hoff + h, h * SSD_HEAD_DIM:(h + 1) * SSD_HEAD_DIM] = 1.0
    return e


def _head_masks():
    lane = lax.broadcasted_iota(jnp.int32, (1, GW), 1)
    return [((lane >= r * SSD_HEAD_DIM) & (lane < (r + 1) * SSD_HEAD_DIM)).astype(F32) for r in range(SSD_REP)]


def _ssd_common(dtc_ref, dtr_ref, bc_ref, br_ref, alc_ref, alr_ref, e_ref, rev):
    L = CHUNK
    ri = lax.broadcasted_iota(jnp.int32, (L, L), 0)
    ci = lax.broadcasted_iota(jnp.int32, (L, L), 1)
    tri = (ri <= ci) if rev else (ri >= ci)
    trit = (ri >= ci) if rev else (ri <= ci)
    raw_c = dtc_ref[...] + bc_ref[...]
    dt_c = _softplus(raw_c)
    a_c = -jnp.exp(alc_ref[...])
    cs_c = jnp.dot(tri.astype(F32), dt_c * a_c, precision=lax.Precision.HIGHEST, preferred_element_type=F32)
    dt_r = _softplus(dtr_ref[...] + br_ref[...])
    a_r = -jnp.exp(alr_ref[...])
    cs_r = jnp.dot(dt_r * a_r, trit.astype(F32), precision=lax.Precision.HIGHEST, preferred_element_type=F32)
    il = 0 if rev else L - 1
    e_c = jnp.exp(cs_c)
    w_c = jnp.exp(cs_c[il:il + 1, :] - cs_c)
    ex = _dot(_mx(jnp.concatenate([dt_c, e_c, w_c], axis=0)), e_ref[...])
    return dict(tri=tri, trit=trit, raw_c=raw_c, dt_c=dt_c, a_c=a_c, cs_c=cs_c, cs_r=cs_r, il=il, e_last=e_c[il:il + 1],
                dt_x=ex[0:L], e_x=ex[L:2 * L], w_x=ex[2 * L:3 * L], cd_x=ex[L + il:L + il + 1])


def _ssd_fwd_body(*refs, rev, hoff, nsteps, comm):
    L = CHUNK
    ins, cin, (y_ref, st_ref), cout, (s_scr,), csem = _split_refs(refs, 10, 2, comm)
    xs_ref, b_ref, c_ref, dtc_ref, dtr_ref, bc_ref, br_ref, alc_ref, alr_ref, e_ref = ins
    if comm is not None:
        comm.run(cin, cout, csem, pl.program_id(0), nsteps)

    @pl.when(pl.program_id(0) == 0)
    def _():
        s_scr[...] = jnp.zeros_like(s_scr)

    st_ref[0] = s_scr[...]
    q = _ssd_common(dtc_ref, dtr_ref, bc_ref, br_ref, alc_ref, alr_ref, e_ref, rev)
    masks = _head_masks()
    xdt = xs_ref[...] * q["dt_x"]
    xw = xdt * q["w_x"]
    early = []
    for g in range(SSD_GROUPS):
        gc = slice(g * GW, (g + 1) * GW)
        nc = slice(g * SSD_STATE, (g + 1) * SSD_STATE)
        bg = _mx(b_ref[:, nc])
        cg = _mx(c_ref[:, nc])
        sg = s_scr[:, gc]
        early.append((_dot(cg, bg, ((1,), (1,))), _dot(cg, _mx(sg)) * q["e_x"][:, gc]))
        s_scr[:, gc] = sg * q["cd_x"][:, gc] + _dot(bg, _mx(xw[:, gc]), ((0,), (0,)))
    for g in range(SSD_GROUPS):
        gc = slice(g * GW, (g + 1) * GW)
        cb, y = early[g]
        ms = []
        for r in range(SSD_REP):
            hh = hoff + g * SSD_REP + r
            diff = q["cs_c"][:, hh:hh + 1] - q["cs_r"][hh:hh + 1, :]
            ms.append(_mx(cb * jnp.exp(jnp.where(q["tri"], diff, NEG))))
        y4 = _dot(jnp.concatenate(ms, axis=0), _mx(xdt[:, gc]))
        for r in range(SSD_REP):
            y = y + y4[r * L:(r + 1) * L] * masks[r]
        y_ref[:, gc] = y


def _dt_rows_body(p_ref, o_ref):
    o_ref[...] = p_ref[...].T


def _dt_rows(proj, tr=512):
    t = proj.shape[0]
    tr = min(tr, t)
    return pl.pallas_call(
        functools.partial(_dt_rows_body), name="dt_rows", grid=(t // tr,),
        in_specs=[pl.BlockSpec((tr, LANE), lambda i: (i, DT0 // LANE))], out_specs=pl.BlockSpec((LANE, tr), lambda i: (0, i)),
        out_shape=jax.ShapeDtypeStruct((LANE, t), F32), compiler_params=_params(1),
    )(proj)


def _ssd_specs(t, rev):
    nc = t // CHUNK
    cm = (lambda i: nc - 1 - i) if rev else (lambda i: i)
    xs = pl.BlockSpec((CHUNK, D_INNER), lambda i: (cm(i), 0))
    bb = pl.BlockSpec((CHUNK, SSD_GROUPS * SSD_STATE), lambda i: (cm(i), 2))
    cc = pl.BlockSpec((CHUNK, SSD_GROUPS * SSD_STATE), lambda i: (cm(i), 3))
    dtc = pl.BlockSpec((CHUNK, LANE), lambda i: (cm(i), DT0 // LANE))
    dtr = pl.BlockSpec((LANE, CHUNK), lambda i: (0, cm(i)))
    colv = pl.BlockSpec((1, LANE), lambda i: (0, 0))
    rowv = pl.BlockSpec((LANE, 1), lambda i: (0, 0))
    return cm, xs, bb, cc, dtc, dtr, colv, rowv


def _ssd_fwd(xbc, proj, dtt, bias, alog, rev, comm=None):
    t = xbc.shape[0]
    nc = t // CHUNK
    cm, xs, bb, cc, dtc, dtr, colv, rowv = _ssd_specs(t, rev)
    hoff = SSD_HEADS if rev else 0
    e = jnp.asarray(_expand_np(hoff), BF16)
    in_specs = [xs, bb, cc, dtc, dtr, colv, rowv, colv, rowv, pl.BlockSpec((LANE, D_INNER), lambda i: (0, 0))]
    args = [xbc, xbc, xbc, proj, dtt, bias.reshape(1, LANE), bias.reshape(LANE, 1), alog.reshape(1, LANE), alog.reshape(LANE, 1), e]
    out_specs = [xs, pl.BlockSpec((1, SSD_STATE, D_INNER), lambda i: (cm(i), 0, 0))]
    out_shape = [jax.ShapeDtypeStruct((t, D_INNER), F32), jax.ShapeDtypeStruct((nc, SSD_STATE, D_INNER), F32)]
    scratch = [pltpu.VMEM((SSD_STATE, D_INNER), F32)]
    if comm is not None:
        in_specs, args = in_specs + [ANY] * comm.n, args + comm.arrs
        out_specs, out_shape = out_specs + [ANY] * comm.n, out_shape + comm.out_shapes()
        scratch = scratch + comm.sems()
    return pl.pallas_call(
        functools.partial(_ssd_fwd_body, rev=rev, hoff=hoff, nsteps=nc, comm=comm), name="ssd_fwd_rev" if rev else "ssd_fwd",
        grid=(nc,), in_specs=in_specs, out_specs=out_specs, out_shape=out_shape, scratch_shapes=scratch,
        compiler_params=_params(1),
    )(*args)


def _ssd_bwd_body(*refs, rev, hoff, first, nsteps, comm):
    L = CHUNK
    ins, cin, (dx_ref, ddt_ref, sm_ref), cout, (ds_scr,), csem = _split_refs(refs, 14 if first else 17, 3, comm)
    (xs_ref, b_ref, c_ref, dtc_ref, dtr_ref, bc_ref, br_ref, alc_ref, alr_ref, e_ref, et_ref, dy_ref, sp_ref,
     dsk_ref) = ins[:14]
    if not first:
        pdx_ref, pddt_ref, psm_ref = ins[14:]
    if comm is not None:
        comm.run(cin, cout, csem, pl.program_id(0), nsteps)

    @pl.when(pl.program_id(0) == 0)
    def _():
        ds_scr[...] = jnp.zeros_like(ds_scr)
        sm_ref[...] = jnp.zeros_like(sm_ref) if first else psm_ref[...]

    q = _ssd_common(dtc_ref, dtr_ref, bc_ref, br_ref, alc_ref, alr_ref, e_ref, rev)
    masks = _head_masks()
    xs = xs_ref[...]
    dy = dy_ref[...]
    xdt = xs * q["dt_x"]
    xw = xdt * q["w_x"]
    dye = dy * q["e_x"]
    ds_old = ds_scr[...]
    dxdt_parts, q1_parts, q2_parts = [], [], []
    ddiag = jnp.zeros((L, LANE), F32)
    early = []
    for g in range(SSD_GROUPS):
        gc = slice(g * GW, (g + 1) * GW)
        nc = slice(g * SSD_STATE, (g + 1) * SSD_STATE)
        bg = _mx(b_ref[:, nc])
        cg = _mx(c_ref[:, nc])
        dy_g = _mx(dy[:, gc])
        xdt_g = _mx(xdt[:, gc])
        dsg = ds_old[:, gc]
        spg = _mx(sp_ref[0, :, gc])
        dye_g = _mx(dye[:, gc])
        bds = _dot(bg, _mx(dsg))
        early.append(dict(
            bg=bg, cg=cg, dy_g=dy_g, bds=bds,
            cb=_dot(cg, bg, ((1,), (1,))), cbt=_dot(bg, cg, ((1,), (1,))),
            dm=_dot(_mx(jnp.concatenate([dy[:, gc] * m for m in masks], axis=0)), xdt_g, ((1,), (1,))),
            dmt=_dot(_mx(jnp.concatenate([xdt[:, gc] * m for m in masks], axis=0)), dy_g, ((1,), (1,))),
            dc=_dot(dye_g, spg, ((1,), (1,))), db=_dot(_mx(xw[:, gc]), _mx(dsg), ((1,), (1,)))))
        q1_parts.append(dye[:, gc] * _dot(cg, spg))
        q2_parts.append(xw[:, gc] * bds)
        ds_scr[:, gc] = dsg * q["cd_x"][:, gc] + _dot(cg, dye_g, ((0,), (0,)))
    for g in range(SSD_GROUPS):
        gc = slice(g * GW, (g + 1) * GW)
        e = early[g]
        bg, cg, dy_g, bds, cb, cbt, dm, dmt = (e[k] for k in ("bg", "cg", "dy_g", "bds", "cb", "cbt", "dm", "dmt"))
        dcb = jnp.zeros((L, L), F32)
        dcbt = jnp.zeros((L, L), F32)
        mts = []
        for r in range(SSD_REP):
            hh = hoff + g * SSD_REP + r
            col = q["cs_c"][:, hh:hh + 1]
            row = q["cs_r"][hh:hh + 1, :]
            dec = jnp.exp(jnp.where(q["tri"], col - row, NEG))
            dect = jnp.exp(jnp.where(q["trit"], row - col, NEG))
            pd = dm[r * L:(r + 1) * L] * dec
            pdt = dmt[r * L:(r + 1) * L] * dect
            dcb = dcb + pd
            dcbt = dcbt + pdt
            mts.append(_mx(cbt * dect))
            ddiag = ddiag + (jnp.sum(pd * cb, axis=1, keepdims=True) - jnp.sum(pdt * cbt, axis=1, keepdims=True)) * _onehot_lane(hh)
        x4 = _dot(jnp.concatenate(mts, axis=0), dy_g)
        dxdt_g = bds * q["w_x"][:, gc]
        for r in range(SSD_REP):
            dxdt_g = dxdt_g + x4[r * L:(r + 1) * L] * masks[r]
        dxdt_parts.append(dxdt_g)
        dc_g = _dot(_mx(dcb), bg) + e["dc"]
        db_g = _dot(_mx(dcbt), cg) + e["db"]
        boff = D_INNER + g * SSD_STATE
        coff = D_INNER + SSD_GROUPS * SSD_STATE + g * SSD_STATE
        if first:
            dx_ref[:, boff:boff + SSD_STATE] = db_g
            dx_ref[:, coff:coff + SSD_STATE] = dc_g
        else:
            dx_ref[:, boff:boff + SSD_STATE] = pdx_ref[:, boff:boff + SSD_STATE] + db_g
            dx_ref[:, coff:coff + SSD_STATE] = pdx_ref[:, coff:coff + SSD_STATE] + dc_g
    dxdt = jnp.concatenate(dxdt_parts, axis=1)
    et = et_ref[...]
    hs = _dot(_mx(jnp.concatenate([jnp.concatenate(q1_parts, axis=1), jnp.concatenate(q2_parts, axis=1), dxdt * xs], axis=0)), et)
    q1, q2, r3 = hs[0:L], hs[L:2 * L], hs[2 * L:3 * L]
    t_prev = _split_dot(jnp.sum(ds_old * sp_ref[0], axis=0, keepdims=True), et) * q["e_last"]
    rows = lax.broadcasted_iota(jnp.int32, (L, LANE), 0)
    dcs = ddiag + q1 - q2 + jnp.where(rows == q["il"], jnp.sum(q2, axis=0, keepdims=True) + t_prev, 0.0)
    dad = jnp.dot(q["trit"].astype(F32), dcs, precision=lax.Precision.HIGHEST, preferred_element_type=F32)
    ddt = dad * q["a_c"] + r3
    ddt_raw = ddt * _sigmoid(q["raw_c"])
    dal = jnp.sum(dad * q["dt_c"], axis=0, keepdims=True) * q["a_c"]
    dbias = jnp.sum(ddt_raw, axis=0, keepdims=True)
    sm_ref[...] += jnp.concatenate([dal, dbias, jnp.zeros((6, LANE), F32)], axis=0)
    dxs = dxdt * q["dt_x"]
    if first:
        dx_ref[:, 0:D_INNER] = dxs + dy * dsk_ref[...]
        ddt_ref[...] = ddt_raw
    else:
        dx_ref[:, 0:D_INNER] = pdx_ref[:, 0:D_INNER] + dxs
        ddt_ref[...] = pddt_ref[...] + ddt_raw


def _ssd_bwd(xbc, proj, dtt, bias, alog, dy, states, dskip_x, rev, prev=None, comm=None):
    t = xbc.shape[0]
    nc = t // CHUNK
    first = prev is None
    cm, xs, bb, cc, dtc, dtr, colv, rowv = _ssd_specs(t, not rev)
    hoff = SSD_HEADS if rev else 0
    e_np = _expand_np(hoff)
    e = jnp.asarray(e_np, BF16)
    et = jnp.asarray(e_np.T, BF16)
    st_spec = pl.BlockSpec((1, SSD_STATE, D_INNER), lambda i: (cm(i), 0, 0))
    dxo = pl.BlockSpec((CHUNK, CONV_DIM), lambda i: (cm(i), 0))
    ddto = pl.BlockSpec((CHUNK, LANE), lambda i: (cm(i), 0))
    smo = pl.BlockSpec((8, LANE), lambda i: (0, 0))
    in_specs = [xs, bb, cc, dtc, dtr, colv, rowv, colv, rowv, pl.BlockSpec((LANE, D_INNER), lambda i: (0, 0)),
                pl.BlockSpec((D_INNER, LANE), lambda i: (0, 0)), xs, st_spec, pl.BlockSpec((1, D_INNER), lambda i: (0, 0))]
    args = [xbc, xbc, xbc, proj, dtt, bias.reshape(1, LANE), bias.reshape(LANE, 1), alog.reshape(1, LANE),
            alog.reshape(LANE, 1), e, et, dy, states, dskip_x]
    if not first:
        in_specs += [dxo, ddto, smo]
        args += list(prev)
    out_specs = [dxo, ddto, smo]
    out_shape = [jax.ShapeDtypeStruct((t, CONV_DIM), F32), jax.ShapeDtypeStruct((t, LANE), F32),
                 jax.ShapeDtypeStruct((8, LANE), F32)]
    scratch = [pltpu.VMEM((SSD_STATE, D_INNER), F32)]
    if comm is not None:
        in_specs, args = in_specs + [ANY] * comm.n, args + comm.arrs
        out_specs, out_shape = out_specs + [ANY] * comm.n, out_shape + comm.out_shapes()
        scratch = scratch + comm.sems()
    return pl.pallas_call(
        functools.partial(_ssd_bwd_body, rev=rev, hoff=hoff, first=first, nsteps=nc, comm=comm),
        name="ssd_bwd_rev" if rev else "ssd_bwd", grid=(nc,), in_specs=in_specs, out_specs=out_specs, out_shape=out_shape,
        scratch_shapes=scratch, compiler_params=_params(1),
    )(*args)


def _gnorm_parts(yf_ref, yb_ref, xs_ref, z_ref, dsk_ref):
    xs = xs_ref[...]
    y = yf_ref[...] + yb_ref[...] + xs * dsk_ref[...]
    z = z_ref[...]
    sg = _sigmoid(z)
    s = z * sg
    v = y * s
    rs = []
    for g in range(SSD_GROUPS):
        vg = v[:, g * GW:(g + 1) * GW]
        rs.append(jnp.broadcast_to(lax.rsqrt(jnp.mean(vg * vg, axis=-1, keepdims=True) + EPS), vg.shape))
    return xs, y, z, sg, s, v, jnp.concatenate(rs, axis=1)


def _gnorm_fwd_body(yf_ref, yb_ref, xs_ref, z_ref, dsk_ref, w_ref, o_ref):
    _, _, _, _, _, v, rstd = _gnorm_parts(yf_ref, yb_ref, xs_ref, z_ref, dsk_ref)
    o_ref[...] = (v * rstd * w_ref[...]).astype(o_ref.dtype)


def _gnorm_fwd(yf, yb, xbc, proj, dskip_x, norm_w, tr=256):
    t = yf.shape[0]
    tr = min(tr, t)
    row = pl.BlockSpec((tr, D_INNER), lambda i: (i, 0))
    vec = pl.BlockSpec((1, D_INNER), lambda i: (0, 0))
    return pl.pallas_call(
        functools.partial(_gnorm_fwd_body), name="gnorm_fwd", grid=(t // tr,),
        in_specs=[row, row, row, pl.BlockSpec((tr, D_INNER), lambda i: (i, Z0 // D_INNER)), vec, vec],
        out_specs=row, out_shape=jax.ShapeDtypeStruct((t, D_INNER), BF16), compiler_params=_params(1),
    )(yf, yb, xbc, proj, dskip_x, norm_w.reshape(1, D_INNER))


def _gnorm_bwd_body(du_ref, yf_ref, yb_ref, xs_ref, z_ref, dsk_ref, w_ref, et_ref, dy_ref, dz_ref, dw_ref, dd_ref, acc_ref, *, nsteps):
    i = pl.program_id(0)

    @pl.when(i == 0)
    def _():
        dw_ref[...] = jnp.zeros_like(dw_ref)
        acc_ref[...] = jnp.zeros_like(acc_ref)

    xs, y, z, sg, s, v, rstd = _gnorm_parts(yf_ref, yb_ref, xs_ref, z_ref, dsk_ref)
    du = du_ref[...]
    n = v * rstd
    dn = du * w_ref[...]
    dw_ref[...] += jnp.sum(du * n, axis=0, keepdims=True)
    prod = dn * n
    means = []
    for g in range(SSD_GROUPS):
        pg = prod[:, g * GW:(g + 1) * GW]
        means.append(jnp.broadcast_to(jnp.mean(pg, axis=-1, keepdims=True), pg.shape))
    dv = rstd * (dn - n * jnp.concatenate(means, axis=1))
    dy = dv * s
    dy_ref[...] = dy
    dz_ref[...] = (dv * y * (sg * (1.0 + z * (1.0 - sg)))).astype(dz_ref.dtype)
    acc_ref[...] += jnp.sum(dy * xs, axis=0, keepdims=True)

    @pl.when(i == nsteps - 1)
    def _():
        dd_ref[...] = _split_dot(acc_ref[...], et_ref[...])


def _gnorm_bwd(du, yf, yb, xbc, proj, dskip_x, norm_w, tr=256):
    t = yf.shape[0]
    tr = min(tr, t)
    row = pl.BlockSpec((tr, D_INNER), lambda i: (i, 0))
    vec = pl.BlockSpec((1, D_INNER), lambda i: (0, 0))
    et = jnp.asarray(_expand_np(0).T, BF16)
    return pl.pallas_call(
        functools.partial(_gnorm_bwd_body, nsteps=t // tr), name="gnorm_bwd", grid=(t // tr,),
        in_specs=[row, row, row, row, pl.BlockSpec((tr, D_INNER), lambda i: (i, Z0 // D_INNER)), vec, vec,
                  pl.BlockSpec((D_INNER, LANE), lambda i: (0, 0))],
        out_specs=[row, row, vec, pl.BlockSpec((1, LANE), lambda i: (0, 0))],
        out_shape=[jax.ShapeDtypeStruct((t, D_INNER), F32), jax.ShapeDtypeStruct((t, D_INNER), BF16),
                   jax.ShapeDtypeStruct((1, D_INNER), F32), jax.ShapeDtypeStruct((1, LANE), F32)],
        scratch_shapes=[pltpu.VMEM((1, D_INNER), F32)],
        compiler_params=_params(1),
    )(du, yf, yb, xbc, proj, dskip_x, norm_w.reshape(1, D_INNER), et)


def _gate_fwd_body(gt_ref, b_ref, ys_ref, ya_ref, o_ref):
    g = _sigmoid(gt_ref[...] + b_ref[...])
    o_ref[...] = (g[:, 0:D_MODEL] * ys_ref[...] + g[:, D_MODEL:] * ya_ref[...]).astype(o_ref.dtype)


def _gate_fwd(proj, b_gate, y_ssd, y_attn, tr=512):
    t = y_ssd.shape[0]
    tr = min(tr, t)
    row = pl.BlockSpec((tr, D_MODEL), lambda i: (i, 0))
    return pl.pallas_call(
        functools.partial(_gate_fwd_body), name="gate_fwd", grid=(t // tr,),
        in_specs=[pl.BlockSpec((tr, 2 * D_MODEL), lambda i: (i, GT0 // (2 * D_MODEL))),
                  pl.BlockSpec((1, 2 * D_MODEL), lambda i: (0, 0)), row, row],
        out_specs=row, out_shape=jax.ShapeDtypeStruct((t, D_MODEL), BF16), compiler_params=_params(1),
    )(proj, b_gate.reshape(1, 2 * D_MODEL), y_ssd, y_attn)


def _gate_bwd_body(dm_ref, gt_ref, b_ref, ys_ref, ya_ref, dys_ref, dya_ref, dgt_ref, db_ref):
    @pl.when(pl.program_id(0) == 0)
    def _():
        db_ref[...] = jnp.zeros_like(db_ref)

    g = _sigmoid(gt_ref[...] + b_ref[...])
    gs = g[:, 0:D_MODEL]
    ga = g[:, D_MODEL:]
    dm = dm_ref[...]
    dys_ref[...] = (dm * gs).astype(dys_ref.dtype)
    dya_ref[...] = (dm * ga).astype(dya_ref.dtype)
    dgt = jnp.concatenate([dm * ys_ref[...] * gs * (1.0 - gs), dm * ya_ref[...] * ga * (1.0 - ga)], axis=1)
    dgt_ref[...] = dgt.astype(dgt_ref.dtype)
    db_ref[...] += jnp.sum(dgt, axis=0, keepdims=True)


def _gate_bwd(dmix, proj, b_gate, y_ssd, y_attn, tr=512):
    t = y_ssd.shape[0]
    tr = min(tr, t)
    row = pl.BlockSpec((tr, D_MODEL), lambda i: (i, 0))
    row2 = pl.BlockSpec((tr, 2 * D_MODEL), lambda i: (i, 0))
    vec2 = pl.BlockSpec((1, 2 * D_MODEL), lambda i: (0, 0))
    return pl.pallas_call(
        functools.partial(_gate_bwd_body), name="gate_bwd", grid=(t // tr,),
        in_specs=[row, pl.BlockSpec((tr, 2 * D_MODEL), lambda i: (i, GT0 // (2 * D_MODEL))), vec2, row, row],
        out_specs=[row, row, row2, vec2],
        out_shape=[jax.ShapeDtypeStruct((t, D_MODEL), BF16), jax.ShapeDtypeStruct((t, D_MODEL), BF16),
                   jax.ShapeDtypeStruct((t, 2 * D_MODEL), BF16), jax.ShapeDtypeStruct((1, 2 * D_MODEL), F32)],
        compiler_params=_params(1),
    )(dmix, proj, b_gate.reshape(1, 2 * D_MODEL), y_ssd, y_attn)


def _bucket_np():
    i = np.arange(BLOCK)[:, None]
    j = np.arange(3 * BLOCK)[None, :]
    rel = j - BLOCK - i
    nb = N_BUCKETS // 2
    max_exact = nb // 2
    ret = np.where(rel > 0, nb, 0)
    n = np.abs(rel)
    nf = np.maximum(n, 1).astype(np.float32)
    large = max_exact + (np.log(nf / np.float32(max_exact)) / np.float32(math.log(MAX_DISTANCE / max_exact))
                         * np.float32(nb - max_exact)).astype(np.int32)
    large = np.minimum(large, nb - 1)
    bucket = ret + np.where(n < max_exact, n, large)
    band = np.abs(rel) <= WINDOW
    return bucket, band


def _onehot_np():
    bucket, _ = _bucket_np()
    oh = np.zeros((BLOCK * 3 * BLOCK, LANE), np.float32)
    oh[np.arange(oh.shape[0]), bucket.reshape(-1)] = 1.0
    return oh


REP = Q_HEADS // KV_HEADS


def _group_cols(ref, g):
    return jnp.concatenate([ref[:, (g * REP + r) * HEAD_PAD:(g * REP + r + 1) * HEAD_PAD] for r in range(REP)], axis=0)


def _group_lane(ref, g, rows):
    return jnp.concatenate([jnp.broadcast_to(ref[:, g * REP + r:g * REP + r + 1], (rows, 1)) for r in range(REP)], axis=0)


def _attn_kv(refs, g):
    gc = slice(g * HEAD_PAD, (g + 1) * HEAD_PAD)
    return _mx(jnp.concatenate([r[:, gc] for r in refs], axis=0))


def _attn_logits(q_ref, kg, bias_ref, edge, g):
    q4 = _mx(_group_cols(q_ref, g))
    s = _dot(q4, kg, ((1,), (1,))) * (HEAD_DIM ** -0.5) + bias_ref[g] + edge
    return q4, s


def _attn_edge(n, nb):
    col = lax.broadcasted_iota(jnp.int32, (1, 3 * BLOCK), 1)
    bad = ((col < BLOCK) & (n == 0)) | ((col >= 2 * BLOCK) & (n == nb - 1))
    return jnp.where(bad, NEG, 0.0)


def _onehot_lane(h):
    return (lax.broadcasted_iota(jnp.int32, (1, LANE), 1) == h).astype(F32)


def _attn_fwd_body(*refs, nb, comm):
    ins, cin, (o_ref, lse_ref), cout, _, csem = _split_refs(refs, 9, 2, comm)
    q_ref, kp_ref, kc_ref, kn_ref, vp_ref, vc_ref, vn_ref, bias_ref, sink_ref = ins
    n = pl.program_id(0)
    if comm is not None:
        comm.run(cin, cout, csem, n, nb)
    edge = _attn_edge(n, nb)
    lse = jnp.zeros((BLOCK, LANE), F32)
    logits = [_attn_logits(q_ref, _attn_kv((kp_ref, kc_ref, kn_ref), g), bias_ref, edge, g)[1] for g in range(KV_HEADS)]
    for g in range(KV_HEADS):
        vg = _attn_kv((vp_ref, vc_ref, vn_ref), g)
        s = logits[g]
        sink = _group_lane(sink_ref, g, BLOCK)
        m = jnp.maximum(jnp.max(s, axis=-1, keepdims=True), sink)
        p = jnp.exp(s - m)
        den = jnp.sum(p, axis=-1, keepdims=True) + jnp.exp(sink - m)
        o4 = _dot(_mx(p * (1.0 / den)), vg)
        l4 = m + jnp.log(den)
        for r in range(REP):
            h = g * REP + r
            o_ref[:, h * HEAD_PAD:(h + 1) * HEAD_PAD] = o4[r * BLOCK:(r + 1) * BLOCK].astype(o_ref.dtype)
            lse = lse + l4[r * BLOCK:(r + 1) * BLOCK] * _onehot_lane(h)
    lse_ref[...] = lse


def _attn_specs(t, clamp):
    nb = t // BLOCK
    cur = (lambda n: jnp.minimum(n, nb - 1)) if clamp else (lambda n: n)
    prv = lambda n: jnp.maximum(cur(n) - 1, 0)
    nxt = lambda n: jnp.minimum(cur(n) + 1, nb - 1)
    kb, vb = K0 // KP, V0 // KP
    qs = pl.BlockSpec((BLOCK, QP), lambda n: (cur(n), Q0 // QP))
    ks = [pl.BlockSpec((BLOCK, KP), lambda n, f=f: (f(n), kb)) for f in (prv, cur, nxt)]
    vs = [pl.BlockSpec((BLOCK, KP), lambda n, f=f: (f(n), vb)) for f in (prv, cur, nxt)]
    consts = [pl.BlockSpec((KV_HEADS, REP * BLOCK, 3 * BLOCK), lambda n: (0, 0, 0)), pl.BlockSpec((1, LANE), lambda n: (0, 0))]
    return nb, cur, qs, ks, vs, consts


def _attn_fwd(proj, bias, sink, comm=None):
    t = proj.shape[0]
    nb, cur, qs, ks, vs, consts = _attn_specs(t, False)
    in_specs, args = [qs] + ks + vs + consts, [proj] * 7 + [bias, sink]
    out_specs = [pl.BlockSpec((BLOCK, QP), lambda n: (n, 0)), pl.BlockSpec((BLOCK, LANE), lambda n: (n, 0))]
    out_shape = [jax.ShapeDtypeStruct((t, QP), BF16), jax.ShapeDtypeStruct((t, LANE), F32)]
    scratch = []
    if comm is not None:
        in_specs, args = in_specs + [ANY] * comm.n, args + comm.arrs
        out_specs, out_shape = out_specs + [ANY] * comm.n, out_shape + comm.out_shapes()
        scratch = comm.sems()
    return pl.pallas_call(
        functools.partial(_attn_fwd_body, nb=nb, comm=comm), name="attn_fwd", grid=(nb,),
        in_specs=in_specs, out_specs=out_specs, out_shape=out_shape, scratch_shapes=scratch, compiler_params=_params(1),
    )(*args)


def _attn_bwd_body(*refs, nb, comm):
    ins, cin, (dq_ref, dk_ref, dv_ref, dbias_ref, dsink_ref), cout, (ak, bk, av, bv), csem = _split_refs(refs, 12, 5, comm)
    q_ref, kp_ref, kc_ref, kn_ref, vp_ref, vc_ref, vn_ref, bias_ref, sink_ref, do_ref, o_ref, lse_ref = ins
    n = pl.program_id(0)
    scale = HEAD_DIM ** -0.5
    if comm is not None:
        comm.run(cin, cout, csem, n, nb + 1)

    @pl.when(n == 0)
    def _():
        for r in (ak, bk, av, bv, dbias_ref, dsink_ref):
            r[...] = jnp.zeros_like(r)

    @pl.when(n < nb)
    def _():
        edge = _attn_edge(n, nb)
        dsink = jnp.zeros((1, LANE), F32)
        early = []
        for g in range(KV_HEADS):
            kg = _attn_kv((kp_ref, kc_ref, kn_ref), g)
            q4, s = _attn_logits(q_ref, kg, bias_ref, edge, g)
            do = _group_cols(do_ref, g)
            early.append((kg, q4, s, do, _dot(_mx(do), _attn_kv((vp_ref, vc_ref, vn_ref), g), ((1,), (1,)))))
        for g in range(KV_HEADS):
            gc = slice(g * HEAD_PAD, (g + 1) * HEAD_PAD)
            kg, q4, s, do, dp = early[g]
            lse = _group_lane(lse_ref, g, BLOCK)
            p = jnp.exp(s - lse)
            delta = jnp.sum(do * _group_cols(o_ref, g).astype(F32), axis=-1, keepdims=True)
            ds = p * (dp - delta)
            sunk = jnp.exp(_group_lane(sink_ref, g, BLOCK) - lse) * delta
            dbias_ref[g] += ds
            dsb = _mx(ds)
            dq4 = _dot(dsb, kg) * scale
            for r in range(REP):
                h = g * REP + r
                dq_ref[:, h * HEAD_PAD:(h + 1) * HEAD_PAD] = dq4[r * BLOCK:(r + 1) * BLOCK].astype(dq_ref.dtype)
                dsink = dsink - jnp.sum(sunk[r * BLOCK:(r + 1) * BLOCK], axis=0, keepdims=True) * _onehot_lane(h)
            dk_c = _dot(dsb, q4, ((0,), (0,))) * scale
            dv_c = _dot(_mx(p), _mx(do), ((0,), (0,)))
            for acc_a, acc_b, out, c in ((ak, bk, dk_ref, dk_c), (av, bv, dv_ref, dv_c)):
                out[:, gc] = (acc_a[:, gc] + c[0:BLOCK]).astype(out.dtype)
                acc_a[:, gc] = acc_b[:, gc] + c[BLOCK:2 * BLOCK]
                acc_b[:, gc] = c[2 * BLOCK:3 * BLOCK]
        dsink_ref[...] += dsink

    @pl.when(n == nb)
    def _():
        dk_ref[...] = ak[...].astype(dk_ref.dtype)
        dv_ref[...] = av[...].astype(dv_ref.dtype)


def _attn_bwd(proj, bias, sink, do, o, lse, comm=None):
    t = proj.shape[0]
    nb, cur, qs, ks, vs, consts = _attn_specs(t, True)
    rowq = pl.BlockSpec((BLOCK, QP), lambda n: (cur(n), 0))
    late = pl.BlockSpec((BLOCK, KP), lambda n: (jnp.maximum(n - 1, 0), 0))
    in_specs = [qs] + ks + vs + consts + [rowq, rowq, pl.BlockSpec((BLOCK, LANE), lambda n: (cur(n), 0))]
    args = [proj] * 7 + [bias, sink, do, o, lse]
    out_specs = [rowq, late, late, pl.BlockSpec((KV_HEADS, REP * BLOCK, 3 * BLOCK), lambda n: (0, 0, 0)),
                 pl.BlockSpec((1, LANE), lambda n: (0, 0))]
    out_shape = [jax.ShapeDtypeStruct((t, QP), BF16), jax.ShapeDtypeStruct((t, KP), BF16), jax.ShapeDtypeStruct((t, KP), BF16),
                 jax.ShapeDtypeStruct((KV_HEADS, REP * BLOCK, 3 * BLOCK), F32), jax.ShapeDtypeStruct((1, LANE), F32)]
    scratch = [pltpu.VMEM((BLOCK, KP), F32) for _ in range(4)]
    if comm is not None:
        in_specs, args = in_specs + [ANY] * comm.n, args + comm.arrs
        out_specs, out_shape = out_specs + [ANY] * comm.n, out_shape + comm.out_shapes()
        scratch = scratch + comm.sems()
    return pl.pallas_call(
        functools.partial(_attn_bwd_body, nb=nb, comm=comm), name="attn_bwd", grid=(nb + 1,),
        in_specs=in_specs, out_specs=out_specs, out_shape=out_shape, scratch_shapes=scratch, compiler_params=_params(1),
    )(*args)


def _adamw_body(gp_ref, w_ref, m_ref, v_ref, g_ref, d_ref, nm_ref, nv_ref, *, nparts):
    g = gp_ref[0].astype(F32)
    for j in range(1, nparts):
        g = g + gp_ref[j].astype(F32)
    m = ADAM_B1 * m_ref[...] + (1.0 - ADAM_B1) * g
    v = ADAM_B2 * v_ref[...] + (1.0 - ADAM_B2) * (g * g)
    m_hat = m / (1.0 - ADAM_B1 ** ADAM_STEP)
    v_hat = v / (1.0 - ADAM_B2 ** ADAM_STEP)
    g_ref[...] = g
    d_ref[...] = -ADAM_LR * (m_hat / (jnp.sqrt(v_hat) + ADAM_EPS) + ADAM_WD * w_ref[...])
    nm_ref[...] = m
    nv_ref[...] = v


def _adamw(gparts, w, m, v, name, tr=256):
    p, r, c = gparts.shape
    tr = min(tr, r)
    assert r % tr == 0
    row = pl.BlockSpec((tr, c), lambda i: (i, 0))
    sd = jax.ShapeDtypeStruct((r, c), F32)
    return pl.pallas_call(
        functools.partial(_adamw_body, nparts=p), name=name, grid=(r // tr,),
        in_specs=[pl.BlockSpec((p, tr, c), lambda i: (0, i, 0)), row, row, row],
        out_specs=[row, row, row, row], out_shape=[sd, sd, sd, sd], compiler_params=_params(1),
    )(gparts, w, m, v)


MESH = pl.DeviceIdType.MESH
N_REL = N_DEV - 1


def _gather_phases(ins, outs, sems):
    n = len(ins)
    send_sems, recv_sems, local_sems = sems
    x, y, c = lax.axis_index("x"), lax.axis_index("y"), lax.axis_index("c")
    me, sibling = (x, y, c), (x, y, 1 - c)
    chips = [(1 - x, y), (x, 1 - y), (1 - x, 1 - y)]

    def slot(a, p):
        return outs[a].at[4 * p[0] + 2 * p[1] + p[2]]

    def copy(a, k, block, to, src=None):
        return pltpu.make_async_remote_copy(
            src_ref=slot(a, block) if src is None else src, dst_ref=slot(a, block),
            send_sem=send_sems.at[a * N_REL + k], recv_sem=recv_sems.at[a * N_REL + k], device_id=to, device_id_type=MESH)

    mine = [pltpu.make_async_copy(ins[a], slot(a, me), local_sems.at[a]) for a in range(n)]
    first = []
    for a in range(n):
        first.append(copy(a, 0, me, sibling, src=ins[a]))
        first += [copy(a, 1 + j, me, (*chip, c), src=ins[a]) for j, chip in enumerate(chips)]
    passed = [copy(a, 4 + j, (*chip, c), sibling) for j, chip in enumerate(chips) for a in range(n)]

    def start():
        for cp in mine + first:
            cp.start()

    def pass_on():
        i = 0
        for j, chip in enumerate(chips):
            for a in range(n):
                copy(a, 1 + j, (*chip, c), me).wait_recv()
                passed[i].start()
                i += 1

    def finish():
        for a in range(n):
            copy(a, 0, sibling, me).wait_recv()
            for j, chip in enumerate(chips):
                copy(a, 4 + j, (*chip, 1 - c), me).wait_recv()
        for cp in first + passed:
            cp.wait_send()
        for cp in mine:
            cp.wait()

    return start, pass_on, finish


def _exchange_phases(ins, outs, sems):
    n = len(ins)
    send_sems, recv_sems, local_sems = sems
    x, y, c = lax.axis_index("x"), lax.axis_index("y"), lax.axis_index("c")
    me = 4 * x + 2 * y + c
    mine = [pltpu.make_async_copy(ins[a].at[me], outs[a].at[me], local_sems.at[a]) for a in range(n)]
    copies = []
    for a in range(n):
        for k in range(1, N_DEV):
            px = 1 - x if k & 4 else x
            py = 1 - y if k & 2 else y
            pc = 1 - c if k & 1 else c
            peer = 4 * px + 2 * py + pc
            send = pltpu.make_async_remote_copy(
                src_ref=ins[a].at[peer], dst_ref=outs[a].at[me], send_sem=send_sems.at[a * N_REL + k - 1],
                recv_sem=recv_sems.at[a * N_REL + k - 1], device_id=(px, py, pc), device_id_type=MESH)
            recv = pltpu.make_async_remote_copy(
                src_ref=ins[a].at[me], dst_ref=outs[a].at[peer], send_sem=send_sems.at[a * N_REL + k - 1],
                recv_sem=recv_sems.at[a * N_REL + k - 1], device_id=(px, py, pc), device_id_type=MESH)
            copies.append((send, recv))

    def start():
        for cp in mine:
            cp.start()
        for send, _ in copies:
            send.start()

    def finish():
        for send, recv in copies:
            send.wait_send()
            recv.wait_recv()
        for cp in mine:
            cp.wait()

    return start, None, finish


class _Comm:
    def __init__(self, kind, arrs):
        self.kind, self.arrs, self.n = kind, list(arrs), len(arrs)

    def out_shapes(self):
        if self.kind == "gather":
            return [jax.ShapeDtypeStruct((N_DEV,) + a.shape, a.dtype) for a in self.arrs]
        return [jax.ShapeDtypeStruct(a.shape, a.dtype) for a in self.arrs]

    def sems(self):
        return [pltpu.SemaphoreType.DMA((self.n * N_REL,)), pltpu.SemaphoreType.DMA((self.n * N_REL,)),
                pltpu.SemaphoreType.DMA((self.n,))]

    def run(self, ins, outs, sems, step, nsteps):
        start, mid, finish = (_gather_phases if self.kind == "gather" else _exchange_phases)(ins, outs, sems)
        if nsteps is None:
            start()
            if mid is not None:
                mid()
            finish()
            return
        pl.when(step == 0)(start)
        if mid is not None:
            pl.when(step == nsteps // 2)(mid)
        pl.when(step == nsteps - 1)(finish)


def _split_refs(refs, n_in, n_out, comm):
    k = comm.n if comm is not None else 0
    i = 0
    parts = []
    for cnt in (n_in, k, n_out, k):
        parts.append(refs[i:i + cnt])
        i += cnt
    rest = refs[i:]
    if comm is None:
        return parts[0], (), parts[2], (), rest, ()
    return parts[0], parts[1], parts[2], parts[3], rest[:len(rest) - 3], rest[len(rest) - 3:]


def _comm_body(*refs, comm):
    n = comm.n
    comm.run(refs[:n], refs[n:2 * n], refs[2 * n:], None, None)


def _communicate(comm, name, vmem=False):
    spec = pl.BlockSpec(memory_space=pltpu.VMEM if vmem else pl.ANY)
    return pl.pallas_call(
        functools.partial(_comm_body, comm=comm), name=name,
        in_specs=[spec] * comm.n, out_specs=[spec] * comm.n, out_shape=comm.out_shapes(), scratch_shapes=comm.sems(),
        compiler_params=pltpu.CompilerParams(vmem_limit_bytes=VMEM_LIMIT),
    )(*comm.arrs)


ANY = pl.BlockSpec(memory_space=pl.ANY)


def _pad_heads(w, axis):
    shp = w.shape
    heads = shp[axis] // HEAD_DIM
    w = w.reshape(shp[:axis] + (heads, HEAD_DIM) + shp[axis + 1:])
    pad = [(0, 0)] * w.ndim
    pad[axis + 1] = (0, HEAD_PAD - HEAD_DIM)
    w = jnp.pad(w, pad)
    return w.reshape(shp[:axis] + (heads * HEAD_PAD,) + shp[axis + 1:])


def _unpad_heads(w, axis):
    shp = w.shape
    heads = shp[axis] // HEAD_PAD
    w = w.reshape(shp[:axis] + (heads, HEAD_PAD) + shp[axis + 1:])
    w = lax.slice_in_dim(w, 0, HEAD_DIM, axis=axis + 1)
    return w.reshape(shp[:axis] + (heads * HEAD_DIM,) + shp[axis + 1:])


def _w_in_to_padded(w):
    idx = np.cumsum((0,) + IN_SPLITS)
    z, xbc, dt, q, k, v, gates = [w[:, idx[i]:idx[i + 1]] for i in range(7)]
    zeros = lambda n: jnp.zeros((w.shape[0], n), w.dtype)
    return jnp.concatenate([z, gates, _pad_heads(q, 1), xbc, _pad_heads(k, 1), _pad_heads(v, 1), dt, zeros(LANE - 2 * SSD_HEADS),
                            zeros(NP - DT0 - LANE)], axis=1)


def _w_in_from_padded(w):
    z = w[:, Z0:Z0 + D_INNER]
    gates = w[:, GT0:GT0 + 2 * D_MODEL]
    q = _unpad_heads(w[:, Q0:Q0 + QP], 1)
    xbc = w[:, XBC0:XBC0 + CONV_DIM]
    k = _unpad_heads(w[:, K0:K0 + KP], 1)
    v = _unpad_heads(w[:, V0:V0 + KP], 1)
    dt = w[:, DT0:DT0 + 2 * SSD_HEADS]
    return jnp.concatenate([z, xbc, dt, q, k, v, gates], axis=1)


def _pad_lane(v):
    v = v.reshape(-1)
    return jnp.pad(v, (0, LANE - v.shape[0]))


def _layer_consts(p):
    c = dict(p)
    c["conv_w8"] = jnp.pad(p["conv_w"].reshape(SSD_CONV, CONV_DIM), ((0, 8 - SSD_CONV), (0, 0)))
    c["conv_b1"] = p["conv_b"].reshape(1, CONV_DIM)
    c["dtb"] = _pad_lane(p["dt_bias"])
    c["alog"] = _pad_lane(p["a_log"])
    c["dskip_x"] = jnp.repeat(p["d_skip"], SSD_HEAD_DIM).reshape(1, D_INNER)
    c["sink"] = _pad_lane(p["attn_sink"]).reshape(1, LANE)
    return c


def _hosted(res, n_own, hook):
    res = list(res)
    if hook is not None:
        hook[1](res[n_own:])
    return res[0] if n_own == 1 else res[:n_own]


def _layer_fwd(x, c, w, bias, l, hooks):
    comm = lambda k: hooks[k][0] if k in hooks else None
    s = {"x": x}
    s["h"] = _rms_fwd(x, c["pre_mix_norm"], f"pre_mix_norm_{l}", out_dtype=BF16)
    res = _mm(s["h"], w["w_in"], f"in_proj_{l}", comm=comm("in_proj"))
    proj = _hosted(res, 1, hooks.get("in_proj")) if "in_proj" in hooks else res
    s["proj"] = proj
    s["xbc"] = _conv_fwd(proj, c["conv_w8"], c["conv_b1"])
    s["dtt"] = _dt_rows(proj)
    s["yf"], s["stf"] = _hosted(_ssd_fwd(s["xbc"], proj, s["dtt"], c["dtb"], c["alog"], False, comm=comm("ssd_fwd")), 2,
                                hooks.get("ssd_fwd"))
    s["yb"], s["stb"] = _hosted(_ssd_fwd(s["xbc"], proj, s["dtt"], c["dtb"], c["alog"], True, comm=comm("ssd_fwd_rev")), 2,
                                hooks.get("ssd_fwd_rev"))
    s["u"] = _gnorm_fwd(s["yf"], s["yb"], s["xbc"], proj, c["dskip_x"], c["ssd_norm"])
    s["y_ssd"] = _mm(s["u"], w["w_ssd_out"], f"ssd_out_{l}")
    s["o"], s["lse"] = _hosted(_attn_fwd(proj, bias, c["sink"], comm=comm("attn_fwd")), 2, hooks.get("attn_fwd"))
    s["y_attn"] = _mm(s["o"], w["w_attn_out"], f"attn_out_{l}")
    s["mix"] = _gate_fwd(proj, c["b_gate"], s["y_ssd"], s["y_attn"])
    s["mixed"] = _mm(s["mix"], w["w_o"], f"w_o_{l}")
    s["x1"] = _rms_fwd(s["mixed"], c["post_mix_norm"], f"post_mix_norm_{l}", res=x)
    s["h2"] = _rms_fwd(s["x1"], c["pre_mlp_norm"], f"pre_mlp_norm_{l}", out_dtype=BF16)
    s["f1"], s["a"] = _mm(s["h2"], w["w_mlp_in"], f"mlp_in_{l}", mode="relu2")
    s["f"] = _mm(s["a"], w["w_mlp_out"], f"mlp_out_{l}")
    x2 = _rms_fwd(s["f"], c["post_mlp_norm"], f"post_mlp_norm_{l}", res=s["x1"])
    return x2, s


def _send_rest(gw):
    s_mi = gw["w_mlp_in"].reshape(D_MODEL, N_DEV, -1).transpose(1, 0, 2).astype(BF16)
    parts = [(_unpad_heads(gw[n], 0) if n == "w_attn_out" else gw[n]).reshape(N_DEV, -1, D_MODEL) for n in ROWS]
    return [s_mi, jnp.concatenate(parts, axis=1).astype(BF16)]


def _send_w_in(g):
    return [_w_in_from_padded(g).reshape(D_MODEL, N_DEV, -1).transpose(1, 0, 2).astype(BF16)]


def _layer_bwd(dx2, s, c, w, bias, l, earlier=None, host_own=False):
    gw, gs, got = {}, {}, {}
    df, gs["post_mlp_norm"] = _rms_bwd(s["f"], c["post_mlp_norm"], dx2, f"post_mlp_norm_bwd_{l}", out_dtype=BF16)
    df1 = _mm(df, w["w_mlp_out"], f"mlp_out_dx_{l}", tb=True, mode="relu2_bwd", extra=s["f1"], out_dtype=BF16)
    gw["w_mlp_out"] = _mm(s["a"], df, f"mlp_out_dw_{l}", ta=True)
    gw["w_mlp_in"] = _mm(s["h2"], df1, f"mlp_in_dw_{l}", ta=True)
    dh2 = _mm(df1, w["w_mlp_in"], f"mlp_in_dx_{l}", tb=True)
    dx1, gs["pre_mlp_norm"] = _rms_bwd(s["x1"], c["pre_mlp_norm"], dh2, f"pre_mlp_norm_bwd_{l}", add=dx2)
    dmixed, gs["post_mix_norm"] = _rms_bwd(s["mixed"], c["post_mix_norm"], dx1, f"post_mix_norm_bwd_{l}", out_dtype=BF16)
    gw["w_o"] = _mm(s["mix"], dmixed, f"w_o_dw_{l}", ta=True)
    dmix = _mm(dmixed, w["w_o"], f"w_o_dx_{l}", tb=True)
    dys, dya, dgt, gs["b_gate"] = _gate_bwd(dmix, s["proj"], c["b_gate"], s["y_ssd"], s["y_attn"])
    gw["w_ssd_out"] = _mm(s["u"], dys, f"ssd_out_dw_{l}", ta=True)
    du = _mm(dys, w["w_ssd_out"], f"ssd_out_dx_{l}", tb=True)
    gw["w_attn_out"] = _mm(s["o"], dya, f"attn_out_dw_{l}", ta=True)
    do = _mm(dya, w["w_attn_out"], f"attn_out_dx_{l}", tb=True)
    res = _attn_bwd(s["proj"], bias, c["sink"], do, s["o"], s["lse"],
                    comm=None if earlier is None else _Comm("exchange", earlier[:1]))
    (dq, dk, dv, dbias, dsink), got["earlier"] = res[:5], list(res[5:])
    gs["attn_sink"] = dsink
    dy, dz, gs["ssd_norm"], gs["d_skip"] = _gnorm_bwd(du, s["yf"], s["yb"], s["xbc"], s["proj"], c["dskip_x"], c["ssd_norm"])
    res = _ssd_bwd(s["xbc"], s["proj"], s["dtt"], c["dtb"], c["alog"], dy, s["stf"], c["dskip_x"], False,
                   comm=None if earlier is None else _Comm("exchange", earlier[1:]))
    part, got["earlier"] = res[:3], got["earlier"] + list(res[3:])
    res = _ssd_bwd(s["xbc"], s["proj"], s["dtt"], c["dtb"], c["alog"], dy, s["stb"], c["dskip_x"], True, prev=part,
                   comm=_Comm("exchange", _send_rest(gw)) if host_own else None)
    (dxbc, ddt, sm), got["rest"] = res[:3], res[3:]
    gs["a_log"], gs["dt_bias"] = sm[0:1], sm[1:2]
    dxbc_raw, gs["conv"] = _conv_bwd(s["proj"], dxbc, c["conv_w8"], c["conv_b1"])
    t = dx2.shape[0]
    dproj = jnp.concatenate([dz, dgt, dq, dxbc_raw, dk, dv, ddt.astype(BF16), jnp.zeros((t, NP - DT0 - LANE), BF16)], axis=1)
    gw["w_in"] = _mm(s["h"], dproj, f"in_proj_dw_{l}", ta=True)
    if host_own:
        dh, *got["w_in"] = _mm(dproj, w["w_in"], f"in_proj_dx_{l}", tb=True, comm=_Comm("exchange", _send_w_in(gw["w_in"])))
    else:
        dh = _mm(dproj, w["w_in"], f"in_proj_dx_{l}", tb=True)
    dx, gs["pre_mix_norm"] = _rms_bwd(s["x"], c["pre_mix_norm"], dh, f"pre_mix_norm_bwd_{l}", add=dx1)
    return dx, gw, gs, dbias, got


SMALL = (("pre_mix_norm", DEPTH * D_MODEL), ("b_gate", DEPTH * 2 * D_MODEL), ("conv_w", DEPTH * SSD_CONV * CONV_DIM),
         ("conv_b", DEPTH * CONV_DIM), ("dt_bias", DEPTH * 2 * SSD_HEADS), ("a_log", DEPTH * 2 * SSD_HEADS),
         ("d_skip", DEPTH * SSD_HEADS), ("ssd_norm", DEPTH * D_INNER), ("attn_sink", DEPTH * Q_HEADS),
         ("rel_bias_table", N_BUCKETS * Q_HEADS), ("post_mix_norm", DEPTH * D_MODEL), ("pre_mlp_norm", DEPTH * D_MODEL),
         ("post_mlp_norm", DEPTH * D_MODEL))


def _small_rows(n):
    return -(-n // LANE)


def _pack_small(vals, fill=0.0):
    rows = []
    for name, n in SMALL:
        v = vals[name].reshape(-1).astype(F32)
        rows.append(jnp.pad(v, (0, _small_rows(n) * LANE - n), constant_values=fill).reshape(-1, LANE))
    out = jnp.concatenate(rows, axis=0)
    return jnp.pad(out, ((0, -out.shape[0] % 8), (0, 0)), constant_values=fill)


def _unpack_small(packed, shapes):
    out, r = {}, 0
    for name, n in SMALL:
        nr = _small_rows(n)
        out[name] = packed[r:r + nr].reshape(-1)[:n].reshape(shapes[name])
        r += nr
    return out


BIG = ("w_in", "w_ssd_out", "w_attn_out", "w_o", "w_mlp_in", "w_mlp_out")
ROWS = ("w_ssd_out", "w_attn_out", "w_o", "w_mlp_out")


def _pack_rows(vals, lead):
    return jnp.concatenate([vals[n].reshape(lead + (-1, D_MODEL)) for n in ROWS], axis=len(lead))


def kernel(x, pre_mix_norm, w_in, b_gate, conv_w, conv_b, dt_bias, a_log, d_skip, ssd_norm, w_ssd_out, attn_sink, rel_bias_table, w_attn_out, w_o, post_mix_norm, pre_mlp_norm, w_mlp_in, w_mlp_out, post_mlp_norm, loss_target, m_pre_mix_norm, m_w_in, m_b_gate, m_conv_w, m_conv_b, m_dt_bias, m_a_log, m_d_skip, m_ssd_norm, m_w_ssd_out, m_attn_sink, m_rel_bias_table, m_w_attn_out, m_w_o, m_post_mix_norm, m_pre_mlp_norm, m_w_mlp_in, m_w_mlp_out, m_post_mlp_norm, v_pre_mix_norm, v_w_in, v_b_gate, v_conv_w, v_conv_b, v_dt_bias, v_a_log, v_d_skip, v_ssd_norm, v_w_ssd_out, v_attn_sink, v_rel_bias_table, v_w_attn_out, v_w_o, v_post_mix_norm, v_pre_mlp_norm, v_w_mlp_in, v_w_mlp_out, v_post_mlp_norm):
    names = ("pre_mix_norm", "w_in", "b_gate", "conv_w", "conv_b", "dt_bias", "a_log", "d_skip", "ssd_norm", "w_ssd_out",
             "attn_sink", "rel_bias_table", "w_attn_out", "w_o", "post_mix_norm", "pre_mlp_norm", "w_mlp_in", "w_mlp_out",
             "post_mlp_norm")
    W = dict(zip(names, (pre_mix_norm, w_in, b_gate, conv_w, conv_b, dt_bias, a_log, d_skip, ssd_norm, w_ssd_out, attn_sink,
                         rel_bias_table, w_attn_out, w_o, post_mix_norm, pre_mlp_norm, w_mlp_in, w_mlp_out, post_mlp_norm)))
    M = dict(zip(names, (m_pre_mix_norm, m_w_in, m_b_gate, m_conv_w, m_conv_b, m_dt_bias, m_a_log, m_d_skip, m_ssd_norm,
                         m_w_ssd_out, m_attn_sink, m_rel_bias_table, m_w_attn_out, m_w_o, m_post_mix_norm, m_pre_mlp_norm,
                         m_w_mlp_in, m_w_mlp_out, m_post_mlp_norm)))
    V = dict(zip(names, (v_pre_mix_norm, v_w_in, v_b_gate, v_conv_w, v_conv_b, v_dt_bias, v_a_log, v_d_skip, v_ssd_norm,
                         v_w_ssd_out, v_attn_sink, v_rel_bias_table, v_w_attn_out, v_w_o, v_post_mix_norm, v_pre_mlp_norm,
                         v_w_mlp_in, v_w_mlp_out, v_post_mlp_norm)))
    t = x.shape[1]
    shard = {n: W[n].shape for n in names}
    dev = 4 * lax.axis_index("x") + 2 * lax.axis_index("y") + lax.axis_index("c")
    cshard = CONV_DIM // N_DEV

    sh_in = [w_in[l].astype(BF16) for l in range(DEPTH)]
    sh_rest = [[w_mlp_in[l].astype(BF16), _pack_rows({n: W[n][l].astype(BF16) for n in ROWS}, ())] for l in range(DEPTH)]
    full = [{} for _ in range(DEPTH)]

    def take_w_in(l):
        def take(got):
            full[l]["w_in"] = _w_in_to_padded(got[0].transpose(1, 0, 2).reshape(D_MODEL, N_IN))
        return take

    def take_mlp_in(l):
        def take(got):
            full[l]["w_mlp_in"] = got[0].transpose(1, 0, 2).reshape(D_MODEL, D_FF)
        return take

    def take_rows(l):
        def take(got):
            r = 0
            for n in ROWS:
                per = shard[n][1]
                full[l][n] = got[0][:, r:r + per].reshape(N_DEV * per, D_MODEL)
                r += per
            full[l]["w_attn_out"] = _pad_heads(full[l]["w_attn_out"], 0)
        return take

    def take_rest(l):
        def take(got):
            take_mlp_in(l)(got[:1])
            take_rows(l)(got[1:])
        return take

    take_w_in(0)(_communicate(_Comm("gather", [sh_in[0]]), "gather_w_in_0"))
    (g_conv,) = _communicate(_Comm("gather", [conv_w.reshape(-1, LANE)]), "gather_conv_w", vmem=True)
    conv_full = g_conv.reshape(N_DEV, DEPTH, SSD_CONV, cshard).transpose(1, 2, 0, 3).reshape(DEPTH, SSD_CONV, CONV_DIM)
    hooks = [{"in_proj": (_Comm("gather", sh_rest[0]), take_rest(0)),
              "ssd_fwd": (_Comm("gather", sh_rest[1][1:]), take_rows(1)),
              "ssd_fwd_rev": (_Comm("gather", sh_rest[1][:1]), take_mlp_in(1)),
              "attn_fwd": (_Comm("gather", [sh_in[1]]), take_w_in(1))}, {}]

    _, band = _bucket_np()
    onehot = jnp.asarray(_onehot_np(), BF16)
    table_t = jnp.pad(rel_bias_table.T, ((0, 0), (0, LANE - N_BUCKETS)))
    bias = _mm(table_t, onehot, "t5_bias", tb=True, tm=Q_HEADS, tn=3 * BLOCK * BLOCK // 8, split_a=True)
    bias = (bias.reshape(Q_HEADS, BLOCK, 3 * BLOCK) + jnp.asarray(np.where(band, 0.0, NEG), F32)).reshape(
        KV_HEADS, REP * BLOCK, 3 * BLOCK)

    consts = []
    for l in range(DEPTH):
        p = {n: W[n][l] for n in names if n not in BIG and n not in ("rel_bias_table", "conv_w")}
        p["conv_w"] = conv_full[l]
        consts.append(_layer_consts(p))
    h = x[0]
    saved = []
    for l in range(DEPTH):
        h, s = _layer_fwd(h, consts[l], full[l], bias, l, hooks[l])
        saved.append(s)
    dy, lsum = _loss(h, loss_target[0])
    loss = lax.psum(0.5 / D_MODEL * jnp.sum(lsum), ("x", "y", "c"))

    gss, dbs = [None] * DEPTH, [None] * DEPTH
    dy, gw1, gss[1], dbs[1], _ = _layer_bwd(dy, saved[1], consts[1], full[1], bias, 1)
    dy, _, gss[0], dbs[0], got = _layer_bwd(dy, saved[0], consts[0], full[0], bias, 0,
                                            earlier=_send_w_in(gw1["w_in"]) + _send_rest(gw1), host_own=True)
    recv = [list(got["w_in"]) + list(got["rest"]), list(got["earlier"])]
    grad_x = dy[None]
    dbias = jnp.concatenate([d.reshape(Q_HEADS, -1) for d in dbs], axis=1)
    d_table = _mm(dbias, jnp.concatenate([onehot] * DEPTH, axis=0), "t5_bias_bwd", tm=Q_HEADS, tk=3 * BLOCK * BLOCK // 8,
                  split_a=True)

    sg = {}
    for n in ("pre_mix_norm", "b_gate", "ssd_norm", "post_mix_norm", "pre_mlp_norm", "post_mlp_norm"):
        sg[n] = jnp.stack([gss[l][n].reshape(-1) for l in range(DEPTH)])
    sg["conv_w"] = jnp.stack([gss[l]["conv"][0:SSD_CONV] for l in range(DEPTH)])
    sg["conv_b"] = jnp.stack([gss[l]["conv"][SSD_CONV] for l in range(DEPTH)])
    sg["dt_bias"] = jnp.stack([gss[l]["dt_bias"][0, 0:2 * SSD_HEADS] for l in range(DEPTH)])
    sg["a_log"] = jnp.stack([gss[l]["a_log"][0, 0:2 * SSD_HEADS] for l in range(DEPTH)])
    sg["d_skip"] = jnp.stack([gss[l]["d_skip"][0, 0:SSD_HEADS] for l in range(DEPTH)])
    sg["attn_sink"] = jnp.stack([gss[l]["attn_sink"][0, 0:Q_HEADS] for l in range(DEPTH)])
    sg["rel_bias_table"] = d_table[:, 0:N_BUCKETS].T
    (small_parts,) = _communicate(_Comm("gather", [_pack_small(sg)]), "gather_small_grads", vmem=True)
    sshape = {n: W[n].shape for n, _ in SMALL}
    sshape["conv_w"] = (DEPTH, SSD_CONV, 1, CONV_DIM)
    pk = lambda d, fill: _pack_small({n: (jnp.full((DEPTH, SSD_CONV, CONV_DIM), fill, F32) if n == "conv_w" else d[n])
                                      for n, _ in SMALL}, fill)
    s_out = [_unpack_small(o, sshape) for o in _adamw(small_parts, pk(W, 1.0), pk(M, 1.0), pk(V, 1.0), "adamw_small", tr=1024)]
    conv_g = lax.dynamic_slice_in_dim(s_out[0]["conv_w"], dev * cshard, cshard, axis=3)
    c_out = _adamw(conv_g.reshape(1, DEPTH * SSD_CONV, cshard), conv_w.reshape(-1, cshard), m_conv_w.reshape(-1, cshard),
                   v_conv_w.reshape(-1, cshard), "adamw_conv_w", tr=DEPTH * SSD_CONV)
    for i in range(4):
        s_out[i]["conv_w"] = c_out[i].reshape(conv_w.shape)

    layer_out = []
    for l in range(DEPTH):
        r_in, r_mi, r_rows = recv[l]
        lo = {"w_in": _adamw(r_in, w_in[l], m_w_in[l], v_w_in[l], f"adamw_w_in_{l}"),
              "w_mlp_in": _adamw(r_mi, w_mlp_in[l], m_w_mlp_in[l], v_w_mlp_in[l], f"adamw_w_mlp_in_{l}")}
        o = _adamw(r_rows, *[_pack_rows({n: d[n][l] for n in ROWS}, ()) for d in (W, M, V)], f"adamw_rows_{l}")
        r = 0
        for n in ROWS:
            per = shard[n][1]
            lo[n] = [a[r:r + per] for a in o]
            r += per
        layer_out.append(lo)
    b_out = {n: [jnp.stack([layer_out[l][n][i] for l in range(DEPTH)]) for i in range(4)] for n in BIG}

    outs = [loss, grad_x]
    for i in range(4):
        for n in names:
            outs.append(b_out[n][i] if n in BIG else s_out[i][n])
    return tuple(outs)
```

```python
import functools
import math

import numpy as np
import jax
import jax.numpy as jnp
from jax import lax
from jax.experimental import pallas as pl
from jax.experimental.pallas import tpu as pltpu

F32 = jnp.float32
BF16 = jnp.bfloat16

D_MODEL = 1024
DEPTH = 2
D_INNER = 2048
SSD_HEADS = 32
SSD_HEAD_DIM = 64
SSD_GROUPS = 8
SSD_REP = 4
SSD_STATE = 128
SSD_CONV = 5
CHUNK = 128
CONV_DIM = 4096
Q_HEADS = 16
KV_HEADS = 4
HEAD_DIM = 64
ATTN_WIDTH = 1024
KV_WIDTH = 256
WINDOW = 128
BLOCK = 128
N_BUCKETS = 32
MAX_DISTANCE = 128
D_FF = 4096
EPS = 1e-6
IN_SPLITS = (D_INNER, CONV_DIM, 2 * SSD_HEADS, ATTN_WIDTH, KV_WIDTH, KV_WIDTH, 2 * D_MODEL)
N_IN = sum(IN_SPLITS)
N_DEV = 8

ADAM_LR = 0.001
ADAM_B1 = 0.9
ADAM_B2 = 0.999
ADAM_EPS = 1e-08
ADAM_WD = 0.01
ADAM_STEP = 10

LANE = 128
HEAD_PAD = LANE
QP = Q_HEADS * HEAD_PAD
KP = KV_HEADS * HEAD_PAD
QKV = QP + 2 * KP
Q0 = 0
K0 = QP
V0 = QP + KP
XBC0 = QKV
DT0 = XBC0 + CONV_DIM
DTW = 1024
Z0 = DT0 + DTW
GT0 = Z0 + D_INNER
NP = GT0 + 2 * D_MODEL
NEG = -1e30
VMEM_LIMIT = 56 * 1024 * 1024


def _params(n_grid):
    return pltpu.CompilerParams(dimension_semantics=("arbitrary",) * n_grid, vmem_limit_bytes=VMEM_LIMIT)


def _dot(a, b, dims=((1,), (0,))):
    return lax.dot_general(a, b, (dims, ((), ())), preferred_element_type=F32)


def _mx(a):
    return a.astype(BF16)


def _split_dot(x, e):
    hi = x.astype(BF16)
    lo = (x - hi.astype(F32)).astype(BF16)
    return _dot(hi, e) + _dot(lo, e)


def _softplus(x):
    u = jnp.exp(-jnp.abs(x))
    w = 1.0 + u
    log1p = jnp.where(w == 1.0, u, jnp.log(w) * u / jnp.where(w == 1.0, 1.0, w - 1.0))
    return jnp.maximum(x, 0.0) + log1p


def _sigmoid(x):
    return 1.0 / (1.0 + jnp.exp(-x))


def _mm_body(*refs, grid, ta, tb, split_a, mode, comm):
    n_in = 3 if mode == "relu2_bwd" else 2
    n_out = 2 if mode == "relu2" else 1
    ins, cin, outs, cout, (acc_ref,), csem = _split_refs(refs, n_in, n_out, comm)
    a_ref, b_ref = ins[0], ins[1]
    i, j, k = pl.program_id(0), pl.program_id(1), pl.program_id(2)
    nk = grid[2]
    if comm is not None:
        comm.run(cin, cout, csem, (i * grid[1] + j) * nk + k, grid[0] * grid[1] * nk)

    dims = ((0 if ta else 1,), (1 if tb else 0,))
    b = _mx(b_ref[...])
    if split_a:
        a32 = a_ref[...].astype(F32)
        hi = a32.astype(BF16)
        lo = (a32 - hi.astype(F32)).astype(BF16)
        part = _dot(hi, b, dims) + _dot(lo, b, dims)
    else:
        part = _dot(_mx(a_ref[...]), b, dims)

    if nk > 1:
        @pl.when(k == 0)
        def _():
            acc_ref[...] = jnp.zeros_like(acc_ref)

        acc_ref[...] += part

    @pl.when(k == nk - 1)
    def _():
        acc = part if nk == 1 else acc_ref[...]
        if mode == "relu2":
            outs[0][...] = acc
            r = jnp.maximum(acc, 0.0)
            outs[1][...] = (r * r).astype(outs[1].dtype)
        elif mode == "relu2_bwd":
            outs[0][...] = (acc * 2.0 * jnp.maximum(ins[2][...], 0.0)).astype(outs[0].dtype)
        else:
            outs[0][...] = acc.astype(outs[0].dtype)


def _tile(n, pref):
    t = min(n, pref)
    while n % t:
        t -= LANE
    assert t > 0 and n % t == 0, (n, pref)
    return t


def _mm(a, b, name, ta=False, tb=False, out_dtype=F32, tm=1024, tn=2048, tk=1024, split_a=False, mode=None, extra=None,
        comm=None):
    m, k = (a.shape[1], a.shape[0]) if ta else a.shape
    n = b.shape[0] if tb else b.shape[1]
    assert (b.shape[1] if tb else b.shape[0]) == k
    tm, tn, tk = _tile(m, tm), _tile(n, tn), _tile(k, tk)
    grid = (m // tm, n // tn, k // tk)
    a_spec = pl.BlockSpec((tk, tm), lambda i, j, kk: (kk, i)) if ta else pl.BlockSpec((tm, tk), lambda i, j, kk: (i, kk))
    b_spec = pl.BlockSpec((tn, tk), lambda i, j, kk: (j, kk)) if tb else pl.BlockSpec((tk, tn), lambda i, j, kk: (kk, j))
    o_spec = pl.BlockSpec((tm, tn), lambda i, j, kk: (i, j))
    in_specs, args = [a_spec, b_spec], [a, b]
    out_specs, out_shape = [o_spec], [jax.ShapeDtypeStruct((m, n), out_dtype)]
    if mode == "relu2":
        out_specs, out_shape = [o_spec, o_spec], [jax.ShapeDtypeStruct((m, n), F32), jax.ShapeDtypeStruct((m, n), BF16)]
    elif mode == "relu2_bwd":
        in_specs, args = in_specs + [o_spec], args + [extra]
    scratch = [pltpu.VMEM((tm, tn) if grid[2] > 1 else (8, LANE), F32)]
    if comm is not None:
        in_specs, args = in_specs + [ANY] * comm.n, args + comm.arrs
        out_specs, out_shape = out_specs + [ANY] * comm.n, out_shape + comm.out_shapes()
        scratch = scratch + comm.sems()
    res = pl.pallas_call(
        functools.partial(_mm_body, grid=grid, ta=ta, tb=tb, split_a=split_a, mode=mode, comm=comm),
        name=name, grid=grid, in_specs=in_specs, out_specs=out_specs, out_shape=out_shape, scratch_shapes=scratch,
        compiler_params=_params(3),
    )(*args)
    return res[0] if len(res) == 1 else res


def _rms_fwd_body(*refs, has_res):
    if has_res:
        x_ref, g_ref, r_ref, o_ref = refs
    else:
        x_ref, g_ref, o_ref = refs
    x = x_ref[...]
    y = x * lax.rsqrt(jnp.mean(x * x, axis=-1, keepdims=True) + EPS) * g_ref[...]
    if has_res:
        y = r_ref[...] + y
    o_ref[...] = y.astype(o_ref.dtype)


def _rms_fwd(x, g, name, res=None, out_dtype=F32, tr=512):
    t, d = x.shape
    tr = min(tr, t)
    row = pl.BlockSpec((tr, d), lambda i: (i, 0))
    vec = pl.BlockSpec((1, d), lambda i: (0, 0))
    args = [x, g.reshape(1, d)] + ([res] if res is not None else [])
    return pl.pallas_call(
        functools.partial(_rms_fwd_body, has_res=res is not None),
        name=name,
        grid=(t // tr,),
        in_specs=[row, vec] + ([row] if res is not None else []),
        out_specs=row,
        out_shape=jax.ShapeDtypeStruct((t, d), out_dtype),
        compiler_params=_params(1),
    )(*args)


def _rms_bwd_body(*refs, has_add):
    if has_add:
        x_ref, g_ref, dy_ref, add_ref, dx_ref, dg_ref = refs
    else:
        x_ref, g_ref, dy_ref, dx_ref, dg_ref = refs

    @pl.when(pl.program_id(0) == 0)
    def _():
        dg_ref[...] = jnp.zeros_like(dg_ref)

    x = x_ref[...]
    dy = dy_ref[...].astype(F32)
    rstd = lax.rsqrt(jnp.mean(x * x, axis=-1, keepdims=True) + EPS)
    n = x * rstd
    dn = dy * g_ref[...]
    dx = rstd * (dn - n * jnp.mean(dn * n, axis=-1, keepdims=True))
    if has_add:
        dx = dx + add_ref[...]
    dx_ref[...] = dx.astype(dx_ref.dtype)
    dg_ref[...] += jnp.sum(dy * n, axis=0, keepdims=True)


def _rms_bwd(x, g, dy, name, add=None, out_dtype=F32, tr=512):
    t, d = x.shape
    tr = min(tr, t)
    row = pl.BlockSpec((tr, d), lambda i: (i, 0))
    vec = pl.BlockSpec((1, d), lambda i: (0, 0))
    args = [x, g.reshape(1, d), dy] + ([add] if add is not None else [])
    return pl.pallas_call(
        functools.partial(_rms_bwd_body, has_add=add is not None),
        name=name,
        grid=(t // tr,),
        in_specs=[row, vec, row] + ([row] if add is not None else []),
        out_specs=[row, vec],
        out_shape=[jax.ShapeDtypeStruct((t, d), out_dtype), jax.ShapeDtypeStruct((1, d), F32)],
        compiler_params=_params(1),
    )(*args)


def _loss_body(y_ref, t_ref, dy_ref, l_ref):
    @pl.when(pl.program_id(0) == 0)
    def _():
        l_ref[...] = jnp.zeros_like(l_ref)

    e = y_ref[...] - t_ref[...]
    dy_ref[...] = e * (1.0 / D_MODEL)
    s = jnp.sum(e * e, axis=0, keepdims=True)
    acc = s[:, 0:LANE]
    for j in range(1, D_MODEL // LANE):
        acc = acc + s[:, j * LANE:(j + 1) * LANE]
    l_ref[...] += acc


def _loss(y, target, tr=512):
    t, d = y.shape
    tr = min(tr, t)
    row = pl.BlockSpec((tr, d), lambda i: (i, 0))
    return pl.pallas_call(
        functools.partial(_loss_body), name="loss_head", grid=(t // tr,), in_specs=[row, row],
        out_specs=[row, pl.BlockSpec((1, LANE), lambda i: (0, 0))],
        out_shape=[jax.ShapeDtypeStruct((t, d), F32), jax.ShapeDtypeStruct((1, LANE), F32)],
        compiler_params=_params(1),
    )(y, target)


CONV_CT = 128
CONV_RB = 512
CONV_PAD = 8


def _conv_rows(t):
    return min(CONV_RB, t)


def _conv_fwd_body(u_ref, w_ref, b_ref, o_ref, pad_ref, *, t):
    rb = _conv_rows(t)
    zeros = jnp.zeros((CONV_PAD, CONV_CT), F32)
    pad_ref[pl.ds(0, CONV_PAD), :] = zeros
    pad_ref[pl.ds(t + CONV_PAD, CONV_PAD), :] = zeros
    pad_ref[pl.ds(CONV_PAD, t), :] = u_ref[...]
    w = w_ref[...]
    b = b_ref[...]
    for c in range(t // rb):
        base = CONV_PAD + c * rb
        acc = b + w[0:1, :] * pad_ref[pl.ds(base - 2, rb), :]
        for k in range(1, SSD_CONV):
            acc = acc + w[k:k + 1, :] * pad_ref[pl.ds(base + k - 2, rb), :]
        o_ref[pl.ds(c * rb, rb), :] = acc * _sigmoid(acc)


def _conv_fwd(proj, conv_w8, conv_b):
    t = proj.shape[0]
    off = XBC0 // CONV_CT
    return pl.pallas_call(
        functools.partial(_conv_fwd_body, t=t), name="conv_fwd", grid=(CONV_DIM // CONV_CT,),
        in_specs=[pl.BlockSpec((t, CONV_CT), lambda i: (0, off + i)), pl.BlockSpec((8, CONV_CT), lambda i: (0, i)),
                  pl.BlockSpec((1, CONV_CT), lambda i: (0, i))],
        out_specs=pl.BlockSpec((t, CONV_CT), lambda i: (0, i)),
        out_shape=jax.ShapeDtypeStruct((t, CONV_DIM), F32),
        scratch_shapes=[pltpu.VMEM((t + 2 * CONV_PAD, CONV_CT), F32)],
        compiler_params=_params(1),
    )(proj, conv_w8, conv_b)


def _conv_bwd_body(u_ref, da_ref, w_ref, b_ref, _, du_ref, dw_ref, pad_ref, pad2_ref, *, t):
    rb = _conv_rows(t)
    zeros = jnp.zeros((CONV_PAD, CONV_CT), F32)
    for p in (pad_ref, pad2_ref):
        p[pl.ds(0, CONV_PAD), :] = zeros
        p[pl.ds(t + CONV_PAD, CONV_PAD), :] = zeros
    pad_ref[pl.ds(CONV_PAD, t), :] = u_ref[...]
    w = w_ref[...]
    b = b_ref[...]
    dw = [jnp.zeros((1, CONV_CT), F32) for _ in range(SSD_CONV + 1)]
    for c in range(t // rb):
        base = CONV_PAD + c * rb
        us = [pad_ref[pl.ds(base + k - 2, rb), :] for k in range(SSD_CONV)]
        acc = b + w[0:1, :] * us[0]
        for k in range(1, SSD_CONV):
            acc = acc + w[k:k + 1, :] * us[k]
        sg = _sigmoid(acc)
        dyc = da_ref[pl.ds(c * rb, rb), :] * (sg * (1.0 + acc * (1.0 - sg)))
        pad2_ref[pl.ds(base, rb), :] = dyc
        for k in range(SSD_CONV):
            dw[k] = dw[k] + jnp.sum(dyc * us[k], axis=0, keepdims=True)
        dw[SSD_CONV] = dw[SSD_CONV] + jnp.sum(dyc, axis=0, keepdims=True)
    for c in range(t // rb):
        base = CONV_PAD + c * rb
        acc = w[0:1, :] * pad2_ref[pl.ds(base + 2, rb), :]
        for k in range(1, SSD_CONV):
            acc = acc + w[k:k + 1, :] * pad2_ref[pl.ds(base + 2 - k, rb), :]
        du_ref[pl.ds(c * rb, rb), :] = acc.astype(du_ref.dtype)
    dw_ref[...] = jnp.concatenate(dw + [jnp.zeros((2, CONV_CT), F32)], axis=0)


def _conv_bwd(proj, dact, conv_w8, conv_b, dproj):
    t = proj.shape[0]
    off = XBC0 // CONV_CT
    col = pl.BlockSpec((t, CONV_CT), lambda i: (0, i))
    seg = pl.BlockSpec((t, CONV_CT), lambda i: (0, off + i))
    return pl.pallas_call(
        functools.partial(_conv_bwd_body, t=t), name="conv_bwd", grid=(CONV_DIM // CONV_CT,),
        in_specs=[seg, col, pl.BlockSpec((8, CONV_CT), lambda i: (0, i)), pl.BlockSpec((1, CONV_CT), lambda i: (0, i)), ANY],
        out_specs=[seg, pl.BlockSpec((8, CONV_CT), lambda i: (0, i))],
        out_shape=[jax.ShapeDtypeStruct((t, NP), BF16), jax.ShapeDtypeStruct((8, CONV_DIM), F32)],
        scratch_shapes=[pltpu.VMEM((t + 2 * CONV_PAD, CONV_CT), F32), pltpu.VMEM((t + 2 * CONV_PAD, CONV_CT), F32)],
        input_output_aliases={4: 0}, compiler_params=_params(1),
    )(proj, dact, conv_w8, conv_b, dproj)


GW = SSD_REP * SSD_HEAD_DIM


def _expand_np(hoff):
    e = np.zeros((LANE, D_INNER), np.float32)
    for h in range(SSD_HEADS):
        e[hoff + h, h * SSD_HEAD_DIM:(h + 1) * SSD_HEAD_DIM] = 1.0
    return e


def _head_masks():
    lane = lax.broadcasted_iota(jnp.int32, (1, GW), 1)
    return [((lane >= r * SSD_HEAD_DIM) & (lane < (r + 1) * SSD_HEAD_DIM)).astype(F32) for r in range(SSD_REP)]


def _ssd_common(dtc_ref, dtr_ref, bc_ref, br_ref, alc_ref, alr_ref, e_ref, rev):
    L = CHUNK
    ri = lax.broadcasted_iota(jnp.int32, (L, L), 0)
    ci = lax.broadcasted_iota(jnp.int32, (L, L), 1)
    tri = (ri <= ci) if rev else (ri >= ci)
    trit = (ri >= ci) if rev else (ri <= ci)
    raw_c = dtc_ref[...] + bc_ref[...]
    dt_c = _softplus(raw_c)
    a_c = -jnp.exp(alc_ref[...])
    cs_c = jnp.dot(tri.astype(F32), dt_c * a_c, precision=lax.Precision.HIGHEST, preferred_element_type=F32)
    dt_r = _softplus(dtr_ref[...] + br_ref[...])
    a_r = -jnp.exp(alr_ref[...])
    cs_r = jnp.dot(dt_r * a_r, trit.astype(F32), precision=lax.Precision.HIGHEST, preferred_element_type=F32)
    il = 0 if rev else L - 1
    e_c = jnp.exp(cs_c)
    w_c = jnp.exp(cs_c[il:il + 1, :] - cs_c)
    ex = _dot(_mx(jnp.concatenate([dt_c, e_c, w_c], axis=0)), e_ref[...])
    return dict(tri=tri, trit=trit, raw_c=raw_c, dt_c=dt_c, a_c=a_c, cs_c=cs_c, cs_r=cs_r, il=il, e_last=e_c[il:il + 1],
                dt_x=ex[0:L], e_x=ex[L:2 * L], w_x=ex[2 * L:3 * L], cd_x=ex[L + il:L + il + 1])


def _ssd_fwd_body(*refs, rev, hoff, nsteps, comm):
    L = CHUNK
    ins, cin, (y_ref, st_ref), cout, (s_scr,), csem = _split_refs(refs, 10, 2, comm)
    xs_ref, b_ref, c_ref, dtc_ref, dtr_ref, bc_ref, br_ref, alc_ref, alr_ref, e_ref = ins
    if comm is not None:
        comm.run(cin, cout, csem, pl.program_id(0), nsteps)

    @pl.when(pl.program_id(0) == 0)
    def _():
        s_scr[...] = jnp.zeros_like(s_scr)

    st_ref[0] = s_scr[...]
    q = _ssd_common(dtc_ref, dtr_ref, bc_ref, br_ref, alc_ref, alr_ref, e_ref, rev)
    masks = _head_masks()
    xdt = xs_ref[...] * q["dt_x"]
    xw = xdt * q["w_x"]
    early = []
    for g in range(SSD_GROUPS):
        gc = slice(g * GW, (g + 1) * GW)
        nc = slice(g * SSD_STATE, (g + 1) * SSD_STATE)
        bg = _mx(b_ref[:, nc])
        cg = _mx(c_ref[:, nc])
        sg = s_scr[:, gc]
        early.append((_dot(cg, bg, ((1,), (1,))), _dot(cg, _mx(sg)) * q["e_x"][:, gc]))
        s_scr[:, gc] = sg * q["cd_x"][:, gc] + _dot(bg, _mx(xw[:, gc]), ((0,), (0,)))
    for g in range(SSD_GROUPS):
        gc = slice(g * GW, (g + 1) * GW)
        cb, y = early[g]
        ms = []
        for r in range(SSD_REP):
            hh = hoff + g * SSD_REP + r
            diff = q["cs_c"][:, hh:hh + 1] - q["cs_r"][hh:hh + 1, :]
            ms.append(_mx(cb * jnp.exp(jnp.where(q["tri"], diff, NEG))))
        y4 = _dot(jnp.concatenate(ms, axis=0), _mx(xdt[:, gc]))
        for r in range(SSD_REP):
            y = y + y4[r * L:(r + 1) * L] * masks[r]
        y_ref[:, gc] = y


def _dt_rows_body(p_ref, o_ref):
    o_ref[...] = p_ref[...].T


def _dt_rows(proj, tr=512):
    t = proj.shape[0]
    tr = min(tr, t)
    return pl.pallas_call(
        functools.partial(_dt_rows_body), name="dt_rows", grid=(t // tr,),
        in_specs=[pl.BlockSpec((tr, LANE), lambda i: (i, DT0 // LANE))], out_specs=pl.BlockSpec((LANE, tr), lambda i: (0, i)),
        out_shape=jax.ShapeDtypeStruct((LANE, t), F32), compiler_params=_params(1),
    )(proj)


def _ssd_specs(t, rev):
    nc = t // CHUNK
    cm = (lambda i: nc - 1 - i) if rev else (lambda i: i)
    xs = pl.BlockSpec((CHUNK, D_INNER), lambda i: (cm(i), 0))
    bb = pl.BlockSpec((CHUNK, SSD_GROUPS * SSD_STATE), lambda i: (cm(i), 2))
    cc = pl.BlockSpec((CHUNK, SSD_GROUPS * SSD_STATE), lambda i: (cm(i), 3))
    dtc = pl.BlockSpec((CHUNK, LANE), lambda i: (cm(i), DT0 // LANE))
    dtr = pl.BlockSpec((LANE, CHUNK), lambda i: (0, cm(i)))
    colv = pl.BlockSpec((1, LANE), lambda i: (0, 0))
    rowv = pl.BlockSpec((LANE, 1), lambda i: (0, 0))
    return cm, xs, bb, cc, dtc, dtr, colv, rowv


def _ssd_fwd(xbc, proj, dtt, bias, alog, rev, comm=None):
    t = xbc.shape[0]
    nc = t // CHUNK
    cm, xs, bb, cc, dtc, dtr, colv, rowv = _ssd_specs(t, rev)
    hoff = SSD_HEADS if rev else 0
    e = jnp.asarray(_expand_np(hoff), BF16)
    in_specs = [xs, bb, cc, dtc, dtr, colv, rowv, colv, rowv, pl.BlockSpec((LANE, D_INNER), lambda i: (0, 0))]
    args = [xbc, xbc, xbc, proj, dtt, bias.reshape(1, LANE), bias.reshape(LANE, 1), alog.reshape(1, LANE), alog.reshape(LANE, 1), e]
    out_specs = [xs, pl.BlockSpec((1, SSD_STATE, D_INNER), lambda i: (cm(i), 0, 0))]
    out_shape = [jax.ShapeDtypeStruct((t, D_INNER), F32), jax.ShapeDtypeStruct((nc, SSD_STATE, D_INNER), F32)]
    scratch = [pltpu.VMEM((SSD_STATE, D_INNER), F32)]
    if comm is not None:
        in_specs, args = in_specs + [ANY] * comm.n, args + comm.arrs
        out_specs, out_shape = out_specs + [ANY] * comm.n, out_shape + comm.out_shapes()
        scratch = scratch + comm.sems()
    return pl.pallas_call(
        functools.partial(_ssd_fwd_body, rev=rev, hoff=hoff, nsteps=nc, comm=comm), name="ssd_fwd_rev" if rev else "ssd_fwd",
        grid=(nc,), in_specs=in_specs, out_specs=out_specs, out_shape=out_shape, scratch_shapes=scratch,
        compiler_params=_params(1),
    )(*args)


def _ssd_bwd_body(*refs, rev, hoff, first, nsteps, comm):
    L = CHUNK
    ins, cin, (dx_ref, ddt_ref, sm_ref), cout, (ds_scr,), csem = _split_refs(refs, 14 if first else 18, 3, comm)
    (xs_ref, b_ref, c_ref, dtc_ref, dtr_ref, bc_ref, br_ref, alc_ref, alr_ref, e_ref, et_ref, dy_ref, sp_ref,
     dsk_ref) = ins[:14]
    if not first:
        pdx_ref, pddt_ref, psm_ref = ins[14:17]
    if comm is not None:
        comm.run(cin, cout, csem, pl.program_id(0), nsteps)

    @pl.when(pl.program_id(0) == 0)
    def _():
        ds_scr[...] = jnp.zeros_like(ds_scr)
        sm_ref[...] = jnp.zeros_like(sm_ref) if first else psm_ref[...]

    q = _ssd_common(dtc_ref, dtr_ref, bc_ref, br_ref, alc_ref, alr_ref, e_ref, rev)
    masks = _head_masks()
    xs = xs_ref[...]
    dy = dy_ref[...]
    xdt = xs * q["dt_x"]
    xw = xdt * q["w_x"]
    dye = dy * q["e_x"]
    ds_old = ds_scr[...]
    dxdt_parts, q1_parts, q2_parts = [], [], []
    ddiag = jnp.zeros((L, LANE), F32)
    early = []
    for g in range(SSD_GROUPS):
        gc = slice(g * GW, (g + 1) * GW)
        nc = slice(g * SSD_STATE, (g + 1) * SSD_STATE)
        bg = _mx(b_ref[:, nc])
        cg = _mx(c_ref[:, nc])
        dy_g = _mx(dy[:, gc])
        xdt_g = _mx(xdt[:, gc])
        dsg = ds_old[:, gc]
        spg = _mx(sp_ref[0, :, gc])
        dye_g = _mx(dye[:, gc])
        bds = _dot(bg, _mx(dsg))
        early.append(dict(
            bg=bg, cg=cg, dy_g=dy_g, bds=bds,
            cb=_dot(cg, bg, ((1,), (1,))), cbt=_dot(bg, cg, ((1,), (1,))),
            dm=_dot(_mx(jnp.concatenate([dy[:, gc] * m for m in masks], axis=0)), xdt_g, ((1,), (1,))),
            dmt=_dot(_mx(jnp.concatenate([xdt[:, gc] * m for m in masks], axis=0)), dy_g, ((1,), (1,))),
            dc=_dot(dye_g, spg, ((1,), (1,))), db=_dot(_mx(xw[:, gc]), _mx(dsg), ((1,), (1,)))))
        q1_parts.append(dye[:, gc] * _dot(cg, spg))
        q2_parts.append(xw[:, gc] * bds)
        ds_scr[:, gc] = dsg * q["cd_x"][:, gc] + _dot(cg, dye_g, ((0,), (0,)))
    for g in range(SSD_GROUPS):
        gc = slice(g * GW, (g + 1) * GW)
        e = early[g]
        bg, cg, dy_g, bds, cb, cbt, dm, dmt = (e[k] for k in ("bg", "cg", "dy_g", "bds", "cb", "cbt", "dm", "dmt"))
        dcb = jnp.zeros((L, L), F32)
        dcbt = jnp.zeros((L, L), F32)
        mts = []
        for r in range(SSD_REP):
            hh = hoff + g * SSD_REP + r
            col = q["cs_c"][:, hh:hh + 1]
            row = q["cs_r"][hh:hh + 1, :]
            dec = jnp.exp(jnp.where(q["tri"], col - row, NEG))
            dect = jnp.exp(jnp.where(q["trit"], row - col, NEG))
            pd = dm[r * L:(r + 1) * L] * dec
            pdt = dmt[r * L:(r + 1) * L] * dect
            dcb = dcb + pd
            dcbt = dcbt + pdt
            mts.append(_mx(cbt * dect))
            ddiag = ddiag + (jnp.sum(pd * cb, axis=1, keepdims=True) - jnp.sum(pdt * cbt, axis=1, keepdims=True)) * _onehot_lane(hh)
        x4 = _dot(jnp.concatenate(mts, axis=0), dy_g)
        dxdt_g = bds * q["w_x"][:, gc]
        for r in range(SSD_REP):
            dxdt_g = dxdt_g + x4[r * L:(r + 1) * L] * masks[r]
        dxdt_parts.append(dxdt_g)
        dc_g = _dot(_mx(dcb), bg) + e["dc"]
        db_g = _dot(_mx(dcbt), cg) + e["db"]
        boff = D_INNER + g * SSD_STATE
        coff = D_INNER + SSD_GROUPS * SSD_STATE + g * SSD_STATE
        if first:
            dx_ref[:, boff:boff + SSD_STATE] = db_g
            dx_ref[:, coff:coff + SSD_STATE] = dc_g
        else:
            dx_ref[:, boff:boff + SSD_STATE] = pdx_ref[:, boff:boff + SSD_STATE] + db_g
            dx_ref[:, coff:coff + SSD_STATE] = pdx_ref[:, coff:coff + SSD_STATE] + dc_g
    dxdt = jnp.concatenate(dxdt_parts, axis=1)
    et = et_ref[...]
    hs = _dot(_mx(jnp.concatenate([jnp.concatenate(q1_parts, axis=1), jnp.concatenate(q2_parts, axis=1), dxdt * xs], axis=0)), et)
    q1, q2, r3 = hs[0:L], hs[L:2 * L], hs[2 * L:3 * L]
    t_prev = _split_dot(jnp.sum(ds_old * sp_ref[0], axis=0, keepdims=True), et) * q["e_last"]
    rows = lax.broadcasted_iota(jnp.int32, (L, LANE), 0)
    dcs = ddiag + q1 - q2 + jnp.where(rows == q["il"], jnp.sum(q2, axis=0, keepdims=True) + t_prev, 0.0)
    dad = jnp.dot(q["trit"].astype(F32), dcs, precision=lax.Precision.HIGHEST, preferred_element_type=F32)
    ddt = dad * q["a_c"] + r3
    ddt_raw = ddt * _sigmoid(q["raw_c"])
    dal = jnp.sum(dad * q["dt_c"], axis=0, keepdims=True) * q["a_c"]
    dbias = jnp.sum(ddt_raw, axis=0, keepdims=True)
    sm_ref[...] += jnp.concatenate([dal, dbias, jnp.zeros((6, LANE), F32)], axis=0)
    dxs = dxdt * q["dt_x"]
    if first:
        dx_ref[:, 0:D_INNER] = dxs + dy * dsk_ref[...]
        ddt_ref[...] = ddt_raw
    else:
        dx_ref[:, 0:D_INNER] = pdx_ref[:, 0:D_INNER] + dxs
        ddt_ref[...] = jnp.concatenate([(pddt_ref[...] + ddt_raw).astype(ddt_ref.dtype), jnp.zeros((L, DTW - LANE), ddt_ref.dtype)],
                                       axis=1)


def _ssd_bwd(xbc, proj, dtt, bias, alog, dy, states, dskip_x, rev, prev=None, dproj=None, comm=None):
    t = xbc.shape[0]
    nc = t // CHUNK
    first = prev is None
    cm, xs, bb, cc, dtc, dtr, colv, rowv = _ssd_specs(t, not rev)
    hoff = SSD_HEADS if rev else 0
    e_np = _expand_np(hoff)
    e = jnp.asarray(e_np, BF16)
    et = jnp.asarray(e_np.T, BF16)
    st_spec = pl.BlockSpec((1, SSD_STATE, D_INNER), lambda i: (cm(i), 0, 0))
    dxo = pl.BlockSpec((CHUNK, CONV_DIM), lambda i: (cm(i), 0))
    ddto = pl.BlockSpec((CHUNK, LANE), lambda i: (cm(i), 0))
    smo = pl.BlockSpec((8, LANE), lambda i: (0, 0))
    in_specs = [xs, bb, cc, dtc, dtr, colv, rowv, colv, rowv, pl.BlockSpec((LANE, D_INNER), lambda i: (0, 0)),
                pl.BlockSpec((D_INNER, LANE), lambda i: (0, 0)), xs, st_spec, pl.BlockSpec((1, D_INNER), lambda i: (0, 0))]
    args = [xbc, xbc, xbc, proj, dtt, bias.reshape(1, LANE), bias.reshape(LANE, 1), alog.reshape(1, LANE),
            alog.reshape(LANE, 1), e, et, dy, states, dskip_x]
    out_specs = [dxo, ddto, smo]
    out_shape = [jax.ShapeDtypeStruct((t, CONV_DIM), F32), jax.ShapeDtypeStruct((t, LANE), F32),
                 jax.ShapeDtypeStruct((8, LANE), F32)]
    aliases = {}
    if not first:
        in_specs += [dxo, ddto, smo, ANY]
        args += list(prev) + [dproj]
        out_specs[1] = pl.BlockSpec((CHUNK, DTW), lambda i: (cm(i), DT0 // DTW))
        out_shape[1] = jax.ShapeDtypeStruct((t, NP), BF16)
        aliases = {17: 1}
    scratch = [pltpu.VMEM((SSD_STATE, D_INNER), F32)]
    if comm is not None:
        in_specs, args = in_specs + [ANY] * comm.n, args + comm.arrs
        out_specs, out_shape = out_specs + [ANY] * comm.n, out_shape + comm.out_shapes()
        scratch = scratch + comm.sems()
    return pl.pallas_call(
        functools.partial(_ssd_bwd_body, rev=rev, hoff=hoff, first=first, nsteps=nc, comm=comm),
        name="ssd_bwd_rev" if rev else "ssd_bwd", grid=(nc,), in_specs=in_specs, out_specs=out_specs, out_shape=out_shape,
        scratch_shapes=scratch, input_output_aliases=aliases, compiler_params=_params(1),
    )(*args)


def _gnorm_parts(yf_ref, yb_ref, xs_ref, z_ref, dsk_ref):
    xs = xs_ref[...]
    y = yf_ref[...] + yb_ref[...] + xs * dsk_ref[...]
    z = z_ref[...]
    sg = _sigmoid(z)
    s = z * sg
    v = y * s
    rs = []
    for g in range(SSD_GROUPS):
        vg = v[:, g * GW:(g + 1) * GW]
        rs.append(jnp.broadcast_to(lax.rsqrt(jnp.mean(vg * vg, axis=-1, keepdims=True) + EPS), vg.shape))
    return xs, y, z, sg, s, v, jnp.concatenate(rs, axis=1)


def _gnorm_fwd_body(yf_ref, yb_ref, xs_ref, z_ref, dsk_ref, w_ref, o_ref):
    _, _, _, _, _, v, rstd = _gnorm_parts(yf_ref, yb_ref, xs_ref, z_ref, dsk_ref)
    o_ref[...] = (v * rstd * w_ref[...]).astype(o_ref.dtype)


def _gnorm_fwd(yf, yb, xbc, proj, dskip_x, norm_w, tr=256):
    t = yf.shape[0]
    tr = min(tr, t)
    row = pl.BlockSpec((tr, D_INNER), lambda i: (i, 0))
    vec = pl.BlockSpec((1, D_INNER), lambda i: (0, 0))
    return pl.pallas_call(
        functools.partial(_gnorm_fwd_body), name="gnorm_fwd", grid=(t // tr,),
        in_specs=[row, row, row, pl.BlockSpec((tr, D_INNER), lambda i: (i, Z0 // D_INNER)), vec, vec],
        out_specs=row, out_shape=jax.ShapeDtypeStruct((t, D_INNER), BF16), compiler_params=_params(1),
    )(yf, yb, xbc, proj, dskip_x, norm_w.reshape(1, D_INNER))


def _gnorm_bwd_body(du_ref, yf_ref, yb_ref, xs_ref, z_ref, dsk_ref, w_ref, et_ref, _, dy_ref, dz_ref, dw_ref, dd_ref, acc_ref, *,
                    nsteps):
    i = pl.program_id(0)

    @pl.when(i == 0)
    def _():
        dw_ref[...] = jnp.zeros_like(dw_ref)
        acc_ref[...] = jnp.zeros_like(acc_ref)

    xs, y, z, sg, s, v, rstd = _gnorm_parts(yf_ref, yb_ref, xs_ref, z_ref, dsk_ref)
    du = du_ref[...]
    n = v * rstd
    dn = du * w_ref[...]
    dw_ref[...] += jnp.sum(du * n, axis=0, keepdims=True)
    prod = dn * n
    means = []
    for g in range(SSD_GROUPS):
        pg = prod[:, g * GW:(g + 1) * GW]
        means.append(jnp.broadcast_to(jnp.mean(pg, axis=-1, keepdims=True), pg.shape))
    dv = rstd * (dn - n * jnp.concatenate(means, axis=1))
    dy = dv * s
    dy_ref[...] = dy
    dz_ref[...] = (dv * y * (sg * (1.0 + z * (1.0 - sg)))).astype(dz_ref.dtype)
    acc_ref[...] += jnp.sum(dy * xs, axis=0, keepdims=True)

    @pl.when(i == nsteps - 1)
    def _():
        dd_ref[...] = _split_dot(acc_ref[...], et_ref[...])


def _gnorm_bwd(du, yf, yb, xbc, proj, dskip_x, norm_w, dproj, tr=256):
    t = yf.shape[0]
    tr = min(tr, t)
    row = pl.BlockSpec((tr, D_INNER), lambda i: (i, 0))
    zcol = pl.BlockSpec((tr, D_INNER), lambda i: (i, Z0 // D_INNER))
    vec = pl.BlockSpec((1, D_INNER), lambda i: (0, 0))
    et = jnp.asarray(_expand_np(0).T, BF16)
    return pl.pallas_call(
        functools.partial(_gnorm_bwd_body, nsteps=t // tr), name="gnorm_bwd", grid=(t // tr,),
        in_specs=[row, row, row, row, zcol, vec, vec, pl.BlockSpec((D_INNER, LANE), lambda i: (0, 0)), ANY],
        out_specs=[row, zcol, vec, pl.BlockSpec((1, LANE), lambda i: (0, 0))],
        out_shape=[jax.ShapeDtypeStruct((t, D_INNER), F32), jax.ShapeDtypeStruct((t, NP), BF16),
                   jax.ShapeDtypeStruct((1, D_INNER), F32), jax.ShapeDtypeStruct((1, LANE), F32)],
        scratch_shapes=[pltpu.VMEM((1, D_INNER), F32)], input_output_aliases={8: 1},
        compiler_params=_params(1),
    )(du, yf, yb, xbc, proj, dskip_x, norm_w.reshape(1, D_INNER), et, dproj)


def _gate_fwd_body(gt_ref, b_ref, ys_ref, ya_ref, o_ref):
    g = _sigmoid(gt_ref[...] + b_ref[...])
    o_ref[...] = (g[:, 0:D_MODEL] * ys_ref[...] + g[:, D_MODEL:] * ya_ref[...]).astype(o_ref.dtype)


def _gate_fwd(proj, b_gate, y_ssd, y_attn, tr=512):
    t = y_ssd.shape[0]
    tr = min(tr, t)
    row = pl.BlockSpec((tr, D_MODEL), lambda i: (i, 0))
    return pl.pallas_call(
        functools.partial(_gate_fwd_body), name="gate_fwd", grid=(t // tr,),
        in_specs=[pl.BlockSpec((tr, 2 * D_MODEL), lambda i: (i, GT0 // (2 * D_MODEL))),
                  pl.BlockSpec((1, 2 * D_MODEL), lambda i: (0, 0)), row, row],
        out_specs=row, out_shape=jax.ShapeDtypeStruct((t, D_MODEL), BF16), compiler_params=_params(1),
    )(proj, b_gate.reshape(1, 2 * D_MODEL), y_ssd, y_attn)


def _gate_bwd_body(dm_ref, gt_ref, b_ref, ys_ref, ya_ref, dys_ref, dya_ref, dgt_ref, db_ref):
    @pl.when(pl.program_id(0) == 0)
    def _():
        db_ref[...] = jnp.zeros_like(db_ref)

    g = _sigmoid(gt_ref[...] + b_ref[...])
    gs = g[:, 0:D_MODEL]
    ga = g[:, D_MODEL:]
    dm = dm_ref[...]
    dys_ref[...] = (dm * gs).astype(dys_ref.dtype)
    dya_ref[...] = (dm * ga).astype(dya_ref.dtype)
    dgt = jnp.concatenate([dm * ys_ref[...] * gs * (1.0 - gs), dm * ya_ref[...] * ga * (1.0 - ga)], axis=1)
    dgt_ref[...] = dgt.astype(dgt_ref.dtype)
    db_ref[...] += jnp.sum(dgt, axis=0, keepdims=True)


def _gate_bwd(dmix, proj, b_gate, y_ssd, y_attn, tr=512):
    t = y_ssd.shape[0]
    tr = min(tr, t)
    row = pl.BlockSpec((tr, D_MODEL), lambda i: (i, 0))
    gates = pl.BlockSpec((tr, 2 * D_MODEL), lambda i: (i, GT0 // (2 * D_MODEL)))
    vec2 = pl.BlockSpec((1, 2 * D_MODEL), lambda i: (0, 0))
    return pl.pallas_call(
        functools.partial(_gate_bwd_body), name="gate_bwd", grid=(t // tr,),
        in_specs=[row, gates, vec2, row, row],
        out_specs=[row, row, gates, vec2],
        out_shape=[jax.ShapeDtypeStruct((t, D_MODEL), BF16), jax.ShapeDtypeStruct((t, D_MODEL), BF16),
                   jax.ShapeDtypeStruct((t, NP), BF16), jax.ShapeDtypeStruct((1, 2 * D_MODEL), F32)],
        compiler_params=_params(1),
    )(dmix, proj, b_gate.reshape(1, 2 * D_MODEL), y_ssd, y_attn)


def _bucket_np():
    i = np.arange(BLOCK)[:, None]
    j = np.arange(3 * BLOCK)[None, :]
    rel = j - BLOCK - i
    nb = N_BUCKETS // 2
    max_exact = nb // 2
    ret = np.where(rel > 0, nb, 0)
    n = np.abs(rel)
    nf = np.maximum(n, 1).astype(np.float32)
    large = max_exact + (np.log(nf / np.float32(max_exact)) / np.float32(math.log(MAX_DISTANCE / max_exact))
                         * np.float32(nb - max_exact)).astype(np.int32)
    large = np.minimum(large, nb - 1)
    bucket = ret + np.where(n < max_exact, n, large)
    band = np.abs(rel) <= WINDOW
    return bucket, band


def _onehot_np():
    bucket, _ = _bucket_np()
    oh = np.zeros((BLOCK * 3 * BLOCK, LANE), np.float32)
    oh[np.arange(oh.shape[0]), bucket.reshape(-1)] = 1.0
    return oh


REP = Q_HEADS // KV_HEADS


def _group_cols(ref, g):
    return jnp.concatenate([ref[:, (g * REP + r) * HEAD_PAD:(g * REP + r + 1) * HEAD_PAD] for r in range(REP)], axis=0)


def _group_lane(ref, g, rows):
    return jnp.concatenate([jnp.broadcast_to(ref[:, g * REP + r:g * REP + r + 1], (rows, 1)) for r in range(REP)], axis=0)


def _attn_kv(refs, g):
    gc = slice(g * HEAD_PAD, (g + 1) * HEAD_PAD)
    return _mx(jnp.concatenate([r[:, gc] for r in refs], axis=0))


def _attn_logits(q_ref, kg, bias_ref, edge, g):
    q4 = _mx(_group_cols(q_ref, g))
    s = _dot(q4, kg, ((1,), (1,))) * (HEAD_DIM ** -0.5) + bias_ref[g] + edge
    return q4, s


def _attn_edge(n, nb):
    col = lax.broadcasted_iota(jnp.int32, (1, 3 * BLOCK), 1)
    bad = ((col < BLOCK) & (n == 0)) | ((col >= 2 * BLOCK) & (n == nb - 1))
    return jnp.where(bad, NEG, 0.0)


def _onehot_lane(h):
    return (lax.broadcasted_iota(jnp.int32, (1, LANE), 1) == h).astype(F32)


def _attn_fwd_body(*refs, nb, comm):
    ins, cin, (o_ref, lse_ref), cout, _, csem = _split_refs(refs, 9, 2, comm)
    q_ref, kp_ref, kc_ref, kn_ref, vp_ref, vc_ref, vn_ref, bias_ref, sink_ref = ins
    n = pl.program_id(0)
    if comm is not None:
        comm.run(cin, cout, csem, n, nb)
    edge = _attn_edge(n, nb)
    lse = jnp.zeros((BLOCK, LANE), F32)
    logits = [_attn_logits(q_ref, _attn_kv((kp_ref, kc_ref, kn_ref), g), bias_ref, edge, g)[1] for g in range(KV_HEADS)]
    for g in range(KV_HEADS):
        vg = _attn_kv((vp_ref, vc_ref, vn_ref), g)
        s = logits[g]
        sink = _group_lane(sink_ref, g, BLOCK)
        m = jnp.maximum(jnp.max(s, axis=-1, keepdims=True), sink)
        p = jnp.exp(s - m)
        den = jnp.sum(p, axis=-1, keepdims=True) + jnp.exp(sink - m)
        o4 = _dot(_mx(p * (1.0 / den)), vg)
        l4 = m + jnp.log(den)
        for r in range(REP):
            h = g * REP + r
            o_ref[:, h * HEAD_PAD:(h + 1) * HEAD_PAD] = o4[r * BLOCK:(r + 1) * BLOCK].astype(o_ref.dtype)
            lse = lse + l4[r * BLOCK:(r + 1) * BLOCK] * _onehot_lane(h)
    lse_ref[...] = lse


def _attn_specs(t, clamp):
    nb = t // BLOCK
    cur = (lambda n: jnp.minimum(n, nb - 1)) if clamp else (lambda n: n)
    prv = lambda n: jnp.maximum(cur(n) - 1, 0)
    nxt = lambda n: jnp.minimum(cur(n) + 1, nb - 1)
    kb, vb = K0 // KP, V0 // KP
    qs = pl.BlockSpec((BLOCK, QP), lambda n: (cur(n), Q0 // QP))
    ks = [pl.BlockSpec((BLOCK, KP), lambda n, f=f: (f(n), kb)) for f in (prv, cur, nxt)]
    vs = [pl.BlockSpec((BLOCK, KP), lambda n, f=f: (f(n), vb)) for f in (prv, cur, nxt)]
    consts = [pl.BlockSpec((KV_HEADS, REP * BLOCK, 3 * BLOCK), lambda n: (0, 0, 0)), pl.BlockSpec((1, LANE), lambda n: (0, 0))]
    return nb, cur, qs, ks, vs, consts


def _attn_fwd(proj, bias, sink, comm=None):
    t = proj.shape[0]
    nb, cur, qs, ks, vs, consts = _attn_specs(t, False)
    in_specs, args = [qs] + ks + vs + consts, [proj] * 7 + [bias, sink]
    out_specs = [pl.BlockSpec((BLOCK, QP), lambda n: (n, 0)), pl.BlockSpec((BLOCK, LANE), lambda n: (n, 0))]
    out_shape = [jax.ShapeDtypeStruct((t, QP), BF16), jax.ShapeDtypeStruct((t, LANE), F32)]
    scratch = []
    if comm is not None:
        in_specs, args = in_specs + [ANY] * comm.n, args + comm.arrs
        out_specs, out_shape = out_specs + [ANY] * comm.n, out_shape + comm.out_shapes()
        scratch = comm.sems()
    return pl.pallas_call(
        functools.partial(_attn_fwd_body, nb=nb, comm=comm), name="attn_fwd", grid=(nb,),
        in_specs=in_specs, out_specs=out_specs, out_shape=out_shape, scratch_shapes=scratch, compiler_params=_params(1),
    )(*args)


def _attn_bwd_body(*refs, nb, comm):
    ins, cin, (dqkv_ref, dbias_ref, dsink_ref), cout, (ak, bk, av, bv, dq_scr), csem = _split_refs(refs, 13, 3, comm)
    q_ref, kp_ref, kc_ref, kn_ref, vp_ref, vc_ref, vn_ref, bias_ref, sink_ref, do_ref, o_ref, lse_ref = ins[:12]
    dk_ref = dqkv_ref.at[:, K0:K0 + KP]
    dv_ref = dqkv_ref.at[:, V0:V0 + KP]
    n = pl.program_id(0)
    scale = HEAD_DIM ** -0.5
    if comm is not None:
        comm.run(cin, cout, csem, n, nb + 1)

    @pl.when(n == 0)
    def _():
        for r in (ak, bk, av, bv, dq_scr, dbias_ref, dsink_ref):
            r[...] = jnp.zeros_like(r)

    dqkv_ref[:, Q0:Q0 + QP] = dq_scr[...]

    @pl.when(n < nb)
    def _():
        edge = _attn_edge(n, nb)
        dsink = jnp.zeros((1, LANE), F32)
        early = []
        for g in range(KV_HEADS):
            kg = _attn_kv((kp_ref, kc_ref, kn_ref), g)
            q4, s = _attn_logits(q_ref, kg, bias_ref, edge, g)
            do = _group_cols(do_ref, g)
            early.append((kg, q4, s, do, _dot(_mx(do), _attn_kv((vp_ref, vc_ref, vn_ref), g), ((1,), (1,)))))
        for g in range(KV_HEADS):
            gc = slice(g * HEAD_PAD, (g + 1) * HEAD_PAD)
            kg, q4, s, do, dp = early[g]
            lse = _group_lane(lse_ref, g, BLOCK)
            p = jnp.exp(s - lse)
            delta = jnp.sum(do * _group_cols(o_ref, g).astype(F32), axis=-1, keepdims=True)
            ds = p * (dp - delta)
            sunk = jnp.exp(_group_lane(sink_ref, g, BLOCK) - lse) * delta
            dbias_ref[g] += ds
            dsb = _mx(ds)
            dq4 = _dot(dsb, kg) * scale
            for r in range(REP):
                h = g * REP + r
                dq_scr[:, h * HEAD_PAD:(h + 1) * HEAD_PAD] = dq4[r * BLOCK:(r + 1) * BLOCK].astype(dq_scr.dtype)
                dsink = dsink - jnp.sum(sunk[r * BLOCK:(r + 1) * BLOCK], axis=0, keepdims=True) * _onehot_lane(h)
            dk_c = _dot(dsb, q4, ((0,), (0,))) * scale
            dv_c = _dot(_mx(p), _mx(do), ((0,), (0,)))
            for acc_a, acc_b, out, c in ((ak, bk, dk_ref, dk_c), (av, bv, dv_ref, dv_c)):
                out[:, gc] = (acc_a[:, gc] + c[0:BLOCK]).astype(out.dtype)
                acc_a[:, gc] = acc_b[:, gc] + c[BLOCK:2 * BLOCK]
                acc_b[:, gc] = c[2 * BLOCK:3 * BLOCK]
        dsink_ref[...] += dsink

    @pl.when(n == nb)
    def _():
        dk_ref[...] = ak[...].astype(dk_ref.dtype)
        dv_ref[...] = av[...].astype(dv_ref.dtype)


def _attn_bwd(proj, bias, sink, do, o, lse, dproj, comm=None):
    t = proj.shape[0]
    nb, cur, qs, ks, vs, consts = _attn_specs(t, True)
    rowq = pl.BlockSpec((BLOCK, QP), lambda n: (cur(n), 0))
    late = pl.BlockSpec((BLOCK, QKV), lambda n: (jnp.maximum(n - 1, 0), Q0 // QKV))
    in_specs = [qs] + ks + vs + consts + [rowq, rowq, pl.BlockSpec((BLOCK, LANE), lambda n: (cur(n), 0)), ANY]
    args = [proj] * 7 + [bias, sink, do, o, lse, dproj]
    out_specs = [late, pl.BlockSpec((KV_HEADS, REP * BLOCK, 3 * BLOCK), lambda n: (0, 0, 0)), pl.BlockSpec((1, LANE), lambda n: (0, 0))]
    out_shape = [jax.ShapeDtypeStruct((t, NP), BF16), jax.ShapeDtypeStruct((KV_HEADS, REP * BLOCK, 3 * BLOCK), F32),
                 jax.ShapeDtypeStruct((1, LANE), F32)]
    scratch = [pltpu.VMEM((BLOCK, KP), F32) for _ in range(4)] + [pltpu.VMEM((BLOCK, QP), BF16)]
    if comm is not None:
        in_specs, args = in_specs + [ANY] * comm.n, args + comm.arrs
        out_specs, out_shape = out_specs + [ANY] * comm.n, out_shape + comm.out_shapes()
        scratch = scratch + comm.sems()
    return pl.pallas_call(
        functools.partial(_attn_bwd_body, nb=nb, comm=comm), name="attn_bwd", grid=(nb + 1,),
        in_specs=in_specs, out_specs=out_specs, out_shape=out_shape, scratch_shapes=scratch, input_output_aliases={12: 0},
        compiler_params=_params(1),
    )(*args)


def _adamw_body(gp_ref, w_ref, m_ref, v_ref, g_ref, d_ref, nm_ref, nv_ref, *, nparts):
    g = gp_ref[0].astype(F32)
    for j in range(1, nparts):
        g = g + gp_ref[j].astype(F32)
    m = ADAM_B1 * m_ref[...] + (1.0 - ADAM_B1) * g
    v = ADAM_B2 * v_ref[...] + (1.0 - ADAM_B2) * (g * g)
    m_hat = m / (1.0 - ADAM_B1 ** ADAM_STEP)
    v_hat = v / (1.0 - ADAM_B2 ** ADAM_STEP)
    g_ref[...] = g
    d_ref[...] = -ADAM_LR * (m_hat / (jnp.sqrt(v_hat) + ADAM_EPS) + ADAM_WD * w_ref[...])
    nm_ref[...] = m
    nv_ref[...] = v


def _adamw(gparts, w, m, v, name, tr=256):
    p, r, c = gparts.shape
    tr = min(tr, r)
    assert r % tr == 0
    row = pl.BlockSpec((tr, c), lambda i: (i, 0))
    sd = jax.ShapeDtypeStruct((r, c), F32)
    return pl.pallas_call(
        functools.partial(_adamw_body, nparts=p), name=name, grid=(r // tr,),
        in_specs=[pl.BlockSpec((p, tr, c), lambda i: (0, i, 0)), row, row, row],
        out_specs=[row, row, row, row], out_shape=[sd, sd, sd, sd], compiler_params=_params(1),
    )(gparts, w, m, v)


MESH = pl.DeviceIdType.MESH
N_REL = N_DEV - 1


def _gather_phases(ins, outs, sems):
    n = len(ins)
    send_sems, recv_sems, local_sems = sems
    x, y, c = lax.axis_index("x"), lax.axis_index("y"), lax.axis_index("c")
    me, sibling = (x, y, c), (x, y, 1 - c)
    chips = [(1 - x, y), (x, 1 - y), (1 - x, 1 - y)]

    def slot(a, p):
        return outs[a].at[4 * p[0] + 2 * p[1] + p[2]]

    def copy(a, k, block, to, src=None):
        return pltpu.make_async_remote_copy(
            src_ref=slot(a, block) if src is None else src, dst_ref=slot(a, block),
            send_sem=send_sems.at[a * N_REL + k], recv_sem=recv_sems.at[a * N_REL + k], device_id=to, device_id_type=MESH)

    mine = [pltpu.make_async_copy(ins[a], slot(a, me), local_sems.at[a]) for a in range(n)]
    first = []
    for a in range(n):
        first.append(copy(a, 0, me, sibling, src=ins[a]))
        first += [copy(a, 1 + j, me, (*chip, c), src=ins[a]) for j, chip in enumerate(chips)]
    passed = [copy(a, 4 + j, (*chip, c), sibling) for j, chip in enumerate(chips) for a in range(n)]

    def start():
        for cp in mine + first:
            cp.start()

    def pass_on():
        i = 0
        for j, chip in enumerate(chips):
            for a in range(n):
                copy(a, 1 + j, (*chip, c), me).wait_recv()
                passed[i].start()
                i += 1

    def finish():
        for a in range(n):
            copy(a, 0, sibling, me).wait_recv()
            for j, chip in enumerate(chips):
                copy(a, 4 + j, (*chip, 1 - c), me).wait_recv()
        for cp in first + passed:
            cp.wait_send()
        for cp in mine:
            cp.wait()

    return start, pass_on, finish


def _exchange_phases(ins, outs, sems):
    n = len(ins)
    send_sems, recv_sems, local_sems = sems
    x, y, c = lax.axis_index("x"), lax.axis_index("y"), lax.axis_index("c")
    me = 4 * x + 2 * y + c
    mine = [pltpu.make_async_copy(ins[a].at[me], outs[a].at[me], local_sems.at[a]) for a in range(n)]
    copies = []
    for a in range(n):
        for k in range(1, N_DEV):
            px = 1 - x if k & 4 else x
            py = 1 - y if k & 2 else y
            pc = 1 - c if k & 1 else c
            peer = 4 * px + 2 * py + pc
            send = pltpu.make_async_remote_copy(
                src_ref=ins[a].at[peer], dst_ref=outs[a].at[me], send_sem=send_sems.at[a * N_REL + k - 1],
                recv_sem=recv_sems.at[a * N_REL + k - 1], device_id=(px, py, pc), device_id_type=MESH)
            recv = pltpu.make_async_remote_copy(
                src_ref=ins[a].at[me], dst_ref=outs[a].at[peer], send_sem=send_sems.at[a * N_REL + k - 1],
                recv_sem=recv_sems.at[a * N_REL + k - 1], device_id=(px, py, pc), device_id_type=MESH)
            copies.append((send, recv))

    def start():
        for cp in mine:
            cp.start()
        for send, _ in copies:
            send.start()

    def finish():
        for send, recv in copies:
            send.wait_send()
            recv.wait_recv()
        for cp in mine:
            cp.wait()

    return start, None, finish


class _Comm:
    def __init__(self, kind, arrs):
        self.kind, self.arrs, self.n = kind, list(arrs), len(arrs)

    def out_shapes(self):
        if self.kind == "gather":
            return [jax.ShapeDtypeStruct((N_DEV,) + a.shape, a.dtype) for a in self.arrs]
        return [jax.ShapeDtypeStruct(a.shape, a.dtype) for a in self.arrs]

    def sems(self):
        return [pltpu.SemaphoreType.DMA((self.n * N_REL,)), pltpu.SemaphoreType.DMA((self.n * N_REL,)),
                pltpu.SemaphoreType.DMA((self.n,))]

    def run(self, ins, outs, sems, step, nsteps):
        start, mid, finish = (_gather_phases if self.kind == "gather" else _exchange_phases)(ins, outs, sems)
        if nsteps is None:
            start()
            if mid is not None:
                mid()
            finish()
            return
        pl.when(step == 0)(start)
        if mid is not None:
            pl.when(step == nsteps // 2)(mid)
        pl.when(step == nsteps - 1)(finish)


def _split_refs(refs, n_in, n_out, comm):
    k = comm.n if comm is not None else 0
    i = 0
    parts = []
    for cnt in (n_in, k, n_out, k):
        parts.append(refs[i:i + cnt])
        i += cnt
    rest = refs[i:]
    if comm is None:
        return parts[0], (), parts[2], (), rest, ()
    return parts[0], parts[1], parts[2], parts[3], rest[:len(rest) - 3], rest[len(rest) - 3:]


def _comm_body(*refs, comm):
    n = comm.n
    comm.run(refs[:n], refs[n:2 * n], refs[2 * n:], None, None)


def _communicate(comm, name, vmem=False):
    spec = pl.BlockSpec(memory_space=pltpu.VMEM if vmem else pl.ANY)
    return pl.pallas_call(
        functools.partial(_comm_body, comm=comm), name=name,
        in_specs=[spec] * comm.n, out_specs=[spec] * comm.n, out_shape=comm.out_shapes(), scratch_shapes=comm.sems(),
        compiler_params=pltpu.CompilerParams(vmem_limit_bytes=VMEM_LIMIT),
    )(*comm.arrs)


ANY = pl.BlockSpec(memory_space=pl.ANY)


def _pad_heads(w, axis):
    shp = w.shape
    heads = shp[axis] // HEAD_DIM
    w = w.reshape(shp[:axis] + (heads, HEAD_DIM) + shp[axis + 1:])
    pad = [(0, 0)] * w.ndim
    pad[axis + 1] = (0, HEAD_PAD - HEAD_DIM)
    w = jnp.pad(w, pad)
    return w.reshape(shp[:axis] + (heads * HEAD_PAD,) + shp[axis + 1:])


def _unpad_heads(w, axis):
    shp = w.shape
    heads = shp[axis] // HEAD_PAD
    w = w.reshape(shp[:axis] + (heads, HEAD_PAD) + shp[axis + 1:])
    w = lax.slice_in_dim(w, 0, HEAD_DIM, axis=axis + 1)
    return w.reshape(shp[:axis] + (heads * HEAD_DIM,) + shp[axis + 1:])


def _w_in_to_padded(w):
    idx = np.cumsum((0,) + IN_SPLITS)
    z, xbc, dt, q, k, v, gates = [w[:, idx[i]:idx[i + 1]] for i in range(7)]
    zeros = lambda n: jnp.zeros((w.shape[0], n), w.dtype)
    return jnp.concatenate([_pad_heads(q, 1), _pad_heads(k, 1), _pad_heads(v, 1), xbc, dt, zeros(DTW - 2 * SSD_HEADS), z, gates],
                           axis=1)


def _w_in_from_padded(w):
    z = w[:, Z0:Z0 + D_INNER]
    gates = w[:, GT0:GT0 + 2 * D_MODEL]
    q = _unpad_heads(w[:, Q0:Q0 + QP], 1)
    xbc = w[:, XBC0:XBC0 + CONV_DIM]
    k = _unpad_heads(w[:, K0:K0 + KP], 1)
    v = _unpad_heads(w[:, V0:V0 + KP], 1)
    dt = w[:, DT0:DT0 + 2 * SSD_HEADS]
    return jnp.concatenate([z, xbc, dt, q, k, v, gates], axis=1)


def _pad_lane(v):
    v = v.reshape(-1)
    return jnp.pad(v, (0, LANE - v.shape[0]))


def _layer_consts(p):
    c = dict(p)
    c["conv_w8"] = jnp.pad(p["conv_w"].reshape(SSD_CONV, CONV_DIM), ((0, 8 - SSD_CONV), (0, 0)))
    c["conv_b1"] = p["conv_b"].reshape(1, CONV_DIM)
    c["dtb"] = _pad_lane(p["dt_bias"])
    c["alog"] = _pad_lane(p["a_log"])
    c["dskip_x"] = jnp.repeat(p["d_skip"], SSD_HEAD_DIM).reshape(1, D_INNER)
    c["sink"] = _pad_lane(p["attn_sink"]).reshape(1, LANE)
    return c


def _hosted(res, n_own, hook):
    res = list(res)
    if hook is not None:
        hook[1](res[n_own:])
    return res[0] if n_own == 1 else res[:n_own]


def _layer_fwd(x, c, w, bias, l, hooks):
    comm = lambda k: hooks[k][0] if k in hooks else None
    s = {"x": x}
    s["h"] = _rms_fwd(x, c["pre_mix_norm"], f"pre_mix_norm_{l}", out_dtype=BF16)
    res = _mm(s["h"], w["w_in"], f"in_proj_{l}", comm=comm("in_proj"))
    proj = _hosted(res, 1, hooks.get("in_proj")) if "in_proj" in hooks else res
    s["proj"] = proj
    s["xbc"] = _conv_fwd(proj, c["conv_w8"], c["conv_b1"])
    s["dtt"] = _dt_rows(proj)
    s["yf"], s["stf"] = _hosted(_ssd_fwd(s["xbc"], proj, s["dtt"], c["dtb"], c["alog"], False, comm=comm("ssd_fwd")), 2,
                                hooks.get("ssd_fwd"))
    s["yb"], s["stb"] = _hosted(_ssd_fwd(s["xbc"], proj, s["dtt"], c["dtb"], c["alog"], True, comm=comm("ssd_fwd_rev")), 2,
                                hooks.get("ssd_fwd_rev"))
    s["u"] = _gnorm_fwd(s["yf"], s["yb"], s["xbc"], proj, c["dskip_x"], c["ssd_norm"])
    s["y_ssd"] = _mm(s["u"], w["w_ssd_out"], f"ssd_out_{l}")
    s["o"], s["lse"] = _hosted(_attn_fwd(proj, bias, c["sink"], comm=comm("attn_fwd")), 2, hooks.get("attn_fwd"))
    s["y_attn"] = _mm(s["o"], w["w_attn_out"], f"attn_out_{l}")
    s["mix"] = _gate_fwd(proj, c["b_gate"], s["y_ssd"], s["y_attn"])
    s["mixed"] = _mm(s["mix"], w["w_o"], f"w_o_{l}")
    s["x1"] = _rms_fwd(s["mixed"], c["post_mix_norm"], f"post_mix_norm_{l}", res=x)
    s["h2"] = _rms_fwd(s["x1"], c["pre_mlp_norm"], f"pre_mlp_norm_{l}", out_dtype=BF16)
    s["f1"], s["a"] = _mm(s["h2"], w["w_mlp_in"], f"mlp_in_{l}", mode="relu2")
    s["f"] = _mm(s["a"], w["w_mlp_out"], f"mlp_out_{l}")
    x2 = _rms_fwd(s["f"], c["post_mlp_norm"], f"post_mlp_norm_{l}", res=s["x1"])
    return x2, s


def _send_rest(gw):
    s_mi = gw["w_mlp_in"].reshape(D_MODEL, N_DEV, -1).transpose(1, 0, 2).astype(BF16)
    parts = [(_unpad_heads(gw[n], 0) if n == "w_attn_out" else gw[n]).reshape(N_DEV, -1, D_MODEL) for n in ROWS]
    return [s_mi, jnp.concatenate(parts, axis=1).astype(BF16)]


def _send_w_in(g):
    return [_w_in_from_padded(g).reshape(D_MODEL, N_DEV, -1).transpose(1, 0, 2).astype(BF16)]


def _layer_bwd(dx2, s, c, w, bias, l, earlier=None, host_own=False):
    gw, gs, got = {}, {}, {}
    df, gs["post_mlp_norm"] = _rms_bwd(s["f"], c["post_mlp_norm"], dx2, f"post_mlp_norm_bwd_{l}", out_dtype=BF16)
    df1 = _mm(df, w["w_mlp_out"], f"mlp_out_dx_{l}", tb=True, mode="relu2_bwd", extra=s["f1"], out_dtype=BF16)
    gw["w_mlp_out"] = _mm(s["a"], df, f"mlp_out_dw_{l}", ta=True)
    gw["w_mlp_in"] = _mm(s["h2"], df1, f"mlp_in_dw_{l}", ta=True)
    dh2 = _mm(df1, w["w_mlp_in"], f"mlp_in_dx_{l}", tb=True)
    dx1, gs["pre_mlp_norm"] = _rms_bwd(s["x1"], c["pre_mlp_norm"], dh2, f"pre_mlp_norm_bwd_{l}", add=dx2)
    dmixed, gs["post_mix_norm"] = _rms_bwd(s["mixed"], c["post_mix_norm"], dx1, f"post_mix_norm_bwd_{l}", out_dtype=BF16)
    gw["w_o"] = _mm(s["mix"], dmixed, f"w_o_dw_{l}", ta=True)
    dmix = _mm(dmixed, w["w_o"], f"w_o_dx_{l}", tb=True)
    dys, dya, dproj, gs["b_gate"] = _gate_bwd(dmix, s["proj"], c["b_gate"], s["y_ssd"], s["y_attn"])
    gw["w_ssd_out"] = _mm(s["u"], dys, f"ssd_out_dw_{l}", ta=True)
    du = _mm(dys, w["w_ssd_out"], f"ssd_out_dx_{l}", tb=True)
    gw["w_attn_out"] = _mm(s["o"], dya, f"attn_out_dw_{l}", ta=True)
    do = _mm(dya, w["w_attn_out"], f"attn_out_dx_{l}", tb=True)
    res = _attn_bwd(s["proj"], bias, c["sink"], do, s["o"], s["lse"], dproj,
                    comm=None if earlier is None else _Comm("exchange", earlier[:1]))
    (dproj, dbias, dsink), got["earlier"] = res[:3], list(res[3:])
    gs["attn_sink"] = dsink
    dy, dproj, gs["ssd_norm"], gs["d_skip"] = _gnorm_bwd(du, s["yf"], s["yb"], s["xbc"], s["proj"], c["dskip_x"], c["ssd_norm"],
                                                         dproj)
    res = _ssd_bwd(s["xbc"], s["proj"], s["dtt"], c["dtb"], c["alog"], dy, s["stf"], c["dskip_x"], False,
                   comm=None if earlier is None else _Comm("exchange", earlier[1:]))
    part, got["earlier"] = res[:3], got["earlier"] + list(res[3:])
    res = _ssd_bwd(s["xbc"], s["proj"], s["dtt"], c["dtb"], c["alog"], dy, s["stb"], c["dskip_x"], True, prev=part, dproj=dproj,
                   comm=_Comm("exchange", _send_rest(gw)) if host_own else None)
    (dxbc, dproj, sm), got["rest"] = res[:3], res[3:]
    gs["a_log"], gs["dt_bias"] = sm[0:1], sm[1:2]
    dproj, gs["conv"] = _conv_bwd(s["proj"], dxbc, c["conv_w8"], c["conv_b1"], dproj)
    gw["w_in"] = _mm(s["h"], dproj, f"in_proj_dw_{l}", ta=True)
    if host_own:
        dh, *got["w_in"] = _mm(dproj, w["w_in"], f"in_proj_dx_{l}", tb=True, comm=_Comm("exchange", _send_w_in(gw["w_in"])))
    else:
        dh = _mm(dproj, w["w_in"], f"in_proj_dx_{l}", tb=True)
    dx, gs["pre_mix_norm"] = _rms_bwd(s["x"], c["pre_mix_norm"], dh, f"pre_mix_norm_bwd_{l}", add=dx1)
    return dx, gw, gs, dbias, got


SMALL = (("pre_mix_norm", DEPTH * D_MODEL), ("b_gate", DEPTH * 2 * D_MODEL), ("conv_w", DEPTH * SSD_CONV * CONV_DIM),
         ("conv_b", DEPTH * CONV_DIM), ("dt_bias", DEPTH * 2 * SSD_HEADS), ("a_log", DEPTH * 2 * SSD_HEADS),
         ("d_skip", DEPTH * SSD_HEADS), ("ssd_norm", DEPTH * D_INNER), ("attn_sink", DEPTH * Q_HEADS),
         ("rel_bias_table", N_BUCKETS * Q_HEADS), ("post_mix_norm", DEPTH * D_MODEL), ("pre_mlp_norm", DEPTH * D_MODEL),
         ("post_mlp_norm", DEPTH * D_MODEL))


def _small_rows(n):
    return -(-n // LANE)


def _pack_small(vals, fill=0.0):
    rows = []
    for name, n in SMALL:
        v = vals[name].reshape(-1).astype(F32)
        rows.append(jnp.pad(v, (0, _small_rows(n) * LANE - n), constant_values=fill).reshape(-1, LANE))
    out = jnp.concatenate(rows, axis=0)
    return jnp.pad(out, ((0, -out.shape[0] % 8), (0, 0)), constant_values=fill)


def _unpack_small(packed, shapes):
    out, r = {}, 0
    for name, n in SMALL:
        nr = _small_rows(n)
        out[name] = packed[r:r + nr].reshape(-1)[:n].reshape(shapes[name])
        r += nr
    return out


BIG = ("w_in", "w_ssd_out", "w_attn_out", "w_o", "w_mlp_in", "w_mlp_out")
ROWS = ("w_ssd_out", "w_attn_out", "w_o", "w_mlp_out")


def _pack_rows(vals, lead):
    return jnp.concatenate([vals[n].reshape(lead + (-1, D_MODEL)) for n in ROWS], axis=len(lead))


def kernel(x, pre_mix_norm, w_in, b_gate, conv_w, conv_b, dt_bias, a_log, d_skip, ssd_norm, w_ssd_out, attn_sink, rel_bias_table, w_attn_out, w_o, post_mix_norm, pre_mlp_norm, w_mlp_in, w_mlp_out, post_mlp_norm, loss_target, m_pre_mix_norm, m_w_in, m_b_gate, m_conv_w, m_conv_b, m_dt_bias, m_a_log, m_d_skip, m_ssd_norm, m_w_ssd_out, m_attn_sink, m_rel_bias_table, m_w_attn_out, m_w_o, m_post_mix_norm, m_pre_mlp_norm, m_w_mlp_in, m_w_mlp_out, m_post_mlp_norm, v_pre_mix_norm, v_w_in, v_b_gate, v_conv_w, v_conv_b, v_dt_bias, v_a_log, v_d_skip, v_ssd_norm, v_w_ssd_out, v_attn_sink, v_rel_bias_table, v_w_attn_out, v_w_o, v_post_mix_norm, v_pre_mlp_norm, v_w_mlp_in, v_w_mlp_out, v_post_mlp_norm):
    names = ("pre_mix_norm", "w_in", "b_gate", "conv_w", "conv_b", "dt_bias", "a_log", "d_skip", "ssd_norm", "w_ssd_out",
             "attn_sink", "rel_bias_table", "w_attn_out", "w_o", "post_mix_norm", "pre_mlp_norm", "w_mlp_in", "w_mlp_out",
             "post_mlp_norm")
    W = dict(zip(names, (pre_mix_norm, w_in, b_gate, conv_w, conv_b, dt_bias, a_log, d_skip, ssd_norm, w_ssd_out, attn_sink,
                         rel_bias_table, w_attn_out, w_o, post_mix_norm, pre_mlp_norm, w_mlp_in, w_mlp_out, post_mlp_norm)))
    M = dict(zip(names, (m_pre_mix_norm, m_w_in, m_b_gate, m_conv_w, m_conv_b, m_dt_bias, m_a_log, m_d_skip, m_ssd_norm,
                         m_w_ssd_out, m_attn_sink, m_rel_bias_table, m_w_attn_out, m_w_o, m_post_mix_norm, m_pre_mlp_norm,
                         m_w_mlp_in, m_w_mlp_out, m_post_mlp_norm)))
    V = dict(zip(names, (v_pre_mix_norm, v_w_in, v_b_gate, v_conv_w, v_conv_b, v_dt_bias, v_a_log, v_d_skip, v_ssd_norm,
                         v_w_ssd_out, v_attn_sink, v_rel_bias_table, v_w_attn_out, v_w_o, v_post_mix_norm, v_pre_mlp_norm,
                         v_w_mlp_in, v_w_mlp_out, v_post_mlp_norm)))
    t = x.shape[1]
    shard = {n: W[n].shape for n in names}
    dev = 4 * lax.axis_index("x") + 2 * lax.axis_index("y") + lax.axis_index("c")
    cshard = CONV_DIM // N_DEV

    sh_in = [w_in[l].astype(BF16) for l in range(DEPTH)]
    sh_rest = [[w_mlp_in[l].astype(BF16), _pack_rows({n: W[n][l].astype(BF16) for n in ROWS}, ())] for l in range(DEPTH)]
    full = [{} for _ in range(DEPTH)]

    def take_w_in(l):
        def take(got):
            full[l]["w_in"] = _w_in_to_padded(got[0].transpose(1, 0, 2).reshape(D_MODEL, N_IN))
        return take

    def take_mlp_in(l):
        def take(got):
            full[l]["w_mlp_in"] = got[0].transpose(1, 0, 2).reshape(D_MODEL, D_FF)
        return take

    def take_rows(l):
        def take(got):
            r = 0
            for n in ROWS:
                per = shard[n][1]
                full[l][n] = got[0][:, r:r + per].reshape(N_DEV * per, D_MODEL)
                r += per
            full[l]["w_attn_out"] = _pad_heads(full[l]["w_attn_out"], 0)
        return take

    def take_rest(l):
        def take(got):
            take_mlp_in(l)(got[:1])
            take_rows(l)(got[1:])
        return take

    take_w_in(0)(_communicate(_Comm("gather", [sh_in[0]]), "gather_w_in_0"))
    (g_conv,) = _communicate(_Comm("gather", [conv_w.reshape(-1, LANE)]), "gather_conv_w", vmem=True)
    conv_full = g_conv.reshape(N_DEV, DEPTH, SSD_CONV, cshard).transpose(1, 2, 0, 3).reshape(DEPTH, SSD_CONV, CONV_DIM)
    hooks = [{"in_proj": (_Comm("gather", sh_rest[0]), take_rest(0)),
              "ssd_fwd": (_Comm("gather", sh_rest[1][1:]), take_rows(1)),
              "ssd_fwd_rev": (_Comm("gather", sh_rest[1][:1]), take_mlp_in(1)),
              "attn_fwd": (_Comm("gather", [sh_in[1]]), take_w_in(1))}, {}]

    _, band = _bucket_np()
    onehot = jnp.asarray(_onehot_np(), BF16)
    table_t = jnp.pad(rel_bias_table.T, ((0, 0), (0, LANE - N_BUCKETS)))
    bias = _mm(table_t, onehot, "t5_bias", tb=True, tm=Q_HEADS, tn=3 * BLOCK * BLOCK // 8, split_a=True)
    bias = (bias.reshape(Q_HEADS, BLOCK, 3 * BLOCK) + jnp.asarray(np.where(band, 0.0, NEG), F32)).reshape(
        KV_HEADS, REP * BLOCK, 3 * BLOCK)

    consts = []
    for l in range(DEPTH):
        p = {n: W[n][l] for n in names if n not in BIG and n not in ("rel_bias_table", "conv_w")}
        p["conv_w"] = conv_full[l]
        consts.append(_layer_consts(p))
    h = x[0]
    saved = []
    for l in range(DEPTH):
        h, s = _layer_fwd(h, consts[l], full[l], bias, l, hooks[l])
        saved.append(s)
    dy, lsum = _loss(h, loss_target[0])
    loss = lax.psum(0.5 / D_MODEL * jnp.sum(lsum), ("x", "y", "c"))

    gss, dbs = [None] * DEPTH, [None] * DEPTH
    dy, gw1, gss[1], dbs[1], _ = _layer_bwd(dy, saved[1], consts[1], full[1], bias, 1)
    dy, _, gss[0], dbs[0], got = _layer_bwd(dy, saved[0], consts[0], full[0], bias, 0,
                                            earlier=_send_w_in(gw1["w_in"]) + _send_rest(gw1), host_own=True)
    recv = [list(got["w_in"]) + list(got["rest"]), list(got["earlier"])]
    grad_x = dy[None]
    dbias = jnp.concatenate([d.reshape(Q_HEADS, -1) for d in dbs], axis=1)
    d_table = _mm(dbias, jnp.concatenate([onehot] * DEPTH, axis=0), "t5_bias_bwd", tm=Q_HEADS, tk=3 * BLOCK * BLOCK // 8,
                  split_a=True)

    sg = {}
    for n in ("pre_mix_norm", "b_gate", "ssd_norm", "post_mix_norm", "pre_mlp_norm", "post_mlp_norm"):
        sg[n] = jnp.stack([gss[l][n].reshape(-1) for l in range(DEPTH)])
    sg["conv_w"] = jnp.stack([gss[l]["conv"][0:SSD_CONV] for l in range(DEPTH)])
    sg["conv_b"] = jnp.stack([gss[l]["conv"][SSD_CONV] for l in range(DEPTH)])
    sg["dt_bias"] = jnp.stack([gss[l]["dt_bias"][0, 0:2 * SSD_HEADS] for l in range(DEPTH)])
    sg["a_log"] = jnp.stack([gss[l]["a_log"][0, 0:2 * SSD_HEADS] for l in range(DEPTH)])
    sg["d_skip"] = jnp.stack([gss[l]["d_skip"][0, 0:SSD_HEADS] for l in range(DEPTH)])
    sg["attn_sink"] = jnp.stack([gss[l]["attn_sink"][0, 0:Q_HEADS] for l in range(DEPTH)])
    sg["rel_bias_table"] = d_table[:, 0:N_BUCKETS].T
    (small_parts,) = _communicate(_Comm("gather", [_pack_small(sg)]), "gather_small_grads", vmem=True)
    sshape = {n: W[n].shape for n, _ in SMALL}
    sshape["conv_w"] = (DEPTH, SSD_CONV, 1, CONV_DIM)
    pk = lambda d, fill: _pack_small({n: (jnp.full((DEPTH, SSD_CONV, CONV_DIM), fill, F32) if n == "conv_w" else d[n])
                                      for n, _ in SMALL}, fill)
    s_out = [_unpack_small(o, sshape) for o in _adamw(small_parts, pk(W, 1.0), pk(M, 1.0), pk(V, 1.0), "adamw_small", tr=1024)]
    conv_g = lax.dynamic_slice_in_dim(s_out[0]["conv_w"], dev * cshard, cshard, axis=3)
    c_out = _adamw(conv_g.reshape(1, DEPTH * SSD_CONV, cshard), conv_w.reshape(-1, cshard), m_conv_w.reshape(-1, cshard),
                   v_conv_w.reshape(-1, cshard), "adamw_conv_w", tr=DEPTH * SSD_CONV)
    for i in range(4):
        s_out[i]["conv_w"] = c_out[i].reshape(conv_w.shape)

    layer_out = []
    for l in range(DEPTH):
        r_in, r_mi, r_rows = recv[l]
        lo = {"w_in": _adamw(r_in, w_in[l], m_w_in[l], v_w_in[l], f"adamw_w_in_{l}"),
              "w_mlp_in": _adamw(r_mi, w_mlp_in[l], m_w_mlp_in[l], v_w_mlp_in[l], f"adamw_w_mlp_in_{l}")}
        o = _adamw(r_rows, *[_pack_rows({n: d[n][l] for n in ROWS}, ()) for d in (W, M, V)], f"adamw_rows_{l}")
        r = 0
        for n in ROWS:
            per = shard[n][1]
            lo[n] = [a[r:r + per] for a in o]
            r += per
        layer_out.append(lo)
    b_out = {n: [jnp.stack([layer_out[l][n][i] for l in range(DEPTH)]) for i in range(4)] for n in BIG}

    outs = [loss, grad_x]
    for i in range(4):
        for n in names:
            outs.append(b_out[n][i] if n in BIG else s_out[i][n])
    return tuple(outs)
```

```python
import functools
import math

import numpy as np
import jax
import jax.numpy as jnp
from jax import lax
from jax.experimental import pallas as pl
from jax.experimental.pallas import tpu as pltpu

F32 = jnp.float32
BF16 = jnp.bfloat16

D_MODEL = 1024
DEPTH = 2
D_INNER = 2048
SSD_HEADS = 32
SSD_HEAD_DIM = 64
SSD_GROUPS = 8
SSD_REP = 4
SSD_STATE = 128
SSD_CONV = 5
CHUNK = 128
CONV_DIM = 4096
Q_HEADS = 16
KV_HEADS = 4
HEAD_DIM = 64
ATTN_WIDTH = 1024
KV_WIDTH = 256
WINDOW = 128
BLOCK = 128
N_BUCKETS = 32
MAX_DISTANCE = 128
D_FF = 4096
EPS = 1e-6
IN_SPLITS = (D_INNER, CONV_DIM, 2 * SSD_HEADS, ATTN_WIDTH, KV_WIDTH, KV_WIDTH, 2 * D_MODEL)
N_IN = sum(IN_SPLITS)
N_DEV = 8

ADAM_LR = 0.001
ADAM_B1 = 0.9
ADAM_B2 = 0.999
ADAM_EPS = 1e-08
ADAM_WD = 0.01
ADAM_STEP = 10

LANE = 128
HEAD_PAD = LANE
QP = Q_HEADS * HEAD_PAD
KP = KV_HEADS * HEAD_PAD
QKV = QP + 2 * KP
Q0 = 0
K0 = QP
V0 = QP + KP
XBC0 = QKV
DT0 = XBC0 + CONV_DIM
DTW = 1024
Z0 = DT0 + DTW
GT0 = Z0 + D_INNER
NP = GT0 + 2 * D_MODEL
NEG = -1e30
VMEM_LIMIT = 56 * 1024 * 1024


def _params(n_grid):
    return pltpu.CompilerParams(dimension_semantics=("arbitrary",) * n_grid, vmem_limit_bytes=VMEM_LIMIT)


def _dot(a, b, dims=((1,), (0,))):
    return lax.dot_general(a, b, (dims, ((), ())), preferred_element_type=F32)


def _mx(a):
    return a.astype(BF16)


def _split_dot(x, e):
    hi = x.astype(BF16)
    lo = (x - hi.astype(F32)).astype(BF16)
    return _dot(hi, e) + _dot(lo, e)


def _softplus(x):
    u = jnp.exp(-jnp.abs(x))
    w = 1.0 + u
    log1p = jnp.where(w == 1.0, u, jnp.log(w) * u / jnp.where(w == 1.0, 1.0, w - 1.0))
    return jnp.maximum(x, 0.0) + log1p


def _sigmoid(x):
    return 1.0 / (1.0 + jnp.exp(-x))


def _mm_body(*refs, grid, ta, tb, split_a, mode, comm):
    n_in = 3 if mode == "relu2_bwd" else 2
    n_out = 2 if mode in ("relu2", "dt_side") else 1
    ins, cin, outs, cout, (acc_ref,), csem = _split_refs(refs, n_in, n_out, comm)
    a_ref, b_ref = ins[0], ins[1]
    i, j, k = pl.program_id(0), pl.program_id(1), pl.program_id(2)
    nk = grid[2]
    if comm is not None:
        comm.run(cin, cout, csem, (i * grid[1] + j) * nk + k, grid[0] * grid[1] * nk)

    dims = ((0 if ta else 1,), (1 if tb else 0,))
    b = _mx(b_ref[...])
    if split_a:
        a32 = a_ref[...].astype(F32)
        hi = a32.astype(BF16)
        lo = (a32 - hi.astype(F32)).astype(BF16)
        part = _dot(hi, b, dims) + _dot(lo, b, dims)
    else:
        part = _dot(_mx(a_ref[...]), b, dims)

    if nk > 1:
        @pl.when(k == 0)
        def _():
            acc_ref[...] = jnp.zeros_like(acc_ref)

        acc_ref[...] += part

    @pl.when(k == nk - 1)
    def _():
        acc = part if nk == 1 else acc_ref[...]
        if mode == "relu2":
            outs[0][...] = acc.astype(outs[0].dtype)
            r = jnp.maximum(acc, 0.0)
            outs[1][...] = (r * r).astype(outs[1].dtype)
        elif mode == "relu2_bwd":
            outs[0][...] = (acc * 2.0 * jnp.maximum(ins[2][...].astype(F32), 0.0)).astype(outs[0].dtype)
        else:
            outs[0][...] = acc.astype(outs[0].dtype)
            if mode == "dt_side":
                @pl.when(j == DT0 // outs[0].shape[1])
                def _():
                    off = DT0 % outs[0].shape[1]
                    outs[1][...] = acc[:, off:off + LANE]


def _tile(n, pref):
    t = min(n, pref)
    while n % t:
        t -= LANE
    assert t > 0 and n % t == 0, (n, pref)
    return t


def _mm(a, b, name, ta=False, tb=False, out_dtype=F32, tm=1024, tn=2048, tk=1024, split_a=False, mode=None, extra=None,
        comm=None):
    m, k = (a.shape[1], a.shape[0]) if ta else a.shape
    n = b.shape[0] if tb else b.shape[1]
    assert (b.shape[1] if tb else b.shape[0]) == k
    tm, tn, tk = _tile(m, tm), _tile(n, tn), _tile(k, tk)
    grid = (m // tm, n // tn, k // tk)
    a_spec = pl.BlockSpec((tk, tm), lambda i, j, kk: (kk, i)) if ta else pl.BlockSpec((tm, tk), lambda i, j, kk: (i, kk))
    b_spec = pl.BlockSpec((tn, tk), lambda i, j, kk: (j, kk)) if tb else pl.BlockSpec((tk, tn), lambda i, j, kk: (kk, j))
    o_spec = pl.BlockSpec((tm, tn), lambda i, j, kk: (i, j))
    in_specs, args = [a_spec, b_spec], [a, b]
    out_specs, out_shape = [o_spec], [jax.ShapeDtypeStruct((m, n), out_dtype)]
    if mode == "relu2":
        out_specs, out_shape = [o_spec, o_spec], [jax.ShapeDtypeStruct((m, n), BF16), jax.ShapeDtypeStruct((m, n), BF16)]
    elif mode == "dt_side":
        assert DT0 % tn + LANE <= tn
        out_specs = [o_spec, pl.BlockSpec((tm, LANE), lambda i, j, kk: (i, 0))]
        out_shape = out_shape + [jax.ShapeDtypeStruct((m, LANE), F32)]
    elif mode == "relu2_bwd":
        in_specs, args = in_specs + [o_spec], args + [extra]
    scratch = [pltpu.VMEM((tm, tn) if grid[2] > 1 else (8, LANE), F32)]
    if comm is not None:
        in_specs, args = in_specs + [ANY] * comm.n, args + comm.arrs
        out_specs, out_shape = out_specs + [ANY] * comm.n, out_shape + comm.out_shapes()
        scratch = scratch + comm.sems()
    res = pl.pallas_call(
        functools.partial(_mm_body, grid=grid, ta=ta, tb=tb, split_a=split_a, mode=mode, comm=comm),
        name=name, grid=grid, in_specs=in_specs, out_specs=out_specs, out_shape=out_shape, scratch_shapes=scratch,
        compiler_params=_params(3),
    )(*args)
    return res[0] if len(res) == 1 else res


def _rms_fwd_body(*refs, has_res):
    if has_res:
        x_ref, g_ref, r_ref, o_ref = refs
    else:
        x_ref, g_ref, o_ref = refs
    x = x_ref[...]
    y = x * lax.rsqrt(jnp.mean(x * x, axis=-1, keepdims=True) + EPS) * g_ref[...]
    if has_res:
        y = r_ref[...] + y
    o_ref[...] = y.astype(o_ref.dtype)


def _rms_fwd(x, g, name, res=None, out_dtype=F32, tr=512):
    t, d = x.shape
    tr = min(tr, t)
    row = pl.BlockSpec((tr, d), lambda i: (i, 0))
    vec = pl.BlockSpec((1, d), lambda i: (0, 0))
    args = [x, g.reshape(1, d)] + ([res] if res is not None else [])
    return pl.pallas_call(
        functools.partial(_rms_fwd_body, has_res=res is not None),
        name=name,
        grid=(t // tr,),
        in_specs=[row, vec] + ([row] if res is not None else []),
        out_specs=row,
        out_shape=jax.ShapeDtypeStruct((t, d), out_dtype),
        compiler_params=_params(1),
    )(*args)


def _rms_bwd_body(*refs, has_add):
    if has_add:
        x_ref, g_ref, dy_ref, add_ref, dx_ref, dg_ref = refs
    else:
        x_ref, g_ref, dy_ref, dx_ref, dg_ref = refs

    @pl.when(pl.program_id(0) == 0)
    def _():
        dg_ref[...] = jnp.zeros_like(dg_ref)

    x = x_ref[...]
    dy = dy_ref[...].astype(F32)
    rstd = lax.rsqrt(jnp.mean(x * x, axis=-1, keepdims=True) + EPS)
    n = x * rstd
    dn = dy * g_ref[...]
    dx = rstd * (dn - n * jnp.mean(dn * n, axis=-1, keepdims=True))
    if has_add:
        dx = dx + add_ref[...]
    dx_ref[...] = dx.astype(dx_ref.dtype)
    dg_ref[...] += jnp.sum(dy * n, axis=0, keepdims=True)


def _rms_bwd(x, g, dy, name, add=None, out_dtype=F32, tr=512):
    t, d = x.shape
    tr = min(tr, t)
    row = pl.BlockSpec((tr, d), lambda i: (i, 0))
    vec = pl.BlockSpec((1, d), lambda i: (0, 0))
    args = [x, g.reshape(1, d), dy] + ([add] if add is not None else [])
    return pl.pallas_call(
        functools.partial(_rms_bwd_body, has_add=add is not None),
        name=name,
        grid=(t // tr,),
        in_specs=[row, vec, row] + ([row] if add is not None else []),
        out_specs=[row, vec],
        out_shape=[jax.ShapeDtypeStruct((t, d), out_dtype), jax.ShapeDtypeStruct((1, d), F32)],
        compiler_params=_params(1),
    )(*args)


def _loss_body(y_ref, t_ref, dy_ref, l_ref):
    @pl.when(pl.program_id(0) == 0)
    def _():
        l_ref[...] = jnp.zeros_like(l_ref)

    e = y_ref[...] - t_ref[...]
    dy_ref[...] = e * (1.0 / D_MODEL)
    s = jnp.sum(e * e, axis=0, keepdims=True)
    acc = s[:, 0:LANE]
    for j in range(1, D_MODEL // LANE):
        acc = acc + s[:, j * LANE:(j + 1) * LANE]
    l_ref[...] += acc


def _loss(y, target, tr=512):
    t, d = y.shape
    tr = min(tr, t)
    row = pl.BlockSpec((tr, d), lambda i: (i, 0))
    return pl.pallas_call(
        functools.partial(_loss_body), name="loss_head", grid=(t // tr,), in_specs=[row, row],
        out_specs=[row, pl.BlockSpec((1, LANE), lambda i: (0, 0))],
        out_shape=[jax.ShapeDtypeStruct((t, d), F32), jax.ShapeDtypeStruct((1, LANE), F32)],
        compiler_params=_params(1),
    )(y, target)


CONV_CT = 128
CONV_RB = 512
CONV_PAD = 8


def _conv_rows(t):
    return min(CONV_RB, t)


def _conv_fwd_body(u_ref, w_ref, b_ref, o_ref, pad_ref, *, t):
    rb = _conv_rows(t)
    zeros = jnp.zeros((CONV_PAD, CONV_CT), F32)
    pad_ref[pl.ds(0, CONV_PAD), :] = zeros
    pad_ref[pl.ds(t + CONV_PAD, CONV_PAD), :] = zeros
    pad_ref[pl.ds(CONV_PAD, t), :] = u_ref[...].astype(F32)
    w = w_ref[...]
    b = b_ref[...]
    for c in range(t // rb):
        base = CONV_PAD + c * rb
        acc = b + w[0:1, :] * pad_ref[pl.ds(base - 2, rb), :]
        for k in range(1, SSD_CONV):
            acc = acc + w[k:k + 1, :] * pad_ref[pl.ds(base + k - 2, rb), :]
        o_ref[pl.ds(c * rb, rb), :] = acc * _sigmoid(acc)


def _conv_fwd(proj, conv_w8, conv_b):
    t = proj.shape[0]
    off = XBC0 // CONV_CT
    return pl.pallas_call(
        functools.partial(_conv_fwd_body, t=t), name="conv_fwd", grid=(CONV_DIM // CONV_CT,),
        in_specs=[pl.BlockSpec((t, CONV_CT), lambda i: (0, off + i)), pl.BlockSpec((8, CONV_CT), lambda i: (0, i)),
                  pl.BlockSpec((1, CONV_CT), lambda i: (0, i))],
        out_specs=pl.BlockSpec((t, CONV_CT), lambda i: (0, i)),
        out_shape=jax.ShapeDtypeStruct((t, CONV_DIM), F32),
        scratch_shapes=[pltpu.VMEM((t + 2 * CONV_PAD, CONV_CT), F32)],
        compiler_params=_params(1),
    )(proj, conv_w8, conv_b)


def _conv_bwd_body(u_ref, da_ref, w_ref, b_ref, _, du_ref, dw_ref, pad_ref, pad2_ref, *, t):
    rb = _conv_rows(t)
    zeros = jnp.zeros((CONV_PAD, CONV_CT), F32)
    for p in (pad_ref, pad2_ref):
        p[pl.ds(0, CONV_PAD), :] = zeros
        p[pl.ds(t + CONV_PAD, CONV_PAD), :] = zeros
    pad_ref[pl.ds(CONV_PAD, t), :] = u_ref[...].astype(F32)
    w = w_ref[...]
    b = b_ref[...]
    dw = [jnp.zeros((1, CONV_CT), F32) for _ in range(SSD_CONV + 1)]
    for c in range(t // rb):
        base = CONV_PAD + c * rb
        us = [pad_ref[pl.ds(base + k - 2, rb), :] for k in range(SSD_CONV)]
        acc = b + w[0:1, :] * us[0]
        for k in range(1, SSD_CONV):
            acc = acc + w[k:k + 1, :] * us[k]
        sg = _sigmoid(acc)
        dyc = da_ref[pl.ds(c * rb, rb), :] * (sg * (1.0 + acc * (1.0 - sg)))
        pad2_ref[pl.ds(base, rb), :] = dyc
        for k in range(SSD_CONV):
            dw[k] = dw[k] + jnp.sum(dyc * us[k], axis=0, keepdims=True)
        dw[SSD_CONV] = dw[SSD_CONV] + jnp.sum(dyc, axis=0, keepdims=True)
    for c in range(t // rb):
        base = CONV_PAD + c * rb
        acc = w[0:1, :] * pad2_ref[pl.ds(base + 2, rb), :]
        for k in range(1, SSD_CONV):
            acc = acc + w[k:k + 1, :] * pad2_ref[pl.ds(base + 2 - k, rb), :]
        du_ref[pl.ds(c * rb, rb), :] = acc.astype(du_ref.dtype)
    dw_ref[...] = jnp.concatenate(dw + [jnp.zeros((2, CONV_CT), F32)], axis=0)


def _conv_bwd(proj, dact, conv_w8, conv_b, dproj):
    t = proj.shape[0]
    off = XBC0 // CONV_CT
    col = pl.BlockSpec((t, CONV_CT), lambda i: (0, i))
    seg = pl.BlockSpec((t, CONV_CT), lambda i: (0, off + i))
    return pl.pallas_call(
        functools.partial(_conv_bwd_body, t=t), name="conv_bwd", grid=(CONV_DIM // CONV_CT,),
        in_specs=[seg, col, pl.BlockSpec((8, CONV_CT), lambda i: (0, i)), pl.BlockSpec((1, CONV_CT), lambda i: (0, i)), ANY],
        out_specs=[seg, pl.BlockSpec((8, CONV_CT), lambda i: (0, i))],
        out_shape=[jax.ShapeDtypeStruct((t, NP), BF16), jax.ShapeDtypeStruct((8, CONV_DIM), F32)],
        scratch_shapes=[pltpu.VMEM((t + 2 * CONV_PAD, CONV_CT), F32), pltpu.VMEM((t + 2 * CONV_PAD, CONV_CT), F32)],
        input_output_aliases={4: 0}, compiler_params=_params(1),
    )(proj, dact, conv_w8, conv_b, dproj)


GW = SSD_REP * SSD_HEAD_DIM


def _expand_np(hoff):
    e = np.zeros((LANE, D_INNER), np.float32)
    for h in range(SSD_HEADS):
        e[hoff + h, h * SSD_HEAD_DIM:(h + 1) * SSD_HEAD_DIM] = 1.0
    return e


def _head_masks():
    lane = lax.broadcasted_iota(jnp.int32, (1, GW), 1)
    return [((lane >= r * SSD_HEAD_DIM) & (lane < (r + 1) * SSD_HEAD_DIM)).astype(F32) for r in range(SSD_REP)]


def _ssd_common(dtc_ref, dtr_ref, bc_ref, br_ref, alc_ref, alr_ref, e_ref, rev):
    L = CHUNK
    ri = lax.broadcasted_iota(jnp.int32, (L, L), 0)
    ci = lax.broadcasted_iota(jnp.int32, (L, L), 1)
    tri = (ri <= ci) if rev else (ri >= ci)
    trit = (ri >= ci) if rev else (ri <= ci)
    raw_c = dtc_ref[...] + bc_ref[...]
    dt_c = _softplus(raw_c)
    a_c = -jnp.exp(alc_ref[...])
    cs_c = jnp.dot(tri.astype(F32), dt_c * a_c, precision=lax.Precision.HIGHEST, preferred_element_type=F32)
    dt_r = _softplus(dtr_ref[...] + br_ref[...])
    a_r = -jnp.exp(alr_ref[...])
    cs_r = jnp.dot(dt_r * a_r, trit.astype(F32), precision=lax.Precision.HIGHEST, preferred_element_type=F32)
    il = 0 if rev else L - 1
    e_c = jnp.exp(cs_c)
    w_c = jnp.exp(cs_c[il:il + 1, :] - cs_c)
    ex = _dot(_mx(jnp.concatenate([dt_c, e_c, w_c], axis=0)), e_ref[...])
    return dict(tri=tri, trit=trit, raw_c=raw_c, dt_c=dt_c, a_c=a_c, cs_c=cs_c, cs_r=cs_r, il=il, e_last=e_c[il:il + 1],
                dt_x=ex[0:L], e_x=ex[L:2 * L], w_x=ex[2 * L:3 * L], cd_x=ex[L + il:L + il + 1])


def _ssd_fwd_body(*refs, rev, hoff, nsteps, comm):
    L = CHUNK
    ins, cin, (y_ref, st_ref), cout, (s_scr,), csem = _split_refs(refs, 10, 2, comm)
    xs_ref, b_ref, c_ref, dtc_ref, dtr_ref, bc_ref, br_ref, alc_ref, alr_ref, e_ref = ins
    if comm is not None:
        comm.run(cin, cout, csem, pl.program_id(0), nsteps)

    @pl.when(pl.program_id(0) == 0)
    def _():
        s_scr[...] = jnp.zeros_like(s_scr)

    st_ref[0] = s_scr[...]
    q = _ssd_common(dtc_ref, dtr_ref, bc_ref, br_ref, alc_ref, alr_ref, e_ref, rev)
    masks = _head_masks()
    xdt = xs_ref[...] * q["dt_x"]
    xw = xdt * q["w_x"]
    early = []
    for g in range(SSD_GROUPS):
        gc = slice(g * GW, (g + 1) * GW)
        nc = slice(g * SSD_STATE, (g + 1) * SSD_STATE)
        bg = _mx(b_ref[:, nc])
        cg = _mx(c_ref[:, nc])
        sg = s_scr[:, gc]
        early.append((_dot(cg, bg, ((1,), (1,))), _dot(cg, _mx(sg)) * q["e_x"][:, gc]))
        s_scr[:, gc] = sg * q["cd_x"][:, gc] + _dot(bg, _mx(xw[:, gc]), ((0,), (0,)))
    for g in range(SSD_GROUPS):
        gc = slice(g * GW, (g + 1) * GW)
        cb, y = early[g]
        ms = []
        for r in range(SSD_REP):
            hh = hoff + g * SSD_REP + r
            diff = q["cs_c"][:, hh:hh + 1] - q["cs_r"][hh:hh + 1, :]
            ms.append(_mx(cb * jnp.exp(jnp.where(q["tri"], diff, NEG))))
        y4 = _dot(jnp.concatenate(ms, axis=0), _mx(xdt[:, gc]))
        for r in range(SSD_REP):
            y = y + y4[r * L:(r + 1) * L] * masks[r]
        y_ref[:, gc] = y


def _dt_rows_body(p_ref, o_ref):
    o_ref[...] = p_ref[...].T


def _dt_rows(proj, tr=512):
    t = proj.shape[0]
    tr = min(tr, t)
    return pl.pallas_call(
        functools.partial(_dt_rows_body), name="dt_rows", grid=(t // tr,),
        in_specs=[pl.BlockSpec((tr, LANE), lambda i: (i, 0))], out_specs=pl.BlockSpec((LANE, tr), lambda i: (0, i)),
        out_shape=jax.ShapeDtypeStruct((LANE, t), F32), compiler_params=_params(1),
    )(proj)


def _ssd_specs(t, rev):
    nc = t // CHUNK
    cm = (lambda i: nc - 1 - i) if rev else (lambda i: i)
    xs = pl.BlockSpec((CHUNK, D_INNER), lambda i: (cm(i), 0))
    bb = pl.BlockSpec((CHUNK, SSD_GROUPS * SSD_STATE), lambda i: (cm(i), 2))
    cc = pl.BlockSpec((CHUNK, SSD_GROUPS * SSD_STATE), lambda i: (cm(i), 3))
    dtc = pl.BlockSpec((CHUNK, LANE), lambda i: (cm(i), 0))
    dtr = pl.BlockSpec((LANE, CHUNK), lambda i: (0, cm(i)))
    colv = pl.BlockSpec((1, LANE), lambda i: (0, 0))
    rowv = pl.BlockSpec((LANE, 1), lambda i: (0, 0))
    return cm, xs, bb, cc, dtc, dtr, colv, rowv


def _ssd_fwd(xbc, proj, dtt, bias, alog, rev, comm=None):
    t = xbc.shape[0]
    nc = t // CHUNK
    cm, xs, bb, cc, dtc, dtr, colv, rowv = _ssd_specs(t, rev)
    hoff = SSD_HEADS if rev else 0
    e = jnp.asarray(_expand_np(hoff), BF16)
    in_specs = [xs, bb, cc, dtc, dtr, colv, rowv, colv, rowv, pl.BlockSpec((LANE, D_INNER), lambda i: (0, 0))]
    args = [xbc, xbc, xbc, proj, dtt, bias.reshape(1, LANE), bias.reshape(LANE, 1), alog.reshape(1, LANE), alog.reshape(LANE, 1), e]
    out_specs = [xs, pl.BlockSpec((1, SSD_STATE, D_INNER), lambda i: (cm(i), 0, 0))]
    out_shape = [jax.ShapeDtypeStruct((t, D_INNER), F32), jax.ShapeDtypeStruct((nc, SSD_STATE, D_INNER), F32)]
    scratch = [pltpu.VMEM((SSD_STATE, D_INNER), F32)]
    if comm is not None:
        in_specs, args = in_specs + [ANY] * comm.n, args + comm.arrs
        out_specs, out_shape = out_specs + [ANY] * comm.n, out_shape + comm.out_shapes()
        scratch = scratch + comm.sems()
    return pl.pallas_call(
        functools.partial(_ssd_fwd_body, rev=rev, hoff=hoff, nsteps=nc, comm=comm), name="ssd_fwd_rev" if rev else "ssd_fwd",
        grid=(nc,), in_specs=in_specs, out_specs=out_specs, out_shape=out_shape, scratch_shapes=scratch,
        compiler_params=_params(1),
    )(*args)


def _ssd_bwd_body(*refs, rev, hoff, first, nsteps, comm):
    L = CHUNK
    ins, cin, (dx_ref, ddt_ref, sm_ref), cout, (ds_scr,), csem = _split_refs(refs, 14 if first else 18, 3, comm)
    (xs_ref, b_ref, c_ref, dtc_ref, dtr_ref, bc_ref, br_ref, alc_ref, alr_ref, e_ref, et_ref, dy_ref, sp_ref,
     dsk_ref) = ins[:14]
    if not first:
        pdx_ref, pddt_ref, psm_ref = ins[14:17]
    if comm is not None:
        comm.run(cin, cout, csem, pl.program_id(0), nsteps)

    @pl.when(pl.program_id(0) == 0)
    def _():
        ds_scr[...] = jnp.zeros_like(ds_scr)
        sm_ref[...] = jnp.zeros_like(sm_ref) if first else psm_ref[...]

    q = _ssd_common(dtc_ref, dtr_ref, bc_ref, br_ref, alc_ref, alr_ref, e_ref, rev)
    masks = _head_masks()
    xs = xs_ref[...]
    dy = dy_ref[...]
    xdt = xs * q["dt_x"]
    xw = xdt * q["w_x"]
    dye = dy * q["e_x"]
    ds_old = ds_scr[...]
    dxdt_parts, q1_parts, q2_parts = [], [], []
    ddiag = jnp.zeros((L, LANE), F32)
    early = []
    for g in range(SSD_GROUPS):
        gc = slice(g * GW, (g + 1) * GW)
        nc = slice(g * SSD_STATE, (g + 1) * SSD_STATE)
        bg = _mx(b_ref[:, nc])
        cg = _mx(c_ref[:, nc])
        dy_g = _mx(dy[:, gc])
        xdt_g = _mx(xdt[:, gc])
        dsg = ds_old[:, gc]
        spg = _mx(sp_ref[0, :, gc])
        dye_g = _mx(dye[:, gc])
        bds = _dot(bg, _mx(dsg))
        early.append(dict(
            bg=bg, cg=cg, dy_g=dy_g, bds=bds,
            cb=_dot(cg, bg, ((1,), (1,))), cbt=_dot(bg, cg, ((1,), (1,))),
            dm=_dot(_mx(jnp.concatenate([dy[:, gc] * m for m in masks], axis=0)), xdt_g, ((1,), (1,))),
            dmt=_dot(_mx(jnp.concatenate([xdt[:, gc] * m for m in masks], axis=0)), dy_g, ((1,), (1,))),
            dc=_dot(dye_g, spg, ((1,), (1,))), db=_dot(_mx(xw[:, gc]), _mx(dsg), ((1,), (1,)))))
        q1_parts.append(dye[:, gc] * _dot(cg, spg))
        q2_parts.append(xw[:, gc] * bds)
        ds_scr[:, gc] = dsg * q["cd_x"][:, gc] + _dot(cg, dye_g, ((0,), (0,)))
    for g in range(SSD_GROUPS):
        gc = slice(g * GW, (g + 1) * GW)
        e = early[g]
        bg, cg, dy_g, bds, cb, cbt, dm, dmt = (e[k] for k in ("bg", "cg", "dy_g", "bds", "cb", "cbt", "dm", "dmt"))
        dcb = jnp.zeros((L, L), F32)
        dcbt = jnp.zeros((L, L), F32)
        mts = []
        for r in range(SSD_REP):
            hh = hoff + g * SSD_REP + r
            col = q["cs_c"][:, hh:hh + 1]
            row = q["cs_r"][hh:hh + 1, :]
            dec = jnp.exp(jnp.where(q["tri"], col - row, NEG))
            dect = jnp.exp(jnp.where(q["trit"], row - col, NEG))
            pd = dm[r * L:(r + 1) * L] * dec
            pdt = dmt[r * L:(r + 1) * L] * dect
            dcb = dcb + pd
            dcbt = dcbt + pdt
            mts.append(_mx(cbt * dect))
            ddiag = ddiag + (jnp.sum(pd * cb, axis=1, keepdims=True) - jnp.sum(pdt * cbt, axis=1, keepdims=True)) * _onehot_lane(hh)
        x4 = _dot(jnp.concatenate(mts, axis=0), dy_g)
        dxdt_g = bds * q["w_x"][:, gc]
        for r in range(SSD_REP):
            dxdt_g = dxdt_g + x4[r * L:(r + 1) * L] * masks[r]
        dxdt_parts.append(dxdt_g)
        dc_g = _dot(_mx(dcb), bg) + e["dc"]
        db_g = _dot(_mx(dcbt), cg) + e["db"]
        boff = D_INNER + g * SSD_STATE
        coff = D_INNER + SSD_GROUPS * SSD_STATE + g * SSD_STATE
        if first:
            dx_ref[:, boff:boff + SSD_STATE] = db_g
            dx_ref[:, coff:coff + SSD_STATE] = dc_g
        else:
            dx_ref[:, boff:boff + SSD_STATE] = pdx_ref[:, boff:boff + SSD_STATE] + db_g
            dx_ref[:, coff:coff + SSD_STATE] = pdx_ref[:, coff:coff + SSD_STATE] + dc_g
    dxdt = jnp.concatenate(dxdt_parts, axis=1)
    et = et_ref[...]
    hs = _dot(_mx(jnp.concatenate([jnp.concatenate(q1_parts, axis=1), jnp.concatenate(q2_parts, axis=1), dxdt * xs], axis=0)), et)
    q1, q2, r3 = hs[0:L], hs[L:2 * L], hs[2 * L:3 * L]
    t_prev = _split_dot(jnp.sum(ds_old * sp_ref[0], axis=0, keepdims=True), et) * q["e_last"]
    rows = lax.broadcasted_iota(jnp.int32, (L, LANE), 0)
    dcs = ddiag + q1 - q2 + jnp.where(rows == q["il"], jnp.sum(q2, axis=0, keepdims=True) + t_prev, 0.0)
    dad = jnp.dot(q["trit"].astype(F32), dcs, precision=lax.Precision.HIGHEST, preferred_element_type=F32)
    ddt = dad * q["a_c"] + r3
    ddt_raw = ddt * _sigmoid(q["raw_c"])
    dal = jnp.sum(dad * q["dt_c"], axis=0, keepdims=True) * q["a_c"]
    dbias = jnp.sum(ddt_raw, axis=0, keepdims=True)
    sm_ref[...] += jnp.concatenate([dal, dbias, jnp.zeros((6, LANE), F32)], axis=0)
    dxs = dxdt * q["dt_x"]
    if first:
        dx_ref[:, 0:D_INNER] = dxs + dy * dsk_ref[...]
        ddt_ref[...] = ddt_raw
    else:
        dx_ref[:, 0:D_INNER] = pdx_ref[:, 0:D_INNER] + dxs
        ddt_ref[...] = jnp.concatenate([(pddt_ref[...] + ddt_raw).astype(ddt_ref.dtype), jnp.zeros((L, DTW - LANE), ddt_ref.dtype)],
                                       axis=1)


def _ssd_bwd(xbc, proj, dtt, bias, alog, dy, states, dskip_x, rev, prev=None, dproj=None, comm=None):
    t = xbc.shape[0]
    nc = t // CHUNK
    first = prev is None
    cm, xs, bb, cc, dtc, dtr, colv, rowv = _ssd_specs(t, not rev)
    hoff = SSD_HEADS if rev else 0
    e_np = _expand_np(hoff)
    e = jnp.asarray(e_np, BF16)
    et = jnp.asarray(e_np.T, BF16)
    st_spec = pl.BlockSpec((1, SSD_STATE, D_INNER), lambda i: (cm(i), 0, 0))
    dxo = pl.BlockSpec((CHUNK, CONV_DIM), lambda i: (cm(i), 0))
    ddto = pl.BlockSpec((CHUNK, LANE), lambda i: (cm(i), 0))
    smo = pl.BlockSpec((8, LANE), lambda i: (0, 0))
    in_specs = [xs, bb, cc, dtc, dtr, colv, rowv, colv, rowv, pl.BlockSpec((LANE, D_INNER), lambda i: (0, 0)),
                pl.BlockSpec((D_INNER, LANE), lambda i: (0, 0)), xs, st_spec, pl.BlockSpec((1, D_INNER), lambda i: (0, 0))]
    args = [xbc, xbc, xbc, proj, dtt, bias.reshape(1, LANE), bias.reshape(LANE, 1), alog.reshape(1, LANE),
            alog.reshape(LANE, 1), e, et, dy, states, dskip_x]
    out_specs = [dxo, ddto, smo]
    out_shape = [jax.ShapeDtypeStruct((t, CONV_DIM), F32), jax.ShapeDtypeStruct((t, LANE), F32),
                 jax.ShapeDtypeStruct((8, LANE), F32)]
    aliases = {}
    if not first:
        in_specs += [dxo, ddto, smo, ANY]
        args += list(prev) + [dproj]
        out_specs[1] = pl.BlockSpec((CHUNK, DTW), lambda i: (cm(i), DT0 // DTW))
        out_shape[1] = jax.ShapeDtypeStruct((t, NP), BF16)
        aliases = {17: 1}
    scratch = [pltpu.VMEM((SSD_STATE, D_INNER), F32)]
    if comm is not None:
        in_specs, args = in_specs + [ANY] * comm.n, args + comm.arrs
        out_specs, out_shape = out_specs + [ANY] * comm.n, out_shape + comm.out_shapes()
        scratch = scratch + comm.sems()
    return pl.pallas_call(
        functools.partial(_ssd_bwd_body, rev=rev, hoff=hoff, first=first, nsteps=nc, comm=comm),
        name="ssd_bwd_rev" if rev else "ssd_bwd", grid=(nc,), in_specs=in_specs, out_specs=out_specs, out_shape=out_shape,
        scratch_shapes=scratch, input_output_aliases=aliases, compiler_params=_params(1),
    )(*args)


def _gnorm_parts(yf_ref, yb_ref, xs_ref, z_ref, dsk_ref):
    xs = xs_ref[...]
    y = yf_ref[...] + yb_ref[...] + xs * dsk_ref[...]
    z = z_ref[...].astype(F32)
    sg = _sigmoid(z)
    s = z * sg
    v = y * s
    rs = []
    for g in range(SSD_GROUPS):
        vg = v[:, g * GW:(g + 1) * GW]
        rs.append(jnp.broadcast_to(lax.rsqrt(jnp.mean(vg * vg, axis=-1, keepdims=True) + EPS), vg.shape))
    return xs, y, z, sg, s, v, jnp.concatenate(rs, axis=1)


def _gnorm_fwd_body(yf_ref, yb_ref, xs_ref, z_ref, dsk_ref, w_ref, o_ref):
    _, _, _, _, _, v, rstd = _gnorm_parts(yf_ref, yb_ref, xs_ref, z_ref, dsk_ref)
    o_ref[...] = (v * rstd * w_ref[...]).astype(o_ref.dtype)


def _gnorm_fwd(yf, yb, xbc, proj, dskip_x, norm_w, tr=256):
    t = yf.shape[0]
    tr = min(tr, t)
    row = pl.BlockSpec((tr, D_INNER), lambda i: (i, 0))
    vec = pl.BlockSpec((1, D_INNER), lambda i: (0, 0))
    return pl.pallas_call(
        functools.partial(_gnorm_fwd_body), name="gnorm_fwd", grid=(t // tr,),
        in_specs=[row, row, row, pl.BlockSpec((tr, D_INNER), lambda i: (i, Z0 // D_INNER)), vec, vec],
        out_specs=row, out_shape=jax.ShapeDtypeStruct((t, D_INNER), BF16), compiler_params=_params(1),
    )(yf, yb, xbc, proj, dskip_x, norm_w.reshape(1, D_INNER))


def _gnorm_bwd_body(du_ref, yf_ref, yb_ref, xs_ref, z_ref, dsk_ref, w_ref, et_ref, _, dy_ref, dz_ref, dw_ref, dd_ref, acc_ref, *,
                    nsteps):
    i = pl.program_id(0)

    @pl.when(i == 0)
    def _():
        dw_ref[...] = jnp.zeros_like(dw_ref)
        acc_ref[...] = jnp.zeros_like(acc_ref)

    xs, y, z, sg, s, v, rstd = _gnorm_parts(yf_ref, yb_ref, xs_ref, z_ref, dsk_ref)
    du = du_ref[...].astype(F32)
    n = v * rstd
    dn = du * w_ref[...]
    dw_ref[...] += jnp.sum(du * n, axis=0, keepdims=True)
    prod = dn * n
    means = []
    for g in range(SSD_GROUPS):
        pg = prod[:, g * GW:(g + 1) * GW]
        means.append(jnp.broadcast_to(jnp.mean(pg, axis=-1, keepdims=True), pg.shape))
    dv = rstd * (dn - n * jnp.concatenate(means, axis=1))
    dy = dv * s
    dy_ref[...] = dy
    dz_ref[...] = (dv * y * (sg * (1.0 + z * (1.0 - sg)))).astype(dz_ref.dtype)
    acc_ref[...] += jnp.sum(dy * xs, axis=0, keepdims=True)

    @pl.when(i == nsteps - 1)
    def _():
        dd_ref[...] = _split_dot(acc_ref[...], et_ref[...])


def _gnorm_bwd(du, yf, yb, xbc, proj, dskip_x, norm_w, dproj, tr=256):
    t = yf.shape[0]
    tr = min(tr, t)
    row = pl.BlockSpec((tr, D_INNER), lambda i: (i, 0))
    zcol = pl.BlockSpec((tr, D_INNER), lambda i: (i, Z0 // D_INNER))
    vec = pl.BlockSpec((1, D_INNER), lambda i: (0, 0))
    et = jnp.asarray(_expand_np(0).T, BF16)
    return pl.pallas_call(
        functools.partial(_gnorm_bwd_body, nsteps=t // tr), name="gnorm_bwd", grid=(t // tr,),
        in_specs=[row, row, row, row, zcol, vec, vec, pl.BlockSpec((D_INNER, LANE), lambda i: (0, 0)), ANY],
        out_specs=[row, zcol, vec, pl.BlockSpec((1, LANE), lambda i: (0, 0))],
        out_shape=[jax.ShapeDtypeStruct((t, D_INNER), F32), jax.ShapeDtypeStruct((t, NP), BF16),
                   jax.ShapeDtypeStruct((1, D_INNER), F32), jax.ShapeDtypeStruct((1, LANE), F32)],
        scratch_shapes=[pltpu.VMEM((1, D_INNER), F32)], input_output_aliases={8: 1},
        compiler_params=_params(1),
    )(du, yf, yb, xbc, proj, dskip_x, norm_w.reshape(1, D_INNER), et, dproj)


def _gate_fwd_body(gt_ref, b_ref, ys_ref, ya_ref, o_ref):
    g = _sigmoid(gt_ref[...].astype(F32) + b_ref[...])
    o_ref[...] = (g[:, 0:D_MODEL] * ys_ref[...] + g[:, D_MODEL:] * ya_ref[...]).astype(o_ref.dtype)


def _gate_fwd(proj, b_gate, y_ssd, y_attn, tr=512):
    t = y_ssd.shape[0]
    tr = min(tr, t)
    row = pl.BlockSpec((tr, D_MODEL), lambda i: (i, 0))
    return pl.pallas_call(
        functools.partial(_gate_fwd_body), name="gate_fwd", grid=(t // tr,),
        in_specs=[pl.BlockSpec((tr, 2 * D_MODEL), lambda i: (i, GT0 // (2 * D_MODEL))),
                  pl.BlockSpec((1, 2 * D_MODEL), lambda i: (0, 0)), row, row],
        out_specs=row, out_shape=jax.ShapeDtypeStruct((t, D_MODEL), BF16), compiler_params=_params(1),
    )(proj, b_gate.reshape(1, 2 * D_MODEL), y_ssd, y_attn)


def _gate_bwd_body(dm_ref, gt_ref, b_ref, ys_ref, ya_ref, dys_ref, dya_ref, dgt_ref, db_ref):
    @pl.when(pl.program_id(0) == 0)
    def _():
        db_ref[...] = jnp.zeros_like(db_ref)

    g = _sigmoid(gt_ref[...].astype(F32) + b_ref[...])
    gs = g[:, 0:D_MODEL]
    ga = g[:, D_MODEL:]
    dm = dm_ref[...].astype(F32)
    dys_ref[...] = (dm * gs).astype(dys_ref.dtype)
    dya_ref[...] = (dm * ga).astype(dya_ref.dtype)
    dgt = jnp.concatenate([dm * ys_ref[...] * gs * (1.0 - gs), dm * ya_ref[...] * ga * (1.0 - ga)], axis=1)
    dgt_ref[...] = dgt.astype(dgt_ref.dtype)
    db_ref[...] += jnp.sum(dgt, axis=0, keepdims=True)


def _gate_bwd(dmix, proj, b_gate, y_ssd, y_attn, tr=512):
    t = y_ssd.shape[0]
    tr = min(tr, t)
    row = pl.BlockSpec((tr, D_MODEL), lambda i: (i, 0))
    gates = pl.BlockSpec((tr, 2 * D_MODEL), lambda i: (i, GT0 // (2 * D_MODEL)))
    vec2 = pl.BlockSpec((1, 2 * D_MODEL), lambda i: (0, 0))
    return pl.pallas_call(
        functools.partial(_gate_bwd_body), name="gate_bwd", grid=(t // tr,),
        in_specs=[row, gates, vec2, row, row],
        out_specs=[row, row, gates, vec2],
        out_shape=[jax.ShapeDtypeStruct((t, D_MODEL), BF16), jax.ShapeDtypeStruct((t, D_MODEL), BF16),
                   jax.ShapeDtypeStruct((t, NP), BF16), jax.ShapeDtypeStruct((1, 2 * D_MODEL), F32)],
        compiler_params=_params(1),
    )(dmix, proj, b_gate.reshape(1, 2 * D_MODEL), y_ssd, y_attn)


def _bucket_np():
    i = np.arange(BLOCK)[:, None]
    j = np.arange(3 * BLOCK)[None, :]
    rel = j - BLOCK - i
    nb = N_BUCKETS // 2
    max_exact = nb // 2
    ret = np.where(rel > 0, nb, 0)
    n = np.abs(rel)
    nf = np.maximum(n, 1).astype(np.float32)
    large = max_exact + (np.log(nf / np.float32(max_exact)) / np.float32(math.log(MAX_DISTANCE / max_exact))
                         * np.float32(nb - max_exact)).astype(np.int32)
    large = np.minimum(large, nb - 1)
    bucket = ret + np.where(n < max_exact, n, large)
    band = np.abs(rel) <= WINDOW
    return bucket, band


def _onehot_np():
    bucket, _ = _bucket_np()
    oh = np.zeros((BLOCK * 3 * BLOCK, LANE), np.float32)
    oh[np.arange(oh.shape[0]), bucket.reshape(-1)] = 1.0
    return oh


REP = Q_HEADS // KV_HEADS


def _group_cols(ref, g):
    return jnp.concatenate([ref[:, (g * REP + r) * HEAD_PAD:(g * REP + r + 1) * HEAD_PAD] for r in range(REP)], axis=0)


def _group_lane(ref, g, rows):
    return jnp.concatenate([jnp.broadcast_to(ref[:, g * REP + r:g * REP + r + 1], (rows, 1)) for r in range(REP)], axis=0)


def _attn_kv(refs, g):
    gc = slice(g * HEAD_PAD, (g + 1) * HEAD_PAD)
    return _mx(jnp.concatenate([r[:, gc] for r in refs], axis=0))


def _attn_logits(q_ref, kg, bias_ref, edge, g):
    q4 = _mx(_group_cols(q_ref, g))
    s = _dot(q4, kg, ((1,), (1,))) * (HEAD_DIM ** -0.5) + bias_ref[g] + edge
    return q4, s


def _attn_edge(n, nb):
    col = lax.broadcasted_iota(jnp.int32, (1, 3 * BLOCK), 1)
    bad = ((col < BLOCK) & (n == 0)) | ((col >= 2 * BLOCK) & (n == nb - 1))
    return jnp.where(bad, NEG, 0.0)


def _onehot_lane(h):
    return (lax.broadcasted_iota(jnp.int32, (1, LANE), 1) == h).astype(F32)


def _attn_fwd_body(*refs, nb, comm):
    ins, cin, (o_ref, lse_ref), cout, _, csem = _split_refs(refs, 9, 2, comm)
    q_ref, kp_ref, kc_ref, kn_ref, vp_ref, vc_ref, vn_ref, bias_ref, sink_ref = ins
    n = pl.program_id(0)
    if comm is not None:
        comm.run(cin, cout, csem, n, nb)
    edge = _attn_edge(n, nb)
    lse = jnp.zeros((BLOCK, LANE), F32)
    logits = [_attn_logits(q_ref, _attn_kv((kp_ref, kc_ref, kn_ref), g), bias_ref, edge, g)[1] for g in range(KV_HEADS)]
    for g in range(KV_HEADS):
        vg = _attn_kv((vp_ref, vc_ref, vn_ref), g)
        s = logits[g]
        sink = _group_lane(sink_ref, g, BLOCK)
        m = jnp.maximum(jnp.max(s, axis=-1, keepdims=True), sink)
        p = jnp.exp(s - m)
        den = jnp.sum(p, axis=-1, keepdims=True) + jnp.exp(sink - m)
        o4 = _dot(_mx(p * (1.0 / den)), vg)
        l4 = m + jnp.log(den)
        for r in range(REP):
            h = g * REP + r
            o_ref[:, h * HEAD_PAD:(h + 1) * HEAD_PAD] = o4[r * BLOCK:(r + 1) * BLOCK].astype(o_ref.dtype)
            lse = lse + l4[r * BLOCK:(r + 1) * BLOCK] * _onehot_lane(h)
    lse_ref[...] = lse


def _attn_specs(t, clamp):
    nb = t // BLOCK
    cur = (lambda n: jnp.minimum(n, nb - 1)) if clamp else (lambda n: n)
    prv = lambda n: jnp.maximum(cur(n) - 1, 0)
    nxt = lambda n: jnp.minimum(cur(n) + 1, nb - 1)
    kb, vb = K0 // KP, V0 // KP
    qs = pl.BlockSpec((BLOCK, QP), lambda n: (cur(n), Q0 // QP))
    ks = [pl.BlockSpec((BLOCK, KP), lambda n, f=f: (f(n), kb)) for f in (prv, cur, nxt)]
    vs = [pl.BlockSpec((BLOCK, KP), lambda n, f=f: (f(n), vb)) for f in (prv, cur, nxt)]
    consts = [pl.BlockSpec((KV_HEADS, REP * BLOCK, 3 * BLOCK), lambda n: (0, 0, 0)), pl.BlockSpec((1, LANE), lambda n: (0, 0))]
    return nb, cur, qs, ks, vs, consts


def _attn_fwd(proj, bias, sink, comm=None):
    t = proj.shape[0]
    nb, cur, qs, ks, vs, consts = _attn_specs(t, False)
    in_specs, args = [qs] + ks + vs + consts, [proj] * 7 + [bias, sink]
    out_specs = [pl.BlockSpec((BLOCK, QP), lambda n: (n, 0)), pl.BlockSpec((BLOCK, LANE), lambda n: (n, 0))]
    out_shape = [jax.ShapeDtypeStruct((t, QP), BF16), jax.ShapeDtypeStruct((t, LANE), F32)]
    scratch = []
    if comm is not None:
        in_specs, args = in_specs + [ANY] * comm.n, args + comm.arrs
        out_specs, out_shape = out_specs + [ANY] * comm.n, out_shape + comm.out_shapes()
        scratch = comm.sems()
    return pl.pallas_call(
        functools.partial(_attn_fwd_body, nb=nb, comm=comm), name="attn_fwd", grid=(nb,),
        in_specs=in_specs, out_specs=out_specs, out_shape=out_shape, scratch_shapes=scratch, compiler_params=_params(1),
    )(*args)


def _attn_bwd_body(*refs, nb, comm):
    ins, cin, (dqkv_ref, dbias_ref, dsink_ref), cout, (ak, bk, av, bv, dq_scr), csem = _split_refs(refs, 13, 3, comm)
    q_ref, kp_ref, kc_ref, kn_ref, vp_ref, vc_ref, vn_ref, bias_ref, sink_ref, do_ref, o_ref, lse_ref = ins[:12]
    dk_ref = dqkv_ref.at[:, K0:K0 + KP]
    dv_ref = dqkv_ref.at[:, V0:V0 + KP]
    n = pl.program_id(0)
    scale = HEAD_DIM ** -0.5
    if comm is not None:
        comm.run(cin, cout, csem, n, nb + 1)

    @pl.when(n == 0)
    def _():
        for r in (ak, bk, av, bv, dq_scr, dbias_ref, dsink_ref):
            r[...] = jnp.zeros_like(r)

    dqkv_ref[:, Q0:Q0 + QP] = dq_scr[...]

    @pl.when(n < nb)
    def _():
        edge = _attn_edge(n, nb)
        dsink = jnp.zeros((1, LANE), F32)
        early = []
        for g in range(KV_HEADS):
            kg = _attn_kv((kp_ref, kc_ref, kn_ref), g)
            q4, s = _attn_logits(q_ref, kg, bias_ref, edge, g)
            do = _group_cols(do_ref, g).astype(F32)
            early.append((kg, q4, s, do, _dot(_mx(do), _attn_kv((vp_ref, vc_ref, vn_ref), g), ((1,), (1,)))))
        for g in range(KV_HEADS):
            gc = slice(g * HEAD_PAD, (g + 1) * HEAD_PAD)
            kg, q4, s, do, dp = early[g]
            lse = _group_lane(lse_ref, g, BLOCK)
            p = jnp.exp(s - lse)
            delta = jnp.sum(do * _group_cols(o_ref, g).astype(F32), axis=-1, keepdims=True)
            ds = p * (dp - delta)
            sunk = jnp.exp(_group_lane(sink_ref, g, BLOCK) - lse) * delta
            dbias_ref[g] += ds
            dsb = _mx(ds)
            dq4 = _dot(dsb, kg) * scale
            for r in range(REP):
                h = g * REP + r
                dq_scr[:, h * HEAD_PAD:(h + 1) * HEAD_PAD] = dq4[r * BLOCK:(r + 1) * BLOCK].astype(dq_scr.dtype)
                dsink = dsink - jnp.sum(sunk[r * BLOCK:(r + 1) * BLOCK], axis=0, keepdims=True) * _onehot_lane(h)
            dk_c = _dot(dsb, q4, ((0,), (0,))) * scale
            dv_c = _dot(_mx(p), _mx(do), ((0,), (0,)))
            for acc_a, acc_b, out, c in ((ak, bk, dk_ref, dk_c), (av, bv, dv_ref, dv_c)):
                out[:, gc] = (acc_a[:, gc] + c[0:BLOCK]).astype(out.dtype)
                acc_a[:, gc] = acc_b[:, gc] + c[BLOCK:2 * BLOCK]
                acc_b[:, gc] = c[2 * BLOCK:3 * BLOCK]
        dsink_ref[...] += dsink

    @pl.when(n == nb)
    def _():
        dk_ref[...] = ak[...].astype(dk_ref.dtype)
        dv_ref[...] = av[...].astype(dv_ref.dtype)


def _attn_bwd(proj, bias, sink, do, o, lse, dproj, comm=None):
    t = proj.shape[0]
    nb, cur, qs, ks, vs, consts = _attn_specs(t, True)
    rowq = pl.BlockSpec((BLOCK, QP), lambda n: (cur(n), 0))
    late = pl.BlockSpec((BLOCK, QKV), lambda n: (jnp.maximum(n - 1, 0), Q0 // QKV))
    in_specs = [qs] + ks + vs + consts + [rowq, rowq, pl.BlockSpec((BLOCK, LANE), lambda n: (cur(n), 0)), ANY]
    args = [proj] * 7 + [bias, sink, do, o, lse, dproj]
    out_specs = [late, pl.BlockSpec((KV_HEADS, REP * BLOCK, 3 * BLOCK), lambda n: (0, 0, 0)), pl.BlockSpec((1, LANE), lambda n: (0, 0))]
    out_shape = [jax.ShapeDtypeStruct((t, NP), BF16), jax.ShapeDtypeStruct((KV_HEADS, REP * BLOCK, 3 * BLOCK), F32),
                 jax.ShapeDtypeStruct((1, LANE), F32)]
    scratch = [pltpu.VMEM((BLOCK, KP), F32) for _ in range(4)] + [pltpu.VMEM((BLOCK, QP), BF16)]
    if comm is not None:
        in_specs, args = in_specs + [ANY] * comm.n, args + comm.arrs
        out_specs, out_shape = out_specs + [ANY] * comm.n, out_shape + comm.out_shapes()
        scratch = scratch + comm.sems()
    return pl.pallas_call(
        functools.partial(_attn_bwd_body, nb=nb, comm=comm), name="attn_bwd", grid=(nb + 1,),
        in_specs=in_specs, out_specs=out_specs, out_shape=out_shape, scratch_shapes=scratch, input_output_aliases={12: 0},
        compiler_params=_params(1),
    )(*args)


def _adamw_body(gp_ref, w_ref, m_ref, v_ref, g_ref, d_ref, nm_ref, nv_ref, *, nparts):
    g = gp_ref[0].astype(F32)
    for j in range(1, nparts):
        g = g + gp_ref[j].astype(F32)
    m = ADAM_B1 * m_ref[...] + (1.0 - ADAM_B1) * g
    v = ADAM_B2 * v_ref[...] + (1.0 - ADAM_B2) * (g * g)
    m_hat = m / (1.0 - ADAM_B1 ** ADAM_STEP)
    v_hat = v / (1.0 - ADAM_B2 ** ADAM_STEP)
    g_ref[...] = g
    d_ref[...] = -ADAM_LR * (m_hat / (jnp.sqrt(v_hat) + ADAM_EPS) + ADAM_WD * w_ref[...])
    nm_ref[...] = m
    nv_ref[...] = v


def _adamw(gparts, w, m, v, name, tr=256):
    p, r, c = gparts.shape
    tr = min(tr, r)
    assert r % tr == 0
    row = pl.BlockSpec((tr, c), lambda i: (i, 0))
    sd = jax.ShapeDtypeStruct((r, c), F32)
    return pl.pallas_call(
        functools.partial(_adamw_body, nparts=p), name=name, grid=(r // tr,),
        in_specs=[pl.BlockSpec((p, tr, c), lambda i: (0, i, 0)), row, row, row],
        out_specs=[row, row, row, row], out_shape=[sd, sd, sd, sd], compiler_params=_params(1),
    )(gparts, w, m, v)


MESH = pl.DeviceIdType.MESH
N_REL = N_DEV - 1


def _gather_phases(ins, outs, sems):
    n = len(ins)
    send_sems, recv_sems, local_sems = sems
    x, y, c = lax.axis_index("x"), lax.axis_index("y"), lax.axis_index("c")
    me, sibling = (x, y, c), (x, y, 1 - c)
    chips = [(1 - x, y), (x, 1 - y), (1 - x, 1 - y)]

    def slot(a, p):
        return outs[a].at[4 * p[0] + 2 * p[1] + p[2]]

    def copy(a, k, block, to, src=None):
        return pltpu.make_async_remote_copy(
            src_ref=slot(a, block) if src is None else src, dst_ref=slot(a, block),
            send_sem=send_sems.at[a * N_REL + k], recv_sem=recv_sems.at[a * N_REL + k], device_id=to, device_id_type=MESH)

    mine = [pltpu.make_async_copy(ins[a], slot(a, me), local_sems.at[a]) for a in range(n)]
    first = []
    for a in range(n):
        first.append(copy(a, 0, me, sibling, src=ins[a]))
        first += [copy(a, 1 + j, me, (*chip, c), src=ins[a]) for j, chip in enumerate(chips)]
    passed = [copy(a, 4 + j, (*chip, c), sibling) for j, chip in enumerate(chips) for a in range(n)]

    def start():
        for cp in mine + first:
            cp.start()

    def pass_on():
        i = 0
        for j, chip in enumerate(chips):
            for a in range(n):
                copy(a, 1 + j, (*chip, c), me).wait_recv()
                passed[i].start()
                i += 1

    def finish():
        for a in range(n):
            copy(a, 0, sibling, me).wait_recv()
            for j, chip in enumerate(chips):
                copy(a, 4 + j, (*chip, 1 - c), me).wait_recv()
        for cp in first + passed:
            cp.wait_send()
        for cp in mine:
            cp.wait()

    return start, pass_on, finish


def _exchange_phases(ins, outs, sems):
    n = len(ins)
    send_sems, recv_sems, local_sems = sems
    x, y, c = lax.axis_index("x"), lax.axis_index("y"), lax.axis_index("c")
    me = 4 * x + 2 * y + c
    mine = [pltpu.make_async_copy(ins[a].at[me], outs[a].at[me], local_sems.at[a]) for a in range(n)]
    copies = []
    for a in range(n):
        for k in range(1, N_DEV):
            px = 1 - x if k & 4 else x
            py = 1 - y if k & 2 else y
            pc = 1 - c if k & 1 else c
            peer = 4 * px + 2 * py + pc
            send = pltpu.make_async_remote_copy(
                src_ref=ins[a].at[peer], dst_ref=outs[a].at[me], send_sem=send_sems.at[a * N_REL + k - 1],
                recv_sem=recv_sems.at[a * N_REL + k - 1], device_id=(px, py, pc), device_id_type=MESH)
            recv = pltpu.make_async_remote_copy(
                src_ref=ins[a].at[me], dst_ref=outs[a].at[peer], send_sem=send_sems.at[a * N_REL + k - 1],
                recv_sem=recv_sems.at[a * N_REL + k - 1], device_id=(px, py, pc), device_id_type=MESH)
            copies.append((send, recv))

    def start():
        for cp in mine:
            cp.start()
        for send, _ in copies:
            send.start()

    def finish():
        for send, recv in copies:
            send.wait_send()
            recv.wait_recv()
        for cp in mine:
            cp.wait()

    return start, None, finish


class _Comm:
    def __init__(self, kind, arrs):
        self.kind, self.arrs, self.n = kind, list(arrs), len(arrs)

    def out_shapes(self):
        if self.kind == "gather":
            return [jax.ShapeDtypeStruct((N_DEV,) + a.shape, a.dtype) for a in self.arrs]
        return [jax.ShapeDtypeStruct(a.shape, a.dtype) for a in self.arrs]

    def sems(self):
        return [pltpu.SemaphoreType.DMA((self.n * N_REL,)), pltpu.SemaphoreType.DMA((self.n * N_REL,)),
                pltpu.SemaphoreType.DMA((self.n,))]

    def run(self, ins, outs, sems, step, nsteps):
        start, mid, finish = (_gather_phases if self.kind == "gather" else _exchange_phases)(ins, outs, sems)
        if nsteps is None:
            start()
            if mid is not None:
                mid()
            finish()
            return
        pl.when(step == 0)(start)
        if mid is not None:
            pl.when(step == nsteps // 2)(mid)
        pl.when(step == nsteps - 1)(finish)


def _split_refs(refs, n_in, n_out, comm):
    k = comm.n if comm is not None else 0
    i = 0
    parts = []
    for cnt in (n_in, k, n_out, k):
        parts.append(refs[i:i + cnt])
        i += cnt
    rest = refs[i:]
    if comm is None:
        return parts[0], (), parts[2], (), rest, ()
    return parts[0], parts[1], parts[2], parts[3], rest[:len(rest) - 3], rest[len(rest) - 3:]


def _comm_body(*refs, comm):
    n = comm.n
    comm.run(refs[:n], refs[n:2 * n], refs[2 * n:], None, None)


def _communicate(comm, name, vmem=False):
    spec = pl.BlockSpec(memory_space=pltpu.VMEM if vmem else pl.ANY)
    return pl.pallas_call(
        functools.partial(_comm_body, comm=comm), name=name,
        in_specs=[spec] * comm.n, out_specs=[spec] * comm.n, out_shape=comm.out_shapes(), scratch_shapes=comm.sems(),
        compiler_params=pltpu.CompilerParams(vmem_limit_bytes=VMEM_LIMIT),
    )(*comm.arrs)


ANY = pl.BlockSpec(memory_space=pl.ANY)


def _pad_heads(w, axis):
    shp = w.shape
    heads = shp[axis] // HEAD_DIM
    w = w.reshape(shp[:axis] + (heads, HEAD_DIM) + shp[axis + 1:])
    pad = [(0, 0)] * w.ndim
    pad[axis + 1] = (0, HEAD_PAD - HEAD_DIM)
    w = jnp.pad(w, pad)
    return w.reshape(shp[:axis] + (heads * HEAD_PAD,) + shp[axis + 1:])


def _unpad_heads(w, axis):
    shp = w.shape
    heads = shp[axis] // HEAD_PAD
    w = w.reshape(shp[:axis] + (heads, HEAD_PAD) + shp[axis + 1:])
    w = lax.slice_in_dim(w, 0, HEAD_DIM, axis=axis + 1)
    return w.reshape(shp[:axis] + (heads * HEAD_DIM,) + shp[axis + 1:])


def _w_in_to_padded(w):
    idx = np.cumsum((0,) + IN_SPLITS)
    z, xbc, dt, q, k, v, gates = [w[:, idx[i]:idx[i + 1]] for i in range(7)]
    zeros = lambda n: jnp.zeros((w.shape[0], n), w.dtype)
    return jnp.concatenate([_pad_heads(q, 1), _pad_heads(k, 1), _pad_heads(v, 1), xbc, dt, zeros(DTW - 2 * SSD_HEADS), z, gates],
                           axis=1)


def _w_in_from_padded(w):
    z = w[:, Z0:Z0 + D_INNER]
    gates = w[:, GT0:GT0 + 2 * D_MODEL]
    q = _unpad_heads(w[:, Q0:Q0 + QP], 1)
    xbc = w[:, XBC0:XBC0 + CONV_DIM]
    k = _unpad_heads(w[:, K0:K0 + KP], 1)
    v = _unpad_heads(w[:, V0:V0 + KP], 1)
    dt = w[:, DT0:DT0 + 2 * SSD_HEADS]
    return jnp.concatenate([z, xbc, dt, q, k, v, gates], axis=1)


def _pad_lane(v):
    v = v.reshape(-1)
    return jnp.pad(v, (0, LANE - v.shape[0]))


def _layer_consts(p):
    c = dict(p)
    c["conv_w8"] = jnp.pad(p["conv_w"].reshape(SSD_CONV, CONV_DIM), ((0, 8 - SSD_CONV), (0, 0)))
    c["conv_b1"] = p["conv_b"].reshape(1, CONV_DIM)
    c["dtb"] = _pad_lane(p["dt_bias"])
    c["alog"] = _pad_lane(p["a_log"])
    c["dskip_x"] = jnp.repeat(p["d_skip"], SSD_HEAD_DIM).reshape(1, D_INNER)
    c["sink"] = _pad_lane(p["attn_sink"]).reshape(1, LANE)
    return c


def _hosted(res, n_own, hook):
    res = list(res)
    if hook is not None:
        hook[1](res[n_own:])
    return res[0] if n_own == 1 else res[:n_own]


def _layer_fwd(x, c, w, bias, l, hooks):
    comm = lambda k: hooks[k][0] if k in hooks else None
    s = {"x": x}
    s["h"] = _rms_fwd(x, c["pre_mix_norm"], f"pre_mix_norm_{l}", out_dtype=BF16)
    res = _mm(s["h"], w["w_in"], f"in_proj_{l}", out_dtype=BF16, mode="dt_side", comm=comm("in_proj"))
    proj, s["dt"] = _hosted(res, 2, hooks.get("in_proj"))
    s["proj"] = proj
    s["xbc"] = _conv_fwd(proj, c["conv_w8"], c["conv_b1"])
    s["dtt"] = _dt_rows(s["dt"])
    s["yf"], s["stf"] = _hosted(_ssd_fwd(s["xbc"], s["dt"], s["dtt"], c["dtb"], c["alog"], False, comm=comm("ssd_fwd")), 2,
                                hooks.get("ssd_fwd"))
    s["yb"], s["stb"] = _hosted(_ssd_fwd(s["xbc"], s["dt"], s["dtt"], c["dtb"], c["alog"], True, comm=comm("ssd_fwd_rev")), 2,
                                hooks.get("ssd_fwd_rev"))
    s["u"] = _gnorm_fwd(s["yf"], s["yb"], s["xbc"], proj, c["dskip_x"], c["ssd_norm"])
    s["y_ssd"] = _mm(s["u"], w["w_ssd_out"], f"ssd_out_{l}")
    s["o"], s["lse"] = _hosted(_attn_fwd(proj, bias, c["sink"], comm=comm("attn_fwd")), 2, hooks.get("attn_fwd"))
    s["y_attn"] = _mm(s["o"], w["w_attn_out"], f"attn_out_{l}")
    s["mix"] = _gate_fwd(proj, c["b_gate"], s["y_ssd"], s["y_attn"])
    s["mixed"] = _mm(s["mix"], w["w_o"], f"w_o_{l}")
    s["x1"] = _rms_fwd(s["mixed"], c["post_mix_norm"], f"post_mix_norm_{l}", res=x)
    s["h2"] = _rms_fwd(s["x1"], c["pre_mlp_norm"], f"pre_mlp_norm_{l}", out_dtype=BF16)
    s["f1"], s["a"] = _mm(s["h2"], w["w_mlp_in"], f"mlp_in_{l}", mode="relu2")
    s["f"] = _mm(s["a"], w["w_mlp_out"], f"mlp_out_{l}")
    x2 = _rms_fwd(s["f"], c["post_mlp_norm"], f"post_mlp_norm_{l}", res=s["x1"])
    return x2, s


def _send_rest(gw):
    s_mi = gw["w_mlp_in"].reshape(D_MODEL, N_DEV, -1).transpose(1, 0, 2).astype(BF16)
    parts = [(_unpad_heads(gw[n], 0) if n == "w_attn_out" else gw[n]).reshape(N_DEV, -1, D_MODEL) for n in ROWS]
    return [s_mi, jnp.concatenate(parts, axis=1).astype(BF16)]


def _send_w_in(g):
    return [_w_in_from_padded(g).reshape(D_MODEL, N_DEV, -1).transpose(1, 0, 2).astype(BF16)]


def _layer_bwd(dx2, s, c, w, bias, l, earlier=None, host_own=False):
    gw, gs, got = {}, {}, {}
    df, gs["post_mlp_norm"] = _rms_bwd(s["f"], c["post_mlp_norm"], dx2, f"post_mlp_norm_bwd_{l}", out_dtype=BF16)
    df1 = _mm(df, w["w_mlp_out"], f"mlp_out_dx_{l}", tb=True, mode="relu2_bwd", extra=s["f1"], out_dtype=BF16)
    gw["w_mlp_out"] = _mm(s["a"], df, f"mlp_out_dw_{l}", ta=True)
    gw["w_mlp_in"] = _mm(s["h2"], df1, f"mlp_in_dw_{l}", ta=True)
    dh2 = _mm(df1, w["w_mlp_in"], f"mlp_in_dx_{l}", tb=True, out_dtype=BF16)
    dx1, gs["pre_mlp_norm"] = _rms_bwd(s["x1"], c["pre_mlp_norm"], dh2, f"pre_mlp_norm_bwd_{l}", add=dx2)
    dmixed, gs["post_mix_norm"] = _rms_bwd(s["mixed"], c["post_mix_norm"], dx1, f"post_mix_norm_bwd_{l}", out_dtype=BF16)
    gw["w_o"] = _mm(s["mix"], dmixed, f"w_o_dw_{l}", ta=True)
    dmix = _mm(dmixed, w["w_o"], f"w_o_dx_{l}", tb=True, out_dtype=BF16)
    dys, dya, dproj, gs["b_gate"] = _gate_bwd(dmix, s["proj"], c["b_gate"], s["y_ssd"], s["y_attn"])
    gw["w_ssd_out"] = _mm(s["u"], dys, f"ssd_out_dw_{l}", ta=True)
    du = _mm(dys, w["w_ssd_out"], f"ssd_out_dx_{l}", tb=True, out_dtype=BF16)
    gw["w_attn_out"] = _mm(s["o"], dya, f"attn_out_dw_{l}", ta=True)
    do = _mm(dya, w["w_attn_out"], f"attn_out_dx_{l}", tb=True, out_dtype=BF16)
    res = _attn_bwd(s["proj"], bias, c["sink"], do, s["o"], s["lse"], dproj,
                    comm=None if earlier is None else _Comm("exchange", earlier[:1]))
    (dproj, dbias, dsink), got["earlier"] = res[:3], list(res[3:])
    gs["attn_sink"] = dsink
    dy, dproj, gs["ssd_norm"], gs["d_skip"] = _gnorm_bwd(du, s["yf"], s["yb"], s["xbc"], s["proj"], c["dskip_x"], c["ssd_norm"],
                                                         dproj)
    res = _ssd_bwd(s["xbc"], s["dt"], s["dtt"], c["dtb"], c["alog"], dy, s["stf"], c["dskip_x"], False,
                   comm=None if earlier is None else _Comm("exchange", earlier[1:]))
    part, got["earlier"] = res[:3], got["earlier"] + list(res[3:])
    res = _ssd_bwd(s["xbc"], s["dt"], s["dtt"], c["dtb"], c["alog"], dy, s["stb"], c["dskip_x"], True, prev=part, dproj=dproj,
                   comm=_Comm("exchange", _send_rest(gw)) if host_own else None)
    (dxbc, dproj, sm), got["rest"] = res[:3], res[3:]
    gs["a_log"], gs["dt_bias"] = sm[0:1], sm[1:2]
    dproj, gs["conv"] = _conv_bwd(s["proj"], dxbc, c["conv_w8"], c["conv_b1"], dproj)
    gw["w_in"] = _mm(s["h"], dproj, f"in_proj_dw_{l}", ta=True)
    if host_own:
        dh, *got["w_in"] = _mm(dproj, w["w_in"], f"in_proj_dx_{l}", tb=True, out_dtype=BF16, comm=_Comm("exchange", _send_w_in(gw["w_in"])))
    else:
        dh = _mm(dproj, w["w_in"], f"in_proj_dx_{l}", tb=True, out_dtype=BF16)
    dx, gs["pre_mix_norm"] = _rms_bwd(s["x"], c["pre_mix_norm"], dh, f"pre_mix_norm_bwd_{l}", add=dx1)
    return dx, gw, gs, dbias, got


SMALL = (("pre_mix_norm", DEPTH * D_MODEL), ("b_gate", DEPTH * 2 * D_MODEL), ("conv_w", DEPTH * SSD_CONV * CONV_DIM),
         ("conv_b", DEPTH * CONV_DIM), ("dt_bias", DEPTH * 2 * SSD_HEADS), ("a_log", DEPTH * 2 * SSD_HEADS),
         ("d_skip", DEPTH * SSD_HEADS), ("ssd_norm", DEPTH * D_INNER), ("attn_sink", DEPTH * Q_HEADS),
         ("rel_bias_table", N_BUCKETS * Q_HEADS), ("post_mix_norm", DEPTH * D_MODEL), ("pre_mlp_norm", DEPTH * D_MODEL),
         ("post_mlp_norm", DEPTH * D_MODEL))


def _small_rows(n):
    return -(-n // LANE)


def _pack_small(vals, fill=0.0):
    rows = []
    for name, n in SMALL:
        v = vals[name].reshape(-1).astype(F32)
        rows.append(jnp.pad(v, (0, _small_rows(n) * LANE - n), constant_values=fill).reshape(-1, LANE))
    out = jnp.concatenate(rows, axis=0)
    return jnp.pad(out, ((0, -out.shape[0] % 8), (0, 0)), constant_values=fill)


def _unpack_small(packed, shapes):
    out, r = {}, 0
    for name, n in SMALL:
        nr = _small_rows(n)
        out[name] = packed[r:r + nr].reshape(-1)[:n].reshape(shapes[name])
        r += nr
    return out


BIG = ("w_in", "w_ssd_out", "w_attn_out", "w_o", "w_mlp_in", "w_mlp_out")
ROWS = ("w_ssd_out", "w_attn_out", "w_o", "w_mlp_out")


def _pack_rows(vals, lead):
    return jnp.concatenate([vals[n].reshape(lead + (-1, D_MODEL)) for n in ROWS], axis=len(lead))


def kernel(x, pre_mix_norm, w_in, b_gate, conv_w, conv_b, dt_bias, a_log, d_skip, ssd_norm, w_ssd_out, attn_sink, rel_bias_table, w_attn_out, w_o, post_mix_norm, pre_mlp_norm, w_mlp_in, w_mlp_out, post_mlp_norm, loss_target, m_pre_mix_norm, m_w_in, m_b_gate, m_conv_w, m_conv_b, m_dt_bias, m_a_log, m_d_skip, m_ssd_norm, m_w_ssd_out, m_attn_sink, m_rel_bias_table, m_w_attn_out, m_w_o, m_post_mix_norm, m_pre_mlp_norm, m_w_mlp_in, m_w_mlp_out, m_post_mlp_norm, v_pre_mix_norm, v_w_in, v_b_gate, v_conv_w, v_conv_b, v_dt_bias, v_a_log, v_d_skip, v_ssd_norm, v_w_ssd_out, v_attn_sink, v_rel_bias_table, v_w_attn_out, v_w_o, v_post_mix_norm, v_pre_mlp_norm, v_w_mlp_in, v_w_mlp_out, v_post_mlp_norm):
    names = ("pre_mix_norm", "w_in", "b_gate", "conv_w", "conv_b", "dt_bias", "a_log", "d_skip", "ssd_norm", "w_ssd_out",
             "attn_sink", "rel_bias_table", "w_attn_out", "w_o", "post_mix_norm", "pre_mlp_norm", "w_mlp_in", "w_mlp_out",
             "post_mlp_norm")
    W = dict(zip(names, (pre_mix_norm, w_in, b_gate, conv_w, conv_b, dt_bias, a_log, d_skip, ssd_norm, w_ssd_out, attn_sink,
                         rel_bias_table, w_attn_out, w_o, post_mix_norm, pre_mlp_norm, w_mlp_in, w_mlp_out, post_mlp_norm)))
    M = dict(zip(names, (m_pre_mix_norm, m_w_in, m_b_gate, m_conv_w, m_conv_b, m_dt_bias, m_a_log, m_d_skip, m_ssd_norm,
                         m_w_ssd_out, m_attn_sink, m_rel_bias_table, m_w_attn_out, m_w_o, m_post_mix_norm, m_pre_mlp_norm,
                         m_w_mlp_in, m_w_mlp_out, m_post_mlp_norm)))
    V = dict(zip(names, (v_pre_mix_norm, v_w_in, v_b_gate, v_conv_w, v_conv_b, v_dt_bias, v_a_log, v_d_skip, v_ssd_norm,
                         v_w_ssd_out, v_attn_sink, v_rel_bias_table, v_w_attn_out, v_w_o, v_post_mix_norm, v_pre_mlp_norm,
                         v_w_mlp_in, v_w_mlp_out, v_post_mlp_norm)))
    t = x.shape[1]
    shard = {n: W[n].shape for n in names}
    dev = 4 * lax.axis_index("x") + 2 * lax.axis_index("y") + lax.axis_index("c")
    cshard = CONV_DIM // N_DEV

    sh_in = [w_in[l].astype(BF16) for l in range(DEPTH)]
    sh_rest = [[w_mlp_in[l].astype(BF16), _pack_rows({n: W[n][l].astype(BF16) for n in ROWS}, ())] for l in range(DEPTH)]
    full = [{} for _ in range(DEPTH)]

    def take_w_in(l):
        def take(got):
            full[l]["w_in"] = _w_in_to_padded(got[0].transpose(1, 0, 2).reshape(D_MODEL, N_IN))
        return take

    def take_mlp_in(l):
        def take(got):
            full[l]["w_mlp_in"] = got[0].transpose(1, 0, 2).reshape(D_MODEL, D_FF)
        return take

    def take_rows(l):
        def take(got):
            r = 0
            for n in ROWS:
                per = shard[n][1]
                full[l][n] = got[0][:, r:r + per].reshape(N_DEV * per, D_MODEL)
                r += per
            full[l]["w_attn_out"] = _pad_heads(full[l]["w_attn_out"], 0)
        return take

    def take_rest(l):
        def take(got):
            take_mlp_in(l)(got[:1])
            take_rows(l)(got[1:])
        return take

    take_w_in(0)(_communicate(_Comm("gather", [sh_in[0]]), "gather_w_in_0"))
    (g_conv,) = _communicate(_Comm("gather", [conv_w.reshape(-1, LANE)]), "gather_conv_w", vmem=True)
    conv_full = g_conv.reshape(N_DEV, DEPTH, SSD_CONV, cshard).transpose(1, 2, 0, 3).reshape(DEPTH, SSD_CONV, CONV_DIM)
    hooks = [{"in_proj": (_Comm("gather", sh_rest[0]), take_rest(0)),
              "ssd_fwd": (_Comm("gather", sh_rest[1][1:]), take_rows(1)),
              "ssd_fwd_rev": (_Comm("gather", sh_rest[1][:1]), take_mlp_in(1)),
              "attn_fwd": (_Comm("gather", [sh_in[1]]), take_w_in(1))}, {}]

    _, band = _bucket_np()
    onehot = jnp.asarray(_onehot_np(), BF16)
    table_t = jnp.pad(rel_bias_table.T, ((0, 0), (0, LANE - N_BUCKETS)))
    bias = _mm(table_t, onehot, "t5_bias", tb=True, tm=Q_HEADS, tn=3 * BLOCK * BLOCK // 8, split_a=True)
    bias = (bias.reshape(Q_HEADS, BLOCK, 3 * BLOCK) + jnp.asarray(np.where(band, 0.0, NEG), F32)).reshape(
        KV_HEADS, REP * BLOCK, 3 * BLOCK)

    consts = []
    for l in range(DEPTH):
        p = {n: W[n][l] for n in names if n not in BIG and n not in ("rel_bias_table", "conv_w")}
        p["conv_w"] = conv_full[l]
        consts.append(_layer_consts(p))
    h = x[0]
    saved = []
    for l in range(DEPTH):
        h, s = _layer_fwd(h, consts[l], full[l], bias, l, hooks[l])
        saved.append(s)
    dy, lsum = _loss(h, loss_target[0])
    loss = lax.psum(0.5 / D_MODEL * jnp.sum(lsum), ("x", "y", "c"))

    gss, dbs = [None] * DEPTH, [None] * DEPTH
    dy, gw1, gss[1], dbs[1], _ = _layer_bwd(dy, saved[1], consts[1], full[1], bias, 1)
    dy, _, gss[0], dbs[0], got = _layer_bwd(dy, saved[0], consts[0], full[0], bias, 0,
                                            earlier=_send_w_in(gw1["w_in"]) + _send_rest(gw1), host_own=True)
    recv = [list(got["w_in"]) + list(got["rest"]), list(got["earlier"])]
    grad_x = dy[None]
    dbias = jnp.concatenate([d.reshape(Q_HEADS, -1) for d in dbs], axis=1)
    d_table = _mm(dbias, jnp.concatenate([onehot] * DEPTH, axis=0), "t5_bias_bwd", tm=Q_HEADS, tk=3 * BLOCK * BLOCK // 8,
                  split_a=True)

    sg = {}
    for n in ("pre_mix_norm", "b_gate", "ssd_norm", "post_mix_norm", "pre_mlp_norm", "post_mlp_norm"):
        sg[n] = jnp.stack([gss[l][n].reshape(-1) for l in range(DEPTH)])
    sg["conv_w"] = jnp.stack([gss[l]["conv"][0:SSD_CONV] for l in range(DEPTH)])
    sg["conv_b"] = jnp.stack([gss[l]["conv"][SSD_CONV] for l in range(DEPTH)])
    sg["dt_bias"] = jnp.stack([gss[l]["dt_bias"][0, 0:2 * SSD_HEADS] for l in range(DEPTH)])
    sg["a_log"] = jnp.stack([gss[l]["a_log"][0, 0:2 * SSD_HEADS] for l in range(DEPTH)])
    sg["d_skip"] = jnp.stack([gss[l]["d_skip"][0, 0:SSD_HEADS] for l in range(DEPTH)])
    sg["attn_sink"] = jnp.stack([gss[l]["attn_sink"][0, 0:Q_HEADS] for l in range(DEPTH)])
    sg["rel_bias_table"] = d_table[:, 0:N_BUCKETS].T
    (small_parts,) = _communicate(_Comm("gather", [_pack_small(sg)]), "gather_small_grads", vmem=True)
    sshape = {n: W[n].shape for n, _ in SMALL}
    sshape["conv_w"] = (DEPTH, SSD_CONV, 1, CONV_DIM)
    pk = lambda d, fill: _pack_small({n: (jnp.full((DEPTH, SSD_CONV, CONV_DIM), fill, F32) if n == "conv_w" else d[n])
                                      for n, _ in SMALL}, fill)
    s_out = [_unpack_small(o, sshape) for o in _adamw(small_parts, pk(W, 1.0), pk(M, 1.0), pk(V, 1.0), "adamw_small", tr=1024)]
    conv_g = lax.dynamic_slice_in_dim(s_out[0]["conv_w"], dev * cshard, cshard, axis=3)
    c_out = _adamw(conv_g.reshape(1, DEPTH * SSD_CONV, cshard), conv_w.reshape(-1, cshard), m_conv_w.reshape(-1, cshard),
                   v_conv_w.reshape(-1, cshard), "adamw_conv_w", tr=DEPTH * SSD_CONV)
    for i in range(4):
        s_out[i]["conv_w"] = c_out[i].reshape(conv_w.shape)

    layer_out = []
    for l in range(DEPTH):
        r_in, r_mi, r_rows = recv[l]
        lo = {"w_in": _adamw(r_in, w_in[l], m_w_in[l], v_w_in[l], f"adamw_w_in_{l}"),
              "w_mlp_in": _adamw(r_mi, w_mlp_in[l], m_w_mlp_in[l], v_w_mlp_in[l], f"adamw_w_mlp_in_{l}")}
        o = _adamw(r_rows, *[_pack_rows({n: d[n][l] for n in ROWS}, ()) for d in (W, M, V)], f"adamw_rows_{l}")
        r = 0
        for n in ROWS:
            per = shard[n][1]
            lo[n] = [a[r:r + per] for a in o]
            r += per
        layer_out.append(lo)
    b_out = {n: [jnp.stack([layer_out[l][n][i] for l in range(DEPTH)]) for i in range(4)] for n in BIG}

    outs = [loss, grad_x]
    for i in range(4):
        for n in names:
            outs.append(b_out[n][i] if n in BIG else s_out[i][n])
    return tuple(outs)
```

```python
import functools
import math

import numpy as np
import jax
import jax.numpy as jnp
from jax import lax
from jax.experimental import pallas as pl
from jax.experimental.pallas import tpu as pltpu

F32 = jnp.float32
BF16 = jnp.bfloat16

D_MODEL = 1024
DEPTH = 2
D_INNER = 2048
SSD_HEADS = 32
SSD_HEAD_DIM = 64
SSD_GROUPS = 8
SSD_REP = 4
SSD_STATE = 128
SSD_CONV = 5
CHUNK = 128
CONV_DIM = 4096
Q_HEADS = 16
KV_HEADS = 4
HEAD_DIM = 64
ATTN_WIDTH = 1024
KV_WIDTH = 256
WINDOW = 128
BLOCK = 128
N_BUCKETS = 32
MAX_DISTANCE = 128
D_FF = 4096
EPS = 1e-6
IN_SPLITS = (D_INNER, CONV_DIM, 2 * SSD_HEADS, ATTN_WIDTH, KV_WIDTH, KV_WIDTH, 2 * D_MODEL)
N_IN = sum(IN_SPLITS)
N_DEV = 8

ADAM_LR = 0.001
ADAM_B1 = 0.9
ADAM_B2 = 0.999
ADAM_EPS = 1e-08
ADAM_WD = 0.01
ADAM_STEP = 10

LANE = 128
HEAD_PAD = LANE
QP = Q_HEADS * HEAD_PAD
KP = KV_HEADS * HEAD_PAD
QKV = QP + 2 * KP
Q0 = 0
K0 = QP
V0 = QP + KP
XBC0 = QKV
DT0 = XBC0 + CONV_DIM
DTW = 1024
Z0 = DT0 + DTW
GT0 = Z0 + D_INNER
NP = GT0 + 2 * D_MODEL
NEG = -1e30
VMEM_LIMIT = 56 * 1024 * 1024


def _params(n_grid):
    return pltpu.CompilerParams(dimension_semantics=("arbitrary",) * n_grid, vmem_limit_bytes=VMEM_LIMIT)


def _dot(a, b, dims=((1,), (0,))):
    return lax.dot_general(a, b, (dims, ((), ())), preferred_element_type=F32)


def _mx(a):
    return a.astype(BF16)


def _split_dot(x, e):
    hi = x.astype(BF16)
    lo = (x - hi.astype(F32)).astype(BF16)
    return _dot(hi, e) + _dot(lo, e)


def _softplus(x):
    u = jnp.exp(-jnp.abs(x))
    w = 1.0 + u
    log1p = jnp.where(w == 1.0, u, jnp.log(w) * u / jnp.where(w == 1.0, 1.0, w - 1.0))
    return jnp.maximum(x, 0.0) + log1p


def _sigmoid(x):
    return 1.0 / (1.0 + jnp.exp(-x))


def _mm_body(*refs, grid, ta, tb, split_a, mode, comm):
    n_in = 3 if mode == "relu2_bwd" else 2
    n_out = 2 if mode in ("relu2", "dt_side") else 1
    ins, cin, outs, cout, (acc_ref,), csem = _split_refs(refs, n_in, n_out, comm)
    a_ref, b_ref = ins[0], ins[1]
    i, j, k = pl.program_id(0), pl.program_id(1), pl.program_id(2)
    nk = grid[2]
    if comm is not None:
        comm.run(cin, cout, csem, (i * grid[1] + j) * nk + k, grid[0] * grid[1] * nk)

    dims = ((0 if ta else 1,), (1 if tb else 0,))
    b = _mx(b_ref[...])
    if split_a:
        a32 = a_ref[...].astype(F32)
        hi = a32.astype(BF16)
        lo = (a32 - hi.astype(F32)).astype(BF16)
        part = _dot(hi, b, dims) + _dot(lo, b, dims)
    else:
        part = _dot(_mx(a_ref[...]), b, dims)

    if nk > 1:
        @pl.when(k == 0)
        def _():
            acc_ref[...] = jnp.zeros_like(acc_ref)

        acc_ref[...] += part

    @pl.when(k == nk - 1)
    def _():
        acc = part if nk == 1 else acc_ref[...]
        if mode == "relu2":
            outs[0][...] = acc.astype(outs[0].dtype)
            r = jnp.maximum(acc, 0.0)
            outs[1][...] = (r * r).astype(outs[1].dtype)
        elif mode == "relu2_bwd":
            outs[0][...] = (acc * 2.0 * jnp.maximum(ins[2][...].astype(F32), 0.0)).astype(outs[0].dtype)
        else:
            outs[0][...] = acc.astype(outs[0].dtype)
            if mode == "dt_side":
                @pl.when(j == DT0 // outs[0].shape[1])
                def _():
                    off = DT0 % outs[0].shape[1]
                    outs[1][...] = acc[:, off:off + LANE]


def _tile(n, pref):
    t = min(n, pref)
    while n % t:
        t -= LANE
    assert t > 0 and n % t == 0, (n, pref)
    return t


def _mm(a, b, name, ta=False, tb=False, out_dtype=F32, tm=1024, tn=2048, tk=1024, split_a=False, mode=None, extra=None,
        comm=None):
    m, k = (a.shape[1], a.shape[0]) if ta else a.shape
    n = b.shape[0] if tb else b.shape[1]
    assert (b.shape[1] if tb else b.shape[0]) == k
    tm, tn, tk = _tile(m, tm), _tile(n, tn), _tile(k, tk)
    grid = (m // tm, n // tn, k // tk)
    a_spec = pl.BlockSpec((tk, tm), lambda i, j, kk: (kk, i)) if ta else pl.BlockSpec((tm, tk), lambda i, j, kk: (i, kk))
    b_spec = pl.BlockSpec((tn, tk), lambda i, j, kk: (j, kk)) if tb else pl.BlockSpec((tk, tn), lambda i, j, kk: (kk, j))
    o_spec = pl.BlockSpec((tm, tn), lambda i, j, kk: (i, j))
    in_specs, args = [a_spec, b_spec], [a, b]
    out_specs, out_shape = [o_spec], [jax.ShapeDtypeStruct((m, n), out_dtype)]
    if mode == "relu2":
        out_specs, out_shape = [o_spec, o_spec], [jax.ShapeDtypeStruct((m, n), BF16), jax.ShapeDtypeStruct((m, n), BF16)]
    elif mode == "dt_side":
        assert DT0 % tn + LANE <= tn
        out_specs = [o_spec, pl.BlockSpec((tm, LANE), lambda i, j, kk: (i, 0))]
        out_shape = out_shape + [jax.ShapeDtypeStruct((m, LANE), F32)]
    elif mode == "relu2_bwd":
        in_specs, args = in_specs + [o_spec], args + [extra]
    scratch = [pltpu.VMEM((tm, tn) if grid[2] > 1 else (8, LANE), F32)]
    if comm is not None:
        in_specs, args = in_specs + [ANY] * comm.n, args + comm.arrs
        out_specs, out_shape = out_specs + [ANY] * comm.n, out_shape + comm.out_shapes()
        scratch = scratch + comm.sems()
    res = pl.pallas_call(
        functools.partial(_mm_body, grid=grid, ta=ta, tb=tb, split_a=split_a, mode=mode, comm=comm),
        name=name, grid=grid, in_specs=in_specs, out_specs=out_specs, out_shape=out_shape, scratch_shapes=scratch,
        compiler_params=_params(3),
    )(*args)
    return res[0] if len(res) == 1 else res


def _rms_fwd_body(*refs, has_res):
    if has_res:
        x_ref, g_ref, r_ref, o_ref = refs
    else:
        x_ref, g_ref, o_ref = refs
    x = x_ref[...]
    y = x * lax.rsqrt(jnp.mean(x * x, axis=-1, keepdims=True) + EPS) * g_ref[...]
    if has_res:
        y = r_ref[...] + y
    o_ref[...] = y.astype(o_ref.dtype)


def _rms_fwd(x, g, name, res=None, out_dtype=F32, tr=512):
    t, d = x.shape
    tr = min(tr, t)
    row = pl.BlockSpec((tr, d), lambda i: (i, 0))
    vec = pl.BlockSpec((1, d), lambda i: (0, 0))
    args = [x, g.reshape(1, d)] + ([res] if res is not None else [])
    return pl.pallas_call(
        functools.partial(_rms_fwd_body, has_res=res is not None),
        name=name,
        grid=(t // tr,),
        in_specs=[row, vec] + ([row] if res is not None else []),
        out_specs=row,
        out_shape=jax.ShapeDtypeStruct((t, d), out_dtype),
        compiler_params=_params(1),
    )(*args)


def _rms_bwd_body(*refs, has_add):
    if has_add:
        x_ref, g_ref, dy_ref, add_ref, dx_ref, dg_ref = refs
    else:
        x_ref, g_ref, dy_ref, dx_ref, dg_ref = refs

    @pl.when(pl.program_id(0) == 0)
    def _():
        dg_ref[...] = jnp.zeros_like(dg_ref)

    x = x_ref[...]
    dy = dy_ref[...].astype(F32)
    rstd = lax.rsqrt(jnp.mean(x * x, axis=-1, keepdims=True) + EPS)
    n = x * rstd
    dn = dy * g_ref[...]
    dx = rstd * (dn - n * jnp.mean(dn * n, axis=-1, keepdims=True))
    if has_add:
        dx = dx + add_ref[...]
    dx_ref[...] = dx.astype(dx_ref.dtype)
    dg_ref[...] += jnp.sum(dy * n, axis=0, keepdims=True)


def _rms_bwd(x, g, dy, name, add=None, out_dtype=F32, tr=512):
    t, d = x.shape
    tr = min(tr, t)
    row = pl.BlockSpec((tr, d), lambda i: (i, 0))
    vec = pl.BlockSpec((1, d), lambda i: (0, 0))
    args = [x, g.reshape(1, d), dy] + ([add] if add is not None else [])
    return pl.pallas_call(
        functools.partial(_rms_bwd_body, has_add=add is not None),
        name=name,
        grid=(t // tr,),
        in_specs=[row, vec, row] + ([row] if add is not None else []),
        out_specs=[row, vec],
        out_shape=[jax.ShapeDtypeStruct((t, d), out_dtype), jax.ShapeDtypeStruct((1, d), F32)],
        compiler_params=_params(1),
    )(*args)


def _loss_body(y_ref, t_ref, dy_ref, l_ref):
    @pl.when(pl.program_id(0) == 0)
    def _():
        l_ref[...] = jnp.zeros_like(l_ref)

    e = y_ref[...] - t_ref[...]
    dy_ref[...] = e * (1.0 / D_MODEL)
    s = jnp.sum(e * e, axis=0, keepdims=True)
    acc = s[:, 0:LANE]
    for j in range(1, D_MODEL // LANE):
        acc = acc + s[:, j * LANE:(j + 1) * LANE]
    l_ref[...] += acc


def _loss(y, target, tr=512):
    t, d = y.shape
    tr = min(tr, t)
    row = pl.BlockSpec((tr, d), lambda i: (i, 0))
    return pl.pallas_call(
        functools.partial(_loss_body), name="loss_head", grid=(t // tr,), in_specs=[row, row],
        out_specs=[row, pl.BlockSpec((1, LANE), lambda i: (0, 0))],
        out_shape=[jax.ShapeDtypeStruct((t, d), F32), jax.ShapeDtypeStruct((1, LANE), F32)],
        compiler_params=_params(1),
    )(y, target)


CONV_CT = 128
CONV_RB = 512
CONV_PAD = 8


def _conv_rows(t):
    return min(CONV_RB, t)


def _conv_fwd_body(u_ref, w_ref, b_ref, o_ref, pad_ref, *, t):
    rb = _conv_rows(t)
    zeros = jnp.zeros((CONV_PAD, CONV_CT), F32)
    pad_ref[pl.ds(0, CONV_PAD), :] = zeros
    pad_ref[pl.ds(t + CONV_PAD, CONV_PAD), :] = zeros
    pad_ref[pl.ds(CONV_PAD, t), :] = u_ref[...].astype(F32)
    w = w_ref[...]
    b = b_ref[...]
    for c in range(t // rb):
        base = CONV_PAD + c * rb
        acc = b + w[0:1, :] * pad_ref[pl.ds(base - 2, rb), :]
        for k in range(1, SSD_CONV):
            acc = acc + w[k:k + 1, :] * pad_ref[pl.ds(base + k - 2, rb), :]
        o_ref[pl.ds(c * rb, rb), :] = acc * _sigmoid(acc)


def _conv_fwd(proj, conv_w8, conv_b):
    t = proj.shape[0]
    off = XBC0 // CONV_CT
    return pl.pallas_call(
        functools.partial(_conv_fwd_body, t=t), name="conv_fwd", grid=(CONV_DIM // CONV_CT,),
        in_specs=[pl.BlockSpec((t, CONV_CT), lambda i: (0, off + i)), pl.BlockSpec((8, CONV_CT), lambda i: (0, i)),
                  pl.BlockSpec((1, CONV_CT), lambda i: (0, i))],
        out_specs=pl.BlockSpec((t, CONV_CT), lambda i: (0, i)),
        out_shape=jax.ShapeDtypeStruct((t, CONV_DIM), F32),
        scratch_shapes=[pltpu.VMEM((t + 2 * CONV_PAD, CONV_CT), F32)],
        compiler_params=_params(1),
    )(proj, conv_w8, conv_b)


def _conv_bwd_body(u_ref, da_ref, w_ref, b_ref, _, du_ref, dw_ref, pad_ref, pad2_ref, *, t):
    rb = _conv_rows(t)
    zeros = jnp.zeros((CONV_PAD, CONV_CT), F32)
    for p in (pad_ref, pad2_ref):
        p[pl.ds(0, CONV_PAD), :] = zeros
        p[pl.ds(t + CONV_PAD, CONV_PAD), :] = zeros
    pad_ref[pl.ds(CONV_PAD, t), :] = u_ref[...].astype(F32)
    w = w_ref[...]
    b = b_ref[...]
    dw = [jnp.zeros((1, CONV_CT), F32) for _ in range(SSD_CONV + 1)]
    for c in range(t // rb):
        base = CONV_PAD + c * rb
        us = [pad_ref[pl.ds(base + k - 2, rb), :] for k in range(SSD_CONV)]
        acc = b + w[0:1, :] * us[0]
        for k in range(1, SSD_CONV):
            acc = acc + w[k:k + 1, :] * us[k]
        sg = _sigmoid(acc)
        dyc = da_ref[pl.ds(c * rb, rb), :] * (sg * (1.0 + acc * (1.0 - sg)))
        pad2_ref[pl.ds(base, rb), :] = dyc
        for k in range(SSD_CONV):
            dw[k] = dw[k] + jnp.sum(dyc * us[k], axis=0, keepdims=True)
        dw[SSD_CONV] = dw[SSD_CONV] + jnp.sum(dyc, axis=0, keepdims=True)
    for c in range(t // rb):
        base = CONV_PAD + c * rb
        acc = w[0:1, :] * pad2_ref[pl.ds(base + 2, rb), :]
        for k in range(1, SSD_CONV):
            acc = acc + w[k:k + 1, :] * pad2_ref[pl.ds(base + 2 - k, rb), :]
        du_ref[pl.ds(c * rb, rb), :] = acc.astype(du_ref.dtype)
    dw_ref[...] = jnp.concatenate(dw + [jnp.zeros((2, CONV_CT), F32)], axis=0)


def _conv_bwd(proj, dact, conv_w8, conv_b, dproj):
    t = proj.shape[0]
    off = XBC0 // CONV_CT
    col = pl.BlockSpec((t, CONV_CT), lambda i: (0, i))
    seg = pl.BlockSpec((t, CONV_CT), lambda i: (0, off + i))
    return pl.pallas_call(
        functools.partial(_conv_bwd_body, t=t), name="conv_bwd", grid=(CONV_DIM // CONV_CT,),
        in_specs=[seg, col, pl.BlockSpec((8, CONV_CT), lambda i: (0, i)), pl.BlockSpec((1, CONV_CT), lambda i: (0, i)), ANY],
        out_specs=[seg, pl.BlockSpec((8, CONV_CT), lambda i: (0, i))],
        out_shape=[jax.ShapeDtypeStruct((t, NP), BF16), jax.ShapeDtypeStruct((8, CONV_DIM), F32)],
        scratch_shapes=[pltpu.VMEM((t + 2 * CONV_PAD, CONV_CT), F32), pltpu.VMEM((t + 2 * CONV_PAD, CONV_CT), F32)],
        input_output_aliases={4: 0}, compiler_params=_params(1),
    )(proj, dact, conv_w8, conv_b, dproj)


GW = SSD_REP * SSD_HEAD_DIM


def _expand_np(hoff):
    e = np.zeros((LANE, D_INNER), np.float32)
    for h in range(SSD_HEADS):
        e[hoff + h, h * SSD_HEAD_DIM:(h + 1) * SSD_HEAD_DIM] = 1.0
    return e


def _head_masks():
    lane = lax.broadcasted_iota(jnp.int32, (1, GW), 1)
    return [((lane >= r * SSD_HEAD_DIM) & (lane < (r + 1) * SSD_HEAD_DIM)).astype(F32) for r in range(SSD_REP)]


def _ssd_common(dtc_ref, dtr_ref, bc_ref, br_ref, alc_ref, alr_ref, e_ref, rev):
    L = CHUNK
    ri = lax.broadcasted_iota(jnp.int32, (L, L), 0)
    ci = lax.broadcasted_iota(jnp.int32, (L, L), 1)
    tri = (ri <= ci) if rev else (ri >= ci)
    trit = (ri >= ci) if rev else (ri <= ci)
    raw_c = dtc_ref[...] + bc_ref[...]
    dt_c = _softplus(raw_c)
    a_c = -jnp.exp(alc_ref[...])
    cs_c = jnp.dot(tri.astype(F32), dt_c * a_c, precision=lax.Precision.HIGHEST, preferred_element_type=F32)
    dt_r = _softplus(dtr_ref[...] + br_ref[...])
    a_r = -jnp.exp(alr_ref[...])
    cs_r = jnp.dot(dt_r * a_r, trit.astype(F32), precision=lax.Precision.HIGHEST, preferred_element_type=F32)
    il = 0 if rev else L - 1
    e_c = jnp.exp(cs_c)
    w_c = jnp.exp(cs_c[il:il + 1, :] - cs_c)
    ex = _dot(_mx(jnp.concatenate([dt_c, e_c, w_c], axis=0)), e_ref[...])
    return dict(tri=tri, trit=trit, raw_c=raw_c, dt_c=dt_c, a_c=a_c, cs_c=cs_c, cs_r=cs_r, il=il, e_last=e_c[il:il + 1],
                dt_x=ex[0:L], e_x=ex[L:2 * L], w_x=ex[2 * L:3 * L], cd_x=ex[L + il:L + il + 1])


def _ssd_fwd_body(*refs, rev, hoff, nsteps, comm):
    L = CHUNK
    ins, cin, (y_ref, st_ref), cout, (s_scr,), csem = _split_refs(refs, 10, 2, comm)
    xs_ref, b_ref, c_ref, dtc_ref, dtr_ref, bc_ref, br_ref, alc_ref, alr_ref, e_ref = ins
    if comm is not None:
        comm.run(cin, cout, csem, pl.program_id(0), nsteps)

    @pl.when(pl.program_id(0) == 0)
    def _():
        s_scr[...] = jnp.zeros_like(s_scr)

    st_ref[0] = s_scr[...]
    q = _ssd_common(dtc_ref, dtr_ref, bc_ref, br_ref, alc_ref, alr_ref, e_ref, rev)
    masks = _head_masks()
    xdt = xs_ref[...] * q["dt_x"]
    xw = xdt * q["w_x"]
    early = []
    for g in range(SSD_GROUPS):
        gc = slice(g * GW, (g + 1) * GW)
        nc = slice(g * SSD_STATE, (g + 1) * SSD_STATE)
        bg = _mx(b_ref[:, nc])
        cg = _mx(c_ref[:, nc])
        sg = s_scr[:, gc]
        early.append((_dot(cg, bg, ((1,), (1,))), _dot(cg, _mx(sg)) * q["e_x"][:, gc]))
        s_scr[:, gc] = sg * q["cd_x"][:, gc] + _dot(bg, _mx(xw[:, gc]), ((0,), (0,)))
    for g in range(SSD_GROUPS):
        gc = slice(g * GW, (g + 1) * GW)
        cb, y = early[g]
        ms = []
        for r in range(SSD_REP):
            hh = hoff + g * SSD_REP + r
            diff = q["cs_c"][:, hh:hh + 1] - q["cs_r"][hh:hh + 1, :]
            ms.append(_mx(cb * jnp.exp(jnp.where(q["tri"], diff, NEG))))
        y4 = _dot(jnp.concatenate(ms, axis=0), _mx(xdt[:, gc]))
        for r in range(SSD_REP):
            y = y + y4[r * L:(r + 1) * L] * masks[r]
        y_ref[:, gc] = y


def _dt_rows_body(p_ref, o_ref):
    o_ref[...] = p_ref[...].T


def _dt_rows(proj, tr=512):
    t = proj.shape[0]
    tr = min(tr, t)
    return pl.pallas_call(
        functools.partial(_dt_rows_body), name="dt_rows", grid=(t // tr,),
        in_specs=[pl.BlockSpec((tr, LANE), lambda i: (i, 0))], out_specs=pl.BlockSpec((LANE, tr), lambda i: (0, i)),
        out_shape=jax.ShapeDtypeStruct((LANE, t), F32), compiler_params=_params(1),
    )(proj)


def _ssd_specs(t, rev):
    nc = t // CHUNK
    cm = (lambda i: nc - 1 - i) if rev else (lambda i: i)
    xs = pl.BlockSpec((CHUNK, D_INNER), lambda i: (cm(i), 0))
    bb = pl.BlockSpec((CHUNK, SSD_GROUPS * SSD_STATE), lambda i: (cm(i), 2))
    cc = pl.BlockSpec((CHUNK, SSD_GROUPS * SSD_STATE), lambda i: (cm(i), 3))
    dtc = pl.BlockSpec((CHUNK, LANE), lambda i: (cm(i), 0))
    dtr = pl.BlockSpec((LANE, CHUNK), lambda i: (0, cm(i)))
    colv = pl.BlockSpec((1, LANE), lambda i: (0, 0))
    rowv = pl.BlockSpec((LANE, 1), lambda i: (0, 0))
    return cm, xs, bb, cc, dtc, dtr, colv, rowv


def _ssd_fwd(xbc, proj, dtt, bias, alog, rev, comm=None):
    t = xbc.shape[0]
    nc = t // CHUNK
    cm, xs, bb, cc, dtc, dtr, colv, rowv = _ssd_specs(t, rev)
    hoff = SSD_HEADS if rev else 0
    e = jnp.asarray(_expand_np(hoff), BF16)
    in_specs = [xs, bb, cc, dtc, dtr, colv, rowv, colv, rowv, pl.BlockSpec((LANE, D_INNER), lambda i: (0, 0))]
    args = [xbc, xbc, xbc, proj, dtt, bias.reshape(1, LANE), bias.reshape(LANE, 1), alog.reshape(1, LANE), alog.reshape(LANE, 1), e]
    out_specs = [xs, pl.BlockSpec((1, SSD_STATE, D_INNER), lambda i: (cm(i), 0, 0))]
    out_shape = [jax.ShapeDtypeStruct((t, D_INNER), F32), jax.ShapeDtypeStruct((nc, SSD_STATE, D_INNER), F32)]
    scratch = [pltpu.VMEM((SSD_STATE, D_INNER), F32)]
    if comm is not None:
        in_specs, args = in_specs + [ANY] * comm.n, args + comm.arrs
        out_specs, out_shape = out_specs + [ANY] * comm.n, out_shape + comm.out_shapes()
        scratch = scratch + comm.sems()
    return pl.pallas_call(
        functools.partial(_ssd_fwd_body, rev=rev, hoff=hoff, nsteps=nc, comm=comm), name="ssd_fwd_rev" if rev else "ssd_fwd",
        grid=(nc,), in_specs=in_specs, out_specs=out_specs, out_shape=out_shape, scratch_shapes=scratch,
        compiler_params=_params(1),
    )(*args)


def _ssd_bwd_body(*refs, rev, hoff, first, nsteps, comm):
    L = CHUNK
    ins, cin, (dx_ref, ddt_ref, sm_ref), cout, (ds_scr,), csem = _split_refs(refs, 14 if first else 18, 3, comm)
    (xs_ref, b_ref, c_ref, dtc_ref, dtr_ref, bc_ref, br_ref, alc_ref, alr_ref, e_ref, et_ref, dy_ref, sp_ref,
     dsk_ref) = ins[:14]
    if not first:
        pdx_ref, pddt_ref, psm_ref = ins[14:17]
    if comm is not None:
        comm.run(cin, cout, csem, pl.program_id(0), nsteps)

    @pl.when(pl.program_id(0) == 0)
    def _():
        ds_scr[...] = jnp.zeros_like(ds_scr)
        sm_ref[...] = jnp.zeros_like(sm_ref) if first else psm_ref[...]

    q = _ssd_common(dtc_ref, dtr_ref, bc_ref, br_ref, alc_ref, alr_ref, e_ref, rev)
    masks = _head_masks()
    xs = xs_ref[...]
    dy = dy_ref[...]
    xdt = xs * q["dt_x"]
    xw = xdt * q["w_x"]
    dye = dy * q["e_x"]
    ds_old = ds_scr[...]
    dxdt_parts, q1_parts, q2_parts = [], [], []
    ddiag = jnp.zeros((L, LANE), F32)
    early = []
    for g in range(SSD_GROUPS):
        gc = slice(g * GW, (g + 1) * GW)
        nc = slice(g * SSD_STATE, (g + 1) * SSD_STATE)
        bg = _mx(b_ref[:, nc])
        cg = _mx(c_ref[:, nc])
        dy_g = _mx(dy[:, gc])
        xdt_g = _mx(xdt[:, gc])
        dsg = ds_old[:, gc]
        spg = _mx(sp_ref[0, :, gc])
        dye_g = _mx(dye[:, gc])
        bds = _dot(bg, _mx(dsg))
        early.append(dict(
            bg=bg, cg=cg, dy_g=dy_g, bds=bds,
            cb=_dot(cg, bg, ((1,), (1,))), cbt=_dot(bg, cg, ((1,), (1,))),
            dm=_dot(_mx(jnp.concatenate([dy[:, gc] * m for m in masks], axis=0)), xdt_g, ((1,), (1,))),
            dmt=_dot(_mx(jnp.concatenate([xdt[:, gc] * m for m in masks], axis=0)), dy_g, ((1,), (1,))),
            dc=_dot(dye_g, spg, ((1,), (1,))), db=_dot(_mx(xw[:, gc]), _mx(dsg), ((1,), (1,)))))
        q1_parts.append(dye[:, gc] * _dot(cg, spg))
        q2_parts.append(xw[:, gc] * bds)
        ds_scr[:, gc] = dsg * q["cd_x"][:, gc] + _dot(cg, dye_g, ((0,), (0,)))
    for g in range(SSD_GROUPS):
        gc = slice(g * GW, (g + 1) * GW)
        e = early[g]
        bg, cg, dy_g, bds, cb, cbt, dm, dmt = (e[k] for k in ("bg", "cg", "dy_g", "bds", "cb", "cbt", "dm", "dmt"))
        dcb = jnp.zeros((L, L), F32)
        dcbt = jnp.zeros((L, L), F32)
        mts = []
        for r in range(SSD_REP):
            hh = hoff + g * SSD_REP + r
            col = q["cs_c"][:, hh:hh + 1]
            row = q["cs_r"][hh:hh + 1, :]
            dec = jnp.exp(jnp.where(q["tri"], col - row, NEG))
            dect = jnp.exp(jnp.where(q["trit"], row - col, NEG))
            pd = dm[r * L:(r + 1) * L] * dec
            pdt = dmt[r * L:(r + 1) * L] * dect
            dcb = dcb + pd
            dcbt = dcbt + pdt
            mts.append(_mx(cbt * dect))
            ddiag = ddiag + (jnp.sum(pd * cb, axis=1, keepdims=True) - jnp.sum(pdt * cbt, axis=1, keepdims=True)) * _onehot_lane(hh)
        x4 = _dot(jnp.concatenate(mts, axis=0), dy_g)
        dxdt_g = bds * q["w_x"][:, gc]
        for r in range(SSD_REP):
            dxdt_g = dxdt_g + x4[r * L:(r + 1) * L] * masks[r]
        dxdt_parts.append(dxdt_g)
        dc_g = _dot(_mx(dcb), bg) + e["dc"]
        db_g = _dot(_mx(dcbt), cg) + e["db"]
        boff = D_INNER + g * SSD_STATE
        coff = D_INNER + SSD_GROUPS * SSD_STATE + g * SSD_STATE
        if first:
            dx_ref[:, boff:boff + SSD_STATE] = db_g
            dx_ref[:, coff:coff + SSD_STATE] = dc_g
        else:
            dx_ref[:, boff:boff + SSD_STATE] = pdx_ref[:, boff:boff + SSD_STATE] + db_g
            dx_ref[:, coff:coff + SSD_STATE] = pdx_ref[:, coff:coff + SSD_STATE] + dc_g
    dxdt = jnp.concatenate(dxdt_parts, axis=1)
    et = et_ref[...]
    hs = _dot(_mx(jnp.concatenate([jnp.concatenate(q1_parts, axis=1), jnp.concatenate(q2_parts, axis=1), dxdt * xs], axis=0)), et)
    q1, q2, r3 = hs[0:L], hs[L:2 * L], hs[2 * L:3 * L]
    t_prev = _split_dot(jnp.sum(ds_old * sp_ref[0], axis=0, keepdims=True), et) * q["e_last"]
    rows = lax.broadcasted_iota(jnp.int32, (L, LANE), 0)
    dcs = ddiag + q1 - q2 + jnp.where(rows == q["il"], jnp.sum(q2, axis=0, keepdims=True) + t_prev, 0.0)
    dad = jnp.dot(q["trit"].astype(F32), dcs, precision=lax.Precision.HIGHEST, preferred_element_type=F32)
    ddt = dad * q["a_c"] + r3
    ddt_raw = ddt * _sigmoid(q["raw_c"])
    dal = jnp.sum(dad * q["dt_c"], axis=0, keepdims=True) * q["a_c"]
    dbias = jnp.sum(ddt_raw, axis=0, keepdims=True)
    sm_ref[...] += jnp.concatenate([dal, dbias, jnp.zeros((6, LANE), F32)], axis=0)
    dxs = dxdt * q["dt_x"]
    if first:
        dx_ref[:, 0:D_INNER] = dxs + dy * dsk_ref[...]
        ddt_ref[...] = ddt_raw
    else:
        dx_ref[:, 0:D_INNER] = pdx_ref[:, 0:D_INNER] + dxs
        ddt_ref[...] = jnp.concatenate([(pddt_ref[...] + ddt_raw).astype(ddt_ref.dtype), jnp.zeros((L, DTW - LANE), ddt_ref.dtype)],
                                       axis=1)


def _ssd_bwd(xbc, proj, dtt, bias, alog, dy, states, dskip_x, rev, prev=None, dproj=None, comm=None):
    t = xbc.shape[0]
    nc = t // CHUNK
    first = prev is None
    cm, xs, bb, cc, dtc, dtr, colv, rowv = _ssd_specs(t, not rev)
    hoff = SSD_HEADS if rev else 0
    e_np = _expand_np(hoff)
    e = jnp.asarray(e_np, BF16)
    et = jnp.asarray(e_np.T, BF16)
    st_spec = pl.BlockSpec((1, SSD_STATE, D_INNER), lambda i: (cm(i), 0, 0))
    dxo = pl.BlockSpec((CHUNK, CONV_DIM), lambda i: (cm(i), 0))
    ddto = pl.BlockSpec((CHUNK, LANE), lambda i: (cm(i), 0))
    smo = pl.BlockSpec((8, LANE), lambda i: (0, 0))
    in_specs = [xs, bb, cc, dtc, dtr, colv, rowv, colv, rowv, pl.BlockSpec((LANE, D_INNER), lambda i: (0, 0)),
                pl.BlockSpec((D_INNER, LANE), lambda i: (0, 0)), xs, st_spec, pl.BlockSpec((1, D_INNER), lambda i: (0, 0))]
    args = [xbc, xbc, xbc, proj, dtt, bias.reshape(1, LANE), bias.reshape(LANE, 1), alog.reshape(1, LANE),
            alog.reshape(LANE, 1), e, et, dy, states, dskip_x]
    out_specs = [dxo, ddto, smo]
    out_shape = [jax.ShapeDtypeStruct((t, CONV_DIM), F32), jax.ShapeDtypeStruct((t, LANE), F32),
                 jax.ShapeDtypeStruct((8, LANE), F32)]
    aliases = {}
    if not first:
        in_specs += [dxo, ddto, smo, ANY]
        args += list(prev) + [dproj]
        out_specs[1] = pl.BlockSpec((CHUNK, DTW), lambda i: (cm(i), DT0 // DTW))
        out_shape[1] = jax.ShapeDtypeStruct((t, NP), BF16)
        aliases = {17: 1}
    scratch = [pltpu.VMEM((SSD_STATE, D_INNER), F32)]
    if comm is not None:
        in_specs, args = in_specs + [ANY] * comm.n, args + comm.arrs
        out_specs, out_shape = out_specs + [ANY] * comm.n, out_shape + comm.out_shapes()
        scratch = scratch + comm.sems()
    return pl.pallas_call(
        functools.partial(_ssd_bwd_body, rev=rev, hoff=hoff, first=first, nsteps=nc, comm=comm),
        name="ssd_bwd_rev" if rev else "ssd_bwd", grid=(nc,), in_specs=in_specs, out_specs=out_specs, out_shape=out_shape,
        scratch_shapes=scratch, input_output_aliases=aliases, compiler_params=_params(1),
    )(*args)


def _gnorm_parts(yf_ref, yb_ref, xs_ref, z_ref, dsk_ref):
    xs = xs_ref[...]
    y = yf_ref[...] + yb_ref[...] + xs * dsk_ref[...]
    z = z_ref[...].astype(F32)
    sg = _sigmoid(z)
    s = z * sg
    v = y * s
    rs = []
    for g in range(SSD_GROUPS):
        vg = v[:, g * GW:(g + 1) * GW]
        rs.append(jnp.broadcast_to(lax.rsqrt(jnp.mean(vg * vg, axis=-1, keepdims=True) + EPS), vg.shape))
    return xs, y, z, sg, s, v, jnp.concatenate(rs, axis=1)


def _gnorm_fwd_body(yf_ref, yb_ref, xs_ref, z_ref, dsk_ref, w_ref, o_ref):
    _, _, _, _, _, v, rstd = _gnorm_parts(yf_ref, yb_ref, xs_ref, z_ref, dsk_ref)
    o_ref[...] = (v * rstd * w_ref[...]).astype(o_ref.dtype)


def _gnorm_fwd(yf, yb, xbc, proj, dskip_x, norm_w, tr=256):
    t = yf.shape[0]
    tr = min(tr, t)
    row = pl.BlockSpec((tr, D_INNER), lambda i: (i, 0))
    vec = pl.BlockSpec((1, D_INNER), lambda i: (0, 0))
    return pl.pallas_call(
        functools.partial(_gnorm_fwd_body), name="gnorm_fwd", grid=(t // tr,),
        in_specs=[row, row, row, pl.BlockSpec((tr, D_INNER), lambda i: (i, Z0 // D_INNER)), vec, vec],
        out_specs=row, out_shape=jax.ShapeDtypeStruct((t, D_INNER), BF16), compiler_params=_params(1),
    )(yf, yb, xbc, proj, dskip_x, norm_w.reshape(1, D_INNER))


def _gnorm_bwd_body(du_ref, yf_ref, yb_ref, xs_ref, z_ref, dsk_ref, w_ref, et_ref, _, dy_ref, dz_ref, dw_ref, dd_ref, acc_ref, *,
                    nsteps):
    i = pl.program_id(0)

    @pl.when(i == 0)
    def _():
        dw_ref[...] = jnp.zeros_like(dw_ref)
        acc_ref[...] = jnp.zeros_like(acc_ref)

    xs, y, z, sg, s, v, rstd = _gnorm_parts(yf_ref, yb_ref, xs_ref, z_ref, dsk_ref)
    du = du_ref[...].astype(F32)
    n = v * rstd
    dn = du * w_ref[...]
    dw_ref[...] += jnp.sum(du * n, axis=0, keepdims=True)
    prod = dn * n
    means = []
    for g in range(SSD_GROUPS):
        pg = prod[:, g * GW:(g + 1) * GW]
        means.append(jnp.broadcast_to(jnp.mean(pg, axis=-1, keepdims=True), pg.shape))
    dv = rstd * (dn - n * jnp.concatenate(means, axis=1))
    dy = dv * s
    dy_ref[...] = dy
    dz_ref[...] = (dv * y * (sg * (1.0 + z * (1.0 - sg)))).astype(dz_ref.dtype)
    acc_ref[...] += jnp.sum(dy * xs, axis=0, keepdims=True)

    @pl.when(i == nsteps - 1)
    def _():
        dd_ref[...] = _split_dot(acc_ref[...], et_ref[...])


def _gnorm_bwd(du, yf, yb, xbc, proj, dskip_x, norm_w, dproj, tr=256):
    t = yf.shape[0]
    tr = min(tr, t)
    row = pl.BlockSpec((tr, D_INNER), lambda i: (i, 0))
    zcol = pl.BlockSpec((tr, D_INNER), lambda i: (i, Z0 // D_INNER))
    vec = pl.BlockSpec((1, D_INNER), lambda i: (0, 0))
    et = jnp.asarray(_expand_np(0).T, BF16)
    return pl.pallas_call(
        functools.partial(_gnorm_bwd_body, nsteps=t // tr), name="gnorm_bwd", grid=(t // tr,),
        in_specs=[row, row, row, row, zcol, vec, vec, pl.BlockSpec((D_INNER, LANE), lambda i: (0, 0)), ANY],
        out_specs=[row, zcol, vec, pl.BlockSpec((1, LANE), lambda i: (0, 0))],
        out_shape=[jax.ShapeDtypeStruct((t, D_INNER), F32), jax.ShapeDtypeStruct((t, NP), BF16),
                   jax.ShapeDtypeStruct((1, D_INNER), F32), jax.ShapeDtypeStruct((1, LANE), F32)],
        scratch_shapes=[pltpu.VMEM((1, D_INNER), F32)], input_output_aliases={8: 1},
        compiler_params=_params(1),
    )(du, yf, yb, xbc, proj, dskip_x, norm_w.reshape(1, D_INNER), et, dproj)


def _gate_fwd_body(gt_ref, b_ref, ys_ref, ya_ref, o_ref):
    g = _sigmoid(gt_ref[...].astype(F32) + b_ref[...])
    o_ref[...] = (g[:, 0:D_MODEL] * ys_ref[...] + g[:, D_MODEL:] * ya_ref[...]).astype(o_ref.dtype)


def _gate_fwd(proj, b_gate, y_ssd, y_attn, tr=512):
    t = y_ssd.shape[0]
    tr = min(tr, t)
    row = pl.BlockSpec((tr, D_MODEL), lambda i: (i, 0))
    return pl.pallas_call(
        functools.partial(_gate_fwd_body), name="gate_fwd", grid=(t // tr,),
        in_specs=[pl.BlockSpec((tr, 2 * D_MODEL), lambda i: (i, GT0 // (2 * D_MODEL))),
                  pl.BlockSpec((1, 2 * D_MODEL), lambda i: (0, 0)), row, row],
        out_specs=row, out_shape=jax.ShapeDtypeStruct((t, D_MODEL), BF16), compiler_params=_params(1),
    )(proj, b_gate.reshape(1, 2 * D_MODEL), y_ssd, y_attn)


def _gate_bwd_body(dm_ref, gt_ref, b_ref, ys_ref, ya_ref, dys_ref, dya_ref, dgt_ref, db_ref):
    @pl.when(pl.program_id(0) == 0)
    def _():
        db_ref[...] = jnp.zeros_like(db_ref)

    g = _sigmoid(gt_ref[...].astype(F32) + b_ref[...])
    gs = g[:, 0:D_MODEL]
    ga = g[:, D_MODEL:]
    dm = dm_ref[...].astype(F32)
    dys_ref[...] = (dm * gs).astype(dys_ref.dtype)
    dya_ref[...] = (dm * ga).astype(dya_ref.dtype)
    dgt = jnp.concatenate([dm * ys_ref[...] * gs * (1.0 - gs), dm * ya_ref[...] * ga * (1.0 - ga)], axis=1)
    dgt_ref[...] = dgt.astype(dgt_ref.dtype)
    db_ref[...] += jnp.sum(dgt, axis=0, keepdims=True)


def _gate_bwd(dmix, proj, b_gate, y_ssd, y_attn, tr=512):
    t = y_ssd.shape[0]
    tr = min(tr, t)
    row = pl.BlockSpec((tr, D_MODEL), lambda i: (i, 0))
    gates = pl.BlockSpec((tr, 2 * D_MODEL), lambda i: (i, GT0 // (2 * D_MODEL)))
    vec2 = pl.BlockSpec((1, 2 * D_MODEL), lambda i: (0, 0))
    return pl.pallas_call(
        functools.partial(_gate_bwd_body), name="gate_bwd", grid=(t // tr,),
        in_specs=[row, gates, vec2, row, row],
        out_specs=[row, row, gates, vec2],
        out_shape=[jax.ShapeDtypeStruct((t, D_MODEL), BF16), jax.ShapeDtypeStruct((t, D_MODEL), BF16),
                   jax.ShapeDtypeStruct((t, NP), BF16), jax.ShapeDtypeStruct((1, 2 * D_MODEL), F32)],
        compiler_params=_params(1),
    )(dmix, proj, b_gate.reshape(1, 2 * D_MODEL), y_ssd, y_attn)


def _bucket_np():
    i = np.arange(BLOCK)[:, None]
    j = np.arange(3 * BLOCK)[None, :]
    rel = j - BLOCK - i
    nb = N_BUCKETS // 2
    max_exact = nb // 2
    ret = np.where(rel > 0, nb, 0)
    n = np.abs(rel)
    nf = np.maximum(n, 1).astype(np.float32)
    large = max_exact + (np.log(nf / np.float32(max_exact)) / np.float32(math.log(MAX_DISTANCE / max_exact))
                         * np.float32(nb - max_exact)).astype(np.int32)
    large = np.minimum(large, nb - 1)
    bucket = ret + np.where(n < max_exact, n, large)
    band = np.abs(rel) <= WINDOW
    return bucket, band


def _onehot_np():
    bucket, _ = _bucket_np()
    oh = np.zeros((BLOCK * 3 * BLOCK, LANE), np.float32)
    oh[np.arange(oh.shape[0]), bucket.reshape(-1)] = 1.0
    return oh


REP = Q_HEADS // KV_HEADS


def _group_cols(ref, g):
    return jnp.concatenate([ref[:, (g * REP + r) * HEAD_PAD:(g * REP + r + 1) * HEAD_PAD] for r in range(REP)], axis=0)


def _group_lane(ref, g, rows):
    return jnp.concatenate([jnp.broadcast_to(ref[:, g * REP + r:g * REP + r + 1], (rows, 1)) for r in range(REP)], axis=0)


def _attn_kv(refs, g):
    gc = slice(g * HEAD_PAD, (g + 1) * HEAD_PAD)
    return _mx(jnp.concatenate([r[:, gc] for r in refs], axis=0))


def _attn_logits(q_ref, kg, bias_ref, edge, g):
    q4 = _mx(_group_cols(q_ref, g))
    s = _dot(q4, kg, ((1,), (1,))) * (HEAD_DIM ** -0.5) + bias_ref[g] + edge
    return q4, s


def _attn_edge(n, nb):
    col = lax.broadcasted_iota(jnp.int32, (1, 3 * BLOCK), 1)
    bad = ((col < BLOCK) & (n == 0)) | ((col >= 2 * BLOCK) & (n == nb - 1))
    return jnp.where(bad, NEG, 0.0)


def _onehot_lane(h):
    return (lax.broadcasted_iota(jnp.int32, (1, LANE), 1) == h).astype(F32)


def _attn_fwd_body(*refs, nb, comm):
    ins, cin, (o_ref, lse_ref), cout, _, csem = _split_refs(refs, 9, 2, comm)
    q_ref, kp_ref, kc_ref, kn_ref, vp_ref, vc_ref, vn_ref, bias_ref, sink_ref = ins
    n = pl.program_id(0)
    if comm is not None:
        comm.run(cin, cout, csem, n, nb)
    edge = _attn_edge(n, nb)
    lse = jnp.zeros((BLOCK, LANE), F32)
    logits = [_attn_logits(q_ref, _attn_kv((kp_ref, kc_ref, kn_ref), g), bias_ref, edge, g)[1] for g in range(KV_HEADS)]
    for g in range(KV_HEADS):
        vg = _attn_kv((vp_ref, vc_ref, vn_ref), g)
        s = logits[g]
        sink = _group_lane(sink_ref, g, BLOCK)
        m = jnp.maximum(jnp.max(s, axis=-1, keepdims=True), sink)
        p = jnp.exp(s - m)
        den = jnp.sum(p, axis=-1, keepdims=True) + jnp.exp(sink - m)
        o4 = _dot(_mx(p * (1.0 / den)), vg)
        l4 = m + jnp.log(den)
        for r in range(REP):
            h = g * REP + r
            o_ref[:, h * HEAD_PAD:(h + 1) * HEAD_PAD] = o4[r * BLOCK:(r + 1) * BLOCK].astype(o_ref.dtype)
            lse = lse + l4[r * BLOCK:(r + 1) * BLOCK] * _onehot_lane(h)
    lse_ref[...] = lse


def _attn_specs(t, clamp):
    nb = t // BLOCK
    cur = (lambda n: jnp.minimum(n, nb - 1)) if clamp else (lambda n: n)
    prv = lambda n: jnp.maximum(cur(n) - 1, 0)
    nxt = lambda n: jnp.minimum(cur(n) + 1, nb - 1)
    kb, vb = K0 // KP, V0 // KP
    qs = pl.BlockSpec((BLOCK, QP), lambda n: (cur(n), Q0 // QP))
    ks = [pl.BlockSpec((BLOCK, KP), lambda n, f=f: (f(n), kb)) for f in (prv, cur, nxt)]
    vs = [pl.BlockSpec((BLOCK, KP), lambda n, f=f: (f(n), vb)) for f in (prv, cur, nxt)]
    consts = [pl.BlockSpec((KV_HEADS, REP * BLOCK, 3 * BLOCK), lambda n: (0, 0, 0)), pl.BlockSpec((1, LANE), lambda n: (0, 0))]
    return nb, cur, qs, ks, vs, consts


def _attn_fwd(proj, bias, sink, comm=None):
    t = proj.shape[0]
    nb, cur, qs, ks, vs, consts = _attn_specs(t, False)
    in_specs, args = [qs] + ks + vs + consts, [proj] * 7 + [bias, sink]
    out_specs = [pl.BlockSpec((BLOCK, QP), lambda n: (n, 0)), pl.BlockSpec((BLOCK, LANE), lambda n: (n, 0))]
    out_shape = [jax.ShapeDtypeStruct((t, QP), BF16), jax.ShapeDtypeStruct((t, LANE), F32)]
    scratch = []
    if comm is not None:
        in_specs, args = in_specs + [ANY] * comm.n, args + comm.arrs
        out_specs, out_shape = out_specs + [ANY] * comm.n, out_shape + comm.out_shapes()
        scratch = comm.sems()
    return pl.pallas_call(
        functools.partial(_attn_fwd_body, nb=nb, comm=comm), name="attn_fwd", grid=(nb,),
        in_specs=in_specs, out_specs=out_specs, out_shape=out_shape, scratch_shapes=scratch, compiler_params=_params(1),
    )(*args)


def _attn_bwd_body(*refs, nb, comm):
    ins, cin, (dqkv_ref, dbias_ref, dsink_ref), cout, (ak, bk, av, bv, dq_scr), csem = _split_refs(refs, 13, 3, comm)
    q_ref, kp_ref, kc_ref, kn_ref, vp_ref, vc_ref, vn_ref, bias_ref, sink_ref, do_ref, o_ref, lse_ref = ins[:12]
    dk_ref = dqkv_ref.at[:, K0:K0 + KP]
    dv_ref = dqkv_ref.at[:, V0:V0 + KP]
    n = pl.program_id(0)
    scale = HEAD_DIM ** -0.5
    if comm is not None:
        comm.run(cin, cout, csem, n, nb + 1)

    @pl.when(n == 0)
    def _():
        for r in (ak, bk, av, bv, dq_scr, dbias_ref, dsink_ref):
            r[...] = jnp.zeros_like(r)

    dqkv_ref[:, Q0:Q0 + QP] = dq_scr[...]

    @pl.when(n < nb)
    def _():
        edge = _attn_edge(n, nb)
        dsink = jnp.zeros((1, LANE), F32)
        early = []
        for g in range(KV_HEADS):
            kg = _attn_kv((kp_ref, kc_ref, kn_ref), g)
            q4, s = _attn_logits(q_ref, kg, bias_ref, edge, g)
            do = _group_cols(do_ref, g).astype(F32)
            early.append((kg, q4, s, do, _dot(_mx(do), _attn_kv((vp_ref, vc_ref, vn_ref), g), ((1,), (1,)))))
        for g in range(KV_HEADS):
            gc = slice(g * HEAD_PAD, (g + 1) * HEAD_PAD)
            kg, q4, s, do, dp = early[g]
            lse = _group_lane(lse_ref, g, BLOCK)
            p = jnp.exp(s - lse)
            delta = jnp.sum(do * _group_cols(o_ref, g).astype(F32), axis=-1, keepdims=True)
            ds = p * (dp - delta)
            sunk = jnp.exp(_group_lane(sink_ref, g, BLOCK) - lse) * delta
            dbias_ref[g] += ds
            dsb = _mx(ds)
            dq4 = _dot(dsb, kg) * scale
            for r in range(REP):
                h = g * REP + r
                dq_scr[:, h * HEAD_PAD:(h + 1) * HEAD_PAD] = dq4[r * BLOCK:(r + 1) * BLOCK].astype(dq_scr.dtype)
                dsink = dsink - jnp.sum(sunk[r * BLOCK:(r + 1) * BLOCK], axis=0, keepdims=True) * _onehot_lane(h)
            dk_c = _dot(dsb, q4, ((0,), (0,))) * scale
            dv_c = _dot(_mx(p), _mx(do), ((0,), (0,)))
            for acc_a, acc_b, out, c in ((ak, bk, dk_ref, dk_c), (av, bv, dv_ref, dv_c)):
                out[:, gc] = (acc_a[:, gc] + c[0:BLOCK]).astype(out.dtype)
                acc_a[:, gc] = acc_b[:, gc] + c[BLOCK:2 * BLOCK]
                acc_b[:, gc] = c[2 * BLOCK:3 * BLOCK]
        dsink_ref[...] += dsink

    @pl.when(n == nb)
    def _():
        dk_ref[...] = ak[...].astype(dk_ref.dtype)
        dv_ref[...] = av[...].astype(dv_ref.dtype)


def _attn_bwd(proj, bias, sink, do, o, lse, dproj, comm=None):
    t = proj.shape[0]
    nb, cur, qs, ks, vs, consts = _attn_specs(t, True)
    rowq = pl.BlockSpec((BLOCK, QP), lambda n: (cur(n), 0))
    late = pl.BlockSpec((BLOCK, QKV), lambda n: (jnp.maximum(n - 1, 0), Q0 // QKV))
    in_specs = [qs] + ks + vs + consts + [rowq, rowq, pl.BlockSpec((BLOCK, LANE), lambda n: (cur(n), 0)), ANY]
    args = [proj] * 7 + [bias, sink, do, o, lse, dproj]
    out_specs = [late, pl.BlockSpec((KV_HEADS, REP * BLOCK, 3 * BLOCK), lambda n: (0, 0, 0)), pl.BlockSpec((1, LANE), lambda n: (0, 0))]
    out_shape = [jax.ShapeDtypeStruct((t, NP), BF16), jax.ShapeDtypeStruct((KV_HEADS, REP * BLOCK, 3 * BLOCK), F32),
                 jax.ShapeDtypeStruct((1, LANE), F32)]
    scratch = [pltpu.VMEM((BLOCK, KP), F32) for _ in range(4)] + [pltpu.VMEM((BLOCK, QP), BF16)]
    if comm is not None:
        in_specs, args = in_specs + [ANY] * comm.n, args + comm.arrs
        out_specs, out_shape = out_specs + [ANY] * comm.n, out_shape + comm.out_shapes()
        scratch = scratch + comm.sems()
    return pl.pallas_call(
        functools.partial(_attn_bwd_body, nb=nb, comm=comm), name="attn_bwd", grid=(nb + 1,),
        in_specs=in_specs, out_specs=out_specs, out_shape=out_shape, scratch_shapes=scratch, input_output_aliases={12: 0},
        compiler_params=_params(1),
    )(*args)


def _adamw_body(gp_ref, w_ref, m_ref, v_ref, g_ref, d_ref, nm_ref, nv_ref, *, nparts):
    g = gp_ref[0].astype(F32)
    for j in range(1, nparts):
        g = g + gp_ref[j].astype(F32)
    m = ADAM_B1 * m_ref[...] + (1.0 - ADAM_B1) * g
    v = ADAM_B2 * v_ref[...] + (1.0 - ADAM_B2) * (g * g)
    m_hat = m / (1.0 - ADAM_B1 ** ADAM_STEP)
    v_hat = v / (1.0 - ADAM_B2 ** ADAM_STEP)
    g_ref[...] = g
    d_ref[...] = -ADAM_LR * (m_hat / (jnp.sqrt(v_hat) + ADAM_EPS) + ADAM_WD * w_ref[...])
    nm_ref[...] = m
    nv_ref[...] = v


def _adamw_layer_body(*refs, nparts):
    gp_ref, w_ref, m_ref, v_ref = refs[:4]
    g_ref, d_ref, nm_ref, nv_ref = refs[-4:]
    g = gp_ref[0].astype(F32)
    for j in range(1, nparts):
        g = g + gp_ref[j].astype(F32)
    m = ADAM_B1 * m_ref[0] + (1.0 - ADAM_B1) * g
    v = ADAM_B2 * v_ref[0] + (1.0 - ADAM_B2) * (g * g)
    m_hat = m / (1.0 - ADAM_B1 ** ADAM_STEP)
    v_hat = v / (1.0 - ADAM_B2 ** ADAM_STEP)
    g_ref[0] = g
    d_ref[0] = -ADAM_LR * (m_hat / (jnp.sqrt(v_hat) + ADAM_EPS) + ADAM_WD * w_ref[0])
    nm_ref[0] = m
    nv_ref[0] = v


def _adamw_layer(gparts, row0, w, m, v, l, prev, name, tr=128):
    p, c = gparts.shape[0], gparts.shape[2]
    per = w.shape[1]
    tr = min(tr, per)
    assert per % tr == 0 and row0 % tr == 0 and w.shape[2] == c
    lay = pl.BlockSpec((1, tr, c), lambda i: (l, i, 0))
    in_specs = [pl.BlockSpec((p, tr, c), lambda i: (0, row0 // tr + i, 0)), lay, lay, lay]
    args = [gparts, w, m, v]
    aliases = {}
    if prev is not None:
        in_specs, args, aliases = in_specs + [ANY] * 4, args + list(prev), {4 + k: k for k in range(4)}
    sd = jax.ShapeDtypeStruct(w.shape, F32)
    return pl.pallas_call(
        functools.partial(_adamw_layer_body, nparts=p), name=name, grid=(per // tr,),
        in_specs=in_specs, out_specs=[lay] * 4, out_shape=[sd] * 4, input_output_aliases=aliases, compiler_params=_params(1),
    )(*args)


def _adamw(gparts, w, m, v, name, tr=256):
    p, r, c = gparts.shape
    tr = min(tr, r)
    assert r % tr == 0
    row = pl.BlockSpec((tr, c), lambda i: (i, 0))
    sd = jax.ShapeDtypeStruct((r, c), F32)
    return pl.pallas_call(
        functools.partial(_adamw_body, nparts=p), name=name, grid=(r // tr,),
        in_specs=[pl.BlockSpec((p, tr, c), lambda i: (0, i, 0)), row, row, row],
        out_specs=[row, row, row, row], out_shape=[sd, sd, sd, sd], compiler_params=_params(1),
    )(gparts, w, m, v)


MESH = pl.DeviceIdType.MESH
N_REL = N_DEV - 1


def _gather_phases(ins, outs, sems):
    n = len(ins)
    send_sems, recv_sems, local_sems = sems
    x, y, c = lax.axis_index("x"), lax.axis_index("y"), lax.axis_index("c")
    me, sibling = (x, y, c), (x, y, 1 - c)
    chips = [(1 - x, y), (x, 1 - y), (1 - x, 1 - y)]

    def slot(a, p):
        return outs[a].at[4 * p[0] + 2 * p[1] + p[2]]

    def copy(a, k, block, to, src=None):
        return pltpu.make_async_remote_copy(
            src_ref=slot(a, block) if src is None else src, dst_ref=slot(a, block),
            send_sem=send_sems.at[a * N_REL + k], recv_sem=recv_sems.at[a * N_REL + k], device_id=to, device_id_type=MESH)

    mine = [pltpu.make_async_copy(ins[a], slot(a, me), local_sems.at[a]) for a in range(n)]
    first = []
    for a in range(n):
        first.append(copy(a, 0, me, sibling, src=ins[a]))
        first += [copy(a, 1 + j, me, (*chip, c), src=ins[a]) for j, chip in enumerate(chips)]
    passed = [copy(a, 4 + j, (*chip, c), sibling) for j, chip in enumerate(chips) for a in range(n)]

    def start():
        for cp in mine + first:
            cp.start()

    def pass_on():
        i = 0
        for j, chip in enumerate(chips):
            for a in range(n):
                copy(a, 1 + j, (*chip, c), me).wait_recv()
                passed[i].start()
                i += 1

    def finish():
        for a in range(n):
            copy(a, 0, sibling, me).wait_recv()
            for j, chip in enumerate(chips):
                copy(a, 4 + j, (*chip, 1 - c), me).wait_recv()
        for cp in first + passed:
            cp.wait_send()
        for cp in mine:
            cp.wait()

    return start, pass_on, finish


def _exchange_phases(ins, outs, sems):
    n = len(ins)
    send_sems, recv_sems, local_sems = sems
    x, y, c = lax.axis_index("x"), lax.axis_index("y"), lax.axis_index("c")
    me = 4 * x + 2 * y + c
    mine = [pltpu.make_async_copy(ins[a].at[me], outs[a].at[me], local_sems.at[a]) for a in range(n)]
    copies = []
    for a in range(n):
        for k in range(1, N_DEV):
            px = 1 - x if k & 4 else x
            py = 1 - y if k & 2 else y
            pc = 1 - c if k & 1 else c
            peer = 4 * px + 2 * py + pc
            send = pltpu.make_async_remote_copy(
                src_ref=ins[a].at[peer], dst_ref=outs[a].at[me], send_sem=send_sems.at[a * N_REL + k - 1],
                recv_sem=recv_sems.at[a * N_REL + k - 1], device_id=(px, py, pc), device_id_type=MESH)
            recv = pltpu.make_async_remote_copy(
                src_ref=ins[a].at[me], dst_ref=outs[a].at[peer], send_sem=send_sems.at[a * N_REL + k - 1],
                recv_sem=recv_sems.at[a * N_REL + k - 1], device_id=(px, py, pc), device_id_type=MESH)
            copies.append((send, recv))

    def start():
        for cp in mine:
            cp.start()
        for send, _ in copies:
            send.start()

    def finish():
        for send, recv in copies:
            send.wait_send()
            recv.wait_recv()
        for cp in mine:
            cp.wait()

    return start, None, finish


class _Comm:
    def __init__(self, kind, arrs):
        self.kind, self.arrs, self.n = kind, list(arrs), len(arrs)

    def out_shapes(self):
        if self.kind == "gather":
            return [jax.ShapeDtypeStruct((N_DEV,) + a.shape, a.dtype) for a in self.arrs]
        return [jax.ShapeDtypeStruct(a.shape, a.dtype) for a in self.arrs]

    def sems(self):
        return [pltpu.SemaphoreType.DMA((self.n * N_REL,)), pltpu.SemaphoreType.DMA((self.n * N_REL,)),
                pltpu.SemaphoreType.DMA((self.n,))]

    def run(self, ins, outs, sems, step, nsteps):
        start, mid, finish = (_gather_phases if self.kind == "gather" else _exchange_phases)(ins, outs, sems)
        if nsteps is None:
            start()
            if mid is not None:
                mid()
            finish()
            return
        pl.when(step == 0)(start)
        if mid is not None:
            pl.when(step == nsteps // 2)(mid)
        pl.when(step == nsteps - 1)(finish)


def _split_refs(refs, n_in, n_out, comm):
    k = comm.n if comm is not None else 0
    i = 0
    parts = []
    for cnt in (n_in, k, n_out, k):
        parts.append(refs[i:i + cnt])
        i += cnt
    rest = refs[i:]
    if comm is None:
        return parts[0], (), parts[2], (), rest, ()
    return parts[0], parts[1], parts[2], parts[3], rest[:len(rest) - 3], rest[len(rest) - 3:]


def _comm_body(*refs, comm):
    n = comm.n
    comm.run(refs[:n], refs[n:2 * n], refs[2 * n:], None, None)


def _communicate(comm, name, vmem=False):
    spec = pl.BlockSpec(memory_space=pltpu.VMEM if vmem else pl.ANY)
    return pl.pallas_call(
        functools.partial(_comm_body, comm=comm), name=name,
        in_specs=[spec] * comm.n, out_specs=[spec] * comm.n, out_shape=comm.out_shapes(), scratch_shapes=comm.sems(),
        compiler_params=pltpu.CompilerParams(vmem_limit_bytes=VMEM_LIMIT),
    )(*comm.arrs)


ANY = pl.BlockSpec(memory_space=pl.ANY)


def _pad_heads(w, axis):
    shp = w.shape
    heads = shp[axis] // HEAD_DIM
    w = w.reshape(shp[:axis] + (heads, HEAD_DIM) + shp[axis + 1:])
    pad = [(0, 0)] * w.ndim
    pad[axis + 1] = (0, HEAD_PAD - HEAD_DIM)
    w = jnp.pad(w, pad)
    return w.reshape(shp[:axis] + (heads * HEAD_PAD,) + shp[axis + 1:])


def _unpad_heads(w, axis):
    shp = w.shape
    heads = shp[axis] // HEAD_PAD
    w = w.reshape(shp[:axis] + (heads, HEAD_PAD) + shp[axis + 1:])
    w = lax.slice_in_dim(w, 0, HEAD_DIM, axis=axis + 1)
    return w.reshape(shp[:axis] + (heads * HEAD_DIM,) + shp[axis + 1:])


def _w_in_to_padded(w):
    idx = np.cumsum((0,) + IN_SPLITS)
    z, xbc, dt, q, k, v, gates = [w[:, idx[i]:idx[i + 1]] for i in range(7)]
    zeros = lambda n: jnp.zeros((w.shape[0], n), w.dtype)
    return jnp.concatenate([_pad_heads(q, 1), _pad_heads(k, 1), _pad_heads(v, 1), xbc, dt, zeros(DTW - 2 * SSD_HEADS), z, gates],
                           axis=1)


def _w_in_from_padded(w):
    z = w[:, Z0:Z0 + D_INNER]
    gates = w[:, GT0:GT0 + 2 * D_MODEL]
    q = _unpad_heads(w[:, Q0:Q0 + QP], 1)
    xbc = w[:, XBC0:XBC0 + CONV_DIM]
    k = _unpad_heads(w[:, K0:K0 + KP], 1)
    v = _unpad_heads(w[:, V0:V0 + KP], 1)
    dt = w[:, DT0:DT0 + 2 * SSD_HEADS]
    return jnp.concatenate([z, xbc, dt, q, k, v, gates], axis=1)


def _pad_lane(v):
    v = v.reshape(-1)
    return jnp.pad(v, (0, LANE - v.shape[0]))


def _layer_consts(p):
    c = dict(p)
    c["conv_w8"] = jnp.pad(p["conv_w"].reshape(SSD_CONV, CONV_DIM), ((0, 8 - SSD_CONV), (0, 0)))
    c["conv_b1"] = p["conv_b"].reshape(1, CONV_DIM)
    c["dtb"] = _pad_lane(p["dt_bias"])
    c["alog"] = _pad_lane(p["a_log"])
    c["dskip_x"] = jnp.repeat(p["d_skip"], SSD_HEAD_DIM).reshape(1, D_INNER)
    c["sink"] = _pad_lane(p["attn_sink"]).reshape(1, LANE)
    return c


def _hosted(res, n_own, hook):
    res = list(res)
    if hook is not None:
        hook[1](res[n_own:])
    return res[0] if n_own == 1 else res[:n_own]


def _layer_fwd(x, c, w, bias, l, hooks):
    comm = lambda k: hooks[k][0] if k in hooks else None
    s = {"x": x}
    s["h"] = _rms_fwd(x, c["pre_mix_norm"], f"pre_mix_norm_{l}", out_dtype=BF16)
    res = _mm(s["h"], w["w_in"], f"in_proj_{l}", out_dtype=BF16, mode="dt_side", comm=comm("in_proj"))
    proj, s["dt"] = _hosted(res, 2, hooks.get("in_proj"))
    s["proj"] = proj
    s["xbc"] = _conv_fwd(proj, c["conv_w8"], c["conv_b1"])
    s["dtt"] = _dt_rows(s["dt"])
    s["yf"], s["stf"] = _hosted(_ssd_fwd(s["xbc"], s["dt"], s["dtt"], c["dtb"], c["alog"], False, comm=comm("ssd_fwd")), 2,
                                hooks.get("ssd_fwd"))
    s["yb"], s["stb"] = _hosted(_ssd_fwd(s["xbc"], s["dt"], s["dtt"], c["dtb"], c["alog"], True, comm=comm("ssd_fwd_rev")), 2,
                                hooks.get("ssd_fwd_rev"))
    s["u"] = _gnorm_fwd(s["yf"], s["yb"], s["xbc"], proj, c["dskip_x"], c["ssd_norm"])
    s["y_ssd"] = _mm(s["u"], w["w_ssd_out"], f"ssd_out_{l}")
    s["o"], s["lse"] = _hosted(_attn_fwd(proj, bias, c["sink"], comm=comm("attn_fwd")), 2, hooks.get("attn_fwd"))
    s["y_attn"] = _mm(s["o"], w["w_attn_out"], f"attn_out_{l}")
    s["mix"] = _gate_fwd(proj, c["b_gate"], s["y_ssd"], s["y_attn"])
    s["mixed"] = _mm(s["mix"], w["w_o"], f"w_o_{l}")
    s["x1"] = _rms_fwd(s["mixed"], c["post_mix_norm"], f"post_mix_norm_{l}", res=x)
    s["h2"] = _rms_fwd(s["x1"], c["pre_mlp_norm"], f"pre_mlp_norm_{l}", out_dtype=BF16)
    s["f1"], s["a"] = _mm(s["h2"], w["w_mlp_in"], f"mlp_in_{l}", mode="relu2")
    s["f"] = _mm(s["a"], w["w_mlp_out"], f"mlp_out_{l}")
    x2 = _rms_fwd(s["f"], c["post_mlp_norm"], f"post_mlp_norm_{l}", res=s["x1"])
    return x2, s


def _send_rest(gw):
    s_mi = gw["w_mlp_in"].reshape(D_MODEL, N_DEV, -1).transpose(1, 0, 2).astype(BF16)
    parts = [(_unpad_heads(gw[n], 0) if n == "w_attn_out" else gw[n]).reshape(N_DEV, -1, D_MODEL) for n in ROWS]
    return [s_mi, jnp.concatenate(parts, axis=1).astype(BF16)]


def _send_w_in(g):
    return [_w_in_from_padded(g).reshape(D_MODEL, N_DEV, -1).transpose(1, 0, 2).astype(BF16)]


def _layer_bwd(dx2, s, c, w, bias, l, earlier=None, host_own=False):
    gw, gs, got = {}, {}, {}
    df, gs["post_mlp_norm"] = _rms_bwd(s["f"], c["post_mlp_norm"], dx2, f"post_mlp_norm_bwd_{l}", out_dtype=BF16)
    df1 = _mm(df, w["w_mlp_out"], f"mlp_out_dx_{l}", tb=True, mode="relu2_bwd", extra=s["f1"], out_dtype=BF16)
    gw["w_mlp_out"] = _mm(s["a"], df, f"mlp_out_dw_{l}", ta=True, out_dtype=BF16)
    gw["w_mlp_in"] = _mm(s["h2"], df1, f"mlp_in_dw_{l}", ta=True, out_dtype=BF16)
    dh2 = _mm(df1, w["w_mlp_in"], f"mlp_in_dx_{l}", tb=True, out_dtype=BF16)
    dx1, gs["pre_mlp_norm"] = _rms_bwd(s["x1"], c["pre_mlp_norm"], dh2, f"pre_mlp_norm_bwd_{l}", add=dx2)
    dmixed, gs["post_mix_norm"] = _rms_bwd(s["mixed"], c["post_mix_norm"], dx1, f"post_mix_norm_bwd_{l}", out_dtype=BF16)
    gw["w_o"] = _mm(s["mix"], dmixed, f"w_o_dw_{l}", ta=True, out_dtype=BF16)
    dmix = _mm(dmixed, w["w_o"], f"w_o_dx_{l}", tb=True, out_dtype=BF16)
    dys, dya, dproj, gs["b_gate"] = _gate_bwd(dmix, s["proj"], c["b_gate"], s["y_ssd"], s["y_attn"])
    gw["w_ssd_out"] = _mm(s["u"], dys, f"ssd_out_dw_{l}", ta=True, out_dtype=BF16)
    du = _mm(dys, w["w_ssd_out"], f"ssd_out_dx_{l}", tb=True, out_dtype=BF16)
    gw["w_attn_out"] = _mm(s["o"], dya, f"attn_out_dw_{l}", ta=True, out_dtype=BF16)
    do = _mm(dya, w["w_attn_out"], f"attn_out_dx_{l}", tb=True, out_dtype=BF16)
    res = _attn_bwd(s["proj"], bias, c["sink"], do, s["o"], s["lse"], dproj,
                    comm=None if earlier is None else _Comm("exchange", earlier[:1]))
    (dproj, dbias, dsink), got["earlier"] = res[:3], list(res[3:])
    gs["attn_sink"] = dsink
    dy, dproj, gs["ssd_norm"], gs["d_skip"] = _gnorm_bwd(du, s["yf"], s["yb"], s["xbc"], s["proj"], c["dskip_x"], c["ssd_norm"],
                                                         dproj)
    res = _ssd_bwd(s["xbc"], s["dt"], s["dtt"], c["dtb"], c["alog"], dy, s["stf"], c["dskip_x"], False,
                   comm=None if earlier is None else _Comm("exchange", earlier[1:]))
    part, got["earlier"] = res[:3], got["earlier"] + list(res[3:])
    res = _ssd_bwd(s["xbc"], s["dt"], s["dtt"], c["dtb"], c["alog"], dy, s["stb"], c["dskip_x"], True, prev=part, dproj=dproj,
                   comm=_Comm("exchange", _send_rest(gw)) if host_own else None)
    (dxbc, dproj, sm), got["rest"] = res[:3], res[3:]
    gs["a_log"], gs["dt_bias"] = sm[0:1], sm[1:2]
    dproj, gs["conv"] = _conv_bwd(s["proj"], dxbc, c["conv_w8"], c["conv_b1"], dproj)
    gw["w_in"] = _mm(s["h"], dproj, f"in_proj_dw_{l}", ta=True, out_dtype=BF16)
    if host_own:
        dh, *got["w_in"] = _mm(dproj, w["w_in"], f"in_proj_dx_{l}", tb=True, out_dtype=BF16, comm=_Comm("exchange", _send_w_in(gw["w_in"])))
    else:
        dh = _mm(dproj, w["w_in"], f"in_proj_dx_{l}", tb=True, out_dtype=BF16)
    dx, gs["pre_mix_norm"] = _rms_bwd(s["x"], c["pre_mix_norm"], dh, f"pre_mix_norm_bwd_{l}", add=dx1)
    return dx, gw, gs, dbias, got


SMALL = (("pre_mix_norm", DEPTH * D_MODEL), ("b_gate", DEPTH * 2 * D_MODEL), ("conv_w", DEPTH * SSD_CONV * CONV_DIM),
         ("conv_b", DEPTH * CONV_DIM), ("dt_bias", DEPTH * 2 * SSD_HEADS), ("a_log", DEPTH * 2 * SSD_HEADS),
         ("d_skip", DEPTH * SSD_HEADS), ("ssd_norm", DEPTH * D_INNER), ("attn_sink", DEPTH * Q_HEADS),
         ("rel_bias_table", N_BUCKETS * Q_HEADS), ("post_mix_norm", DEPTH * D_MODEL), ("pre_mlp_norm", DEPTH * D_MODEL),
         ("post_mlp_norm", DEPTH * D_MODEL))


def _small_rows(n):
    return -(-n // LANE)


def _pack_small(vals, fill=0.0):
    rows = []
    for name, n in SMALL:
        v = vals[name].reshape(-1).astype(F32)
        rows.append(jnp.pad(v, (0, _small_rows(n) * LANE - n), constant_values=fill).reshape(-1, LANE))
    out = jnp.concatenate(rows, axis=0)
    return jnp.pad(out, ((0, -out.shape[0] % 8), (0, 0)), constant_values=fill)


def _unpack_small(packed, shapes):
    out, r = {}, 0
    for name, n in SMALL:
        nr = _small_rows(n)
        out[name] = packed[r:r + nr].reshape(-1)[:n].reshape(shapes[name])
        r += nr
    return out


BIG = ("w_in", "w_ssd_out", "w_attn_out", "w_o", "w_mlp_in", "w_mlp_out")
ROWS = ("w_ssd_out", "w_attn_out", "w_o", "w_mlp_out")


def _pack_rows(vals, lead):
    return jnp.concatenate([vals[n].reshape(lead + (-1, D_MODEL)) for n in ROWS], axis=len(lead))


def kernel(x, pre_mix_norm, w_in, b_gate, conv_w, conv_b, dt_bias, a_log, d_skip, ssd_norm, w_ssd_out, attn_sink, rel_bias_table, w_attn_out, w_o, post_mix_norm, pre_mlp_norm, w_mlp_in, w_mlp_out, post_mlp_norm, loss_target, m_pre_mix_norm, m_w_in, m_b_gate, m_conv_w, m_conv_b, m_dt_bias, m_a_log, m_d_skip, m_ssd_norm, m_w_ssd_out, m_attn_sink, m_rel_bias_table, m_w_attn_out, m_w_o, m_post_mix_norm, m_pre_mlp_norm, m_w_mlp_in, m_w_mlp_out, m_post_mlp_norm, v_pre_mix_norm, v_w_in, v_b_gate, v_conv_w, v_conv_b, v_dt_bias, v_a_log, v_d_skip, v_ssd_norm, v_w_ssd_out, v_attn_sink, v_rel_bias_table, v_w_attn_out, v_w_o, v_post_mix_norm, v_pre_mlp_norm, v_w_mlp_in, v_w_mlp_out, v_post_mlp_norm):
    names = ("pre_mix_norm", "w_in", "b_gate", "conv_w", "conv_b", "dt_bias", "a_log", "d_skip", "ssd_norm", "w_ssd_out",
             "attn_sink", "rel_bias_table", "w_attn_out", "w_o", "post_mix_norm", "pre_mlp_norm", "w_mlp_in", "w_mlp_out",
             "post_mlp_norm")
    W = dict(zip(names, (pre_mix_norm, w_in, b_gate, conv_w, conv_b, dt_bias, a_log, d_skip, ssd_norm, w_ssd_out, attn_sink,
                         rel_bias_table, w_attn_out, w_o, post_mix_norm, pre_mlp_norm, w_mlp_in, w_mlp_out, post_mlp_norm)))
    M = dict(zip(names, (m_pre_mix_norm, m_w_in, m_b_gate, m_conv_w, m_conv_b, m_dt_bias, m_a_log, m_d_skip, m_ssd_norm,
                         m_w_ssd_out, m_attn_sink, m_rel_bias_table, m_w_attn_out, m_w_o, m_post_mix_norm, m_pre_mlp_norm,
                         m_w_mlp_in, m_w_mlp_out, m_post_mlp_norm)))
    V = dict(zip(names, (v_pre_mix_norm, v_w_in, v_b_gate, v_conv_w, v_conv_b, v_dt_bias, v_a_log, v_d_skip, v_ssd_norm,
                         v_w_ssd_out, v_attn_sink, v_rel_bias_table, v_w_attn_out, v_w_o, v_post_mix_norm, v_pre_mlp_norm,
                         v_w_mlp_in, v_w_mlp_out, v_post_mlp_norm)))
    t = x.shape[1]
    shard = {n: W[n].shape for n in names}
    dev = 4 * lax.axis_index("x") + 2 * lax.axis_index("y") + lax.axis_index("c")
    cshard = CONV_DIM // N_DEV

    sh_in = [w_in[l].astype(BF16) for l in range(DEPTH)]
    sh_rest = [[w_mlp_in[l].astype(BF16), _pack_rows({n: W[n][l].astype(BF16) for n in ROWS}, ())] for l in range(DEPTH)]
    full = [{} for _ in range(DEPTH)]

    def take_w_in(l):
        def take(got):
            full[l]["w_in"] = _w_in_to_padded(got[0].transpose(1, 0, 2).reshape(D_MODEL, N_IN))
        return take

    def take_mlp_in(l):
        def take(got):
            full[l]["w_mlp_in"] = got[0].transpose(1, 0, 2).reshape(D_MODEL, D_FF)
        return take

    def take_rows(l):
        def take(got):
            r = 0
            for n in ROWS:
                per = shard[n][1]
                full[l][n] = got[0][:, r:r + per].reshape(N_DEV * per, D_MODEL)
                r += per
            full[l]["w_attn_out"] = _pad_heads(full[l]["w_attn_out"], 0)
        return take

    def take_rest(l):
        def take(got):
            take_mlp_in(l)(got[:1])
            take_rows(l)(got[1:])
        return take

    take_w_in(0)(_communicate(_Comm("gather", [sh_in[0]]), "gather_w_in_0"))
    (g_conv,) = _communicate(_Comm("gather", [conv_w.reshape(-1, LANE)]), "gather_conv_w", vmem=True)
    conv_full = g_conv.reshape(N_DEV, DEPTH, SSD_CONV, cshard).transpose(1, 2, 0, 3).reshape(DEPTH, SSD_CONV, CONV_DIM)
    hooks = [{"in_proj": (_Comm("gather", sh_rest[0]), take_rest(0)),
              "ssd_fwd": (_Comm("gather", sh_rest[1][1:]), take_rows(1)),
              "ssd_fwd_rev": (_Comm("gather", sh_rest[1][:1]), take_mlp_in(1)),
              "attn_fwd": (_Comm("gather", [sh_in[1]]), take_w_in(1))}, {}]

    _, band = _bucket_np()
    onehot = jnp.asarray(_onehot_np(), BF16)
    table_t = jnp.pad(rel_bias_table.T, ((0, 0), (0, LANE - N_BUCKETS)))
    bias = _mm(table_t, onehot, "t5_bias", tb=True, tm=Q_HEADS, tn=3 * BLOCK * BLOCK // 8, split_a=True)
    bias = (bias.reshape(Q_HEADS, BLOCK, 3 * BLOCK) + jnp.asarray(np.where(band, 0.0, NEG), F32)).reshape(
        KV_HEADS, REP * BLOCK, 3 * BLOCK)

    consts = []
    for l in range(DEPTH):
        p = {n: W[n][l] for n in names if n not in BIG and n not in ("rel_bias_table", "conv_w")}
        p["conv_w"] = conv_full[l]
        consts.append(_layer_consts(p))
    h = x[0]
    saved = []
    for l in range(DEPTH):
        h, s = _layer_fwd(h, consts[l], full[l], bias, l, hooks[l])
        saved.append(s)
    dy, lsum = _loss(h, loss_target[0])
    loss = lax.psum(0.5 / D_MODEL * jnp.sum(lsum), ("x", "y", "c"))

    gss, dbs = [None] * DEPTH, [None] * DEPTH
    dy, gw1, gss[1], dbs[1], _ = _layer_bwd(dy, saved[1], consts[1], full[1], bias, 1)
    dy, _, gss[0], dbs[0], got = _layer_bwd(dy, saved[0], consts[0], full[0], bias, 0,
                                            earlier=_send_w_in(gw1["w_in"]) + _send_rest(gw1), host_own=True)
    recv = [list(got["w_in"]) + list(got["rest"]), list(got["earlier"])]
    grad_x = dy[None]
    dbias = jnp.concatenate([d.reshape(Q_HEADS, -1) for d in dbs], axis=1)
    d_table = _mm(dbias, jnp.concatenate([onehot] * DEPTH, axis=0), "t5_bias_bwd", tm=Q_HEADS, tk=3 * BLOCK * BLOCK // 8,
                  split_a=True)

    sg = {}
    for n in ("pre_mix_norm", "b_gate", "ssd_norm", "post_mix_norm", "pre_mlp_norm", "post_mlp_norm"):
        sg[n] = jnp.stack([gss[l][n].reshape(-1) for l in range(DEPTH)])
    sg["conv_w"] = jnp.stack([gss[l]["conv"][0:SSD_CONV] for l in range(DEPTH)])
    sg["conv_b"] = jnp.stack([gss[l]["conv"][SSD_CONV] for l in range(DEPTH)])
    sg["dt_bias"] = jnp.stack([gss[l]["dt_bias"][0, 0:2 * SSD_HEADS] for l in range(DEPTH)])
    sg["a_log"] = jnp.stack([gss[l]["a_log"][0, 0:2 * SSD_HEADS] for l in range(DEPTH)])
    sg["d_skip"] = jnp.stack([gss[l]["d_skip"][0, 0:SSD_HEADS] for l in range(DEPTH)])
    sg["attn_sink"] = jnp.stack([gss[l]["attn_sink"][0, 0:Q_HEADS] for l in range(DEPTH)])
    sg["rel_bias_table"] = d_table[:, 0:N_BUCKETS].T
    (small_parts,) = _communicate(_Comm("gather", [_pack_small(sg)]), "gather_small_grads", vmem=True)
    sshape = {n: W[n].shape for n, _ in SMALL}
    sshape["conv_w"] = (DEPTH, SSD_CONV, 1, CONV_DIM)
    pk = lambda d, fill: _pack_small({n: (jnp.full((DEPTH, SSD_CONV, CONV_DIM), fill, F32) if n == "conv_w" else d[n])
                                      for n, _ in SMALL}, fill)
    s_out = [_unpack_small(o, sshape) for o in _adamw(small_parts, pk(W, 1.0), pk(M, 1.0), pk(V, 1.0), "adamw_small", tr=1024)]
    conv_g = lax.dynamic_slice_in_dim(s_out[0]["conv_w"], dev * cshard, cshard, axis=3)
    c_out = _adamw(conv_g.reshape(1, DEPTH * SSD_CONV, cshard), conv_w.reshape(-1, cshard), m_conv_w.reshape(-1, cshard),
                   v_conv_w.reshape(-1, cshard), "adamw_conv_w", tr=DEPTH * SSD_CONV)
    for i in range(4):
        s_out[i]["conv_w"] = c_out[i].reshape(conv_w.shape)

    b_out = {n: None for n in BIG}
    for l in (1, 0):
        r_in, r_mi, r_rows = recv[l]
        b_out["w_in"] = _adamw_layer(r_in, 0, w_in, m_w_in, v_w_in, l, b_out["w_in"], f"adamw_w_in_{l}", tr=256)
        b_out["w_mlp_in"] = _adamw_layer(r_mi, 0, w_mlp_in, m_w_mlp_in, v_w_mlp_in, l, b_out["w_mlp_in"], f"adamw_w_mlp_in_{l}",
                                         tr=256)
        r = 0
        for n in ROWS:
            b_out[n] = _adamw_layer(r_rows, r, W[n], M[n], V[n], l, b_out[n], f"adamw_{n}_{l}")
            r += shard[n][1]

    outs = [loss, grad_x]
    for i in range(4):
        for n in names:
            outs.append(b_out[n][i] if n in BIG else s_out[i][n])
    return tuple(outs)
```

```python
import functools
import math

import numpy as np
import jax
import jax.numpy as jnp
from jax import lax
from jax.experimental import pallas as pl
from jax.experimental.pallas import tpu as pltpu

F32 = jnp.float32
BF16 = jnp.bfloat16

D_MODEL = 1024
DEPTH = 2
D_INNER = 2048
SSD_HEADS = 32
SSD_HEAD_DIM = 64
SSD_GROUPS = 8
SSD_REP = 4
SSD_STATE = 128
SSD_CONV = 5
CHUNK = 128
CONV_DIM = 4096
Q_HEADS = 16
KV_HEADS = 4
HEAD_DIM = 64
ATTN_WIDTH = 1024
KV_WIDTH = 256
WINDOW = 128
BLOCK = 128
N_BUCKETS = 32
MAX_DISTANCE = 128
D_FF = 4096
EPS = 1e-6
IN_SPLITS = (D_INNER, CONV_DIM, 2 * SSD_HEADS, ATTN_WIDTH, KV_WIDTH, KV_WIDTH, 2 * D_MODEL)
N_IN = sum(IN_SPLITS)
N_DEV = 8

ADAM_LR = 0.001
ADAM_B1 = 0.9
ADAM_B2 = 0.999
ADAM_EPS = 1e-08
ADAM_WD = 0.01
ADAM_STEP = 10

LANE = 128
HEAD_PAD = LANE
QP = Q_HEADS * HEAD_PAD
KP = KV_HEADS * HEAD_PAD
QKV = QP + 2 * KP
Q0 = 0
K0 = QP
V0 = QP + KP
XBC0 = QKV
DT0 = XBC0 + CONV_DIM
DTW = 1024
Z0 = DT0 + DTW
GT0 = Z0 + D_INNER
NP = GT0 + 2 * D_MODEL
NEG = -1e30
VMEM_LIMIT = 56 * 1024 * 1024


def _params(n_grid):
    return pltpu.CompilerParams(dimension_semantics=("arbitrary",) * n_grid, vmem_limit_bytes=VMEM_LIMIT)


def _dot(a, b, dims=((1,), (0,))):
    return lax.dot_general(a, b, (dims, ((), ())), preferred_element_type=F32)


def _mx(a):
    return a.astype(BF16)


def _split_dot(x, e):
    hi = x.astype(BF16)
    lo = (x - hi.astype(F32)).astype(BF16)
    return _dot(hi, e) + _dot(lo, e)


def _softplus(x):
    u = jnp.exp(-jnp.abs(x))
    w = 1.0 + u
    log1p = jnp.where(w == 1.0, u, jnp.log(w) * u / jnp.where(w == 1.0, 1.0, w - 1.0))
    return jnp.maximum(x, 0.0) + log1p


def _sigmoid(x):
    return 1.0 / (1.0 + jnp.exp(-x))


def _mm_body(*refs, grid, ta, tb, split_a, mode, comm):
    n_in = 3 if mode == "relu2_bwd" else 2
    n_out = 2 if mode in ("relu2", "dt_side") else 1
    ins, cin, outs, cout, (acc_ref,), csem = _split_refs(refs, n_in, n_out, comm)
    a_ref, b_ref = ins[0], ins[1]
    i, j, k = pl.program_id(0), pl.program_id(1), pl.program_id(2)
    nk = grid[2]
    if comm is not None:
        comm.run(cin, cout, csem, (i * grid[1] + j) * nk + k, grid[0] * grid[1] * nk)

    dims = ((0 if ta else 1,), (1 if tb else 0,))
    b = _mx(b_ref[...])
    if split_a:
        a32 = a_ref[...].astype(F32)
        hi = a32.astype(BF16)
        lo = (a32 - hi.astype(F32)).astype(BF16)
        part = _dot(hi, b, dims) + _dot(lo, b, dims)
    else:
        part = _dot(_mx(a_ref[...]), b, dims)

    if nk > 1:
        @pl.when(k == 0)
        def _():
            acc_ref[...] = jnp.zeros_like(acc_ref)

        acc_ref[...] += part

    @pl.when(k == nk - 1)
    def _():
        acc = part if nk == 1 else acc_ref[...]
        if mode == "relu2":
            outs[0][...] = acc.astype(outs[0].dtype)
            r = jnp.maximum(acc, 0.0)
            outs[1][...] = (r * r).astype(outs[1].dtype)
        elif mode == "relu2_bwd":
            outs[0][...] = (acc * 2.0 * jnp.maximum(ins[2][...].astype(F32), 0.0)).astype(outs[0].dtype)
        else:
            outs[0][...] = acc.astype(outs[0].dtype)
            if mode == "dt_side":
                @pl.when(j == DT0 // outs[0].shape[1])
                def _():
                    off = DT0 % outs[0].shape[1]
                    outs[1][...] = acc[:, off:off + LANE]


def _tile(n, pref):
    t = min(n, pref)
    while n % t:
        t -= LANE
    assert t > 0 and n % t == 0, (n, pref)
    return t


def _mm(a, b, name, ta=False, tb=False, out_dtype=F32, tm=1024, tn=2048, tk=1024, split_a=False, mode=None, extra=None,
        comm=None):
    m, k = (a.shape[1], a.shape[0]) if ta else a.shape
    n = b.shape[0] if tb else b.shape[1]
    assert (b.shape[1] if tb else b.shape[0]) == k
    tm, tn, tk = _tile(m, tm), _tile(n, tn), _tile(k, tk)
    grid = (m // tm, n // tn, k // tk)
    a_spec = pl.BlockSpec((tk, tm), lambda i, j, kk: (kk, i)) if ta else pl.BlockSpec((tm, tk), lambda i, j, kk: (i, kk))
    b_spec = pl.BlockSpec((tn, tk), lambda i, j, kk: (j, kk)) if tb else pl.BlockSpec((tk, tn), lambda i, j, kk: (kk, j))
    o_spec = pl.BlockSpec((tm, tn), lambda i, j, kk: (i, j))
    in_specs, args = [a_spec, b_spec], [a, b]
    out_specs, out_shape = [o_spec], [jax.ShapeDtypeStruct((m, n), out_dtype)]
    if mode == "relu2":
        out_specs, out_shape = [o_spec, o_spec], [jax.ShapeDtypeStruct((m, n), BF16), jax.ShapeDtypeStruct((m, n), BF16)]
    elif mode == "dt_side":
        assert DT0 % tn + LANE <= tn
        out_specs = [o_spec, pl.BlockSpec((tm, LANE), lambda i, j, kk: (i, 0))]
        out_shape = out_shape + [jax.ShapeDtypeStruct((m, LANE), F32)]
    elif mode == "relu2_bwd":
        in_specs, args = in_specs + [o_spec], args + [extra]
    scratch = [pltpu.VMEM((tm, tn) if grid[2] > 1 else (8, LANE), F32)]
    if comm is not None:
        in_specs, args = in_specs + [ANY] * comm.n, args + comm.arrs
        out_specs, out_shape = out_specs + [ANY] * comm.n, out_shape + comm.out_shapes()
        scratch = scratch + comm.sems()
    res = pl.pallas_call(
        functools.partial(_mm_body, grid=grid, ta=ta, tb=tb, split_a=split_a, mode=mode, comm=comm),
        name=name, grid=grid, in_specs=in_specs, out_specs=out_specs, out_shape=out_shape, scratch_shapes=scratch,
        compiler_params=_params(3),
    )(*args)
    return res[0] if len(res) == 1 else res


def _rms_fwd_body(*refs, has_res, has_next):
    x_ref, g_ref = refs[:2]
    x = x_ref[...]
    y = x * lax.rsqrt(jnp.mean(x * x, axis=-1, keepdims=True) + EPS) * g_ref[...]
    if has_res:
        y = refs[2][...] + y
    if has_next:
        o_ref, h_ref = refs[-2:]
        g2_ref = refs[3 if has_res else 2]
        h_ref[...] = (y * lax.rsqrt(jnp.mean(y * y, axis=-1, keepdims=True) + EPS) * g2_ref[...]).astype(h_ref.dtype)
    else:
        o_ref = refs[-1]
    o_ref[...] = y.astype(o_ref.dtype)


def _rms_fwd(x, g, name, res=None, out_dtype=F32, next_g=None, tr=512):
    t, d = x.shape
    tr = min(tr, t)
    row = pl.BlockSpec((tr, d), lambda i: (i, 0))
    vec = pl.BlockSpec((1, d), lambda i: (0, 0))
    args = [x, g.reshape(1, d)] + ([res] if res is not None else []) + ([next_g.reshape(1, d)] if next_g is not None else [])
    out_specs, out_shape = [row], [jax.ShapeDtypeStruct((t, d), out_dtype)]
    if next_g is not None:
        out_specs, out_shape = [row, row], out_shape + [jax.ShapeDtypeStruct((t, d), BF16)]
    res_ = pl.pallas_call(
        functools.partial(_rms_fwd_body, has_res=res is not None, has_next=next_g is not None),
        name=name,
        grid=(t // tr,),
        in_specs=[row, vec] + ([row] if res is not None else []) + ([vec] if next_g is not None else []),
        out_specs=out_specs,
        out_shape=out_shape,
        compiler_params=_params(1),
    )(*args)
    return res_ if next_g is not None else res_[0]


def _rms_bwd_body(*refs, has_add):
    if has_add:
        x_ref, g_ref, dy_ref, add_ref, dx_ref, dg_ref = refs
    else:
        x_ref, g_ref, dy_ref, dx_ref, dg_ref = refs

    @pl.when(pl.program_id(0) == 0)
    def _():
        dg_ref[...] = jnp.zeros_like(dg_ref)

    x = x_ref[...]
    dy = dy_ref[...].astype(F32)
    rstd = lax.rsqrt(jnp.mean(x * x, axis=-1, keepdims=True) + EPS)
    n = x * rstd
    dn = dy * g_ref[...]
    dx = rstd * (dn - n * jnp.mean(dn * n, axis=-1, keepdims=True))
    if has_add:
        dx = dx + add_ref[...]
    dx_ref[...] = dx.astype(dx_ref.dtype)
    dg_ref[...] += jnp.sum(dy * n, axis=0, keepdims=True)


def _rms_bwd(x, g, dy, name, add=None, out_dtype=F32, tr=512):
    t, d = x.shape
    tr = min(tr, t)
    row = pl.BlockSpec((tr, d), lambda i: (i, 0))
    vec = pl.BlockSpec((1, d), lambda i: (0, 0))
    args = [x, g.reshape(1, d), dy] + ([add] if add is not None else [])
    return pl.pallas_call(
        functools.partial(_rms_bwd_body, has_add=add is not None),
        name=name,
        grid=(t // tr,),
        in_specs=[row, vec, row] + ([row] if add is not None else []),
        out_specs=[row, vec],
        out_shape=[jax.ShapeDtypeStruct((t, d), out_dtype), jax.ShapeDtypeStruct((1, d), F32)],
        compiler_params=_params(1),
    )(*args)


def _loss_body(y_ref, t_ref, dy_ref, l_ref):
    @pl.when(pl.program_id(0) == 0)
    def _():
        l_ref[...] = jnp.zeros_like(l_ref)

    e = y_ref[...] - t_ref[...]
    dy_ref[...] = e * (1.0 / D_MODEL)
    s = jnp.sum(e * e, axis=0, keepdims=True)
    acc = s[:, 0:LANE]
    for j in range(1, D_MODEL // LANE):
        acc = acc + s[:, j * LANE:(j + 1) * LANE]
    l_ref[...] += acc


def _loss(y, target, tr=512):
    t, d = y.shape
    tr = min(tr, t)
    row = pl.BlockSpec((tr, d), lambda i: (i, 0))
    return pl.pallas_call(
        functools.partial(_loss_body), name="loss_head", grid=(t // tr,), in_specs=[row, row],
        out_specs=[row, pl.BlockSpec((1, LANE), lambda i: (0, 0))],
        out_shape=[jax.ShapeDtypeStruct((t, d), F32), jax.ShapeDtypeStruct((1, LANE), F32)],
        compiler_params=_params(1),
    )(y, target)


CONV_CT = 128
CONV_RB = 512
CONV_PAD = 8


def _conv_rows(t):
    return min(CONV_RB, t)


def _conv_fwd_body(u_ref, w_ref, b_ref, o_ref, pad_ref, *, t):
    rb = _conv_rows(t)
    zeros = jnp.zeros((CONV_PAD, CONV_CT), F32)
    pad_ref[pl.ds(0, CONV_PAD), :] = zeros
    pad_ref[pl.ds(t + CONV_PAD, CONV_PAD), :] = zeros
    pad_ref[pl.ds(CONV_PAD, t), :] = u_ref[...].astype(F32)
    w = w_ref[...]
    b = b_ref[...]
    for c in range(t // rb):
        base = CONV_PAD + c * rb
        acc = b + w[0:1, :] * pad_ref[pl.ds(base - 2, rb), :]
        for k in range(1, SSD_CONV):
            acc = acc + w[k:k + 1, :] * pad_ref[pl.ds(base + k - 2, rb), :]
        o_ref[pl.ds(c * rb, rb), :] = acc * _sigmoid(acc)


def _conv_fwd(proj, conv_w8, conv_b):
    t = proj.shape[0]
    off = XBC0 // CONV_CT
    return pl.pallas_call(
        functools.partial(_conv_fwd_body, t=t), name="conv_fwd", grid=(CONV_DIM // CONV_CT,),
        in_specs=[pl.BlockSpec((t, CONV_CT), lambda i: (0, off + i)), pl.BlockSpec((8, CONV_CT), lambda i: (0, i)),
                  pl.BlockSpec((1, CONV_CT), lambda i: (0, i))],
        out_specs=pl.BlockSpec((t, CONV_CT), lambda i: (0, i)),
        out_shape=jax.ShapeDtypeStruct((t, CONV_DIM), F32),
        scratch_shapes=[pltpu.VMEM((t + 2 * CONV_PAD, CONV_CT), F32)],
        compiler_params=_params(1),
    )(proj, conv_w8, conv_b)


def _conv_bwd_body(u_ref, da_ref, w_ref, b_ref, _, du_ref, dw_ref, pad_ref, pad2_ref, *, t):
    rb = _conv_rows(t)
    zeros = jnp.zeros((CONV_PAD, CONV_CT), F32)
    for p in (pad_ref, pad2_ref):
        p[pl.ds(0, CONV_PAD), :] = zeros
        p[pl.ds(t + CONV_PAD, CONV_PAD), :] = zeros
    pad_ref[pl.ds(CONV_PAD, t), :] = u_ref[...].astype(F32)
    w = w_ref[...]
    b = b_ref[...]
    dw = [jnp.zeros((1, CONV_CT), F32) for _ in range(SSD_CONV + 1)]
    for c in range(t // rb):
        base = CONV_PAD + c * rb
        us = [pad_ref[pl.ds(base + k - 2, rb), :] for k in range(SSD_CONV)]
        acc = b + w[0:1, :] * us[0]
        for k in range(1, SSD_CONV):
            acc = acc + w[k:k + 1, :] * us[k]
        sg = _sigmoid(acc)
        dyc = da_ref[pl.ds(c * rb, rb), :] * (sg * (1.0 + acc * (1.0 - sg)))
        pad2_ref[pl.ds(base, rb), :] = dyc
        for k in range(SSD_CONV):
            dw[k] = dw[k] + jnp.sum(dyc * us[k], axis=0, keepdims=True)
        dw[SSD_CONV] = dw[SSD_CONV] + jnp.sum(dyc, axis=0, keepdims=True)
    for c in range(t // rb):
        base = CONV_PAD + c * rb
        acc = w[0:1, :] * pad2_ref[pl.ds(base + 2, rb), :]
        for k in range(1, SSD_CONV):
            acc = acc + w[k:k + 1, :] * pad2_ref[pl.ds(base + 2 - k, rb), :]
        du_ref[pl.ds(c * rb, rb), :] = acc.astype(du_ref.dtype)
    dw_ref[...] = jnp.concatenate(dw + [jnp.zeros((2, CONV_CT), F32)], axis=0)


def _conv_bwd(proj, dact, conv_w8, conv_b, dproj):
    t = proj.shape[0]
    off = XBC0 // CONV_CT
    col = pl.BlockSpec((t, CONV_CT), lambda i: (0, i))
    seg = pl.BlockSpec((t, CONV_CT), lambda i: (0, off + i))
    return pl.pallas_call(
        functools.partial(_conv_bwd_body, t=t), name="conv_bwd", grid=(CONV_DIM // CONV_CT,),
        in_specs=[seg, col, pl.BlockSpec((8, CONV_CT), lambda i: (0, i)), pl.BlockSpec((1, CONV_CT), lambda i: (0, i)), ANY],
        out_specs=[seg, pl.BlockSpec((8, CONV_CT), lambda i: (0, i))],
        out_shape=[jax.ShapeDtypeStruct((t, NP), BF16), jax.ShapeDtypeStruct((8, CONV_DIM), F32)],
        scratch_shapes=[pltpu.VMEM((t + 2 * CONV_PAD, CONV_CT), F32), pltpu.VMEM((t + 2 * CONV_PAD, CONV_CT), F32)],
        input_output_aliases={4: 0}, compiler_params=_params(1),
    )(proj, dact, conv_w8, conv_b, dproj)


GW = SSD_REP * SSD_HEAD_DIM


def _expand_np(hoff):
    e = np.zeros((LANE, D_INNER), np.float32)
    for h in range(SSD_HEADS):
        e[hoff + h, h * SSD_HEAD_DIM:(h + 1) * SSD_HEAD_DIM] = 1.0
    return e


def _head_masks():
    lane = lax.broadcasted_iota(jnp.int32, (1, GW), 1)
    return [((lane >= r * SSD_HEAD_DIM) & (lane < (r + 1) * SSD_HEAD_DIM)).astype(F32) for r in range(SSD_REP)]


def _ssd_common(dtc_ref, dtr_ref, bc_ref, br_ref, alc_ref, alr_ref, e_ref, rev):
    L = CHUNK
    ri = lax.broadcasted_iota(jnp.int32, (L, L), 0)
    ci = lax.broadcasted_iota(jnp.int32, (L, L), 1)
    tri = (ri <= ci) if rev else (ri >= ci)
    trit = (ri >= ci) if rev else (ri <= ci)
    raw_c = dtc_ref[...] + bc_ref[...]
    dt_c = _softplus(raw_c)
    a_c = -jnp.exp(alc_ref[...])
    cs_c = jnp.dot(tri.astype(F32), dt_c * a_c, precision=lax.Precision.HIGHEST, preferred_element_type=F32)
    dt_r = _softplus(dtr_ref[...] + br_ref[...])
    a_r = -jnp.exp(alr_ref[...])
    cs_r = jnp.dot(dt_r * a_r, trit.astype(F32), precision=lax.Precision.HIGHEST, preferred_element_type=F32)
    il = 0 if rev else L - 1
    e_c = jnp.exp(cs_c)
    w_c = jnp.exp(cs_c[il:il + 1, :] - cs_c)
    ex = _dot(_mx(jnp.concatenate([dt_c, e_c, w_c], axis=0)), e_ref[...])
    return dict(tri=tri, trit=trit, raw_c=raw_c, dt_c=dt_c, a_c=a_c, cs_c=cs_c, cs_r=cs_r, il=il, e_last=e_c[il:il + 1],
                dt_x=ex[0:L], e_x=ex[L:2 * L], w_x=ex[2 * L:3 * L], cd_x=ex[L + il:L + il + 1])


def _ssd_fwd_body(*refs, rev, hoff, nsteps, comm, has_add):
    L = CHUNK
    ins, cin, (y_ref, st_ref), cout, (s_scr,), csem = _split_refs(refs, 11 if has_add else 10, 2, comm)
    xs_ref, b_ref, c_ref, dtc_ref, dtr_ref, bc_ref, br_ref, alc_ref, alr_ref, e_ref = ins[:10]
    if comm is not None:
        comm.run(cin, cout, csem, pl.program_id(0), nsteps)

    @pl.when(pl.program_id(0) == 0)
    def _():
        s_scr[...] = jnp.zeros_like(s_scr)

    st_ref[0] = s_scr[...]
    q = _ssd_common(dtc_ref, dtr_ref, bc_ref, br_ref, alc_ref, alr_ref, e_ref, rev)
    masks = _head_masks()
    xdt = xs_ref[...] * q["dt_x"]
    xw = xdt * q["w_x"]
    early = []
    for g in range(SSD_GROUPS):
        gc = slice(g * GW, (g + 1) * GW)
        nc = slice(g * SSD_STATE, (g + 1) * SSD_STATE)
        bg = _mx(b_ref[:, nc])
        cg = _mx(c_ref[:, nc])
        sg = s_scr[:, gc]
        early.append((_dot(cg, bg, ((1,), (1,))), _dot(cg, _mx(sg)) * q["e_x"][:, gc]))
        s_scr[:, gc] = sg * q["cd_x"][:, gc] + _dot(bg, _mx(xw[:, gc]), ((0,), (0,)))
    for g in range(SSD_GROUPS):
        gc = slice(g * GW, (g + 1) * GW)
        cb, y = early[g]
        ms = []
        for r in range(SSD_REP):
            hh = hoff + g * SSD_REP + r
            diff = q["cs_c"][:, hh:hh + 1] - q["cs_r"][hh:hh + 1, :]
            ms.append(_mx(cb * jnp.exp(jnp.where(q["tri"], diff, NEG))))
        y4 = _dot(jnp.concatenate(ms, axis=0), _mx(xdt[:, gc]))
        for r in range(SSD_REP):
            y = y + y4[r * L:(r + 1) * L] * masks[r]
        y_ref[:, gc] = y + ins[10][:, gc] if has_add else y


def _dt_rows_body(p_ref, o_ref):
    o_ref[...] = p_ref[...].T


def _dt_rows(proj, tr=512):
    t = proj.shape[0]
    tr = min(tr, t)
    return pl.pallas_call(
        functools.partial(_dt_rows_body), name="dt_rows", grid=(t // tr,),
        in_specs=[pl.BlockSpec((tr, LANE), lambda i: (i, 0))], out_specs=pl.BlockSpec((LANE, tr), lambda i: (0, i)),
        out_shape=jax.ShapeDtypeStruct((LANE, t), F32), compiler_params=_params(1),
    )(proj)


def _ssd_specs(t, rev):
    nc = t // CHUNK
    cm = (lambda i: nc - 1 - i) if rev else (lambda i: i)
    xs = pl.BlockSpec((CHUNK, D_INNER), lambda i: (cm(i), 0))
    bb = pl.BlockSpec((CHUNK, SSD_GROUPS * SSD_STATE), lambda i: (cm(i), 2))
    cc = pl.BlockSpec((CHUNK, SSD_GROUPS * SSD_STATE), lambda i: (cm(i), 3))
    dtc = pl.BlockSpec((CHUNK, LANE), lambda i: (cm(i), 0))
    dtr = pl.BlockSpec((LANE, CHUNK), lambda i: (0, cm(i)))
    colv = pl.BlockSpec((1, LANE), lambda i: (0, 0))
    rowv = pl.BlockSpec((LANE, 1), lambda i: (0, 0))
    return cm, xs, bb, cc, dtc, dtr, colv, rowv


def _ssd_fwd(xbc, proj, dtt, bias, alog, rev, comm=None, add=None):
    t = xbc.shape[0]
    nc = t // CHUNK
    cm, xs, bb, cc, dtc, dtr, colv, rowv = _ssd_specs(t, rev)
    hoff = SSD_HEADS if rev else 0
    e = jnp.asarray(_expand_np(hoff), BF16)
    in_specs = [xs, bb, cc, dtc, dtr, colv, rowv, colv, rowv, pl.BlockSpec((LANE, D_INNER), lambda i: (0, 0))]
    args = [xbc, xbc, xbc, proj, dtt, bias.reshape(1, LANE), bias.reshape(LANE, 1), alog.reshape(1, LANE), alog.reshape(LANE, 1), e]
    out_specs = [xs, pl.BlockSpec((1, SSD_STATE, D_INNER), lambda i: (cm(i), 0, 0))]
    out_shape = [jax.ShapeDtypeStruct((t, D_INNER), F32), jax.ShapeDtypeStruct((nc, SSD_STATE, D_INNER), F32)]
    scratch = [pltpu.VMEM((SSD_STATE, D_INNER), F32)]
    if add is not None:
        in_specs, args = in_specs + [xs], args + [add]
    if comm is not None:
        in_specs, args = in_specs + [ANY] * comm.n, args + comm.arrs
        out_specs, out_shape = out_specs + [ANY] * comm.n, out_shape + comm.out_shapes()
        scratch = scratch + comm.sems()
    return pl.pallas_call(
        functools.partial(_ssd_fwd_body, rev=rev, hoff=hoff, nsteps=nc, comm=comm, has_add=add is not None), name="ssd_fwd_rev" if rev else "ssd_fwd",
        grid=(nc,), in_specs=in_specs, out_specs=out_specs, out_shape=out_shape, scratch_shapes=scratch,
        compiler_params=_params(1),
    )(*args)


def _ssd_bwd_body(*refs, rev, hoff, first, nsteps, comm):
    L = CHUNK
    ins, cin, (dx_ref, ddt_ref, sm_ref), cout, (ds_scr,), csem = _split_refs(refs, 14 if first else 18, 3, comm)
    (xs_ref, b_ref, c_ref, dtc_ref, dtr_ref, bc_ref, br_ref, alc_ref, alr_ref, e_ref, et_ref, dy_ref, sp_ref,
     dsk_ref) = ins[:14]
    if not first:
        pdx_ref, pddt_ref, psm_ref = ins[14:17]
    if comm is not None:
        comm.run(cin, cout, csem, pl.program_id(0), nsteps)

    @pl.when(pl.program_id(0) == 0)
    def _():
        ds_scr[...] = jnp.zeros_like(ds_scr)
        sm_ref[...] = jnp.zeros_like(sm_ref) if first else psm_ref[...]

    q = _ssd_common(dtc_ref, dtr_ref, bc_ref, br_ref, alc_ref, alr_ref, e_ref, rev)
    masks = _head_masks()
    xs = xs_ref[...]
    dy = dy_ref[...]
    xdt = xs * q["dt_x"]
    xw = xdt * q["w_x"]
    dye = dy * q["e_x"]
    ds_old = ds_scr[...]
    dxdt_parts, q1_parts, q2_parts = [], [], []
    ddiag = jnp.zeros((L, LANE), F32)
    early = []
    for g in range(SSD_GROUPS):
        gc = slice(g * GW, (g + 1) * GW)
        nc = slice(g * SSD_STATE, (g + 1) * SSD_STATE)
        bg = _mx(b_ref[:, nc])
        cg = _mx(c_ref[:, nc])
        dy_g = _mx(dy[:, gc])
        xdt_g = _mx(xdt[:, gc])
        dsg = ds_old[:, gc]
        spg = _mx(sp_ref[0, :, gc])
        dye_g = _mx(dye[:, gc])
        bds = _dot(bg, _mx(dsg))
        early.append(dict(
            bg=bg, cg=cg, dy_g=dy_g, bds=bds,
            cb=_dot(cg, bg, ((1,), (1,))), cbt=_dot(bg, cg, ((1,), (1,))),
            dm=_dot(_mx(jnp.concatenate([dy[:, gc] * m for m in masks], axis=0)), xdt_g, ((1,), (1,))),
            dmt=_dot(_mx(jnp.concatenate([xdt[:, gc] * m for m in masks], axis=0)), dy_g, ((1,), (1,))),
            dc=_dot(dye_g, spg, ((1,), (1,))), db=_dot(_mx(xw[:, gc]), _mx(dsg), ((1,), (1,)))))
        q1_parts.append(dye[:, gc] * _dot(cg, spg))
        q2_parts.append(xw[:, gc] * bds)
        ds_scr[:, gc] = dsg * q["cd_x"][:, gc] + _dot(cg, dye_g, ((0,), (0,)))
    for g in range(SSD_GROUPS):
        gc = slice(g * GW, (g + 1) * GW)
        e = early[g]
        bg, cg, dy_g, bds, cb, cbt, dm, dmt = (e[k] for k in ("bg", "cg", "dy_g", "bds", "cb", "cbt", "dm", "dmt"))
        dcb = jnp.zeros((L, L), F32)
        dcbt = jnp.zeros((L, L), F32)
        mts = []
        for r in range(SSD_REP):
            hh = hoff + g * SSD_REP + r
            col = q["cs_c"][:, hh:hh + 1]
            row = q["cs_r"][hh:hh + 1, :]
            dec = jnp.exp(jnp.where(q["tri"], col - row, NEG))
            dect = jnp.exp(jnp.where(q["trit"], row - col, NEG))
            pd = dm[r * L:(r + 1) * L] * dec
            pdt = dmt[r * L:(r + 1) * L] * dect
            dcb = dcb + pd
            dcbt = dcbt + pdt
            mts.append(_mx(cbt * dect))
            ddiag = ddiag + (jnp.sum(pd * cb, axis=1, keepdims=True) - jnp.sum(pdt * cbt, axis=1, keepdims=True)) * _onehot_lane(hh)
        x4 = _dot(jnp.concatenate(mts, axis=0), dy_g)
        dxdt_g = bds * q["w_x"][:, gc]
        for r in range(SSD_REP):
            dxdt_g = dxdt_g + x4[r * L:(r + 1) * L] * masks[r]
        dxdt_parts.append(dxdt_g)
        dc_g = _dot(_mx(dcb), bg) + e["dc"]
        db_g = _dot(_mx(dcbt), cg) + e["db"]
        boff = D_INNER + g * SSD_STATE
        coff = D_INNER + SSD_GROUPS * SSD_STATE + g * SSD_STATE
        if first:
            dx_ref[:, boff:boff + SSD_STATE] = db_g
            dx_ref[:, coff:coff + SSD_STATE] = dc_g
        else:
            dx_ref[:, boff:boff + SSD_STATE] = pdx_ref[:, boff:boff + SSD_STATE] + db_g
            dx_ref[:, coff:coff + SSD_STATE] = pdx_ref[:, coff:coff + SSD_STATE] + dc_g
    dxdt = jnp.concatenate(dxdt_parts, axis=1)
    et = et_ref[...]
    hs = _dot(_mx(jnp.concatenate([jnp.concatenate(q1_parts, axis=1), jnp.concatenate(q2_parts, axis=1), dxdt * xs], axis=0)), et)
    q1, q2, r3 = hs[0:L], hs[L:2 * L], hs[2 * L:3 * L]
    t_prev = _split_dot(jnp.sum(ds_old * sp_ref[0], axis=0, keepdims=True), et) * q["e_last"]
    rows = lax.broadcasted_iota(jnp.int32, (L, LANE), 0)
    dcs = ddiag + q1 - q2 + jnp.where(rows == q["il"], jnp.sum(q2, axis=0, keepdims=True) + t_prev, 0.0)
    dad = jnp.dot(q["trit"].astype(F32), dcs, precision=lax.Precision.HIGHEST, preferred_element_type=F32)
    ddt = dad * q["a_c"] + r3
    ddt_raw = ddt * _sigmoid(q["raw_c"])
    dal = jnp.sum(dad * q["dt_c"], axis=0, keepdims=True) * q["a_c"]
    dbias = jnp.sum(ddt_raw, axis=0, keepdims=True)
    sm_ref[...] += jnp.concatenate([dal, dbias, jnp.zeros((6, LANE), F32)], axis=0)
    dxs = dxdt * q["dt_x"]
    if first:
        dx_ref[:, 0:D_INNER] = dxs + dy * dsk_ref[...]
        ddt_ref[...] = ddt_raw
    else:
        dx_ref[:, 0:D_INNER] = pdx_ref[:, 0:D_INNER] + dxs
        ddt_ref[...] = jnp.concatenate([(pddt_ref[...] + ddt_raw).astype(ddt_ref.dtype), jnp.zeros((L, DTW - LANE), ddt_ref.dtype)],
                                       axis=1)


def _ssd_bwd(xbc, proj, dtt, bias, alog, dy, states, dskip_x, rev, prev=None, dproj=None, comm=None):
    t = xbc.shape[0]
    nc = t // CHUNK
    first = prev is None
    cm, xs, bb, cc, dtc, dtr, colv, rowv = _ssd_specs(t, not rev)
    hoff = SSD_HEADS if rev else 0
    e_np = _expand_np(hoff)
    e = jnp.asarray(e_np, BF16)
    et = jnp.asarray(e_np.T, BF16)
    st_spec = pl.BlockSpec((1, SSD_STATE, D_INNER), lambda i: (cm(i), 0, 0))
    dxo = pl.BlockSpec((CHUNK, CONV_DIM), lambda i: (cm(i), 0))
    ddto = pl.BlockSpec((CHUNK, LANE), lambda i: (cm(i), 0))
    smo = pl.BlockSpec((8, LANE), lambda i: (0, 0))
    in_specs = [xs, bb, cc, dtc, dtr, colv, rowv, colv, rowv, pl.BlockSpec((LANE, D_INNER), lambda i: (0, 0)),
                pl.BlockSpec((D_INNER, LANE), lambda i: (0, 0)), xs, st_spec, pl.BlockSpec((1, D_INNER), lambda i: (0, 0))]
    args = [xbc, xbc, xbc, proj, dtt, bias.reshape(1, LANE), bias.reshape(LANE, 1), alog.reshape(1, LANE),
            alog.reshape(LANE, 1), e, et, dy, states, dskip_x]
    out_specs = [dxo, ddto, smo]
    out_shape = [jax.ShapeDtypeStruct((t, CONV_DIM), F32), jax.ShapeDtypeStruct((t, LANE), F32),
                 jax.ShapeDtypeStruct((8, LANE), F32)]
    aliases = {}
    if not first:
        in_specs += [dxo, ddto, smo, ANY]
        args += list(prev) + [dproj]
        out_specs[1] = pl.BlockSpec((CHUNK, DTW), lambda i: (cm(i), DT0 // DTW))
        out_shape[1] = jax.ShapeDtypeStruct((t, NP), BF16)
        aliases = {17: 1}
    scratch = [pltpu.VMEM((SSD_STATE, D_INNER), F32)]
    if comm is not None:
        in_specs, args = in_specs + [ANY] * comm.n, args + comm.arrs
        out_specs, out_shape = out_specs + [ANY] * comm.n, out_shape + comm.out_shapes()
        scratch = scratch + comm.sems()
    return pl.pallas_call(
        functools.partial(_ssd_bwd_body, rev=rev, hoff=hoff, first=first, nsteps=nc, comm=comm),
        name="ssd_bwd_rev" if rev else "ssd_bwd", grid=(nc,), in_specs=in_specs, out_specs=out_specs, out_shape=out_shape,
        scratch_shapes=scratch, input_output_aliases=aliases, compiler_params=_params(1),
    )(*args)


def _gnorm_parts(y_ref, xs_ref, z_ref, dsk_ref):
    xs = xs_ref[...]
    y = y_ref[...] + xs * dsk_ref[...]
    z = z_ref[...].astype(F32)
    sg = _sigmoid(z)
    s = z * sg
    v = y * s
    rs = []
    for g in range(SSD_GROUPS):
        vg = v[:, g * GW:(g + 1) * GW]
        rs.append(jnp.broadcast_to(lax.rsqrt(jnp.mean(vg * vg, axis=-1, keepdims=True) + EPS), vg.shape))
    return xs, y, z, sg, s, v, jnp.concatenate(rs, axis=1)


def _gnorm_fwd_body(y_ref, xs_ref, z_ref, dsk_ref, w_ref, o_ref):
    _, _, _, _, _, v, rstd = _gnorm_parts(y_ref, xs_ref, z_ref, dsk_ref)
    o_ref[...] = (v * rstd * w_ref[...]).astype(o_ref.dtype)


def _gnorm_fwd(y, xbc, proj, dskip_x, norm_w, tr=256):
    t = y.shape[0]
    tr = min(tr, t)
    row = pl.BlockSpec((tr, D_INNER), lambda i: (i, 0))
    vec = pl.BlockSpec((1, D_INNER), lambda i: (0, 0))
    return pl.pallas_call(
        functools.partial(_gnorm_fwd_body), name="gnorm_fwd", grid=(t // tr,),
        in_specs=[row, row, pl.BlockSpec((tr, D_INNER), lambda i: (i, Z0 // D_INNER)), vec, vec],
        out_specs=row, out_shape=jax.ShapeDtypeStruct((t, D_INNER), BF16), compiler_params=_params(1),
    )(y, xbc, proj, dskip_x, norm_w.reshape(1, D_INNER))


def _gnorm_bwd_body(du_ref, y_ref, xs_ref, z_ref, dsk_ref, w_ref, et_ref, _, dy_ref, dz_ref, dw_ref, dd_ref, acc_ref, *,
                    nsteps):
    i = pl.program_id(0)

    @pl.when(i == 0)
    def _():
        dw_ref[...] = jnp.zeros_like(dw_ref)
        acc_ref[...] = jnp.zeros_like(acc_ref)

    xs, y, z, sg, s, v, rstd = _gnorm_parts(y_ref, xs_ref, z_ref, dsk_ref)
    du = du_ref[...].astype(F32)
    n = v * rstd
    dn = du * w_ref[...]
    dw_ref[...] += jnp.sum(du * n, axis=0, keepdims=True)
    prod = dn * n
    means = []
    for g in range(SSD_GROUPS):
        pg = prod[:, g * GW:(g + 1) * GW]
        means.append(jnp.broadcast_to(jnp.mean(pg, axis=-1, keepdims=True), pg.shape))
    dv = rstd * (dn - n * jnp.concatenate(means, axis=1))
    dy = dv * s
    dy_ref[...] = dy
    dz_ref[...] = (dv * y * (sg * (1.0 + z * (1.0 - sg)))).astype(dz_ref.dtype)
    acc_ref[...] += jnp.sum(dy * xs, axis=0, keepdims=True)

    @pl.when(i == nsteps - 1)
    def _():
        dd_ref[...] = _split_dot(acc_ref[...], et_ref[...])


def _gnorm_bwd(du, y, xbc, proj, dskip_x, norm_w, dproj, tr=256):
    t = y.shape[0]
    tr = min(tr, t)
    row = pl.BlockSpec((tr, D_INNER), lambda i: (i, 0))
    zcol = pl.BlockSpec((tr, D_INNER), lambda i: (i, Z0 // D_INNER))
    vec = pl.BlockSpec((1, D_INNER), lambda i: (0, 0))
    et = jnp.asarray(_expand_np(0).T, BF16)
    return pl.pallas_call(
        functools.partial(_gnorm_bwd_body, nsteps=t // tr), name="gnorm_bwd", grid=(t // tr,),
        in_specs=[row, row, row, zcol, vec, vec, pl.BlockSpec((D_INNER, LANE), lambda i: (0, 0)), ANY],
        out_specs=[row, zcol, vec, pl.BlockSpec((1, LANE), lambda i: (0, 0))],
        out_shape=[jax.ShapeDtypeStruct((t, D_INNER), F32), jax.ShapeDtypeStruct((t, NP), BF16),
                   jax.ShapeDtypeStruct((1, D_INNER), F32), jax.ShapeDtypeStruct((1, LANE), F32)],
        scratch_shapes=[pltpu.VMEM((1, D_INNER), F32)], input_output_aliases={7: 1},
        compiler_params=_params(1),
    )(du, y, xbc, proj, dskip_x, norm_w.reshape(1, D_INNER), et, dproj)


def _gate_fwd_body(gt_ref, b_ref, ys_ref, ya_ref, o_ref):
    g = _sigmoid(gt_ref[...].astype(F32) + b_ref[...])
    o_ref[...] = (g[:, 0:D_MODEL] * ys_ref[...] + g[:, D_MODEL:] * ya_ref[...]).astype(o_ref.dtype)


def _gate_fwd(proj, b_gate, y_ssd, y_attn, tr=512):
    t = y_ssd.shape[0]
    tr = min(tr, t)
    row = pl.BlockSpec((tr, D_MODEL), lambda i: (i, 0))
    return pl.pallas_call(
        functools.partial(_gate_fwd_body), name="gate_fwd", grid=(t // tr,),
        in_specs=[pl.BlockSpec((tr, 2 * D_MODEL), lambda i: (i, GT0 // (2 * D_MODEL))),
                  pl.BlockSpec((1, 2 * D_MODEL), lambda i: (0, 0)), row, row],
        out_specs=row, out_shape=jax.ShapeDtypeStruct((t, D_MODEL), BF16), compiler_params=_params(1),
    )(proj, b_gate.reshape(1, 2 * D_MODEL), y_ssd, y_attn)


def _gate_bwd_body(dm_ref, gt_ref, b_ref, ys_ref, ya_ref, dys_ref, dya_ref, dgt_ref, db_ref):
    @pl.when(pl.program_id(0) == 0)
    def _():
        db_ref[...] = jnp.zeros_like(db_ref)

    g = _sigmoid(gt_ref[...].astype(F32) + b_ref[...])
    gs = g[:, 0:D_MODEL]
    ga = g[:, D_MODEL:]
    dm = dm_ref[...].astype(F32)
    dys_ref[...] = (dm * gs).astype(dys_ref.dtype)
    dya_ref[...] = (dm * ga).astype(dya_ref.dtype)
    dgt = jnp.concatenate([dm * ys_ref[...] * gs * (1.0 - gs), dm * ya_ref[...] * ga * (1.0 - ga)], axis=1)
    dgt_ref[...] = dgt.astype(dgt_ref.dtype)
    db_ref[...] += jnp.sum(dgt, axis=0, keepdims=True)


def _gate_bwd(dmix, proj, b_gate, y_ssd, y_attn, tr=512):
    t = y_ssd.shape[0]
    tr = min(tr, t)
    row = pl.BlockSpec((tr, D_MODEL), lambda i: (i, 0))
    gates = pl.BlockSpec((tr, 2 * D_MODEL), lambda i: (i, GT0 // (2 * D_MODEL)))
    vec2 = pl.BlockSpec((1, 2 * D_MODEL), lambda i: (0, 0))
    return pl.pallas_call(
        functools.partial(_gate_bwd_body), name="gate_bwd", grid=(t // tr,),
        in_specs=[row, gates, vec2, row, row],
        out_specs=[row, row, gates, vec2],
        out_shape=[jax.ShapeDtypeStruct((t, D_MODEL), BF16), jax.ShapeDtypeStruct((t, D_MODEL), BF16),
                   jax.ShapeDtypeStruct((t, NP), BF16), jax.ShapeDtypeStruct((1, 2 * D_MODEL), F32)],
        compiler_params=_params(1),
    )(dmix, proj, b_gate.reshape(1, 2 * D_MODEL), y_ssd, y_attn)


def _bucket_np():
    i = np.arange(BLOCK)[:, None]
    j = np.arange(3 * BLOCK)[None, :]
    rel = j - BLOCK - i
    nb = N_BUCKETS // 2
    max_exact = nb // 2
    ret = np.where(rel > 0, nb, 0)
    n = np.abs(rel)
    nf = np.maximum(n, 1).astype(np.float32)
    large = max_exact + (np.log(nf / np.float32(max_exact)) / np.float32(math.log(MAX_DISTANCE / max_exact))
                         * np.float32(nb - max_exact)).astype(np.int32)
    large = np.minimum(large, nb - 1)
    bucket = ret + np.where(n < max_exact, n, large)
    band = np.abs(rel) <= WINDOW
    return bucket, band


def _onehot_np():
    bucket, _ = _bucket_np()
    oh = np.zeros((BLOCK * 3 * BLOCK, LANE), np.float32)
    oh[np.arange(oh.shape[0]), bucket.reshape(-1)] = 1.0
    return oh


REP = Q_HEADS // KV_HEADS


def _group_cols(ref, g):
    return jnp.concatenate([ref[:, (g * REP + r) * HEAD_PAD:(g * REP + r + 1) * HEAD_PAD] for r in range(REP)], axis=0)


def _group_lane(ref, g, rows):
    return jnp.concatenate([jnp.broadcast_to(ref[:, g * REP + r:g * REP + r + 1], (rows, 1)) for r in range(REP)], axis=0)


def _attn_kv(refs, g):
    gc = slice(g * HEAD_PAD, (g + 1) * HEAD_PAD)
    return _mx(jnp.concatenate([r[:, gc] for r in refs], axis=0))


def _attn_logits(q_ref, kg, bias_ref, edge, g):
    q4 = _mx(_group_cols(q_ref, g))
    s = _dot(q4, kg, ((1,), (1,))) * (HEAD_DIM ** -0.5) + bias_ref[g] + edge
    return q4, s


def _attn_edge(n, nb):
    col = lax.broadcasted_iota(jnp.int32, (1, 3 * BLOCK), 1)
    bad = ((col < BLOCK) & (n == 0)) | ((col >= 2 * BLOCK) & (n == nb - 1))
    return jnp.where(bad, NEG, 0.0)


def _onehot_lane(h):
    return (lax.broadcasted_iota(jnp.int32, (1, LANE), 1) == h).astype(F32)


def _attn_fwd_body(*refs, nb, comm):
    ins, cin, (o_ref, lse_ref), cout, _, csem = _split_refs(refs, 9, 2, comm)
    q_ref, kp_ref, kc_ref, kn_ref, vp_ref, vc_ref, vn_ref, bias_ref, sink_ref = ins
    n = pl.program_id(0)
    if comm is not None:
        comm.run(cin, cout, csem, n, nb)
    edge = _attn_edge(n, nb)
    lse = jnp.zeros((BLOCK, LANE), F32)
    logits = [_attn_logits(q_ref, _attn_kv((kp_ref, kc_ref, kn_ref), g), bias_ref, edge, g)[1] for g in range(KV_HEADS)]
    for g in range(KV_HEADS):
        vg = _attn_kv((vp_ref, vc_ref, vn_ref), g)
        s = logits[g]
        sink = _group_lane(sink_ref, g, BLOCK)
        m = jnp.maximum(jnp.max(s, axis=-1, keepdims=True), sink)
        p = jnp.exp(s - m)
        den = jnp.sum(p, axis=-1, keepdims=True) + jnp.exp(sink - m)
        o4 = _dot(_mx(p * (1.0 / den)), vg)
        l4 = m + jnp.log(den)
        for r in range(REP):
            h = g * REP + r
            o_ref[:, h * HEAD_PAD:(h + 1) * HEAD_PAD] = o4[r * BLOCK:(r + 1) * BLOCK].astype(o_ref.dtype)
            lse = lse + l4[r * BLOCK:(r + 1) * BLOCK] * _onehot_lane(h)
    lse_ref[...] = lse


def _attn_specs(t, clamp):
    nb = t // BLOCK
    cur = (lambda n: jnp.minimum(n, nb - 1)) if clamp else (lambda n: n)
    prv = lambda n: jnp.maximum(cur(n) - 1, 0)
    nxt = lambda n: jnp.minimum(cur(n) + 1, nb - 1)
    kb, vb = K0 // KP, V0 // KP
    qs = pl.BlockSpec((BLOCK, QP), lambda n: (cur(n), Q0 // QP))
    ks = [pl.BlockSpec((BLOCK, KP), lambda n, f=f: (f(n), kb)) for f in (prv, cur, nxt)]
    vs = [pl.BlockSpec((BLOCK, KP), lambda n, f=f: (f(n), vb)) for f in (prv, cur, nxt)]
    consts = [pl.BlockSpec((KV_HEADS, REP * BLOCK, 3 * BLOCK), lambda n: (0, 0, 0)), pl.BlockSpec((1, LANE), lambda n: (0, 0))]
    return nb, cur, qs, ks, vs, consts


def _attn_fwd(proj, bias, sink, comm=None):
    t = proj.shape[0]
    nb, cur, qs, ks, vs, consts = _attn_specs(t, False)
    in_specs, args = [qs] + ks + vs + consts, [proj] * 7 + [bias, sink]
    out_specs = [pl.BlockSpec((BLOCK, QP), lambda n: (n, 0)), pl.BlockSpec((BLOCK, LANE), lambda n: (n, 0))]
    out_shape = [jax.ShapeDtypeStruct((t, QP), BF16), jax.ShapeDtypeStruct((t, LANE), F32)]
    scratch = []
    if comm is not None:
        in_specs, args = in_specs + [ANY] * comm.n, args + comm.arrs
        out_specs, out_shape = out_specs + [ANY] * comm.n, out_shape + comm.out_shapes()
        scratch = comm.sems()
    return pl.pallas_call(
        functools.partial(_attn_fwd_body, nb=nb, comm=comm), name="attn_fwd", grid=(nb,),
        in_specs=in_specs, out_specs=out_specs, out_shape=out_shape, scratch_shapes=scratch, compiler_params=_params(1),
    )(*args)


def _attn_bwd_body(*refs, nb, comm):
    ins, cin, (dqkv_ref, dbias_ref, dsink_ref), cout, (ak, bk, av, bv, dq_scr), csem = _split_refs(refs, 13, 3, comm)
    q_ref, kp_ref, kc_ref, kn_ref, vp_ref, vc_ref, vn_ref, bias_ref, sink_ref, do_ref, o_ref, lse_ref = ins[:12]
    dk_ref = dqkv_ref.at[:, K0:K0 + KP]
    dv_ref = dqkv_ref.at[:, V0:V0 + KP]
    n = pl.program_id(0)
    scale = HEAD_DIM ** -0.5
    if comm is not None:
        comm.run(cin, cout, csem, n, nb + 1)

    @pl.when(n == 0)
    def _():
        for r in (ak, bk, av, bv, dq_scr, dbias_ref, dsink_ref):
            r[...] = jnp.zeros_like(r)

    dqkv_ref[:, Q0:Q0 + QP] = dq_scr[...]

    @pl.when(n < nb)
    def _():
        edge = _attn_edge(n, nb)
        dsink = jnp.zeros((1, LANE), F32)
        early = []
        for g in range(KV_HEADS):
            kg = _attn_kv((kp_ref, kc_ref, kn_ref), g)
            q4, s = _attn_logits(q_ref, kg, bias_ref, edge, g)
            do = _group_cols(do_ref, g).astype(F32)
            early.append((kg, q4, s, do, _dot(_mx(do), _attn_kv((vp_ref, vc_ref, vn_ref), g), ((1,), (1,)))))
        for g in range(KV_HEADS):
            gc = slice(g * HEAD_PAD, (g + 1) * HEAD_PAD)
            kg, q4, s, do, dp = early[g]
            lse = _group_lane(lse_ref, g, BLOCK)
            p = jnp.exp(s - lse)
            delta = jnp.sum(do * _group_cols(o_ref, g).astype(F32), axis=-1, keepdims=True)
            ds = p * (dp - delta)
            sunk = jnp.exp(_group_lane(sink_ref, g, BLOCK) - lse) * delta
            dbias_ref[g] += ds
            dsb = _mx(ds)
            dq4 = _dot(dsb, kg) * scale
            for r in range(REP):
                h = g * REP + r
                dq_scr[:, h * HEAD_PAD:(h + 1) * HEAD_PAD] = dq4[r * BLOCK:(r + 1) * BLOCK].astype(dq_scr.dtype)
                dsink = dsink - jnp.sum(sunk[r * BLOCK:(r + 1) * BLOCK], axis=0, keepdims=True) * _onehot_lane(h)
            dk_c = _dot(dsb, q4, ((0,), (0,))) * scale
            dv_c = _dot(_mx(p), _mx(do), ((0,), (0,)))
            for acc_a, acc_b, out, c in ((ak, bk, dk_ref, dk_c), (av, bv, dv_ref, dv_c)):
                out[:, gc] = (acc_a[:, gc] + c[0:BLOCK]).astype(out.dtype)
                acc_a[:, gc] = acc_b[:, gc] + c[BLOCK:2 * BLOCK]
                acc_b[:, gc] = c[2 * BLOCK:3 * BLOCK]
        dsink_ref[...] += dsink

    @pl.when(n == nb)
    def _():
        dk_ref[...] = ak[...].astype(dk_ref.dtype)
        dv_ref[...] = av[...].astype(dv_ref.dtype)


def _attn_bwd(proj, bias, sink, do, o, lse, dproj, comm=None):
    t = proj.shape[0]
    nb, cur, qs, ks, vs, consts = _attn_specs(t, True)
    rowq = pl.BlockSpec((BLOCK, QP), lambda n: (cur(n), 0))
    late = pl.BlockSpec((BLOCK, QKV), lambda n: (jnp.maximum(n - 1, 0), Q0 // QKV))
    in_specs = [qs] + ks + vs + consts + [rowq, rowq, pl.BlockSpec((BLOCK, LANE), lambda n: (cur(n), 0)), ANY]
    args = [proj] * 7 + [bias, sink, do, o, lse, dproj]
    out_specs = [late, pl.BlockSpec((KV_HEADS, REP * BLOCK, 3 * BLOCK), lambda n: (0, 0, 0)), pl.BlockSpec((1, LANE), lambda n: (0, 0))]
    out_shape = [jax.ShapeDtypeStruct((t, NP), BF16), jax.ShapeDtypeStruct((KV_HEADS, REP * BLOCK, 3 * BLOCK), F32),
                 jax.ShapeDtypeStruct((1, LANE), F32)]
    scratch = [pltpu.VMEM((BLOCK, KP), F32) for _ in range(4)] + [pltpu.VMEM((BLOCK, QP), BF16)]
    if comm is not None:
        in_specs, args = in_specs + [ANY] * comm.n, args + comm.arrs
        out_specs, out_shape = out_specs + [ANY] * comm.n, out_shape + comm.out_shapes()
        scratch = scratch + comm.sems()
    return pl.pallas_call(
        functools.partial(_attn_bwd_body, nb=nb, comm=comm), name="attn_bwd", grid=(nb + 1,),
        in_specs=in_specs, out_specs=out_specs, out_shape=out_shape, scratch_shapes=scratch, input_output_aliases={12: 0},
        compiler_params=_params(1),
    )(*args)


def _adamw_body(gp_ref, w_ref, m_ref, v_ref, g_ref, d_ref, nm_ref, nv_ref, *, nparts):
    g = gp_ref[0].astype(F32)
    for j in range(1, nparts):
        g = g + gp_ref[j].astype(F32)
    m = ADAM_B1 * m_ref[...] + (1.0 - ADAM_B1) * g
    v = ADAM_B2 * v_ref[...] + (1.0 - ADAM_B2) * (g * g)
    m_hat = m / (1.0 - ADAM_B1 ** ADAM_STEP)
    v_hat = v / (1.0 - ADAM_B2 ** ADAM_STEP)
    g_ref[...] = g
    d_ref[...] = -ADAM_LR * (m_hat / (jnp.sqrt(v_hat) + ADAM_EPS) + ADAM_WD * w_ref[...])
    nm_ref[...] = m
    nv_ref[...] = v


def _adamw_layer_body(*refs, nparts):
    gp_ref, w_ref, m_ref, v_ref = refs[:4]
    g_ref, d_ref, nm_ref, nv_ref = refs[-4:]
    g = gp_ref[0].astype(F32)
    for j in range(1, nparts):
        g = g + gp_ref[j].astype(F32)
    m = ADAM_B1 * m_ref[0] + (1.0 - ADAM_B1) * g
    v = ADAM_B2 * v_ref[0] + (1.0 - ADAM_B2) * (g * g)
    m_hat = m / (1.0 - ADAM_B1 ** ADAM_STEP)
    v_hat = v / (1.0 - ADAM_B2 ** ADAM_STEP)
    g_ref[0] = g
    d_ref[0] = -ADAM_LR * (m_hat / (jnp.sqrt(v_hat) + ADAM_EPS) + ADAM_WD * w_ref[0])
    nm_ref[0] = m
    nv_ref[0] = v


def _adamw_layer(gparts, row0, w, m, v, l, prev, name, tr=128):
    p, c = gparts.shape[0], gparts.shape[2]
    per = w.shape[1]
    tr = min(tr, per)
    assert per % tr == 0 and row0 % tr == 0 and w.shape[2] == c
    lay = pl.BlockSpec((1, tr, c), lambda i: (l, i, 0))
    in_specs = [pl.BlockSpec((p, tr, c), lambda i: (0, row0 // tr + i, 0)), lay, lay, lay]
    args = [gparts, w, m, v]
    aliases = {}
    if prev is not None:
        in_specs, args, aliases = in_specs + [ANY] * 4, args + list(prev), {4 + k: k for k in range(4)}
    sd = jax.ShapeDtypeStruct(w.shape, F32)
    return pl.pallas_call(
        functools.partial(_adamw_layer_body, nparts=p), name=name, grid=(per // tr,),
        in_specs=in_specs, out_specs=[lay] * 4, out_shape=[sd] * 4, input_output_aliases=aliases, compiler_params=_params(1),
    )(*args)


def _adamw(gparts, w, m, v, name, tr=256):
    p, r, c = gparts.shape
    tr = min(tr, r)
    assert r % tr == 0
    row = pl.BlockSpec((tr, c), lambda i: (i, 0))
    sd = jax.ShapeDtypeStruct((r, c), F32)
    return pl.pallas_call(
        functools.partial(_adamw_body, nparts=p), name=name, grid=(r // tr,),
        in_specs=[pl.BlockSpec((p, tr, c), lambda i: (0, i, 0)), row, row, row],
        out_specs=[row, row, row, row], out_shape=[sd, sd, sd, sd], compiler_params=_params(1),
    )(gparts, w, m, v)


MESH = pl.DeviceIdType.MESH
N_REL = N_DEV - 1


def _gather_phases(ins, outs, sems):
    n = len(ins)
    send_sems, recv_sems, local_sems = sems
    x, y, c = lax.axis_index("x"), lax.axis_index("y"), lax.axis_index("c")
    me, sibling = (x, y, c), (x, y, 1 - c)
    chips = [(1 - x, y), (x, 1 - y), (1 - x, 1 - y)]

    def slot(a, p):
        return outs[a].at[4 * p[0] + 2 * p[1] + p[2]]

    def copy(a, k, block, to, src=None):
        return pltpu.make_async_remote_copy(
            src_ref=slot(a, block) if src is None else src, dst_ref=slot(a, block),
            send_sem=send_sems.at[a * N_REL + k], recv_sem=recv_sems.at[a * N_REL + k], device_id=to, device_id_type=MESH)

    mine = [pltpu.make_async_copy(ins[a], slot(a, me), local_sems.at[a]) for a in range(n)]
    first = []
    for a in range(n):
        first.append(copy(a, 0, me, sibling, src=ins[a]))
        first += [copy(a, 1 + j, me, (*chip, c), src=ins[a]) for j, chip in enumerate(chips)]
    passed = [copy(a, 4 + j, (*chip, c), sibling) for j, chip in enumerate(chips) for a in range(n)]

    def start():
        for cp in mine + first:
            cp.start()

    def pass_on():
        i = 0
        for j, chip in enumerate(chips):
            for a in range(n):
                copy(a, 1 + j, (*chip, c), me).wait_recv()
                passed[i].start()
                i += 1

    def finish():
        for a in range(n):
            copy(a, 0, sibling, me).wait_recv()
            for j, chip in enumerate(chips):
                copy(a, 4 + j, (*chip, 1 - c), me).wait_recv()
        for cp in first + passed:
            cp.wait_send()
        for cp in mine:
            cp.wait()

    return start, pass_on, finish


def _exchange_phases(ins, outs, sems):
    n = len(ins)
    send_sems, recv_sems, local_sems = sems
    x, y, c = lax.axis_index("x"), lax.axis_index("y"), lax.axis_index("c")
    me = 4 * x + 2 * y + c
    mine = [pltpu.make_async_copy(ins[a].at[me], outs[a].at[me], local_sems.at[a]) for a in range(n)]
    copies = []
    for a in range(n):
        for k in range(1, N_DEV):
            px = 1 - x if k & 4 else x
            py = 1 - y if k & 2 else y
            pc = 1 - c if k & 1 else c
            peer = 4 * px + 2 * py + pc
            send = pltpu.make_async_remote_copy(
                src_ref=ins[a].at[peer], dst_ref=outs[a].at[me], send_sem=send_sems.at[a * N_REL + k - 1],
                recv_sem=recv_sems.at[a * N_REL + k - 1], device_id=(px, py, pc), device_id_type=MESH)
            recv = pltpu.make_async_remote_copy(
                src_ref=ins[a].at[me], dst_ref=outs[a].at[peer], send_sem=send_sems.at[a * N_REL + k - 1],
                recv_sem=recv_sems.at[a * N_REL + k - 1], device_id=(px, py, pc), device_id_type=MESH)
            copies.append((send, recv))

    def start():
        for cp in mine:
            cp.start()
        for send, _ in copies:
            send.start()

    def finish():
        for send, recv in copies:
            send.wait_send()
            recv.wait_recv()
        for cp in mine:
            cp.wait()

    return start, None, finish


class _Comm:
    def __init__(self, kind, arrs):
        self.kind, self.arrs, self.n = kind, list(arrs), len(arrs)

    def out_shapes(self):
        if self.kind == "gather":
            return [jax.ShapeDtypeStruct((N_DEV,) + a.shape, a.dtype) for a in self.arrs]
        return [jax.ShapeDtypeStruct(a.shape, a.dtype) for a in self.arrs]

    def sems(self):
        return [pltpu.SemaphoreType.DMA((self.n * N_REL,)), pltpu.SemaphoreType.DMA((self.n * N_REL,)),
                pltpu.SemaphoreType.DMA((self.n,))]

    def run(self, ins, outs, sems, step, nsteps):
        start, mid, finish = (_gather_phases if self.kind == "gather" else _exchange_phases)(ins, outs, sems)
        if nsteps is None:
            start()
            if mid is not None:
                mid()
            finish()
            return
        pl.when(step == 0)(start)
        if mid is not None:
            pl.when(step == (3 * nsteps) // 4)(mid)
        pl.when(step == nsteps - 1)(finish)


def _split_refs(refs, n_in, n_out, comm):
    k = comm.n if comm is not None else 0
    i = 0
    parts = []
    for cnt in (n_in, k, n_out, k):
        parts.append(refs[i:i + cnt])
        i += cnt
    rest = refs[i:]
    if comm is None:
        return parts[0], (), parts[2], (), rest, ()
    return parts[0], parts[1], parts[2], parts[3], rest[:len(rest) - 3], rest[len(rest) - 3:]


def _comm_body(*refs, comm):
    n = comm.n
    comm.run(refs[:n], refs[n:2 * n], refs[2 * n:], None, None)


def _communicate(comm, name, vmem=False):
    spec = pl.BlockSpec(memory_space=pltpu.VMEM if vmem else pl.ANY)
    return pl.pallas_call(
        functools.partial(_comm_body, comm=comm), name=name,
        in_specs=[spec] * comm.n, out_specs=[spec] * comm.n, out_shape=comm.out_shapes(), scratch_shapes=comm.sems(),
        compiler_params=pltpu.CompilerParams(vmem_limit_bytes=VMEM_LIMIT),
    )(*comm.arrs)


ANY = pl.BlockSpec(memory_space=pl.ANY)


def _pad_heads(w, axis):
    shp = w.shape
    heads = shp[axis] // HEAD_DIM
    w = w.reshape(shp[:axis] + (heads, HEAD_DIM) + shp[axis + 1:])
    pad = [(0, 0)] * w.ndim
    pad[axis + 1] = (0, HEAD_PAD - HEAD_DIM)
    w = jnp.pad(w, pad)
    return w.reshape(shp[:axis] + (heads * HEAD_PAD,) + shp[axis + 1:])


def _unpad_heads(w, axis):
    shp = w.shape
    heads = shp[axis] // HEAD_PAD
    w = w.reshape(shp[:axis] + (heads, HEAD_PAD) + shp[axis + 1:])
    w = lax.slice_in_dim(w, 0, HEAD_DIM, axis=axis + 1)
    return w.reshape(shp[:axis] + (heads * HEAD_DIM,) + shp[axis + 1:])


def _w_in_to_padded(w):
    idx = np.cumsum((0,) + IN_SPLITS)
    z, xbc, dt, q, k, v, gates = [w[:, idx[i]:idx[i + 1]] for i in range(7)]
    zeros = lambda n: jnp.zeros((w.shape[0], n), w.dtype)
    return jnp.concatenate([_pad_heads(q, 1), _pad_heads(k, 1), _pad_heads(v, 1), xbc, dt, zeros(DTW - 2 * SSD_HEADS), z, gates],
                           axis=1)


def _w_in_from_padded(w):
    z = w[:, Z0:Z0 + D_INNER]
    gates = w[:, GT0:GT0 + 2 * D_MODEL]
    q = _unpad_heads(w[:, Q0:Q0 + QP], 1)
    xbc = w[:, XBC0:XBC0 + CONV_DIM]
    k = _unpad_heads(w[:, K0:K0 + KP], 1)
    v = _unpad_heads(w[:, V0:V0 + KP], 1)
    dt = w[:, DT0:DT0 + 2 * SSD_HEADS]
    return jnp.concatenate([z, xbc, dt, q, k, v, gates], axis=1)


def _pad_lane(v):
    v = v.reshape(-1)
    return jnp.pad(v, (0, LANE - v.shape[0]))


def _layer_consts(p):
    c = dict(p)
    c["conv_w8"] = jnp.pad(p["conv_w"].reshape(SSD_CONV, CONV_DIM), ((0, 8 - SSD_CONV), (0, 0)))
    c["conv_b1"] = p["conv_b"].reshape(1, CONV_DIM)
    c["dtb"] = _pad_lane(p["dt_bias"])
    c["alog"] = _pad_lane(p["a_log"])
    c["dskip_x"] = jnp.repeat(p["d_skip"], SSD_HEAD_DIM).reshape(1, D_INNER)
    c["sink"] = _pad_lane(p["attn_sink"]).reshape(1, LANE)
    return c


def _hosted(res, n_own, hook):
    res = list(res)
    if hook is not None:
        hook[1](res[n_own:])
    return res[0] if n_own == 1 else res[:n_own]


def _layer_fwd(x, h, c, w, bias, l, hooks, next_g):
    comm = lambda k: hooks[k][0] if k in hooks else None
    s = {"x": x}
    s["h"] = _rms_fwd(x, c["pre_mix_norm"], f"pre_mix_norm_{l}", out_dtype=BF16) if h is None else h
    res = _mm(s["h"], w["w_in"], f"in_proj_{l}", out_dtype=BF16, mode="dt_side", comm=comm("in_proj"))
    proj, s["dt"] = _hosted(res, 2, hooks.get("in_proj"))
    s["proj"] = proj
    s["xbc"] = _conv_fwd(proj, c["conv_w8"], c["conv_b1"])
    s["dtt"] = _dt_rows(s["dt"])
    yf, s["stf"] = _hosted(_ssd_fwd(s["xbc"], s["dt"], s["dtt"], c["dtb"], c["alog"], False, comm=comm("ssd_fwd")), 2,
                                hooks.get("ssd_fwd"))
    s["y"], s["stb"] = _hosted(_ssd_fwd(s["xbc"], s["dt"], s["dtt"], c["dtb"], c["alog"], True, comm=comm("ssd_fwd_rev"), add=yf), 2,
                                hooks.get("ssd_fwd_rev"))
    s["u"] = _gnorm_fwd(s["y"], s["xbc"], proj, c["dskip_x"], c["ssd_norm"])
    s["y_ssd"] = _mm(s["u"], w["w_ssd_out"], f"ssd_out_{l}")
    s["o"], s["lse"] = _hosted(_attn_fwd(proj, bias, c["sink"], comm=comm("attn_fwd")), 2, hooks.get("attn_fwd"))
    s["y_attn"] = _mm(s["o"], w["w_attn_out"], f"attn_out_{l}")
    s["mix"] = _gate_fwd(proj, c["b_gate"], s["y_ssd"], s["y_attn"])
    s["mixed"] = _mm(s["mix"], w["w_o"], f"w_o_{l}")
    s["x1"], s["h2"] = _rms_fwd(s["mixed"], c["post_mix_norm"], f"post_mix_norm_{l}", res=x, next_g=c["pre_mlp_norm"])
    s["f1"], s["a"] = _mm(s["h2"], w["w_mlp_in"], f"mlp_in_{l}", mode="relu2")
    s["f"] = _mm(s["a"], w["w_mlp_out"], f"mlp_out_{l}")
    if next_g is None:
        return _rms_fwd(s["f"], c["post_mlp_norm"], f"post_mlp_norm_{l}", res=s["x1"]), None, s
    x2, h_next = _rms_fwd(s["f"], c["post_mlp_norm"], f"post_mlp_norm_{l}", res=s["x1"], next_g=next_g)
    return x2, h_next, s


def _send_rest(gw):
    s_mi = gw["w_mlp_in"].reshape(D_MODEL, N_DEV, -1).transpose(1, 0, 2).astype(BF16)
    parts = [(_unpad_heads(gw[n], 0) if n == "w_attn_out" else gw[n]).reshape(N_DEV, -1, D_MODEL) for n in ROWS]
    return [s_mi, jnp.concatenate(parts, axis=1).astype(BF16)]


def _send_w_in(g):
    return [_w_in_from_padded(g).reshape(D_MODEL, N_DEV, -1).transpose(1, 0, 2).astype(BF16)]


def _layer_bwd(dx2, s, c, w, bias, l, earlier=None, host_own=False):
    gw, gs, got = {}, {}, {}
    df, gs["post_mlp_norm"] = _rms_bwd(s["f"], c["post_mlp_norm"], dx2, f"post_mlp_norm_bwd_{l}", out_dtype=BF16)
    df1 = _mm(df, w["w_mlp_out"], f"mlp_out_dx_{l}", tb=True, mode="relu2_bwd", extra=s["f1"], out_dtype=BF16)
    gw["w_mlp_out"] = _mm(s["a"], df, f"mlp_out_dw_{l}", ta=True, out_dtype=BF16)
    gw["w_mlp_in"] = _mm(s["h2"], df1, f"mlp_in_dw_{l}", ta=True, out_dtype=BF16)
    dh2 = _mm(df1, w["w_mlp_in"], f"mlp_in_dx_{l}", tb=True, out_dtype=BF16)
    dx1, gs["pre_mlp_norm"] = _rms_bwd(s["x1"], c["pre_mlp_norm"], dh2, f"pre_mlp_norm_bwd_{l}", add=dx2)
    dmixed, gs["post_mix_norm"] = _rms_bwd(s["mixed"], c["post_mix_norm"], dx1, f"post_mix_norm_bwd_{l}", out_dtype=BF16)
    gw["w_o"] = _mm(s["mix"], dmixed, f"w_o_dw_{l}", ta=True, out_dtype=BF16)
    dmix = _mm(dmixed, w["w_o"], f"w_o_dx_{l}", tb=True, out_dtype=BF16)
    dys, dya, dproj, gs["b_gate"] = _gate_bwd(dmix, s["proj"], c["b_gate"], s["y_ssd"], s["y_attn"])
    gw["w_ssd_out"] = _mm(s["u"], dys, f"ssd_out_dw_{l}", ta=True, out_dtype=BF16)
    du = _mm(dys, w["w_ssd_out"], f"ssd_out_dx_{l}", tb=True, out_dtype=BF16)
    gw["w_attn_out"] = _mm(s["o"], dya, f"attn_out_dw_{l}", ta=True, out_dtype=BF16)
    do = _mm(dya, w["w_attn_out"], f"attn_out_dx_{l}", tb=True, out_dtype=BF16)
    res = _attn_bwd(s["proj"], bias, c["sink"], do, s["o"], s["lse"], dproj,
                    comm=None if earlier is None else _Comm("exchange", earlier[:1]))
    (dproj, dbias, dsink), got["earlier"] = res[:3], list(res[3:])
    gs["attn_sink"] = dsink
    dy, dproj, gs["ssd_norm"], gs["d_skip"] = _gnorm_bwd(du, s["y"], s["xbc"], s["proj"], c["dskip_x"], c["ssd_norm"],
                                                         dproj)
    res = _ssd_bwd(s["xbc"], s["dt"], s["dtt"], c["dtb"], c["alog"], dy, s["stf"], c["dskip_x"], False,
                   comm=None if earlier is None else _Comm("exchange", earlier[1:]))
    part, got["earlier"] = res[:3], got["earlier"] + list(res[3:])
    res = _ssd_bwd(s["xbc"], s["dt"], s["dtt"], c["dtb"], c["alog"], dy, s["stb"], c["dskip_x"], True, prev=part, dproj=dproj,
                   comm=_Comm("exchange", _send_rest(gw)) if host_own else None)
    (dxbc, dproj, sm), got["rest"] = res[:3], res[3:]
    gs["a_log"], gs["dt_bias"] = sm[0:1], sm[1:2]
    dproj, gs["conv"] = _conv_bwd(s["proj"], dxbc, c["conv_w8"], c["conv_b1"], dproj)
    gw["w_in"] = _mm(s["h"], dproj, f"in_proj_dw_{l}", ta=True, out_dtype=BF16)
    if host_own:
        dh, *got["w_in"] = _mm(dproj, w["w_in"], f"in_proj_dx_{l}", tb=True, out_dtype=BF16, comm=_Comm("exchange", _send_w_in(gw["w_in"])))
    else:
        dh = _mm(dproj, w["w_in"], f"in_proj_dx_{l}", tb=True, out_dtype=BF16)
    dx, gs["pre_mix_norm"] = _rms_bwd(s["x"], c["pre_mix_norm"], dh, f"pre_mix_norm_bwd_{l}", add=dx1)
    return dx, gw, gs, dbias, got


SMALL = (("pre_mix_norm", DEPTH * D_MODEL), ("b_gate", DEPTH * 2 * D_MODEL), ("conv_w", DEPTH * SSD_CONV * CONV_DIM),
         ("conv_b", DEPTH * CONV_DIM), ("dt_bias", DEPTH * 2 * SSD_HEADS), ("a_log", DEPTH * 2 * SSD_HEADS),
         ("d_skip", DEPTH * SSD_HEADS), ("ssd_norm", DEPTH * D_INNER), ("attn_sink", DEPTH * Q_HEADS),
         ("rel_bias_table", N_BUCKETS * Q_HEADS), ("post_mix_norm", DEPTH * D_MODEL), ("pre_mlp_norm", DEPTH * D_MODEL),
         ("post_mlp_norm", DEPTH * D_MODEL))


def _small_rows(n):
    return -(-n // LANE)


def _pack_small(vals, fill=0.0):
    rows = []
    for name, n in SMALL:
        v = vals[name].reshape(-1).astype(F32)
        rows.append(jnp.pad(v, (0, _small_rows(n) * LANE - n), constant_values=fill).reshape(-1, LANE))
    out = jnp.concatenate(rows, axis=0)
    return jnp.pad(out, ((0, -out.shape[0] % 8), (0, 0)), constant_values=fill)


def _unpack_small(packed, shapes):
    out, r = {}, 0
    for name, n in SMALL:
        nr = _small_rows(n)
        out[name] = packed[r:r + nr].reshape(-1)[:n].reshape(shapes[name])
        r += nr
    return out


BIG = ("w_in", "w_ssd_out", "w_attn_out", "w_o", "w_mlp_in", "w_mlp_out")
ROWS = ("w_ssd_out", "w_attn_out", "w_o", "w_mlp_out")


def _pack_rows(vals, lead):
    return jnp.concatenate([vals[n].reshape(lead + (-1, D_MODEL)) for n in ROWS], axis=len(lead))


def kernel(x, pre_mix_norm, w_in, b_gate, conv_w, conv_b, dt_bias, a_log, d_skip, ssd_norm, w_ssd_out, attn_sink, rel_bias_table, w_attn_out, w_o, post_mix_norm, pre_mlp_norm, w_mlp_in, w_mlp_out, post_mlp_norm, loss_target, m_pre_mix_norm, m_w_in, m_b_gate, m_conv_w, m_conv_b, m_dt_bias, m_a_log, m_d_skip, m_ssd_norm, m_w_ssd_out, m_attn_sink, m_rel_bias_table, m_w_attn_out, m_w_o, m_post_mix_norm, m_pre_mlp_norm, m_w_mlp_in, m_w_mlp_out, m_post_mlp_norm, v_pre_mix_norm, v_w_in, v_b_gate, v_conv_w, v_conv_b, v_dt_bias, v_a_log, v_d_skip, v_ssd_norm, v_w_ssd_out, v_attn_sink, v_rel_bias_table, v_w_attn_out, v_w_o, v_post_mix_norm, v_pre_mlp_norm, v_w_mlp_in, v_w_mlp_out, v_post_mlp_norm):
    names = ("pre_mix_norm", "w_in", "b_gate", "conv_w", "conv_b", "dt_bias", "a_log", "d_skip", "ssd_norm", "w_ssd_out",
             "attn_sink", "rel_bias_table", "w_attn_out", "w_o", "post_mix_norm", "pre_mlp_norm", "w_mlp_in", "w_mlp_out",
             "post_mlp_norm")
    W = dict(zip(names, (pre_mix_norm, w_in, b_gate, conv_w, conv_b, dt_bias, a_log, d_skip, ssd_norm, w_ssd_out, attn_sink,
                         rel_bias_table, w_attn_out, w_o, post_mix_norm, pre_mlp_norm, w_mlp_in, w_mlp_out, post_mlp_norm)))
    M = dict(zip(names, (m_pre_mix_norm, m_w_in, m_b_gate, m_conv_w, m_conv_b, m_dt_bias, m_a_log, m_d_skip, m_ssd_norm,
                         m_w_ssd_out, m_attn_sink, m_rel_bias_table, m_w_attn_out, m_w_o, m_post_mix_norm, m_pre_mlp_norm,
                         m_w_mlp_in, m_w_mlp_out, m_post_mlp_norm)))
    V = dict(zip(names, (v_pre_mix_norm, v_w_in, v_b_gate, v_conv_w, v_conv_b, v_dt_bias, v_a_log, v_d_skip, v_ssd_norm,
                         v_w_ssd_out, v_attn_sink, v_rel_bias_table, v_w_attn_out, v_w_o, v_post_mix_norm, v_pre_mlp_norm,
                         v_w_mlp_in, v_w_mlp_out, v_post_mlp_norm)))
    t = x.shape[1]
    shard = {n: W[n].shape for n in names}
    dev = 4 * lax.axis_index("x") + 2 * lax.axis_index("y") + lax.axis_index("c")
    cshard = CONV_DIM // N_DEV

    sh_in = [w_in[l].astype(BF16) for l in range(DEPTH)]
    sh_rest = [[w_mlp_in[l].astype(BF16), _pack_rows({n: W[n][l].astype(BF16) for n in ROWS}, ())] for l in range(DEPTH)]
    full = [{} for _ in range(DEPTH)]

    def take_w_in(l):
        def take(got):
            full[l]["w_in"] = _w_in_to_padded(got[0].transpose(1, 0, 2).reshape(D_MODEL, N_IN))
        return take

    def take_mlp_in(l):
        def take(got):
            full[l]["w_mlp_in"] = got[0].transpose(1, 0, 2).reshape(D_MODEL, D_FF)
        return take

    def take_rows(l):
        def take(got):
            r = 0
            for n in ROWS:
                per = shard[n][1]
                full[l][n] = got[0][:, r:r + per].reshape(N_DEV * per, D_MODEL)
                r += per
            full[l]["w_attn_out"] = _pad_heads(full[l]["w_attn_out"], 0)
        return take

    def take_rest(l):
        def take(got):
            take_mlp_in(l)(got[:1])
            take_rows(l)(got[1:])
        return take

    take_w_in(0)(_communicate(_Comm("gather", [sh_in[0]]), "gather_w_in_0"))
    (g_conv,) = _communicate(_Comm("gather", [conv_w.reshape(-1, LANE)]), "gather_conv_w", vmem=True)
    conv_full = g_conv.reshape(N_DEV, DEPTH, SSD_CONV, cshard).transpose(1, 2, 0, 3).reshape(DEPTH, SSD_CONV, CONV_DIM)
    hooks = [{"in_proj": (_Comm("gather", sh_rest[0]), take_rest(0)),
              "ssd_fwd": (_Comm("gather", sh_rest[1][1:]), take_rows(1)),
              "ssd_fwd_rev": (_Comm("gather", sh_rest[1][:1]), take_mlp_in(1)),
              "attn_fwd": (_Comm("gather", [sh_in[1]]), take_w_in(1))}, {}]

    _, band = _bucket_np()
    onehot = jnp.asarray(_onehot_np(), BF16)
    table_t = jnp.pad(rel_bias_table.T, ((0, 0), (0, LANE - N_BUCKETS)))
    bias = _mm(table_t, onehot, "t5_bias", tb=True, tm=Q_HEADS, tn=3 * BLOCK * BLOCK // 8, split_a=True)
    bias = (bias.reshape(Q_HEADS, BLOCK, 3 * BLOCK) + jnp.asarray(np.where(band, 0.0, NEG), F32)).reshape(
        KV_HEADS, REP * BLOCK, 3 * BLOCK)

    consts = []
    for l in range(DEPTH):
        p = {n: W[n][l] for n in names if n not in BIG and n not in ("rel_bias_table", "conv_w")}
        p["conv_w"] = conv_full[l]
        consts.append(_layer_consts(p))
    act, normed = x[0], None
    saved = []
    for l in range(DEPTH):
        next_g = consts[l + 1]["pre_mix_norm"] if l + 1 < DEPTH else None
        act, normed, s = _layer_fwd(act, normed, consts[l], full[l], bias, l, hooks[l], next_g)
        saved.append(s)
    dy, lsum = _loss(act, loss_target[0])
    loss = lax.psum(0.5 / D_MODEL * jnp.sum(lsum), ("x", "y", "c"))

    gss, dbs = [None] * DEPTH, [None] * DEPTH
    dy, gw1, gss[1], dbs[1], _ = _layer_bwd(dy, saved[1], consts[1], full[1], bias, 1)
    dy, _, gss[0], dbs[0], got = _layer_bwd(dy, saved[0], consts[0], full[0], bias, 0,
                                            earlier=_send_w_in(gw1["w_in"]) + _send_rest(gw1), host_own=True)
    recv = [list(got["w_in"]) + list(got["rest"]), list(got["earlier"])]
    grad_x = dy[None]
    dbias = jnp.concatenate([d.reshape(Q_HEADS, -1) for d in dbs], axis=1)
    d_table = _mm(dbias, jnp.concatenate([onehot] * DEPTH, axis=0), "t5_bias_bwd", tm=Q_HEADS, tk=3 * BLOCK * BLOCK // 8,
                  split_a=True)

    sg = {}
    for n in ("pre_mix_norm", "b_gate", "ssd_norm", "post_mix_norm", "pre_mlp_norm", "post_mlp_norm"):
        sg[n] = jnp.stack([gss[l][n].reshape(-1) for l in range(DEPTH)])
    sg["conv_w"] = jnp.stack([gss[l]["conv"][0:SSD_CONV] for l in range(DEPTH)])
    sg["conv_b"] = jnp.stack([gss[l]["conv"][SSD_CONV] for l in range(DEPTH)])
    sg["dt_bias"] = jnp.stack([gss[l]["dt_bias"][0, 0:2 * SSD_HEADS] for l in range(DEPTH)])
    sg["a_log"] = jnp.stack([gss[l]["a_log"][0, 0:2 * SSD_HEADS] for l in range(DEPTH)])
    sg["d_skip"] = jnp.stack([gss[l]["d_skip"][0, 0:SSD_HEADS] for l in range(DEPTH)])
    sg["attn_sink"] = jnp.stack([gss[l]["attn_sink"][0, 0:Q_HEADS] for l in range(DEPTH)])
    sg["rel_bias_table"] = d_table[:, 0:N_BUCKETS].T
    (small_parts,) = _communicate(_Comm("gather", [_pack_small(sg)]), "gather_small_grads", vmem=True)
    sshape = {n: W[n].shape for n, _ in SMALL}
    sshape["conv_w"] = (DEPTH, SSD_CONV, 1, CONV_DIM)
    pk = lambda d, fill: _pack_small({n: (jnp.full((DEPTH, SSD_CONV, CONV_DIM), fill, F32) if n == "conv_w" else d[n])
                                      for n, _ in SMALL}, fill)
    s_out = [_unpack_small(o, sshape) for o in _adamw(small_parts, pk(W, 1.0), pk(M, 1.0), pk(V, 1.0), "adamw_small", tr=1024)]
    conv_g = lax.dynamic_slice_in_dim(s_out[0]["conv_w"], dev * cshard, cshard, axis=3)
    c_out = _adamw(conv_g.reshape(1, DEPTH * SSD_CONV, cshard), conv_w.reshape(-1, cshard), m_conv_w.reshape(-1, cshard),
                   v_conv_w.reshape(-1, cshard), "adamw_conv_w", tr=DEPTH * SSD_CONV)
    for i in range(4):
        s_out[i]["conv_w"] = c_out[i].reshape(conv_w.shape)

    b_out = {n: None for n in BIG}
    for l in (1, 0):
        r_in, r_mi, r_rows = recv[l]
        b_out["w_in"] = _adamw_layer(r_in, 0, w_in, m_w_in, v_w_in, l, b_out["w_in"], f"adamw_w_in_{l}", tr=256)
        b_out["w_mlp_in"] = _adamw_layer(r_mi, 0, w_mlp_in, m_w_mlp_in, v_w_mlp_in, l, b_out["w_mlp_in"], f"adamw_w_mlp_in_{l}",
                                         tr=256)
        r = 0
        for n in ROWS:
            b_out[n] = _adamw_layer(r_rows, r, W[n], M[n], V[n], l, b_out[n], f"adamw_{n}_{l}")
            r += shard[n][1]

    outs = [loss, grad_x]
    for i in range(4):
        for n in names:
            outs.append(b_out[n][i] if n in BIG else s_out[i][n])
    return tuple(outs)
```

```python
import functools
import math

import numpy as np
import jax
import jax.numpy as jnp
from jax import lax
from jax.experimental import pallas as pl
from jax.experimental.pallas import tpu as pltpu

F32 = jnp.float32
BF16 = jnp.bfloat16

D_MODEL = 1024
DEPTH = 2
D_INNER = 2048
SSD_HEADS = 32
SSD_HEAD_DIM = 64
SSD_GROUPS = 8
SSD_REP = 4
SSD_STATE = 128
SSD_CONV = 5
CHUNK = 128
CONV_DIM = 4096
Q_HEADS = 16
KV_HEADS = 4
HEAD_DIM = 64
ATTN_WIDTH = 1024
KV_WIDTH = 256
WINDOW = 128
BLOCK = 128
N_BUCKETS = 32
MAX_DISTANCE = 128
D_FF = 4096
EPS = 1e-6
IN_SPLITS = (D_INNER, CONV_DIM, 2 * SSD_HEADS, ATTN_WIDTH, KV_WIDTH, KV_WIDTH, 2 * D_MODEL)
N_IN = sum(IN_SPLITS)
N_DEV = 8

ADAM_LR = 0.001
ADAM_B1 = 0.9
ADAM_B2 = 0.999
ADAM_EPS = 1e-08
ADAM_WD = 0.01
ADAM_STEP = 10

LANE = 128
HEAD_PAD = LANE
QP = Q_HEADS * HEAD_PAD
KP = KV_HEADS * HEAD_PAD
QKV = QP + 2 * KP
Q0 = 0
K0 = QP
V0 = QP + KP
XBC0 = QKV
DT0 = XBC0 + CONV_DIM
DTW = 1024
Z0 = DT0 + DTW
GT0 = Z0 + D_INNER
NP = GT0 + 2 * D_MODEL
NEG = -1e30
VMEM_LIMIT = 56 * 1024 * 1024


def _params(n_grid):
    return pltpu.CompilerParams(dimension_semantics=("arbitrary",) * n_grid, vmem_limit_bytes=VMEM_LIMIT)


def _dot(a, b, dims=((1,), (0,))):
    return lax.dot_general(a, b, (dims, ((), ())), preferred_element_type=F32)


def _mx(a):
    return a.astype(BF16)


def _split_dot(x, e):
    hi = x.astype(BF16)
    lo = (x - hi.astype(F32)).astype(BF16)
    return _dot(hi, e) + _dot(lo, e)


def _softplus(x):
    u = jnp.exp(-jnp.abs(x))
    w = 1.0 + u
    log1p = jnp.where(w == 1.0, u, jnp.log(w) * u / jnp.where(w == 1.0, 1.0, w - 1.0))
    return jnp.maximum(x, 0.0) + log1p


def _sigmoid(x):
    return 1.0 / (1.0 + jnp.exp(-x))


def _mm_body(*refs, grid, ta, tb, split_a, mode, comm):
    n_in = 3 if mode == "relu2_bwd" else 2
    n_out = 2 if mode in ("relu2", "dt_side") else 1
    ins, cin, outs, cout, (acc_ref,), csem = _split_refs(refs, n_in, n_out, comm)
    a_ref, b_ref = ins[0], ins[1]
    i, j, k = pl.program_id(0), pl.program_id(1), pl.program_id(2)
    nk = grid[2]
    if comm is not None:
        comm.run(cin, cout, csem, (i * grid[1] + j) * nk + k, grid[0] * grid[1] * nk)

    dims = ((0 if ta else 1,), (1 if tb else 0,))
    b = _mx(b_ref[...])
    if split_a:
        a32 = a_ref[...].astype(F32)
        hi = a32.astype(BF16)
        lo = (a32 - hi.astype(F32)).astype(BF16)
        part = _dot(hi, b, dims) + _dot(lo, b, dims)
    else:
        part = _dot(_mx(a_ref[...]), b, dims)

    if nk > 1:
        @pl.when(k == 0)
        def _():
            acc_ref[...] = jnp.zeros_like(acc_ref)

        acc_ref[...] += part

    @pl.when(k == nk - 1)
    def _():
        acc = part if nk == 1 else acc_ref[...]
        if mode == "relu2":
            outs[0][...] = acc.astype(outs[0].dtype)
            r = jnp.maximum(acc, 0.0)
            outs[1][...] = (r * r).astype(outs[1].dtype)
        elif mode == "relu2_bwd":
            outs[0][...] = (acc * 2.0 * jnp.maximum(ins[2][...].astype(F32), 0.0)).astype(outs[0].dtype)
        else:
            outs[0][...] = acc.astype(outs[0].dtype)
            if mode == "dt_side":
                @pl.when(j == DT0 // outs[0].shape[1])
                def _():
                    off = DT0 % outs[0].shape[1]
                    outs[1][...] = acc[:, off:off + LANE]


def _tile(n, pref):
    t = min(n, pref)
    while n % t:
        t -= LANE
    assert t > 0 and n % t == 0, (n, pref)
    return t


MM_VMEM_BUDGET = 42 * 1024 * 1024


def _mm_tiles(m, n, k, a_bytes, b_bytes, out_bytes, extra_bytes):
    tm = _tile(m, 1024)
    best = None
    for tn in (2048, 1024):
        if n % tn:
            continue
        for nk in range(1, k // LANE + 1):
            if k % nk or (k // nk) % LANE:
                continue
            tk = k // nk
            need = 2 * (tm * tk * a_bytes + tk * tn * b_bytes + tm * tn * (out_bytes + extra_bytes)) + (tm * tn * 4 if nk > 1 else 0)
            if need <= MM_VMEM_BUDGET:
                if best is None or nk < best[0]:
                    best = (nk, tn, tk)
                break
    assert best is not None, (m, n, k)
    return tm, best[1], best[2]


def _mm(a, b, name, ta=False, tb=False, out_dtype=F32, tm=None, tn=None, tk=None, split_a=False, mode=None, extra=None,
        comm=None):
    m, k = (a.shape[1], a.shape[0]) if ta else a.shape
    n = b.shape[0] if tb else b.shape[1]
    assert (b.shape[1] if tb else b.shape[0]) == k
    if tm is None:
        side = {"relu2": 2, "relu2_bwd": extra.dtype.itemsize if extra is not None else 0}.get(mode, 0)
        tm, tn, tk = _mm_tiles(m, n, k, a.dtype.itemsize, b.dtype.itemsize, jnp.dtype(out_dtype).itemsize, side)
    else:
        tm, tn, tk = _tile(m, tm), _tile(n, tn or 2048), _tile(k, tk or 1024)
    grid = (m // tm, n // tn, k // tk)
    a_spec = pl.BlockSpec((tk, tm), lambda i, j, kk: (kk, i)) if ta else pl.BlockSpec((tm, tk), lambda i, j, kk: (i, kk))
    b_spec = pl.BlockSpec((tn, tk), lambda i, j, kk: (j, kk)) if tb else pl.BlockSpec((tk, tn), lambda i, j, kk: (kk, j))
    o_spec = pl.BlockSpec((tm, tn), lambda i, j, kk: (i, j))
    in_specs, args = [a_spec, b_spec], [a, b]
    out_specs, out_shape = [o_spec], [jax.ShapeDtypeStruct((m, n), out_dtype)]
    if mode == "relu2":
        out_specs, out_shape = [o_spec, o_spec], [jax.ShapeDtypeStruct((m, n), BF16), jax.ShapeDtypeStruct((m, n), BF16)]
    elif mode == "dt_side":
        assert DT0 % tn + LANE <= tn
        out_specs = [o_spec, pl.BlockSpec((tm, LANE), lambda i, j, kk: (i, 0))]
        out_shape = out_shape + [jax.ShapeDtypeStruct((m, LANE), F32)]
    elif mode == "relu2_bwd":
        in_specs, args = in_specs + [o_spec], args + [extra]
    scratch = [pltpu.VMEM((tm, tn) if grid[2] > 1 else (8, LANE), F32)]
    if comm is not None:
        in_specs, args = in_specs + [ANY] * comm.n, args + comm.arrs
        out_specs, out_shape = out_specs + [ANY] * comm.n, out_shape + comm.out_shapes()
        scratch = scratch + comm.sems()
    res = pl.pallas_call(
        functools.partial(_mm_body, grid=grid, ta=ta, tb=tb, split_a=split_a, mode=mode, comm=comm),
        name=name, grid=grid, in_specs=in_specs, out_specs=out_specs, out_shape=out_shape, scratch_shapes=scratch,
        compiler_params=_params(3),
    )(*args)
    return res[0] if len(res) == 1 else res


def _rms_fwd_body(*refs, has_res, has_next):
    x_ref, g_ref = refs[:2]
    x = x_ref[...]
    y = x * lax.rsqrt(jnp.mean(x * x, axis=-1, keepdims=True) + EPS) * g_ref[...]
    if has_res:
        y = refs[2][...] + y
    if has_next:
        o_ref, h_ref = refs[-2:]
        g2_ref = refs[3 if has_res else 2]
        h_ref[...] = (y * lax.rsqrt(jnp.mean(y * y, axis=-1, keepdims=True) + EPS) * g2_ref[...]).astype(h_ref.dtype)
    else:
        o_ref = refs[-1]
    o_ref[...] = y.astype(o_ref.dtype)


def _rms_fwd(x, g, name, res=None, out_dtype=F32, next_g=None, tr=512):
    t, d = x.shape
    tr = min(tr, t)
    row = pl.BlockSpec((tr, d), lambda i: (i, 0))
    vec = pl.BlockSpec((1, d), lambda i: (0, 0))
    args = [x, g.reshape(1, d)] + ([res] if res is not None else []) + ([next_g.reshape(1, d)] if next_g is not None else [])
    out_specs, out_shape = [row], [jax.ShapeDtypeStruct((t, d), out_dtype)]
    if next_g is not None:
        out_specs, out_shape = [row, row], out_shape + [jax.ShapeDtypeStruct((t, d), BF16)]
    res_ = pl.pallas_call(
        functools.partial(_rms_fwd_body, has_res=res is not None, has_next=next_g is not None),
        name=name,
        grid=(t // tr,),
        in_specs=[row, vec] + ([row] if res is not None else []) + ([vec] if next_g is not None else []),
        out_specs=out_specs,
        out_shape=out_shape,
        compiler_params=_params(1),
    )(*args)
    return res_ if next_g is not None else res_[0]


def _rms_bwd_body(*refs, has_add):
    if has_add:
        x_ref, g_ref, dy_ref, add_ref, dx_ref, dg_ref = refs
    else:
        x_ref, g_ref, dy_ref, dx_ref, dg_ref = refs

    @pl.when(pl.program_id(0) == 0)
    def _():
        dg_ref[...] = jnp.zeros_like(dg_ref)

    x = x_ref[...]
    dy = dy_ref[...].astype(F32)
    rstd = lax.rsqrt(jnp.mean(x * x, axis=-1, keepdims=True) + EPS)
    n = x * rstd
    dn = dy * g_ref[...]
    dx = rstd * (dn - n * jnp.mean(dn * n, axis=-1, keepdims=True))
    if has_add:
        dx = dx + add_ref[...]
    dx_ref[...] = dx.astype(dx_ref.dtype)
    dg_ref[...] += jnp.sum(dy * n, axis=0, keepdims=True)


def _rms_bwd(x, g, dy, name, add=None, out_dtype=F32, tr=512):
    t, d = x.shape
    tr = min(tr, t)
    row = pl.BlockSpec((tr, d), lambda i: (i, 0))
    vec = pl.BlockSpec((1, d), lambda i: (0, 0))
    args = [x, g.reshape(1, d), dy] + ([add] if add is not None else [])
    return pl.pallas_call(
        functools.partial(_rms_bwd_body, has_add=add is not None),
        name=name,
        grid=(t // tr,),
        in_specs=[row, vec, row] + ([row] if add is not None else []),
        out_specs=[row, vec],
        out_shape=[jax.ShapeDtypeStruct((t, d), out_dtype), jax.ShapeDtypeStruct((1, d), F32)],
        compiler_params=_params(1),
    )(*args)


def _loss_body(y_ref, t_ref, dy_ref, l_ref):
    @pl.when(pl.program_id(0) == 0)
    def _():
        l_ref[...] = jnp.zeros_like(l_ref)

    e = y_ref[...] - t_ref[...]
    dy_ref[...] = e * (1.0 / D_MODEL)
    s = jnp.sum(e * e, axis=0, keepdims=True)
    acc = s[:, 0:LANE]
    for j in range(1, D_MODEL // LANE):
        acc = acc + s[:, j * LANE:(j + 1) * LANE]
    l_ref[...] += acc


def _loss(y, target, tr=512):
    t, d = y.shape
    tr = min(tr, t)
    row = pl.BlockSpec((tr, d), lambda i: (i, 0))
    return pl.pallas_call(
        functools.partial(_loss_body), name="loss_head", grid=(t // tr,), in_specs=[row, row],
        out_specs=[row, pl.BlockSpec((1, LANE), lambda i: (0, 0))],
        out_shape=[jax.ShapeDtypeStruct((t, d), F32), jax.ShapeDtypeStruct((1, LANE), F32)],
        compiler_params=_params(1),
    )(y, target)


CONV_CT = 128
CONV_RB = 512
CONV_PAD = 8


def _conv_rows(t):
    return min(CONV_RB, t)


def _conv_fwd_body(u_ref, w_ref, b_ref, o_ref, pad_ref, *, t):
    rb = _conv_rows(t)
    zeros = jnp.zeros((CONV_PAD, CONV_CT), F32)
    pad_ref[pl.ds(0, CONV_PAD), :] = zeros
    pad_ref[pl.ds(t + CONV_PAD, CONV_PAD), :] = zeros
    pad_ref[pl.ds(CONV_PAD, t), :] = u_ref[...].astype(F32)
    w = w_ref[...]
    b = b_ref[...]
    for c in range(t // rb):
        base = CONV_PAD + c * rb
        acc = b + w[0:1, :] * pad_ref[pl.ds(base - 2, rb), :]
        for k in range(1, SSD_CONV):
            acc = acc + w[k:k + 1, :] * pad_ref[pl.ds(base + k - 2, rb), :]
        o_ref[pl.ds(c * rb, rb), :] = acc * _sigmoid(acc)


def _conv_fwd(proj, conv_w8, conv_b):
    t = proj.shape[0]
    off = XBC0 // CONV_CT
    return pl.pallas_call(
        functools.partial(_conv_fwd_body, t=t), name="conv_fwd", grid=(CONV_DIM // CONV_CT,),
        in_specs=[pl.BlockSpec((t, CONV_CT), lambda i: (0, off + i)), pl.BlockSpec((8, CONV_CT), lambda i: (0, i)),
                  pl.BlockSpec((1, CONV_CT), lambda i: (0, i))],
        out_specs=pl.BlockSpec((t, CONV_CT), lambda i: (0, i)),
        out_shape=jax.ShapeDtypeStruct((t, CONV_DIM), F32),
        scratch_shapes=[pltpu.VMEM((t + 2 * CONV_PAD, CONV_CT), F32)],
        compiler_params=_params(1),
    )(proj, conv_w8, conv_b)


def _conv_bwd_body(u_ref, da_ref, w_ref, b_ref, _, du_ref, dw_ref, pad_ref, pad2_ref, *, t):
    rb = _conv_rows(t)
    zeros = jnp.zeros((CONV_PAD, CONV_CT), F32)
    for p in (pad_ref, pad2_ref):
        p[pl.ds(0, CONV_PAD), :] = zeros
        p[pl.ds(t + CONV_PAD, CONV_PAD), :] = zeros
    pad_ref[pl.ds(CONV_PAD, t), :] = u_ref[...].astype(F32)
    w = w_ref[...]
    b = b_ref[...]
    dw = [jnp.zeros((1, CONV_CT), F32) for _ in range(SSD_CONV + 1)]
    for c in range(t // rb):
        base = CONV_PAD + c * rb
        us = [pad_ref[pl.ds(base + k - 2, rb), :] for k in range(SSD_CONV)]
        acc = b + w[0:1, :] * us[0]
        for k in range(1, SSD_CONV):
            acc = acc + w[k:k + 1, :] * us[k]
        sg = _sigmoid(acc)
        dyc = da_ref[pl.ds(c * rb, rb), :] * (sg * (1.0 + acc * (1.0 - sg)))
        pad2_ref[pl.ds(base, rb), :] = dyc
        for k in range(SSD_CONV):
            dw[k] = dw[k] + jnp.sum(dyc * us[k], axis=0, keepdims=True)
        dw[SSD_CONV] = dw[SSD_CONV] + jnp.sum(dyc, axis=0, keepdims=True)
    for c in range(t // rb):
        base = CONV_PAD + c * rb
        acc = w[0:1, :] * pad2_ref[pl.ds(base + 2, rb), :]
        for k in range(1, SSD_CONV):
            acc = acc + w[k:k + 1, :] * pad2_ref[pl.ds(base + 2 - k, rb), :]
        du_ref[pl.ds(c * rb, rb), :] = acc.astype(du_ref.dtype)
    dw_ref[...] = jnp.concatenate(dw + [jnp.zeros((2, CONV_CT), F32)], axis=0)


def _conv_bwd(proj, dact, conv_w8, conv_b, dproj):
    t = proj.shape[0]
    off = XBC0 // CONV_CT
    col = pl.BlockSpec((t, CONV_CT), lambda i: (0, i))
    seg = pl.BlockSpec((t, CONV_CT), lambda i: (0, off + i))
    return pl.pallas_call(
        functools.partial(_conv_bwd_body, t=t), name="conv_bwd", grid=(CONV_DIM // CONV_CT,),
        in_specs=[seg, col, pl.BlockSpec((8, CONV_CT), lambda i: (0, i)), pl.BlockSpec((1, CONV_CT), lambda i: (0, i)), ANY],
        out_specs=[seg, pl.BlockSpec((8, CONV_CT), lambda i: (0, i))],
        out_shape=[jax.ShapeDtypeStruct((t, NP), BF16), jax.ShapeDtypeStruct((8, CONV_DIM), F32)],
        scratch_shapes=[pltpu.VMEM((t + 2 * CONV_PAD, CONV_CT), F32), pltpu.VMEM((t + 2 * CONV_PAD, CONV_CT), F32)],
        input_output_aliases={4: 0}, compiler_params=_params(1),
    )(proj, dact, conv_w8, conv_b, dproj)


GW = SSD_REP * SSD_HEAD_DIM


def _expand_np(hoff):
    e = np.zeros((LANE, D_INNER), np.float32)
    for h in range(SSD_HEADS):
        e[hoff + h, h * SSD_HEAD_DIM:(h + 1) * SSD_HEAD_DIM] = 1.0
    return e


def _head_masks():
    lane = lax.broadcasted_iota(jnp.int32, (1, GW), 1)
    return [((lane >= r * SSD_HEAD_DIM) & (lane < (r + 1) * SSD_HEAD_DIM)).astype(F32) for r in range(SSD_REP)]


def _ssd_common(dtc_ref, dtr_ref, bc_ref, br_ref, alc_ref, alr_ref, e_ref, rev):
    L = CHUNK
    ri = lax.broadcasted_iota(jnp.int32, (L, L), 0)
    ci = lax.broadcasted_iota(jnp.int32, (L, L), 1)
    tri = (ri <= ci) if rev else (ri >= ci)
    trit = (ri >= ci) if rev else (ri <= ci)
    raw_c = dtc_ref[...] + bc_ref[...]
    dt_c = _softplus(raw_c)
    a_c = -jnp.exp(alc_ref[...])
    cs_c = jnp.dot(tri.astype(F32), dt_c * a_c, precision=lax.Precision.HIGHEST, preferred_element_type=F32)
    dt_r = _softplus(dtr_ref[...] + br_ref[...])
    a_r = -jnp.exp(alr_ref[...])
    cs_r = jnp.dot(dt_r * a_r, trit.astype(F32), precision=lax.Precision.HIGHEST, preferred_element_type=F32)
    il = 0 if rev else L - 1
    e_c = jnp.exp(cs_c)
    w_c = jnp.exp(cs_c[il:il + 1, :] - cs_c)
    ex = _dot(_mx(jnp.concatenate([dt_c, e_c, w_c], axis=0)), e_ref[...])
    return dict(tri=tri, trit=trit, raw_c=raw_c, dt_c=dt_c, a_c=a_c, cs_c=cs_c, cs_r=cs_r, il=il, e_last=e_c[il:il + 1],
                dt_x=ex[0:L], e_x=ex[L:2 * L], w_x=ex[2 * L:3 * L], cd_x=ex[L + il:L + il + 1])


def _ssd_fwd_body(*refs, rev, hoff, nsteps, comm, has_add):
    L = CHUNK
    ins, cin, (y_ref, st_ref), cout, (s_scr,), csem = _split_refs(refs, 11 if has_add else 10, 2, comm)
    xs_ref, b_ref, c_ref, dtc_ref, dtr_ref, bc_ref, br_ref, alc_ref, alr_ref, e_ref = ins[:10]
    if comm is not None:
        comm.run(cin, cout, csem, pl.program_id(0), nsteps)

    @pl.when(pl.program_id(0) == 0)
    def _():
        s_scr[...] = jnp.zeros_like(s_scr)

    st_ref[0] = s_scr[...]
    q = _ssd_common(dtc_ref, dtr_ref, bc_ref, br_ref, alc_ref, alr_ref, e_ref, rev)
    masks = _head_masks()
    xdt = xs_ref[...] * q["dt_x"]
    xw = xdt * q["w_x"]
    early = []
    for g in range(SSD_GROUPS):
        gc = slice(g * GW, (g + 1) * GW)
        nc = slice(g * SSD_STATE, (g + 1) * SSD_STATE)
        bg = _mx(b_ref[:, nc])
        cg = _mx(c_ref[:, nc])
        sg = s_scr[:, gc]
        early.append((_dot(cg, bg, ((1,), (1,))), _dot(cg, _mx(sg)) * q["e_x"][:, gc]))
        s_scr[:, gc] = sg * q["cd_x"][:, gc] + _dot(bg, _mx(xw[:, gc]), ((0,), (0,)))
    for g in range(SSD_GROUPS):
        gc = slice(g * GW, (g + 1) * GW)
        cb, y = early[g]
        ms = []
        for r in range(SSD_REP):
            hh = hoff + g * SSD_REP + r
            diff = q["cs_c"][:, hh:hh + 1] - q["cs_r"][hh:hh + 1, :]
            ms.append(_mx(cb * jnp.exp(jnp.where(q["tri"], diff, NEG))))
        y4 = _dot(jnp.concatenate(ms, axis=0), _mx(xdt[:, gc]))
        for r in range(SSD_REP):
            y = y + y4[r * L:(r + 1) * L] * masks[r]
        y_ref[:, gc] = y + ins[10][:, gc] if has_add else y


def _dt_rows_body(p_ref, o_ref):
    o_ref[...] = p_ref[...].T


def _dt_rows(proj, tr=512):
    t = proj.shape[0]
    tr = min(tr, t)
    return pl.pallas_call(
        functools.partial(_dt_rows_body), name="dt_rows", grid=(t // tr,),
        in_specs=[pl.BlockSpec((tr, LANE), lambda i: (i, 0))], out_specs=pl.BlockSpec((LANE, tr), lambda i: (0, i)),
        out_shape=jax.ShapeDtypeStruct((LANE, t), F32), compiler_params=_params(1),
    )(proj)


def _ssd_specs(t, rev):
    nc = t // CHUNK
    cm = (lambda i: nc - 1 - i) if rev else (lambda i: i)
    xs = pl.BlockSpec((CHUNK, D_INNER), lambda i: (cm(i), 0))
    bb = pl.BlockSpec((CHUNK, SSD_GROUPS * SSD_STATE), lambda i: (cm(i), 2))
    cc = pl.BlockSpec((CHUNK, SSD_GROUPS * SSD_STATE), lambda i: (cm(i), 3))
    dtc = pl.BlockSpec((CHUNK, LANE), lambda i: (cm(i), 0))
    dtr = pl.BlockSpec((LANE, CHUNK), lambda i: (0, cm(i)))
    colv = pl.BlockSpec((1, LANE), lambda i: (0, 0))
    rowv = pl.BlockSpec((LANE, 1), lambda i: (0, 0))
    return cm, xs, bb, cc, dtc, dtr, colv, rowv


def _ssd_fwd(xbc, proj, dtt, bias, alog, rev, comm=None, add=None):
    t = xbc.shape[0]
    nc = t // CHUNK
    cm, xs, bb, cc, dtc, dtr, colv, rowv = _ssd_specs(t, rev)
    hoff = SSD_HEADS if rev else 0
    e = jnp.asarray(_expand_np(hoff), BF16)
    in_specs = [xs, bb, cc, dtc, dtr, colv, rowv, colv, rowv, pl.BlockSpec((LANE, D_INNER), lambda i: (0, 0))]
    args = [xbc, xbc, xbc, proj, dtt, bias.reshape(1, LANE), bias.reshape(LANE, 1), alog.reshape(1, LANE), alog.reshape(LANE, 1), e]
    out_specs = [xs, pl.BlockSpec((1, SSD_STATE, D_INNER), lambda i: (cm(i), 0, 0))]
    out_shape = [jax.ShapeDtypeStruct((t, D_INNER), F32), jax.ShapeDtypeStruct((nc, SSD_STATE, D_INNER), F32)]
    scratch = [pltpu.VMEM((SSD_STATE, D_INNER), F32)]
    if add is not None:
        in_specs, args = in_specs + [xs], args + [add]
    if comm is not None:
        in_specs, args = in_specs + [ANY] * comm.n, args + comm.arrs
        out_specs, out_shape = out_specs + [ANY] * comm.n, out_shape + comm.out_shapes()
        scratch = scratch + comm.sems()
    return pl.pallas_call(
        functools.partial(_ssd_fwd_body, rev=rev, hoff=hoff, nsteps=nc, comm=comm, has_add=add is not None), name="ssd_fwd_rev" if rev else "ssd_fwd",
        grid=(nc,), in_specs=in_specs, out_specs=out_specs, out_shape=out_shape, scratch_shapes=scratch,
        compiler_params=_params(1),
    )(*args)


def _ssd_bwd_body(*refs, rev, hoff, first, nsteps, comm):
    L = CHUNK
    ins, cin, (dx_ref, ddt_ref, sm_ref), cout, (ds_scr,), csem = _split_refs(refs, 14 if first else 18, 3, comm)
    (xs_ref, b_ref, c_ref, dtc_ref, dtr_ref, bc_ref, br_ref, alc_ref, alr_ref, e_ref, et_ref, dy_ref, sp_ref,
     dsk_ref) = ins[:14]
    if not first:
        pdx_ref, pddt_ref, psm_ref = ins[14:17]
    if comm is not None:
        comm.run(cin, cout, csem, pl.program_id(0), nsteps)

    @pl.when(pl.program_id(0) == 0)
    def _():
        ds_scr[...] = jnp.zeros_like(ds_scr)
        sm_ref[...] = jnp.zeros_like(sm_ref) if first else psm_ref[...]

    q = _ssd_common(dtc_ref, dtr_ref, bc_ref, br_ref, alc_ref, alr_ref, e_ref, rev)
    masks = _head_masks()
    xs = xs_ref[...]
    dy = dy_ref[...]
    xdt = xs * q["dt_x"]
    xw = xdt * q["w_x"]
    dye = dy * q["e_x"]
    ds_old = ds_scr[...]
    dxdt_parts, q1_parts, q2_parts = [], [], []
    ddiag = jnp.zeros((L, LANE), F32)
    early = []
    for g in range(SSD_GROUPS):
        gc = slice(g * GW, (g + 1) * GW)
        nc = slice(g * SSD_STATE, (g + 1) * SSD_STATE)
        bg = _mx(b_ref[:, nc])
        cg = _mx(c_ref[:, nc])
        dy_g = _mx(dy[:, gc])
        xdt_g = _mx(xdt[:, gc])
        dsg = ds_old[:, gc]
        spg = _mx(sp_ref[0, :, gc])
        dye_g = _mx(dye[:, gc])
        bds = _dot(bg, _mx(dsg))
        early.append(dict(
            bg=bg, cg=cg, dy_g=dy_g, bds=bds,
            cb=_dot(cg, bg, ((1,), (1,))), cbt=_dot(bg, cg, ((1,), (1,))),
            dm=_dot(_mx(jnp.concatenate([dy[:, gc] * m for m in masks], axis=0)), xdt_g, ((1,), (1,))),
            dmt=_dot(_mx(jnp.concatenate([xdt[:, gc] * m for m in masks], axis=0)), dy_g, ((1,), (1,))),
            dc=_dot(dye_g, spg, ((1,), (1,))), db=_dot(_mx(xw[:, gc]), _mx(dsg), ((1,), (1,)))))
        q1_parts.append(dye[:, gc] * _dot(cg, spg))
        q2_parts.append(xw[:, gc] * bds)
        ds_scr[:, gc] = dsg * q["cd_x"][:, gc] + _dot(cg, dye_g, ((0,), (0,)))
    for g in range(SSD_GROUPS):
        gc = slice(g * GW, (g + 1) * GW)
        e = early[g]
        bg, cg, dy_g, bds, cb, cbt, dm, dmt = (e[k] for k in ("bg", "cg", "dy_g", "bds", "cb", "cbt", "dm", "dmt"))
        dcb = jnp.zeros((L, L), F32)
        dcbt = jnp.zeros((L, L), F32)
        mts = []
        for r in range(SSD_REP):
            hh = hoff + g * SSD_REP + r
            col = q["cs_c"][:, hh:hh + 1]
            row = q["cs_r"][hh:hh + 1, :]
            dec = jnp.exp(jnp.where(q["tri"], col - row, NEG))
            dect = jnp.exp(jnp.where(q["trit"], row - col, NEG))
            pd = dm[r * L:(r + 1) * L] * dec
            pdt = dmt[r * L:(r + 1) * L] * dect
            dcb = dcb + pd
            dcbt = dcbt + pdt
            mts.append(_mx(cbt * dect))
            ddiag = ddiag + (jnp.sum(pd * cb, axis=1, keepdims=True) - jnp.sum(pdt * cbt, axis=1, keepdims=True)) * _onehot_lane(hh)
        x4 = _dot(jnp.concatenate(mts, axis=0), dy_g)
        dxdt_g = bds * q["w_x"][:, gc]
        for r in range(SSD_REP):
            dxdt_g = dxdt_g + x4[r * L:(r + 1) * L] * masks[r]
        dxdt_parts.append(dxdt_g)
        dc_g = _dot(_mx(dcb), bg) + e["dc"]
        db_g = _dot(_mx(dcbt), cg) + e["db"]
        boff = D_INNER + g * SSD_STATE
        coff = D_INNER + SSD_GROUPS * SSD_STATE + g * SSD_STATE
        if first:
            dx_ref[:, boff:boff + SSD_STATE] = db_g
            dx_ref[:, coff:coff + SSD_STATE] = dc_g
        else:
            dx_ref[:, boff:boff + SSD_STATE] = pdx_ref[:, boff:boff + SSD_STATE] + db_g
            dx_ref[:, coff:coff + SSD_STATE] = pdx_ref[:, coff:coff + SSD_STATE] + dc_g
    dxdt = jnp.concatenate(dxdt_parts, axis=1)
    et = et_ref[...]
    hs = _dot(_mx(jnp.concatenate([jnp.concatenate(q1_parts, axis=1), jnp.concatenate(q2_parts, axis=1), dxdt * xs], axis=0)), et)
    q1, q2, r3 = hs[0:L], hs[L:2 * L], hs[2 * L:3 * L]
    t_prev = _split_dot(jnp.sum(ds_old * sp_ref[0], axis=0, keepdims=True), et) * q["e_last"]
    rows = lax.broadcasted_iota(jnp.int32, (L, LANE), 0)
    dcs = ddiag + q1 - q2 + jnp.where(rows == q["il"], jnp.sum(q2, axis=0, keepdims=True) + t_prev, 0.0)
    dad = jnp.dot(q["trit"].astype(F32), dcs, precision=lax.Precision.HIGHEST, preferred_element_type=F32)
    ddt = dad * q["a_c"] + r3
    ddt_raw = ddt * _sigmoid(q["raw_c"])
    dal = jnp.sum(dad * q["dt_c"], axis=0, keepdims=True) * q["a_c"]
    dbias = jnp.sum(ddt_raw, axis=0, keepdims=True)
    sm_ref[...] += jnp.concatenate([dal, dbias, jnp.zeros((6, LANE), F32)], axis=0)
    dxs = dxdt * q["dt_x"]
    if first:
        dx_ref[:, 0:D_INNER] = dxs + dy * dsk_ref[...]
        ddt_ref[...] = ddt_raw
    else:
        dx_ref[:, 0:D_INNER] = pdx_ref[:, 0:D_INNER] + dxs
        ddt_ref[...] = jnp.concatenate([(pddt_ref[...] + ddt_raw).astype(ddt_ref.dtype), jnp.zeros((L, DTW - LANE), ddt_ref.dtype)],
                                       axis=1)


def _ssd_bwd(xbc, proj, dtt, bias, alog, dy, states, dskip_x, rev, prev=None, dproj=None, comm=None):
    t = xbc.shape[0]
    nc = t // CHUNK
    first = prev is None
    cm, xs, bb, cc, dtc, dtr, colv, rowv = _ssd_specs(t, not rev)
    hoff = SSD_HEADS if rev else 0
    e_np = _expand_np(hoff)
    e = jnp.asarray(e_np, BF16)
    et = jnp.asarray(e_np.T, BF16)
    st_spec = pl.BlockSpec((1, SSD_STATE, D_INNER), lambda i: (cm(i), 0, 0))
    dxo = pl.BlockSpec((CHUNK, CONV_DIM), lambda i: (cm(i), 0))
    ddto = pl.BlockSpec((CHUNK, LANE), lambda i: (cm(i), 0))
    smo = pl.BlockSpec((8, LANE), lambda i: (0, 0))
    in_specs = [xs, bb, cc, dtc, dtr, colv, rowv, colv, rowv, pl.BlockSpec((LANE, D_INNER), lambda i: (0, 0)),
                pl.BlockSpec((D_INNER, LANE), lambda i: (0, 0)), xs, st_spec, pl.BlockSpec((1, D_INNER), lambda i: (0, 0))]
    args = [xbc, xbc, xbc, proj, dtt, bias.reshape(1, LANE), bias.reshape(LANE, 1), alog.reshape(1, LANE),
            alog.reshape(LANE, 1), e, et, dy, states, dskip_x]
    out_specs = [dxo, ddto, smo]
    out_shape = [jax.ShapeDtypeStruct((t, CONV_DIM), F32), jax.ShapeDtypeStruct((t, LANE), F32),
                 jax.ShapeDtypeStruct((8, LANE), F32)]
    aliases = {}
    if not first:
        in_specs += [dxo, ddto, smo, ANY]
        args += list(prev) + [dproj]
        out_specs[1] = pl.BlockSpec((CHUNK, DTW), lambda i: (cm(i), DT0 // DTW))
        out_shape[1] = jax.ShapeDtypeStruct((t, NP), BF16)
        aliases = {17: 1}
    scratch = [pltpu.VMEM((SSD_STATE, D_INNER), F32)]
    if comm is not None:
        in_specs, args = in_specs + [ANY] * comm.n, args + comm.arrs
        out_specs, out_shape = out_specs + [ANY] * comm.n, out_shape + comm.out_shapes()
        scratch = scratch + comm.sems()
    return pl.pallas_call(
        functools.partial(_ssd_bwd_body, rev=rev, hoff=hoff, first=first, nsteps=nc, comm=comm),
        name="ssd_bwd_rev" if rev else "ssd_bwd", grid=(nc,), in_specs=in_specs, out_specs=out_specs, out_shape=out_shape,
        scratch_shapes=scratch, input_output_aliases=aliases, compiler_params=_params(1),
    )(*args)


def _gnorm_parts(y_ref, xs_ref, z_ref, dsk_ref):
    xs = xs_ref[...]
    y = y_ref[...] + xs * dsk_ref[...]
    z = z_ref[...].astype(F32)
    sg = _sigmoid(z)
    s = z * sg
    v = y * s
    rs = []
    for g in range(SSD_GROUPS):
        vg = v[:, g * GW:(g + 1) * GW]
        rs.append(jnp.broadcast_to(lax.rsqrt(jnp.mean(vg * vg, axis=-1, keepdims=True) + EPS), vg.shape))
    return xs, y, z, sg, s, v, jnp.concatenate(rs, axis=1)


def _gnorm_fwd_body(y_ref, xs_ref, z_ref, dsk_ref, w_ref, o_ref):
    _, _, _, _, _, v, rstd = _gnorm_parts(y_ref, xs_ref, z_ref, dsk_ref)
    o_ref[...] = (v * rstd * w_ref[...]).astype(o_ref.dtype)


def _gnorm_fwd(y, xbc, proj, dskip_x, norm_w, tr=256):
    t = y.shape[0]
    tr = min(tr, t)
    row = pl.BlockSpec((tr, D_INNER), lambda i: (i, 0))
    vec = pl.BlockSpec((1, D_INNER), lambda i: (0, 0))
    return pl.pallas_call(
        functools.partial(_gnorm_fwd_body), name="gnorm_fwd", grid=(t // tr,),
        in_specs=[row, row, pl.BlockSpec((tr, D_INNER), lambda i: (i, Z0 // D_INNER)), vec, vec],
        out_specs=row, out_shape=jax.ShapeDtypeStruct((t, D_INNER), BF16), compiler_params=_params(1),
    )(y, xbc, proj, dskip_x, norm_w.reshape(1, D_INNER))


def _gnorm_bwd_body(du_ref, y_ref, xs_ref, z_ref, dsk_ref, w_ref, et_ref, _, dy_ref, dz_ref, dw_ref, dd_ref, acc_ref, *,
                    nsteps):
    i = pl.program_id(0)

    @pl.when(i == 0)
    def _():
        dw_ref[...] = jnp.zeros_like(dw_ref)
        acc_ref[...] = jnp.zeros_like(acc_ref)

    xs, y, z, sg, s, v, rstd = _gnorm_parts(y_ref, xs_ref, z_ref, dsk_ref)
    du = du_ref[...].astype(F32)
    n = v * rstd
    dn = du * w_ref[...]
    dw_ref[...] += jnp.sum(du * n, axis=0, keepdims=True)
    prod = dn * n
    means = []
    for g in range(SSD_GROUPS):
        pg = prod[:, g * GW:(g + 1) * GW]
        means.append(jnp.broadcast_to(jnp.mean(pg, axis=-1, keepdims=True), pg.shape))
    dv = rstd * (dn - n * jnp.concatenate(means, axis=1))
    dy = dv * s
    dy_ref[...] = dy
    dz_ref[...] = (dv * y * (sg * (1.0 + z * (1.0 - sg)))).astype(dz_ref.dtype)
    acc_ref[...] += jnp.sum(dy * xs, axis=0, keepdims=True)

    @pl.when(i == nsteps - 1)
    def _():
        dd_ref[...] = _split_dot(acc_ref[...], et_ref[...])


def _gnorm_bwd(du, y, xbc, proj, dskip_x, norm_w, dproj, tr=256):
    t = y.shape[0]
    tr = min(tr, t)
    row = pl.BlockSpec((tr, D_INNER), lambda i: (i, 0))
    zcol = pl.BlockSpec((tr, D_INNER), lambda i: (i, Z0 // D_INNER))
    vec = pl.BlockSpec((1, D_INNER), lambda i: (0, 0))
    et = jnp.asarray(_expand_np(0).T, BF16)
    return pl.pallas_call(
        functools.partial(_gnorm_bwd_body, nsteps=t // tr), name="gnorm_bwd", grid=(t // tr,),
        in_specs=[row, row, row, zcol, vec, vec, pl.BlockSpec((D_INNER, LANE), lambda i: (0, 0)), ANY],
        out_specs=[row, zcol, vec, pl.BlockSpec((1, LANE), lambda i: (0, 0))],
        out_shape=[jax.ShapeDtypeStruct((t, D_INNER), F32), jax.ShapeDtypeStruct((t, NP), BF16),
                   jax.ShapeDtypeStruct((1, D_INNER), F32), jax.ShapeDtypeStruct((1, LANE), F32)],
        scratch_shapes=[pltpu.VMEM((1, D_INNER), F32)], input_output_aliases={7: 1},
        compiler_params=_params(1),
    )(du, y, xbc, proj, dskip_x, norm_w.reshape(1, D_INNER), et, dproj)


def _gate_fwd_body(gt_ref, b_ref, ys_ref, ya_ref, o_ref):
    g = _sigmoid(gt_ref[...].astype(F32) + b_ref[...])
    o_ref[...] = (g[:, 0:D_MODEL] * ys_ref[...] + g[:, D_MODEL:] * ya_ref[...]).astype(o_ref.dtype)


def _gate_fwd(proj, b_gate, y_ssd, y_attn, tr=512):
    t = y_ssd.shape[0]
    tr = min(tr, t)
    row = pl.BlockSpec((tr, D_MODEL), lambda i: (i, 0))
    return pl.pallas_call(
        functools.partial(_gate_fwd_body), name="gate_fwd", grid=(t // tr,),
        in_specs=[pl.BlockSpec((tr, 2 * D_MODEL), lambda i: (i, GT0 // (2 * D_MODEL))),
                  pl.BlockSpec((1, 2 * D_MODEL), lambda i: (0, 0)), row, row],
        out_specs=row, out_shape=jax.ShapeDtypeStruct((t, D_MODEL), BF16), compiler_params=_params(1),
    )(proj, b_gate.reshape(1, 2 * D_MODEL), y_ssd, y_attn)


def _gate_bwd_body(dm_ref, gt_ref, b_ref, ys_ref, ya_ref, dys_ref, dya_ref, dgt_ref, db_ref):
    @pl.when(pl.program_id(0) == 0)
    def _():
        db_ref[...] = jnp.zeros_like(db_ref)

    g = _sigmoid(gt_ref[...].astype(F32) + b_ref[...])
    gs = g[:, 0:D_MODEL]
    ga = g[:, D_MODEL:]
    dm = dm_ref[...].astype(F32)
    dys_ref[...] = (dm * gs).astype(dys_ref.dtype)
    dya_ref[...] = (dm * ga).astype(dya_ref.dtype)
    dgt = jnp.concatenate([dm * ys_ref[...] * gs * (1.0 - gs), dm * ya_ref[...] * ga * (1.0 - ga)], axis=1)
    dgt_ref[...] = dgt.astype(dgt_ref.dtype)
    db_ref[...] += jnp.sum(dgt, axis=0, keepdims=True)


def _gate_bwd(dmix, proj, b_gate, y_ssd, y_attn, tr=512):
    t = y_ssd.shape[0]
    tr = min(tr, t)
    row = pl.BlockSpec((tr, D_MODEL), lambda i: (i, 0))
    gates = pl.BlockSpec((tr, 2 * D_MODEL), lambda i: (i, GT0 // (2 * D_MODEL)))
    vec2 = pl.BlockSpec((1, 2 * D_MODEL), lambda i: (0, 0))
    return pl.pallas_call(
        functools.partial(_gate_bwd_body), name="gate_bwd", grid=(t // tr,),
        in_specs=[row, gates, vec2, row, row],
        out_specs=[row, row, gates, vec2],
        out_shape=[jax.ShapeDtypeStruct((t, D_MODEL), BF16), jax.ShapeDtypeStruct((t, D_MODEL), BF16),
                   jax.ShapeDtypeStruct((t, NP), BF16), jax.ShapeDtypeStruct((1, 2 * D_MODEL), F32)],
        compiler_params=_params(1),
    )(dmix, proj, b_gate.reshape(1, 2 * D_MODEL), y_ssd, y_attn)


def _bucket_np():
    i = np.arange(BLOCK)[:, None]
    j = np.arange(3 * BLOCK)[None, :]
    rel = j - BLOCK - i
    nb = N_BUCKETS // 2
    max_exact = nb // 2
    ret = np.where(rel > 0, nb, 0)
    n = np.abs(rel)
    nf = np.maximum(n, 1).astype(np.float32)
    large = max_exact + (np.log(nf / np.float32(max_exact)) / np.float32(math.log(MAX_DISTANCE / max_exact))
                         * np.float32(nb - max_exact)).astype(np.int32)
    large = np.minimum(large, nb - 1)
    bucket = ret + np.where(n < max_exact, n, large)
    band = np.abs(rel) <= WINDOW
    return bucket, band


def _onehot_np():
    bucket, _ = _bucket_np()
    oh = np.zeros((BLOCK * 3 * BLOCK, LANE), np.float32)
    oh[np.arange(oh.shape[0]), bucket.reshape(-1)] = 1.0
    return oh


REP = Q_HEADS // KV_HEADS


def _group_cols(ref, g):
    return jnp.concatenate([ref[:, (g * REP + r) * HEAD_PAD:(g * REP + r + 1) * HEAD_PAD] for r in range(REP)], axis=0)


def _group_lane(ref, g, rows):
    return jnp.concatenate([jnp.broadcast_to(ref[:, g * REP + r:g * REP + r + 1], (rows, 1)) for r in range(REP)], axis=0)


def _attn_kv(refs, g):
    gc = slice(g * HEAD_PAD, (g + 1) * HEAD_PAD)
    return _mx(jnp.concatenate([r[:, gc] for r in refs], axis=0))


def _attn_logits(q_ref, kg, bias_ref, edge, g):
    q4 = _mx(_group_cols(q_ref, g))
    s = _dot(q4, kg, ((1,), (1,))) * (HEAD_DIM ** -0.5) + bias_ref[g] + edge
    return q4, s


def _attn_edge(n, nb):
    col = lax.broadcasted_iota(jnp.int32, (1, 3 * BLOCK), 1)
    bad = ((col < BLOCK) & (n == 0)) | ((col >= 2 * BLOCK) & (n == nb - 1))
    return jnp.where(bad, NEG, 0.0)


def _onehot_lane(h):
    return (lax.broadcasted_iota(jnp.int32, (1, LANE), 1) == h).astype(F32)


def _attn_fwd_body(*refs, nb, comm):
    ins, cin, (o_ref, lse_ref), cout, _, csem = _split_refs(refs, 9, 2, comm)
    q_ref, kp_ref, kc_ref, kn_ref, vp_ref, vc_ref, vn_ref, bias_ref, sink_ref = ins
    n = pl.program_id(0)
    if comm is not None:
        comm.run(cin, cout, csem, n, nb)
    edge = _attn_edge(n, nb)
    lse = jnp.zeros((BLOCK, LANE), F32)
    logits = [_attn_logits(q_ref, _attn_kv((kp_ref, kc_ref, kn_ref), g), bias_ref, edge, g)[1] for g in range(KV_HEADS)]
    for g in range(KV_HEADS):
        vg = _attn_kv((vp_ref, vc_ref, vn_ref), g)
        s = logits[g]
        sink = _group_lane(sink_ref, g, BLOCK)
        m = jnp.maximum(jnp.max(s, axis=-1, keepdims=True), sink)
        p = jnp.exp(s - m)
        den = jnp.sum(p, axis=-1, keepdims=True) + jnp.exp(sink - m)
        o4 = _dot(_mx(p * (1.0 / den)), vg)
        l4 = m + jnp.log(den)
        for r in range(REP):
            h = g * REP + r
            o_ref[:, h * HEAD_PAD:(h + 1) * HEAD_PAD] = o4[r * BLOCK:(r + 1) * BLOCK].astype(o_ref.dtype)
            lse = lse + l4[r * BLOCK:(r + 1) * BLOCK] * _onehot_lane(h)
    lse_ref[...] = lse


def _attn_specs(t, clamp):
    nb = t // BLOCK
    cur = (lambda n: jnp.minimum(n, nb - 1)) if clamp else (lambda n: n)
    prv = lambda n: jnp.maximum(cur(n) - 1, 0)
    nxt = lambda n: jnp.minimum(cur(n) + 1, nb - 1)
    kb, vb = K0 // KP, V0 // KP
    qs = pl.BlockSpec((BLOCK, QP), lambda n: (cur(n), Q0 // QP))
    ks = [pl.BlockSpec((BLOCK, KP), lambda n, f=f: (f(n), kb)) for f in (prv, cur, nxt)]
    vs = [pl.BlockSpec((BLOCK, KP), lambda n, f=f: (f(n), vb)) for f in (prv, cur, nxt)]
    consts = [pl.BlockSpec((KV_HEADS, REP * BLOCK, 3 * BLOCK), lambda n: (0, 0, 0)), pl.BlockSpec((1, LANE), lambda n: (0, 0))]
    return nb, cur, qs, ks, vs, consts


def _attn_fwd(proj, bias, sink, comm=None):
    t = proj.shape[0]
    nb, cur, qs, ks, vs, consts = _attn_specs(t, False)
    in_specs, args = [qs] + ks + vs + consts, [proj] * 7 + [bias, sink]
    out_specs = [pl.BlockSpec((BLOCK, QP), lambda n: (n, 0)), pl.BlockSpec((BLOCK, LANE), lambda n: (n, 0))]
    out_shape = [jax.ShapeDtypeStruct((t, QP), BF16), jax.ShapeDtypeStruct((t, LANE), F32)]
    scratch = []
    if comm is not None:
        in_specs, args = in_specs + [ANY] * comm.n, args + comm.arrs
        out_specs, out_shape = out_specs + [ANY] * comm.n, out_shape + comm.out_shapes()
        scratch = comm.sems()
    return pl.pallas_call(
        functools.partial(_attn_fwd_body, nb=nb, comm=comm), name="attn_fwd", grid=(nb,),
        in_specs=in_specs, out_specs=out_specs, out_shape=out_shape, scratch_shapes=scratch, compiler_params=_params(1),
    )(*args)


def _attn_bwd_body(*refs, nb, comm):
    ins, cin, (dqkv_ref, dbias_ref, dsink_ref), cout, (ak, bk, av, bv, dq_scr), csem = _split_refs(refs, 13, 3, comm)
    q_ref, kp_ref, kc_ref, kn_ref, vp_ref, vc_ref, vn_ref, bias_ref, sink_ref, do_ref, o_ref, lse_ref = ins[:12]
    dk_ref = dqkv_ref.at[:, K0:K0 + KP]
    dv_ref = dqkv_ref.at[:, V0:V0 + KP]
    n = pl.program_id(0)
    scale = HEAD_DIM ** -0.5
    if comm is not None:
        comm.run(cin, cout, csem, n, nb + 1)

    @pl.when(n == 0)
    def _():
        for r in (ak, bk, av, bv, dq_scr, dbias_ref, dsink_ref):
            r[...] = jnp.zeros_like(r)

    dqkv_ref[:, Q0:Q0 + QP] = dq_scr[...]

    @pl.when(n < nb)
    def _():
        edge = _attn_edge(n, nb)
        dsink = jnp.zeros((1, LANE), F32)
        early = []
        for g in range(KV_HEADS):
            kg = _attn_kv((kp_ref, kc_ref, kn_ref), g)
            q4, s = _attn_logits(q_ref, kg, bias_ref, edge, g)
            do = _group_cols(do_ref, g).astype(F32)
            early.append((kg, q4, s, do, _dot(_mx(do), _attn_kv((vp_ref, vc_ref, vn_ref), g), ((1,), (1,)))))
        for g in range(KV_HEADS):
            gc = slice(g * HEAD_PAD, (g + 1) * HEAD_PAD)
            kg, q4, s, do, dp = early[g]
            lse = _group_lane(lse_ref, g, BLOCK)
            p = jnp.exp(s - lse)
            delta = jnp.sum(do * _group_cols(o_ref, g).astype(F32), axis=-1, keepdims=True)
            ds = p * (dp - delta)
            sunk = jnp.exp(_group_lane(sink_ref, g, BLOCK) - lse) * delta
            dbias_ref[g] += ds
            dsb = _mx(ds)
            dq4 = _dot(dsb, kg) * scale
            for r in range(REP):
                h = g * REP + r
                dq_scr[:, h * HEAD_PAD:(h + 1) * HEAD_PAD] = dq4[r * BLOCK:(r + 1) * BLOCK].astype(dq_scr.dtype)
                dsink = dsink - jnp.sum(sunk[r * BLOCK:(r + 1) * BLOCK], axis=0, keepdims=True) * _onehot_lane(h)
            dk_c = _dot(dsb, q4, ((0,), (0,))) * scale
            dv_c = _dot(_mx(p), _mx(do), ((0,), (0,)))
            for acc_a, acc_b, out, c in ((ak, bk, dk_ref, dk_c), (av, bv, dv_ref, dv_c)):
                out[:, gc] = (acc_a[:, gc] + c[0:BLOCK]).astype(out.dtype)
                acc_a[:, gc] = acc_b[:, gc] + c[BLOCK:2 * BLOCK]
                acc_b[:, gc] = c[2 * BLOCK:3 * BLOCK]
        dsink_ref[...] += dsink

    @pl.when(n == nb)
    def _():
        dk_ref[...] = ak[...].astype(dk_ref.dtype)
        dv_ref[...] = av[...].astype(dv_ref.dtype)


def _attn_bwd(proj, bias, sink, do, o, lse, dproj, comm=None):
    t = proj.shape[0]
    nb, cur, qs, ks, vs, consts = _attn_specs(t, True)
    rowq = pl.BlockSpec((BLOCK, QP), lambda n: (cur(n), 0))
    late = pl.BlockSpec((BLOCK, QKV), lambda n: (jnp.maximum(n - 1, 0), Q0 // QKV))
    in_specs = [qs] + ks + vs + consts + [rowq, rowq, pl.BlockSpec((BLOCK, LANE), lambda n: (cur(n), 0)), ANY]
    args = [proj] * 7 + [bias, sink, do, o, lse, dproj]
    out_specs = [late, pl.BlockSpec((KV_HEADS, REP * BLOCK, 3 * BLOCK), lambda n: (0, 0, 0)), pl.BlockSpec((1, LANE), lambda n: (0, 0))]
    out_shape = [jax.ShapeDtypeStruct((t, NP), BF16), jax.ShapeDtypeStruct((KV_HEADS, REP * BLOCK, 3 * BLOCK), F32),
                 jax.ShapeDtypeStruct((1, LANE), F32)]
    scratch = [pltpu.VMEM((BLOCK, KP), F32) for _ in range(4)] + [pltpu.VMEM((BLOCK, QP), BF16)]
    if comm is not None:
        in_specs, args = in_specs + [ANY] * comm.n, args + comm.arrs
        out_specs, out_shape = out_specs + [ANY] * comm.n, out_shape + comm.out_shapes()
        scratch = scratch + comm.sems()
    return pl.pallas_call(
        functools.partial(_attn_bwd_body, nb=nb, comm=comm), name="attn_bwd", grid=(nb + 1,),
        in_specs=in_specs, out_specs=out_specs, out_shape=out_shape, scratch_shapes=scratch, input_output_aliases={12: 0},
        compiler_params=_params(1),
    )(*args)


def _adamw_body(gp_ref, w_ref, m_ref, v_ref, g_ref, d_ref, nm_ref, nv_ref, *, nparts):
    g = gp_ref[0].astype(F32)
    for j in range(1, nparts):
        g = g + gp_ref[j].astype(F32)
    m = ADAM_B1 * m_ref[...] + (1.0 - ADAM_B1) * g
    v = ADAM_B2 * v_ref[...] + (1.0 - ADAM_B2) * (g * g)
    m_hat = m / (1.0 - ADAM_B1 ** ADAM_STEP)
    v_hat = v / (1.0 - ADAM_B2 ** ADAM_STEP)
    g_ref[...] = g
    d_ref[...] = -ADAM_LR * (m_hat / (jnp.sqrt(v_hat) + ADAM_EPS) + ADAM_WD * w_ref[...])
    nm_ref[...] = m
    nv_ref[...] = v


def _adamw_layer_body(*refs, nparts):
    gp_ref, w_ref, m_ref, v_ref = refs[:4]
    g_ref, d_ref, nm_ref, nv_ref = refs[-4:]
    g = gp_ref[0].astype(F32)
    for j in range(1, nparts):
        g = g + gp_ref[j].astype(F32)
    m = ADAM_B1 * m_ref[0] + (1.0 - ADAM_B1) * g
    v = ADAM_B2 * v_ref[0] + (1.0 - ADAM_B2) * (g * g)
    m_hat = m / (1.0 - ADAM_B1 ** ADAM_STEP)
    v_hat = v / (1.0 - ADAM_B2 ** ADAM_STEP)
    g_ref[0] = g
    d_ref[0] = -ADAM_LR * (m_hat / (jnp.sqrt(v_hat) + ADAM_EPS) + ADAM_WD * w_ref[0])
    nm_ref[0] = m
    nv_ref[0] = v


def _adamw_layer(gparts, row0, w, m, v, l, prev, name, tr=128):
    p, c = gparts.shape[0], gparts.shape[2]
    per = w.shape[1]
    tr = min(tr, per)
    assert per % tr == 0 and row0 % tr == 0 and w.shape[2] == c
    lay = pl.BlockSpec((1, tr, c), lambda i: (l, i, 0))
    in_specs = [pl.BlockSpec((p, tr, c), lambda i: (0, row0 // tr + i, 0)), lay, lay, lay]
    args = [gparts, w, m, v]
    aliases = {}
    if prev is not None:
        in_specs, args, aliases = in_specs + [ANY] * 4, args + list(prev), {4 + k: k for k in range(4)}
    sd = jax.ShapeDtypeStruct(w.shape, F32)
    return pl.pallas_call(
        functools.partial(_adamw_layer_body, nparts=p), name=name, grid=(per // tr,),
        in_specs=in_specs, out_specs=[lay] * 4, out_shape=[sd] * 4, input_output_aliases=aliases, compiler_params=_params(1),
    )(*args)


def _adamw(gparts, w, m, v, name, tr=256):
    p, r, c = gparts.shape
    tr = min(tr, r)
    assert r % tr == 0
    row = pl.BlockSpec((tr, c), lambda i: (i, 0))
    sd = jax.ShapeDtypeStruct((r, c), F32)
    return pl.pallas_call(
        functools.partial(_adamw_body, nparts=p), name=name, grid=(r // tr,),
        in_specs=[pl.BlockSpec((p, tr, c), lambda i: (0, i, 0)), row, row, row],
        out_specs=[row, row, row, row], out_shape=[sd, sd, sd, sd], compiler_params=_params(1),
    )(gparts, w, m, v)


MESH = pl.DeviceIdType.MESH
N_REL = N_DEV - 1


def _gather_phases(ins, outs, sems):
    n = len(ins)
    send_sems, recv_sems, local_sems = sems
    x, y, c = lax.axis_index("x"), lax.axis_index("y"), lax.axis_index("c")
    me, sibling = (x, y, c), (x, y, 1 - c)
    chips = [(1 - x, y), (x, 1 - y), (1 - x, 1 - y)]

    def slot(a, p):
        return outs[a].at[4 * p[0] + 2 * p[1] + p[2]]

    def copy(a, k, block, to, src=None):
        return pltpu.make_async_remote_copy(
            src_ref=slot(a, block) if src is None else src, dst_ref=slot(a, block),
            send_sem=send_sems.at[a * N_REL + k], recv_sem=recv_sems.at[a * N_REL + k], device_id=to, device_id_type=MESH)

    mine = [pltpu.make_async_copy(ins[a], slot(a, me), local_sems.at[a]) for a in range(n)]
    first = []
    for a in range(n):
        first.append(copy(a, 0, me, sibling, src=ins[a]))
        first += [copy(a, 1 + j, me, (*chip, c), src=ins[a]) for j, chip in enumerate(chips)]
    passed = [copy(a, 4 + j, (*chip, c), sibling) for j, chip in enumerate(chips) for a in range(n)]

    def start():
        for cp in mine + first:
            cp.start()

    def pass_on():
        i = 0
        for j, chip in enumerate(chips):
            for a in range(n):
                copy(a, 1 + j, (*chip, c), me).wait_recv()
                passed[i].start()
                i += 1

    def finish():
        for a in range(n):
            copy(a, 0, sibling, me).wait_recv()
            for j, chip in enumerate(chips):
                copy(a, 4 + j, (*chip, 1 - c), me).wait_recv()
        for cp in first + passed:
            cp.wait_send()
        for cp in mine:
            cp.wait()

    return start, pass_on, finish


def _exchange_phases(ins, outs, sems):
    n = len(ins)
    send_sems, recv_sems, local_sems = sems
    x, y, c = lax.axis_index("x"), lax.axis_index("y"), lax.axis_index("c")
    me = 4 * x + 2 * y + c
    mine = [pltpu.make_async_copy(ins[a].at[me], outs[a].at[me], local_sems.at[a]) for a in range(n)]
    copies = []
    for a in range(n):
        for k in range(1, N_DEV):
            px = 1 - x if k & 4 else x
            py = 1 - y if k & 2 else y
            pc = 1 - c if k & 1 else c
            peer = 4 * px + 2 * py + pc
            send = pltpu.make_async_remote_copy(
                src_ref=ins[a].at[peer], dst_ref=outs[a].at[me], send_sem=send_sems.at[a * N_REL + k - 1],
                recv_sem=recv_sems.at[a * N_REL + k - 1], device_id=(px, py, pc), device_id_type=MESH)
            recv = pltpu.make_async_remote_copy(
                src_ref=ins[a].at[me], dst_ref=outs[a].at[peer], send_sem=send_sems.at[a * N_REL + k - 1],
                recv_sem=recv_sems.at[a * N_REL + k - 1], device_id=(px, py, pc), device_id_type=MESH)
            copies.append((send, recv))

    def start():
        for cp in mine:
            cp.start()
        for send, _ in copies:
            send.start()

    def finish():
        for send, recv in copies:
            send.wait_send()
            recv.wait_recv()
        for cp in mine:
            cp.wait()

    return start, None, finish


class _Comm:
    def __init__(self, kind, arrs):
        self.kind, self.arrs, self.n = kind, list(arrs), len(arrs)

    def out_shapes(self):
        if self.kind == "gather":
            return [jax.ShapeDtypeStruct((N_DEV,) + a.shape, a.dtype) for a in self.arrs]
        return [jax.ShapeDtypeStruct(a.shape, a.dtype) for a in self.arrs]

    def sems(self):
        return [pltpu.SemaphoreType.DMA((self.n * N_REL,)), pltpu.SemaphoreType.DMA((self.n * N_REL,)),
                pltpu.SemaphoreType.DMA((self.n,))]

    def run(self, ins, outs, sems, step, nsteps):
        start, mid, finish = (_gather_phases if self.kind == "gather" else _exchange_phases)(ins, outs, sems)
        if nsteps is None:
            start()
            if mid is not None:
                mid()
            finish()
            return
        pl.when(step == 0)(start)
        if mid is not None:
            pl.when(step == (3 * nsteps) // 4)(mid)
        pl.when(step == nsteps - 1)(finish)


def _split_refs(refs, n_in, n_out, comm):
    k = comm.n if comm is not None else 0
    i = 0
    parts = []
    for cnt in (n_in, k, n_out, k):
        parts.append(refs[i:i + cnt])
        i += cnt
    rest = refs[i:]
    if comm is None:
        return parts[0], (), parts[2], (), rest, ()
    return parts[0], parts[1], parts[2], parts[3], rest[:len(rest) - 3], rest[len(rest) - 3:]


def _comm_body(*refs, comm):
    n = comm.n
    comm.run(refs[:n], refs[n:2 * n], refs[2 * n:], None, None)


def _communicate(comm, name, vmem=False):
    spec = pl.BlockSpec(memory_space=pltpu.VMEM if vmem else pl.ANY)
    return pl.pallas_call(
        functools.partial(_comm_body, comm=comm), name=name,
        in_specs=[spec] * comm.n, out_specs=[spec] * comm.n, out_shape=comm.out_shapes(), scratch_shapes=comm.sems(),
        compiler_params=pltpu.CompilerParams(vmem_limit_bytes=VMEM_LIMIT),
    )(*comm.arrs)


ANY = pl.BlockSpec(memory_space=pl.ANY)


def _pad_heads(w, axis):
    shp = w.shape
    heads = shp[axis] // HEAD_DIM
    w = w.reshape(shp[:axis] + (heads, HEAD_DIM) + shp[axis + 1:])
    pad = [(0, 0)] * w.ndim
    pad[axis + 1] = (0, HEAD_PAD - HEAD_DIM)
    w = jnp.pad(w, pad)
    return w.reshape(shp[:axis] + (heads * HEAD_PAD,) + shp[axis + 1:])


def _unpad_heads(w, axis):
    shp = w.shape
    heads = shp[axis] // HEAD_PAD
    w = w.reshape(shp[:axis] + (heads, HEAD_PAD) + shp[axis + 1:])
    w = lax.slice_in_dim(w, 0, HEAD_DIM, axis=axis + 1)
    return w.reshape(shp[:axis] + (heads * HEAD_DIM,) + shp[axis + 1:])


def _w_in_to_padded(w):
    idx = np.cumsum((0,) + IN_SPLITS)
    z, xbc, dt, q, k, v, gates = [w[:, idx[i]:idx[i + 1]] for i in range(7)]
    zeros = lambda n: jnp.zeros((w.shape[0], n), w.dtype)
    return jnp.concatenate([_pad_heads(q, 1), _pad_heads(k, 1), _pad_heads(v, 1), xbc, dt, zeros(DTW - 2 * SSD_HEADS), z, gates],
                           axis=1)


def _w_in_from_padded(w):
    z = w[:, Z0:Z0 + D_INNER]
    gates = w[:, GT0:GT0 + 2 * D_MODEL]
    q = _unpad_heads(w[:, Q0:Q0 + QP], 1)
    xbc = w[:, XBC0:XBC0 + CONV_DIM]
    k = _unpad_heads(w[:, K0:K0 + KP], 1)
    v = _unpad_heads(w[:, V0:V0 + KP], 1)
    dt = w[:, DT0:DT0 + 2 * SSD_HEADS]
    return jnp.concatenate([z, xbc, dt, q, k, v, gates], axis=1)


def _pad_lane(v):
    v = v.reshape(-1)
    return jnp.pad(v, (0, LANE - v.shape[0]))


def _layer_consts(p):
    c = dict(p)
    c["conv_w8"] = jnp.pad(p["conv_w"].reshape(SSD_CONV, CONV_DIM), ((0, 8 - SSD_CONV), (0, 0)))
    c["conv_b1"] = p["conv_b"].reshape(1, CONV_DIM)
    c["dtb"] = _pad_lane(p["dt_bias"])
    c["alog"] = _pad_lane(p["a_log"])
    c["dskip_x"] = jnp.repeat(p["d_skip"], SSD_HEAD_DIM).reshape(1, D_INNER)
    c["sink"] = _pad_lane(p["attn_sink"]).reshape(1, LANE)
    return c


def _hosted(res, n_own, hook):
    res = list(res)
    if hook is not None:
        hook[1](res[n_own:])
    return res[0] if n_own == 1 else res[:n_own]


def _layer_fwd(x, h, c, w, bias, l, hooks, next_g):
    comm = lambda k: hooks[k][0] if k in hooks else None
    s = {"x": x}
    s["h"] = _rms_fwd(x, c["pre_mix_norm"], f"pre_mix_norm_{l}", out_dtype=BF16) if h is None else h
    res = _mm(s["h"], w["w_in"], f"in_proj_{l}", out_dtype=BF16, mode="dt_side", comm=comm("in_proj"))
    proj, s["dt"] = _hosted(res, 2, hooks.get("in_proj"))
    s["proj"] = proj
    s["xbc"] = _conv_fwd(proj, c["conv_w8"], c["conv_b1"])
    s["dtt"] = _dt_rows(s["dt"])
    yf, s["stf"] = _hosted(_ssd_fwd(s["xbc"], s["dt"], s["dtt"], c["dtb"], c["alog"], False, comm=comm("ssd_fwd")), 2,
                                hooks.get("ssd_fwd"))
    s["y"], s["stb"] = _hosted(_ssd_fwd(s["xbc"], s["dt"], s["dtt"], c["dtb"], c["alog"], True, comm=comm("ssd_fwd_rev"), add=yf), 2,
                                hooks.get("ssd_fwd_rev"))
    s["u"] = _gnorm_fwd(s["y"], s["xbc"], proj, c["dskip_x"], c["ssd_norm"])
    s["y_ssd"] = _mm(s["u"], w["w_ssd_out"], f"ssd_out_{l}")
    s["o"], s["lse"] = _hosted(_attn_fwd(proj, bias, c["sink"], comm=comm("attn_fwd")), 2, hooks.get("attn_fwd"))
    s["y_attn"] = _mm(s["o"], w["w_attn_out"], f"attn_out_{l}")
    s["mix"] = _gate_fwd(proj, c["b_gate"], s["y_ssd"], s["y_attn"])
    s["mixed"] = _mm(s["mix"], w["w_o"], f"w_o_{l}")
    s["x1"], s["h2"] = _rms_fwd(s["mixed"], c["post_mix_norm"], f"post_mix_norm_{l}", res=x, next_g=c["pre_mlp_norm"])
    s["f1"], s["a"] = _mm(s["h2"], w["w_mlp_in"], f"mlp_in_{l}", mode="relu2")
    s["f"] = _mm(s["a"], w["w_mlp_out"], f"mlp_out_{l}")
    if next_g is None:
        return _rms_fwd(s["f"], c["post_mlp_norm"], f"post_mlp_norm_{l}", res=s["x1"]), None, s
    x2, h_next = _rms_fwd(s["f"], c["post_mlp_norm"], f"post_mlp_norm_{l}", res=s["x1"], next_g=next_g)
    return x2, h_next, s


def _send_rest(gw):
    s_mi = gw["w_mlp_in"].reshape(D_MODEL, N_DEV, -1).transpose(1, 0, 2).astype(BF16)
    parts = [(_unpad_heads(gw[n], 0) if n == "w_attn_out" else gw[n]).reshape(N_DEV, -1, D_MODEL) for n in ROWS]
    return [s_mi, jnp.concatenate(parts, axis=1).astype(BF16)]


def _send_w_in(g):
    return [_w_in_from_padded(g).reshape(D_MODEL, N_DEV, -1).transpose(1, 0, 2).astype(BF16)]


def _layer_bwd(dx2, s, c, w, bias, l, earlier=None, host_own=False):
    gw, gs, got = {}, {}, {}
    df, gs["post_mlp_norm"] = _rms_bwd(s["f"], c["post_mlp_norm"], dx2, f"post_mlp_norm_bwd_{l}", out_dtype=BF16)
    df1 = _mm(df, w["w_mlp_out"], f"mlp_out_dx_{l}", tb=True, mode="relu2_bwd", extra=s["f1"], out_dtype=BF16)
    gw["w_mlp_out"] = _mm(s["a"], df, f"mlp_out_dw_{l}", ta=True, out_dtype=BF16)
    gw["w_mlp_in"] = _mm(s["h2"], df1, f"mlp_in_dw_{l}", ta=True, out_dtype=BF16)
    dh2 = _mm(df1, w["w_mlp_in"], f"mlp_in_dx_{l}", tb=True, out_dtype=BF16)
    dx1, gs["pre_mlp_norm"] = _rms_bwd(s["x1"], c["pre_mlp_norm"], dh2, f"pre_mlp_norm_bwd_{l}", add=dx2)
    dmixed, gs["post_mix_norm"] = _rms_bwd(s["mixed"], c["post_mix_norm"], dx1, f"post_mix_norm_bwd_{l}", out_dtype=BF16)
    gw["w_o"] = _mm(s["mix"], dmixed, f"w_o_dw_{l}", ta=True, out_dtype=BF16)
    dmix = _mm(dmixed, w["w_o"], f"w_o_dx_{l}", tb=True, out_dtype=BF16)
    dys, dya, dproj, gs["b_gate"] = _gate_bwd(dmix, s["proj"], c["b_gate"], s["y_ssd"], s["y_attn"])
    gw["w_ssd_out"] = _mm(s["u"], dys, f"ssd_out_dw_{l}", ta=True, out_dtype=BF16)
    du = _mm(dys, w["w_ssd_out"], f"ssd_out_dx_{l}", tb=True, out_dtype=BF16)
    gw["w_attn_out"] = _mm(s["o"], dya, f"attn_out_dw_{l}", ta=True, out_dtype=BF16)
    do = _mm(dya, w["w_attn_out"], f"attn_out_dx_{l}", tb=True, out_dtype=BF16)
    res = _attn_bwd(s["proj"], bias, c["sink"], do, s["o"], s["lse"], dproj,
                    comm=None if earlier is None else _Comm("exchange", earlier[:1]))
    (dproj, dbias, dsink), got["earlier"] = res[:3], list(res[3:])
    gs["attn_sink"] = dsink
    dy, dproj, gs["ssd_norm"], gs["d_skip"] = _gnorm_bwd(du, s["y"], s["xbc"], s["proj"], c["dskip_x"], c["ssd_norm"],
                                                         dproj)
    res = _ssd_bwd(s["xbc"], s["dt"], s["dtt"], c["dtb"], c["alog"], dy, s["stf"], c["dskip_x"], False,
                   comm=None if earlier is None else _Comm("exchange", earlier[1:]))
    part, got["earlier"] = res[:3], got["earlier"] + list(res[3:])
    res = _ssd_bwd(s["xbc"], s["dt"], s["dtt"], c["dtb"], c["alog"], dy, s["stb"], c["dskip_x"], True, prev=part, dproj=dproj,
                   comm=_Comm("exchange", _send_rest(gw)) if host_own else None)
    (dxbc, dproj, sm), got["rest"] = res[:3], res[3:]
    gs["a_log"], gs["dt_bias"] = sm[0:1], sm[1:2]
    dproj, gs["conv"] = _conv_bwd(s["proj"], dxbc, c["conv_w8"], c["conv_b1"], dproj)
    gw["w_in"] = _mm(s["h"], dproj, f"in_proj_dw_{l}", ta=True, out_dtype=BF16)
    if host_own:
        dh, *got["w_in"] = _mm(dproj, w["w_in"], f"in_proj_dx_{l}", tb=True, out_dtype=BF16, comm=_Comm("exchange", _send_w_in(gw["w_in"])))
    else:
        dh = _mm(dproj, w["w_in"], f"in_proj_dx_{l}", tb=True, out_dtype=BF16)
    dx, gs["pre_mix_norm"] = _rms_bwd(s["x"], c["pre_mix_norm"], dh, f"pre_mix_norm_bwd_{l}", add=dx1)
    return dx, gw, gs, dbias, got


SMALL = (("pre_mix_norm", DEPTH * D_MODEL), ("b_gate", DEPTH * 2 * D_MODEL), ("conv_w", DEPTH * SSD_CONV * CONV_DIM),
         ("conv_b", DEPTH * CONV_DIM), ("dt_bias", DEPTH * 2 * SSD_HEADS), ("a_log", DEPTH * 2 * SSD_HEADS),
         ("d_skip", DEPTH * SSD_HEADS), ("ssd_norm", DEPTH * D_INNER), ("attn_sink", DEPTH * Q_HEADS),
         ("rel_bias_table", N_BUCKETS * Q_HEADS), ("post_mix_norm", DEPTH * D_MODEL), ("pre_mlp_norm", DEPTH * D_MODEL),
         ("post_mlp_norm", DEPTH * D_MODEL))


def _small_rows(n):
    return -(-n // LANE)


def _pack_small(vals, fill=0.0):
    rows = []
    for name, n in SMALL:
        v = vals[name].reshape(-1).astype(F32)
        rows.append(jnp.pad(v, (0, _small_rows(n) * LANE - n), constant_values=fill).reshape(-1, LANE))
    out = jnp.concatenate(rows, axis=0)
    return jnp.pad(out, ((0, -out.shape[0] % 8), (0, 0)), constant_values=fill)


def _unpack_small(packed, shapes):
    out, r = {}, 0
    for name, n in SMALL:
        nr = _small_rows(n)
        out[name] = packed[r:r + nr].reshape(-1)[:n].reshape(shapes[name])
        r += nr
    return out


BIG = ("w_in", "w_ssd_out", "w_attn_out", "w_o", "w_mlp_in", "w_mlp_out")
ROWS = ("w_ssd_out", "w_attn_out", "w_o", "w_mlp_out")


def _pack_rows(vals, lead):
    return jnp.concatenate([vals[n].reshape(lead + (-1, D_MODEL)) for n in ROWS], axis=len(lead))


def kernel(x, pre_mix_norm, w_in, b_gate, conv_w, conv_b, dt_bias, a_log, d_skip, ssd_norm, w_ssd_out, attn_sink, rel_bias_table, w_attn_out, w_o, post_mix_norm, pre_mlp_norm, w_mlp_in, w_mlp_out, post_mlp_norm, loss_target, m_pre_mix_norm, m_w_in, m_b_gate, m_conv_w, m_conv_b, m_dt_bias, m_a_log, m_d_skip, m_ssd_norm, m_w_ssd_out, m_attn_sink, m_rel_bias_table, m_w_attn_out, m_w_o, m_post_mix_norm, m_pre_mlp_norm, m_w_mlp_in, m_w_mlp_out, m_post_mlp_norm, v_pre_mix_norm, v_w_in, v_b_gate, v_conv_w, v_conv_b, v_dt_bias, v_a_log, v_d_skip, v_ssd_norm, v_w_ssd_out, v_attn_sink, v_rel_bias_table, v_w_attn_out, v_w_o, v_post_mix_norm, v_pre_mlp_norm, v_w_mlp_in, v_w_mlp_out, v_post_mlp_norm):
    names = ("pre_mix_norm", "w_in", "b_gate", "conv_w", "conv_b", "dt_bias", "a_log", "d_skip", "ssd_norm", "w_ssd_out",
             "attn_sink", "rel_bias_table", "w_attn_out", "w_o", "post_mix_norm", "pre_mlp_norm", "w_mlp_in", "w_mlp_out",
             "post_mlp_norm")
    W = dict(zip(names, (pre_mix_norm, w_in, b_gate, conv_w, conv_b, dt_bias, a_log, d_skip, ssd_norm, w_ssd_out, attn_sink,
                         rel_bias_table, w_attn_out, w_o, post_mix_norm, pre_mlp_norm, w_mlp_in, w_mlp_out, post_mlp_norm)))
    M = dict(zip(names, (m_pre_mix_norm, m_w_in, m_b_gate, m_conv_w, m_conv_b, m_dt_bias, m_a_log, m_d_skip, m_ssd_norm,
                         m_w_ssd_out, m_attn_sink, m_rel_bias_table, m_w_attn_out, m_w_o, m_post_mix_norm, m_pre_mlp_norm,
                         m_w_mlp_in, m_w_mlp_out, m_post_mlp_norm)))
    V = dict(zip(names, (v_pre_mix_norm, v_w_in, v_b_gate, v_conv_w, v_conv_b, v_dt_bias, v_a_log, v_d_skip, v_ssd_norm,
                         v_w_ssd_out, v_attn_sink, v_rel_bias_table, v_w_attn_out, v_w_o, v_post_mix_norm, v_pre_mlp_norm,
                         v_w_mlp_in, v_w_mlp_out, v_post_mlp_norm)))
    t = x.shape[1]
    shard = {n: W[n].shape for n in names}
    dev = 4 * lax.axis_index("x") + 2 * lax.axis_index("y") + lax.axis_index("c")
    cshard = CONV_DIM // N_DEV

    sh_in = [w_in[l].astype(BF16) for l in range(DEPTH)]
    sh_rest = [[w_mlp_in[l].astype(BF16), _pack_rows({n: W[n][l].astype(BF16) for n in ROWS}, ())] for l in range(DEPTH)]
    full = [{} for _ in range(DEPTH)]

    def take_w_in(l):
        def take(got):
            full[l]["w_in"] = _w_in_to_padded(got[0].transpose(1, 0, 2).reshape(D_MODEL, N_IN))
        return take

    def take_mlp_in(l):
        def take(got):
            full[l]["w_mlp_in"] = got[0].transpose(1, 0, 2).reshape(D_MODEL, D_FF)
        return take

    def take_rows(l):
        def take(got):
            r = 0
            for n in ROWS:
                per = shard[n][1]
                full[l][n] = got[0][:, r:r + per].reshape(N_DEV * per, D_MODEL)
                r += per
            full[l]["w_attn_out"] = _pad_heads(full[l]["w_attn_out"], 0)
        return take

    def take_rest(l):
        def take(got):
            take_mlp_in(l)(got[:1])
            take_rows(l)(got[1:])
        return take

    take_w_in(0)(_communicate(_Comm("gather", [sh_in[0]]), "gather_w_in_0"))
    (g_conv,) = _communicate(_Comm("gather", [conv_w.reshape(-1, LANE)]), "gather_conv_w", vmem=True)
    conv_full = g_conv.reshape(N_DEV, DEPTH, SSD_CONV, cshard).transpose(1, 2, 0, 3).reshape(DEPTH, SSD_CONV, CONV_DIM)
    hooks = [{"in_proj": (_Comm("gather", sh_rest[0]), take_rest(0)),
              "ssd_fwd": (_Comm("gather", sh_rest[1][1:]), take_rows(1)),
              "ssd_fwd_rev": (_Comm("gather", sh_rest[1][:1]), take_mlp_in(1)),
              "attn_fwd": (_Comm("gather", [sh_in[1]]), take_w_in(1))}, {}]

    _, band = _bucket_np()
    onehot = jnp.asarray(_onehot_np(), BF16)
    table_t = jnp.pad(rel_bias_table.T, ((0, 0), (0, LANE - N_BUCKETS)))
    bias = _mm(table_t, onehot, "t5_bias", tb=True, tm=Q_HEADS, tn=3 * BLOCK * BLOCK // 8, split_a=True)
    bias = (bias.reshape(Q_HEADS, BLOCK, 3 * BLOCK) + jnp.asarray(np.where(band, 0.0, NEG), F32)).reshape(
        KV_HEADS, REP * BLOCK, 3 * BLOCK)

    consts = []
    for l in range(DEPTH):
        p = {n: W[n][l] for n in names if n not in BIG and n not in ("rel_bias_table", "conv_w")}
        p["conv_w"] = conv_full[l]
        consts.append(_layer_consts(p))
    act, normed = x[0], None
    saved = []
    for l in range(DEPTH):
        next_g = consts[l + 1]["pre_mix_norm"] if l + 1 < DEPTH else None
        act, normed, s = _layer_fwd(act, normed, consts[l], full[l], bias, l, hooks[l], next_g)
        saved.append(s)
    dy, lsum = _loss(act, loss_target[0])
    loss = lax.psum(0.5 / D_MODEL * jnp.sum(lsum), ("x", "y", "c"))

    gss, dbs = [None] * DEPTH, [None] * DEPTH
    dy, gw1, gss[1], dbs[1], _ = _layer_bwd(dy, saved[1], consts[1], full[1], bias, 1)
    dy, _, gss[0], dbs[0], got = _layer_bwd(dy, saved[0], consts[0], full[0], bias, 0,
                                            earlier=_send_w_in(gw1["w_in"]) + _send_rest(gw1), host_own=True)
    recv = [list(got["w_in"]) + list(got["rest"]), list(got["earlier"])]
    grad_x = dy[None]
    dbias = jnp.concatenate([d.reshape(Q_HEADS, -1) for d in dbs], axis=1)
    d_table = _mm(dbias, jnp.concatenate([onehot] * DEPTH, axis=0), "t5_bias_bwd", tm=Q_HEADS, tk=3 * BLOCK * BLOCK // 8,
                  split_a=True)

    sg = {}
    for n in ("pre_mix_norm", "b_gate", "ssd_norm", "post_mix_norm", "pre_mlp_norm", "post_mlp_norm"):
        sg[n] = jnp.stack([gss[l][n].reshape(-1) for l in range(DEPTH)])
    sg["conv_w"] = jnp.stack([gss[l]["conv"][0:SSD_CONV] for l in range(DEPTH)])
    sg["conv_b"] = jnp.stack([gss[l]["conv"][SSD_CONV] for l in range(DEPTH)])
    sg["dt_bias"] = jnp.stack([gss[l]["dt_bias"][0, 0:2 * SSD_HEADS] for l in range(DEPTH)])
    sg["a_log"] = jnp.stack([gss[l]["a_log"][0, 0:2 * SSD_HEADS] for l in range(DEPTH)])
    sg["d_skip"] = jnp.stack([gss[l]["d_skip"][0, 0:SSD_HEADS] for l in range(DEPTH)])
    sg["attn_sink"] = jnp.stack([gss[l]["attn_sink"][0, 0:Q_HEADS] for l in range(DEPTH)])
    sg["rel_bias_table"] = d_table[:, 0:N_BUCKETS].T
    (small_parts,) = _communicate(_Comm("gather", [_pack_small(sg)]), "gather_small_grads", vmem=True)
    sshape = {n: W[n].shape for n, _ in SMALL}
    sshape["conv_w"] = (DEPTH, SSD_CONV, 1, CONV_DIM)
    pk = lambda d, fill: _pack_small({n: (jnp.full((DEPTH, SSD_CONV, CONV_DIM), fill, F32) if n == "conv_w" else d[n])
                                      for n, _ in SMALL}, fill)
    s_out = [_unpack_small(o, sshape) for o in _adamw(small_parts, pk(W, 1.0), pk(M, 1.0), pk(V, 1.0), "adamw_small", tr=1024)]
    conv_g = lax.dynamic_slice_in_dim(s_out[0]["conv_w"], dev * cshard, cshard, axis=3)
    c_out = _adamw(conv_g.reshape(1, DEPTH * SSD_CONV, cshard), conv_w.reshape(-1, cshard), m_conv_w.reshape(-1, cshard),
                   v_conv_w.reshape(-1, cshard), "adamw_conv_w", tr=DEPTH * SSD_CONV)
    for i in range(4):
        s_out[i]["conv_w"] = c_out[i].reshape(conv_w.shape)

    b_out = {n: None for n in BIG}
    for l in (1, 0):
        r_in, r_mi, r_rows = recv[l]
        b_out["w_in"] = _adamw_layer(r_in, 0, w_in, m_w_in, v_w_in, l, b_out["w_in"], f"adamw_w_in_{l}", tr=256)
        b_out["w_mlp_in"] = _adamw_layer(r_mi, 0, w_mlp_in, m_w_mlp_in, v_w_mlp_in, l, b_out["w_mlp_in"], f"adamw_w_mlp_in_{l}",
                                         tr=256)
        r = 0
        for n in ROWS:
            b_out[n] = _adamw_layer(r_rows, r, W[n], M[n], V[n], l, b_out[n], f"adamw_{n}_{l}")
            r += shard[n][1]

    outs = [loss, grad_x]
    for i in range(4):
        for n in names:
            outs.append(b_out[n][i] if n in BIG else s_out[i][n])
    return tuple(outs)
```

```python
import functools
import math

import numpy as np
import jax
import jax.numpy as jnp
from jax import lax
from jax.experimental import pallas as pl
from jax.experimental.pallas import tpu as pltpu

F32 = jnp.float32
BF16 = jnp.bfloat16

D_MODEL = 1024
DEPTH = 2
D_INNER = 2048
SSD_HEADS = 32
SSD_HEAD_DIM = 64
SSD_GROUPS = 8
SSD_REP = 4
SSD_STATE = 128
SSD_CONV = 5
CHUNK = 128
CONV_DIM = 4096
Q_HEADS = 16
KV_HEADS = 4
HEAD_DIM = 64
ATTN_WIDTH = 1024
KV_WIDTH = 256
WINDOW = 128
BLOCK = 128
N_BUCKETS = 32
MAX_DISTANCE = 128
D_FF = 4096
EPS = 1e-6
IN_SPLITS = (D_INNER, CONV_DIM, 2 * SSD_HEADS, ATTN_WIDTH, KV_WIDTH, KV_WIDTH, 2 * D_MODEL)
N_IN = sum(IN_SPLITS)
N_DEV = 8

ADAM_LR = 0.001
ADAM_B1 = 0.9
ADAM_B2 = 0.999
ADAM_EPS = 1e-08
ADAM_WD = 0.01
ADAM_STEP = 10

LANE = 128
HEAD_PAD = LANE
QP = Q_HEADS * HEAD_PAD
KP = KV_HEADS * HEAD_PAD
QKV = QP + 2 * KP
Q0 = 0
K0 = QP
V0 = QP + KP
XBC0 = QKV
DT0 = XBC0 + CONV_DIM
DTW = 1024
Z0 = DT0 + DTW
GT0 = Z0 + D_INNER
NP = GT0 + 2 * D_MODEL
NEG = -1e30
VMEM_LIMIT = 56 * 1024 * 1024


def _params(n_grid):
    return pltpu.CompilerParams(dimension_semantics=("arbitrary",) * n_grid, vmem_limit_bytes=VMEM_LIMIT)


def _dot(a, b, dims=((1,), (0,))):
    return lax.dot_general(a, b, (dims, ((), ())), preferred_element_type=F32)


def _mx(a):
    return a.astype(BF16)


def _split_dot(x, e):
    hi = x.astype(BF16)
    lo = (x - hi.astype(F32)).astype(BF16)
    return _dot(hi, e) + _dot(lo, e)


def _softplus(x):
    u = jnp.exp(-jnp.abs(x))
    w = 1.0 + u
    log1p = jnp.where(w == 1.0, u, jnp.log(w) * u / jnp.where(w == 1.0, 1.0, w - 1.0))
    return jnp.maximum(x, 0.0) + log1p


def _sigmoid(x):
    return 1.0 / (1.0 + jnp.exp(-x))


def _mm_body(*refs, grid, ta, tb, split_a, mode, comm):
    n_in = 3 if mode == "relu2_bwd" else 2
    n_out = 2 if mode in ("relu2", "dt_side") else 1
    ins, cin, outs, cout, (acc_ref,), csem = _split_refs(refs, n_in, n_out, comm)
    a_ref, b_ref = ins[0], ins[1]
    i, j, k = pl.program_id(0), pl.program_id(1), pl.program_id(2)
    nk = grid[2]
    if comm is not None:
        comm.run(cin, cout, csem, (i * grid[1] + j) * nk + k, grid[0] * grid[1] * nk)

    dims = ((0 if ta else 1,), (1 if tb else 0,))
    b = _mx(b_ref[...])
    if split_a:
        a32 = a_ref[...].astype(F32)
        hi = a32.astype(BF16)
        lo = (a32 - hi.astype(F32)).astype(BF16)
        part = _dot(hi, b, dims) + _dot(lo, b, dims)
    else:
        part = _dot(_mx(a_ref[...]), b, dims)

    if nk > 1:
        @pl.when(k == 0)
        def _():
            acc_ref[...] = jnp.zeros_like(acc_ref)

        acc_ref[...] += part

    @pl.when(k == nk - 1)
    def _():
        acc = part if nk == 1 else acc_ref[...]
        if mode == "relu2":
            outs[0][...] = acc.astype(outs[0].dtype)
            r = jnp.maximum(acc, 0.0)
            outs[1][...] = (r * r).astype(outs[1].dtype)
        elif mode == "relu2_bwd":
            outs[0][...] = (acc * 2.0 * jnp.maximum(ins[2][...].astype(F32), 0.0)).astype(outs[0].dtype)
        else:
            outs[0][...] = acc.astype(outs[0].dtype)
            if mode == "dt_side":
                @pl.when(j == DT0 // outs[0].shape[1])
                def _():
                    off = DT0 % outs[0].shape[1]
                    outs[1][...] = acc[:, off:off + LANE]


def _tile(n, pref):
    t = min(n, pref)
    while n % t:
        t -= LANE
    assert t > 0 and n % t == 0, (n, pref)
    return t


MM_VMEM_BUDGET = 42 * 1024 * 1024


def _mm_tiles(m, n, k, a_bytes, b_bytes, out_bytes, extra_bytes):
    tm = _tile(m, 1024)
    best = None
    for tn in (2048, 1024):
        if n % tn:
            continue
        for nk in range(1, k // LANE + 1):
            if k % nk or (k // nk) % LANE:
                continue
            tk = k // nk
            need = 2 * (tm * tk * a_bytes + tk * tn * b_bytes + tm * tn * (out_bytes + extra_bytes)) + (tm * tn * 4 if nk > 1 else 0)
            if need <= MM_VMEM_BUDGET:
                if best is None or nk < best[0]:
                    best = (nk, tn, tk)
                break
    assert best is not None, (m, n, k)
    return tm, best[1], best[2]


def _mm(a, b, name, ta=False, tb=False, out_dtype=F32, tm=None, tn=None, tk=None, split_a=False, mode=None, extra=None,
        comm=None):
    m, k = (a.shape[1], a.shape[0]) if ta else a.shape
    n = b.shape[0] if tb else b.shape[1]
    assert (b.shape[1] if tb else b.shape[0]) == k
    if tm is None:
        side = {"relu2": 2, "relu2_bwd": extra.dtype.itemsize if extra is not None else 0}.get(mode, 0)
        tm, tn, tk = _mm_tiles(m, n, k, a.dtype.itemsize, b.dtype.itemsize, jnp.dtype(out_dtype).itemsize, side)
    else:
        tm, tn, tk = _tile(m, tm), _tile(n, tn or 2048), _tile(k, tk or 1024)
    grid = (m // tm, n // tn, k // tk)
    a_spec = pl.BlockSpec((tk, tm), lambda i, j, kk: (kk, i)) if ta else pl.BlockSpec((tm, tk), lambda i, j, kk: (i, kk))
    b_spec = pl.BlockSpec((tn, tk), lambda i, j, kk: (j, kk)) if tb else pl.BlockSpec((tk, tn), lambda i, j, kk: (kk, j))
    o_spec = pl.BlockSpec((tm, tn), lambda i, j, kk: (i, j))
    in_specs, args = [a_spec, b_spec], [a, b]
    out_specs, out_shape = [o_spec], [jax.ShapeDtypeStruct((m, n), out_dtype)]
    if mode == "relu2":
        out_specs, out_shape = [o_spec, o_spec], [jax.ShapeDtypeStruct((m, n), BF16), jax.ShapeDtypeStruct((m, n), BF16)]
    elif mode == "dt_side":
        assert DT0 % tn + LANE <= tn
        out_specs = [o_spec, pl.BlockSpec((tm, LANE), lambda i, j, kk: (i, 0))]
        out_shape = out_shape + [jax.ShapeDtypeStruct((m, LANE), F32)]
    elif mode == "relu2_bwd":
        in_specs, args = in_specs + [o_spec], args + [extra]
    scratch = [pltpu.VMEM((tm, tn) if grid[2] > 1 else (8, LANE), F32)]
    if comm is not None:
        in_specs, args = in_specs + [ANY] * comm.n, args + comm.arrs
        out_specs, out_shape = out_specs + [ANY] * comm.n, out_shape + comm.out_shapes()
        scratch = scratch + comm.sems()
    res = pl.pallas_call(
        functools.partial(_mm_body, grid=grid, ta=ta, tb=tb, split_a=split_a, mode=mode, comm=comm),
        name=name, grid=grid, in_specs=in_specs, out_specs=out_specs, out_shape=out_shape, scratch_shapes=scratch,
        compiler_params=_params(3),
    )(*args)
    return res[0] if len(res) == 1 else res


def _rms_fwd_body(*refs, has_res, has_next):
    x_ref, g_ref = refs[:2]
    x = x_ref[...]
    y = x * lax.rsqrt(jnp.mean(x * x, axis=-1, keepdims=True) + EPS) * g_ref[...]
    if has_res:
        y = refs[2][...] + y
    if has_next:
        o_ref, h_ref = refs[-2:]
        g2_ref = refs[3 if has_res else 2]
        h_ref[...] = (y * lax.rsqrt(jnp.mean(y * y, axis=-1, keepdims=True) + EPS) * g2_ref[...]).astype(h_ref.dtype)
    else:
        o_ref = refs[-1]
    o_ref[...] = y.astype(o_ref.dtype)


def _rms_fwd(x, g, name, res=None, out_dtype=F32, next_g=None, tr=512):
    t, d = x.shape
    tr = min(tr, t)
    row = pl.BlockSpec((tr, d), lambda i: (i, 0))
    vec = pl.BlockSpec((1, d), lambda i: (0, 0))
    args = [x, g.reshape(1, d)] + ([res] if res is not None else []) + ([next_g.reshape(1, d)] if next_g is not None else [])
    out_specs, out_shape = [row], [jax.ShapeDtypeStruct((t, d), out_dtype)]
    if next_g is not None:
        out_specs, out_shape = [row, row], out_shape + [jax.ShapeDtypeStruct((t, d), BF16)]
    res_ = pl.pallas_call(
        functools.partial(_rms_fwd_body, has_res=res is not None, has_next=next_g is not None),
        name=name,
        grid=(t // tr,),
        in_specs=[row, vec] + ([row] if res is not None else []) + ([vec] if next_g is not None else []),
        out_specs=out_specs,
        out_shape=out_shape,
        compiler_params=_params(1),
    )(*args)
    return res_ if next_g is not None else res_[0]


def _rms_bwd_body(*refs, has_add):
    if has_add:
        x_ref, g_ref, dy_ref, add_ref, dx_ref, dg_ref = refs
    else:
        x_ref, g_ref, dy_ref, dx_ref, dg_ref = refs

    @pl.when(pl.program_id(0) == 0)
    def _():
        dg_ref[...] = jnp.zeros_like(dg_ref)

    x = x_ref[...]
    dy = dy_ref[...].astype(F32)
    rstd = lax.rsqrt(jnp.mean(x * x, axis=-1, keepdims=True) + EPS)
    n = x * rstd
    dn = dy * g_ref[...]
    dx = rstd * (dn - n * jnp.mean(dn * n, axis=-1, keepdims=True))
    if has_add:
        dx = dx + add_ref[...]
    dx_ref[...] = dx.astype(dx_ref.dtype)
    dg_ref[...] += jnp.sum(dy * n, axis=0, keepdims=True)


def _rms_bwd(x, g, dy, name, add=None, out_dtype=F32, tr=512):
    t, d = x.shape
    tr = min(tr, t)
    row = pl.BlockSpec((tr, d), lambda i: (i, 0))
    vec = pl.BlockSpec((1, d), lambda i: (0, 0))
    args = [x, g.reshape(1, d), dy] + ([add] if add is not None else [])
    return pl.pallas_call(
        functools.partial(_rms_bwd_body, has_add=add is not None),
        name=name,
        grid=(t // tr,),
        in_specs=[row, vec, row] + ([row] if add is not None else []),
        out_specs=[row, vec],
        out_shape=[jax.ShapeDtypeStruct((t, d), out_dtype), jax.ShapeDtypeStruct((1, d), F32)],
        compiler_params=_params(1),
    )(*args)


def _loss_body(y_ref, t_ref, dy_ref, l_ref):
    @pl.when(pl.program_id(0) == 0)
    def _():
        l_ref[...] = jnp.zeros_like(l_ref)

    e = y_ref[...] - t_ref[...]
    dy_ref[...] = e * (1.0 / D_MODEL)
    s = jnp.sum(e * e, axis=0, keepdims=True)
    acc = s[:, 0:LANE]
    for j in range(1, D_MODEL // LANE):
        acc = acc + s[:, j * LANE:(j + 1) * LANE]
    l_ref[...] += acc


def _loss(y, target, tr=512):
    t, d = y.shape
    tr = min(tr, t)
    row = pl.BlockSpec((tr, d), lambda i: (i, 0))
    return pl.pallas_call(
        functools.partial(_loss_body), name="loss_head", grid=(t // tr,), in_specs=[row, row],
        out_specs=[row, pl.BlockSpec((1, LANE), lambda i: (0, 0))],
        out_shape=[jax.ShapeDtypeStruct((t, d), F32), jax.ShapeDtypeStruct((1, LANE), F32)],
        compiler_params=_params(1),
    )(y, target)


CONV_CT = 128
CONV_RB = 128
CONV_PAD = 8


def _conv_rows(t):
    return min(CONV_RB, t)


def _conv_fwd_body(u_ref, w_ref, b_ref, o_ref, pad_ref, *, t):
    rb = _conv_rows(t)
    zeros = jnp.zeros((CONV_PAD, CONV_CT), F32)
    pad_ref[pl.ds(0, CONV_PAD), :] = zeros
    pad_ref[pl.ds(t + CONV_PAD, CONV_PAD), :] = zeros
    pad_ref[pl.ds(CONV_PAD, t), :] = u_ref[...].astype(F32)
    w = w_ref[...]
    b = b_ref[...]
    for c in range(t // rb):
        base = CONV_PAD + c * rb
        acc = b + w[0:1, :] * pad_ref[pl.ds(base - 2, rb), :]
        for k in range(1, SSD_CONV):
            acc = acc + w[k:k + 1, :] * pad_ref[pl.ds(base + k - 2, rb), :]
        o_ref[pl.ds(c * rb, rb), :] = acc * _sigmoid(acc)


def _conv_fwd(proj, conv_w8, conv_b):
    t = proj.shape[0]
    off = XBC0 // CONV_CT
    return pl.pallas_call(
        functools.partial(_conv_fwd_body, t=t), name="conv_fwd", grid=(CONV_DIM // CONV_CT,),
        in_specs=[pl.BlockSpec((t, CONV_CT), lambda i: (0, off + i)), pl.BlockSpec((8, CONV_CT), lambda i: (0, i)),
                  pl.BlockSpec((1, CONV_CT), lambda i: (0, i))],
        out_specs=pl.BlockSpec((t, CONV_CT), lambda i: (0, i)),
        out_shape=jax.ShapeDtypeStruct((t, CONV_DIM), F32),
        scratch_shapes=[pltpu.VMEM((t + 2 * CONV_PAD, CONV_CT), F32)],
        compiler_params=_params(1),
    )(proj, conv_w8, conv_b)


def _conv_bwd_body(u_ref, da_ref, w_ref, b_ref, _, du_ref, dw_ref, pad_ref, pad2_ref, *, t):
    rb = _conv_rows(t)
    zeros = jnp.zeros((CONV_PAD, CONV_CT), F32)
    for p in (pad_ref, pad2_ref):
        p[pl.ds(0, CONV_PAD), :] = zeros
        p[pl.ds(t + CONV_PAD, CONV_PAD), :] = zeros
    pad_ref[pl.ds(CONV_PAD, t), :] = u_ref[...].astype(F32)
    w = w_ref[...]
    b = b_ref[...]
    dw = [jnp.zeros((1, CONV_CT), F32) for _ in range(SSD_CONV + 1)]
    for c in range(t // rb):
        base = CONV_PAD + c * rb
        us = [pad_ref[pl.ds(base + k - 2, rb), :] for k in range(SSD_CONV)]
        acc = b + w[0:1, :] * us[0]
        for k in range(1, SSD_CONV):
            acc = acc + w[k:k + 1, :] * us[k]
        sg = _sigmoid(acc)
        dyc = da_ref[pl.ds(c * rb, rb), :] * (sg * (1.0 + acc * (1.0 - sg)))
        pad2_ref[pl.ds(base, rb), :] = dyc
        for k in range(SSD_CONV):
            dw[k] = dw[k] + jnp.sum(dyc * us[k], axis=0, keepdims=True)
        dw[SSD_CONV] = dw[SSD_CONV] + jnp.sum(dyc, axis=0, keepdims=True)
    for c in range(t // rb):
        base = CONV_PAD + c * rb
        acc = w[0:1, :] * pad2_ref[pl.ds(base + 2, rb), :]
        for k in range(1, SSD_CONV):
            acc = acc + w[k:k + 1, :] * pad2_ref[pl.ds(base + 2 - k, rb), :]
        du_ref[pl.ds(c * rb, rb), :] = acc.astype(du_ref.dtype)
    dw_ref[...] = jnp.concatenate(dw + [jnp.zeros((2, CONV_CT), F32)], axis=0)


def _conv_bwd(proj, dact, conv_w8, conv_b, dproj):
    t = proj.shape[0]
    off = XBC0 // CONV_CT
    col = pl.BlockSpec((t, CONV_CT), lambda i: (0, i))
    seg = pl.BlockSpec((t, CONV_CT), lambda i: (0, off + i))
    return pl.pallas_call(
        functools.partial(_conv_bwd_body, t=t), name="conv_bwd", grid=(CONV_DIM // CONV_CT,),
        in_specs=[seg, col, pl.BlockSpec((8, CONV_CT), lambda i: (0, i)), pl.BlockSpec((1, CONV_CT), lambda i: (0, i)), ANY],
        out_specs=[seg, pl.BlockSpec((8, CONV_CT), lambda i: (0, i))],
        out_shape=[jax.ShapeDtypeStruct((t, NP), BF16), jax.ShapeDtypeStruct((8, CONV_DIM), F32)],
        scratch_shapes=[pltpu.VMEM((t + 2 * CONV_PAD, CONV_CT), F32), pltpu.VMEM((t + 2 * CONV_PAD, CONV_CT), F32)],
        input_output_aliases={4: 0}, compiler_params=_params(1),
    )(proj, dact, conv_w8, conv_b, dproj)


GW = SSD_REP * SSD_HEAD_DIM


def _expand_np(hoff):
    e = np.zeros((LANE, D_INNER), np.float32)
    for h in range(SSD_HEADS):
        e[hoff + h, h * SSD_HEAD_DIM:(h + 1) * SSD_HEAD_DIM] = 1.0
    return e


def _head_masks():
    lane = lax.broadcasted_iota(jnp.int32, (1, GW), 1)
    return [((lane >= r * SSD_HEAD_DIM) & (lane < (r + 1) * SSD_HEAD_DIM)).astype(F32) for r in range(SSD_REP)]


def _ssd_common(dtc_ref, dtr_ref, bc_ref, br_ref, alc_ref, alr_ref, e_ref, rev):
    L = CHUNK
    ri = lax.broadcasted_iota(jnp.int32, (L, L), 0)
    ci = lax.broadcasted_iota(jnp.int32, (L, L), 1)
    tri = (ri <= ci) if rev else (ri >= ci)
    trit = (ri >= ci) if rev else (ri <= ci)
    raw_c = dtc_ref[...] + bc_ref[...]
    dt_c = _softplus(raw_c)
    a_c = -jnp.exp(alc_ref[...])
    cs_c = jnp.dot(tri.astype(F32), dt_c * a_c, precision=lax.Precision.HIGHEST, preferred_element_type=F32)
    dt_r = _softplus(dtr_ref[...] + br_ref[...])
    a_r = -jnp.exp(alr_ref[...])
    cs_r = jnp.dot(dt_r * a_r, trit.astype(F32), precision=lax.Precision.HIGHEST, preferred_element_type=F32)
    il = 0 if rev else L - 1
    e_c = jnp.exp(cs_c)
    w_c = jnp.exp(cs_c[il:il + 1, :] - cs_c)
    ex = _dot(_mx(jnp.concatenate([dt_c, e_c, w_c], axis=0)), e_ref[...])
    return dict(tri=tri, trit=trit, raw_c=raw_c, dt_c=dt_c, a_c=a_c, cs_c=cs_c, cs_r=cs_r, il=il, e_last=e_c[il:il + 1],
                dt_x=ex[0:L], e_x=ex[L:2 * L], w_x=ex[2 * L:3 * L], cd_x=ex[L + il:L + il + 1])


def _ssd_fwd_body(*refs, rev, hoff, nsteps, comm, has_add):
    L = CHUNK
    ins, cin, (y_ref, st_ref), cout, (s_scr,), csem = _split_refs(refs, 11 if has_add else 10, 2, comm)
    xs_ref, b_ref, c_ref, dtc_ref, dtr_ref, bc_ref, br_ref, alc_ref, alr_ref, e_ref = ins[:10]
    if comm is not None:
        comm.run(cin, cout, csem, pl.program_id(0), nsteps)

    @pl.when(pl.program_id(0) == 0)
    def _():
        s_scr[...] = jnp.zeros_like(s_scr)

    st_ref[0] = s_scr[...]
    q = _ssd_common(dtc_ref, dtr_ref, bc_ref, br_ref, alc_ref, alr_ref, e_ref, rev)
    masks = _head_masks()
    xdt = xs_ref[...] * q["dt_x"]
    xw = xdt * q["w_x"]
    early = []
    for g in range(SSD_GROUPS):
        gc = slice(g * GW, (g + 1) * GW)
        nc = slice(g * SSD_STATE, (g + 1) * SSD_STATE)
        bg = _mx(b_ref[:, nc])
        cg = _mx(c_ref[:, nc])
        sg = s_scr[:, gc]
        early.append((_dot(cg, bg, ((1,), (1,))), _dot(cg, _mx(sg)) * q["e_x"][:, gc]))
        s_scr[:, gc] = sg * q["cd_x"][:, gc] + _dot(bg, _mx(xw[:, gc]), ((0,), (0,)))
    for g in range(SSD_GROUPS):
        gc = slice(g * GW, (g + 1) * GW)
        cb, y = early[g]
        ms = []
        for r in range(SSD_REP):
            hh = hoff + g * SSD_REP + r
            diff = q["cs_c"][:, hh:hh + 1] - q["cs_r"][hh:hh + 1, :]
            ms.append(_mx(cb * jnp.exp(jnp.where(q["tri"], diff, NEG))))
        y4 = _dot(jnp.concatenate(ms, axis=0), _mx(xdt[:, gc]))
        for r in range(SSD_REP):
            y = y + y4[r * L:(r + 1) * L] * masks[r]
        y_ref[:, gc] = y + ins[10][:, gc] if has_add else y


def _dt_rows_body(p_ref, o_ref):
    o_ref[...] = p_ref[...].T


def _dt_rows(proj, tr=512):
    t = proj.shape[0]
    tr = min(tr, t)
    return pl.pallas_call(
        functools.partial(_dt_rows_body), name="dt_rows", grid=(t // tr,),
        in_specs=[pl.BlockSpec((tr, LANE), lambda i: (i, 0))], out_specs=pl.BlockSpec((LANE, tr), lambda i: (0, i)),
        out_shape=jax.ShapeDtypeStruct((LANE, t), F32), compiler_params=_params(1),
    )(proj)


def _ssd_specs(t, rev):
    nc = t // CHUNK
    cm = (lambda i: nc - 1 - i) if rev else (lambda i: i)
    xs = pl.BlockSpec((CHUNK, D_INNER), lambda i: (cm(i), 0))
    bb = pl.BlockSpec((CHUNK, SSD_GROUPS * SSD_STATE), lambda i: (cm(i), 2))
    cc = pl.BlockSpec((CHUNK, SSD_GROUPS * SSD_STATE), lambda i: (cm(i), 3))
    dtc = pl.BlockSpec((CHUNK, LANE), lambda i: (cm(i), 0))
    dtr = pl.BlockSpec((LANE, CHUNK), lambda i: (0, cm(i)))
    colv = pl.BlockSpec((1, LANE), lambda i: (0, 0))
    rowv = pl.BlockSpec((LANE, 1), lambda i: (0, 0))
    return cm, xs, bb, cc, dtc, dtr, colv, rowv


def _ssd_fwd(xbc, proj, dtt, bias, alog, rev, comm=None, add=None):
    t = xbc.shape[0]
    nc = t // CHUNK
    cm, xs, bb, cc, dtc, dtr, colv, rowv = _ssd_specs(t, rev)
    hoff = SSD_HEADS if rev else 0
    e = jnp.asarray(_expand_np(hoff), BF16)
    in_specs = [xs, bb, cc, dtc, dtr, colv, rowv, colv, rowv, pl.BlockSpec((LANE, D_INNER), lambda i: (0, 0))]
    args = [xbc, xbc, xbc, proj, dtt, bias.reshape(1, LANE), bias.reshape(LANE, 1), alog.reshape(1, LANE), alog.reshape(LANE, 1), e]
    out_specs = [xs, pl.BlockSpec((1, SSD_STATE, D_INNER), lambda i: (cm(i), 0, 0))]
    out_shape = [jax.ShapeDtypeStruct((t, D_INNER), F32), jax.ShapeDtypeStruct((nc, SSD_STATE, D_INNER), F32)]
    scratch = [pltpu.VMEM((SSD_STATE, D_INNER), F32)]
    if add is not None:
        in_specs, args = in_specs + [xs], args + [add]
    if comm is not None:
        in_specs, args = in_specs + [ANY] * comm.n, args + comm.arrs
        out_specs, out_shape = out_specs + [ANY] * comm.n, out_shape + comm.out_shapes()
        scratch = scratch + comm.sems()
    return pl.pallas_call(
        functools.partial(_ssd_fwd_body, rev=rev, hoff=hoff, nsteps=nc, comm=comm, has_add=add is not None), name="ssd_fwd_rev" if rev else "ssd_fwd",
        grid=(nc,), in_specs=in_specs, out_specs=out_specs, out_shape=out_shape, scratch_shapes=scratch,
        compiler_params=_params(1),
    )(*args)


def _ssd_bwd_body(*refs, rev, hoff, first, nsteps, comm):
    L = CHUNK
    ins, cin, (dx_ref, ddt_ref, sm_ref), cout, (ds_scr,), csem = _split_refs(refs, 14 if first else 18, 3, comm)
    (xs_ref, b_ref, c_ref, dtc_ref, dtr_ref, bc_ref, br_ref, alc_ref, alr_ref, e_ref, et_ref, dy_ref, sp_ref,
     dsk_ref) = ins[:14]
    if not first:
        pdx_ref, pddt_ref, psm_ref = ins[14:17]
    if comm is not None:
        comm.run(cin, cout, csem, pl.program_id(0), nsteps)

    @pl.when(pl.program_id(0) == 0)
    def _():
        ds_scr[...] = jnp.zeros_like(ds_scr)
        sm_ref[...] = jnp.zeros_like(sm_ref) if first else psm_ref[...]

    q = _ssd_common(dtc_ref, dtr_ref, bc_ref, br_ref, alc_ref, alr_ref, e_ref, rev)
    masks = _head_masks()
    xs = xs_ref[...]
    dy = dy_ref[...]
    xdt = xs * q["dt_x"]
    xw = xdt * q["w_x"]
    dye = dy * q["e_x"]
    ds_old = ds_scr[...]
    dxdt_parts, q1_parts, q2_parts = [], [], []
    ddiag = jnp.zeros((L, LANE), F32)
    early = []
    for g in range(SSD_GROUPS):
        gc = slice(g * GW, (g + 1) * GW)
        nc = slice(g * SSD_STATE, (g + 1) * SSD_STATE)
        bg = _mx(b_ref[:, nc])
        cg = _mx(c_ref[:, nc])
        dy_g = _mx(dy[:, gc])
        xdt_g = _mx(xdt[:, gc])
        dsg = ds_old[:, gc]
        spg = _mx(sp_ref[0, :, gc])
        dye_g = _mx(dye[:, gc])
        bds = _dot(bg, _mx(dsg))
        early.append(dict(
            bg=bg, cg=cg, dy_g=dy_g, bds=bds,
            cb=_dot(cg, bg, ((1,), (1,))), cbt=_dot(bg, cg, ((1,), (1,))),
            dm=_dot(_mx(jnp.concatenate([dy[:, gc] * m for m in masks], axis=0)), xdt_g, ((1,), (1,))),
            dmt=_dot(_mx(jnp.concatenate([xdt[:, gc] * m for m in masks], axis=0)), dy_g, ((1,), (1,))),
            dc=_dot(dye_g, spg, ((1,), (1,))), db=_dot(_mx(xw[:, gc]), _mx(dsg), ((1,), (1,)))))
        q1_parts.append(dye[:, gc] * _dot(cg, spg))
        q2_parts.append(xw[:, gc] * bds)
        ds_scr[:, gc] = dsg * q["cd_x"][:, gc] + _dot(cg, dye_g, ((0,), (0,)))
    for g in range(SSD_GROUPS):
        gc = slice(g * GW, (g + 1) * GW)
        e = early[g]
        bg, cg, dy_g, bds, cb, cbt, dm, dmt = (e[k] for k in ("bg", "cg", "dy_g", "bds", "cb", "cbt", "dm", "dmt"))
        dcb = jnp.zeros((L, L), F32)
        dcbt = jnp.zeros((L, L), F32)
        mts = []
        for r in range(SSD_REP):
            hh = hoff + g * SSD_REP + r
            col = q["cs_c"][:, hh:hh + 1]
            row = q["cs_r"][hh:hh + 1, :]
            dec = jnp.exp(jnp.where(q["tri"], col - row, NEG))
            dect = jnp.exp(jnp.where(q["trit"], row - col, NEG))
            pd = dm[r * L:(r + 1) * L] * dec
            pdt = dmt[r * L:(r + 1) * L] * dect
            dcb = dcb + pd
            dcbt = dcbt + pdt
            mts.append(_mx(cbt * dect))
            ddiag = ddiag + (jnp.sum(pd * cb, axis=1, keepdims=True) - jnp.sum(pdt * cbt, axis=1, keepdims=True)) * _onehot_lane(hh)
        x4 = _dot(jnp.concatenate(mts, axis=0), dy_g)
        dxdt_g = bds * q["w_x"][:, gc]
        for r in range(SSD_REP):
            dxdt_g = dxdt_g + x4[r * L:(r + 1) * L] * masks[r]
        dxdt_parts.append(dxdt_g)
        dc_g = _dot(_mx(dcb), bg) + e["dc"]
        db_g = _dot(_mx(dcbt), cg) + e["db"]
        boff = D_INNER + g * SSD_STATE
        coff = D_INNER + SSD_GROUPS * SSD_STATE + g * SSD_STATE
        if first:
            dx_ref[:, boff:boff + SSD_STATE] = db_g
            dx_ref[:, coff:coff + SSD_STATE] = dc_g
        else:
            dx_ref[:, boff:boff + SSD_STATE] = pdx_ref[:, boff:boff + SSD_STATE] + db_g
            dx_ref[:, coff:coff + SSD_STATE] = pdx_ref[:, coff:coff + SSD_STATE] + dc_g
    dxdt = jnp.concatenate(dxdt_parts, axis=1)
    et = et_ref[...]
    hs = _dot(_mx(jnp.concatenate([jnp.concatenate(q1_parts, axis=1), jnp.concatenate(q2_parts, axis=1), dxdt * xs], axis=0)), et)
    q1, q2, r3 = hs[0:L], hs[L:2 * L], hs[2 * L:3 * L]
    t_prev = _split_dot(jnp.sum(ds_old * sp_ref[0], axis=0, keepdims=True), et) * q["e_last"]
    rows = lax.broadcasted_iota(jnp.int32, (L, LANE), 0)
    dcs = ddiag + q1 - q2 + jnp.where(rows == q["il"], jnp.sum(q2, axis=0, keepdims=True) + t_prev, 0.0)
    dad = jnp.dot(q["trit"].astype(F32), dcs, precision=lax.Precision.HIGHEST, preferred_element_type=F32)
    ddt = dad * q["a_c"] + r3
    ddt_raw = ddt * _sigmoid(q["raw_c"])
    dal = jnp.sum(dad * q["dt_c"], axis=0, keepdims=True) * q["a_c"]
    dbias = jnp.sum(ddt_raw, axis=0, keepdims=True)
    sm_ref[...] += jnp.concatenate([dal, dbias, jnp.zeros((6, LANE), F32)], axis=0)
    dxs = dxdt * q["dt_x"]
    if first:
        dx_ref[:, 0:D_INNER] = dxs + dy * dsk_ref[...]
        ddt_ref[...] = ddt_raw
    else:
        dx_ref[:, 0:D_INNER] = pdx_ref[:, 0:D_INNER] + dxs
        ddt_ref[...] = jnp.concatenate([(pddt_ref[...] + ddt_raw).astype(ddt_ref.dtype), jnp.zeros((L, DTW - LANE), ddt_ref.dtype)],
                                       axis=1)


def _ssd_bwd(xbc, proj, dtt, bias, alog, dy, states, dskip_x, rev, prev=None, dproj=None, comm=None):
    t = xbc.shape[0]
    nc = t // CHUNK
    first = prev is None
    cm, xs, bb, cc, dtc, dtr, colv, rowv = _ssd_specs(t, not rev)
    hoff = SSD_HEADS if rev else 0
    e_np = _expand_np(hoff)
    e = jnp.asarray(e_np, BF16)
    et = jnp.asarray(e_np.T, BF16)
    st_spec = pl.BlockSpec((1, SSD_STATE, D_INNER), lambda i: (cm(i), 0, 0))
    dxo = pl.BlockSpec((CHUNK, CONV_DIM), lambda i: (cm(i), 0))
    ddto = pl.BlockSpec((CHUNK, LANE), lambda i: (cm(i), 0))
    smo = pl.BlockSpec((8, LANE), lambda i: (0, 0))
    in_specs = [xs, bb, cc, dtc, dtr, colv, rowv, colv, rowv, pl.BlockSpec((LANE, D_INNER), lambda i: (0, 0)),
                pl.BlockSpec((D_INNER, LANE), lambda i: (0, 0)), xs, st_spec, pl.BlockSpec((1, D_INNER), lambda i: (0, 0))]
    args = [xbc, xbc, xbc, proj, dtt, bias.reshape(1, LANE), bias.reshape(LANE, 1), alog.reshape(1, LANE),
            alog.reshape(LANE, 1), e, et, dy, states, dskip_x]
    out_specs = [dxo, ddto, smo]
    out_shape = [jax.ShapeDtypeStruct((t, CONV_DIM), F32), jax.ShapeDtypeStruct((t, LANE), F32),
                 jax.ShapeDtypeStruct((8, LANE), F32)]
    aliases = {}
    if not first:
        in_specs += [dxo, ddto, smo, ANY]
        args += list(prev) + [dproj]
        out_specs[1] = pl.BlockSpec((CHUNK, DTW), lambda i: (cm(i), DT0 // DTW))
        out_shape[1] = jax.ShapeDtypeStruct((t, NP), BF16)
        aliases = {17: 1}
    scratch = [pltpu.VMEM((SSD_STATE, D_INNER), F32)]
    if comm is not None:
        in_specs, args = in_specs + [ANY] * comm.n, args + comm.arrs
        out_specs, out_shape = out_specs + [ANY] * comm.n, out_shape + comm.out_shapes()
        scratch = scratch + comm.sems()
    return pl.pallas_call(
        functools.partial(_ssd_bwd_body, rev=rev, hoff=hoff, first=first, nsteps=nc, comm=comm),
        name="ssd_bwd_rev" if rev else "ssd_bwd", grid=(nc,), in_specs=in_specs, out_specs=out_specs, out_shape=out_shape,
        scratch_shapes=scratch, input_output_aliases=aliases, compiler_params=_params(1),
    )(*args)


def _gnorm_parts(y_ref, xs_ref, z_ref, dsk_ref):
    xs = xs_ref[...]
    y = y_ref[...] + xs * dsk_ref[...]
    z = z_ref[...].astype(F32)
    sg = _sigmoid(z)
    s = z * sg
    v = y * s
    rs = []
    for g in range(SSD_GROUPS):
        vg = v[:, g * GW:(g + 1) * GW]
        rs.append(jnp.broadcast_to(lax.rsqrt(jnp.mean(vg * vg, axis=-1, keepdims=True) + EPS), vg.shape))
    return xs, y, z, sg, s, v, jnp.concatenate(rs, axis=1)


def _gnorm_fwd_body(y_ref, xs_ref, z_ref, dsk_ref, w_ref, o_ref):
    _, _, _, _, _, v, rstd = _gnorm_parts(y_ref, xs_ref, z_ref, dsk_ref)
    o_ref[...] = (v * rstd * w_ref[...]).astype(o_ref.dtype)


def _gnorm_fwd(y, xbc, proj, dskip_x, norm_w, tr=256):
    t = y.shape[0]
    tr = min(tr, t)
    row = pl.BlockSpec((tr, D_INNER), lambda i: (i, 0))
    vec = pl.BlockSpec((1, D_INNER), lambda i: (0, 0))
    return pl.pallas_call(
        functools.partial(_gnorm_fwd_body), name="gnorm_fwd", grid=(t // tr,),
        in_specs=[row, row, pl.BlockSpec((tr, D_INNER), lambda i: (i, Z0 // D_INNER)), vec, vec],
        out_specs=row, out_shape=jax.ShapeDtypeStruct((t, D_INNER), BF16), compiler_params=_params(1),
    )(y, xbc, proj, dskip_x, norm_w.reshape(1, D_INNER))


def _gnorm_bwd_body(du_ref, y_ref, xs_ref, z_ref, dsk_ref, w_ref, et_ref, _, dy_ref, dz_ref, dw_ref, dd_ref, acc_ref, *,
                    nsteps):
    i = pl.program_id(0)

    @pl.when(i == 0)
    def _():
        dw_ref[...] = jnp.zeros_like(dw_ref)
        acc_ref[...] = jnp.zeros_like(acc_ref)

    xs, y, z, sg, s, v, rstd = _gnorm_parts(y_ref, xs_ref, z_ref, dsk_ref)
    du = du_ref[...].astype(F32)
    n = v * rstd
    dn = du * w_ref[...]
    dw_ref[...] += jnp.sum(du * n, axis=0, keepdims=True)
    prod = dn * n
    means = []
    for g in range(SSD_GROUPS):
        pg = prod[:, g * GW:(g + 1) * GW]
        means.append(jnp.broadcast_to(jnp.mean(pg, axis=-1, keepdims=True), pg.shape))
    dv = rstd * (dn - n * jnp.concatenate(means, axis=1))
    dy = dv * s
    dy_ref[...] = dy
    dz_ref[...] = (dv * y * (sg * (1.0 + z * (1.0 - sg)))).astype(dz_ref.dtype)
    acc_ref[...] += jnp.sum(dy * xs, axis=0, keepdims=True)

    @pl.when(i == nsteps - 1)
    def _():
        dd_ref[...] = _split_dot(acc_ref[...], et_ref[...])


def _gnorm_bwd(du, y, xbc, proj, dskip_x, norm_w, dproj, tr=256):
    t = y.shape[0]
    tr = min(tr, t)
    row = pl.BlockSpec((tr, D_INNER), lambda i: (i, 0))
    zcol = pl.BlockSpec((tr, D_INNER), lambda i: (i, Z0 // D_INNER))
    vec = pl.BlockSpec((1, D_INNER), lambda i: (0, 0))
    et = jnp.asarray(_expand_np(0).T, BF16)
    return pl.pallas_call(
        functools.partial(_gnorm_bwd_body, nsteps=t // tr), name="gnorm_bwd", grid=(t // tr,),
        in_specs=[row, row, row, zcol, vec, vec, pl.BlockSpec((D_INNER, LANE), lambda i: (0, 0)), ANY],
        out_specs=[row, zcol, vec, pl.BlockSpec((1, LANE), lambda i: (0, 0))],
        out_shape=[jax.ShapeDtypeStruct((t, D_INNER), F32), jax.ShapeDtypeStruct((t, NP), BF16),
                   jax.ShapeDtypeStruct((1, D_INNER), F32), jax.ShapeDtypeStruct((1, LANE), F32)],
        scratch_shapes=[pltpu.VMEM((1, D_INNER), F32)], input_output_aliases={7: 1},
        compiler_params=_params(1),
    )(du, y, xbc, proj, dskip_x, norm_w.reshape(1, D_INNER), et, dproj)


def _gate_fwd_body(gt_ref, b_ref, ys_ref, ya_ref, o_ref):
    g = _sigmoid(gt_ref[...].astype(F32) + b_ref[...])
    o_ref[...] = (g[:, 0:D_MODEL] * ys_ref[...] + g[:, D_MODEL:] * ya_ref[...]).astype(o_ref.dtype)


def _gate_fwd(proj, b_gate, y_ssd, y_attn, tr=512):
    t = y_ssd.shape[0]
    tr = min(tr, t)
    row = pl.BlockSpec((tr, D_MODEL), lambda i: (i, 0))
    return pl.pallas_call(
        functools.partial(_gate_fwd_body), name="gate_fwd", grid=(t // tr,),
        in_specs=[pl.BlockSpec((tr, 2 * D_MODEL), lambda i: (i, GT0 // (2 * D_MODEL))),
                  pl.BlockSpec((1, 2 * D_MODEL), lambda i: (0, 0)), row, row],
        out_specs=row, out_shape=jax.ShapeDtypeStruct((t, D_MODEL), BF16), compiler_params=_params(1),
    )(proj, b_gate.reshape(1, 2 * D_MODEL), y_ssd, y_attn)


def _gate_bwd_body(dm_ref, gt_ref, b_ref, ys_ref, ya_ref, dys_ref, dya_ref, dgt_ref, db_ref):
    @pl.when(pl.program_id(0) == 0)
    def _():
        db_ref[...] = jnp.zeros_like(db_ref)

    g = _sigmoid(gt_ref[...].astype(F32) + b_ref[...])
    gs = g[:, 0:D_MODEL]
    ga = g[:, D_MODEL:]
    dm = dm_ref[...].astype(F32)
    dys_ref[...] = (dm * gs).astype(dys_ref.dtype)
    dya_ref[...] = (dm * ga).astype(dya_ref.dtype)
    dgt = jnp.concatenate([dm * ys_ref[...] * gs * (1.0 - gs), dm * ya_ref[...] * ga * (1.0 - ga)], axis=1)
    dgt_ref[...] = dgt.astype(dgt_ref.dtype)
    db_ref[...] += jnp.sum(dgt, axis=0, keepdims=True)


def _gate_bwd(dmix, proj, b_gate, y_ssd, y_attn, tr=512):
    t = y_ssd.shape[0]
    tr = min(tr, t)
    row = pl.BlockSpec((tr, D_MODEL), lambda i: (i, 0))
    gates = pl.BlockSpec((tr, 2 * D_MODEL), lambda i: (i, GT0 // (2 * D_MODEL)))
    vec2 = pl.BlockSpec((1, 2 * D_MODEL), lambda i: (0, 0))
    return pl.pallas_call(
        functools.partial(_gate_bwd_body), name="gate_bwd", grid=(t // tr,),
        in_specs=[row, gates, vec2, row, row],
        out_specs=[row, row, gates, vec2],
        out_shape=[jax.ShapeDtypeStruct((t, D_MODEL), BF16), jax.ShapeDtypeStruct((t, D_MODEL), BF16),
                   jax.ShapeDtypeStruct((t, NP), BF16), jax.ShapeDtypeStruct((1, 2 * D_MODEL), F32)],
        compiler_params=_params(1),
    )(dmix, proj, b_gate.reshape(1, 2 * D_MODEL), y_ssd, y_attn)


def _bucket_np():
    i = np.arange(BLOCK)[:, None]
    j = np.arange(3 * BLOCK)[None, :]
    rel = j - BLOCK - i
    nb = N_BUCKETS // 2
    max_exact = nb // 2
    ret = np.where(rel > 0, nb, 0)
    n = np.abs(rel)
    nf = np.maximum(n, 1).astype(np.float32)
    large = max_exact + (np.log(nf / np.float32(max_exact)) / np.float32(math.log(MAX_DISTANCE / max_exact))
                         * np.float32(nb - max_exact)).astype(np.int32)
    large = np.minimum(large, nb - 1)
    bucket = ret + np.where(n < max_exact, n, large)
    band = np.abs(rel) <= WINDOW
    return bucket, band


def _onehot_np():
    bucket, _ = _bucket_np()
    oh = np.zeros((BLOCK * 3 * BLOCK, LANE), np.float32)
    oh[np.arange(oh.shape[0]), bucket.reshape(-1)] = 1.0
    return oh


REP = Q_HEADS // KV_HEADS


def _group_cols(ref, g):
    return jnp.concatenate([ref[:, (g * REP + r) * HEAD_PAD:(g * REP + r + 1) * HEAD_PAD] for r in range(REP)], axis=0)


def _group_lane(ref, g, rows):
    return jnp.concatenate([jnp.broadcast_to(ref[:, g * REP + r:g * REP + r + 1], (rows, 1)) for r in range(REP)], axis=0)


def _attn_kv(refs, g):
    gc = slice(g * HEAD_PAD, (g + 1) * HEAD_PAD)
    return _mx(jnp.concatenate([r[:, gc] for r in refs], axis=0))


def _attn_logits(q_ref, kg, bias_ref, edge, g):
    q4 = _mx(_group_cols(q_ref, g))
    s = _dot(q4, kg, ((1,), (1,))) * (HEAD_DIM ** -0.5) + bias_ref[g] + edge
    return q4, s


def _attn_edge(n, nb):
    col = lax.broadcasted_iota(jnp.int32, (1, 3 * BLOCK), 1)
    bad = ((col < BLOCK) & (n == 0)) | ((col >= 2 * BLOCK) & (n == nb - 1))
    return jnp.where(bad, NEG, 0.0)


def _onehot_lane(h):
    return (lax.broadcasted_iota(jnp.int32, (1, LANE), 1) == h).astype(F32)


def _attn_fwd_body(*refs, nb, comm):
    ins, cin, (o_ref, lse_ref), cout, _, csem = _split_refs(refs, 9, 2, comm)
    q_ref, kp_ref, kc_ref, kn_ref, vp_ref, vc_ref, vn_ref, bias_ref, sink_ref = ins
    n = pl.program_id(0)
    if comm is not None:
        comm.run(cin, cout, csem, n, nb)
    edge = _attn_edge(n, nb)
    lse = jnp.zeros((BLOCK, LANE), F32)
    logits = [_attn_logits(q_ref, _attn_kv((kp_ref, kc_ref, kn_ref), g), bias_ref, edge, g)[1] for g in range(KV_HEADS)]
    for g in range(KV_HEADS):
        vg = _attn_kv((vp_ref, vc_ref, vn_ref), g)
        s = logits[g]
        sink = _group_lane(sink_ref, g, BLOCK)
        m = jnp.maximum(jnp.max(s, axis=-1, keepdims=True), sink)
        p = jnp.exp(s - m)
        den = jnp.sum(p, axis=-1, keepdims=True) + jnp.exp(sink - m)
        o4 = _dot(_mx(p * (1.0 / den)), vg)
        l4 = m + jnp.log(den)
        for r in range(REP):
            h = g * REP + r
            o_ref[:, h * HEAD_PAD:(h + 1) * HEAD_PAD] = o4[r * BLOCK:(r + 1) * BLOCK].astype(o_ref.dtype)
            lse = lse + l4[r * BLOCK:(r + 1) * BLOCK] * _onehot_lane(h)
    lse_ref[...] = lse


def _attn_specs(t, clamp):
    nb = t // BLOCK
    cur = (lambda n: jnp.minimum(n, nb - 1)) if clamp else (lambda n: n)
    prv = lambda n: jnp.maximum(cur(n) - 1, 0)
    nxt = lambda n: jnp.minimum(cur(n) + 1, nb - 1)
    kb, vb = K0 // KP, V0 // KP
    qs = pl.BlockSpec((BLOCK, QP), lambda n: (cur(n), Q0 // QP))
    ks = [pl.BlockSpec((BLOCK, KP), lambda n, f=f: (f(n), kb)) for f in (prv, cur, nxt)]
    vs = [pl.BlockSpec((BLOCK, KP), lambda n, f=f: (f(n), vb)) for f in (prv, cur, nxt)]
    consts = [pl.BlockSpec((KV_HEADS, REP * BLOCK, 3 * BLOCK), lambda n: (0, 0, 0)), pl.BlockSpec((1, LANE), lambda n: (0, 0))]
    return nb, cur, qs, ks, vs, consts


def _attn_fwd(proj, bias, sink, comm=None):
    t = proj.shape[0]
    nb, cur, qs, ks, vs, consts = _attn_specs(t, False)
    in_specs, args = [qs] + ks + vs + consts, [proj] * 7 + [bias, sink]
    out_specs = [pl.BlockSpec((BLOCK, QP), lambda n: (n, 0)), pl.BlockSpec((BLOCK, LANE), lambda n: (n, 0))]
    out_shape = [jax.ShapeDtypeStruct((t, QP), BF16), jax.ShapeDtypeStruct((t, LANE), F32)]
    scratch = []
    if comm is not None:
        in_specs, args = in_specs + [ANY] * comm.n, args + comm.arrs
        out_specs, out_shape = out_specs + [ANY] * comm.n, out_shape + comm.out_shapes()
        scratch = comm.sems()
    return pl.pallas_call(
        functools.partial(_attn_fwd_body, nb=nb, comm=comm), name="attn_fwd", grid=(nb,),
        in_specs=in_specs, out_specs=out_specs, out_shape=out_shape, scratch_shapes=scratch, compiler_params=_params(1),
    )(*args)


def _attn_bwd_body(*refs, nb, comm):
    ins, cin, (dqkv_ref, dbias_ref, dsink_ref), cout, (ak, bk, av, bv, dq_scr), csem = _split_refs(refs, 13, 3, comm)
    q_ref, kp_ref, kc_ref, kn_ref, vp_ref, vc_ref, vn_ref, bias_ref, sink_ref, do_ref, o_ref, lse_ref = ins[:12]
    dk_ref = dqkv_ref.at[:, K0:K0 + KP]
    dv_ref = dqkv_ref.at[:, V0:V0 + KP]
    n = pl.program_id(0)
    scale = HEAD_DIM ** -0.5
    if comm is not None:
        comm.run(cin, cout, csem, n, nb + 1)

    @pl.when(n == 0)
    def _():
        for r in (ak, bk, av, bv, dq_scr, dbias_ref, dsink_ref):
            r[...] = jnp.zeros_like(r)

    dqkv_ref[:, Q0:Q0 + QP] = dq_scr[...]

    @pl.when(n < nb)
    def _():
        edge = _attn_edge(n, nb)
        dsink = jnp.zeros((1, LANE), F32)
        early = []
        for g in range(KV_HEADS):
            kg = _attn_kv((kp_ref, kc_ref, kn_ref), g)
            q4, s = _attn_logits(q_ref, kg, bias_ref, edge, g)
            do = _group_cols(do_ref, g).astype(F32)
            early.append((kg, q4, s, do, _dot(_mx(do), _attn_kv((vp_ref, vc_ref, vn_ref), g), ((1,), (1,)))))
        for g in range(KV_HEADS):
            gc = slice(g * HEAD_PAD, (g + 1) * HEAD_PAD)
            kg, q4, s, do, dp = early[g]
            lse = _group_lane(lse_ref, g, BLOCK)
            p = jnp.exp(s - lse)
            delta = jnp.sum(do * _group_cols(o_ref, g).astype(F32), axis=-1, keepdims=True)
            ds = p * (dp - delta)
            sunk = jnp.exp(_group_lane(sink_ref, g, BLOCK) - lse) * delta
            dbias_ref[g] += ds
            dsb = _mx(ds)
            dq4 = _dot(dsb, kg) * scale
            for r in range(REP):
                h = g * REP + r
                dq_scr[:, h * HEAD_PAD:(h + 1) * HEAD_PAD] = dq4[r * BLOCK:(r + 1) * BLOCK].astype(dq_scr.dtype)
                dsink = dsink - jnp.sum(sunk[r * BLOCK:(r + 1) * BLOCK], axis=0, keepdims=True) * _onehot_lane(h)
            dk_c = _dot(dsb, q4, ((0,), (0,))) * scale
            dv_c = _dot(_mx(p), _mx(do), ((0,), (0,)))
            for acc_a, acc_b, out, c in ((ak, bk, dk_ref, dk_c), (av, bv, dv_ref, dv_c)):
                out[:, gc] = (acc_a[:, gc] + c[0:BLOCK]).astype(out.dtype)
                acc_a[:, gc] = acc_b[:, gc] + c[BLOCK:2 * BLOCK]
                acc_b[:, gc] = c[2 * BLOCK:3 * BLOCK]
        dsink_ref[...] += dsink

    @pl.when(n == nb)
    def _():
        dk_ref[...] = ak[...].astype(dk_ref.dtype)
        dv_ref[...] = av[...].astype(dv_ref.dtype)


def _attn_bwd(proj, bias, sink, do, o, lse, dproj, comm=None):
    t = proj.shape[0]
    nb, cur, qs, ks, vs, consts = _attn_specs(t, True)
    rowq = pl.BlockSpec((BLOCK, QP), lambda n: (cur(n), 0))
    late = pl.BlockSpec((BLOCK, QKV), lambda n: (jnp.maximum(n - 1, 0), Q0 // QKV))
    in_specs = [qs] + ks + vs + consts + [rowq, rowq, pl.BlockSpec((BLOCK, LANE), lambda n: (cur(n), 0)), ANY]
    args = [proj] * 7 + [bias, sink, do, o, lse, dproj]
    out_specs = [late, pl.BlockSpec((KV_HEADS, REP * BLOCK, 3 * BLOCK), lambda n: (0, 0, 0)), pl.BlockSpec((1, LANE), lambda n: (0, 0))]
    out_shape = [jax.ShapeDtypeStruct((t, NP), BF16), jax.ShapeDtypeStruct((KV_HEADS, REP * BLOCK, 3 * BLOCK), F32),
                 jax.ShapeDtypeStruct((1, LANE), F32)]
    scratch = [pltpu.VMEM((BLOCK, KP), F32) for _ in range(4)] + [pltpu.VMEM((BLOCK, QP), BF16)]
    if comm is not None:
        in_specs, args = in_specs + [ANY] * comm.n, args + comm.arrs
        out_specs, out_shape = out_specs + [ANY] * comm.n, out_shape + comm.out_shapes()
        scratch = scratch + comm.sems()
    return pl.pallas_call(
        functools.partial(_attn_bwd_body, nb=nb, comm=comm), name="attn_bwd", grid=(nb + 1,),
        in_specs=in_specs, out_specs=out_specs, out_shape=out_shape, scratch_shapes=scratch, input_output_aliases={12: 0},
        compiler_params=_params(1),
    )(*args)


def _adamw_body(gp_ref, w_ref, m_ref, v_ref, g_ref, d_ref, nm_ref, nv_ref, *, nparts):
    g = gp_ref[0].astype(F32)
    for j in range(1, nparts):
        g = g + gp_ref[j].astype(F32)
    m = ADAM_B1 * m_ref[...] + (1.0 - ADAM_B1) * g
    v = ADAM_B2 * v_ref[...] + (1.0 - ADAM_B2) * (g * g)
    m_hat = m / (1.0 - ADAM_B1 ** ADAM_STEP)
    v_hat = v / (1.0 - ADAM_B2 ** ADAM_STEP)
    g_ref[...] = g
    d_ref[...] = -ADAM_LR * (m_hat / (jnp.sqrt(v_hat) + ADAM_EPS) + ADAM_WD * w_ref[...])
    nm_ref[...] = m
    nv_ref[...] = v


def _adamw_layer_body(*refs, nparts):
    gp_ref, w_ref, m_ref, v_ref = refs[:4]
    g_ref, d_ref, nm_ref, nv_ref = refs[-4:]
    g = gp_ref[0].astype(F32)
    for j in range(1, nparts):
        g = g + gp_ref[j].astype(F32)
    m = ADAM_B1 * m_ref[0] + (1.0 - ADAM_B1) * g
    v = ADAM_B2 * v_ref[0] + (1.0 - ADAM_B2) * (g * g)
    m_hat = m / (1.0 - ADAM_B1 ** ADAM_STEP)
    v_hat = v / (1.0 - ADAM_B2 ** ADAM_STEP)
    g_ref[0] = g
    d_ref[0] = -ADAM_LR * (m_hat / (jnp.sqrt(v_hat) + ADAM_EPS) + ADAM_WD * w_ref[0])
    nm_ref[0] = m
    nv_ref[0] = v


def _adamw_layer(gparts, row0, w, m, v, l, prev, name, tr=128):
    p, c = gparts.shape[0], gparts.shape[2]
    per = w.shape[1]
    tr = min(tr, per)
    assert per % tr == 0 and row0 % tr == 0 and w.shape[2] == c
    lay = pl.BlockSpec((1, tr, c), lambda i: (l, i, 0))
    in_specs = [pl.BlockSpec((p, tr, c), lambda i: (0, row0 // tr + i, 0)), lay, lay, lay]
    args = [gparts, w, m, v]
    aliases = {}
    if prev is not None:
        in_specs, args, aliases = in_specs + [ANY] * 4, args + list(prev), {4 + k: k for k in range(4)}
    sd = jax.ShapeDtypeStruct(w.shape, F32)
    return pl.pallas_call(
        functools.partial(_adamw_layer_body, nparts=p), name=name, grid=(per // tr,),
        in_specs=in_specs, out_specs=[lay] * 4, out_shape=[sd] * 4, input_output_aliases=aliases, compiler_params=_params(1),
    )(*args)


def _adamw(gparts, w, m, v, name, tr=256):
    p, r, c = gparts.shape
    tr = min(tr, r)
    assert r % tr == 0
    row = pl.BlockSpec((tr, c), lambda i: (i, 0))
    sd = jax.ShapeDtypeStruct((r, c), F32)
    return pl.pallas_call(
        functools.partial(_adamw_body, nparts=p), name=name, grid=(r // tr,),
        in_specs=[pl.BlockSpec((p, tr, c), lambda i: (0, i, 0)), row, row, row],
        out_specs=[row, row, row, row], out_shape=[sd, sd, sd, sd], compiler_params=_params(1),
    )(gparts, w, m, v)


MESH = pl.DeviceIdType.MESH
N_REL = N_DEV - 1


def _gather_phases(ins, outs, sems):
    n = len(ins)
    send_sems, recv_sems, local_sems = sems
    x, y, c = lax.axis_index("x"), lax.axis_index("y"), lax.axis_index("c")
    me, sibling = (x, y, c), (x, y, 1 - c)
    chips = [(1 - x, y), (x, 1 - y), (1 - x, 1 - y)]

    def slot(a, p):
        return outs[a].at[4 * p[0] + 2 * p[1] + p[2]]

    def copy(a, k, block, to, src=None):
        return pltpu.make_async_remote_copy(
            src_ref=slot(a, block) if src is None else src, dst_ref=slot(a, block),
            send_sem=send_sems.at[a * N_REL + k], recv_sem=recv_sems.at[a * N_REL + k], device_id=to, device_id_type=MESH)

    mine = [pltpu.make_async_copy(ins[a], slot(a, me), local_sems.at[a]) for a in range(n)]
    first = []
    for a in range(n):
        first.append(copy(a, 0, me, sibling, src=ins[a]))
        first += [copy(a, 1 + j, me, (*chip, c), src=ins[a]) for j, chip in enumerate(chips)]
    passed = [copy(a, 4 + j, (*chip, c), sibling) for j, chip in enumerate(chips) for a in range(n)]

    def start():
        for cp in mine + first:
            cp.start()

    def pass_on():
        i = 0
        for j, chip in enumerate(chips):
            for a in range(n):
                copy(a, 1 + j, (*chip, c), me).wait_recv()
                passed[i].start()
                i += 1

    def finish():
        for a in range(n):
            copy(a, 0, sibling, me).wait_recv()
            for j, chip in enumerate(chips):
                copy(a, 4 + j, (*chip, 1 - c), me).wait_recv()
        for cp in first + passed:
            cp.wait_send()
        for cp in mine:
            cp.wait()

    return start, pass_on, finish


def _exchange_phases(ins, outs, sems):
    n = len(ins)
    send_sems, recv_sems, local_sems = sems
    x, y, c = lax.axis_index("x"), lax.axis_index("y"), lax.axis_index("c")
    me = 4 * x + 2 * y + c
    mine = [pltpu.make_async_copy(ins[a].at[me], outs[a].at[me], local_sems.at[a]) for a in range(n)]
    copies = []
    for a in range(n):
        for k in range(1, N_DEV):
            px = 1 - x if k & 4 else x
            py = 1 - y if k & 2 else y
            pc = 1 - c if k & 1 else c
            peer = 4 * px + 2 * py + pc
            send = pltpu.make_async_remote_copy(
                src_ref=ins[a].at[peer], dst_ref=outs[a].at[me], send_sem=send_sems.at[a * N_REL + k - 1],
                recv_sem=recv_sems.at[a * N_REL + k - 1], device_id=(px, py, pc), device_id_type=MESH)
            recv = pltpu.make_async_remote_copy(
                src_ref=ins[a].at[me], dst_ref=outs[a].at[peer], send_sem=send_sems.at[a * N_REL + k - 1],
                recv_sem=recv_sems.at[a * N_REL + k - 1], device_id=(px, py, pc), device_id_type=MESH)
            copies.append((send, recv))

    def start():
        for cp in mine:
            cp.start()
        for send, _ in copies:
            send.start()

    def finish():
        for send, recv in copies:
            send.wait_send()
            recv.wait_recv()
        for cp in mine:
            cp.wait()

    return start, None, finish


class _Comm:
    def __init__(self, kind, arrs):
        self.kind, self.arrs, self.n = kind, list(arrs), len(arrs)

    def out_shapes(self):
        if self.kind == "gather":
            return [jax.ShapeDtypeStruct((N_DEV,) + a.shape, a.dtype) for a in self.arrs]
        return [jax.ShapeDtypeStruct(a.shape, a.dtype) for a in self.arrs]

    def sems(self):
        return [pltpu.SemaphoreType.DMA((self.n * N_REL,)), pltpu.SemaphoreType.DMA((self.n * N_REL,)),
                pltpu.SemaphoreType.DMA((self.n,))]

    def run(self, ins, outs, sems, step, nsteps):
        start, mid, finish = (_gather_phases if self.kind == "gather" else _exchange_phases)(ins, outs, sems)
        if nsteps is None:
            start()
            if mid is not None:
                mid()
            finish()
            return
        pl.when(step == 0)(start)
        if mid is not None:
            pl.when(step == (3 * nsteps) // 4)(mid)
        pl.when(step == nsteps - 1)(finish)


def _split_refs(refs, n_in, n_out, comm):
    k = comm.n if comm is not None else 0
    i = 0
    parts = []
    for cnt in (n_in, k, n_out, k):
        parts.append(refs[i:i + cnt])
        i += cnt
    rest = refs[i:]
    if comm is None:
        return parts[0], (), parts[2], (), rest, ()
    return parts[0], parts[1], parts[2], parts[3], rest[:len(rest) - 3], rest[len(rest) - 3:]


def _comm_body(*refs, comm):
    n = comm.n
    comm.run(refs[:n], refs[n:2 * n], refs[2 * n:], None, None)


def _communicate(comm, name, vmem=False):
    spec = pl.BlockSpec(memory_space=pltpu.VMEM if vmem else pl.ANY)
    return pl.pallas_call(
        functools.partial(_comm_body, comm=comm), name=name,
        in_specs=[spec] * comm.n, out_specs=[spec] * comm.n, out_shape=comm.out_shapes(), scratch_shapes=comm.sems(),
        compiler_params=pltpu.CompilerParams(vmem_limit_bytes=VMEM_LIMIT),
    )(*comm.arrs)


ANY = pl.BlockSpec(memory_space=pl.ANY)


def _pad_heads(w, axis):
    shp = w.shape
    heads = shp[axis] // HEAD_DIM
    w = w.reshape(shp[:axis] + (heads, HEAD_DIM) + shp[axis + 1:])
    pad = [(0, 0)] * w.ndim
    pad[axis + 1] = (0, HEAD_PAD - HEAD_DIM)
    w = jnp.pad(w, pad)
    return w.reshape(shp[:axis] + (heads * HEAD_PAD,) + shp[axis + 1:])


def _unpad_heads(w, axis):
    shp = w.shape
    heads = shp[axis] // HEAD_PAD
    w = w.reshape(shp[:axis] + (heads, HEAD_PAD) + shp[axis + 1:])
    w = lax.slice_in_dim(w, 0, HEAD_DIM, axis=axis + 1)
    return w.reshape(shp[:axis] + (heads * HEAD_DIM,) + shp[axis + 1:])


def _w_in_to_padded(w):
    idx = np.cumsum((0,) + IN_SPLITS)
    z, xbc, dt, q, k, v, gates = [w[:, idx[i]:idx[i + 1]] for i in range(7)]
    zeros = lambda n: jnp.zeros((w.shape[0], n), w.dtype)
    return jnp.concatenate([_pad_heads(q, 1), _pad_heads(k, 1), _pad_heads(v, 1), xbc, dt, zeros(DTW - 2 * SSD_HEADS), z, gates],
                           axis=1)


def _w_in_from_padded(w):
    z = w[:, Z0:Z0 + D_INNER]
    gates = w[:, GT0:GT0 + 2 * D_MODEL]
    q = _unpad_heads(w[:, Q0:Q0 + QP], 1)
    xbc = w[:, XBC0:XBC0 + CONV_DIM]
    k = _unpad_heads(w[:, K0:K0 + KP], 1)
    v = _unpad_heads(w[:, V0:V0 + KP], 1)
    dt = w[:, DT0:DT0 + 2 * SSD_HEADS]
    return jnp.concatenate([z, xbc, dt, q, k, v, gates], axis=1)


def _pad_lane(v):
    v = v.reshape(-1)
    return jnp.pad(v, (0, LANE - v.shape[0]))


def _layer_consts(p):
    c = dict(p)
    c["conv_w8"] = jnp.pad(p["conv_w"].reshape(SSD_CONV, CONV_DIM), ((0, 8 - SSD_CONV), (0, 0)))
    c["conv_b1"] = p["conv_b"].reshape(1, CONV_DIM)
    c["dtb"] = _pad_lane(p["dt_bias"])
    c["alog"] = _pad_lane(p["a_log"])
    c["dskip_x"] = jnp.repeat(p["d_skip"], SSD_HEAD_DIM).reshape(1, D_INNER)
    c["sink"] = _pad_lane(p["attn_sink"]).reshape(1, LANE)
    return c


def _hosted(res, n_own, hook):
    res = list(res)
    if hook is not None:
        hook[1](res[n_own:])
    return res[0] if n_own == 1 else res[:n_own]


def _layer_fwd(x, h, c, w, bias, l, hooks, next_g):
    comm = lambda k: hooks[k][0] if k in hooks else None
    s = {"x": x}
    s["h"] = _rms_fwd(x, c["pre_mix_norm"], f"pre_mix_norm_{l}", out_dtype=BF16) if h is None else h
    res = _mm(s["h"], w["w_in"], f"in_proj_{l}", out_dtype=BF16, mode="dt_side", comm=comm("in_proj"))
    proj, s["dt"] = _hosted(res, 2, hooks.get("in_proj"))
    s["proj"] = proj
    s["xbc"] = _conv_fwd(proj, c["conv_w8"], c["conv_b1"])
    s["dtt"] = _dt_rows(s["dt"])
    yf, s["stf"] = _hosted(_ssd_fwd(s["xbc"], s["dt"], s["dtt"], c["dtb"], c["alog"], False, comm=comm("ssd_fwd")), 2,
                                hooks.get("ssd_fwd"))
    s["y"], s["stb"] = _hosted(_ssd_fwd(s["xbc"], s["dt"], s["dtt"], c["dtb"], c["alog"], True, comm=comm("ssd_fwd_rev"), add=yf), 2,
                                hooks.get("ssd_fwd_rev"))
    s["u"] = _gnorm_fwd(s["y"], s["xbc"], proj, c["dskip_x"], c["ssd_norm"])
    s["y_ssd"] = _mm(s["u"], w["w_ssd_out"], f"ssd_out_{l}")
    s["o"], s["lse"] = _hosted(_attn_fwd(proj, bias, c["sink"], comm=comm("attn_fwd")), 2, hooks.get("attn_fwd"))
    s["y_attn"] = _mm(s["o"], w["w_attn_out"], f"attn_out_{l}")
    s["mix"] = _gate_fwd(proj, c["b_gate"], s["y_ssd"], s["y_attn"])
    s["mixed"] = _mm(s["mix"], w["w_o"], f"w_o_{l}")
    s["x1"], s["h2"] = _rms_fwd(s["mixed"], c["post_mix_norm"], f"post_mix_norm_{l}", res=x, next_g=c["pre_mlp_norm"])
    s["f1"], s["a"] = _mm(s["h2"], w["w_mlp_in"], f"mlp_in_{l}", mode="relu2")
    s["f"] = _mm(s["a"], w["w_mlp_out"], f"mlp_out_{l}")
    if next_g is None:
        return _rms_fwd(s["f"], c["post_mlp_norm"], f"post_mlp_norm_{l}", res=s["x1"]), None, s
    x2, h_next = _rms_fwd(s["f"], c["post_mlp_norm"], f"post_mlp_norm_{l}", res=s["x1"], next_g=next_g)
    return x2, h_next, s


def _send_rest(gw):
    s_mi = gw["w_mlp_in"].reshape(D_MODEL, N_DEV, -1).transpose(1, 0, 2).astype(BF16)
    parts = [(_unpad_heads(gw[n], 0) if n == "w_attn_out" else gw[n]).reshape(N_DEV, -1, D_MODEL) for n in ROWS]
    return [s_mi, jnp.concatenate(parts, axis=1).astype(BF16)]


def _send_w_in(g):
    return [_w_in_from_padded(g).reshape(D_MODEL, N_DEV, -1).transpose(1, 0, 2).astype(BF16)]


def _layer_bwd(dx2, s, c, w, bias, l, earlier=None, host_own=False):
    gw, gs, got = {}, {}, {}
    df, gs["post_mlp_norm"] = _rms_bwd(s["f"], c["post_mlp_norm"], dx2, f"post_mlp_norm_bwd_{l}", out_dtype=BF16)
    df1 = _mm(df, w["w_mlp_out"], f"mlp_out_dx_{l}", tb=True, mode="relu2_bwd", extra=s["f1"], out_dtype=BF16)
    gw["w_mlp_out"] = _mm(s["a"], df, f"mlp_out_dw_{l}", ta=True, out_dtype=BF16)
    gw["w_mlp_in"] = _mm(s["h2"], df1, f"mlp_in_dw_{l}", ta=True, out_dtype=BF16)
    dh2 = _mm(df1, w["w_mlp_in"], f"mlp_in_dx_{l}", tb=True, out_dtype=BF16)
    dx1, gs["pre_mlp_norm"] = _rms_bwd(s["x1"], c["pre_mlp_norm"], dh2, f"pre_mlp_norm_bwd_{l}", add=dx2)
    dmixed, gs["post_mix_norm"] = _rms_bwd(s["mixed"], c["post_mix_norm"], dx1, f"post_mix_norm_bwd_{l}", out_dtype=BF16)
    gw["w_o"] = _mm(s["mix"], dmixed, f"w_o_dw_{l}", ta=True, out_dtype=BF16)
    dmix = _mm(dmixed, w["w_o"], f"w_o_dx_{l}", tb=True, out_dtype=BF16)
    dys, dya, dproj, gs["b_gate"] = _gate_bwd(dmix, s["proj"], c["b_gate"], s["y_ssd"], s["y_attn"])
    gw["w_ssd_out"] = _mm(s["u"], dys, f"ssd_out_dw_{l}", ta=True, out_dtype=BF16)
    du = _mm(dys, w["w_ssd_out"], f"ssd_out_dx_{l}", tb=True, out_dtype=BF16)
    gw["w_attn_out"] = _mm(s["o"], dya, f"attn_out_dw_{l}", ta=True, out_dtype=BF16)
    do = _mm(dya, w["w_attn_out"], f"attn_out_dx_{l}", tb=True, out_dtype=BF16)
    res = _attn_bwd(s["proj"], bias, c["sink"], do, s["o"], s["lse"], dproj,
                    comm=None if earlier is None else _Comm("exchange", earlier[:1]))
    (dproj, dbias, dsink), got["earlier"] = res[:3], list(res[3:])
    gs["attn_sink"] = dsink
    dy, dproj, gs["ssd_norm"], gs["d_skip"] = _gnorm_bwd(du, s["y"], s["xbc"], s["proj"], c["dskip_x"], c["ssd_norm"],
                                                         dproj)
    res = _ssd_bwd(s["xbc"], s["dt"], s["dtt"], c["dtb"], c["alog"], dy, s["stf"], c["dskip_x"], False,
                   comm=None if earlier is None else _Comm("exchange", earlier[1:]))
    part, got["earlier"] = res[:3], got["earlier"] + list(res[3:])
    res = _ssd_bwd(s["xbc"], s["dt"], s["dtt"], c["dtb"], c["alog"], dy, s["stb"], c["dskip_x"], True, prev=part, dproj=dproj,
                   comm=_Comm("exchange", _send_rest(gw)) if host_own else None)
    (dxbc, dproj, sm), got["rest"] = res[:3], res[3:]
    gs["a_log"], gs["dt_bias"] = sm[0:1], sm[1:2]
    dproj, gs["conv"] = _conv_bwd(s["proj"], dxbc, c["conv_w8"], c["conv_b1"], dproj)
    gw["w_in"] = _mm(s["h"], dproj, f"in_proj_dw_{l}", ta=True, out_dtype=BF16)
    if host_own:
        dh, *got["w_in"] = _mm(dproj, w["w_in"], f"in_proj_dx_{l}", tb=True, out_dtype=BF16, comm=_Comm("exchange", _send_w_in(gw["w_in"])))
    else:
        dh = _mm(dproj, w["w_in"], f"in_proj_dx_{l}", tb=True, out_dtype=BF16)
    dx, gs["pre_mix_norm"] = _rms_bwd(s["x"], c["pre_mix_norm"], dh, f"pre_mix_norm_bwd_{l}", add=dx1)
    return dx, gw, gs, dbias, got


SMALL = (("pre_mix_norm", DEPTH * D_MODEL), ("b_gate", DEPTH * 2 * D_MODEL), ("conv_w", DEPTH * SSD_CONV * CONV_DIM),
         ("conv_b", DEPTH * CONV_DIM), ("dt_bias", DEPTH * 2 * SSD_HEADS), ("a_log", DEPTH * 2 * SSD_HEADS),
         ("d_skip", DEPTH * SSD_HEADS), ("ssd_norm", DEPTH * D_INNER), ("attn_sink", DEPTH * Q_HEADS),
         ("rel_bias_table", N_BUCKETS * Q_HEADS), ("post_mix_norm", DEPTH * D_MODEL), ("pre_mlp_norm", DEPTH * D_MODEL),
         ("post_mlp_norm", DEPTH * D_MODEL))


def _small_rows(n):
    return -(-n // LANE)


def _pack_small(vals, fill=0.0):
    rows = []
    for name, n in SMALL:
        v = vals[name].reshape(-1).astype(F32)
        rows.append(jnp.pad(v, (0, _small_rows(n) * LANE - n), constant_values=fill).reshape(-1, LANE))
    out = jnp.concatenate(rows, axis=0)
    return jnp.pad(out, ((0, -out.shape[0] % 8), (0, 0)), constant_values=fill)


def _unpack_small(packed, shapes):
    out, r = {}, 0
    for name, n in SMALL:
        nr = _small_rows(n)
        out[name] = packed[r:r + nr].reshape(-1)[:n].reshape(shapes[name])
        r += nr
    return out


BIG = ("w_in", "w_ssd_out", "w_attn_out", "w_o", "w_mlp_in", "w_mlp_out")
ROWS = ("w_ssd_out", "w_attn_out", "w_o", "w_mlp_out")


def _pack_rows(vals, lead):
    return jnp.concatenate([vals[n].reshape(lead + (-1, D_MODEL)) for n in ROWS], axis=len(lead))


def kernel(x, pre_mix_norm, w_in, b_gate, conv_w, conv_b, dt_bias, a_log, d_skip, ssd_norm, w_ssd_out, attn_sink, rel_bias_table, w_attn_out, w_o, post_mix_norm, pre_mlp_norm, w_mlp_in, w_mlp_out, post_mlp_norm, loss_target, m_pre_mix_norm, m_w_in, m_b_gate, m_conv_w, m_conv_b, m_dt_bias, m_a_log, m_d_skip, m_ssd_norm, m_w_ssd_out, m_attn_sink, m_rel_bias_table, m_w_attn_out, m_w_o, m_post_mix_norm, m_pre_mlp_norm, m_w_mlp_in, m_w_mlp_out, m_post_mlp_norm, v_pre_mix_norm, v_w_in, v_b_gate, v_conv_w, v_conv_b, v_dt_bias, v_a_log, v_d_skip, v_ssd_norm, v_w_ssd_out, v_attn_sink, v_rel_bias_table, v_w_attn_out, v_w_o, v_post_mix_norm, v_pre_mlp_norm, v_w_mlp_in, v_w_mlp_out, v_post_mlp_norm):
    names = ("pre_mix_norm", "w_in", "b_gate", "conv_w", "conv_b", "dt_bias", "a_log", "d_skip", "ssd_norm", "w_ssd_out",
             "attn_sink", "rel_bias_table", "w_attn_out", "w_o", "post_mix_norm", "pre_mlp_norm", "w_mlp_in", "w_mlp_out",
             "post_mlp_norm")
    W = dict(zip(names, (pre_mix_norm, w_in, b_gate, conv_w, conv_b, dt_bias, a_log, d_skip, ssd_norm, w_ssd_out, attn_sink,
                         rel_bias_table, w_attn_out, w_o, post_mix_norm, pre_mlp_norm, w_mlp_in, w_mlp_out, post_mlp_norm)))
    M = dict(zip(names, (m_pre_mix_norm, m_w_in, m_b_gate, m_conv_w, m_conv_b, m_dt_bias, m_a_log, m_d_skip, m_ssd_norm,
                         m_w_ssd_out, m_attn_sink, m_rel_bias_table, m_w_attn_out, m_w_o, m_post_mix_norm, m_pre_mlp_norm,
                         m_w_mlp_in, m_w_mlp_out, m_post_mlp_norm)))
    V = dict(zip(names, (v_pre_mix_norm, v_w_in, v_b_gate, v_conv_w, v_conv_b, v_dt_bias, v_a_log, v_d_skip, v_ssd_norm,
                         v_w_ssd_out, v_attn_sink, v_rel_bias_table, v_w_attn_out, v_w_o, v_post_mix_norm, v_pre_mlp_norm,
                         v_w_mlp_in, v_w_mlp_out, v_post_mlp_norm)))
    t = x.shape[1]
    shard = {n: W[n].shape for n in names}
    dev = 4 * lax.axis_index("x") + 2 * lax.axis_index("y") + lax.axis_index("c")
    cshard = CONV_DIM // N_DEV

    sh_in = [w_in[l].astype(BF16) for l in range(DEPTH)]
    sh_rest = [[w_mlp_in[l].astype(BF16), _pack_rows({n: W[n][l].astype(BF16) for n in ROWS}, ())] for l in range(DEPTH)]
    full = [{} for _ in range(DEPTH)]

    def take_w_in(l):
        def take(got):
            full[l]["w_in"] = _w_in_to_padded(got[0].transpose(1, 0, 2).reshape(D_MODEL, N_IN))
        return take

    def take_mlp_in(l):
        def take(got):
            full[l]["w_mlp_in"] = got[0].transpose(1, 0, 2).reshape(D_MODEL, D_FF)
        return take

    def take_rows(l):
        def take(got):
            r = 0
            for n in ROWS:
                per = shard[n][1]
                full[l][n] = got[0][:, r:r + per].reshape(N_DEV * per, D_MODEL)
                r += per
            full[l]["w_attn_out"] = _pad_heads(full[l]["w_attn_out"], 0)
        return take

    def take_rest(l):
        def take(got):
            take_mlp_in(l)(got[:1])
            take_rows(l)(got[1:])
        return take

    take_w_in(0)(_communicate(_Comm("gather", [sh_in[0]]), "gather_w_in_0"))
    (g_conv,) = _communicate(_Comm("gather", [conv_w.reshape(-1, LANE)]), "gather_conv_w", vmem=True)
    conv_full = g_conv.reshape(N_DEV, DEPTH, SSD_CONV, cshard).transpose(1, 2, 0, 3).reshape(DEPTH, SSD_CONV, CONV_DIM)
    hooks = [{"in_proj": (_Comm("gather", sh_rest[0]), take_rest(0)),
              "ssd_fwd": (_Comm("gather", sh_rest[1][1:]), take_rows(1)),
              "ssd_fwd_rev": (_Comm("gather", sh_rest[1][:1]), take_mlp_in(1)),
              "attn_fwd": (_Comm("gather", [sh_in[1]]), take_w_in(1))}, {}]

    _, band = _bucket_np()
    onehot = jnp.asarray(_onehot_np(), BF16)
    table_t = jnp.pad(rel_bias_table.T, ((0, 0), (0, LANE - N_BUCKETS)))
    bias = _mm(table_t, onehot, "t5_bias", tb=True, tm=Q_HEADS, tn=3 * BLOCK * BLOCK // 8, split_a=True)
    bias = (bias.reshape(Q_HEADS, BLOCK, 3 * BLOCK) + jnp.asarray(np.where(band, 0.0, NEG), F32)).reshape(
        KV_HEADS, REP * BLOCK, 3 * BLOCK)

    consts = []
    for l in range(DEPTH):
        p = {n: W[n][l] for n in names if n not in BIG and n not in ("rel_bias_table", "conv_w")}
        p["conv_w"] = conv_full[l]
        consts.append(_layer_consts(p))
    act, normed = x[0], None
    saved = []
    for l in range(DEPTH):
        next_g = consts[l + 1]["pre_mix_norm"] if l + 1 < DEPTH else None
        act, normed, s = _layer_fwd(act, normed, consts[l], full[l], bias, l, hooks[l], next_g)
        saved.append(s)
    dy, lsum = _loss(act, loss_target[0])
    loss = lax.psum(0.5 / D_MODEL * jnp.sum(lsum), ("x", "y", "c"))

    gss, dbs = [None] * DEPTH, [None] * DEPTH
    dy, gw1, gss[1], dbs[1], _ = _layer_bwd(dy, saved[1], consts[1], full[1], bias, 1)
    dy, _, gss[0], dbs[0], got = _layer_bwd(dy, saved[0], consts[0], full[0], bias, 0,
                                            earlier=_send_w_in(gw1["w_in"]) + _send_rest(gw1), host_own=True)
    recv = [list(got["w_in"]) + list(got["rest"]), list(got["earlier"])]
    grad_x = dy[None]
    dbias = jnp.concatenate([d.reshape(Q_HEADS, -1) for d in dbs], axis=1)
    d_table = _mm(dbias, jnp.concatenate([onehot] * DEPTH, axis=0), "t5_bias_bwd", tm=Q_HEADS, tk=3 * BLOCK * BLOCK // 8,
                  split_a=True)

    sg = {}
    for n in ("pre_mix_norm", "b_gate", "ssd_norm", "post_mix_norm", "pre_mlp_norm", "post_mlp_norm"):
        sg[n] = jnp.stack([gss[l][n].reshape(-1) for l in range(DEPTH)])
    sg["conv_w"] = jnp.stack([gss[l]["conv"][0:SSD_CONV] for l in range(DEPTH)])
    sg["conv_b"] = jnp.stack([gss[l]["conv"][SSD_CONV] for l in range(DEPTH)])
    sg["dt_bias"] = jnp.stack([gss[l]["dt_bias"][0, 0:2 * SSD_HEADS] for l in range(DEPTH)])
    sg["a_log"] = jnp.stack([gss[l]["a_log"][0, 0:2 * SSD_HEADS] for l in range(DEPTH)])
    sg["d_skip"] = jnp.stack([gss[l]["d_skip"][0, 0:SSD_HEADS] for l in range(DEPTH)])
    sg["attn_sink"] = jnp.stack([gss[l]["attn_sink"][0, 0:Q_HEADS] for l in range(DEPTH)])
    sg["rel_bias_table"] = d_table[:, 0:N_BUCKETS].T
    (small_parts,) = _communicate(_Comm("gather", [_pack_small(sg)]), "gather_small_grads", vmem=True)
    sshape = {n: W[n].shape for n, _ in SMALL}
    sshape["conv_w"] = (DEPTH, SSD_CONV, 1, CONV_DIM)
    pk = lambda d, fill: _pack_small({n: (jnp.full((DEPTH, SSD_CONV, CONV_DIM), fill, F32) if n == "conv_w" else d[n])
                                      for n, _ in SMALL}, fill)
    s_out = [_unpack_small(o, sshape) for o in _adamw(small_parts, pk(W, 1.0), pk(M, 1.0), pk(V, 1.0), "adamw_small", tr=1024)]
    conv_g = lax.dynamic_slice_in_dim(s_out[0]["conv_w"], dev * cshard, cshard, axis=3)
    c_out = _adamw(conv_g.reshape(1, DEPTH * SSD_CONV, cshard), conv_w.reshape(-1, cshard), m_conv_w.reshape(-1, cshard),
                   v_conv_w.reshape(-1, cshard), "adamw_conv_w", tr=DEPTH * SSD_CONV)
    for i in range(4):
        s_out[i]["conv_w"] = c_out[i].reshape(conv_w.shape)

    b_out = {n: None for n in BIG}
    for l in (1, 0):
        r_in, r_mi, r_rows = recv[l]
        b_out["w_in"] = _adamw_layer(r_in, 0, w_in, m_w_in, v_w_in, l, b_out["w_in"], f"adamw_w_in_{l}", tr=256)
        b_out["w_mlp_in"] = _adamw_layer(r_mi, 0, w_mlp_in, m_w_mlp_in, v_w_mlp_in, l, b_out["w_mlp_in"], f"adamw_w_mlp_in_{l}",
                                         tr=256)
        r = 0
        for n in ROWS:
            b_out[n] = _adamw_layer(r_rows, r, W[n], M[n], V[n], l, b_out[n], f"adamw_{n}_{l}")
            r += shard[n][1]

    outs = [loss, grad_x]
    for i in range(4):
        for n in names:
            outs.append(b_out[n][i] if n in BIG else s_out[i][n])
    return tuple(outs)
```

```python
import functools
import math

import numpy as np
import jax
import jax.numpy as jnp
from jax import lax
from jax.experimental import pallas as pl
from jax.experimental.pallas import tpu as pltpu

F32 = jnp.float32
BF16 = jnp.bfloat16

D_MODEL = 1024
DEPTH = 2
D_INNER = 2048
SSD_HEADS = 32
SSD_HEAD_DIM = 64
SSD_GROUPS = 8
SSD_REP = 4
SSD_STATE = 128
SSD_CONV = 5
CHUNK = 128
CONV_DIM = 4096
Q_HEADS = 16
KV_HEADS = 4
HEAD_DIM = 64
ATTN_WIDTH = 1024
KV_WIDTH = 256
WINDOW = 128
BLOCK = 128
N_BUCKETS = 32
MAX_DISTANCE = 128
D_FF = 4096
EPS = 1e-6
IN_SPLITS = (D_INNER, CONV_DIM, 2 * SSD_HEADS, ATTN_WIDTH, KV_WIDTH, KV_WIDTH, 2 * D_MODEL)
N_IN = sum(IN_SPLITS)
N_DEV = 8

ADAM_LR = 0.001
ADAM_B1 = 0.9
ADAM_B2 = 0.999
ADAM_EPS = 1e-08
ADAM_WD = 0.01
ADAM_STEP = 10

LANE = 128
HEAD_PAD = LANE
QP = Q_HEADS * HEAD_PAD
KP = KV_HEADS * HEAD_PAD
QKV = QP + 2 * KP
Q0 = 0
K0 = QP
V0 = QP + KP
XBC0 = QKV
DT0 = XBC0 + CONV_DIM
DTW = 1024
Z0 = DT0 + DTW
GT0 = Z0 + D_INNER
NP = GT0 + 2 * D_MODEL
NEG = -1e30
VMEM_LIMIT = 56 * 1024 * 1024


def _params(n_grid):
    return pltpu.CompilerParams(dimension_semantics=("arbitrary",) * n_grid, vmem_limit_bytes=VMEM_LIMIT)


def _dot(a, b, dims=((1,), (0,))):
    return lax.dot_general(a, b, (dims, ((), ())), preferred_element_type=F32)


def _mx(a):
    return a.astype(BF16)


def _split_dot(x, e):
    hi = x.astype(BF16)
    lo = (x - hi.astype(F32)).astype(BF16)
    return _dot(hi, e) + _dot(lo, e)


def _softplus(x):
    u = jnp.exp(-jnp.abs(x))
    w = 1.0 + u
    log1p = jnp.where(w == 1.0, u, jnp.log(w) * u / jnp.where(w == 1.0, 1.0, w - 1.0))
    return jnp.maximum(x, 0.0) + log1p


def _sigmoid(x):
    return 1.0 / (1.0 + jnp.exp(-x))


def _mm_body(*refs, grid, ta, tb, split_a, mode, comm):
    n_in = 3 if mode == "relu2_bwd" else 2
    n_out = 2 if mode in ("relu2", "dt_side") else 1
    ins, cin, outs, cout, (acc_ref,), csem = _split_refs(refs, n_in, n_out, comm)
    a_ref, b_ref = ins[0], ins[1]
    i, j, k = pl.program_id(0), pl.program_id(1), pl.program_id(2)
    nk = grid[2]
    if comm is not None:
        comm.run(cin, cout, csem, (i * grid[1] + j) * nk + k, grid[0] * grid[1] * nk)

    dims = ((0 if ta else 1,), (1 if tb else 0,))
    b = _mx(b_ref[...])
    if split_a:
        a32 = a_ref[...].astype(F32)
        hi = a32.astype(BF16)
        lo = (a32 - hi.astype(F32)).astype(BF16)
        part = _dot(hi, b, dims) + _dot(lo, b, dims)
    else:
        part = _dot(_mx(a_ref[...]), b, dims)

    if nk > 1:
        @pl.when(k == 0)
        def _():
            acc_ref[...] = jnp.zeros_like(acc_ref)

        acc_ref[...] += part

    @pl.when(k == nk - 1)
    def _():
        acc = part if nk == 1 else acc_ref[...]
        if mode == "relu2":
            outs[0][...] = acc.astype(outs[0].dtype)
            r = jnp.maximum(acc, 0.0)
            outs[1][...] = (r * r).astype(outs[1].dtype)
        elif mode == "relu2_bwd":
            outs[0][...] = (acc * 2.0 * jnp.maximum(ins[2][...].astype(F32), 0.0)).astype(outs[0].dtype)
        else:
            outs[0][...] = acc.astype(outs[0].dtype)
            if mode == "dt_side":
                @pl.when(j == DT0 // outs[0].shape[1])
                def _():
                    off = DT0 % outs[0].shape[1]
                    outs[1][...] = acc[:, off:off + LANE]


def _tile(n, pref):
    t = min(n, pref)
    while n % t:
        t -= LANE
    assert t > 0 and n % t == 0, (n, pref)
    return t


MM_VMEM_BUDGET = 42 * 1024 * 1024


def _mm_tiles(m, n, k, a_bytes, b_bytes, out_bytes, extra_bytes):
    tm = _tile(m, 1024)
    best = None
    for tn in (2048, 1024):
        if n % tn:
            continue
        for nk in range(1, k // LANE + 1):
            if k % nk or (k // nk) % LANE:
                continue
            tk = k // nk
            need = 2 * (tm * tk * a_bytes + tk * tn * b_bytes + tm * tn * (out_bytes + extra_bytes)) + (tm * tn * 4 if nk > 1 else 0)
            if need <= MM_VMEM_BUDGET:
                if best is None or nk < best[0]:
                    best = (nk, tn, tk)
                break
    assert best is not None, (m, n, k)
    return tm, best[1], best[2]


def _mm(a, b, name, ta=False, tb=False, out_dtype=F32, tm=None, tn=None, tk=None, split_a=False, mode=None, extra=None,
        comm=None):
    m, k = (a.shape[1], a.shape[0]) if ta else a.shape
    n = b.shape[0] if tb else b.shape[1]
    assert (b.shape[1] if tb else b.shape[0]) == k
    if tm is None:
        side = {"relu2": 2, "relu2_bwd": extra.dtype.itemsize if extra is not None else 0}.get(mode, 0)
        tm, tn, tk = _mm_tiles(m, n, k, a.dtype.itemsize, b.dtype.itemsize, jnp.dtype(out_dtype).itemsize, side)
    else:
        tm, tn, tk = _tile(m, tm), _tile(n, tn or 2048), _tile(k, tk or 1024)
    grid = (m // tm, n // tn, k // tk)
    a_spec = pl.BlockSpec((tk, tm), lambda i, j, kk: (kk, i)) if ta else pl.BlockSpec((tm, tk), lambda i, j, kk: (i, kk))
    b_spec = pl.BlockSpec((tn, tk), lambda i, j, kk: (j, kk)) if tb else pl.BlockSpec((tk, tn), lambda i, j, kk: (kk, j))
    o_spec = pl.BlockSpec((tm, tn), lambda i, j, kk: (i, j))
    in_specs, args = [a_spec, b_spec], [a, b]
    out_specs, out_shape = [o_spec], [jax.ShapeDtypeStruct((m, n), out_dtype)]
    if mode == "relu2":
        out_specs, out_shape = [o_spec, o_spec], [jax.ShapeDtypeStruct((m, n), BF16), jax.ShapeDtypeStruct((m, n), BF16)]
    elif mode == "dt_side":
        assert DT0 % tn + LANE <= tn
        out_specs = [o_spec, pl.BlockSpec((tm, LANE), lambda i, j, kk: (i, 0))]
        out_shape = out_shape + [jax.ShapeDtypeStruct((m, LANE), F32)]
    elif mode == "relu2_bwd":
        in_specs, args = in_specs + [o_spec], args + [extra]
    scratch = [pltpu.VMEM((tm, tn) if grid[2] > 1 else (8, LANE), F32)]
    if comm is not None:
        in_specs, args = in_specs + [ANY] * comm.n, args + comm.arrs
        out_specs, out_shape = out_specs + [ANY] * comm.n, out_shape + comm.out_shapes()
        scratch = scratch + comm.sems()
    res = pl.pallas_call(
        functools.partial(_mm_body, grid=grid, ta=ta, tb=tb, split_a=split_a, mode=mode, comm=comm),
        name=name, grid=grid, in_specs=in_specs, out_specs=out_specs, out_shape=out_shape, scratch_shapes=scratch,
        compiler_params=_params(3),
    )(*args)
    return res[0] if len(res) == 1 else res


def _rms_fwd_body(*refs, has_res, has_next, nsteps, comm):
    ins, cin, outs, cout, _, csem = _split_refs(refs, 2 + has_res + has_next, 1 + has_next, comm)
    if comm is not None:
        comm.run(cin, cout, csem, pl.program_id(0), nsteps)
    x_ref, g_ref = ins[:2]
    x = x_ref[...]
    y = x * lax.rsqrt(jnp.mean(x * x, axis=-1, keepdims=True) + EPS) * g_ref[...]
    if has_res:
        y = ins[2][...] + y
    if has_next:
        g2_ref = ins[3 if has_res else 2]
        outs[1][...] = (y * lax.rsqrt(jnp.mean(y * y, axis=-1, keepdims=True) + EPS) * g2_ref[...]).astype(outs[1].dtype)
    outs[0][...] = y.astype(outs[0].dtype)


def _rms_fwd(x, g, name, res=None, out_dtype=F32, next_g=None, tr=512, comm=None):
    t, d = x.shape
    tr = min(tr, t)
    row = pl.BlockSpec((tr, d), lambda i: (i, 0))
    vec = pl.BlockSpec((1, d), lambda i: (0, 0))
    args = [x, g.reshape(1, d)] + ([res] if res is not None else []) + ([next_g.reshape(1, d)] if next_g is not None else [])
    in_specs = [row, vec] + ([row] if res is not None else []) + ([vec] if next_g is not None else [])
    out_specs, out_shape = [row], [jax.ShapeDtypeStruct((t, d), out_dtype)]
    if next_g is not None:
        out_specs, out_shape = [row, row], out_shape + [jax.ShapeDtypeStruct((t, d), BF16)]
    scratch = []
    if comm is not None:
        in_specs, args = in_specs + [ANY] * comm.n, args + comm.arrs
        out_specs, out_shape = out_specs + [ANY] * comm.n, out_shape + comm.out_shapes()
        scratch = comm.sems()
    res_ = pl.pallas_call(
        functools.partial(_rms_fwd_body, has_res=res is not None, has_next=next_g is not None, nsteps=t // tr, comm=comm),
        name=name,
        grid=(t // tr,),
        in_specs=in_specs,
        out_specs=out_specs,
        out_shape=out_shape,
        scratch_shapes=scratch,
        compiler_params=_params(1),
    )(*args)
    return res_ if (next_g is not None or comm is not None) else res_[0]


def _rms_bwd_body(*refs, has_add):
    if has_add:
        x_ref, g_ref, dy_ref, add_ref, dx_ref, dg_ref = refs
    else:
        x_ref, g_ref, dy_ref, dx_ref, dg_ref = refs

    @pl.when(pl.program_id(0) == 0)
    def _():
        dg_ref[...] = jnp.zeros_like(dg_ref)

    x = x_ref[...]
    dy = dy_ref[...].astype(F32)
    rstd = lax.rsqrt(jnp.mean(x * x, axis=-1, keepdims=True) + EPS)
    n = x * rstd
    dn = dy * g_ref[...]
    dx = rstd * (dn - n * jnp.mean(dn * n, axis=-1, keepdims=True))
    if has_add:
        dx = dx + add_ref[...]
    dx_ref[...] = dx.astype(dx_ref.dtype)
    dg_ref[...] += jnp.sum(dy * n, axis=0, keepdims=True)


def _rms_bwd(x, g, dy, name, add=None, out_dtype=F32, tr=512):
    t, d = x.shape
    tr = min(tr, t)
    row = pl.BlockSpec((tr, d), lambda i: (i, 0))
    vec = pl.BlockSpec((1, d), lambda i: (0, 0))
    args = [x, g.reshape(1, d), dy] + ([add] if add is not None else [])
    return pl.pallas_call(
        functools.partial(_rms_bwd_body, has_add=add is not None),
        name=name,
        grid=(t // tr,),
        in_specs=[row, vec, row] + ([row] if add is not None else []),
        out_specs=[row, vec],
        out_shape=[jax.ShapeDtypeStruct((t, d), out_dtype), jax.ShapeDtypeStruct((1, d), F32)],
        compiler_params=_params(1),
    )(*args)


def _loss_body(y_ref, t_ref, dy_ref, l_ref):
    @pl.when(pl.program_id(0) == 0)
    def _():
        l_ref[...] = jnp.zeros_like(l_ref)

    e = y_ref[...] - t_ref[...]
    dy_ref[...] = e * (1.0 / D_MODEL)
    s = jnp.sum(e * e, axis=0, keepdims=True)
    acc = s[:, 0:LANE]
    for j in range(1, D_MODEL // LANE):
        acc = acc + s[:, j * LANE:(j + 1) * LANE]
    l_ref[...] += acc


def _loss(y, target, tr=512):
    t, d = y.shape
    tr = min(tr, t)
    row = pl.BlockSpec((tr, d), lambda i: (i, 0))
    return pl.pallas_call(
        functools.partial(_loss_body), name="loss_head", grid=(t // tr,), in_specs=[row, row],
        out_specs=[row, pl.BlockSpec((1, LANE), lambda i: (0, 0))],
        out_shape=[jax.ShapeDtypeStruct((t, d), F32), jax.ShapeDtypeStruct((1, LANE), F32)],
        compiler_params=_params(1),
    )(y, target)


CONV_CT = 128
CONV_RB = 128
CONV_PAD = 8


def _conv_rows(t):
    return min(CONV_RB, t)


def _conv_fwd_body(u_ref, w_ref, b_ref, o_ref, pad_ref, *, t):
    rb = _conv_rows(t)
    zeros = jnp.zeros((CONV_PAD, CONV_CT), F32)
    pad_ref[pl.ds(0, CONV_PAD), :] = zeros
    pad_ref[pl.ds(t + CONV_PAD, CONV_PAD), :] = zeros
    pad_ref[pl.ds(CONV_PAD, t), :] = u_ref[...].astype(F32)
    w = w_ref[...]
    b = b_ref[...]
    for c in range(t // rb):
        base = CONV_PAD + c * rb
        acc = b + w[0:1, :] * pad_ref[pl.ds(base - 2, rb), :]
        for k in range(1, SSD_CONV):
            acc = acc + w[k:k + 1, :] * pad_ref[pl.ds(base + k - 2, rb), :]
        o_ref[pl.ds(c * rb, rb), :] = acc * _sigmoid(acc)


def _conv_fwd(proj, conv_w8, conv_b):
    t = proj.shape[0]
    off = XBC0 // CONV_CT
    return pl.pallas_call(
        functools.partial(_conv_fwd_body, t=t), name="conv_fwd", grid=(CONV_DIM // CONV_CT,),
        in_specs=[pl.BlockSpec((t, CONV_CT), lambda i: (0, off + i)), pl.BlockSpec((8, CONV_CT), lambda i: (0, i)),
                  pl.BlockSpec((1, CONV_CT), lambda i: (0, i))],
        out_specs=pl.BlockSpec((t, CONV_CT), lambda i: (0, i)),
        out_shape=jax.ShapeDtypeStruct((t, CONV_DIM), F32),
        scratch_shapes=[pltpu.VMEM((t + 2 * CONV_PAD, CONV_CT), F32)],
        compiler_params=_params(1),
    )(proj, conv_w8, conv_b)


def _conv_bwd_body(u_ref, da_ref, w_ref, b_ref, _, du_ref, dw_ref, pad_ref, pad2_ref, *, t):
    rb = _conv_rows(t)
    zeros = jnp.zeros((CONV_PAD, CONV_CT), F32)
    for p in (pad_ref, pad2_ref):
        p[pl.ds(0, CONV_PAD), :] = zeros
        p[pl.ds(t + CONV_PAD, CONV_PAD), :] = zeros
    pad_ref[pl.ds(CONV_PAD, t), :] = u_ref[...].astype(F32)
    w = w_ref[...]
    b = b_ref[...]
    dw = [jnp.zeros((1, CONV_CT), F32) for _ in range(SSD_CONV + 1)]
    for c in range(t // rb):
        base = CONV_PAD + c * rb
        us = [pad_ref[pl.ds(base + k - 2, rb), :] for k in range(SSD_CONV)]
        acc = b + w[0:1, :] * us[0]
        for k in range(1, SSD_CONV):
            acc = acc + w[k:k + 1, :] * us[k]
        sg = _sigmoid(acc)
        dyc = da_ref[pl.ds(c * rb, rb), :] * (sg * (1.0 + acc * (1.0 - sg)))
        pad2_ref[pl.ds(base, rb), :] = dyc
        for k in range(SSD_CONV):
            dw[k] = dw[k] + jnp.sum(dyc * us[k], axis=0, keepdims=True)
        dw[SSD_CONV] = dw[SSD_CONV] + jnp.sum(dyc, axis=0, keepdims=True)
    for c in range(t // rb):
        base = CONV_PAD + c * rb
        acc = w[0:1, :] * pad2_ref[pl.ds(base + 2, rb), :]
        for k in range(1, SSD_CONV):
            acc = acc + w[k:k + 1, :] * pad2_ref[pl.ds(base + 2 - k, rb), :]
        du_ref[pl.ds(c * rb, rb), :] = acc.astype(du_ref.dtype)
    dw_ref[...] = jnp.concatenate(dw + [jnp.zeros((2, CONV_CT), F32)], axis=0)


def _conv_bwd(proj, dact, conv_w8, conv_b, dproj):
    t = proj.shape[0]
    off = XBC0 // CONV_CT
    col = pl.BlockSpec((t, CONV_CT), lambda i: (0, i))
    seg = pl.BlockSpec((t, CONV_CT), lambda i: (0, off + i))
    return pl.pallas_call(
        functools.partial(_conv_bwd_body, t=t), name="conv_bwd", grid=(CONV_DIM // CONV_CT,),
        in_specs=[seg, col, pl.BlockSpec((8, CONV_CT), lambda i: (0, i)), pl.BlockSpec((1, CONV_CT), lambda i: (0, i)), ANY],
        out_specs=[seg, pl.BlockSpec((8, CONV_CT), lambda i: (0, i))],
        out_shape=[jax.ShapeDtypeStruct((t, NP), BF16), jax.ShapeDtypeStruct((8, CONV_DIM), F32)],
        scratch_shapes=[pltpu.VMEM((t + 2 * CONV_PAD, CONV_CT), F32), pltpu.VMEM((t + 2 * CONV_PAD, CONV_CT), F32)],
        input_output_aliases={4: 0}, compiler_params=_params(1),
    )(proj, dact, conv_w8, conv_b, dproj)


GW = SSD_REP * SSD_HEAD_DIM


def _expand_np(hoff):
    e = np.zeros((LANE, D_INNER), np.float32)
    for h in range(SSD_HEADS):
        e[hoff + h, h * SSD_HEAD_DIM:(h + 1) * SSD_HEAD_DIM] = 1.0
    return e


def _head_masks():
    lane = lax.broadcasted_iota(jnp.int32, (1, GW), 1)
    return [((lane >= r * SSD_HEAD_DIM) & (lane < (r + 1) * SSD_HEAD_DIM)).astype(F32) for r in range(SSD_REP)]


def _ssd_common(dtc_ref, dtr_ref, bc_ref, br_ref, alc_ref, alr_ref, e_ref, rev):
    L = CHUNK
    ri = lax.broadcasted_iota(jnp.int32, (L, L), 0)
    ci = lax.broadcasted_iota(jnp.int32, (L, L), 1)
    tri = (ri <= ci) if rev else (ri >= ci)
    trit = (ri >= ci) if rev else (ri <= ci)
    raw_c = dtc_ref[...] + bc_ref[...]
    dt_c = _softplus(raw_c)
    a_c = -jnp.exp(alc_ref[...])
    cs_c = jnp.dot(tri.astype(F32), dt_c * a_c, precision=lax.Precision.HIGHEST, preferred_element_type=F32)
    dt_r = _softplus(dtr_ref[...] + br_ref[...])
    a_r = -jnp.exp(alr_ref[...])
    cs_r = jnp.dot(dt_r * a_r, trit.astype(F32), precision=lax.Precision.HIGHEST, preferred_element_type=F32)
    il = 0 if rev else L - 1
    e_c = jnp.exp(cs_c)
    w_c = jnp.exp(cs_c[il:il + 1, :] - cs_c)
    ex = _dot(_mx(jnp.concatenate([dt_c, e_c, w_c], axis=0)), e_ref[...])
    return dict(tri=tri, trit=trit, raw_c=raw_c, dt_c=dt_c, a_c=a_c, cs_c=cs_c, cs_r=cs_r, il=il, e_last=e_c[il:il + 1],
                dt_x=ex[0:L], e_x=ex[L:2 * L], w_x=ex[2 * L:3 * L], cd_x=ex[L + il:L + il + 1])


def _ssd_fwd_body(*refs, rev, hoff, nsteps, comm, has_add):
    L = CHUNK
    ins, cin, (y_ref, st_ref), cout, (s_scr,), csem = _split_refs(refs, 11 if has_add else 10, 2, comm)
    xs_ref, b_ref, c_ref, dtc_ref, dtr_ref, bc_ref, br_ref, alc_ref, alr_ref, e_ref = ins[:10]
    if comm is not None:
        comm.run(cin, cout, csem, pl.program_id(0), nsteps)

    @pl.when(pl.program_id(0) == 0)
    def _():
        s_scr[...] = jnp.zeros_like(s_scr)

    st_ref[0] = s_scr[...]
    q = _ssd_common(dtc_ref, dtr_ref, bc_ref, br_ref, alc_ref, alr_ref, e_ref, rev)
    masks = _head_masks()
    xdt = xs_ref[...] * q["dt_x"]
    xw = xdt * q["w_x"]
    early = []
    for g in range(SSD_GROUPS):
        gc = slice(g * GW, (g + 1) * GW)
        nc = slice(g * SSD_STATE, (g + 1) * SSD_STATE)
        bg = _mx(b_ref[:, nc])
        cg = _mx(c_ref[:, nc])
        sg = s_scr[:, gc]
        early.append((_dot(cg, bg, ((1,), (1,))), _dot(cg, _mx(sg)) * q["e_x"][:, gc]))
        s_scr[:, gc] = sg * q["cd_x"][:, gc] + _dot(bg, _mx(xw[:, gc]), ((0,), (0,)))
    for g in range(SSD_GROUPS):
        gc = slice(g * GW, (g + 1) * GW)
        cb, y = early[g]
        ms = []
        for r in range(SSD_REP):
            hh = hoff + g * SSD_REP + r
            diff = q["cs_c"][:, hh:hh + 1] - q["cs_r"][hh:hh + 1, :]
            ms.append(_mx(cb * jnp.exp(jnp.where(q["tri"], diff, NEG))))
        y4 = _dot(jnp.concatenate(ms, axis=0), _mx(xdt[:, gc]))
        for r in range(SSD_REP):
            y = y + y4[r * L:(r + 1) * L] * masks[r]
        y_ref[:, gc] = y + ins[10][:, gc] if has_add else y


def _dt_rows_body(p_ref, o_ref):
    o_ref[...] = p_ref[...].T


def _dt_rows(proj, tr=512):
    t = proj.shape[0]
    tr = min(tr, t)
    return pl.pallas_call(
        functools.partial(_dt_rows_body), name="dt_rows", grid=(t // tr,),
        in_specs=[pl.BlockSpec((tr, LANE), lambda i: (i, 0))], out_specs=pl.BlockSpec((LANE, tr), lambda i: (0, i)),
        out_shape=jax.ShapeDtypeStruct((LANE, t), F32), compiler_params=_params(1),
    )(proj)


def _ssd_specs(t, rev):
    nc = t // CHUNK
    cm = (lambda i: nc - 1 - i) if rev else (lambda i: i)
    xs = pl.BlockSpec((CHUNK, D_INNER), lambda i: (cm(i), 0))
    bb = pl.BlockSpec((CHUNK, SSD_GROUPS * SSD_STATE), lambda i: (cm(i), 2))
    cc = pl.BlockSpec((CHUNK, SSD_GROUPS * SSD_STATE), lambda i: (cm(i), 3))
    dtc = pl.BlockSpec((CHUNK, LANE), lambda i: (cm(i), 0))
    dtr = pl.BlockSpec((LANE, CHUNK), lambda i: (0, cm(i)))
    colv = pl.BlockSpec((1, LANE), lambda i: (0, 0))
    rowv = pl.BlockSpec((LANE, 1), lambda i: (0, 0))
    return cm, xs, bb, cc, dtc, dtr, colv, rowv


def _ssd_fwd(xbc, proj, dtt, bias, alog, rev, comm=None, add=None):
    t = xbc.shape[0]
    nc = t // CHUNK
    cm, xs, bb, cc, dtc, dtr, colv, rowv = _ssd_specs(t, rev)
    hoff = SSD_HEADS if rev else 0
    e = jnp.asarray(_expand_np(hoff), BF16)
    in_specs = [xs, bb, cc, dtc, dtr, colv, rowv, colv, rowv, pl.BlockSpec((LANE, D_INNER), lambda i: (0, 0))]
    args = [xbc, xbc, xbc, proj, dtt, bias.reshape(1, LANE), bias.reshape(LANE, 1), alog.reshape(1, LANE), alog.reshape(LANE, 1), e]
    out_specs = [xs, pl.BlockSpec((1, SSD_STATE, D_INNER), lambda i: (cm(i), 0, 0))]
    out_shape = [jax.ShapeDtypeStruct((t, D_INNER), F32), jax.ShapeDtypeStruct((nc, SSD_STATE, D_INNER), F32)]
    scratch = [pltpu.VMEM((SSD_STATE, D_INNER), F32)]
    if add is not None:
        in_specs, args = in_specs + [xs], args + [add]
    if comm is not None:
        in_specs, args = in_specs + [ANY] * comm.n, args + comm.arrs
        out_specs, out_shape = out_specs + [ANY] * comm.n, out_shape + comm.out_shapes()
        scratch = scratch + comm.sems()
    return pl.pallas_call(
        functools.partial(_ssd_fwd_body, rev=rev, hoff=hoff, nsteps=nc, comm=comm, has_add=add is not None), name="ssd_fwd_rev" if rev else "ssd_fwd",
        grid=(nc,), in_specs=in_specs, out_specs=out_specs, out_shape=out_shape, scratch_shapes=scratch,
        compiler_params=_params(1),
    )(*args)


def _ssd_bwd_body(*refs, rev, hoff, first, nsteps, comm):
    L = CHUNK
    ins, cin, (dx_ref, ddt_ref, sm_ref), cout, (ds_scr,), csem = _split_refs(refs, 14 if first else 18, 3, comm)
    (xs_ref, b_ref, c_ref, dtc_ref, dtr_ref, bc_ref, br_ref, alc_ref, alr_ref, e_ref, et_ref, dy_ref, sp_ref,
     dsk_ref) = ins[:14]
    if not first:
        pdx_ref, pddt_ref, psm_ref = ins[14:17]
    if comm is not None:
        comm.run(cin, cout, csem, pl.program_id(0), nsteps)

    @pl.when(pl.program_id(0) == 0)
    def _():
        ds_scr[...] = jnp.zeros_like(ds_scr)
        sm_ref[...] = jnp.zeros_like(sm_ref) if first else psm_ref[...]

    q = _ssd_common(dtc_ref, dtr_ref, bc_ref, br_ref, alc_ref, alr_ref, e_ref, rev)
    masks = _head_masks()
    xs = xs_ref[...]
    dy = dy_ref[...]
    xdt = xs * q["dt_x"]
    xw = xdt * q["w_x"]
    dye = dy * q["e_x"]
    ds_old = ds_scr[...]
    dxdt_parts, q1_parts, q2_parts = [], [], []
    ddiag = jnp.zeros((L, LANE), F32)
    early = []
    for g in range(SSD_GROUPS):
        gc = slice(g * GW, (g + 1) * GW)
        nc = slice(g * SSD_STATE, (g + 1) * SSD_STATE)
        bg = _mx(b_ref[:, nc])
        cg = _mx(c_ref[:, nc])
        dy_g = _mx(dy[:, gc])
        xdt_g = _mx(xdt[:, gc])
        dsg = ds_old[:, gc]
        spg = _mx(sp_ref[0, :, gc])
        dye_g = _mx(dye[:, gc])
        bds = _dot(bg, _mx(dsg))
        early.append(dict(
            bg=bg, cg=cg, dy_g=dy_g, bds=bds,
            cb=_dot(cg, bg, ((1,), (1,))), cbt=_dot(bg, cg, ((1,), (1,))),
            dm=_dot(_mx(jnp.concatenate([dy[:, gc] * m for m in masks], axis=0)), xdt_g, ((1,), (1,))),
            dmt=_dot(_mx(jnp.concatenate([xdt[:, gc] * m for m in masks], axis=0)), dy_g, ((1,), (1,))),
            dc=_dot(dye_g, spg, ((1,), (1,))), db=_dot(_mx(xw[:, gc]), _mx(dsg), ((1,), (1,)))))
        q1_parts.append(dye[:, gc] * _dot(cg, spg))
        q2_parts.append(xw[:, gc] * bds)
        ds_scr[:, gc] = dsg * q["cd_x"][:, gc] + _dot(cg, dye_g, ((0,), (0,)))
    for g in range(SSD_GROUPS):
        gc = slice(g * GW, (g + 1) * GW)
        e = early[g]
        bg, cg, dy_g, bds, cb, cbt, dm, dmt = (e[k] for k in ("bg", "cg", "dy_g", "bds", "cb", "cbt", "dm", "dmt"))
        dcb = jnp.zeros((L, L), F32)
        dcbt = jnp.zeros((L, L), F32)
        mts = []
        for r in range(SSD_REP):
            hh = hoff + g * SSD_REP + r
            col = q["cs_c"][:, hh:hh + 1]
            row = q["cs_r"][hh:hh + 1, :]
            dec = jnp.exp(jnp.where(q["tri"], col - row, NEG))
            dect = jnp.exp(jnp.where(q["trit"], row - col, NEG))
            pd = dm[r * L:(r + 1) * L] * dec
            pdt = dmt[r * L:(r + 1) * L] * dect
            dcb = dcb + pd
            dcbt = dcbt + pdt
            mts.append(_mx(cbt * dect))
            ddiag = ddiag + (jnp.sum(pd * cb, axis=1, keepdims=True) - jnp.sum(pdt * cbt, axis=1, keepdims=True)) * _onehot_lane(hh)
        x4 = _dot(jnp.concatenate(mts, axis=0), dy_g)
        dxdt_g = bds * q["w_x"][:, gc]
        for r in range(SSD_REP):
            dxdt_g = dxdt_g + x4[r * L:(r + 1) * L] * masks[r]
        dxdt_parts.append(dxdt_g)
        dc_g = _dot(_mx(dcb), bg) + e["dc"]
        db_g = _dot(_mx(dcbt), cg) + e["db"]
        boff = D_INNER + g * SSD_STATE
        coff = D_INNER + SSD_GROUPS * SSD_STATE + g * SSD_STATE
        if first:
            dx_ref[:, boff:boff + SSD_STATE] = db_g
            dx_ref[:, coff:coff + SSD_STATE] = dc_g
        else:
            dx_ref[:, boff:boff + SSD_STATE] = pdx_ref[:, boff:boff + SSD_STATE] + db_g
            dx_ref[:, coff:coff + SSD_STATE] = pdx_ref[:, coff:coff + SSD_STATE] + dc_g
    dxdt = jnp.concatenate(dxdt_parts, axis=1)
    et = et_ref[...]
    hs = _dot(_mx(jnp.concatenate([jnp.concatenate(q1_parts, axis=1), jnp.concatenate(q2_parts, axis=1), dxdt * xs], axis=0)), et)
    q1, q2, r3 = hs[0:L], hs[L:2 * L], hs[2 * L:3 * L]
    t_prev = _split_dot(jnp.sum(ds_old * sp_ref[0], axis=0, keepdims=True), et) * q["e_last"]
    rows = lax.broadcasted_iota(jnp.int32, (L, LANE), 0)
    dcs = ddiag + q1 - q2 + jnp.where(rows == q["il"], jnp.sum(q2, axis=0, keepdims=True) + t_prev, 0.0)
    dad = jnp.dot(q["trit"].astype(F32), dcs, precision=lax.Precision.HIGHEST, preferred_element_type=F32)
    ddt = dad * q["a_c"] + r3
    ddt_raw = ddt * _sigmoid(q["raw_c"])
    dal = jnp.sum(dad * q["dt_c"], axis=0, keepdims=True) * q["a_c"]
    dbias = jnp.sum(ddt_raw, axis=0, keepdims=True)
    sm_ref[...] += jnp.concatenate([dal, dbias, jnp.zeros((6, LANE), F32)], axis=0)
    dxs = dxdt * q["dt_x"]
    if first:
        dx_ref[:, 0:D_INNER] = dxs + dy * dsk_ref[...]
        ddt_ref[...] = ddt_raw
    else:
        dx_ref[:, 0:D_INNER] = pdx_ref[:, 0:D_INNER] + dxs
        ddt_ref[...] = jnp.concatenate([(pddt_ref[...] + ddt_raw).astype(ddt_ref.dtype), jnp.zeros((L, DTW - LANE), ddt_ref.dtype)],
                                       axis=1)


def _ssd_bwd(xbc, proj, dtt, bias, alog, dy, states, dskip_x, rev, prev=None, dproj=None, comm=None):
    t = xbc.shape[0]
    nc = t // CHUNK
    first = prev is None
    cm, xs, bb, cc, dtc, dtr, colv, rowv = _ssd_specs(t, not rev)
    hoff = SSD_HEADS if rev else 0
    e_np = _expand_np(hoff)
    e = jnp.asarray(e_np, BF16)
    et = jnp.asarray(e_np.T, BF16)
    st_spec = pl.BlockSpec((1, SSD_STATE, D_INNER), lambda i: (cm(i), 0, 0))
    dxo = pl.BlockSpec((CHUNK, CONV_DIM), lambda i: (cm(i), 0))
    ddto = pl.BlockSpec((CHUNK, LANE), lambda i: (cm(i), 0))
    smo = pl.BlockSpec((8, LANE), lambda i: (0, 0))
    in_specs = [xs, bb, cc, dtc, dtr, colv, rowv, colv, rowv, pl.BlockSpec((LANE, D_INNER), lambda i: (0, 0)),
                pl.BlockSpec((D_INNER, LANE), lambda i: (0, 0)), xs, st_spec, pl.BlockSpec((1, D_INNER), lambda i: (0, 0))]
    args = [xbc, xbc, xbc, proj, dtt, bias.reshape(1, LANE), bias.reshape(LANE, 1), alog.reshape(1, LANE),
            alog.reshape(LANE, 1), e, et, dy, states, dskip_x]
    out_specs = [dxo, ddto, smo]
    out_shape = [jax.ShapeDtypeStruct((t, CONV_DIM), F32), jax.ShapeDtypeStruct((t, LANE), F32),
                 jax.ShapeDtypeStruct((8, LANE), F32)]
    aliases = {}
    if not first:
        in_specs += [dxo, ddto, smo, ANY]
        args += list(prev) + [dproj]
        out_specs[1] = pl.BlockSpec((CHUNK, DTW), lambda i: (cm(i), DT0 // DTW))
        out_shape[1] = jax.ShapeDtypeStruct((t, NP), BF16)
        aliases = {17: 1}
    scratch = [pltpu.VMEM((SSD_STATE, D_INNER), F32)]
    if comm is not None:
        in_specs, args = in_specs + [ANY] * comm.n, args + comm.arrs
        out_specs, out_shape = out_specs + [ANY] * comm.n, out_shape + comm.out_shapes()
        scratch = scratch + comm.sems()
    return pl.pallas_call(
        functools.partial(_ssd_bwd_body, rev=rev, hoff=hoff, first=first, nsteps=nc, comm=comm),
        name="ssd_bwd_rev" if rev else "ssd_bwd", grid=(nc,), in_specs=in_specs, out_specs=out_specs, out_shape=out_shape,
        scratch_shapes=scratch, input_output_aliases=aliases, compiler_params=_params(1),
    )(*args)


def _gnorm_parts(y_ref, xs_ref, z_ref, dsk_ref):
    xs = xs_ref[...]
    y = y_ref[...] + xs * dsk_ref[...]
    z = z_ref[...].astype(F32)
    sg = _sigmoid(z)
    s = z * sg
    v = y * s
    rs = []
    for g in range(SSD_GROUPS):
        vg = v[:, g * GW:(g + 1) * GW]
        rs.append(jnp.broadcast_to(lax.rsqrt(jnp.mean(vg * vg, axis=-1, keepdims=True) + EPS), vg.shape))
    return xs, y, z, sg, s, v, jnp.concatenate(rs, axis=1)


def _gnorm_fwd_body(y_ref, xs_ref, z_ref, dsk_ref, w_ref, o_ref):
    _, _, _, _, _, v, rstd = _gnorm_parts(y_ref, xs_ref, z_ref, dsk_ref)
    o_ref[...] = (v * rstd * w_ref[...]).astype(o_ref.dtype)


def _gnorm_fwd(y, xbc, proj, dskip_x, norm_w, tr=256):
    t = y.shape[0]
    tr = min(tr, t)
    row = pl.BlockSpec((tr, D_INNER), lambda i: (i, 0))
    vec = pl.BlockSpec((1, D_INNER), lambda i: (0, 0))
    return pl.pallas_call(
        functools.partial(_gnorm_fwd_body), name="gnorm_fwd", grid=(t // tr,),
        in_specs=[row, row, pl.BlockSpec((tr, D_INNER), lambda i: (i, Z0 // D_INNER)), vec, vec],
        out_specs=row, out_shape=jax.ShapeDtypeStruct((t, D_INNER), BF16), compiler_params=_params(1),
    )(y, xbc, proj, dskip_x, norm_w.reshape(1, D_INNER))


def _gnorm_bwd_body(du_ref, y_ref, xs_ref, z_ref, dsk_ref, w_ref, et_ref, _, dy_ref, dz_ref, dw_ref, dd_ref, acc_ref, *,
                    nsteps):
    i = pl.program_id(0)

    @pl.when(i == 0)
    def _():
        dw_ref[...] = jnp.zeros_like(dw_ref)
        acc_ref[...] = jnp.zeros_like(acc_ref)

    xs, y, z, sg, s, v, rstd = _gnorm_parts(y_ref, xs_ref, z_ref, dsk_ref)
    du = du_ref[...].astype(F32)
    n = v * rstd
    dn = du * w_ref[...]
    dw_ref[...] += jnp.sum(du * n, axis=0, keepdims=True)
    prod = dn * n
    means = []
    for g in range(SSD_GROUPS):
        pg = prod[:, g * GW:(g + 1) * GW]
        means.append(jnp.broadcast_to(jnp.mean(pg, axis=-1, keepdims=True), pg.shape))
    dv = rstd * (dn - n * jnp.concatenate(means, axis=1))
    dy = dv * s
    dy_ref[...] = dy
    dz_ref[...] = (dv * y * (sg * (1.0 + z * (1.0 - sg)))).astype(dz_ref.dtype)
    acc_ref[...] += jnp.sum(dy * xs, axis=0, keepdims=True)

    @pl.when(i == nsteps - 1)
    def _():
        dd_ref[...] = _split_dot(acc_ref[...], et_ref[...])


def _gnorm_bwd(du, y, xbc, proj, dskip_x, norm_w, dproj, tr=256):
    t = y.shape[0]
    tr = min(tr, t)
    row = pl.BlockSpec((tr, D_INNER), lambda i: (i, 0))
    zcol = pl.BlockSpec((tr, D_INNER), lambda i: (i, Z0 // D_INNER))
    vec = pl.BlockSpec((1, D_INNER), lambda i: (0, 0))
    et = jnp.asarray(_expand_np(0).T, BF16)
    return pl.pallas_call(
        functools.partial(_gnorm_bwd_body, nsteps=t // tr), name="gnorm_bwd", grid=(t // tr,),
        in_specs=[row, row, row, zcol, vec, vec, pl.BlockSpec((D_INNER, LANE), lambda i: (0, 0)), ANY],
        out_specs=[row, zcol, vec, pl.BlockSpec((1, LANE), lambda i: (0, 0))],
        out_shape=[jax.ShapeDtypeStruct((t, D_INNER), F32), jax.ShapeDtypeStruct((t, NP), BF16),
                   jax.ShapeDtypeStruct((1, D_INNER), F32), jax.ShapeDtypeStruct((1, LANE), F32)],
        scratch_shapes=[pltpu.VMEM((1, D_INNER), F32)], input_output_aliases={7: 1},
        compiler_params=_params(1),
    )(du, y, xbc, proj, dskip_x, norm_w.reshape(1, D_INNER), et, dproj)


def _gate_fwd_body(gt_ref, b_ref, ys_ref, ya_ref, o_ref):
    g = _sigmoid(gt_ref[...].astype(F32) + b_ref[...])
    o_ref[...] = (g[:, 0:D_MODEL] * ys_ref[...] + g[:, D_MODEL:] * ya_ref[...]).astype(o_ref.dtype)


def _gate_fwd(proj, b_gate, y_ssd, y_attn, tr=512):
    t = y_ssd.shape[0]
    tr = min(tr, t)
    row = pl.BlockSpec((tr, D_MODEL), lambda i: (i, 0))
    return pl.pallas_call(
        functools.partial(_gate_fwd_body), name="gate_fwd", grid=(t // tr,),
        in_specs=[pl.BlockSpec((tr, 2 * D_MODEL), lambda i: (i, GT0 // (2 * D_MODEL))),
                  pl.BlockSpec((1, 2 * D_MODEL), lambda i: (0, 0)), row, row],
        out_specs=row, out_shape=jax.ShapeDtypeStruct((t, D_MODEL), BF16), compiler_params=_params(1),
    )(proj, b_gate.reshape(1, 2 * D_MODEL), y_ssd, y_attn)


def _gate_bwd_body(dm_ref, gt_ref, b_ref, ys_ref, ya_ref, dys_ref, dya_ref, dgt_ref, db_ref):
    @pl.when(pl.program_id(0) == 0)
    def _():
        db_ref[...] = jnp.zeros_like(db_ref)

    g = _sigmoid(gt_ref[...].astype(F32) + b_ref[...])
    gs = g[:, 0:D_MODEL]
    ga = g[:, D_MODEL:]
    dm = dm_ref[...].astype(F32)
    dys_ref[...] = (dm * gs).astype(dys_ref.dtype)
    dya_ref[...] = (dm * ga).astype(dya_ref.dtype)
    dgt = jnp.concatenate([dm * ys_ref[...] * gs * (1.0 - gs), dm * ya_ref[...] * ga * (1.0 - ga)], axis=1)
    dgt_ref[...] = dgt.astype(dgt_ref.dtype)
    db_ref[...] += jnp.sum(dgt, axis=0, keepdims=True)


def _gate_bwd(dmix, proj, b_gate, y_ssd, y_attn, tr=512):
    t = y_ssd.shape[0]
    tr = min(tr, t)
    row = pl.BlockSpec((tr, D_MODEL), lambda i: (i, 0))
    gates = pl.BlockSpec((tr, 2 * D_MODEL), lambda i: (i, GT0 // (2 * D_MODEL)))
    vec2 = pl.BlockSpec((1, 2 * D_MODEL), lambda i: (0, 0))
    return pl.pallas_call(
        functools.partial(_gate_bwd_body), name="gate_bwd", grid=(t // tr,),
        in_specs=[row, gates, vec2, row, row],
        out_specs=[row, row, gates, vec2],
        out_shape=[jax.ShapeDtypeStruct((t, D_MODEL), BF16), jax.ShapeDtypeStruct((t, D_MODEL), BF16),
                   jax.ShapeDtypeStruct((t, NP), BF16), jax.ShapeDtypeStruct((1, 2 * D_MODEL), F32)],
        compiler_params=_params(1),
    )(dmix, proj, b_gate.reshape(1, 2 * D_MODEL), y_ssd, y_attn)


def _bucket_np():
    i = np.arange(BLOCK)[:, None]
    j = np.arange(3 * BLOCK)[None, :]
    rel = j - BLOCK - i
    nb = N_BUCKETS // 2
    max_exact = nb // 2
    ret = np.where(rel > 0, nb, 0)
    n = np.abs(rel)
    nf = np.maximum(n, 1).astype(np.float32)
    large = max_exact + (np.log(nf / np.float32(max_exact)) / np.float32(math.log(MAX_DISTANCE / max_exact))
                         * np.float32(nb - max_exact)).astype(np.int32)
    large = np.minimum(large, nb - 1)
    bucket = ret + np.where(n < max_exact, n, large)
    band = np.abs(rel) <= WINDOW
    return bucket, band


def _onehot_np():
    bucket, _ = _bucket_np()
    oh = np.zeros((BLOCK * 3 * BLOCK, LANE), np.float32)
    oh[np.arange(oh.shape[0]), bucket.reshape(-1)] = 1.0
    return oh


REP = Q_HEADS // KV_HEADS


def _group_cols(ref, g):
    return jnp.concatenate([ref[:, (g * REP + r) * HEAD_PAD:(g * REP + r + 1) * HEAD_PAD] for r in range(REP)], axis=0)


def _group_lane(ref, g, rows):
    return jnp.concatenate([jnp.broadcast_to(ref[:, g * REP + r:g * REP + r + 1], (rows, 1)) for r in range(REP)], axis=0)


def _attn_kv(refs, g):
    gc = slice(g * HEAD_PAD, (g + 1) * HEAD_PAD)
    return _mx(jnp.concatenate([r[:, gc] for r in refs], axis=0))


def _attn_logits(q_ref, kg, bias_ref, edge, g):
    q4 = _mx(_group_cols(q_ref, g))
    s = _dot(q4, kg, ((1,), (1,))) * (HEAD_DIM ** -0.5) + bias_ref[g] + edge
    return q4, s


def _attn_edge(n, nb):
    col = lax.broadcasted_iota(jnp.int32, (1, 3 * BLOCK), 1)
    bad = ((col < BLOCK) & (n == 0)) | ((col >= 2 * BLOCK) & (n == nb - 1))
    return jnp.where(bad, NEG, 0.0)


def _onehot_lane(h):
    return (lax.broadcasted_iota(jnp.int32, (1, LANE), 1) == h).astype(F32)


def _attn_fwd_body(*refs, nb, comm):
    ins, cin, (o_ref, lse_ref), cout, _, csem = _split_refs(refs, 9, 2, comm)
    q_ref, kp_ref, kc_ref, kn_ref, vp_ref, vc_ref, vn_ref, bias_ref, sink_ref = ins
    n = pl.program_id(0)
    if comm is not None:
        comm.run(cin, cout, csem, n, nb)
    edge = _attn_edge(n, nb)
    lse = jnp.zeros((BLOCK, LANE), F32)
    logits = [_attn_logits(q_ref, _attn_kv((kp_ref, kc_ref, kn_ref), g), bias_ref, edge, g)[1] for g in range(KV_HEADS)]
    for g in range(KV_HEADS):
        vg = _attn_kv((vp_ref, vc_ref, vn_ref), g)
        s = logits[g]
        sink = _group_lane(sink_ref, g, BLOCK)
        m = jnp.maximum(jnp.max(s, axis=-1, keepdims=True), sink)
        p = jnp.exp(s - m)
        den = jnp.sum(p, axis=-1, keepdims=True) + jnp.exp(sink - m)
        o4 = _dot(_mx(p * (1.0 / den)), vg)
        l4 = m + jnp.log(den)
        for r in range(REP):
            h = g * REP + r
            o_ref[:, h * HEAD_PAD:(h + 1) * HEAD_PAD] = o4[r * BLOCK:(r + 1) * BLOCK].astype(o_ref.dtype)
            lse = lse + l4[r * BLOCK:(r + 1) * BLOCK] * _onehot_lane(h)
    lse_ref[...] = lse


def _attn_specs(t, clamp):
    nb = t // BLOCK
    cur = (lambda n: jnp.minimum(n, nb - 1)) if clamp else (lambda n: n)
    prv = lambda n: jnp.maximum(cur(n) - 1, 0)
    nxt = lambda n: jnp.minimum(cur(n) + 1, nb - 1)
    kb, vb = K0 // KP, V0 // KP
    qs = pl.BlockSpec((BLOCK, QP), lambda n: (cur(n), Q0 // QP))
    ks = [pl.BlockSpec((BLOCK, KP), lambda n, f=f: (f(n), kb)) for f in (prv, cur, nxt)]
    vs = [pl.BlockSpec((BLOCK, KP), lambda n, f=f: (f(n), vb)) for f in (prv, cur, nxt)]
    consts = [pl.BlockSpec((KV_HEADS, REP * BLOCK, 3 * BLOCK), lambda n: (0, 0, 0)), pl.BlockSpec((1, LANE), lambda n: (0, 0))]
    return nb, cur, qs, ks, vs, consts


def _attn_fwd(proj, bias, sink, comm=None):
    t = proj.shape[0]
    nb, cur, qs, ks, vs, consts = _attn_specs(t, False)
    in_specs, args = [qs] + ks + vs + consts, [proj] * 7 + [bias, sink]
    out_specs = [pl.BlockSpec((BLOCK, QP), lambda n: (n, 0)), pl.BlockSpec((BLOCK, LANE), lambda n: (n, 0))]
    out_shape = [jax.ShapeDtypeStruct((t, QP), BF16), jax.ShapeDtypeStruct((t, LANE), F32)]
    scratch = []
    if comm is not None:
        in_specs, args = in_specs + [ANY] * comm.n, args + comm.arrs
        out_specs, out_shape = out_specs + [ANY] * comm.n, out_shape + comm.out_shapes()
        scratch = comm.sems()
    return pl.pallas_call(
        functools.partial(_attn_fwd_body, nb=nb, comm=comm), name="attn_fwd", grid=(nb,),
        in_specs=in_specs, out_specs=out_specs, out_shape=out_shape, scratch_shapes=scratch, compiler_params=_params(1),
    )(*args)


def _attn_bwd_body(*refs, nb, comm):
    ins, cin, (dqkv_ref, dbias_ref, dsink_ref), cout, (ak, bk, av, bv, dq_scr), csem = _split_refs(refs, 13, 3, comm)
    q_ref, kp_ref, kc_ref, kn_ref, vp_ref, vc_ref, vn_ref, bias_ref, sink_ref, do_ref, o_ref, lse_ref = ins[:12]
    dk_ref = dqkv_ref.at[:, K0:K0 + KP]
    dv_ref = dqkv_ref.at[:, V0:V0 + KP]
    n = pl.program_id(0)
    scale = HEAD_DIM ** -0.5
    if comm is not None:
        comm.run(cin, cout, csem, n, nb + 1)

    @pl.when(n == 0)
    def _():
        for r in (ak, bk, av, bv, dq_scr, dbias_ref, dsink_ref):
            r[...] = jnp.zeros_like(r)

    dqkv_ref[:, Q0:Q0 + QP] = dq_scr[...]

    @pl.when(n < nb)
    def _():
        edge = _attn_edge(n, nb)
        dsink = jnp.zeros((1, LANE), F32)
        early = []
        for g in range(KV_HEADS):
            kg = _attn_kv((kp_ref, kc_ref, kn_ref), g)
            q4, s = _attn_logits(q_ref, kg, bias_ref, edge, g)
            do = _group_cols(do_ref, g).astype(F32)
            early.append((kg, q4, s, do, _dot(_mx(do), _attn_kv((vp_ref, vc_ref, vn_ref), g), ((1,), (1,)))))
        for g in range(KV_HEADS):
            gc = slice(g * HEAD_PAD, (g + 1) * HEAD_PAD)
            kg, q4, s, do, dp = early[g]
            lse = _group_lane(lse_ref, g, BLOCK)
            p = jnp.exp(s - lse)
            delta = jnp.sum(do * _group_cols(o_ref, g).astype(F32), axis=-1, keepdims=True)
            ds = p * (dp - delta)
            sunk = jnp.exp(_group_lane(sink_ref, g, BLOCK) - lse) * delta
            dbias_ref[g] += ds
            dsb = _mx(ds)
            dq4 = _dot(dsb, kg) * scale
            for r in range(REP):
                h = g * REP + r
                dq_scr[:, h * HEAD_PAD:(h + 1) * HEAD_PAD] = dq4[r * BLOCK:(r + 1) * BLOCK].astype(dq_scr.dtype)
                dsink = dsink - jnp.sum(sunk[r * BLOCK:(r + 1) * BLOCK], axis=0, keepdims=True) * _onehot_lane(h)
            dk_c = _dot(dsb, q4, ((0,), (0,))) * scale
            dv_c = _dot(_mx(p), _mx(do), ((0,), (0,)))
            for acc_a, acc_b, out, c in ((ak, bk, dk_ref, dk_c), (av, bv, dv_ref, dv_c)):
                out[:, gc] = (acc_a[:, gc] + c[0:BLOCK]).astype(out.dtype)
                acc_a[:, gc] = acc_b[:, gc] + c[BLOCK:2 * BLOCK]
                acc_b[:, gc] = c[2 * BLOCK:3 * BLOCK]
        dsink_ref[...] += dsink

    @pl.when(n == nb)
    def _():
        dk_ref[...] = ak[...].astype(dk_ref.dtype)
        dv_ref[...] = av[...].astype(dv_ref.dtype)


def _attn_bwd(proj, bias, sink, do, o, lse, dproj, comm=None):
    t = proj.shape[0]
    nb, cur, qs, ks, vs, consts = _attn_specs(t, True)
    rowq = pl.BlockSpec((BLOCK, QP), lambda n: (cur(n), 0))
    late = pl.BlockSpec((BLOCK, QKV), lambda n: (jnp.maximum(n - 1, 0), Q0 // QKV))
    in_specs = [qs] + ks + vs + consts + [rowq, rowq, pl.BlockSpec((BLOCK, LANE), lambda n: (cur(n), 0)), ANY]
    args = [proj] * 7 + [bias, sink, do, o, lse, dproj]
    out_specs = [late, pl.BlockSpec((KV_HEADS, REP * BLOCK, 3 * BLOCK), lambda n: (0, 0, 0)), pl.BlockSpec((1, LANE), lambda n: (0, 0))]
    out_shape = [jax.ShapeDtypeStruct((t, NP), BF16), jax.ShapeDtypeStruct((KV_HEADS, REP * BLOCK, 3 * BLOCK), F32),
                 jax.ShapeDtypeStruct((1, LANE), F32)]
    scratch = [pltpu.VMEM((BLOCK, KP), F32) for _ in range(4)] + [pltpu.VMEM((BLOCK, QP), BF16)]
    if comm is not None:
        in_specs, args = in_specs + [ANY] * comm.n, args + comm.arrs
        out_specs, out_shape = out_specs + [ANY] * comm.n, out_shape + comm.out_shapes()
        scratch = scratch + comm.sems()
    return pl.pallas_call(
        functools.partial(_attn_bwd_body, nb=nb, comm=comm), name="attn_bwd", grid=(nb + 1,),
        in_specs=in_specs, out_specs=out_specs, out_shape=out_shape, scratch_shapes=scratch, input_output_aliases={12: 0},
        compiler_params=_params(1),
    )(*args)


def _adamw_body(gp_ref, w_ref, m_ref, v_ref, g_ref, d_ref, nm_ref, nv_ref, *, nparts):
    g = gp_ref[0].astype(F32)
    for j in range(1, nparts):
        g = g + gp_ref[j].astype(F32)
    m = ADAM_B1 * m_ref[...] + (1.0 - ADAM_B1) * g
    v = ADAM_B2 * v_ref[...] + (1.0 - ADAM_B2) * (g * g)
    m_hat = m / (1.0 - ADAM_B1 ** ADAM_STEP)
    v_hat = v / (1.0 - ADAM_B2 ** ADAM_STEP)
    g_ref[...] = g
    d_ref[...] = -ADAM_LR * (m_hat / (jnp.sqrt(v_hat) + ADAM_EPS) + ADAM_WD * w_ref[...])
    nm_ref[...] = m
    nv_ref[...] = v


def _adamw_layer_body(*refs, nparts):
    gp_ref, w_ref, m_ref, v_ref = refs[:4]
    g_ref, d_ref, nm_ref, nv_ref = refs[-4:]
    g = gp_ref[0].astype(F32)
    for j in range(1, nparts):
        g = g + gp_ref[j].astype(F32)
    m = ADAM_B1 * m_ref[0] + (1.0 - ADAM_B1) * g
    v = ADAM_B2 * v_ref[0] + (1.0 - ADAM_B2) * (g * g)
    m_hat = m / (1.0 - ADAM_B1 ** ADAM_STEP)
    v_hat = v / (1.0 - ADAM_B2 ** ADAM_STEP)
    g_ref[0] = g
    d_ref[0] = -ADAM_LR * (m_hat / (jnp.sqrt(v_hat) + ADAM_EPS) + ADAM_WD * w_ref[0])
    nm_ref[0] = m
    nv_ref[0] = v


def _adamw_layer(gparts, row0, w, m, v, l, prev, name, tr=128):
    p, c = gparts.shape[0], gparts.shape[2]
    per = w.shape[1]
    tr = min(tr, per)
    assert per % tr == 0 and row0 % tr == 0 and w.shape[2] == c
    lay = pl.BlockSpec((1, tr, c), lambda i: (l, i, 0))
    in_specs = [pl.BlockSpec((p, tr, c), lambda i: (0, row0 // tr + i, 0)), lay, lay, lay]
    args = [gparts, w, m, v]
    aliases = {}
    if prev is not None:
        in_specs, args, aliases = in_specs + [ANY] * 4, args + list(prev), {4 + k: k for k in range(4)}
    sd = jax.ShapeDtypeStruct(w.shape, F32)
    return pl.pallas_call(
        functools.partial(_adamw_layer_body, nparts=p), name=name, grid=(per // tr,),
        in_specs=in_specs, out_specs=[lay] * 4, out_shape=[sd] * 4, input_output_aliases=aliases, compiler_params=_params(1),
    )(*args)


def _adamw(gparts, w, m, v, name, tr=256):
    p, r, c = gparts.shape
    tr = min(tr, r)
    assert r % tr == 0
    row = pl.BlockSpec((tr, c), lambda i: (i, 0))
    sd = jax.ShapeDtypeStruct((r, c), F32)
    return pl.pallas_call(
        functools.partial(_adamw_body, nparts=p), name=name, grid=(r // tr,),
        in_specs=[pl.BlockSpec((p, tr, c), lambda i: (0, i, 0)), row, row, row],
        out_specs=[row, row, row, row], out_shape=[sd, sd, sd, sd], compiler_params=_params(1),
    )(gparts, w, m, v)


MESH = pl.DeviceIdType.MESH
N_REL = N_DEV - 1


def _gather_phases(ins, outs, sems):
    n = len(ins)
    send_sems, recv_sems, local_sems = sems
    x, y, c = lax.axis_index("x"), lax.axis_index("y"), lax.axis_index("c")
    me, sibling = (x, y, c), (x, y, 1 - c)
    chips = [(1 - x, y), (x, 1 - y), (1 - x, 1 - y)]

    def slot(a, p):
        return outs[a].at[4 * p[0] + 2 * p[1] + p[2]]

    def copy(a, k, block, to, src=None):
        return pltpu.make_async_remote_copy(
            src_ref=slot(a, block) if src is None else src, dst_ref=slot(a, block),
            send_sem=send_sems.at[a * N_REL + k], recv_sem=recv_sems.at[a * N_REL + k], device_id=to, device_id_type=MESH)

    mine = [pltpu.make_async_copy(ins[a], slot(a, me), local_sems.at[a]) for a in range(n)]
    first = []
    for a in range(n):
        first.append(copy(a, 0, me, sibling, src=ins[a]))
        first += [copy(a, 1 + j, me, (*chip, c), src=ins[a]) for j, chip in enumerate(chips)]
    passed = [copy(a, 4 + j, (*chip, c), sibling) for j, chip in enumerate(chips) for a in range(n)]

    def start():
        for cp in mine + first:
            cp.start()

    def pass_on():
        i = 0
        for j, chip in enumerate(chips):
            for a in range(n):
                copy(a, 1 + j, (*chip, c), me).wait_recv()
                passed[i].start()
                i += 1

    def finish():
        for a in range(n):
            copy(a, 0, sibling, me).wait_recv()
            for j, chip in enumerate(chips):
                copy(a, 4 + j, (*chip, 1 - c), me).wait_recv()
        for cp in first + passed:
            cp.wait_send()
        for cp in mine:
            cp.wait()

    return start, pass_on, finish


def _exchange_phases(ins, outs, sems):
    n = len(ins)
    send_sems, recv_sems, local_sems = sems
    x, y, c = lax.axis_index("x"), lax.axis_index("y"), lax.axis_index("c")
    me = 4 * x + 2 * y + c
    mine = [pltpu.make_async_copy(ins[a].at[me], outs[a].at[me], local_sems.at[a]) for a in range(n)]
    copies = []
    for a in range(n):
        for k in range(1, N_DEV):
            px = 1 - x if k & 4 else x
            py = 1 - y if k & 2 else y
            pc = 1 - c if k & 1 else c
            peer = 4 * px + 2 * py + pc
            send = pltpu.make_async_remote_copy(
                src_ref=ins[a].at[peer], dst_ref=outs[a].at[me], send_sem=send_sems.at[a * N_REL + k - 1],
                recv_sem=recv_sems.at[a * N_REL + k - 1], device_id=(px, py, pc), device_id_type=MESH)
            recv = pltpu.make_async_remote_copy(
                src_ref=ins[a].at[me], dst_ref=outs[a].at[peer], send_sem=send_sems.at[a * N_REL + k - 1],
                recv_sem=recv_sems.at[a * N_REL + k - 1], device_id=(px, py, pc), device_id_type=MESH)
            copies.append((send, recv))

    def start():
        for cp in mine:
            cp.start()
        for send, _ in copies:
            send.start()

    def finish():
        for send, recv in copies:
            send.wait_send()
            recv.wait_recv()
        for cp in mine:
            cp.wait()

    return start, None, finish


class _Comm:
    def __init__(self, kind, arrs):
        self.kind, self.arrs, self.n = kind, list(arrs), len(arrs)

    def out_shapes(self):
        if self.kind == "gather":
            return [jax.ShapeDtypeStruct((N_DEV,) + a.shape, a.dtype) for a in self.arrs]
        return [jax.ShapeDtypeStruct(a.shape, a.dtype) for a in self.arrs]

    def sems(self):
        return [pltpu.SemaphoreType.DMA((self.n * N_REL,)), pltpu.SemaphoreType.DMA((self.n * N_REL,)),
                pltpu.SemaphoreType.DMA((self.n,))]

    def run(self, ins, outs, sems, step, nsteps):
        start, mid, finish = (_gather_phases if self.kind == "gather" else _exchange_phases)(ins, outs, sems)
        if nsteps is None:
            start()
            if mid is not None:
                mid()
            finish()
            return
        pl.when(step == 0)(start)
        if mid is not None:
            pl.when(step == (3 * nsteps) // 4)(mid)
        pl.when(step == nsteps - 1)(finish)


def _split_refs(refs, n_in, n_out, comm):
    k = comm.n if comm is not None else 0
    i = 0
    parts = []
    for cnt in (n_in, k, n_out, k):
        parts.append(refs[i:i + cnt])
        i += cnt
    rest = refs[i:]
    if comm is None:
        return parts[0], (), parts[2], (), rest, ()
    return parts[0], parts[1], parts[2], parts[3], rest[:len(rest) - 3], rest[len(rest) - 3:]


def _comm_body(*refs, comm):
    n = comm.n
    comm.run(refs[:n], refs[n:2 * n], refs[2 * n:], None, None)


def _communicate(comm, name, vmem=False):
    spec = pl.BlockSpec(memory_space=pltpu.VMEM if vmem else pl.ANY)
    return pl.pallas_call(
        functools.partial(_comm_body, comm=comm), name=name,
        in_specs=[spec] * comm.n, out_specs=[spec] * comm.n, out_shape=comm.out_shapes(), scratch_shapes=comm.sems(),
        compiler_params=pltpu.CompilerParams(vmem_limit_bytes=VMEM_LIMIT),
    )(*comm.arrs)


ANY = pl.BlockSpec(memory_space=pl.ANY)


def _pad_heads(w, axis):
    shp = w.shape
    heads = shp[axis] // HEAD_DIM
    w = w.reshape(shp[:axis] + (heads, HEAD_DIM) + shp[axis + 1:])
    pad = [(0, 0)] * w.ndim
    pad[axis + 1] = (0, HEAD_PAD - HEAD_DIM)
    w = jnp.pad(w, pad)
    return w.reshape(shp[:axis] + (heads * HEAD_PAD,) + shp[axis + 1:])


def _unpad_heads(w, axis):
    shp = w.shape
    heads = shp[axis] // HEAD_PAD
    w = w.reshape(shp[:axis] + (heads, HEAD_PAD) + shp[axis + 1:])
    w = lax.slice_in_dim(w, 0, HEAD_DIM, axis=axis + 1)
    return w.reshape(shp[:axis] + (heads * HEAD_DIM,) + shp[axis + 1:])


def _w_in_to_padded(w):
    idx = np.cumsum((0,) + IN_SPLITS)
    z, xbc, dt, q, k, v, gates = [w[:, idx[i]:idx[i + 1]] for i in range(7)]
    zeros = lambda n: jnp.zeros((w.shape[0], n), w.dtype)
    return jnp.concatenate([_pad_heads(q, 1), _pad_heads(k, 1), _pad_heads(v, 1), xbc, dt, zeros(DTW - 2 * SSD_HEADS), z, gates],
                           axis=1)


def _w_in_from_padded(w):
    z = w[:, Z0:Z0 + D_INNER]
    gates = w[:, GT0:GT0 + 2 * D_MODEL]
    q = _unpad_heads(w[:, Q0:Q0 + QP], 1)
    xbc = w[:, XBC0:XBC0 + CONV_DIM]
    k = _unpad_heads(w[:, K0:K0 + KP], 1)
    v = _unpad_heads(w[:, V0:V0 + KP], 1)
    dt = w[:, DT0:DT0 + 2 * SSD_HEADS]
    return jnp.concatenate([z, xbc, dt, q, k, v, gates], axis=1)


def _pad_lane(v):
    v = v.reshape(-1)
    return jnp.pad(v, (0, LANE - v.shape[0]))


def _layer_consts(p):
    c = dict(p)
    c["conv_w8"] = jnp.pad(p["conv_w"].reshape(SSD_CONV, CONV_DIM), ((0, 8 - SSD_CONV), (0, 0)))
    c["conv_b1"] = p["conv_b"].reshape(1, CONV_DIM)
    c["dtb"] = _pad_lane(p["dt_bias"])
    c["alog"] = _pad_lane(p["a_log"])
    c["dskip_x"] = jnp.repeat(p["d_skip"], SSD_HEAD_DIM).reshape(1, D_INNER)
    c["sink"] = _pad_lane(p["attn_sink"]).reshape(1, LANE)
    return c


def _hosted(res, n_own, hook):
    res = list(res)
    if hook is not None:
        hook[1](res[n_own:])
    return res[0] if n_own == 1 else res[:n_own]


def _layer_fwd(x, h, c, w, bias, l, hooks, next_g):
    comm = lambda k: hooks[k][0] if k in hooks else None
    s = {"x": x}
    s["h"] = _rms_fwd(x, c["pre_mix_norm"], f"pre_mix_norm_{l}", out_dtype=BF16) if h is None else h
    res = _mm(s["h"], w["w_in"], f"in_proj_{l}", out_dtype=BF16, mode="dt_side", comm=comm("in_proj"))
    proj, s["dt"] = _hosted(res, 2, hooks.get("in_proj"))
    s["proj"] = proj
    s["xbc"] = _conv_fwd(proj, c["conv_w8"], c["conv_b1"])
    s["dtt"] = _dt_rows(s["dt"])
    yf, s["stf"] = _hosted(_ssd_fwd(s["xbc"], s["dt"], s["dtt"], c["dtb"], c["alog"], False, comm=comm("ssd_fwd")), 2,
                                hooks.get("ssd_fwd"))
    s["y"], s["stb"] = _hosted(_ssd_fwd(s["xbc"], s["dt"], s["dtt"], c["dtb"], c["alog"], True, comm=comm("ssd_fwd_rev"), add=yf), 2,
                                hooks.get("ssd_fwd_rev"))
    s["u"] = _gnorm_fwd(s["y"], s["xbc"], proj, c["dskip_x"], c["ssd_norm"])
    s["y_ssd"] = _mm(s["u"], w["w_ssd_out"], f"ssd_out_{l}")
    s["o"], s["lse"] = _hosted(_attn_fwd(proj, bias, c["sink"], comm=comm("attn_fwd")), 2, hooks.get("attn_fwd"))
    s["y_attn"] = _mm(s["o"], w["w_attn_out"], f"attn_out_{l}")
    s["mix"] = _gate_fwd(proj, c["b_gate"], s["y_ssd"], s["y_attn"])
    s["mixed"] = _mm(s["mix"], w["w_o"], f"w_o_{l}")
    s["x1"], s["h2"] = _rms_fwd(s["mixed"], c["post_mix_norm"], f"post_mix_norm_{l}", res=x, next_g=c["pre_mlp_norm"])
    s["f1"], s["a"] = _mm(s["h2"], w["w_mlp_in"], f"mlp_in_{l}", mode="relu2")
    s["f"] = _mm(s["a"], w["w_mlp_out"], f"mlp_out_{l}")
    if next_g is None:
        return _rms_fwd(s["f"], c["post_mlp_norm"], f"post_mlp_norm_{l}", res=s["x1"]), None, s
    x2, h_next = _rms_fwd(s["f"], c["post_mlp_norm"], f"post_mlp_norm_{l}", res=s["x1"], next_g=next_g)
    return x2, h_next, s


def _send_rest(gw):
    s_mi = gw["w_mlp_in"].reshape(D_MODEL, N_DEV, -1).transpose(1, 0, 2).astype(BF16)
    parts = [(_unpad_heads(gw[n], 0) if n == "w_attn_out" else gw[n]).reshape(N_DEV, -1, D_MODEL) for n in ROWS]
    return [s_mi, jnp.concatenate(parts, axis=1).astype(BF16)]


def _send_w_in(g):
    return [_w_in_from_padded(g).reshape(D_MODEL, N_DEV, -1).transpose(1, 0, 2).astype(BF16)]


def _layer_bwd(dx2, s, c, w, bias, l, earlier=None, host_own=False):
    gw, gs, got = {}, {}, {}
    df, gs["post_mlp_norm"] = _rms_bwd(s["f"], c["post_mlp_norm"], dx2, f"post_mlp_norm_bwd_{l}", out_dtype=BF16)
    df1 = _mm(df, w["w_mlp_out"], f"mlp_out_dx_{l}", tb=True, mode="relu2_bwd", extra=s["f1"], out_dtype=BF16)
    gw["w_mlp_out"] = _mm(s["a"], df, f"mlp_out_dw_{l}", ta=True, out_dtype=BF16)
    gw["w_mlp_in"] = _mm(s["h2"], df1, f"mlp_in_dw_{l}", ta=True, out_dtype=BF16)
    dh2 = _mm(df1, w["w_mlp_in"], f"mlp_in_dx_{l}", tb=True, out_dtype=BF16)
    dx1, gs["pre_mlp_norm"] = _rms_bwd(s["x1"], c["pre_mlp_norm"], dh2, f"pre_mlp_norm_bwd_{l}", add=dx2)
    dmixed, gs["post_mix_norm"] = _rms_bwd(s["mixed"], c["post_mix_norm"], dx1, f"post_mix_norm_bwd_{l}", out_dtype=BF16)
    gw["w_o"] = _mm(s["mix"], dmixed, f"w_o_dw_{l}", ta=True, out_dtype=BF16)
    dmix = _mm(dmixed, w["w_o"], f"w_o_dx_{l}", tb=True, out_dtype=BF16)
    dys, dya, dproj, gs["b_gate"] = _gate_bwd(dmix, s["proj"], c["b_gate"], s["y_ssd"], s["y_attn"])
    gw["w_ssd_out"] = _mm(s["u"], dys, f"ssd_out_dw_{l}", ta=True, out_dtype=BF16)
    du = _mm(dys, w["w_ssd_out"], f"ssd_out_dx_{l}", tb=True, out_dtype=BF16)
    gw["w_attn_out"] = _mm(s["o"], dya, f"attn_out_dw_{l}", ta=True, out_dtype=BF16)
    do = _mm(dya, w["w_attn_out"], f"attn_out_dx_{l}", tb=True, out_dtype=BF16)
    res = _attn_bwd(s["proj"], bias, c["sink"], do, s["o"], s["lse"], dproj,
                    comm=None if earlier is None else _Comm("exchange", earlier[:1]))
    (dproj, dbias, dsink), got["earlier"] = res[:3], list(res[3:])
    gs["attn_sink"] = dsink
    dy, dproj, gs["ssd_norm"], gs["d_skip"] = _gnorm_bwd(du, s["y"], s["xbc"], s["proj"], c["dskip_x"], c["ssd_norm"],
                                                         dproj)
    res = _ssd_bwd(s["xbc"], s["dt"], s["dtt"], c["dtb"], c["alog"], dy, s["stf"], c["dskip_x"], False,
                   comm=None if earlier is None else _Comm("exchange", earlier[1:]))
    part, got["earlier"] = res[:3], got["earlier"] + list(res[3:])
    res = _ssd_bwd(s["xbc"], s["dt"], s["dtt"], c["dtb"], c["alog"], dy, s["stb"], c["dskip_x"], True, prev=part, dproj=dproj,
                   comm=_Comm("exchange", _send_rest(gw)) if host_own else None)
    (dxbc, dproj, sm), got["rest"] = res[:3], res[3:]
    gs["a_log"], gs["dt_bias"] = sm[0:1], sm[1:2]
    dproj, gs["conv"] = _conv_bwd(s["proj"], dxbc, c["conv_w8"], c["conv_b1"], dproj)
    gw["w_in"] = _mm(s["h"], dproj, f"in_proj_dw_{l}", ta=True, out_dtype=BF16)
    if host_own:
        dh, *got["w_in"] = _mm(dproj, w["w_in"], f"in_proj_dx_{l}", tb=True, out_dtype=BF16, comm=_Comm("exchange", _send_w_in(gw["w_in"])))
    else:
        dh = _mm(dproj, w["w_in"], f"in_proj_dx_{l}", tb=True, out_dtype=BF16)
    dx, gs["pre_mix_norm"] = _rms_bwd(s["x"], c["pre_mix_norm"], dh, f"pre_mix_norm_bwd_{l}", add=dx1)
    return dx, gw, gs, dbias, got


SMALL = (("pre_mix_norm", DEPTH * D_MODEL), ("b_gate", DEPTH * 2 * D_MODEL), ("conv_w", DEPTH * SSD_CONV * CONV_DIM),
         ("conv_b", DEPTH * CONV_DIM), ("dt_bias", DEPTH * 2 * SSD_HEADS), ("a_log", DEPTH * 2 * SSD_HEADS),
         ("d_skip", DEPTH * SSD_HEADS), ("ssd_norm", DEPTH * D_INNER), ("attn_sink", DEPTH * Q_HEADS),
         ("rel_bias_table", N_BUCKETS * Q_HEADS), ("post_mix_norm", DEPTH * D_MODEL), ("pre_mlp_norm", DEPTH * D_MODEL),
         ("post_mlp_norm", DEPTH * D_MODEL))


def _small_rows(n):
    return -(-n // LANE)


def _pack_small(vals, fill=0.0):
    rows = []
    for name, n in SMALL:
        v = vals[name].reshape(-1).astype(F32)
        rows.append(jnp.pad(v, (0, _small_rows(n) * LANE - n), constant_values=fill).reshape(-1, LANE))
    out = jnp.concatenate(rows, axis=0)
    return jnp.pad(out, ((0, -out.shape[0] % 8), (0, 0)), constant_values=fill)


def _unpack_small(packed, shapes):
    out, r = {}, 0
    for name, n in SMALL:
        nr = _small_rows(n)
        out[name] = packed[r:r + nr].reshape(-1)[:n].reshape(shapes[name])
        r += nr
    return out


BIG = ("w_in", "w_ssd_out", "w_attn_out", "w_o", "w_mlp_in", "w_mlp_out")
ROWS = ("w_ssd_out", "w_attn_out", "w_o", "w_mlp_out")


def _pack_rows(vals, lead):
    return jnp.concatenate([vals[n].reshape(lead + (-1, D_MODEL)) for n in ROWS], axis=len(lead))


def kernel(x, pre_mix_norm, w_in, b_gate, conv_w, conv_b, dt_bias, a_log, d_skip, ssd_norm, w_ssd_out, attn_sink, rel_bias_table, w_attn_out, w_o, post_mix_norm, pre_mlp_norm, w_mlp_in, w_mlp_out, post_mlp_norm, loss_target, m_pre_mix_norm, m_w_in, m_b_gate, m_conv_w, m_conv_b, m_dt_bias, m_a_log, m_d_skip, m_ssd_norm, m_w_ssd_out, m_attn_sink, m_rel_bias_table, m_w_attn_out, m_w_o, m_post_mix_norm, m_pre_mlp_norm, m_w_mlp_in, m_w_mlp_out, m_post_mlp_norm, v_pre_mix_norm, v_w_in, v_b_gate, v_conv_w, v_conv_b, v_dt_bias, v_a_log, v_d_skip, v_ssd_norm, v_w_ssd_out, v_attn_sink, v_rel_bias_table, v_w_attn_out, v_w_o, v_post_mix_norm, v_pre_mlp_norm, v_w_mlp_in, v_w_mlp_out, v_post_mlp_norm):
    names = ("pre_mix_norm", "w_in", "b_gate", "conv_w", "conv_b", "dt_bias", "a_log", "d_skip", "ssd_norm", "w_ssd_out",
             "attn_sink", "rel_bias_table", "w_attn_out", "w_o", "post_mix_norm", "pre_mlp_norm", "w_mlp_in", "w_mlp_out",
             "post_mlp_norm")
    W = dict(zip(names, (pre_mix_norm, w_in, b_gate, conv_w, conv_b, dt_bias, a_log, d_skip, ssd_norm, w_ssd_out, attn_sink,
                         rel_bias_table, w_attn_out, w_o, post_mix_norm, pre_mlp_norm, w_mlp_in, w_mlp_out, post_mlp_norm)))
    M = dict(zip(names, (m_pre_mix_norm, m_w_in, m_b_gate, m_conv_w, m_conv_b, m_dt_bias, m_a_log, m_d_skip, m_ssd_norm,
                         m_w_ssd_out, m_attn_sink, m_rel_bias_table, m_w_attn_out, m_w_o, m_post_mix_norm, m_pre_mlp_norm,
                         m_w_mlp_in, m_w_mlp_out, m_post_mlp_norm)))
    V = dict(zip(names, (v_pre_mix_norm, v_w_in, v_b_gate, v_conv_w, v_conv_b, v_dt_bias, v_a_log, v_d_skip, v_ssd_norm,
                         v_w_ssd_out, v_attn_sink, v_rel_bias_table, v_w_attn_out, v_w_o, v_post_mix_norm, v_pre_mlp_norm,
                         v_w_mlp_in, v_w_mlp_out, v_post_mlp_norm)))
    t = x.shape[1]
    shard = {n: W[n].shape for n in names}
    dev = 4 * lax.axis_index("x") + 2 * lax.axis_index("y") + lax.axis_index("c")
    cshard = CONV_DIM // N_DEV

    sh_in = [w_in[l].astype(BF16) for l in range(DEPTH)]
    sh_rest = [[w_mlp_in[l].astype(BF16), _pack_rows({n: W[n][l].astype(BF16) for n in ROWS}, ())] for l in range(DEPTH)]
    full = [{} for _ in range(DEPTH)]

    def take_w_in(l):
        def take(got):
            full[l]["w_in"] = _w_in_to_padded(got[0].transpose(1, 0, 2).reshape(D_MODEL, N_IN))
        return take

    def take_mlp_in(l):
        def take(got):
            full[l]["w_mlp_in"] = got[0].transpose(1, 0, 2).reshape(D_MODEL, D_FF)
        return take

    def take_rows(l):
        def take(got):
            r = 0
            for n in ROWS:
                per = shard[n][1]
                full[l][n] = got[0][:, r:r + per].reshape(N_DEV * per, D_MODEL)
                r += per
            full[l]["w_attn_out"] = _pad_heads(full[l]["w_attn_out"], 0)
        return take

    def take_rest(l):
        def take(got):
            take_mlp_in(l)(got[:1])
            take_rows(l)(got[1:])
        return take

    normed0, g_in0, g_conv = _rms_fwd(x[0], pre_mix_norm[0], "pre_mix_norm_0", out_dtype=BF16,
                                      comm=_Comm("gather", [sh_in[0], conv_w.reshape(-1, LANE)]))
    take_w_in(0)([g_in0])
    conv_full = g_conv.reshape(N_DEV, DEPTH, SSD_CONV, cshard).transpose(1, 2, 0, 3).reshape(DEPTH, SSD_CONV, CONV_DIM)
    hooks = [{"in_proj": (_Comm("gather", sh_rest[0]), take_rest(0)),
              "ssd_fwd": (_Comm("gather", sh_rest[1][1:]), take_rows(1)),
              "ssd_fwd_rev": (_Comm("gather", sh_rest[1][:1]), take_mlp_in(1)),
              "attn_fwd": (_Comm("gather", [sh_in[1]]), take_w_in(1))}, {}]

    _, band = _bucket_np()
    onehot = jnp.asarray(_onehot_np(), BF16)
    table_t = jnp.pad(rel_bias_table.T, ((0, 0), (0, LANE - N_BUCKETS)))
    bias = _mm(table_t, onehot, "t5_bias", tb=True, tm=Q_HEADS, tn=3 * BLOCK * BLOCK // 8, split_a=True)
    bias = (bias.reshape(Q_HEADS, BLOCK, 3 * BLOCK) + jnp.asarray(np.where(band, 0.0, NEG), F32)).reshape(
        KV_HEADS, REP * BLOCK, 3 * BLOCK)

    consts = []
    for l in range(DEPTH):
        p = {n: W[n][l] for n in names if n not in BIG and n not in ("rel_bias_table", "conv_w")}
        p["conv_w"] = conv_full[l]
        consts.append(_layer_consts(p))
    act, normed = x[0], normed0
    saved = []
    for l in range(DEPTH):
        next_g = consts[l + 1]["pre_mix_norm"] if l + 1 < DEPTH else None
        act, normed, s = _layer_fwd(act, normed, consts[l], full[l], bias, l, hooks[l], next_g)
        saved.append(s)
    dy, lsum = _loss(act, loss_target[0])
    loss = lax.psum(0.5 / D_MODEL * jnp.sum(lsum), ("x", "y", "c"))

    gss, dbs = [None] * DEPTH, [None] * DEPTH
    dy, gw1, gss[1], dbs[1], _ = _layer_bwd(dy, saved[1], consts[1], full[1], bias, 1)
    dy, _, gss[0], dbs[0], got = _layer_bwd(dy, saved[0], consts[0], full[0], bias, 0,
                                            earlier=_send_w_in(gw1["w_in"]) + _send_rest(gw1), host_own=True)
    recv = [list(got["w_in"]) + list(got["rest"]), list(got["earlier"])]
    grad_x = dy[None]
    dbias = jnp.concatenate([d.reshape(Q_HEADS, -1) for d in dbs], axis=1)
    d_table = _mm(dbias, jnp.concatenate([onehot] * DEPTH, axis=0), "t5_bias_bwd", tm=Q_HEADS, tk=3 * BLOCK * BLOCK // 8,
                  split_a=True)

    sg = {}
    for n in ("pre_mix_norm", "b_gate", "ssd_norm", "post_mix_norm", "pre_mlp_norm", "post_mlp_norm"):
        sg[n] = jnp.stack([gss[l][n].reshape(-1) for l in range(DEPTH)])
    sg["conv_w"] = jnp.stack([gss[l]["conv"][0:SSD_CONV] for l in range(DEPTH)])
    sg["conv_b"] = jnp.stack([gss[l]["conv"][SSD_CONV] for l in range(DEPTH)])
    sg["dt_bias"] = jnp.stack([gss[l]["dt_bias"][0, 0:2 * SSD_HEADS] for l in range(DEPTH)])
    sg["a_log"] = jnp.stack([gss[l]["a_log"][0, 0:2 * SSD_HEADS] for l in range(DEPTH)])
    sg["d_skip"] = jnp.stack([gss[l]["d_skip"][0, 0:SSD_HEADS] for l in range(DEPTH)])
    sg["attn_sink"] = jnp.stack([gss[l]["attn_sink"][0, 0:Q_HEADS] for l in range(DEPTH)])
    sg["rel_bias_table"] = d_table[:, 0:N_BUCKETS].T
    (small_parts,) = _communicate(_Comm("gather", [_pack_small(sg)]), "gather_small_grads", vmem=True)
    sshape = {n: W[n].shape for n, _ in SMALL}
    sshape["conv_w"] = (DEPTH, SSD_CONV, 1, CONV_DIM)
    pk = lambda d, fill: _pack_small({n: (jnp.full((DEPTH, SSD_CONV, CONV_DIM), fill, F32) if n == "conv_w" else d[n])
                                      for n, _ in SMALL}, fill)
    s_out = [_unpack_small(o, sshape) for o in _adamw(small_parts, pk(W, 1.0), pk(M, 1.0), pk(V, 1.0), "adamw_small", tr=1024)]
    conv_g = lax.dynamic_slice_in_dim(s_out[0]["conv_w"], dev * cshard, cshard, axis=3)
    c_out = _adamw(conv_g.reshape(1, DEPTH * SSD_CONV, cshard), conv_w.reshape(-1, cshard), m_conv_w.reshape(-1, cshard),
                   v_conv_w.reshape(-1, cshard), "adamw_conv_w", tr=DEPTH * SSD_CONV)
    for i in range(4):
        s_out[i]["conv_w"] = c_out[i].reshape(conv_w.shape)

    b_out = {n: None for n in BIG}
    for l in (1, 0):
        r_in, r_mi, r_rows = recv[l]
        b_out["w_in"] = _adamw_layer(r_in, 0, w_in, m_w_in, v_w_in, l, b_out["w_in"], f"adamw_w_in_{l}", tr=256)
        b_out["w_mlp_in"] = _adamw_layer(r_mi, 0, w_mlp_in, m_w_mlp_in, v_w_mlp_in, l, b_out["w_mlp_in"], f"adamw_w_mlp_in_{l}",
                                         tr=256)
        r = 0
        for n in ROWS:
            b_out[n] = _adamw_layer(r_rows, r, W[n], M[n], V[n], l, b_out[n], f"adamw_{n}_{l}")
            r += shard[n][1]

    outs = [loss, grad_x]
    for i in range(4):
        for n in names:
            outs.append(b_out[n][i] if n in BIG else s_out[i][n])
    return tuple(outs)
```

```python
import functools
import math

import numpy as np
import jax
import jax.numpy as jnp
from jax import lax
from jax.experimental import pallas as pl
from jax.experimental.pallas import tpu as pltpu

F32 = jnp.float32
BF16 = jnp.bfloat16

D_MODEL = 1024
DEPTH = 2
D_INNER = 2048
SSD_HEADS = 32
SSD_HEAD_DIM = 64
SSD_GROUPS = 8
SSD_REP = 4
SSD_STATE = 128
SSD_CONV = 5
CHUNK = 128
CONV_DIM = 4096
Q_HEADS = 16
KV_HEADS = 4
HEAD_DIM = 64
ATTN_WIDTH = 1024
KV_WIDTH = 256
WINDOW = 128
BLOCK = 128
N_BUCKETS = 32
MAX_DISTANCE = 128
D_FF = 4096
EPS = 1e-6
IN_SPLITS = (D_INNER, CONV_DIM, 2 * SSD_HEADS, ATTN_WIDTH, KV_WIDTH, KV_WIDTH, 2 * D_MODEL)
N_IN = sum(IN_SPLITS)
N_DEV = 8

ADAM_LR = 0.001
ADAM_B1 = 0.9
ADAM_B2 = 0.999
ADAM_EPS = 1e-08
ADAM_WD = 0.01
ADAM_STEP = 10

LANE = 128
HEAD_PAD = LANE
QP = Q_HEADS * HEAD_PAD
KP = KV_HEADS * HEAD_PAD
QKV = QP + 2 * KP
Q0 = 0
K0 = QP
V0 = QP + KP
XBC0 = QKV
DT0 = XBC0 + CONV_DIM
DTW = 1024
Z0 = DT0 + DTW
GT0 = Z0 + D_INNER
NP = GT0 + 2 * D_MODEL
NEG = -1e30
VMEM_LIMIT = 56 * 1024 * 1024


def _params(n_grid):
    return pltpu.CompilerParams(dimension_semantics=("arbitrary",) * n_grid, vmem_limit_bytes=VMEM_LIMIT)


def _dot(a, b, dims=((1,), (0,))):
    return lax.dot_general(a, b, (dims, ((), ())), preferred_element_type=F32)


def _mx(a):
    return a.astype(BF16)


def _split_dot(x, e):
    hi = x.astype(BF16)
    lo = (x - hi.astype(F32)).astype(BF16)
    return _dot(hi, e) + _dot(lo, e)


def _softplus(x):
    u = jnp.exp(-jnp.abs(x))
    w = 1.0 + u
    log1p = jnp.where(w == 1.0, u, jnp.log(w) * u / jnp.where(w == 1.0, 1.0, w - 1.0))
    return jnp.maximum(x, 0.0) + log1p


def _sigmoid(x):
    return 1.0 / (1.0 + jnp.exp(-x))


def _mm_body(*refs, grid, ta, tb, split_a, mode, comm):
    n_in = 3 if mode == "relu2_bwd" else 2
    n_out = 2 if mode in ("relu2", "dt_side") else 1
    ins, cin, outs, cout, (acc_ref,), csem = _split_refs(refs, n_in, n_out, comm)
    a_ref, b_ref = ins[0], ins[1]
    i, j, k = pl.program_id(0), pl.program_id(1), pl.program_id(2)
    nk = grid[2]
    if comm is not None:
        comm.run(cin, cout, csem, (i * grid[1] + j) * nk + k, grid[0] * grid[1] * nk)

    dims = ((0 if ta else 1,), (1 if tb else 0,))
    b = _mx(b_ref[...])
    if split_a:
        a32 = a_ref[...].astype(F32)
        hi = a32.astype(BF16)
        lo = (a32 - hi.astype(F32)).astype(BF16)
        part = _dot(hi, b, dims) + _dot(lo, b, dims)
    else:
        part = _dot(_mx(a_ref[...]), b, dims)

    if nk > 1:
        @pl.when(k == 0)
        def _():
            acc_ref[...] = jnp.zeros_like(acc_ref)

        acc_ref[...] += part

    @pl.when(k == nk - 1)
    def _():
        acc = part if nk == 1 else acc_ref[...]
        if mode == "relu2":
            outs[0][...] = acc.astype(outs[0].dtype)
            r = jnp.maximum(acc, 0.0)
            outs[1][...] = (r * r).astype(outs[1].dtype)
        elif mode == "relu2_bwd":
            outs[0][...] = (acc * 2.0 * jnp.maximum(ins[2][...].astype(F32), 0.0)).astype(outs[0].dtype)
        else:
            outs[0][...] = acc.astype(outs[0].dtype)
            if mode == "dt_side":
                @pl.when(j == DT0 // outs[0].shape[1])
                def _():
                    off = DT0 % outs[0].shape[1]
                    outs[1][...] = acc[:, off:off + LANE]


def _tile(n, pref):
    t = min(n, pref)
    while n % t:
        t -= LANE
    assert t > 0 and n % t == 0, (n, pref)
    return t


MM_VMEM_BUDGET = 42 * 1024 * 1024


def _mm_tiles(m, n, k, a_bytes, b_bytes, out_bytes, extra_bytes):
    tm = _tile(m, 1024)
    best = None
    for tn in (2048, 1024):
        if n % tn:
            continue
        for nk in range(1, k // LANE + 1):
            if k % nk or (k // nk) % LANE:
                continue
            tk = k // nk
            need = 2 * (tm * tk * a_bytes + tk * tn * b_bytes + tm * tn * (out_bytes + extra_bytes)) + (tm * tn * 4 if nk > 1 else 0)
            if need <= MM_VMEM_BUDGET:
                if best is None or nk < best[0]:
                    best = (nk, tn, tk)
                break
    assert best is not None, (m, n, k)
    nk, tn, tk = best
    if nk == 1 and m % (2 * tm) == 0 and 4 * (tm * tk * a_bytes) + 2 * tk * tn * b_bytes + 4 * tm * tn * (out_bytes + extra_bytes) <= MM_VMEM_BUDGET:
        tm = 2 * tm
    return tm, tn, tk


def _mm(a, b, name, ta=False, tb=False, out_dtype=F32, tm=None, tn=None, tk=None, split_a=False, mode=None, extra=None,
        comm=None):
    m, k = (a.shape[1], a.shape[0]) if ta else a.shape
    n = b.shape[0] if tb else b.shape[1]
    assert (b.shape[1] if tb else b.shape[0]) == k
    if tm is None:
        side = {"relu2": 2, "relu2_bwd": extra.dtype.itemsize if extra is not None else 0}.get(mode, 0)
        tm, tn, tk = _mm_tiles(m, n, k, a.dtype.itemsize, b.dtype.itemsize, jnp.dtype(out_dtype).itemsize, side)
    else:
        tm, tn, tk = _tile(m, tm), _tile(n, tn or 2048), _tile(k, tk or 1024)
    grid = (m // tm, n // tn, k // tk)
    a_spec = pl.BlockSpec((tk, tm), lambda i, j, kk: (kk, i)) if ta else pl.BlockSpec((tm, tk), lambda i, j, kk: (i, kk))
    b_spec = pl.BlockSpec((tn, tk), lambda i, j, kk: (j, kk)) if tb else pl.BlockSpec((tk, tn), lambda i, j, kk: (kk, j))
    o_spec = pl.BlockSpec((tm, tn), lambda i, j, kk: (i, j))
    in_specs, args = [a_spec, b_spec], [a, b]
    out_specs, out_shape = [o_spec], [jax.ShapeDtypeStruct((m, n), out_dtype)]
    if mode == "relu2":
        out_specs, out_shape = [o_spec, o_spec], [jax.ShapeDtypeStruct((m, n), BF16), jax.ShapeDtypeStruct((m, n), BF16)]
    elif mode == "dt_side":
        assert DT0 % tn + LANE <= tn
        out_specs = [o_spec, pl.BlockSpec((tm, LANE), lambda i, j, kk: (i, 0))]
        out_shape = out_shape + [jax.ShapeDtypeStruct((m, LANE), F32)]
    elif mode == "relu2_bwd":
        in_specs, args = in_specs + [o_spec], args + [extra]
    scratch = [pltpu.VMEM((tm, tn) if grid[2] > 1 else (8, LANE), F32)]
    if comm is not None:
        in_specs, args = in_specs + [ANY] * comm.n, args + comm.arrs
        out_specs, out_shape = out_specs + [ANY] * comm.n, out_shape + comm.out_shapes()
        scratch = scratch + comm.sems()
    res = pl.pallas_call(
        functools.partial(_mm_body, grid=grid, ta=ta, tb=tb, split_a=split_a, mode=mode, comm=comm),
        name=name, grid=grid, in_specs=in_specs, out_specs=out_specs, out_shape=out_shape, scratch_shapes=scratch,
        compiler_params=_params(3),
    )(*args)
    return res[0] if len(res) == 1 else res


def _rms_fwd_body(*refs, has_res, has_next, nsteps, comm):
    ins, cin, outs, cout, _, csem = _split_refs(refs, 2 + has_res + has_next, 1 + has_next, comm)
    if comm is not None:
        comm.run(cin, cout, csem, pl.program_id(0), nsteps)
    x_ref, g_ref = ins[:2]
    x = x_ref[...]
    y = x * lax.rsqrt(jnp.mean(x * x, axis=-1, keepdims=True) + EPS) * g_ref[...]
    if has_res:
        y = ins[2][...] + y
    if has_next:
        g2_ref = ins[3 if has_res else 2]
        outs[1][...] = (y * lax.rsqrt(jnp.mean(y * y, axis=-1, keepdims=True) + EPS) * g2_ref[...]).astype(outs[1].dtype)
    outs[0][...] = y.astype(outs[0].dtype)


def _rms_fwd(x, g, name, res=None, out_dtype=F32, next_g=None, tr=512, comm=None):
    t, d = x.shape
    tr = min(tr, t)
    row = pl.BlockSpec((tr, d), lambda i: (i, 0))
    vec = pl.BlockSpec((1, d), lambda i: (0, 0))
    args = [x, g.reshape(1, d)] + ([res] if res is not None else []) + ([next_g.reshape(1, d)] if next_g is not None else [])
    in_specs = [row, vec] + ([row] if res is not None else []) + ([vec] if next_g is not None else [])
    out_specs, out_shape = [row], [jax.ShapeDtypeStruct((t, d), out_dtype)]
    if next_g is not None:
        out_specs, out_shape = [row, row], out_shape + [jax.ShapeDtypeStruct((t, d), BF16)]
    scratch = []
    if comm is not None:
        in_specs, args = in_specs + [ANY] * comm.n, args + comm.arrs
        out_specs, out_shape = out_specs + [ANY] * comm.n, out_shape + comm.out_shapes()
        scratch = comm.sems()
    res_ = pl.pallas_call(
        functools.partial(_rms_fwd_body, has_res=res is not None, has_next=next_g is not None, nsteps=t // tr, comm=comm),
        name=name,
        grid=(t // tr,),
        in_specs=in_specs,
        out_specs=out_specs,
        out_shape=out_shape,
        scratch_shapes=scratch,
        compiler_params=_params(1),
    )(*args)
    return res_ if (next_g is not None or comm is not None) else res_[0]


def _rms_bwd_body(*refs, has_add):
    if has_add:
        x_ref, g_ref, dy_ref, add_ref, dx_ref, dg_ref = refs
    else:
        x_ref, g_ref, dy_ref, dx_ref, dg_ref = refs

    @pl.when(pl.program_id(0) == 0)
    def _():
        dg_ref[...] = jnp.zeros_like(dg_ref)

    x = x_ref[...]
    dy = dy_ref[...].astype(F32)
    rstd = lax.rsqrt(jnp.mean(x * x, axis=-1, keepdims=True) + EPS)
    n = x * rstd
    dn = dy * g_ref[...]
    dx = rstd * (dn - n * jnp.mean(dn * n, axis=-1, keepdims=True))
    if has_add:
        dx = dx + add_ref[...]
    dx_ref[...] = dx.astype(dx_ref.dtype)
    dg_ref[...] += jnp.sum(dy * n, axis=0, keepdims=True)


def _rms_bwd(x, g, dy, name, add=None, out_dtype=F32, tr=512):
    t, d = x.shape
    tr = min(tr, t)
    row = pl.BlockSpec((tr, d), lambda i: (i, 0))
    vec = pl.BlockSpec((1, d), lambda i: (0, 0))
    args = [x, g.reshape(1, d), dy] + ([add] if add is not None else [])
    return pl.pallas_call(
        functools.partial(_rms_bwd_body, has_add=add is not None),
        name=name,
        grid=(t // tr,),
        in_specs=[row, vec, row] + ([row] if add is not None else []),
        out_specs=[row, vec],
        out_shape=[jax.ShapeDtypeStruct((t, d), out_dtype), jax.ShapeDtypeStruct((1, d), F32)],
        compiler_params=_params(1),
    )(*args)


def _loss_body(y_ref, t_ref, dy_ref, l_ref):
    @pl.when(pl.program_id(0) == 0)
    def _():
        l_ref[...] = jnp.zeros_like(l_ref)

    e = y_ref[...] - t_ref[...]
    dy_ref[...] = e * (1.0 / D_MODEL)
    s = jnp.sum(e * e, axis=0, keepdims=True)
    acc = s[:, 0:LANE]
    for j in range(1, D_MODEL // LANE):
        acc = acc + s[:, j * LANE:(j + 1) * LANE]
    l_ref[...] += acc


def _loss(y, target, tr=512):
    t, d = y.shape
    tr = min(tr, t)
    row = pl.BlockSpec((tr, d), lambda i: (i, 0))
    return pl.pallas_call(
        functools.partial(_loss_body), name="loss_head", grid=(t // tr,), in_specs=[row, row],
        out_specs=[row, pl.BlockSpec((1, LANE), lambda i: (0, 0))],
        out_shape=[jax.ShapeDtypeStruct((t, d), F32), jax.ShapeDtypeStruct((1, LANE), F32)],
        compiler_params=_params(1),
    )(y, target)


CONV_CT = 128
CONV_RB = 128
CONV_PAD = 8


def _conv_rows(t):
    return min(CONV_RB, t)


def _conv_fwd_body(u_ref, w_ref, b_ref, o_ref, pad_ref, *, t):
    rb = _conv_rows(t)
    zeros = jnp.zeros((CONV_PAD, CONV_CT), F32)
    pad_ref[pl.ds(0, CONV_PAD), :] = zeros
    pad_ref[pl.ds(t + CONV_PAD, CONV_PAD), :] = zeros
    pad_ref[pl.ds(CONV_PAD, t), :] = u_ref[...].astype(F32)
    w = w_ref[...]
    b = b_ref[...]
    for c in range(t // rb):
        base = CONV_PAD + c * rb
        acc = b + w[0:1, :] * pad_ref[pl.ds(base - 2, rb), :]
        for k in range(1, SSD_CONV):
            acc = acc + w[k:k + 1, :] * pad_ref[pl.ds(base + k - 2, rb), :]
        o_ref[pl.ds(c * rb, rb), :] = acc * _sigmoid(acc)


def _conv_fwd(proj, conv_w8, conv_b):
    t = proj.shape[0]
    off = XBC0 // CONV_CT
    return pl.pallas_call(
        functools.partial(_conv_fwd_body, t=t), name="conv_fwd", grid=(CONV_DIM // CONV_CT,),
        in_specs=[pl.BlockSpec((t, CONV_CT), lambda i: (0, off + i)), pl.BlockSpec((8, CONV_CT), lambda i: (0, i)),
                  pl.BlockSpec((1, CONV_CT), lambda i: (0, i))],
        out_specs=pl.BlockSpec((t, CONV_CT), lambda i: (0, i)),
        out_shape=jax.ShapeDtypeStruct((t, CONV_DIM), F32),
        scratch_shapes=[pltpu.VMEM((t + 2 * CONV_PAD, CONV_CT), F32)],
        compiler_params=_params(1),
    )(proj, conv_w8, conv_b)


def _conv_bwd_body(u_ref, da_ref, w_ref, b_ref, _, du_ref, dw_ref, pad_ref, pad2_ref, *, t):
    rb = _conv_rows(t)
    zeros = jnp.zeros((CONV_PAD, CONV_CT), F32)
    for p in (pad_ref, pad2_ref):
        p[pl.ds(0, CONV_PAD), :] = zeros
        p[pl.ds(t + CONV_PAD, CONV_PAD), :] = zeros
    pad_ref[pl.ds(CONV_PAD, t), :] = u_ref[...].astype(F32)
    w = w_ref[...]
    b = b_ref[...]
    dw = [jnp.zeros((1, CONV_CT), F32) for _ in range(SSD_CONV + 1)]
    for c in range(t // rb):
        base = CONV_PAD + c * rb
        us = [pad_ref[pl.ds(base + k - 2, rb), :] for k in range(SSD_CONV)]
        acc = b + w[0:1, :] * us[0]
        for k in range(1, SSD_CONV):
            acc = acc + w[k:k + 1, :] * us[k]
        sg = _sigmoid(acc)
        dyc = da_ref[pl.ds(c * rb, rb), :] * (sg * (1.0 + acc * (1.0 - sg)))
        pad2_ref[pl.ds(base, rb), :] = dyc
        for k in range(SSD_CONV):
            dw[k] = dw[k] + jnp.sum(dyc * us[k], axis=0, keepdims=True)
        dw[SSD_CONV] = dw[SSD_CONV] + jnp.sum(dyc, axis=0, keepdims=True)
    for c in range(t // rb):
        base = CONV_PAD + c * rb
        acc = w[0:1, :] * pad2_ref[pl.ds(base + 2, rb), :]
        for k in range(1, SSD_CONV):
            acc = acc + w[k:k + 1, :] * pad2_ref[pl.ds(base + 2 - k, rb), :]
        du_ref[pl.ds(c * rb, rb), :] = acc.astype(du_ref.dtype)
    dw_ref[...] = jnp.concatenate(dw + [jnp.zeros((2, CONV_CT), F32)], axis=0)


def _conv_bwd(proj, dact, conv_w8, conv_b, dproj):
    t = proj.shape[0]
    off = XBC0 // CONV_CT
    col = pl.BlockSpec((t, CONV_CT), lambda i: (0, i))
    seg = pl.BlockSpec((t, CONV_CT), lambda i: (0, off + i))
    return pl.pallas_call(
        functools.partial(_conv_bwd_body, t=t), name="conv_bwd", grid=(CONV_DIM // CONV_CT,),
        in_specs=[seg, col, pl.BlockSpec((8, CONV_CT), lambda i: (0, i)), pl.BlockSpec((1, CONV_CT), lambda i: (0, i)), ANY],
        out_specs=[seg, pl.BlockSpec((8, CONV_CT), lambda i: (0, i))],
        out_shape=[jax.ShapeDtypeStruct((t, NP), BF16), jax.ShapeDtypeStruct((8, CONV_DIM), F32)],
        scratch_shapes=[pltpu.VMEM((t + 2 * CONV_PAD, CONV_CT), F32), pltpu.VMEM((t + 2 * CONV_PAD, CONV_CT), F32)],
        input_output_aliases={4: 0}, compiler_params=_params(1),
    )(proj, dact, conv_w8, conv_b, dproj)


GW = SSD_REP * SSD_HEAD_DIM


def _expand_np(hoff):
    e = np.zeros((LANE, D_INNER), np.float32)
    for h in range(SSD_HEADS):
        e[hoff + h, h * SSD_HEAD_DIM:(h + 1) * SSD_HEAD_DIM] = 1.0
    return e


def _head_masks():
    lane = lax.broadcasted_iota(jnp.int32, (1, GW), 1)
    return [((lane >= r * SSD_HEAD_DIM) & (lane < (r + 1) * SSD_HEAD_DIM)).astype(F32) for r in range(SSD_REP)]


def _ssd_common(dtc_ref, dtr_ref, bc_ref, br_ref, alc_ref, alr_ref, e_ref, rev):
    L = CHUNK
    ri = lax.broadcasted_iota(jnp.int32, (L, L), 0)
    ci = lax.broadcasted_iota(jnp.int32, (L, L), 1)
    tri = (ri <= ci) if rev else (ri >= ci)
    trit = (ri >= ci) if rev else (ri <= ci)
    raw_c = dtc_ref[...] + bc_ref[...]
    dt_c = _softplus(raw_c)
    a_c = -jnp.exp(alc_ref[...])
    cs_c = jnp.dot(tri.astype(F32), dt_c * a_c, precision=lax.Precision.HIGHEST, preferred_element_type=F32)
    dt_r = _softplus(dtr_ref[...] + br_ref[...])
    a_r = -jnp.exp(alr_ref[...])
    cs_r = jnp.dot(dt_r * a_r, trit.astype(F32), precision=lax.Precision.HIGHEST, preferred_element_type=F32)
    il = 0 if rev else L - 1
    e_c = jnp.exp(cs_c)
    w_c = jnp.exp(cs_c[il:il + 1, :] - cs_c)
    ex = _dot(_mx(jnp.concatenate([dt_c, e_c, w_c], axis=0)), e_ref[...])
    return dict(tri=tri, trit=trit, raw_c=raw_c, dt_c=dt_c, a_c=a_c, cs_c=cs_c, cs_r=cs_r, il=il, e_last=e_c[il:il + 1],
                dt_x=ex[0:L], e_x=ex[L:2 * L], w_x=ex[2 * L:3 * L], cd_x=ex[L + il:L + il + 1])


def _ssd_fwd_body(*refs, rev, hoff, nsteps, comm, has_add):
    L = CHUNK
    ins, cin, (y_ref, st_ref), cout, (s_scr,), csem = _split_refs(refs, 11 if has_add else 10, 2, comm)
    xs_ref, b_ref, c_ref, dtc_ref, dtr_ref, bc_ref, br_ref, alc_ref, alr_ref, e_ref = ins[:10]
    if comm is not None:
        comm.run(cin, cout, csem, pl.program_id(0), nsteps)

    @pl.when(pl.program_id(0) == 0)
    def _():
        s_scr[...] = jnp.zeros_like(s_scr)

    st_ref[0] = s_scr[...]
    q = _ssd_common(dtc_ref, dtr_ref, bc_ref, br_ref, alc_ref, alr_ref, e_ref, rev)
    masks = _head_masks()
    xdt = xs_ref[...] * q["dt_x"]
    xw = xdt * q["w_x"]
    early = []
    for g in range(SSD_GROUPS):
        gc = slice(g * GW, (g + 1) * GW)
        nc = slice(g * SSD_STATE, (g + 1) * SSD_STATE)
        bg = _mx(b_ref[:, nc])
        cg = _mx(c_ref[:, nc])
        sg = s_scr[:, gc]
        early.append((_dot(cg, bg, ((1,), (1,))), _dot(cg, _mx(sg)) * q["e_x"][:, gc]))
        s_scr[:, gc] = sg * q["cd_x"][:, gc] + _dot(bg, _mx(xw[:, gc]), ((0,), (0,)))
    for g in range(SSD_GROUPS):
        gc = slice(g * GW, (g + 1) * GW)
        cb, y = early[g]
        ms = []
        for r in range(SSD_REP):
            hh = hoff + g * SSD_REP + r
            diff = q["cs_c"][:, hh:hh + 1] - q["cs_r"][hh:hh + 1, :]
            ms.append(_mx(cb * jnp.exp(jnp.where(q["tri"], diff, NEG))))
        y4 = _dot(jnp.concatenate(ms, axis=0), _mx(xdt[:, gc]))
        for r in range(SSD_REP):
            y = y + y4[r * L:(r + 1) * L] * masks[r]
        y_ref[:, gc] = y + ins[10][:, gc] if has_add else y


def _dt_rows_body(p_ref, o_ref):
    o_ref[...] = p_ref[...].T


def _dt_rows(proj, tr=512):
    t = proj.shape[0]
    tr = min(tr, t)
    return pl.pallas_call(
        functools.partial(_dt_rows_body), name="dt_rows", grid=(t // tr,),
        in_specs=[pl.BlockSpec((tr, LANE), lambda i: (i, 0))], out_specs=pl.BlockSpec((LANE, tr), lambda i: (0, i)),
        out_shape=jax.ShapeDtypeStruct((LANE, t), F32), compiler_params=_params(1),
    )(proj)


def _ssd_specs(t, rev):
    nc = t // CHUNK
    cm = (lambda i: nc - 1 - i) if rev else (lambda i: i)
    xs = pl.BlockSpec((CHUNK, D_INNER), lambda i: (cm(i), 0))
    bb = pl.BlockSpec((CHUNK, SSD_GROUPS * SSD_STATE), lambda i: (cm(i), 2))
    cc = pl.BlockSpec((CHUNK, SSD_GROUPS * SSD_STATE), lambda i: (cm(i), 3))
    dtc = pl.BlockSpec((CHUNK, LANE), lambda i: (cm(i), 0))
    dtr = pl.BlockSpec((LANE, CHUNK), lambda i: (0, cm(i)))
    colv = pl.BlockSpec((1, LANE), lambda i: (0, 0))
    rowv = pl.BlockSpec((LANE, 1), lambda i: (0, 0))
    return cm, xs, bb, cc, dtc, dtr, colv, rowv


def _ssd_fwd(xbc, proj, dtt, bias, alog, rev, comm=None, add=None):
    t = xbc.shape[0]
    nc = t // CHUNK
    cm, xs, bb, cc, dtc, dtr, colv, rowv = _ssd_specs(t, rev)
    hoff = SSD_HEADS if rev else 0
    e = jnp.asarray(_expand_np(hoff), BF16)
    in_specs = [xs, bb, cc, dtc, dtr, colv, rowv, colv, rowv, pl.BlockSpec((LANE, D_INNER), lambda i: (0, 0))]
    args = [xbc, xbc, xbc, proj, dtt, bias.reshape(1, LANE), bias.reshape(LANE, 1), alog.reshape(1, LANE), alog.reshape(LANE, 1), e]
    out_specs = [xs, pl.BlockSpec((1, SSD_STATE, D_INNER), lambda i: (cm(i), 0, 0))]
    out_shape = [jax.ShapeDtypeStruct((t, D_INNER), F32), jax.ShapeDtypeStruct((nc, SSD_STATE, D_INNER), F32)]
    scratch = [pltpu.VMEM((SSD_STATE, D_INNER), F32)]
    if add is not None:
        in_specs, args = in_specs + [xs], args + [add]
    if comm is not None:
        in_specs, args = in_specs + [ANY] * comm.n, args + comm.arrs
        out_specs, out_shape = out_specs + [ANY] * comm.n, out_shape + comm.out_shapes()
        scratch = scratch + comm.sems()
    return pl.pallas_call(
        functools.partial(_ssd_fwd_body, rev=rev, hoff=hoff, nsteps=nc, comm=comm, has_add=add is not None), name="ssd_fwd_rev" if rev else "ssd_fwd",
        grid=(nc,), in_specs=in_specs, out_specs=out_specs, out_shape=out_shape, scratch_shapes=scratch,
        compiler_params=_params(1),
    )(*args)


def _ssd_bwd_body(*refs, rev, hoff, first, nsteps, comm):
    L = CHUNK
    ins, cin, (dx_ref, ddt_ref, sm_ref), cout, (ds_scr,), csem = _split_refs(refs, 14 if first else 18, 3, comm)
    (xs_ref, b_ref, c_ref, dtc_ref, dtr_ref, bc_ref, br_ref, alc_ref, alr_ref, e_ref, et_ref, dy_ref, sp_ref,
     dsk_ref) = ins[:14]
    if not first:
        pdx_ref, pddt_ref, psm_ref = ins[14:17]
    if comm is not None:
        comm.run(cin, cout, csem, pl.program_id(0), nsteps)

    @pl.when(pl.program_id(0) == 0)
    def _():
        ds_scr[...] = jnp.zeros_like(ds_scr)
        sm_ref[...] = jnp.zeros_like(sm_ref) if first else psm_ref[...]

    q = _ssd_common(dtc_ref, dtr_ref, bc_ref, br_ref, alc_ref, alr_ref, e_ref, rev)
    masks = _head_masks()
    xs = xs_ref[...]
    dy = dy_ref[...]
    xdt = xs * q["dt_x"]
    xw = xdt * q["w_x"]
    dye = dy * q["e_x"]
    ds_old = ds_scr[...]
    dxdt_parts, q1_parts, q2_parts = [], [], []
    ddiag = jnp.zeros((L, LANE), F32)
    early = []
    for g in range(SSD_GROUPS):
        gc = slice(g * GW, (g + 1) * GW)
        nc = slice(g * SSD_STATE, (g + 1) * SSD_STATE)
        bg = _mx(b_ref[:, nc])
        cg = _mx(c_ref[:, nc])
        dy_g = _mx(dy[:, gc])
        xdt_g = _mx(xdt[:, gc])
        dsg = ds_old[:, gc]
        spg = _mx(sp_ref[0, :, gc])
        dye_g = _mx(dye[:, gc])
        bds = _dot(bg, _mx(dsg))
        early.append(dict(
            bg=bg, cg=cg, dy_g=dy_g, bds=bds,
            cb=_dot(cg, bg, ((1,), (1,))), cbt=_dot(bg, cg, ((1,), (1,))),
            dm=_dot(_mx(jnp.concatenate([dy[:, gc] * m for m in masks], axis=0)), xdt_g, ((1,), (1,))),
            dmt=_dot(_mx(jnp.concatenate([xdt[:, gc] * m for m in masks], axis=0)), dy_g, ((1,), (1,))),
            dc=_dot(dye_g, spg, ((1,), (1,))), db=_dot(_mx(xw[:, gc]), _mx(dsg), ((1,), (1,)))))
        q1_parts.append(dye[:, gc] * _dot(cg, spg))
        q2_parts.append(xw[:, gc] * bds)
        ds_scr[:, gc] = dsg * q["cd_x"][:, gc] + _dot(cg, dye_g, ((0,), (0,)))
    for g in range(SSD_GROUPS):
        gc = slice(g * GW, (g + 1) * GW)
        e = early[g]
        bg, cg, dy_g, bds, cb, cbt, dm, dmt = (e[k] for k in ("bg", "cg", "dy_g", "bds", "cb", "cbt", "dm", "dmt"))
        dcb = jnp.zeros((L, L), F32)
        dcbt = jnp.zeros((L, L), F32)
        mts = []
        for r in range(SSD_REP):
            hh = hoff + g * SSD_REP + r
            col = q["cs_c"][:, hh:hh + 1]
            row = q["cs_r"][hh:hh + 1, :]
            dec = jnp.exp(jnp.where(q["tri"], col - row, NEG))
            dect = jnp.exp(jnp.where(q["trit"], row - col, NEG))
            pd = dm[r * L:(r + 1) * L] * dec
            pdt = dmt[r * L:(r + 1) * L] * dect
            dcb = dcb + pd
            dcbt = dcbt + pdt
            mts.append(_mx(cbt * dect))
            ddiag = ddiag + (jnp.sum(pd * cb, axis=1, keepdims=True) - jnp.sum(pdt * cbt, axis=1, keepdims=True)) * _onehot_lane(hh)
        x4 = _dot(jnp.concatenate(mts, axis=0), dy_g)
        dxdt_g = bds * q["w_x"][:, gc]
        for r in range(SSD_REP):
            dxdt_g = dxdt_g + x4[r * L:(r + 1) * L] * masks[r]
        dxdt_parts.append(dxdt_g)
        dc_g = _dot(_mx(dcb), bg) + e["dc"]
        db_g = _dot(_mx(dcbt), cg) + e["db"]
        boff = D_INNER + g * SSD_STATE
        coff = D_INNER + SSD_GROUPS * SSD_STATE + g * SSD_STATE
        if first:
            dx_ref[:, boff:boff + SSD_STATE] = db_g
            dx_ref[:, coff:coff + SSD_STATE] = dc_g
        else:
            dx_ref[:, boff:boff + SSD_STATE] = pdx_ref[:, boff:boff + SSD_STATE] + db_g
            dx_ref[:, coff:coff + SSD_STATE] = pdx_ref[:, coff:coff + SSD_STATE] + dc_g
    dxdt = jnp.concatenate(dxdt_parts, axis=1)
    et = et_ref[...]
    hs = _dot(_mx(jnp.concatenate([jnp.concatenate(q1_parts, axis=1), jnp.concatenate(q2_parts, axis=1), dxdt * xs], axis=0)), et)
    q1, q2, r3 = hs[0:L], hs[L:2 * L], hs[2 * L:3 * L]
    t_prev = _split_dot(jnp.sum(ds_old * sp_ref[0], axis=0, keepdims=True), et) * q["e_last"]
    rows = lax.broadcasted_iota(jnp.int32, (L, LANE), 0)
    dcs = ddiag + q1 - q2 + jnp.where(rows == q["il"], jnp.sum(q2, axis=0, keepdims=True) + t_prev, 0.0)
    dad = jnp.dot(q["trit"].astype(F32), dcs, precision=lax.Precision.HIGHEST, preferred_element_type=F32)
    ddt = dad * q["a_c"] + r3
    ddt_raw = ddt * _sigmoid(q["raw_c"])
    dal = jnp.sum(dad * q["dt_c"], axis=0, keepdims=True) * q["a_c"]
    dbias = jnp.sum(ddt_raw, axis=0, keepdims=True)
    sm_ref[...] += jnp.concatenate([dal, dbias, jnp.zeros((6, LANE), F32)], axis=0)
    dxs = dxdt * q["dt_x"]
    if first:
        dx_ref[:, 0:D_INNER] = dxs + dy * dsk_ref[...]
        ddt_ref[...] = ddt_raw
    else:
        dx_ref[:, 0:D_INNER] = pdx_ref[:, 0:D_INNER] + dxs
        ddt_ref[...] = jnp.concatenate([(pddt_ref[...] + ddt_raw).astype(ddt_ref.dtype), jnp.zeros((L, DTW - LANE), ddt_ref.dtype)],
                                       axis=1)


def _ssd_bwd(xbc, proj, dtt, bias, alog, dy, states, dskip_x, rev, prev=None, dproj=None, comm=None):
    t = xbc.shape[0]
    nc = t // CHUNK
    first = prev is None
    cm, xs, bb, cc, dtc, dtr, colv, rowv = _ssd_specs(t, not rev)
    hoff = SSD_HEADS if rev else 0
    e_np = _expand_np(hoff)
    e = jnp.asarray(e_np, BF16)
    et = jnp.asarray(e_np.T, BF16)
    st_spec = pl.BlockSpec((1, SSD_STATE, D_INNER), lambda i: (cm(i), 0, 0))
    dxo = pl.BlockSpec((CHUNK, CONV_DIM), lambda i: (cm(i), 0))
    ddto = pl.BlockSpec((CHUNK, LANE), lambda i: (cm(i), 0))
    smo = pl.BlockSpec((8, LANE), lambda i: (0, 0))
    in_specs = [xs, bb, cc, dtc, dtr, colv, rowv, colv, rowv, pl.BlockSpec((LANE, D_INNER), lambda i: (0, 0)),
                pl.BlockSpec((D_INNER, LANE), lambda i: (0, 0)), xs, st_spec, pl.BlockSpec((1, D_INNER), lambda i: (0, 0))]
    args = [xbc, xbc, xbc, proj, dtt, bias.reshape(1, LANE), bias.reshape(LANE, 1), alog.reshape(1, LANE),
            alog.reshape(LANE, 1), e, et, dy, states, dskip_x]
    out_specs = [dxo, ddto, smo]
    out_shape = [jax.ShapeDtypeStruct((t, CONV_DIM), F32), jax.ShapeDtypeStruct((t, LANE), F32),
                 jax.ShapeDtypeStruct((8, LANE), F32)]
    aliases = {}
    if not first:
        in_specs += [dxo, ddto, smo, ANY]
        args += list(prev) + [dproj]
        out_specs[1] = pl.BlockSpec((CHUNK, DTW), lambda i: (cm(i), DT0 // DTW))
        out_shape[1] = jax.ShapeDtypeStruct((t, NP), BF16)
        aliases = {17: 1}
    scratch = [pltpu.VMEM((SSD_STATE, D_INNER), F32)]
    if comm is not None:
        in_specs, args = in_specs + [ANY] * comm.n, args + comm.arrs
        out_specs, out_shape = out_specs + [ANY] * comm.n, out_shape + comm.out_shapes()
        scratch = scratch + comm.sems()
    return pl.pallas_call(
        functools.partial(_ssd_bwd_body, rev=rev, hoff=hoff, first=first, nsteps=nc, comm=comm),
        name="ssd_bwd_rev" if rev else "ssd_bwd", grid=(nc,), in_specs=in_specs, out_specs=out_specs, out_shape=out_shape,
        scratch_shapes=scratch, input_output_aliases=aliases, compiler_params=_params(1),
    )(*args)


def _gnorm_parts(y_ref, xs_ref, z_ref, dsk_ref):
    xs = xs_ref[...]
    y = y_ref[...] + xs * dsk_ref[...]
    z = z_ref[...].astype(F32)
    sg = _sigmoid(z)
    s = z * sg
    v = y * s
    rs = []
    for g in range(SSD_GROUPS):
        vg = v[:, g * GW:(g + 1) * GW]
        rs.append(jnp.broadcast_to(lax.rsqrt(jnp.mean(vg * vg, axis=-1, keepdims=True) + EPS), vg.shape))
    return xs, y, z, sg, s, v, jnp.concatenate(rs, axis=1)


def _gnorm_fwd_body(y_ref, xs_ref, z_ref, dsk_ref, w_ref, o_ref):
    _, _, _, _, _, v, rstd = _gnorm_parts(y_ref, xs_ref, z_ref, dsk_ref)
    o_ref[...] = (v * rstd * w_ref[...]).astype(o_ref.dtype)


def _gnorm_fwd(y, xbc, proj, dskip_x, norm_w, tr=256):
    t = y.shape[0]
    tr = min(tr, t)
    row = pl.BlockSpec((tr, D_INNER), lambda i: (i, 0))
    vec = pl.BlockSpec((1, D_INNER), lambda i: (0, 0))
    return pl.pallas_call(
        functools.partial(_gnorm_fwd_body), name="gnorm_fwd", grid=(t // tr,),
        in_specs=[row, row, pl.BlockSpec((tr, D_INNER), lambda i: (i, Z0 // D_INNER)), vec, vec],
        out_specs=row, out_shape=jax.ShapeDtypeStruct((t, D_INNER), BF16), compiler_params=_params(1),
    )(y, xbc, proj, dskip_x, norm_w.reshape(1, D_INNER))


def _gnorm_bwd_body(du_ref, y_ref, xs_ref, z_ref, dsk_ref, w_ref, et_ref, _, dy_ref, dz_ref, dw_ref, dd_ref, acc_ref, *,
                    nsteps):
    i = pl.program_id(0)

    @pl.when(i == 0)
    def _():
        dw_ref[...] = jnp.zeros_like(dw_ref)
        acc_ref[...] = jnp.zeros_like(acc_ref)

    xs, y, z, sg, s, v, rstd = _gnorm_parts(y_ref, xs_ref, z_ref, dsk_ref)
    du = du_ref[...].astype(F32)
    n = v * rstd
    dn = du * w_ref[...]
    dw_ref[...] += jnp.sum(du * n, axis=0, keepdims=True)
    prod = dn * n
    means = []
    for g in range(SSD_GROUPS):
        pg = prod[:, g * GW:(g + 1) * GW]
        means.append(jnp.broadcast_to(jnp.mean(pg, axis=-1, keepdims=True), pg.shape))
    dv = rstd * (dn - n * jnp.concatenate(means, axis=1))
    dy = dv * s
    dy_ref[...] = dy
    dz_ref[...] = (dv * y * (sg * (1.0 + z * (1.0 - sg)))).astype(dz_ref.dtype)
    acc_ref[...] += jnp.sum(dy * xs, axis=0, keepdims=True)

    @pl.when(i == nsteps - 1)
    def _():
        dd_ref[...] = _split_dot(acc_ref[...], et_ref[...])


def _gnorm_bwd(du, y, xbc, proj, dskip_x, norm_w, dproj, tr=256):
    t = y.shape[0]
    tr = min(tr, t)
    row = pl.BlockSpec((tr, D_INNER), lambda i: (i, 0))
    zcol = pl.BlockSpec((tr, D_INNER), lambda i: (i, Z0 // D_INNER))
    vec = pl.BlockSpec((1, D_INNER), lambda i: (0, 0))
    et = jnp.asarray(_expand_np(0).T, BF16)
    return pl.pallas_call(
        functools.partial(_gnorm_bwd_body, nsteps=t // tr), name="gnorm_bwd", grid=(t // tr,),
        in_specs=[row, row, row, zcol, vec, vec, pl.BlockSpec((D_INNER, LANE), lambda i: (0, 0)), ANY],
        out_specs=[row, zcol, vec, pl.BlockSpec((1, LANE), lambda i: (0, 0))],
        out_shape=[jax.ShapeDtypeStruct((t, D_INNER), F32), jax.ShapeDtypeStruct((t, NP), BF16),
                   jax.ShapeDtypeStruct((1, D_INNER), F32), jax.ShapeDtypeStruct((1, LANE), F32)],
        scratch_shapes=[pltpu.VMEM((1, D_INNER), F32)], input_output_aliases={7: 1},
        compiler_params=_params(1),
    )(du, y, xbc, proj, dskip_x, norm_w.reshape(1, D_INNER), et, dproj)


def _gate_fwd_body(gt_ref, b_ref, ys_ref, ya_ref, o_ref):
    g = _sigmoid(gt_ref[...].astype(F32) + b_ref[...])
    o_ref[...] = (g[:, 0:D_MODEL] * ys_ref[...] + g[:, D_MODEL:] * ya_ref[...]).astype(o_ref.dtype)


def _gate_fwd(proj, b_gate, y_ssd, y_attn, tr=512):
    t = y_ssd.shape[0]
    tr = min(tr, t)
    row = pl.BlockSpec((tr, D_MODEL), lambda i: (i, 0))
    return pl.pallas_call(
        functools.partial(_gate_fwd_body), name="gate_fwd", grid=(t // tr,),
        in_specs=[pl.BlockSpec((tr, 2 * D_MODEL), lambda i: (i, GT0 // (2 * D_MODEL))),
                  pl.BlockSpec((1, 2 * D_MODEL), lambda i: (0, 0)), row, row],
        out_specs=row, out_shape=jax.ShapeDtypeStruct((t, D_MODEL), BF16), compiler_params=_params(1),
    )(proj, b_gate.reshape(1, 2 * D_MODEL), y_ssd, y_attn)


def _gate_bwd_body(dm_ref, gt_ref, b_ref, ys_ref, ya_ref, dys_ref, dya_ref, dgt_ref, db_ref):
    @pl.when(pl.program_id(0) == 0)
    def _():
        db_ref[...] = jnp.zeros_like(db_ref)

    g = _sigmoid(gt_ref[...].astype(F32) + b_ref[...])
    gs = g[:, 0:D_MODEL]
    ga = g[:, D_MODEL:]
    dm = dm_ref[...].astype(F32)
    dys_ref[...] = (dm * gs).astype(dys_ref.dtype)
    dya_ref[...] = (dm * ga).astype(dya_ref.dtype)
    dgt = jnp.concatenate([dm * ys_ref[...] * gs * (1.0 - gs), dm * ya_ref[...] * ga * (1.0 - ga)], axis=1)
    dgt_ref[...] = dgt.astype(dgt_ref.dtype)
    db_ref[...] += jnp.sum(dgt, axis=0, keepdims=True)


def _gate_bwd(dmix, proj, b_gate, y_ssd, y_attn, tr=512):
    t = y_ssd.shape[0]
    tr = min(tr, t)
    row = pl.BlockSpec((tr, D_MODEL), lambda i: (i, 0))
    gates = pl.BlockSpec((tr, 2 * D_MODEL), lambda i: (i, GT0 // (2 * D_MODEL)))
    vec2 = pl.BlockSpec((1, 2 * D_MODEL), lambda i: (0, 0))
    return pl.pallas_call(
        functools.partial(_gate_bwd_body), name="gate_bwd", grid=(t // tr,),
        in_specs=[row, gates, vec2, row, row],
        out_specs=[row, row, gates, vec2],
        out_shape=[jax.ShapeDtypeStruct((t, D_MODEL), BF16), jax.ShapeDtypeStruct((t, D_MODEL), BF16),
                   jax.ShapeDtypeStruct((t, NP), BF16), jax.ShapeDtypeStruct((1, 2 * D_MODEL), F32)],
        compiler_params=_params(1),
    )(dmix, proj, b_gate.reshape(1, 2 * D_MODEL), y_ssd, y_attn)


def _bucket_np():
    i = np.arange(BLOCK)[:, None]
    j = np.arange(3 * BLOCK)[None, :]
    rel = j - BLOCK - i
    nb = N_BUCKETS // 2
    max_exact = nb // 2
    ret = np.where(rel > 0, nb, 0)
    n = np.abs(rel)
    nf = np.maximum(n, 1).astype(np.float32)
    large = max_exact + (np.log(nf / np.float32(max_exact)) / np.float32(math.log(MAX_DISTANCE / max_exact))
                         * np.float32(nb - max_exact)).astype(np.int32)
    large = np.minimum(large, nb - 1)
    bucket = ret + np.where(n < max_exact, n, large)
    band = np.abs(rel) <= WINDOW
    return bucket, band


def _onehot_np():
    bucket, _ = _bucket_np()
    oh = np.zeros((BLOCK * 3 * BLOCK, LANE), np.float32)
    oh[np.arange(oh.shape[0]), bucket.reshape(-1)] = 1.0
    return oh


REP = Q_HEADS // KV_HEADS


def _group_cols(ref, g):
    return jnp.concatenate([ref[:, (g * REP + r) * HEAD_PAD:(g * REP + r + 1) * HEAD_PAD] for r in range(REP)], axis=0)


def _group_lane(ref, g, rows):
    return jnp.concatenate([jnp.broadcast_to(ref[:, g * REP + r:g * REP + r + 1], (rows, 1)) for r in range(REP)], axis=0)


def _attn_kv(refs, g):
    gc = slice(g * HEAD_PAD, (g + 1) * HEAD_PAD)
    return _mx(jnp.concatenate([r[:, gc] for r in refs], axis=0))


def _attn_logits(q_ref, kg, bias_ref, edge, g):
    q4 = _mx(_group_cols(q_ref, g))
    s = _dot(q4, kg, ((1,), (1,))) * (HEAD_DIM ** -0.5) + bias_ref[g] + edge
    return q4, s


def _attn_edge(n, nb):
    col = lax.broadcasted_iota(jnp.int32, (1, 3 * BLOCK), 1)
    bad = ((col < BLOCK) & (n == 0)) | ((col >= 2 * BLOCK) & (n == nb - 1))
    return jnp.where(bad, NEG, 0.0)


def _onehot_lane(h):
    return (lax.broadcasted_iota(jnp.int32, (1, LANE), 1) == h).astype(F32)


def _attn_fwd_body(*refs, nb, comm):
    ins, cin, (o_ref, lse_ref), cout, _, csem = _split_refs(refs, 9, 2, comm)
    q_ref, kp_ref, kc_ref, kn_ref, vp_ref, vc_ref, vn_ref, bias_ref, sink_ref = ins
    n = pl.program_id(0)
    if comm is not None:
        comm.run(cin, cout, csem, n, nb)
    edge = _attn_edge(n, nb)
    lse = jnp.zeros((BLOCK, LANE), F32)
    logits = [_attn_logits(q_ref, _attn_kv((kp_ref, kc_ref, kn_ref), g), bias_ref, edge, g)[1] for g in range(KV_HEADS)]
    for g in range(KV_HEADS):
        vg = _attn_kv((vp_ref, vc_ref, vn_ref), g)
        s = logits[g]
        sink = _group_lane(sink_ref, g, BLOCK)
        m = jnp.maximum(jnp.max(s, axis=-1, keepdims=True), sink)
        p = jnp.exp(s - m)
        den = jnp.sum(p, axis=-1, keepdims=True) + jnp.exp(sink - m)
        o4 = _dot(_mx(p * (1.0 / den)), vg)
        l4 = m + jnp.log(den)
        for r in range(REP):
            h = g * REP + r
            o_ref[:, h * HEAD_PAD:(h + 1) * HEAD_PAD] = o4[r * BLOCK:(r + 1) * BLOCK].astype(o_ref.dtype)
            lse = lse + l4[r * BLOCK:(r + 1) * BLOCK] * _onehot_lane(h)
    lse_ref[...] = lse


def _attn_specs(t, clamp):
    nb = t // BLOCK
    cur = (lambda n: jnp.minimum(n, nb - 1)) if clamp else (lambda n: n)
    prv = lambda n: jnp.maximum(cur(n) - 1, 0)
    nxt = lambda n: jnp.minimum(cur(n) + 1, nb - 1)
    kb, vb = K0 // KP, V0 // KP
    qs = pl.BlockSpec((BLOCK, QP), lambda n: (cur(n), Q0 // QP))
    ks = [pl.BlockSpec((BLOCK, KP), lambda n, f=f: (f(n), kb)) for f in (prv, cur, nxt)]
    vs = [pl.BlockSpec((BLOCK, KP), lambda n, f=f: (f(n), vb)) for f in (prv, cur, nxt)]
    consts = [pl.BlockSpec((KV_HEADS, REP * BLOCK, 3 * BLOCK), lambda n: (0, 0, 0)), pl.BlockSpec((1, LANE), lambda n: (0, 0))]
    return nb, cur, qs, ks, vs, consts


def _attn_fwd(proj, bias, sink, comm=None):
    t = proj.shape[0]
    nb, cur, qs, ks, vs, consts = _attn_specs(t, False)
    in_specs, args = [qs] + ks + vs + consts, [proj] * 7 + [bias, sink]
    out_specs = [pl.BlockSpec((BLOCK, QP), lambda n: (n, 0)), pl.BlockSpec((BLOCK, LANE), lambda n: (n, 0))]
    out_shape = [jax.ShapeDtypeStruct((t, QP), BF16), jax.ShapeDtypeStruct((t, LANE), F32)]
    scratch = []
    if comm is not None:
        in_specs, args = in_specs + [ANY] * comm.n, args + comm.arrs
        out_specs, out_shape = out_specs + [ANY] * comm.n, out_shape + comm.out_shapes()
        scratch = comm.sems()
    return pl.pallas_call(
        functools.partial(_attn_fwd_body, nb=nb, comm=comm), name="attn_fwd", grid=(nb,),
        in_specs=in_specs, out_specs=out_specs, out_shape=out_shape, scratch_shapes=scratch, compiler_params=_params(1),
    )(*args)


def _attn_bwd_body(*refs, nb, comm):
    ins, cin, (dqkv_ref, dbias_ref, dsink_ref), cout, (ak, bk, av, bv, dq_scr), csem = _split_refs(refs, 13, 3, comm)
    q_ref, kp_ref, kc_ref, kn_ref, vp_ref, vc_ref, vn_ref, bias_ref, sink_ref, do_ref, o_ref, lse_ref = ins[:12]
    dk_ref = dqkv_ref.at[:, K0:K0 + KP]
    dv_ref = dqkv_ref.at[:, V0:V0 + KP]
    n = pl.program_id(0)
    scale = HEAD_DIM ** -0.5
    if comm is not None:
        comm.run(cin, cout, csem, n, nb + 1)

    @pl.when(n == 0)
    def _():
        for r in (ak, bk, av, bv, dq_scr, dbias_ref, dsink_ref):
            r[...] = jnp.zeros_like(r)

    dqkv_ref[:, Q0:Q0 + QP] = dq_scr[...]

    @pl.when(n < nb)
    def _():
        edge = _attn_edge(n, nb)
        dsink = jnp.zeros((1, LANE), F32)
        early = []
        for g in range(KV_HEADS):
            kg = _attn_kv((kp_ref, kc_ref, kn_ref), g)
            q4, s = _attn_logits(q_ref, kg, bias_ref, edge, g)
            do = _group_cols(do_ref, g).astype(F32)
            early.append((kg, q4, s, do, _dot(_mx(do), _attn_kv((vp_ref, vc_ref, vn_ref), g), ((1,), (1,)))))
        for g in range(KV_HEADS):
            gc = slice(g * HEAD_PAD, (g + 1) * HEAD_PAD)
            kg, q4, s, do, dp = early[g]
            lse = _group_lane(lse_ref, g, BLOCK)
            p = jnp.exp(s - lse)
            delta = jnp.sum(do * _group_cols(o_ref, g).astype(F32), axis=-1, keepdims=True)
            ds = p * (dp - delta)
            sunk = jnp.exp(_group_lane(sink_ref, g, BLOCK) - lse) * delta
            dbias_ref[g] += ds
            dsb = _mx(ds)
            dq4 = _dot(dsb, kg) * scale
            for r in range(REP):
                h = g * REP + r
                dq_scr[:, h * HEAD_PAD:(h + 1) * HEAD_PAD] = dq4[r * BLOCK:(r + 1) * BLOCK].astype(dq_scr.dtype)
                dsink = dsink - jnp.sum(sunk[r * BLOCK:(r + 1) * BLOCK], axis=0, keepdims=True) * _onehot_lane(h)
            dk_c = _dot(dsb, q4, ((0,), (0,))) * scale
            dv_c = _dot(_mx(p), _mx(do), ((0,), (0,)))
            for acc_a, acc_b, out, c in ((ak, bk, dk_ref, dk_c), (av, bv, dv_ref, dv_c)):
                out[:, gc] = (acc_a[:, gc] + c[0:BLOCK]).astype(out.dtype)
                acc_a[:, gc] = acc_b[:, gc] + c[BLOCK:2 * BLOCK]
                acc_b[:, gc] = c[2 * BLOCK:3 * BLOCK]
        dsink_ref[...] += dsink

    @pl.when(n == nb)
    def _():
        dk_ref[...] = ak[...].astype(dk_ref.dtype)
        dv_ref[...] = av[...].astype(dv_ref.dtype)


def _attn_bwd(proj, bias, sink, do, o, lse, dproj, comm=None):
    t = proj.shape[0]
    nb, cur, qs, ks, vs, consts = _attn_specs(t, True)
    rowq = pl.BlockSpec((BLOCK, QP), lambda n: (cur(n), 0))
    late = pl.BlockSpec((BLOCK, QKV), lambda n: (jnp.maximum(n - 1, 0), Q0 // QKV))
    in_specs = [qs] + ks + vs + consts + [rowq, rowq, pl.BlockSpec((BLOCK, LANE), lambda n: (cur(n), 0)), ANY]
    args = [proj] * 7 + [bias, sink, do, o, lse, dproj]
    out_specs = [late, pl.BlockSpec((KV_HEADS, REP * BLOCK, 3 * BLOCK), lambda n: (0, 0, 0)), pl.BlockSpec((1, LANE), lambda n: (0, 0))]
    out_shape = [jax.ShapeDtypeStruct((t, NP), BF16), jax.ShapeDtypeStruct((KV_HEADS, REP * BLOCK, 3 * BLOCK), F32),
                 jax.ShapeDtypeStruct((1, LANE), F32)]
    scratch = [pltpu.VMEM((BLOCK, KP), F32) for _ in range(4)] + [pltpu.VMEM((BLOCK, QP), BF16)]
    if comm is not None:
        in_specs, args = in_specs + [ANY] * comm.n, args + comm.arrs
        out_specs, out_shape = out_specs + [ANY] * comm.n, out_shape + comm.out_shapes()
        scratch = scratch + comm.sems()
    return pl.pallas_call(
        functools.partial(_attn_bwd_body, nb=nb, comm=comm), name="attn_bwd", grid=(nb + 1,),
        in_specs=in_specs, out_specs=out_specs, out_shape=out_shape, scratch_shapes=scratch, input_output_aliases={12: 0},
        compiler_params=_params(1),
    )(*args)


def _adamw_body(gp_ref, w_ref, m_ref, v_ref, g_ref, d_ref, nm_ref, nv_ref, *, nparts):
    g = gp_ref[0].astype(F32)
    for j in range(1, nparts):
        g = g + gp_ref[j].astype(F32)
    m = ADAM_B1 * m_ref[...] + (1.0 - ADAM_B1) * g
    v = ADAM_B2 * v_ref[...] + (1.0 - ADAM_B2) * (g * g)
    m_hat = m / (1.0 - ADAM_B1 ** ADAM_STEP)
    v_hat = v / (1.0 - ADAM_B2 ** ADAM_STEP)
    g_ref[...] = g
    d_ref[...] = -ADAM_LR * (m_hat / (jnp.sqrt(v_hat) + ADAM_EPS) + ADAM_WD * w_ref[...])
    nm_ref[...] = m
    nv_ref[...] = v


def _adamw_layer_body(*refs, nparts):
    gp_ref, w_ref, m_ref, v_ref = refs[:4]
    g_ref, d_ref, nm_ref, nv_ref = refs[-4:]
    g = gp_ref[0].astype(F32)
    for j in range(1, nparts):
        g = g + gp_ref[j].astype(F32)
    m = ADAM_B1 * m_ref[0] + (1.0 - ADAM_B1) * g
    v = ADAM_B2 * v_ref[0] + (1.0 - ADAM_B2) * (g * g)
    m_hat = m / (1.0 - ADAM_B1 ** ADAM_STEP)
    v_hat = v / (1.0 - ADAM_B2 ** ADAM_STEP)
    g_ref[0] = g
    d_ref[0] = -ADAM_LR * (m_hat / (jnp.sqrt(v_hat) + ADAM_EPS) + ADAM_WD * w_ref[0])
    nm_ref[0] = m
    nv_ref[0] = v


def _adamw_layer(gparts, row0, w, m, v, l, prev, name, tr=128):
    p, c = gparts.shape[0], gparts.shape[2]
    per = w.shape[1]
    tr = min(tr, per)
    assert per % tr == 0 and row0 % tr == 0 and w.shape[2] == c
    lay = pl.BlockSpec((1, tr, c), lambda i: (l, i, 0))
    in_specs = [pl.BlockSpec((p, tr, c), lambda i: (0, row0 // tr + i, 0)), lay, lay, lay]
    args = [gparts, w, m, v]
    aliases = {}
    if prev is not None:
        in_specs, args, aliases = in_specs + [ANY] * 4, args + list(prev), {4 + k: k for k in range(4)}
    sd = jax.ShapeDtypeStruct(w.shape, F32)
    return pl.pallas_call(
        functools.partial(_adamw_layer_body, nparts=p), name=name, grid=(per // tr,),
        in_specs=in_specs, out_specs=[lay] * 4, out_shape=[sd] * 4, input_output_aliases=aliases, compiler_params=_params(1),
    )(*args)


def _adamw(gparts, w, m, v, name, tr=256):
    p, r, c = gparts.shape
    tr = min(tr, r)
    assert r % tr == 0
    row = pl.BlockSpec((tr, c), lambda i: (i, 0))
    sd = jax.ShapeDtypeStruct((r, c), F32)
    return pl.pallas_call(
        functools.partial(_adamw_body, nparts=p), name=name, grid=(r // tr,),
        in_specs=[pl.BlockSpec((p, tr, c), lambda i: (0, i, 0)), row, row, row],
        out_specs=[row, row, row, row], out_shape=[sd, sd, sd, sd], compiler_params=_params(1),
    )(gparts, w, m, v)


MESH = pl.DeviceIdType.MESH
N_REL = N_DEV - 1


def _gather_phases(ins, outs, sems):
    n = len(ins)
    send_sems, recv_sems, local_sems = sems
    x, y, c = lax.axis_index("x"), lax.axis_index("y"), lax.axis_index("c")
    me, sibling = (x, y, c), (x, y, 1 - c)
    chips = [(1 - x, y), (x, 1 - y), (1 - x, 1 - y)]

    def slot(a, p):
        return outs[a].at[4 * p[0] + 2 * p[1] + p[2]]

    def copy(a, k, block, to, src=None):
        return pltpu.make_async_remote_copy(
            src_ref=slot(a, block) if src is None else src, dst_ref=slot(a, block),
            send_sem=send_sems.at[a * N_REL + k], recv_sem=recv_sems.at[a * N_REL + k], device_id=to, device_id_type=MESH)

    mine = [pltpu.make_async_copy(ins[a], slot(a, me), local_sems.at[a]) for a in range(n)]
    first = []
    for a in range(n):
        first.append(copy(a, 0, me, sibling, src=ins[a]))
        first += [copy(a, 1 + j, me, (*chip, c), src=ins[a]) for j, chip in enumerate(chips)]
    passed = [copy(a, 4 + j, (*chip, c), sibling) for j, chip in enumerate(chips) for a in range(n)]

    def start():
        for cp in mine + first:
            cp.start()

    def pass_on():
        i = 0
        for j, chip in enumerate(chips):
            for a in range(n):
                copy(a, 1 + j, (*chip, c), me).wait_recv()
                passed[i].start()
                i += 1

    def finish():
        for a in range(n):
            copy(a, 0, sibling, me).wait_recv()
            for j, chip in enumerate(chips):
                copy(a, 4 + j, (*chip, 1 - c), me).wait_recv()
        for cp in first + passed:
            cp.wait_send()
        for cp in mine:
            cp.wait()

    return start, pass_on, finish


def _exchange_phases(ins, outs, sems):
    n = len(ins)
    send_sems, recv_sems, local_sems = sems
    x, y, c = lax.axis_index("x"), lax.axis_index("y"), lax.axis_index("c")
    me = 4 * x + 2 * y + c
    mine = [pltpu.make_async_copy(ins[a].at[me], outs[a].at[me], local_sems.at[a]) for a in range(n)]
    copies = []
    for a in range(n):
        for k in range(1, N_DEV):
            px = 1 - x if k & 4 else x
            py = 1 - y if k & 2 else y
            pc = 1 - c if k & 1 else c
            peer = 4 * px + 2 * py + pc
            send = pltpu.make_async_remote_copy(
                src_ref=ins[a].at[peer], dst_ref=outs[a].at[me], send_sem=send_sems.at[a * N_REL + k - 1],
                recv_sem=recv_sems.at[a * N_REL + k - 1], device_id=(px, py, pc), device_id_type=MESH)
            recv = pltpu.make_async_remote_copy(
                src_ref=ins[a].at[me], dst_ref=outs[a].at[peer], send_sem=send_sems.at[a * N_REL + k - 1],
                recv_sem=recv_sems.at[a * N_REL + k - 1], device_id=(px, py, pc), device_id_type=MESH)
            copies.append((send, recv))

    def start():
        for cp in mine:
            cp.start()
        for send, _ in copies:
            send.start()

    def finish():
        for send, recv in copies:
            send.wait_send()
            recv.wait_recv()
        for cp in mine:
            cp.wait()

    return start, None, finish


class _Comm:
    def __init__(self, kind, arrs):
        self.kind, self.arrs, self.n = kind, list(arrs), len(arrs)

    def out_shapes(self):
        if self.kind == "gather":
            return [jax.ShapeDtypeStruct((N_DEV,) + a.shape, a.dtype) for a in self.arrs]
        return [jax.ShapeDtypeStruct(a.shape, a.dtype) for a in self.arrs]

    def sems(self):
        return [pltpu.SemaphoreType.DMA((self.n * N_REL,)), pltpu.SemaphoreType.DMA((self.n * N_REL,)),
                pltpu.SemaphoreType.DMA((self.n,))]

    def run(self, ins, outs, sems, step, nsteps):
        start, mid, finish = (_gather_phases if self.kind == "gather" else _exchange_phases)(ins, outs, sems)
        if nsteps is None:
            start()
            if mid is not None:
                mid()
            finish()
            return
        pl.when(step == 0)(start)
        if mid is not None:
            pl.when(step == (3 * nsteps) // 4)(mid)
        pl.when(step == nsteps - 1)(finish)


def _split_refs(refs, n_in, n_out, comm):
    k = comm.n if comm is not None else 0
    i = 0
    parts = []
    for cnt in (n_in, k, n_out, k):
        parts.append(refs[i:i + cnt])
        i += cnt
    rest = refs[i:]
    if comm is None:
        return parts[0], (), parts[2], (), rest, ()
    return parts[0], parts[1], parts[2], parts[3], rest[:len(rest) - 3], rest[len(rest) - 3:]


def _comm_body(*refs, comm):
    n = comm.n
    comm.run(refs[:n], refs[n:2 * n], refs[2 * n:], None, None)


def _communicate(comm, name, vmem=False):
    spec = pl.BlockSpec(memory_space=pltpu.VMEM if vmem else pl.ANY)
    return pl.pallas_call(
        functools.partial(_comm_body, comm=comm), name=name,
        in_specs=[spec] * comm.n, out_specs=[spec] * comm.n, out_shape=comm.out_shapes(), scratch_shapes=comm.sems(),
        compiler_params=pltpu.CompilerParams(vmem_limit_bytes=VMEM_LIMIT),
    )(*comm.arrs)


ANY = pl.BlockSpec(memory_space=pl.ANY)


def _pad_heads(w, axis):
    shp = w.shape
    heads = shp[axis] // HEAD_DIM
    w = w.reshape(shp[:axis] + (heads, HEAD_DIM) + shp[axis + 1:])
    pad = [(0, 0)] * w.ndim
    pad[axis + 1] = (0, HEAD_PAD - HEAD_DIM)
    w = jnp.pad(w, pad)
    return w.reshape(shp[:axis] + (heads * HEAD_PAD,) + shp[axis + 1:])


def _unpad_heads(w, axis):
    shp = w.shape
    heads = shp[axis] // HEAD_PAD
    w = w.reshape(shp[:axis] + (heads, HEAD_PAD) + shp[axis + 1:])
    w = lax.slice_in_dim(w, 0, HEAD_DIM, axis=axis + 1)
    return w.reshape(shp[:axis] + (heads * HEAD_DIM,) + shp[axis + 1:])


def _w_in_to_padded(w):
    idx = np.cumsum((0,) + IN_SPLITS)
    z, xbc, dt, q, k, v, gates = [w[:, idx[i]:idx[i + 1]] for i in range(7)]
    zeros = lambda n: jnp.zeros((w.shape[0], n), w.dtype)
    return jnp.concatenate([_pad_heads(q, 1), _pad_heads(k, 1), _pad_heads(v, 1), xbc, dt, zeros(DTW - 2 * SSD_HEADS), z, gates],
                           axis=1)


def _w_in_from_padded(w):
    z = w[:, Z0:Z0 + D_INNER]
    gates = w[:, GT0:GT0 + 2 * D_MODEL]
    q = _unpad_heads(w[:, Q0:Q0 + QP], 1)
    xbc = w[:, XBC0:XBC0 + CONV_DIM]
    k = _unpad_heads(w[:, K0:K0 + KP], 1)
    v = _unpad_heads(w[:, V0:V0 + KP], 1)
    dt = w[:, DT0:DT0 + 2 * SSD_HEADS]
    return jnp.concatenate([z, xbc, dt, q, k, v, gates], axis=1)


def _pad_lane(v):
    v = v.reshape(-1)
    return jnp.pad(v, (0, LANE - v.shape[0]))


def _layer_consts(p):
    c = dict(p)
    c["conv_w8"] = jnp.pad(p["conv_w"].reshape(SSD_CONV, CONV_DIM), ((0, 8 - SSD_CONV), (0, 0)))
    c["conv_b1"] = p["conv_b"].reshape(1, CONV_DIM)
    c["dtb"] = _pad_lane(p["dt_bias"])
    c["alog"] = _pad_lane(p["a_log"])
    c["dskip_x"] = jnp.repeat(p["d_skip"], SSD_HEAD_DIM).reshape(1, D_INNER)
    c["sink"] = _pad_lane(p["attn_sink"]).reshape(1, LANE)
    return c


def _hosted(res, n_own, hook):
    res = list(res)
    if hook is not None:
        hook[1](res[n_own:])
    return res[0] if n_own == 1 else res[:n_own]


def _layer_fwd(x, h, c, w, bias, l, hooks, next_g):
    comm = lambda k: hooks[k][0] if k in hooks else None
    s = {"x": x}
    s["h"] = _rms_fwd(x, c["pre_mix_norm"], f"pre_mix_norm_{l}", out_dtype=BF16) if h is None else h
    res = _mm(s["h"], w["w_in"], f"in_proj_{l}", out_dtype=BF16, mode="dt_side", comm=comm("in_proj"))
    proj, s["dt"] = _hosted(res, 2, hooks.get("in_proj"))
    s["proj"] = proj
    s["xbc"] = _conv_fwd(proj, c["conv_w8"], c["conv_b1"])
    s["dtt"] = _dt_rows(s["dt"])
    yf, s["stf"] = _hosted(_ssd_fwd(s["xbc"], s["dt"], s["dtt"], c["dtb"], c["alog"], False, comm=comm("ssd_fwd")), 2,
                                hooks.get("ssd_fwd"))
    s["y"], s["stb"] = _hosted(_ssd_fwd(s["xbc"], s["dt"], s["dtt"], c["dtb"], c["alog"], True, comm=comm("ssd_fwd_rev"), add=yf), 2,
                                hooks.get("ssd_fwd_rev"))
    s["u"] = _gnorm_fwd(s["y"], s["xbc"], proj, c["dskip_x"], c["ssd_norm"])
    s["y_ssd"] = _mm(s["u"], w["w_ssd_out"], f"ssd_out_{l}")
    s["o"], s["lse"] = _hosted(_attn_fwd(proj, bias, c["sink"], comm=comm("attn_fwd")), 2, hooks.get("attn_fwd"))
    s["y_attn"] = _mm(s["o"], w["w_attn_out"], f"attn_out_{l}")
    s["mix"] = _gate_fwd(proj, c["b_gate"], s["y_ssd"], s["y_attn"])
    s["mixed"] = _mm(s["mix"], w["w_o"], f"w_o_{l}")
    s["x1"], s["h2"] = _rms_fwd(s["mixed"], c["post_mix_norm"], f"post_mix_norm_{l}", res=x, next_g=c["pre_mlp_norm"])
    s["f1"], s["a"] = _mm(s["h2"], w["w_mlp_in"], f"mlp_in_{l}", mode="relu2")
    s["f"] = _mm(s["a"], w["w_mlp_out"], f"mlp_out_{l}")
    if next_g is None:
        return _rms_fwd(s["f"], c["post_mlp_norm"], f"post_mlp_norm_{l}", res=s["x1"]), None, s
    x2, h_next = _rms_fwd(s["f"], c["post_mlp_norm"], f"post_mlp_norm_{l}", res=s["x1"], next_g=next_g)
    return x2, h_next, s


def _send_rest(gw):
    s_mi = gw["w_mlp_in"].reshape(D_MODEL, N_DEV, -1).transpose(1, 0, 2).astype(BF16)
    parts = [(_unpad_heads(gw[n], 0) if n == "w_attn_out" else gw[n]).reshape(N_DEV, -1, D_MODEL) for n in ROWS]
    return [s_mi, jnp.concatenate(parts, axis=1).astype(BF16)]


def _send_w_in(g):
    return [_w_in_from_padded(g).reshape(D_MODEL, N_DEV, -1).transpose(1, 0, 2).astype(BF16)]


def _layer_bwd(dx2, s, c, w, bias, l, earlier=None, host_own=False):
    gw, gs, got = {}, {}, {}
    df, gs["post_mlp_norm"] = _rms_bwd(s["f"], c["post_mlp_norm"], dx2, f"post_mlp_norm_bwd_{l}", out_dtype=BF16)
    df1 = _mm(df, w["w_mlp_out"], f"mlp_out_dx_{l}", tb=True, mode="relu2_bwd", extra=s["f1"], out_dtype=BF16)
    gw["w_mlp_out"] = _mm(s["a"], df, f"mlp_out_dw_{l}", ta=True, out_dtype=BF16)
    gw["w_mlp_in"] = _mm(s["h2"], df1, f"mlp_in_dw_{l}", ta=True, out_dtype=BF16)
    dh2 = _mm(df1, w["w_mlp_in"], f"mlp_in_dx_{l}", tb=True, out_dtype=BF16)
    dx1, gs["pre_mlp_norm"] = _rms_bwd(s["x1"], c["pre_mlp_norm"], dh2, f"pre_mlp_norm_bwd_{l}", add=dx2)
    dmixed, gs["post_mix_norm"] = _rms_bwd(s["mixed"], c["post_mix_norm"], dx1, f"post_mix_norm_bwd_{l}", out_dtype=BF16)
    gw["w_o"] = _mm(s["mix"], dmixed, f"w_o_dw_{l}", ta=True, out_dtype=BF16)
    dmix = _mm(dmixed, w["w_o"], f"w_o_dx_{l}", tb=True, out_dtype=BF16)
    dys, dya, dproj, gs["b_gate"] = _gate_bwd(dmix, s["proj"], c["b_gate"], s["y_ssd"], s["y_attn"])
    gw["w_ssd_out"] = _mm(s["u"], dys, f"ssd_out_dw_{l}", ta=True, out_dtype=BF16)
    du = _mm(dys, w["w_ssd_out"], f"ssd_out_dx_{l}", tb=True, out_dtype=BF16)
    gw["w_attn_out"] = _mm(s["o"], dya, f"attn_out_dw_{l}", ta=True, out_dtype=BF16)
    do = _mm(dya, w["w_attn_out"], f"attn_out_dx_{l}", tb=True, out_dtype=BF16)
    res = _attn_bwd(s["proj"], bias, c["sink"], do, s["o"], s["lse"], dproj,
                    comm=None if earlier is None else _Comm("exchange", earlier[:1]))
    (dproj, dbias, dsink), got["earlier"] = res[:3], list(res[3:])
    gs["attn_sink"] = dsink
    dy, dproj, gs["ssd_norm"], gs["d_skip"] = _gnorm_bwd(du, s["y"], s["xbc"], s["proj"], c["dskip_x"], c["ssd_norm"],
                                                         dproj)
    res = _ssd_bwd(s["xbc"], s["dt"], s["dtt"], c["dtb"], c["alog"], dy, s["stf"], c["dskip_x"], False,
                   comm=None if earlier is None else _Comm("exchange", earlier[1:]))
    part, got["earlier"] = res[:3], got["earlier"] + list(res[3:])
    res = _ssd_bwd(s["xbc"], s["dt"], s["dtt"], c["dtb"], c["alog"], dy, s["stb"], c["dskip_x"], True, prev=part, dproj=dproj,
                   comm=_Comm("exchange", _send_rest(gw)) if host_own else None)
    (dxbc, dproj, sm), got["rest"] = res[:3], res[3:]
    gs["a_log"], gs["dt_bias"] = sm[0:1], sm[1:2]
    dproj, gs["conv"] = _conv_bwd(s["proj"], dxbc, c["conv_w8"], c["conv_b1"], dproj)
    gw["w_in"] = _mm(s["h"], dproj, f"in_proj_dw_{l}", ta=True, out_dtype=BF16)
    if host_own:
        dh, *got["w_in"] = _mm(dproj, w["w_in"], f"in_proj_dx_{l}", tb=True, out_dtype=BF16, comm=_Comm("exchange", _send_w_in(gw["w_in"])))
    else:
        dh = _mm(dproj, w["w_in"], f"in_proj_dx_{l}", tb=True, out_dtype=BF16)
    dx, gs["pre_mix_norm"] = _rms_bwd(s["x"], c["pre_mix_norm"], dh, f"pre_mix_norm_bwd_{l}", add=dx1)
    return dx, gw, gs, dbias, got


SMALL = (("pre_mix_norm", DEPTH * D_MODEL), ("b_gate", DEPTH * 2 * D_MODEL), ("conv_w", DEPTH * SSD_CONV * CONV_DIM),
         ("conv_b", DEPTH * CONV_DIM), ("dt_bias", DEPTH * 2 * SSD_HEADS), ("a_log", DEPTH * 2 * SSD_HEADS),
         ("d_skip", DEPTH * SSD_HEADS), ("ssd_norm", DEPTH * D_INNER), ("attn_sink", DEPTH * Q_HEADS),
         ("rel_bias_table", N_BUCKETS * Q_HEADS), ("post_mix_norm", DEPTH * D_MODEL), ("pre_mlp_norm", DEPTH * D_MODEL),
         ("post_mlp_norm", DEPTH * D_MODEL))


def _small_rows(n):
    return -(-n // LANE)


def _pack_small(vals, fill=0.0):
    rows = []
    for name, n in SMALL:
        v = vals[name].reshape(-1).astype(F32)
        rows.append(jnp.pad(v, (0, _small_rows(n) * LANE - n), constant_values=fill).reshape(-1, LANE))
    out = jnp.concatenate(rows, axis=0)
    return jnp.pad(out, ((0, -out.shape[0] % 8), (0, 0)), constant_values=fill)


def _unpack_small(packed, shapes):
    out, r = {}, 0
    for name, n in SMALL:
        nr = _small_rows(n)
        out[name] = packed[r:r + nr].reshape(-1)[:n].reshape(shapes[name])
        r += nr
    return out


BIG = ("w_in", "w_ssd_out", "w_attn_out", "w_o", "w_mlp_in", "w_mlp_out")
ROWS = ("w_ssd_out", "w_attn_out", "w_o", "w_mlp_out")


def _pack_rows(vals, lead):
    return jnp.concatenate([vals[n].reshape(lead + (-1, D_MODEL)) for n in ROWS], axis=len(lead))


def kernel(x, pre_mix_norm, w_in, b_gate, conv_w, conv_b, dt_bias, a_log, d_skip, ssd_norm, w_ssd_out, attn_sink, rel_bias_table, w_attn_out, w_o, post_mix_norm, pre_mlp_norm, w_mlp_in, w_mlp_out, post_mlp_norm, loss_target, m_pre_mix_norm, m_w_in, m_b_gate, m_conv_w, m_conv_b, m_dt_bias, m_a_log, m_d_skip, m_ssd_norm, m_w_ssd_out, m_attn_sink, m_rel_bias_table, m_w_attn_out, m_w_o, m_post_mix_norm, m_pre_mlp_norm, m_w_mlp_in, m_w_mlp_out, m_post_mlp_norm, v_pre_mix_norm, v_w_in, v_b_gate, v_conv_w, v_conv_b, v_dt_bias, v_a_log, v_d_skip, v_ssd_norm, v_w_ssd_out, v_attn_sink, v_rel_bias_table, v_w_attn_out, v_w_o, v_post_mix_norm, v_pre_mlp_norm, v_w_mlp_in, v_w_mlp_out, v_post_mlp_norm):
    names = ("pre_mix_norm", "w_in", "b_gate", "conv_w", "conv_b", "dt_bias", "a_log", "d_skip", "ssd_norm", "w_ssd_out",
             "attn_sink", "rel_bias_table", "w_attn_out", "w_o", "post_mix_norm", "pre_mlp_norm", "w_mlp_in", "w_mlp_out",
             "post_mlp_norm")
    W = dict(zip(names, (pre_mix_norm, w_in, b_gate, conv_w, conv_b, dt_bias, a_log, d_skip, ssd_norm, w_ssd_out, attn_sink,
                         rel_bias_table, w_attn_out, w_o, post_mix_norm, pre_mlp_norm, w_mlp_in, w_mlp_out, post_mlp_norm)))
    M = dict(zip(names, (m_pre_mix_norm, m_w_in, m_b_gate, m_conv_w, m_conv_b, m_dt_bias, m_a_log, m_d_skip, m_ssd_norm,
                         m_w_ssd_out, m_attn_sink, m_rel_bias_table, m_w_attn_out, m_w_o, m_post_mix_norm, m_pre_mlp_norm,
                         m_w_mlp_in, m_w_mlp_out, m_post_mlp_norm)))
    V = dict(zip(names, (v_pre_mix_norm, v_w_in, v_b_gate, v_conv_w, v_conv_b, v_dt_bias, v_a_log, v_d_skip, v_ssd_norm,
                         v_w_ssd_out, v_attn_sink, v_rel_bias_table, v_w_attn_out, v_w_o, v_post_mix_norm, v_pre_mlp_norm,
                         v_w_mlp_in, v_w_mlp_out, v_post_mlp_norm)))
    t = x.shape[1]
    shard = {n: W[n].shape for n in names}
    dev = 4 * lax.axis_index("x") + 2 * lax.axis_index("y") + lax.axis_index("c")
    cshard = CONV_DIM // N_DEV

    sh_in = [w_in[l].astype(BF16) for l in range(DEPTH)]
    sh_rest = [[w_mlp_in[l].astype(BF16), _pack_rows({n: W[n][l].astype(BF16) for n in ROWS}, ())] for l in range(DEPTH)]
    full = [{} for _ in range(DEPTH)]

    def take_w_in(l):
        def take(got):
            full[l]["w_in"] = _w_in_to_padded(got[0].transpose(1, 0, 2).reshape(D_MODEL, N_IN))
        return take

    def take_mlp_in(l):
        def take(got):
            full[l]["w_mlp_in"] = got[0].transpose(1, 0, 2).reshape(D_MODEL, D_FF)
        return take

    def take_rows(l):
        def take(got):
            r = 0
            for n in ROWS:
                per = shard[n][1]
                full[l][n] = got[0][:, r:r + per].reshape(N_DEV * per, D_MODEL)
                r += per
            full[l]["w_attn_out"] = _pad_heads(full[l]["w_attn_out"], 0)
        return take

    def take_rest(l):
        def take(got):
            take_mlp_in(l)(got[:1])
            take_rows(l)(got[1:])
        return take

    normed0, g_in0, g_conv = _rms_fwd(x[0], pre_mix_norm[0], "pre_mix_norm_0", out_dtype=BF16,
                                      comm=_Comm("gather", [sh_in[0], conv_w.reshape(-1, LANE)]))
    take_w_in(0)([g_in0])
    conv_full = g_conv.reshape(N_DEV, DEPTH, SSD_CONV, cshard).transpose(1, 2, 0, 3).reshape(DEPTH, SSD_CONV, CONV_DIM)
    hooks = [{"in_proj": (_Comm("gather", sh_rest[0]), take_rest(0)),
              "ssd_fwd": (_Comm("gather", sh_rest[1][1:]), take_rows(1)),
              "ssd_fwd_rev": (_Comm("gather", sh_rest[1][:1]), take_mlp_in(1)),
              "attn_fwd": (_Comm("gather", [sh_in[1]]), take_w_in(1))}, {}]

    _, band = _bucket_np()
    onehot = jnp.asarray(_onehot_np(), BF16)
    table_t = jnp.pad(rel_bias_table.T, ((0, 0), (0, LANE - N_BUCKETS)))
    bias = _mm(table_t, onehot, "t5_bias", tb=True, tm=Q_HEADS, tn=3 * BLOCK * BLOCK // 8, split_a=True)
    bias = (bias.reshape(Q_HEADS, BLOCK, 3 * BLOCK) + jnp.asarray(np.where(band, 0.0, NEG), F32)).reshape(
        KV_HEADS, REP * BLOCK, 3 * BLOCK)

    consts = []
    for l in range(DEPTH):
        p = {n: W[n][l] for n in names if n not in BIG and n not in ("rel_bias_table", "conv_w")}
        p["conv_w"] = conv_full[l]
        consts.append(_layer_consts(p))
    act, normed = x[0], normed0
    saved = []
    for l in range(DEPTH):
        next_g = consts[l + 1]["pre_mix_norm"] if l + 1 < DEPTH else None
        act, normed, s = _layer_fwd(act, normed, consts[l], full[l], bias, l, hooks[l], next_g)
        saved.append(s)
    dy, lsum = _loss(act, loss_target[0])
    loss = lax.psum(0.5 / D_MODEL * jnp.sum(lsum), ("x", "y", "c"))

    gss, dbs = [None] * DEPTH, [None] * DEPTH
    dy, gw1, gss[1], dbs[1], _ = _layer_bwd(dy, saved[1], consts[1], full[1], bias, 1)
    dy, _, gss[0], dbs[0], got = _layer_bwd(dy, saved[0], consts[0], full[0], bias, 0,
                                            earlier=_send_w_in(gw1["w_in"]) + _send_rest(gw1), host_own=True)
    recv = [list(got["w_in"]) + list(got["rest"]), list(got["earlier"])]
    grad_x = dy[None]
    dbias = jnp.concatenate([d.reshape(Q_HEADS, -1) for d in dbs], axis=1)
    d_table = _mm(dbias, jnp.concatenate([onehot] * DEPTH, axis=0), "t5_bias_bwd", tm=Q_HEADS, tk=3 * BLOCK * BLOCK // 8,
                  split_a=True)

    sg = {}
    for n in ("pre_mix_norm", "b_gate", "ssd_norm", "post_mix_norm", "pre_mlp_norm", "post_mlp_norm"):
        sg[n] = jnp.stack([gss[l][n].reshape(-1) for l in range(DEPTH)])
    sg["conv_w"] = jnp.stack([gss[l]["conv"][0:SSD_CONV] for l in range(DEPTH)])
    sg["conv_b"] = jnp.stack([gss[l]["conv"][SSD_CONV] for l in range(DEPTH)])
    sg["dt_bias"] = jnp.stack([gss[l]["dt_bias"][0, 0:2 * SSD_HEADS] for l in range(DEPTH)])
    sg["a_log"] = jnp.stack([gss[l]["a_log"][0, 0:2 * SSD_HEADS] for l in range(DEPTH)])
    sg["d_skip"] = jnp.stack([gss[l]["d_skip"][0, 0:SSD_HEADS] for l in range(DEPTH)])
    sg["attn_sink"] = jnp.stack([gss[l]["attn_sink"][0, 0:Q_HEADS] for l in range(DEPTH)])
    sg["rel_bias_table"] = d_table[:, 0:N_BUCKETS].T
    (small_parts,) = _communicate(_Comm("gather", [_pack_small(sg)]), "gather_small_grads", vmem=True)
    sshape = {n: W[n].shape for n, _ in SMALL}
    sshape["conv_w"] = (DEPTH, SSD_CONV, 1, CONV_DIM)
    pk = lambda d, fill: _pack_small({n: (jnp.full((DEPTH, SSD_CONV, CONV_DIM), fill, F32) if n == "conv_w" else d[n])
                                      for n, _ in SMALL}, fill)
    s_out = [_unpack_small(o, sshape) for o in _adamw(small_parts, pk(W, 1.0), pk(M, 1.0), pk(V, 1.0), "adamw_small", tr=1024)]
    conv_g = lax.dynamic_slice_in_dim(s_out[0]["conv_w"], dev * cshard, cshard, axis=3)
    c_out = _adamw(conv_g.reshape(1, DEPTH * SSD_CONV, cshard), conv_w.reshape(-1, cshard), m_conv_w.reshape(-1, cshard),
                   v_conv_w.reshape(-1, cshard), "adamw_conv_w", tr=DEPTH * SSD_CONV)
    for i in range(4):
        s_out[i]["conv_w"] = c_out[i].reshape(conv_w.shape)

    b_out = {n: None for n in BIG}
    for l in (1, 0):
        r_in, r_mi, r_rows = recv[l]
        b_out["w_in"] = _adamw_layer(r_in, 0, w_in, m_w_in, v_w_in, l, b_out["w_in"], f"adamw_w_in_{l}", tr=256)
        b_out["w_mlp_in"] = _adamw_layer(r_mi, 0, w_mlp_in, m_w_mlp_in, v_w_mlp_in, l, b_out["w_mlp_in"], f"adamw_w_mlp_in_{l}",
                                         tr=256)
        r = 0
        for n in ROWS:
            b_out[n] = _adamw_layer(r_rows, r, W[n], M[n], V[n], l, b_out[n], f"adamw_{n}_{l}")
            r += shard[n][1]

    outs = [loss, grad_x]
    for i in range(4):
        for n in names:
            outs.append(b_out[n][i] if n in BIG else s_out[i][n])
    return tuple(outs)
```
